```python
import jax, jax.numpy as jnp
from jax import lax
import numpy as np

D_MODEL = 1024
BATCH = 8
SEQ = 8192
DEPTH = 2

N_AB_LAYERS = (DEPTH + 1) // 2
N_C_LAYERS = DEPTH // 2

DN_HEADS = 4
DN_HEAD_DIM = 128
DN_KEY = DN_HEADS * DN_HEAD_DIM
DN_VAL = DN_HEADS * DN_HEAD_DIM
DN_CONV = 4
DN_CHUNK = 64
SG_GROUPS = 4
SG_GROUP_DIM = 128
SG_WIDTH = SG_GROUPS * SG_GROUP_DIM
SG_CHUNK = 128
POOL_WINDOWS = (2, 4, 8, 16)
POOL_GROUP_DIM = D_MODEL // len(POOL_WINDOWS)
D_FF = 2816
NORM_EPS = 1e-6

MIX_WIDTH = DN_VAL + SG_WIDTH
PROJ_SIZES = (DN_KEY, DN_KEY, DN_VAL, DN_VAL, DN_HEADS, DN_HEADS, SG_WIDTH, SG_WIDTH)
IN_PROJ = int(sum(PROJ_SIZES))
SPLIT_POINTS = tuple(int(s) for s in np.cumsum(PROJ_SIZES)[:-1])
QKV_WIDTH = 2 * DN_KEY + DN_VAL

kernel_name = "hybrid_deltanet_sgmlp_pool_macaron"


def _rmsnorm(x, w):
    xf = x.astype(jnp.float32)
    y = xf * lax.rsqrt(jnp.mean(xf * xf, axis=-1, keepdims=True) + NORM_EPS)
    return (y * w.astype(jnp.float32)).astype(x.dtype)


def _l2norm(x):
    return x * lax.rsqrt(jnp.sum(x * x, axis=-1, keepdims=True) + NORM_EPS)


def _swiglu(h, w_in, w_out):
    gate, up = jnp.split(h @ w_in, 2, axis=-1)
    return (jax.nn.silu(gate) * up) @ w_out


def _causal_dwconv(x, w):
    c = x.shape[-1]
    return lax.conv_general_dilated(
        x, w[:, None, :].astype(x.dtype), window_strides=(1,), padding=[(w.shape[0] - 1, 0)],
        dimension_numbers=("NWC", "WIO", "NWC"), feature_group_count=c)


def _gated_delta_rule(q, k, v, g, beta):
    bsz, t, h, dk = q.shape
    dv = v.shape[-1]
    n, c = t // DN_CHUNK, DN_CHUNK

    def chunk(a):
        return jnp.moveaxis(a.reshape((bsz, n, c, h) + a.shape[3:]), 3, 1)

    q = chunk(_l2norm(q) * dk ** -0.5)
    k = chunk(_l2norm(k))
    v = chunk(v)
    beta = chunk(beta)
    g = jnp.cumsum(chunk(g), axis=-1)
    causal = jnp.tril(jnp.ones((c, c), dtype=bool))
    strict = jnp.tril(jnp.ones((c, c), dtype=bool), k=-1)
    diff = g[..., :, None] - g[..., None, :]
    decay = jnp.where(causal, jnp.exp(jnp.where(causal, diff, 0.0)), 0.0)
    k_beta = k * beta[..., None]
    lower = jnp.where(strict, jnp.einsum('bhncd,bhnsd->bhncs', k_beta, k) * decay, 0.0)
    eye = jnp.eye(c, dtype=q.dtype)
    rhs = jnp.concatenate([v * beta[..., None], k_beta * jnp.exp(g)[..., None]], axis=-1)
    sol = lax.linalg.triangular_solve(eye + lower, rhs, left_side=True, lower=True, unit_diagonal=True)
    u, w = sol[..., :dv], sol[..., dv:]
    attn = jnp.einsum('bhncd,bhnsd->bhncs', q, k) * decay
    q_dec = q * jnp.exp(g)[..., None]
    g_last = g[..., -1]
    k_tail = k * jnp.exp(g_last[..., None] - g)[..., None]

    def step(state, inp):
        q_i, w_i, u_i, a_i, k_i, gl_i = inp
        v_new = u_i - jnp.einsum('bhck,bhkv->bhcv', w_i, state)
        o_i = jnp.einsum('bhck,bhkv->bhcv', q_i, state) + jnp.einsum('bhcs,bhsv->bhcv', a_i, v_new)
        state = state * jnp.exp(gl_i)[..., None, None] + jnp.einsum('bhck,bhcv->bhkv', k_i, v_new)
        return state, o_i

    xs = tuple(jnp.moveaxis(a, 2, 0) for a in (q_dec, w, u, attn, k_tail, g_last))
    state0 = jnp.zeros((bsz, h, dk, dv), q.dtype)
    _, o = lax.scan(step, state0, xs)
    return jnp.transpose(o, (1, 0, 3, 2, 4)).reshape(bsz, t, h, dv)


def _hybrid_ab_mixer(h, w_in, conv_w, a_log, dt_bias, dn_norm, sg_norm, sg_w, sg_b, w_out):
    bsz, t, _ = h.shape
    q, k, v, z, b, a, su, sv = jnp.split(h @ w_in, SPLIT_POINTS, axis=-1)
    qkv = jax.nn.silu(_causal_dwconv(jnp.concatenate([q, k, v], axis=-1), conv_w)).astype(jnp.float32)
    q, k, v = jnp.split(qkv, (DN_KEY, 2 * DN_KEY), axis=-1)
    q = q.reshape(bsz, t, DN_HEADS, DN_HEAD_DIM)
    k = k.reshape(bsz, t, DN_HEADS, DN_HEAD_DIM)
    v = v.reshape(bsz, t, DN_HEADS, DN_HEAD_DIM)
    beta = jax.nn.sigmoid(b.astype(jnp.float32))
    g = -jnp.exp(a_log.astype(jnp.float32)) * jax.nn.softplus(a.astype(jnp.float32) + dt_bias.astype(jnp.float32))
    o = _gated_delta_rule(q, k, v, g, beta)
    zf = z.astype(jnp.float32).reshape(bsz, t, DN_HEADS, DN_HEAD_DIM)
    o = _rmsnorm(o, dn_norm) * jax.nn.silu(zf)
    o_a = o.reshape(bsz, t, DN_VAL).astype(h.dtype)
    su = jax.nn.gelu(su, approximate=False).reshape(bsz, t, SG_GROUPS, SG_GROUP_DIM)
    sv = _rmsnorm(jax.nn.gelu(sv, approximate=False).reshape(bsz, t, SG_GROUPS, SG_GROUP_DIM), sg_norm)
    sv = sv.reshape(bsz, t // SG_CHUNK, SG_CHUNK, SG_GROUPS, SG_GROUP_DIM)
    tri = jnp.tril(jnp.ones((SG_CHUNK, SG_CHUNK), dtype=bool))
    w_s = jnp.where(tri, sg_w, 0.0).astype(sv.dtype)
    mixed = jnp.einsum('gts,bnsgc->bntgc', w_s, sv) + jnp.transpose(sg_b)[:, :, None].astype(sv.dtype)
    o_b = (su * mixed.reshape(bsz, t, SG_GROUPS, SG_GROUP_DIM)).reshape(bsz, t, SG_WIDTH)
    return jnp.concatenate([o_a, o_b], axis=-1) @ w_out


def _pool_mixer(h, pool_w, pool_scale):
    t = h.shape[1]
    hf = h.astype(jnp.float32)
    csum = jnp.cumsum(hf, axis=1)
    pos = jnp.arange(1, t + 1)
    outs = []
    for gi, win in enumerate(POOL_WINDOWS):
        sl = slice(gi * POOL_GROUP_DIM, (gi + 1) * POOL_GROUP_DIM)
        cs = csum[..., sl]
        lag = jnp.pad(cs[:, :-win], ((0, 0), (win, 0), (0, 0)))
        count = jnp.minimum(pos, win).astype(jnp.float32)[None, :, None]
        pooled = ((cs - lag) / count - hf[..., sl]).astype(h.dtype)
        outs.append(pooled @ pool_w[gi])
    return jnp.concatenate(outs, axis=-1) * pool_scale


def _fwd_setup_inputs(seed: int = 0) -> dict:
    key = jax.random.key(seed)
    ks = jax.random.split(key, 24)
    f32 = jnp.float32
    nrm = lambda k, s, sc: jax.random.normal(k, s, f32) * sc
    gain = lambda k, s: 1.0 + 0.02 * jax.random.normal(k, s, f32)
    x = jax.random.normal(ks[0], (BATCH, SEQ, D_MODEL), f32)
    ffn_norm1 = gain(ks[1], (DEPTH, D_MODEL))
    ffn1_w_in = nrm(ks[2], (DEPTH, D_MODEL, 2 * D_FF), D_MODEL ** -0.5)
    ffn1_w_out = nrm(ks[3], (DEPTH, D_FF, D_MODEL), D_FF ** -0.5)
    mix_norm = gain(ks[4], (DEPTH, D_MODEL))
    ffn_norm2 = gain(ks[5], (DEPTH, D_MODEL))
    ffn2_w_in = nrm(ks[6], (DEPTH, D_MODEL, 2 * D_FF), D_MODEL ** -0.5)
    ffn2_w_out = nrm(ks[7], (DEPTH, D_FF, D_MODEL), D_FF ** -0.5)
    ab_w_in = nrm(ks[8], (N_AB_LAYERS, D_MODEL, IN_PROJ), D_MODEL ** -0.5)
    dn_conv_w = nrm(ks[9], (N_AB_LAYERS, DN_CONV, QKV_WIDTH), DN_CONV ** -0.5)
    dn_a_log = jnp.log(jax.random.uniform(ks[10], (N_AB_LAYERS, DN_HEADS), f32, 1.0, 16.0))
    dt = jnp.exp(jax.random.uniform(ks[11], (N_AB_LAYERS, DN_HEADS), f32, np.log(1e-3), np.log(1e-1)))
    dn_dt_bias = dt + jnp.log(-jnp.expm1(-dt))
    dn_out_norm = gain(ks[12], (N_AB_LAYERS, DN_HEAD_DIM))
    sg_norm = gain(ks[13], (N_AB_LAYERS, SG_GROUPS, SG_GROUP_DIM))
    sg_w = nrm(ks[14], (N_AB_LAYERS, SG_GROUPS, SG_CHUNK, SG_CHUNK), SG_CHUNK ** -0.5)
    sg_b = 1.0 + 0.1 * jax.random.normal(ks[15], (N_AB_LAYERS, SG_GROUPS, SG_CHUNK), f32)
    ab_w_out = nrm(ks[16], (N_AB_LAYERS, MIX_WIDTH, D_MODEL), MIX_WIDTH ** -0.5)
    pool_w = nrm(ks[17], (N_C_LAYERS, len(POOL_WINDOWS), POOL_GROUP_DIM, POOL_GROUP_DIM), POOL_GROUP_DIM ** -0.5)
    pool_scale = gain(ks[18], (N_C_LAYERS, D_MODEL))
    final_norm = gain(ks[19], (D_MODEL,))
    return {"x": x, "ffn_norm1": ffn_norm1, "ffn1_w_in": ffn1_w_in, "ffn1_w_out": ffn1_w_out,
            "mix_norm": mix_norm, "ffn_norm2": ffn_norm2, "ffn2_w_in": ffn2_w_in, "ffn2_w_out": ffn2_w_out,
            "ab_w_in": ab_w_in, "dn_conv_w": dn_conv_w, "dn_a_log": dn_a_log, "dn_dt_bias": dn_dt_bias,
            "dn_out_norm": dn_out_norm, "sg_norm": sg_norm, "sg_w": sg_w, "sg_b": sg_b, "ab_w_out": ab_w_out,
            "pool_w": pool_w, "pool_scale": pool_scale, "final_norm": final_norm}


def _fwd_reference(x, ffn_norm1, ffn1_w_in, ffn1_w_out, mix_norm, ffn_norm2, ffn2_w_in, ffn2_w_out,
              ab_w_in, dn_conv_w, dn_a_log, dn_dt_bias, dn_out_norm, sg_norm, sg_w, sg_b, ab_w_out,
              pool_w, pool_scale, final_norm):
    for l in range(DEPTH):
        x = x + 0.5 * _swiglu(_rmsnorm(x, ffn_norm1[l]), ffn1_w_in[l], ffn1_w_out[l])
        h = _rmsnorm(x, mix_norm[l])
        i = l // 2
        if l % 2 == 0:
            x = x + _hybrid_ab_mixer(h, ab_w_in[i], dn_conv_w[i], dn_a_log[i], dn_dt_bias[i], dn_out_norm[i],
                                     sg_norm[i], sg_w[i], sg_b[i], ab_w_out[i])
        else:
            x = x + _pool_mixer(h, pool_w[i], pool_scale[i])
        x = x + 0.5 * _swiglu(_rmsnorm(x, ffn_norm2[l]), ffn2_w_in[l], ffn2_w_out[l])
    return _rmsnorm(x, final_norm)


import jax as _jax
import jax.numpy as _jnp

TWIN_FORMAT = 'train_step'
FWD_PARAMS = ['x', 'ffn_norm1', 'ffn1_w_in', 'ffn1_w_out', 'mix_norm', 'ffn_norm2', 'ffn2_w_in', 'ffn2_w_out', 'ab_w_in', 'dn_conv_w', 'dn_a_log', 'dn_dt_bias', 'dn_out_norm', 'sg_norm', 'sg_w', 'sg_b', 'ab_w_out', 'pool_w', 'pool_scale', 'final_norm']
TWIN_WEIGHTS = ['ffn_norm1', 'ffn1_w_in', 'ffn1_w_out', 'mix_norm', 'ffn_norm2', 'ffn2_w_in', 'ffn2_w_out', 'ab_w_in', 'dn_conv_w', 'dn_a_log', 'dn_dt_bias', 'dn_out_norm', 'sg_norm', 'sg_w', 'sg_b', 'ab_w_out', 'pool_w', 'pool_scale', 'final_norm']
TWIN_DIFF_INPUT = 'x'
TWIN_INPUTS = ['x', 'ffn_norm1', 'ffn1_w_in', 'ffn1_w_out', 'mix_norm', 'ffn_norm2', 'ffn2_w_in', 'ffn2_w_out', 'ab_w_in', 'dn_conv_w', 'dn_a_log', 'dn_dt_bias', 'dn_out_norm', 'sg_norm', 'sg_w', 'sg_b', 'ab_w_out', 'pool_w', 'pool_scale', 'final_norm', 'loss_target', 'm_ffn_norm1', 'm_ffn1_w_in', 'm_ffn1_w_out', 'm_mix_norm', 'm_ffn_norm2', 'm_ffn2_w_in', 'm_ffn2_w_out', 'm_ab_w_in', 'm_dn_conv_w', 'm_dn_a_log', 'm_dn_dt_bias', 'm_dn_out_norm', 'm_sg_norm', 'm_sg_w', 'm_sg_b', 'm_ab_w_out', 'm_pool_w', 'm_pool_scale', 'm_final_norm', 'v_ffn_norm1', 'v_ffn1_w_in', 'v_ffn1_w_out', 'v_mix_norm', 'v_ffn_norm2', 'v_ffn2_w_in', 'v_ffn2_w_out', 'v_ab_w_in', 'v_dn_conv_w', 'v_dn_a_log', 'v_dn_dt_bias', 'v_dn_out_norm', 'v_sg_norm', 'v_sg_w', 'v_sg_b', 'v_ab_w_out', 'v_pool_w', 'v_pool_scale', 'v_final_norm']
TWIN_OUTPUTS = ['loss', 'grad_x', 'grad_ffn_norm1', 'grad_ffn1_w_in', 'grad_ffn1_w_out', 'grad_mix_norm', 'grad_ffn_norm2', 'grad_ffn2_w_in', 'grad_ffn2_w_out', 'grad_ab_w_in', 'grad_dn_conv_w', 'grad_dn_a_log', 'grad_dn_dt_bias', 'grad_dn_out_norm', 'grad_sg_norm', 'grad_sg_w', 'grad_sg_b', 'grad_ab_w_out', 'grad_pool_w', 'grad_pool_scale', 'grad_final_norm', 'delta_ffn_norm1', 'delta_ffn1_w_in', 'delta_ffn1_w_out', 'delta_mix_norm', 'delta_ffn_norm2', 'delta_ffn2_w_in', 'delta_ffn2_w_out', 'delta_ab_w_in', 'delta_dn_conv_w', 'delta_dn_a_log', 'delta_dn_dt_bias', 'delta_dn_out_norm', 'delta_sg_norm', 'delta_sg_w', 'delta_sg_b', 'delta_ab_w_out', 'delta_pool_w', 'delta_pool_scale', 'delta_final_norm', 'new_m_ffn_norm1', 'new_m_ffn1_w_in', 'new_m_ffn1_w_out', 'new_m_mix_norm', 'new_m_ffn_norm2', 'new_m_ffn2_w_in', 'new_m_ffn2_w_out', 'new_m_ab_w_in', 'new_m_dn_conv_w', 'new_m_dn_a_log', 'new_m_dn_dt_bias', 'new_m_dn_out_norm', 'new_m_sg_norm', 'new_m_sg_w', 'new_m_sg_b', 'new_m_ab_w_out', 'new_m_pool_w', 'new_m_pool_scale', 'new_m_final_norm', 'new_v_ffn_norm1', 'new_v_ffn1_w_in', 'new_v_ffn1_w_out', 'new_v_mix_norm', 'new_v_ffn_norm2', 'new_v_ffn2_w_in', 'new_v_ffn2_w_out', 'new_v_ab_w_in', 'new_v_dn_conv_w', 'new_v_dn_a_log', 'new_v_dn_dt_bias', 'new_v_dn_out_norm', 'new_v_sg_norm', 'new_v_sg_w', 'new_v_sg_b', 'new_v_ab_w_out', 'new_v_pool_w', 'new_v_pool_scale', 'new_v_final_norm']
TWIN_LEAF_KINDS = {'loss': 'loss', 'grad_x': 'grad_x', 'grad_ffn_norm1': 'grad_w', 'grad_ffn1_w_in': 'grad_w', 'grad_ffn1_w_out': 'grad_w', 'grad_mix_norm': 'grad_w', 'grad_ffn_norm2': 'grad_w', 'grad_ffn2_w_in': 'grad_w', 'grad_ffn2_w_out': 'grad_w', 'grad_ab_w_in': 'grad_w', 'grad_dn_conv_w': 'grad_w', 'grad_dn_a_log': 'grad_w', 'grad_dn_dt_bias': 'grad_w', 'grad_dn_out_norm': 'grad_w', 'grad_sg_norm': 'grad_w', 'grad_sg_w': 'grad_w', 'grad_sg_b': 'grad_w', 'grad_ab_w_out': 'grad_w', 'grad_pool_w': 'grad_w', 'grad_pool_scale': 'grad_w', 'grad_final_norm': 'grad_w', 'delta_ffn_norm1': 'delta_w', 'delta_ffn1_w_in': 'delta_w', 'delta_ffn1_w_out': 'delta_w', 'delta_mix_norm': 'delta_w', 'delta_ffn_norm2': 'delta_w', 'delta_ffn2_w_in': 'delta_w', 'delta_ffn2_w_out': 'delta_w', 'delta_ab_w_in': 'delta_w', 'delta_dn_conv_w': 'delta_w', 'delta_dn_a_log': 'delta_w', 'delta_dn_dt_bias': 'delta_w', 'delta_dn_out_norm': 'delta_w', 'delta_sg_norm': 'delta_w', 'delta_sg_w': 'delta_w', 'delta_sg_b': 'delta_w', 'delta_ab_w_out': 'delta_w', 'delta_pool_w': 'delta_w', 'delta_pool_scale': 'delta_w', 'delta_final_norm': 'delta_w', 'new_m_ffn_norm1': 'new_m', 'new_m_ffn1_w_in': 'new_m', 'new_m_ffn1_w_out': 'new_m', 'new_m_mix_norm': 'new_m', 'new_m_ffn_norm2': 'new_m', 'new_m_ffn2_w_in': 'new_m', 'new_m_ffn2_w_out': 'new_m', 'new_m_ab_w_in': 'new_m', 'new_m_dn_conv_w': 'new_m', 'new_m_dn_a_log': 'new_m', 'new_m_dn_dt_bias': 'new_m', 'new_m_dn_out_norm': 'new_m', 'new_m_sg_norm': 'new_m', 'new_m_sg_w': 'new_m', 'new_m_sg_b': 'new_m', 'new_m_ab_w_out': 'new_m', 'new_m_pool_w': 'new_m', 'new_m_pool_scale': 'new_m', 'new_m_final_norm': 'new_m', 'new_v_ffn_norm1': 'new_v', 'new_v_ffn1_w_in': 'new_v', 'new_v_ffn1_w_out': 'new_v', 'new_v_mix_norm': 'new_v', 'new_v_ffn_norm2': 'new_v', 'new_v_ffn2_w_in': 'new_v', 'new_v_ffn2_w_out': 'new_v', 'new_v_ab_w_in': 'new_v', 'new_v_dn_conv_w': 'new_v', 'new_v_dn_a_log': 'new_v', 'new_v_dn_dt_bias': 'new_v', 'new_v_dn_out_norm': 'new_v', 'new_v_sg_norm': 'new_v', 'new_v_sg_w': 'new_v', 'new_v_sg_b': 'new_v', 'new_v_ab_w_out': 'new_v', 'new_v_pool_w': 'new_v', 'new_v_pool_scale': 'new_v', 'new_v_final_norm': 'new_v'}


def _forward(args):
    return _fwd_reference(*[args[k] for k in FWD_PARAMS])


def _output_shape():
    def fwd():
        inp = _fwd_setup_inputs(0)
        return _fwd_reference(*[inp[k] for k in FWD_PARAMS])
    out = _jax.eval_shape(fwd)
    return out.shape, out.dtype

N_MICROBATCH = 1
ADAM_LR = 0.001
ADAM_B1 = 0.9
ADAM_B2 = 0.999
ADAM_EPS = 1e-08
ADAM_WD = 0.01
ADAM_STEP = 10
PER_EXAMPLE_BATCH_AXIS = {'x': 0, 'loss_target': 0}
SHARED_INPUTS = []
_WEIGHT_DTYPES = {'ffn_norm1': _jnp.float32, 'ffn1_w_in': _jnp.float32, 'ffn1_w_out': _jnp.float32, 'mix_norm': _jnp.float32, 'ffn_norm2': _jnp.float32, 'ffn2_w_in': _jnp.float32, 'ffn2_w_out': _jnp.float32, 'ab_w_in': _jnp.float32, 'dn_conv_w': _jnp.float32, 'dn_a_log': _jnp.float32, 'dn_dt_bias': _jnp.float32, 'dn_out_norm': _jnp.float32, 'sg_norm': _jnp.float32, 'sg_w': _jnp.float32, 'sg_b': _jnp.float32, 'ab_w_out': _jnp.float32, 'pool_w': _jnp.float32, 'pool_scale': _jnp.float32, 'final_norm': _jnp.float32}
MOMENT_SCALE = {'ffn_norm1': 1.114701e-01, 'ffn1_w_in': 4.551960e-02, 'ffn1_w_out': 7.421718e-02, 'mix_norm': 1.892383e-01, 'ffn_norm2': 8.323945e-02, 'ffn2_w_in': 3.323593e-02, 'ffn2_w_out': 5.428348e-02, 'ab_w_in': 1.195166e-01, 'dn_conv_w': 9.188100e-02, 'dn_a_log': 1.504616e+00, 'dn_dt_bias': 1.234539e+00, 'dn_out_norm': 2.934820e-01, 'sg_norm': 1.017851e-01, 'sg_w': 9.903062e-02, 'sg_b': 1.479534e-01, 'ab_w_out': 1.514896e-01, 'pool_w': 1.456739e-01, 'pool_scale': 6.129059e-01, 'final_norm': 6.424756e+01}


def _to_microbatches(a, axis):
    t = _jnp.moveaxis(a, axis, 0)
    t = t.reshape((N_MICROBATCH, t.shape[0] // N_MICROBATCH) + t.shape[1:])
    return _jnp.moveaxis(t, 1, axis + 1)


def setup_inputs(seed: int = 0) -> dict:
    inp = _fwd_setup_inputs(seed)
    key = _jax.random.fold_in(_jax.random.key(seed), 7919)
    shape, _ = _output_shape()
    out = dict(inp)
    out["loss_target"] = _jax.random.normal(_jax.random.fold_in(key, 0), shape, _jnp.float32)
    for i, name in enumerate(TWIN_WEIGHTS):
        w = inp[name].astype(_jnp.float32)
        if MOMENT_SCALE is None:
            s = _jnp.sqrt(_jnp.mean(_jnp.square(w)) + 1e-30)
        else:
            s = MOMENT_SCALE[name]
        km, kv = _jax.random.split(_jax.random.fold_in(key, i + 1))
        out[name] = w
        out["m_" + name] = s * _jax.random.normal(km, w.shape, _jnp.float32)
        out["v_" + name] = (s * s) * _jax.random.uniform(kv, w.shape, _jnp.float32, 0.5, 1.5)
    if N_MICROBATCH > 1:
        for name, axis in PER_EXAMPLE_BATCH_AXIS.items():
            out[name] = _to_microbatches(out[name], axis)
    return {'x': out['x'], 'ffn_norm1': out['ffn_norm1'], 'ffn1_w_in': out['ffn1_w_in'], 'ffn1_w_out': out['ffn1_w_out'], 'mix_norm': out['mix_norm'], 'ffn_norm2': out['ffn_norm2'], 'ffn2_w_in': out['ffn2_w_in'], 'ffn2_w_out': out['ffn2_w_out'], 'ab_w_in': out['ab_w_in'], 'dn_conv_w': out['dn_conv_w'], 'dn_a_log': out['dn_a_log'], 'dn_dt_bias': out['dn_dt_bias'], 'dn_out_norm': out['dn_out_norm'], 'sg_norm': out['sg_norm'], 'sg_w': out['sg_w'], 'sg_b': out['sg_b'], 'ab_w_out': out['ab_w_out'], 'pool_w': out['pool_w'], 'pool_scale': out['pool_scale'], 'final_norm': out['final_norm'], 'loss_target': out['loss_target'], 'm_ffn_norm1': out['m_ffn_norm1'], 'm_ffn1_w_in': out['m_ffn1_w_in'], 'm_ffn1_w_out': out['m_ffn1_w_out'], 'm_mix_norm': out['m_mix_norm'], 'm_ffn_norm2': out['m_ffn_norm2'], 'm_ffn2_w_in': out['m_ffn2_w_in'], 'm_ffn2_w_out': out['m_ffn2_w_out'], 'm_ab_w_in': out['m_ab_w_in'], 'm_dn_conv_w': out['m_dn_conv_w'], 'm_dn_a_log': out['m_dn_a_log'], 'm_dn_dt_bias': out['m_dn_dt_bias'], 'm_dn_out_norm': out['m_dn_out_norm'], 'm_sg_norm': out['m_sg_norm'], 'm_sg_w': out['m_sg_w'], 'm_sg_b': out['m_sg_b'], 'm_ab_w_out': out['m_ab_w_out'], 'm_pool_w': out['m_pool_w'], 'm_pool_scale': out['m_pool_scale'], 'm_final_norm': out['m_final_norm'], 'v_ffn_norm1': out['v_ffn_norm1'], 'v_ffn1_w_in': out['v_ffn1_w_in'], 'v_ffn1_w_out': out['v_ffn1_w_out'], 'v_mix_norm': out['v_mix_norm'], 'v_ffn_norm2': out['v_ffn_norm2'], 'v_ffn2_w_in': out['v_ffn2_w_in'], 'v_ffn2_w_out': out['v_ffn2_w_out'], 'v_ab_w_in': out['v_ab_w_in'], 'v_dn_conv_w': out['v_dn_conv_w'], 'v_dn_a_log': out['v_dn_a_log'], 'v_dn_dt_bias': out['v_dn_dt_bias'], 'v_dn_out_norm': out['v_dn_out_norm'], 'v_sg_norm': out['v_sg_norm'], 'v_sg_w': out['v_sg_w'], 'v_sg_b': out['v_sg_b'], 'v_ab_w_out': out['v_ab_w_out'], 'v_pool_w': out['v_pool_w'], 'v_pool_scale': out['v_pool_scale'], 'v_final_norm': out['v_final_norm']}


def _loss(weights, diff, rest, loss_target):
    with _jax.named_scope("forward"):
        args = {**rest, TWIN_DIFF_INPUT: diff, **{k: w.astype(_WEIGHT_DTYPES[k]) for k, w in weights.items()}}
        y = _forward(args)
    with _jax.named_scope("loss_head"):
        err = _jnp.square(y.astype(_jnp.float32) - loss_target)
        return 0.5 * _jnp.sum(_jnp.mean(err, axis=-1)) if err.ndim else 0.5 * err


def _adamw(w, g, m, v):
    m = ADAM_B1 * m + (1.0 - ADAM_B1) * g
    v = ADAM_B2 * v + (1.0 - ADAM_B2) * _jnp.square(g)
    m_hat = m / (1.0 - ADAM_B1 ** ADAM_STEP)
    v_hat = v / (1.0 - ADAM_B2 ** ADAM_STEP)
    delta = -ADAM_LR * (m_hat / (_jnp.sqrt(v_hat) + ADAM_EPS) + ADAM_WD * w)
    return delta, m, v


def reference(x, ffn_norm1, ffn1_w_in, ffn1_w_out, mix_norm, ffn_norm2, ffn2_w_in, ffn2_w_out, ab_w_in, dn_conv_w, dn_a_log, dn_dt_bias, dn_out_norm, sg_norm, sg_w, sg_b, ab_w_out, pool_w, pool_scale, final_norm, loss_target, m_ffn_norm1, m_ffn1_w_in, m_ffn1_w_out, m_mix_norm, m_ffn_norm2, m_ffn2_w_in, m_ffn2_w_out, m_ab_w_in, m_dn_conv_w, m_dn_a_log, m_dn_dt_bias, m_dn_out_norm, m_sg_norm, m_sg_w, m_sg_b, m_ab_w_out, m_pool_w, m_pool_scale, m_final_norm, v_ffn_norm1, v_ffn1_w_in, v_ffn1_w_out, v_mix_norm, v_ffn_norm2, v_ffn2_w_in, v_ffn2_w_out, v_ab_w_in, v_dn_conv_w, v_dn_a_log, v_dn_dt_bias, v_dn_out_norm, v_sg_norm, v_sg_w, v_sg_b, v_ab_w_out, v_pool_w, v_pool_scale, v_final_norm):
    given = dict(x=x, ffn_norm1=ffn_norm1, ffn1_w_in=ffn1_w_in, ffn1_w_out=ffn1_w_out, mix_norm=mix_norm, ffn_norm2=ffn_norm2, ffn2_w_in=ffn2_w_in, ffn2_w_out=ffn2_w_out, ab_w_in=ab_w_in, dn_conv_w=dn_conv_w, dn_a_log=dn_a_log, dn_dt_bias=dn_dt_bias, dn_out_norm=dn_out_norm, sg_norm=sg_norm, sg_w=sg_w, sg_b=sg_b, ab_w_out=ab_w_out, pool_w=pool_w, pool_scale=pool_scale, final_norm=final_norm, loss_target=loss_target, m_ffn_norm1=m_ffn_norm1, m_ffn1_w_in=m_ffn1_w_in, m_ffn1_w_out=m_ffn1_w_out, m_mix_norm=m_mix_norm, m_ffn_norm2=m_ffn_norm2, m_ffn2_w_in=m_ffn2_w_in, m_ffn2_w_out=m_ffn2_w_out, m_ab_w_in=m_ab_w_in, m_dn_conv_w=m_dn_conv_w, m_dn_a_log=m_dn_a_log, m_dn_dt_bias=m_dn_dt_bias, m_dn_out_norm=m_dn_out_norm, m_sg_norm=m_sg_norm, m_sg_w=m_sg_w, m_sg_b=m_sg_b, m_ab_w_out=m_ab_w_out, m_pool_w=m_pool_w, m_pool_scale=m_pool_scale, m_final_norm=m_final_norm, v_ffn_norm1=v_ffn_norm1, v_ffn1_w_in=v_ffn1_w_in, v_ffn1_w_out=v_ffn1_w_out, v_mix_norm=v_mix_norm, v_ffn_norm2=v_ffn_norm2, v_ffn2_w_in=v_ffn2_w_in, v_ffn2_w_out=v_ffn2_w_out, v_ab_w_in=v_ab_w_in, v_dn_conv_w=v_dn_conv_w, v_dn_a_log=v_dn_a_log, v_dn_dt_bias=v_dn_dt_bias, v_dn_out_norm=v_dn_out_norm, v_sg_norm=v_sg_norm, v_sg_w=v_sg_w, v_sg_b=v_sg_b, v_ab_w_out=v_ab_w_out, v_pool_w=v_pool_w, v_pool_scale=v_pool_scale, v_final_norm=v_final_norm)
    weights = {n: given[n] for n in TWIN_WEIGHTS}
    shared = {n: given[n] for n in SHARED_INPUTS}
    per_example = {n: given[n] for n in ['x']}
    grad_fn = _jax.value_and_grad(_loss, argnums=(0, 1))

    def one_microbatch(ex, loss_target):
        ex = dict(ex)
        diff = ex.pop(TWIN_DIFF_INPUT)
        return grad_fn(weights, diff, {**shared, **ex}, loss_target)

    if N_MICROBATCH == 1:
        loss, (grad_w, grad_x) = one_microbatch(per_example, given["loss_target"])
    else:
        def body(carry, xs):
            loss_sum, grad_sum = carry
            l_k, (gw_k, gx_k) = one_microbatch(xs[0], xs[1])
            with _jax.named_scope("update"):
                return (loss_sum + l_k, _jax.tree.map(_jnp.add, grad_sum, gw_k)), gx_k

        init = (_jnp.zeros((), _jnp.float32), _jax.tree.map(_jnp.zeros_like, weights))
        (loss, grad_w), grad_x = _jax.lax.scan(body, init, (per_example, given["loss_target"]))
    with _jax.named_scope("update"):
        delta_w, new_m, new_v = {}, {}, {}
        for n in TWIN_WEIGHTS:
            delta_w[n], new_m[n], new_v[n] = _adamw(weights[n], grad_w[n], given["m_" + n], given["v_" + n])
    return (loss, grad_x, *[grad_w[n] for n in TWIN_WEIGHTS], *[delta_w[n] for n in TWIN_WEIGHTS],
            *[new_m[n] for n in TWIN_WEIGHTS], *[new_v[n] for n in TWIN_WEIGHTS])
```

```python
import functools

import jax
import jax.numpy as jnp
from jax import lax
from jax.experimental import pallas as pl
from jax.experimental.pallas import tpu as pltpu

F32 = jnp.float32
_MM = jnp.bfloat16
_HI = lax.Precision.HIGHEST

D = 1024
FF = 2816
EPS = 1e-6
HEADS = 4
HD = 128
DNC = 64
SGC = 128
QKV = 3 * HEADS * HD
HW = HEADS * HD
POOL_WINDOWS = (2, 4, 8, 16)
PG = D // 4
HALO = 16
N_DEV = 8

TM = 512
FC = 704
NJ = FF // FC
WO_ROWS = FF // N_DEV

ADAM_LR, ADAM_B1, ADAM_B2, ADAM_EPS, ADAM_WD, ADAM_STEP = 0.001, 0.9, 0.999, 1e-08, 0.01, 10

MESH_T = pl.DeviceIdType.MESH


def _c(a):
    return a.astype(_MM)


def _dot(a, b):
    return jnp.dot(a, b, preferred_element_type=F32)


def _dot_nt(a, b):
    return lax.dot_general(a, b, (((1,), (1,)), ((), ())), preferred_element_type=F32)


def _dot_tn(a, b):
    return lax.dot_general(a, b, (((0,), (0,)), ((), ())), preferred_element_type=F32)


def _hdot(a, b):
    return jnp.dot(a, b, precision=_HI, preferred_element_type=F32)


def _hdot_tn(a, b):
    return lax.dot_general(a, b, (((0,), (0,)), ((), ())), precision=_HI, preferred_element_type=F32)


def _hdot_nt(a, b):
    return lax.dot_general(a, b, (((1,), (1,)), ((), ())), precision=_HI, preferred_element_type=F32)


def _sigmoid(x):
    return jax.nn.sigmoid(x)


def _gelu(x):
    return 0.5 * x * (1.0 + lax.erf(x * 0.7071067811865476))


def _gelu_grad(x):
    return 0.5 * (1.0 + lax.erf(x * 0.7071067811865476)) + x * jnp.exp(-0.5 * x * x) * 0.3989422804014327


def _accum(ref, val, step):
    @pl.when(step == 0)
    def _():
        ref[...] = val

    @pl.when(step > 0)
    def _():
        ref[...] += val


def _rstd(x):
    return lax.rsqrt(jnp.mean(x * x, axis=-1, keepdims=True) + EPS)


def _rms_bwd(dy, xhat, r, nw):
    dnw = jnp.sum(dy * xhat, axis=0, keepdims=True)
    dxh = dy * nw
    dx = r * (dxh - xhat * jnp.mean(dxh * xhat, axis=-1, keepdims=True))
    return dx, dnw


def _ffn_w_specs(f, l):
    return [
        pl.BlockSpec((None, None, None, D, FC), lambda i, j: (j, f, l, 0, 0)),
        pl.BlockSpec((None, None, None, D, FC), lambda i, j: (j + NJ, f, l, 0, 0)),
        pl.BlockSpec((2, None, None, WO_ROWS, D), lambda i, j: (j, f, l, 0, 0)),
    ]


def _ffn_fwd(x, nw, w_in, w_out, f, l):
    t = x.shape[0]
    nj = NJ

    def body(x_ref, nw_ref, wg_ref, wu_ref, wo3_ref, o_ref, xn_sc, acc_sc):
        j = pl.program_id(1)

        @pl.when(j == 0)
        def _():
            xv = x_ref[...]
            xn_sc[...] = _c(xv * _rstd(xv) * nw_ref[...])
            acc_sc[...] = jnp.zeros_like(acc_sc)

        xn = xn_sc[...]
        g = _dot(xn, wg_ref[...])
        u = _dot(xn, wu_ref[...])
        acc_sc[...] += _dot(_c(g * _sigmoid(g) * u), wo3_ref[...].reshape(FC, D))

        @pl.when(j == nj - 1)
        def _():
            o_ref[...] = x_ref[...] + 0.5 * acc_sc[...]

    return pl.pallas_call(
        body,
        grid=(t // TM, nj),
        in_specs=[pl.BlockSpec((TM, D), lambda i, j: (i, 0)), pl.BlockSpec((1, D), lambda i, j: (0, 0))]
        + _ffn_w_specs(f, l),
        out_specs=pl.BlockSpec((TM, D), lambda i, j: (i, 0)),
        out_shape=jax.ShapeDtypeStruct((t, D), F32),
        scratch_shapes=[pltpu.VMEM((TM, D), _MM), pltpu.VMEM((TM, D), F32)],
        name="ffn_fwd",
    )(x, nw, w_in, w_in, w_out)


def _ffn_bwd(x, nw, w_in, w_out, f, l, dy):
    t = x.shape[0]
    nj = NJ

    def body(x_ref, nw_ref, wg_ref, wu_ref, wo3_ref, dy_ref, dx_ref, xn_ref, a_ref, dh_ref, dnw_ref,
             r_sc, dyb_sc, acc_sc):
        wo = wo3_ref[...].reshape(FC, D)
        i = pl.program_id(0)
        j = pl.program_id(1)

        @pl.when(j == 0)
        def _():
            xv = x_ref[...]
            r = _rstd(xv)
            r_sc[...] = r
            xn_ref[...] = _c(xv * r * nw_ref[...])
            dyb_sc[...] = _c(0.5 * dy_ref[...])
            acc_sc[...] = jnp.zeros_like(acc_sc)

        xn = xn_ref[...]
        g = _dot(xn, wg_ref[...])
        u = _dot(xn, wu_ref[...])
        s = _sigmoid(g)
        sl = g * s
        a_ref[...] = _c(sl * u)
        da = _dot_nt(dyb_sc[...], wo)
        dg = _c(da * u * (s * (1.0 + g * (1.0 - s))))
        du = _c(da * sl)
        dh_ref[0] = dg
        dh_ref[1] = du
        acc_sc[...] += _dot_nt(dg, wg_ref[...]) + _dot_nt(du, wu_ref[...])

        @pl.when(j == nj - 1)
        def _():
            r = r_sc[...]
            dx, dnw = _rms_bwd(acc_sc[...], x_ref[...] * r, r, nw_ref[...])
            dx_ref[...] = dy_ref[...] + dx
            _accum(dnw_ref, dnw, i)

    return pl.pallas_call(
        body,
        grid=(t // TM, nj),
        in_specs=[pl.BlockSpec((TM, D), lambda i, j: (i, 0)), pl.BlockSpec((1, D), lambda i, j: (0, 0))]
        + _ffn_w_specs(f, l) + [pl.BlockSpec((TM, D), lambda i, j: (i, 0))],
        out_specs=[
            pl.BlockSpec((TM, D), lambda i, j: (i, 0)),
            pl.BlockSpec((TM, D), lambda i, j: (i, 0)),
            pl.BlockSpec((None, TM, FC), lambda i, j: (j, i, 0)),
            pl.BlockSpec((None, 2, TM, FC), lambda i, j: (j, 0, i, 0)),
            pl.BlockSpec((1, D), lambda i, j: (0, 0)),
        ],
        out_shape=[
            jax.ShapeDtypeStruct((t, D), F32),
            jax.ShapeDtypeStruct((t, D), _MM),
            jax.ShapeDtypeStruct((nj, t, FC), _MM),
            jax.ShapeDtypeStruct((nj, 2, t, FC), _MM),
            jax.ShapeDtypeStruct((1, D), F32),
        ],
        scratch_shapes=[pltpu.VMEM((TM, 1), F32), pltpu.VMEM((TM, D), _MM), pltpu.VMEM((TM, D), F32)],
        name="ffn_bwd",
    )(x, nw, w_in, w_in, w_out, dy)


def _mm_tn(a, b, bm, bn, bt, out_dtype, name):
    t, m = a.shape
    n = b.shape[1]
    nt = t // bt

    def body(a_ref, b_ref, o_ref, acc_sc):
        k = pl.program_id(2)
        _accum(acc_sc, _dot_tn(_c(a_ref[...]), _c(b_ref[...])), k)

        @pl.when(k == nt - 1)
        def _():
            o_ref[...] = acc_sc[...].astype(out_dtype)

    return pl.pallas_call(
        body,
        grid=(m // bm, n // bn, nt),
        in_specs=[pl.BlockSpec((bt, bm), lambda i, j, k: (k, i)), pl.BlockSpec((bt, bn), lambda i, j, k: (k, j))],
        out_specs=pl.BlockSpec((bm, bn), lambda i, j, k: (i, j)),
        out_shape=jax.ShapeDtypeStruct((m, n), out_dtype),
        scratch_shapes=[pltpu.VMEM((bm, bn), F32)],
        name=name,
    )(a, b)


def _mm_tn_win(xn, dh):
    t = xn.shape[0]
    bt = 512
    nt = t // bt

    def body(a_ref, b_ref, o_ref, acc_sc):
        k = pl.program_id(2)
        _accum(acc_sc, _dot_tn(a_ref[...], b_ref[...]), k)

        @pl.when(k == nt - 1)
        def _():
            o_ref[...] = _c(acc_sc[...])

    return pl.pallas_call(
        body,
        grid=(2, NJ, nt),
        in_specs=[pl.BlockSpec((bt, D), lambda h, j, k: (k, 0)),
                  pl.BlockSpec((None, None, bt, FC), lambda h, j, k: (j, h, k, 0))],
        out_specs=pl.BlockSpec((None, D, FC), lambda h, j, k: (h * NJ + j, 0, 0)),
        out_shape=jax.ShapeDtypeStruct((N_DEV, D, FC), _MM),
        scratch_shapes=[pltpu.VMEM((D, FC), F32)],
        name="mm_tn_win",
    )(xn, dh)


def _mm_tn_wout(act, dy):
    t = dy.shape[0]
    bt = 512
    nt = t // bt

    def body(a_ref, b_ref, o_ref, acc_sc):
        k = pl.program_id(1)
        _accum(acc_sc, _dot_tn(a_ref[...], _c(b_ref[...])), k)

        @pl.when(k == nt - 1)
        def _():
            o_ref[...] = _c((0.5 * acc_sc[...]).reshape(2, WO_ROWS, D))

    return pl.pallas_call(
        body,
        grid=(NJ, nt),
        in_specs=[pl.BlockSpec((None, bt, FC), lambda j, k: (j, k, 0)), pl.BlockSpec((bt, D), lambda j, k: (k, 0))],
        out_specs=pl.BlockSpec((2, WO_ROWS, D), lambda j, k: (j, 0, 0)),
        out_shape=jax.ShapeDtypeStruct((N_DEV, WO_ROWS, D), _MM),
        scratch_shapes=[pltpu.VMEM((FC, D), F32)],
        name="mm_tn_wout",
    )(act, dy)


def _loss_head(x, nw, tgt):
    t = x.shape[0]

    def body(x_ref, nw_ref, t_ref, loss_ref, dx_ref, dnw_ref):
        i = pl.program_id(0)
        xv = x_ref[...]
        r = _rstd(xv)
        xh = xv * r
        e = xh * nw_ref[...] - t_ref[...]
        part = 0.5 * jnp.sum(jnp.mean(e * e, axis=-1, keepdims=True), axis=0, keepdims=True)
        _accum(loss_ref, jnp.broadcast_to(part, (1, 128)), i)
        dx, dnw = _rms_bwd(e * (1.0 / D), xh, r, nw_ref[...])
        dx_ref[...] = dx
        _accum(dnw_ref, dnw, i)

    return pl.pallas_call(
        body,
        grid=(t // TM,),
        in_specs=[pl.BlockSpec((TM, D), lambda i: (i, 0)), pl.BlockSpec((1, D), lambda i: (0, 0)),
                  pl.BlockSpec((TM, D), lambda i: (i, 0))],
        out_specs=[pl.BlockSpec((1, 128), lambda i: (0, 0)), pl.BlockSpec((TM, D), lambda i: (i, 0)),
                   pl.BlockSpec((1, D), lambda i: (0, 0))],
        out_shape=[jax.ShapeDtypeStruct((1, 128), F32), jax.ShapeDtypeStruct((t, D), F32),
                   jax.ShapeDtypeStruct((1, D), F32)],
        name="loss_head",
    )(x, nw, tgt)


PW_F = QKV + 5 * HW
PW_B = QKV + 3 * HW + 128


def _ab_proj(x1, nw, wab):
    t = x1.shape[0]

    def body(x_ref, nw_ref, w_ref, h_ref, qkv_ref, z_ref, su_ref, sv_ref, b_ref, a_ref):
        xv = x_ref[...]
        h = _c(xv * _rstd(xv) * nw_ref[...])
        h_ref[...] = h
        p = _dot(h, w_ref[...])
        qkv_ref[...] = p[:, 0:QKV]
        o = QKV
        for ref in (z_ref, su_ref, sv_ref, b_ref, a_ref):
            ref[...] = p[:, o:o + HW]
            o += HW

    row = lambda w: pl.BlockSpec((TM, w), lambda i: (i, 0))
    return pl.pallas_call(
        body,
        grid=(t // TM,),
        in_specs=[row(D), pl.BlockSpec((1, D), lambda i: (0, 0)), pl.BlockSpec((D, PW_F), lambda i: (0, 0))],
        out_specs=[row(D), row(QKV)] + [row(HW)] * 5,
        out_shape=[jax.ShapeDtypeStruct((t, D), _MM), jax.ShapeDtypeStruct((t, QKV), F32)]
        + [jax.ShapeDtypeStruct((t, HW), F32)] * 5,
        name="ab_proj",
    )(x1, nw, wab)


def _conv_silu_norm(x, halo, cw):
    xe = jnp.concatenate([halo, x], axis=0)
    shifted = []
    c = None
    for k in range(4):
        s = 3 - k
        xs = (xe if s == 0 else pltpu.roll(xe, s, 0))[8:, :]
        shifted.append(xs)
        term = cw[k:k + 1, :] * xs
        c = term if c is None else c + term
    return c, shifted


def _head_rsq(a):
    parts = []
    for h in range(HEADS):
        ah = a[:, h * HD:(h + 1) * HD]
        r = lax.rsqrt(jnp.sum(ah * ah, axis=-1, keepdims=True) + EPS)
        parts.append(jnp.broadcast_to(r, ah.shape))
    return jnp.concatenate(parts, axis=-1)


def _head_sum(a):
    parts = []
    for h in range(HEADS):
        ah = a[:, h * HD:(h + 1) * HD]
        parts.append(jnp.broadcast_to(jnp.sum(ah, axis=-1, keepdims=True), ah.shape))
    return jnp.concatenate(parts, axis=-1)


def _softplus(x):
    return jnp.maximum(x, 0.0) + jnp.log1p(jnp.exp(-jnp.abs(x)))


def _halo_prev_spec(width, rows):
    per = TM // rows
    return pl.BlockSpec((rows, width), lambda i: (jnp.maximum(i * per - 1, 0), 0))


def _halo_next_spec(width, rows, t):
    per = TM // rows
    last = t // rows - 1
    return pl.BlockSpec((rows, width), lambda i: (jnp.minimum((i + 1) * per, last), 0))


def _dn_pre(qkv, b_rep, a_rep, cw, alog, dtb):
    t = qkv.shape[0]
    qscale = HD ** -0.5

    def body(x_ref, halo_ref, b_ref, a_ref, cw_ref, alog_ref, dt_ref, q_ref, k_ref, v_ref, beta_ref, g_ref):
        i = pl.program_id(0)
        halo = jnp.where(i == 0, 0.0, halo_ref[...])
        c, _ = _conv_silu_norm(x_ref[...], halo, cw_ref[...])
        sc = c * _sigmoid(c)
        q = sc[:, 0:HW]
        k = sc[:, HW:2 * HW]
        q_ref[...] = q * _head_rsq(q) * qscale
        k_ref[...] = k * _head_rsq(k)
        v_ref[...] = sc[:, 2 * HW:]
        beta_ref[...] = _sigmoid(b_ref[...])
        g_ref[...] = -jnp.exp(alog_ref[...]) * _softplus(a_ref[...] + dt_ref[...])

    row = lambda w: pl.BlockSpec((TM, w), lambda i: (i, 0))
    full = lambda a: pl.BlockSpec(a.shape, lambda i: (0,) * a.ndim)
    return pl.pallas_call(
        body,
        grid=(t // TM,),
        in_specs=[row(QKV), _halo_prev_spec(QKV, 8), row(HW), row(HW), full(cw), full(alog), full(dtb)],
        out_specs=[row(HW)] * 5,
        out_shape=[jax.ShapeDtypeStruct((t, HW), F32)] * 5,
        name="dn_pre",
    )(qkv, qkv, b_rep, a_rep, cw, alog, dtb)


def _unit_lower_inv(lo, eye):
    p = eye - lo
    lp = lo
    for _ in range(5):
        lp = _hdot(lp, lp)
        p = p + _hdot(p, lp)
    return p


def _dn_chunk_local(q, k, v, b, gr):
    ri = lax.broadcasted_iota(jnp.int32, (DNC, DNC), 0)
    ci = lax.broadcasted_iota(jnp.int32, (DNC, DNC), 1)
    strict = ri > ci
    causal = ri >= ci
    ltri = causal.astype(F32)
    eye = (ri == ci).astype(F32)
    diff = _hdot(ltri, jnp.where(strict, gr[:, 0:DNC], 0.0))
    dm = jnp.where(causal, jnp.exp(diff), 0.0)
    gc = _hdot(ltri, gr)
    gl = jnp.sum(gr, axis=0, keepdims=True)
    e = jnp.exp(gc)
    f = jnp.exp(gl - gc)
    kb = k * b
    m = _dot_nt(_c(kb), _c(k))
    lo = jnp.where(strict, m * dm, 0.0)
    a = _unit_lower_inv(lo, eye)
    vb = v * b
    kbe = kb * e
    u = _hdot(a, vb)
    w = _hdot(a, kbe)
    p = _dot_nt(_c(q), _c(k))
    return dict(strict=strict, ltri=ltri, dm=dm, e=e, f=f, gl=gl, kb=kb, m=m, a=a, vb=vb, kbe=kbe, u=u, w=w,
                p=p, attn=p * dm, qd=q * e, kt=k * f)


def _dn_fwd(q, k, v, beta, g):
    t = q.shape[0]
    n = t // DNC

    def body(q_ref, k_ref, v_ref, b_ref, g_ref, o_ref, sall_ref, s_sc):
        i = pl.program_id(0)

        @pl.when(i == 0)
        def _():
            s_sc[...] = jnp.zeros_like(s_sc)

        for h in range(HEADS):
            sl = slice(h * HD, (h + 1) * HD)
            cl = _dn_chunk_local(q_ref[:, sl], k_ref[:, sl], v_ref[:, sl], b_ref[:, sl], g_ref[:, sl])
            s = s_sc[h]
            sall_ref[h] = s
            sb = _c(s)
            vn = cl["u"] - _dot(_c(cl["w"]), sb)
            o_ref[:, sl] = _dot(_c(cl["qd"]), sb) + _dot(_c(cl["attn"]), _c(vn))
            s_sc[h] = s * jnp.exp(cl["gl"]) + _dot_tn(_c(cl["kt"]), _c(vn))

    row = pl.BlockSpec((DNC, HW), lambda i: (i, 0))
    return pl.pallas_call(
        body,
        grid=(n,),
        in_specs=[row] * 5,
        out_specs=[row, pl.BlockSpec((None, HEADS, HD, HD), lambda i: (i, 0, 0, 0))],
        out_shape=[jax.ShapeDtypeStruct((t, HW), F32), jax.ShapeDtypeStruct((n, HEADS, HD, HD), F32)],
        scratch_shapes=[pltpu.VMEM((HEADS, HD, HD), F32)],
        name="dn_fwd",
    )(q, k, v, beta, g)


def _dn_bwd(q, k, v, beta, g, sall, do):
    t = q.shape[0]
    n = t // DNC

    def body(q_ref, k_ref, v_ref, b_ref, g_ref, sall_ref, do_ref, dq_ref, dk_ref, dv_ref, db_ref, dg_ref, ds_sc):
        i = pl.program_id(0)

        @pl.when(i == 0)
        def _():
            ds_sc[...] = jnp.zeros_like(ds_sc)

        for h in range(HEADS):
            sl = slice(h * HD, (h + 1) * HD)
            qh, kh, vh, bh = q_ref[:, sl], k_ref[:, sl], v_ref[:, sl], b_ref[:, sl]
            cl = _dn_chunk_local(qh, kh, vh, bh, g_ref[:, sl])
            strict, dm, e, f, a = cl["strict"], cl["dm"], cl["e"], cl["f"], cl["a"]
            s = sall_ref[h]
            sb = _c(s)
            dsn = ds_sc[h]
            dsb = _c(dsn)
            dob = _c(do_ref[:, sl])
            egl = jnp.exp(cl["gl"])
            vn = cl["u"] - _dot(_c(cl["w"]), sb)
            vnb = _c(vn)
            dvn = _dot_tn(_c(cl["attn"]), dob) + _dot(_c(cl["kt"]), dsb)
            dvnb = _c(dvn)
            dqd = _dot_nt(dob, sb)
            dattn = _dot_nt(dob, vnb)
            dkt = _dot_nt(vnb, dsb)
            dgl = jnp.sum(jnp.sum(dsn * s, axis=1, keepdims=True), axis=0, keepdims=True) * egl[:, 0:1]
            dw = -_dot_nt(dvnb, sb)
            ds_sc[h] = dsn * egl + _dot_tn(_c(cl["qd"]), dob) - _dot_tn(_c(cl["w"]), dvnb)
            dp = _c(dattn * dm)
            dq = dqd * e + _dot(dp, _c(kh))
            dk = _dot_tn(dp, _c(qh)) + dkt * f
            dd = dattn * cl["p"]
            de = jnp.sum(dqd * qh, axis=-1, keepdims=True)
            dff = jnp.sum(dkt * kh, axis=-1, keepdims=True) * f[:, 0:1]
            dgl = dgl + jnp.sum(dff, axis=0, keepdims=True)
            dvb = _hdot_tn(a, dvn)
            dkbe = _hdot_tn(a, dw)
            dlo = jnp.where(strict, -(_hdot_nt(dvb, cl["u"]) + _hdot_nt(dkbe, cl["w"])), 0.0)
            dmm = _c(dlo * dm)
            dd = dd + dlo * cl["m"]
            dkb = _dot(dmm, _c(kh)) + dkbe * e
            dk = dk + _dot_tn(dmm, _c(cl["kb"])) + dkb * bh
            de = de + jnp.sum(dkbe * cl["kb"], axis=-1, keepdims=True)
            dbeta = jnp.sum(dkb * kh, axis=-1, keepdims=True) + jnp.sum(dvb * vh, axis=-1, keepdims=True)
            gm = dd * dm
            ri = lax.broadcasted_iota(jnp.int32, (DNC, DNC), 0)
            ci = lax.broadcasted_iota(jnp.int32, (DNC, DNC), 1)
            upper = (ri <= ci).astype(F32)
            dgc = de * e[:, 0:1] - dff
            dg = (jnp.sum(jnp.where(strict, _hdot(upper, gm), 0.0), axis=-1, keepdims=True)
                  + _hdot(upper, jnp.broadcast_to(dgc, (DNC, HD)))[:, 0:1] + dgl)
            dq_ref[:, sl] = dq
            dk_ref[:, sl] = dk
            dv_ref[:, sl] = dvb * bh
            db_ref[:, sl] = jnp.broadcast_to(dbeta, (DNC, HD))
            dg_ref[:, sl] = jnp.broadcast_to(dg, (DNC, HD))

    row = pl.BlockSpec((DNC, HW), lambda i: (n - 1 - i, 0))
    return pl.pallas_call(
        body,
        grid=(n,),
        in_specs=[row] * 5 + [pl.BlockSpec((None, HEADS, HD, HD), lambda i: (n - 1 - i, 0, 0, 0)), row],
        out_specs=[row] * 5,
        out_shape=[jax.ShapeDtypeStruct((t, HW), F32)] * 5,
        scratch_shapes=[pltpu.VMEM((HEADS, HD, HD), F32)],
        name="dn_bwd",
    )(q, k, v, beta, g, sall, do)


def _group_norm(a, nw):
    rs = []
    for h in range(HEADS):
        ah = a[:, h * HD:(h + 1) * HD]
        rs.append(jnp.broadcast_to(_rstd(ah), ah.shape))
    r = jnp.concatenate(rs, axis=-1)
    xh = a * r
    return xh * nw, xh, r


def _group_norm_bwd(dy, xh, r, nw):
    dxh = dy * nw
    return r * (dxh - xh * (_head_sum(dxh * xh) * (1.0 / HD)))


def _sg_mix(wt_ref, svn_b, nchunk):
    rows = []
    for cidx in range(nchunk):
        cols = []
        for g in range(HEADS):
            blk = svn_b[cidx * SGC:(cidx + 1) * SGC, g * HD:(g + 1) * HD]
            cols.append(_dot(wt_ref[g], blk))
        rows.append(jnp.concatenate(cols, axis=-1))
    return jnp.concatenate(rows, axis=0)


def _ab_out(x1, o, z, su, sv, dnw, sgnw, wtril, sgb, wout):
    t = x1.shape[0]
    nchunk = TM // SGC

    def body(x_ref, o_ref, z_ref, su_ref, sv_ref, dnw_ref, sgnw_ref, wt_ref, sgb_ref, wo_ref, x2_ref, cat_ref):
        on, _, _ = _group_norm(o_ref[...], dnw_ref[...])
        zv = z_ref[...]
        cat_ref[:, 0:HW] = _c(on * (zv * _sigmoid(zv)))
        svn, _, _ = _group_norm(_gelu(sv_ref[...]), sgnw_ref[...])
        mixed = _sg_mix(wt_ref, _c(svn), nchunk) + jnp.tile(sgb_ref[...], (nchunk, 1))
        cat_ref[:, HW:] = _c(_gelu(su_ref[...]) * mixed)
        x2_ref[...] = x_ref[...] + _dot(cat_ref[...], wo_ref[...])

    row = lambda w: pl.BlockSpec((TM, w), lambda i: (i, 0))
    full = lambda a: pl.BlockSpec(a.shape, lambda i: (0,) * a.ndim)
    return pl.pallas_call(
        body,
        grid=(t // TM,),
        in_specs=[row(D)] + [row(HW)] * 4 + [full(dnw), full(sgnw), full(wtril), full(sgb), full(wout)],
        out_specs=[row(D), row(D)],
        out_shape=[jax.ShapeDtypeStruct((t, D), F32), jax.ShapeDtypeStruct((t, D), _MM)],
        name="ab_out",
    )(x1, o, z, su, sv, dnw, sgnw, wtril, sgb, wout)


def _ab_out_bwd(dx2, o, z, su, sv, dnw, sgnw, wtril, wtril_t, sgb, wout):
    t = dx2.shape[0]
    nchunk = TM // SGC

    def body(dx_ref, o_ref, z_ref, su_ref, sv_ref, dnw_ref, sgnw_ref, wt_ref, wtt_ref, sgb_ref, wo_ref,
             do_ref, dz_ref, dsu_ref, dsv_ref, ddnw_ref, dsgnw_ref, dsgw_ref, dsgb_ref):
        i = pl.program_id(0)
        dcat = _dot_nt(_c(dx_ref[...]), wo_ref[...])
        doa = dcat[:, 0:HW]
        dob = dcat[:, HW:]
        on, oh, ro = _group_norm(o_ref[...], dnw_ref[...])
        zv = z_ref[...]
        sz = _sigmoid(zv)
        dz_ref[...] = _c(doa * on * (sz * (1.0 + zv * (1.0 - sz))))
        don = doa * (zv * sz)
        do_ref[...] = _group_norm_bwd(don, oh, ro, dnw_ref[...])
        dd = jnp.sum(don * oh, axis=0, keepdims=True)
        _accum(ddnw_ref, dd[:, 0:HD] + dd[:, HD:2 * HD] + dd[:, 2 * HD:3 * HD] + dd[:, 3 * HD:], i)
        suv = su_ref[...]
        svv = sv_ref[...]
        svg = _gelu(svv)
        svn, sh, rs = _group_norm(svg, sgnw_ref[...])
        svn_b = _c(svn)
        mixed = _sg_mix(wt_ref, svn_b, nchunk) + jnp.tile(sgb_ref[...], (nchunk, 1))
        dsu_ref[...] = _c(dob * mixed * _gelu_grad(suv))
        dmixed = dob * _gelu(suv)
        dmb = _c(dmixed)
        tri = lax.broadcasted_iota(jnp.int32, (SGC, SGC), 0) >= lax.broadcasted_iota(jnp.int32, (SGC, SGC), 1)
        lane = lax.broadcasted_iota(jnp.int32, (SGC, HD), 1)
        rows = []
        dbias = jnp.zeros((SGC, HD), F32)
        for g in range(HEADS):
            gs = slice(g * HD, (g + 1) * HD)
            dwg = jnp.zeros((SGC, SGC), F32)
            col = jnp.zeros((SGC, 1), F32)
            for cidx in range(nchunk):
                cs = slice(cidx * SGC, (cidx + 1) * SGC)
                dwg = dwg + _dot_nt(dmb[cs, gs], svn_b[cs, gs])
                col = col + jnp.sum(dmixed[cs, gs], axis=-1, keepdims=True)
            _accum(dsgw_ref.at[g], jnp.where(tri, dwg, 0.0), i)
            dbias = dbias + jnp.where(lane == g, col, 0.0)
        _accum(dsgb_ref, dbias, i)
        for cidx in range(nchunk):
            cs = slice(cidx * SGC, (cidx + 1) * SGC)
            rows.append(jnp.concatenate(
                [_dot(wtt_ref[g], dmb[cs, g * HD:(g + 1) * HD]) for g in range(HEADS)], axis=-1))
        dsvn = jnp.concatenate(rows, axis=0)
        _accum(dsgnw_ref, jnp.sum(dsvn * sh, axis=0, keepdims=True), i)
        dsv_ref[...] = _c(_group_norm_bwd(dsvn, sh, rs, sgnw_ref[...]) * _gelu_grad(svv))

    row = lambda w: pl.BlockSpec((TM, w), lambda i: (i, 0))
    full = lambda a: pl.BlockSpec(a.shape, lambda i: (0,) * a.ndim)
    const = lambda shape: pl.BlockSpec(shape, lambda i: (0,) * len(shape))
    return pl.pallas_call(
        body,
        grid=(t // TM,),
        in_specs=[row(D)] + [row(HW)] * 4 + [full(dnw), full(sgnw), full(wtril), full(wtril_t), full(sgb), full(wout)],
        out_specs=[row(HW)] * 4 + [const((1, HD)), const((1, HW)), const((HEADS, SGC, SGC)), const((SGC, HD))],
        out_shape=[jax.ShapeDtypeStruct((t, HW), F32)] + [jax.ShapeDtypeStruct((t, HW), _MM)] * 3
        + [jax.ShapeDtypeStruct((1, HD), F32), jax.ShapeDtypeStruct((1, HW), F32),
           jax.ShapeDtypeStruct((HEADS, SGC, SGC), F32), jax.ShapeDtypeStruct((SGC, HD), F32)],
        name="ab_out_bwd",
    )(dx2, o, z, su, sv, dnw, sgnw, wtril, wtril_t, sgb, wout)


def _dn_pre_bwd(qkv, b_rep, a_rep, cw, alog, dtb, dqn, dkn, dv, dbeta, dg):
    t = qkv.shape[0]
    qscale = HD ** -0.5

    def body(x_ref, halo_ref, b_ref, a_ref, cw_ref, alog_ref, dt_ref, dq_ref, dk_ref, dv_ref, dbeta_ref, dg_ref,
             dc_ref, dba_ref, dcw_ref, dalog_ref, ddt_ref):
        i = pl.program_id(0)
        halo = jnp.where(i == 0, 0.0, halo_ref[...])
        c, shifted = _conv_silu_norm(x_ref[...], halo, cw_ref[...])
        s = _sigmoid(c)
        sc = c * s
        q = sc[:, 0:HW]
        k = sc[:, HW:2 * HW]
        rq = _head_rsq(q)
        rk = _head_rsq(k)
        qu = q * rq
        ku = k * rk
        dqn = dq_ref[...]
        dkn = dk_ref[...]
        dq = qscale * rq * (dqn - qu * _head_sum(dqn * qu))
        dk = rk * (dkn - ku * _head_sum(dkn * ku))
        dsc = jnp.concatenate([dq, dk, dv_ref[...]], axis=-1)
        dc = dsc * (s * (1.0 + c * (1.0 - s)))
        dc_ref[...] = dc
        for kk in range(4):
            _accum(dcw_ref.at[kk], jnp.sum(dc * shifted[kk], axis=0, keepdims=True), i)
        beta = _sigmoid(b_ref[...])
        dbp = dbeta_ref[...] * beta * (1.0 - beta)
        nea = -jnp.exp(alog_ref[...])
        spin = a_ref[...] + dt_ref[...]
        dgv = dg_ref[...]
        dap = dgv * nea * _sigmoid(spin)
        _accum(dalog_ref, jnp.sum(dgv * nea * _softplus(spin), axis=0, keepdims=True), i)
        _accum(ddt_ref, jnp.sum(dap, axis=0, keepdims=True), i)
        lane = lax.broadcasted_iota(jnp.int32, (TM, HD), 1)
        dba = jnp.zeros((TM, HD), F32)
        for h in range(HEADS):
            dba = dba + jnp.where(lane == h, dbp[:, h * HD:(h + 1) * HD], 0.0)
            dba = dba + jnp.where(lane == HEADS + h, dap[:, h * HD:(h + 1) * HD], 0.0)
        dba_ref[...] = _c(dba)

    row = lambda w: pl.BlockSpec((TM, w), lambda i: (i, 0))
    full = lambda a: pl.BlockSpec(a.shape, lambda i: (0,) * a.ndim)
    const = lambda shape: pl.BlockSpec(shape, lambda i: (0,) * len(shape))
    return pl.pallas_call(
        body,
        grid=(t // TM,),
        in_specs=[row(QKV), _halo_prev_spec(QKV, 8), row(HW), row(HW), full(cw), full(alog), full(dtb)] + [row(HW)] * 5,
        out_specs=[row(QKV), row(HD), const((4, 1, QKV)), const((1, HW)), const((1, HW))],
        out_shape=[jax.ShapeDtypeStruct((t, QKV), F32), jax.ShapeDtypeStruct((t, HD), _MM),
                   jax.ShapeDtypeStruct((4, 1, QKV), F32), jax.ShapeDtypeStruct((1, HW), F32),
                   jax.ShapeDtypeStruct((1, HW), F32)],
        name="dn_pre_bwd",
    )(qkv, qkv, b_rep, a_rep, cw, alog, dtb, dqn, dkn, dv, dbeta, dg)


def _conv_bwd(dc, cw):
    t = dc.shape[0]
    nt = t // TM

    def body(dc_ref, halo_ref, cw_ref, dx_ref):
        i = pl.program_id(0)
        halo = jnp.where(i == nt - 1, 0.0, halo_ref[...])
        de = jnp.concatenate([dc_ref[...], halo], axis=0)
        cwv = cw_ref[...]
        acc = None
        for k in range(4):
            s = 3 - k
            ds = (de if s == 0 else pltpu.roll(de, TM + 8 - s, 0))[0:TM, :]
            term = cwv[k:k + 1, :] * ds
            acc = term if acc is None else acc + term
        dx_ref[...] = _c(acc)

    return pl.pallas_call(
        body,
        grid=(nt,),
        in_specs=[pl.BlockSpec((TM, QKV), lambda i: (i, 0)), _halo_next_spec(QKV, 8, t),
                  pl.BlockSpec(cw.shape, lambda i: (0, 0))],
        out_specs=pl.BlockSpec((TM, QKV), lambda i: (i, 0)),
        out_shape=jax.ShapeDtypeStruct((t, QKV), _MM),
        name="conv_bwd",
    )(dc, dc, cw)


def _ab_proj_bwd(x1, nw, dqkv, dz, dsu, dsv, dba, wab_b, dres):
    t = x1.shape[0]

    def body(x_ref, nw_ref, dqkv_ref, dz_ref, dsu_ref, dsv_ref, dba_ref, w_ref, dres_ref, dx_ref, dcat_ref, dnw_ref):
        i = pl.program_id(0)
        dcat_ref[:, 0:QKV] = dqkv_ref[...]
        o = QKV
        for ref in (dz_ref, dsu_ref, dsv_ref):
            dcat_ref[:, o:o + HW] = ref[...]
            o += HW
        dcat_ref[:, o:o + 128] = dba_ref[...]
        dh = _dot_nt(dcat_ref[...], w_ref[...])
        xv = x_ref[...]
        r = _rstd(xv)
        dx, dnw = _rms_bwd(dh, xv * r, r, nw_ref[...])
        dx_ref[...] = dres_ref[...] + dx
        _accum(dnw_ref, dnw, i)

    row = lambda w: pl.BlockSpec((TM, w), lambda i: (i, 0))
    return pl.pallas_call(
        body,
        grid=(t // TM,),
        in_specs=[row(D), pl.BlockSpec((1, D), lambda i: (0, 0)), row(QKV), row(HW), row(HW), row(HW), row(128),
                  pl.BlockSpec((D, PW_B), lambda i: (0, 0)), row(D)],
        out_specs=[row(D), row(PW_B), pl.BlockSpec((1, D), lambda i: (0, 0))],
        out_shape=[jax.ShapeDtypeStruct((t, D), F32), jax.ShapeDtypeStruct((t, PW_B), _MM),
                   jax.ShapeDtypeStruct((1, D), F32)],
        name="ab_proj_bwd",
    )(x1, nw, dqkv, dz, dsu, dsv, dba, wab_b, dres)


def _pool_counts(i):
    pos = (lax.broadcasted_iota(jnp.int32, (TM + HALO, 1), 0) + i * TM + 1).astype(F32)
    return [1.0 / jnp.minimum(pos, float(w)) for w in POOL_WINDOWS]


def _window_sum(ext, win, back):
    r = ext.shape[0]
    s = ext
    step = 1
    while step < win:
        s = s + pltpu.roll(s, step if back else r - step, 0)
        step *= 2
    return s


def _pooled(h_ext, invc, g):
    gs = slice(g * PG, (g + 1) * PG)
    he = h_ext[:, gs]
    ws = _window_sum(he, POOL_WINDOWS[g], True)[HALO:, :]
    return ws * invc[g][0:TM, :] - he[HALO:, :]


def _pool_fwd(x1, nw, pw, scale):
    t = x1.shape[0]

    def body(x_ref, halo_ref, nw_ref, pw_ref, sc_ref, x2_ref):
        i = pl.program_id(0)
        xv = x_ref[...]
        hv = halo_ref[...]
        nwv = nw_ref[...]
        h_ext = jnp.concatenate([jnp.where(i == 0, 0.0, hv * _rstd(hv) * nwv), xv * _rstd(xv) * nwv], axis=0)
        invc = _pool_counts(i)
        outs = [_dot(_c(_pooled(h_ext, invc, g)), pw_ref[g]) for g in range(4)]
        x2_ref[...] = xv + jnp.concatenate(outs, axis=-1) * sc_ref[...]

    return pl.pallas_call(
        body,
        grid=(t // TM,),
        in_specs=[pl.BlockSpec((TM, D), lambda i: (i, 0)), _halo_prev_spec(D, HALO),
                  pl.BlockSpec((1, D), lambda i: (0, 0)), pl.BlockSpec((4, PG, PG), lambda i: (0, 0, 0)),
                  pl.BlockSpec((1, D), lambda i: (0, 0))],
        out_specs=pl.BlockSpec((TM, D), lambda i: (i, 0)),
        out_shape=jax.ShapeDtypeStruct((t, D), F32),
        name="pool_fwd",
    )(x1, x1, nw, pw, scale)


def _pool_bwd(x1, nw, pw, scale, dx2):
    t = x1.shape[0]
    nt = t // TM

    def body(x_ref, halo_ref, nw_ref, pw_ref, sc_ref, dx2_ref, dnext_ref, dx_ref, dnw_ref, dpw_ref, dsc_ref):
        i = pl.program_id(0)
        xv = x_ref[...]
        hv = halo_ref[...]
        nwv = nw_ref[...]
        r = _rstd(xv)
        xh = xv * r
        h_ext = jnp.concatenate([jnp.where(i == 0, 0.0, hv * _rstd(hv) * nwv), xh * nwv], axis=0)
        invc = _pool_counts(i)
        dyv = dx2_ref[...]
        dout_ext = jnp.concatenate([dyv, jnp.where(i == nt - 1, 0.0, dnext_ref[...])], axis=0) * sc_ref[...]
        dh_parts = []
        dsc_parts = []
        for g in range(4):
            gs = slice(g * PG, (g + 1) * PG)
            pooled_b = _c(_pooled(h_ext, invc, g))
            dout_b = _c(dout_ext[:, gs])
            dsc_parts.append(jnp.sum(dyv[:, gs] * _dot(pooled_b, pw_ref[g]), axis=0, keepdims=True))
            _accum(dpw_ref.at[g], _dot_tn(pooled_b, dout_b[0:TM, :]), i)
            dpool_ext = _dot_nt(dout_b, pw_ref[g])
            lead = _window_sum(dpool_ext * invc[g], POOL_WINDOWS[g], False)[0:TM, :]
            dh_parts.append(lead - dpool_ext[0:TM, :])
        _accum(dsc_ref, jnp.concatenate(dsc_parts, axis=-1), i)
        dx, dnw = _rms_bwd(jnp.concatenate(dh_parts, axis=-1), xh, r, nwv)
        dx_ref[...] = dyv + dx
        _accum(dnw_ref, dnw, i)

    vec = pl.BlockSpec((1, D), lambda i: (0, 0))
    return pl.pallas_call(
        body,
        grid=(nt,),
        in_specs=[pl.BlockSpec((TM, D), lambda i: (i, 0)), _halo_prev_spec(D, HALO), vec,
                  pl.BlockSpec((4, PG, PG), lambda i: (0, 0, 0)), vec,
                  pl.BlockSpec((TM, D), lambda i: (i, 0)), _halo_next_spec(D, HALO, t)],
        out_specs=[pl.BlockSpec((TM, D), lambda i: (i, 0)), vec, pl.BlockSpec((4, PG, PG), lambda i: (0, 0, 0)), vec],
        out_shape=[jax.ShapeDtypeStruct((t, D), F32), jax.ShapeDtypeStruct((1, D), F32),
                   jax.ShapeDtypeStruct((4, PG, PG), F32), jax.ShapeDtypeStruct((1, D), F32)],
        name="pool_bwd",
    )(x1, x1, nw, pw, scale, dx2, dx2)


BIG = (
    ("ffn1_w_in", (2, D, 2 * FF), 2), ("ffn1_w_out", (2, FF, D), 1),
    ("ffn2_w_in", (2, D, 2 * FF), 2), ("ffn2_w_out", (2, FF, D), 1),
    ("ab_w_in", (1, D, 3080), 2), ("ab_w_out", (1, D, D), 1),
    ("pool_w", (1, 4, PG, PG), 2), ("pool_scale", (1, D), 1), ("dn_conv_w", (1, 4, QKV), 2),
)
BIG_F32 = ("pool_scale", "dn_conv_w")
SMALL = (
    ("ffn_norm1", (2, D)), ("mix_norm", (2, D)), ("ffn_norm2", (2, D)), ("dn_a_log", (1, 4)), ("dn_dt_bias", (1, 4)),
    ("dn_out_norm", (1, HD)), ("sg_norm", (1, 4, HD)), ("sg_w", (1, 4, SGC, SGC)), ("sg_b", (1, 4, SGC)),
    ("final_norm", (D,)),
)
WEIGHT_ORDER = ("ffn_norm1", "ffn1_w_in", "ffn1_w_out", "mix_norm", "ffn_norm2", "ffn2_w_in", "ffn2_w_out", "ab_w_in",
                "dn_conv_w", "dn_a_log", "dn_dt_bias", "dn_out_norm", "sg_norm", "sg_w", "sg_b", "ab_w_out", "pool_w",
                "pool_scale", "final_norm")
ROW = 1024
R_AG = 4784
R_BIG = 4800
R_SMALL = 80
ADAM_ROWS = 480


def _numel(shape):
    n = 1
    for s in shape:
        n *= s
    return n


def _to_shards(a, axis):
    s = a.shape
    a = a.reshape(s[:axis] + (N_DEV, s[axis] // N_DEV) + s[axis + 1:])
    return jnp.moveaxis(a, axis, 0).reshape(N_DEV, -1)


def _from_shards(f, shape, axis):
    a = f.reshape((N_DEV,) + shape[:axis] + (shape[axis] // N_DEV,) + shape[axis + 1:])
    return jnp.moveaxis(a, 0, axis).reshape(shape)


def _shard_shape(shape, axis):
    return shape[:axis] + (shape[axis] // N_DEV,) + shape[axis + 1:]


def _pack_rows(flat, rows):
    pad = rows * ROW - flat.shape[-1]
    flat = jnp.pad(flat, [(0, 0)] * (flat.ndim - 1) + [(0, pad)])
    return flat.reshape(flat.shape[:-1] + (rows, ROW))


def _pack_big(shards):
    return _pack_rows(jnp.concatenate([shards[n].reshape(-1) for n, _, _ in BIG]), R_BIG)


def _unpack_big(packed):
    flat = packed.reshape(-1)
    out, o = {}, 0
    for n, shape, axis in BIG:
        ss = _shard_shape(shape, axis)
        out[n] = flat[o:o + _numel(ss)].reshape(ss)
        o += _numel(ss)
    return out


def _pack_small(vals):
    return _pack_rows(jnp.concatenate([vals[n].reshape(-1) for n, _ in SMALL]), R_SMALL)


def _unpack_small(packed):
    flat = packed.reshape(-1)
    out, o = {}, 0
    for n, shape in SMALL:
        out[n] = flat[o:o + _numel(shape)].reshape(shape)
        o += _numel(shape)
    return out


def _pack_ag(shards):
    parts = []
    for n, _, _ in BIG:
        if n in BIG_F32:
            parts.append(lax.bitcast_convert_type(shards[n].reshape(-1), _MM).reshape(-1))
        else:
            parts.append(shards[n].reshape(-1).astype(_MM))
    return _pack_rows(jnp.concatenate(parts), R_AG)


def _unpack_ag(gathered):
    flat = gathered.reshape(N_DEV, -1)
    out, o = {}, 0
    for n, shape, axis in BIG:
        ne = _numel(shape) // N_DEV
        if n in BIG_F32:
            per = 4 // jnp.dtype(_MM).itemsize
            f = flat[:, o:o + per * ne]
            f = lax.bitcast_convert_type(f.reshape(N_DEV, ne, per) if per > 1 else f, F32)
            o += per * ne
        else:
            f = flat[:, o:o + ne]
            o += ne
        out[n] = _from_shards(f, shape, axis)
    return out


def _peer(k, x, y, c):
    px = 1 - x if k & 4 else x
    py = 1 - y if k & 2 else y
    pc = 1 - c if k & 1 else c
    return px, py, pc


def _all_gather(xs):
    m_per, n = xs.shape

    def body(x_ref, out_ref, send_sems, recv_sems, local_sem):
        x, y, c = lax.axis_index("x"), lax.axis_index("y"), lax.axis_index("c")
        me, sibling = (x, y, c), (x, y, 1 - c)
        chips = [(1 - x, y), (x, 1 - y), (1 - x, 1 - y)]

        def rows(px, py, pc):
            return out_ref.at[pl.ds((4 * px + 2 * py + pc) * m_per, m_per), :]

        def copy(k, block, to, src=None):
            return pltpu.make_async_remote_copy(
                src_ref=rows(*block) if src is None else src, dst_ref=rows(*block),
                send_sem=send_sems.at[k], recv_sem=recv_sems.at[k], device_id=to, device_id_type=MESH_T)

        mine = pltpu.make_async_copy(x_ref, rows(*me), local_sem)
        mine.start()
        first = [copy(0, me, sibling, src=x_ref)]
        first += [copy(1 + j, me, (*chip, c), src=x_ref) for j, chip in enumerate(chips)]
        for cp in first:
            cp.start()
        passed = [copy(4 + j, (*chip, c), sibling) for j, chip in enumerate(chips)]
        for j, chip in enumerate(chips):
            copy(1 + j, (*chip, c), me).wait_recv()
            passed[j].start()
        copy(0, sibling, me).wait_recv()
        for j, chip in enumerate(chips):
            copy(4 + j, (*chip, 1 - c), me).wait_recv()
        for cp in first + passed:
            cp.wait_send()
        mine.wait()

    return pl.pallas_call(
        body,
        out_shape=jax.ShapeDtypeStruct((N_DEV * m_per, n), xs.dtype),
        in_specs=[pl.BlockSpec(memory_space=pltpu.HBM)],
        out_specs=pl.BlockSpec(memory_space=pltpu.HBM),
        scratch_shapes=[pltpu.SemaphoreType.DMA((7,)), pltpu.SemaphoreType.DMA((7,)), pltpu.SemaphoreType.DMA],
        name="ag_weights",
    )(xs)


def _exchange_grads(gbig, gsmall):
    def body(g_ref, s_ref, lb_ref, ls_ref, send_b, recv_b, send_s, recv_s, loc_sems):
        x, y, c = lax.axis_index("x"), lax.axis_index("y"), lax.axis_index("c")
        me = 4 * x + 2 * y + c
        own_b = pltpu.make_async_copy(g_ref.at[me], lb_ref.at[me], loc_sems.at[0])
        own_s = pltpu.make_async_copy(s_ref, ls_ref.at[me], loc_sems.at[1])
        own_b.start()
        own_s.start()
        copies = []
        for k in range(1, N_DEV):
            px, py, pc = _peer(k, x, y, c)
            peer = 4 * px + 2 * py + pc
            copies.append(pltpu.make_async_remote_copy(
                src_ref=g_ref.at[peer], dst_ref=lb_ref.at[me], send_sem=send_b.at[k - 1], recv_sem=recv_b.at[k - 1],
                device_id=(px, py, pc), device_id_type=MESH_T))
            copies.append(pltpu.make_async_remote_copy(
                src_ref=s_ref, dst_ref=ls_ref.at[me], send_sem=send_s.at[k - 1], recv_sem=recv_s.at[k - 1],
                device_id=(px, py, pc), device_id_type=MESH_T))
        for cp in copies:
            cp.start()
        for cp in copies:
            cp.wait()
        own_b.wait()
        own_s.wait()

    hbm = pl.BlockSpec(memory_space=pltpu.HBM)
    return pl.pallas_call(
        body,
        out_shape=[jax.ShapeDtypeStruct(gbig.shape, gbig.dtype),
                   jax.ShapeDtypeStruct((N_DEV,) + gsmall.shape, gsmall.dtype)],
        in_specs=[hbm, hbm],
        out_specs=[hbm, hbm],
        scratch_shapes=[pltpu.SemaphoreType.DMA((7,)), pltpu.SemaphoreType.DMA((7,)),
                        pltpu.SemaphoreType.DMA((7,)), pltpu.SemaphoreType.DMA((7,)),
                        pltpu.SemaphoreType.DMA((2,))],
        name="exchange_grads",
    )(gbig, gsmall)


def _adamw(land, w, m, v, tr, name):
    rows = w.shape[0]

    def body(l_ref, w_ref, m_ref, v_ref, g_ref, d_ref, m2_ref, v2_ref):
        g = l_ref[0].astype(F32)
        for s in range(1, N_DEV):
            g = g + l_ref[s].astype(F32)
        m2 = ADAM_B1 * m_ref[...] + (1.0 - ADAM_B1) * g
        v2 = ADAM_B2 * v_ref[...] + (1.0 - ADAM_B2) * (g * g)
        m_hat = m2 / (1.0 - ADAM_B1 ** ADAM_STEP)
        v_hat = v2 / (1.0 - ADAM_B2 ** ADAM_STEP)
        g_ref[...] = g
        d_ref[...] = -ADAM_LR * (m_hat / (jnp.sqrt(v_hat) + ADAM_EPS) + ADAM_WD * w_ref[...])
        m2_ref[...] = m2
        v2_ref[...] = v2

    row = pl.BlockSpec((tr, ROW), lambda i: (i, 0))
    return pl.pallas_call(
        body,
        grid=(rows // tr,),
        in_specs=[pl.BlockSpec((N_DEV, tr, ROW), lambda i: (0, i, 0)), row, row, row],
        out_specs=[row] * 4,
        out_shape=[jax.ShapeDtypeStruct((rows, ROW), F32)] * 4,
        name=name,
    )(land, w, m, v)


def _prep_weights(p):
    w = {}
    row = lambda a: a.reshape(1, -1).astype(F32)
    for l in range(2):
        w["n1", l] = row(p["ffn_norm1"][l])
        w["mix", l] = row(p["mix_norm"][l])
        w["n2", l] = row(p["ffn_norm2"][l])
        w["f1in", l] = _c(p["ffn1_w_in"][l])
        w["f1out", l] = _c(p["ffn1_w_out"][l])
        w["f2in", l] = _c(p["ffn2_w_in"][l])
        w["f2out", l] = _c(p["ffn2_w_out"][l])
    wi = p["ab_w_in"][0]
    main = [wi[:, 0:2048], wi[:, 2056:3080]]
    w["wab_f"] = _c(jnp.concatenate(
        main + [jnp.repeat(wi[:, 2048:2052], HD, axis=1), jnp.repeat(wi[:, 2052:2056], HD, axis=1)], axis=1))
    w["wab_b"] = _c(jnp.concatenate(main + [wi[:, 2048:2056], jnp.zeros((D, 120), wi.dtype)], axis=1))
    w["cw"] = p["dn_conv_w"][0].astype(F32)
    w["alog"] = jnp.repeat(p["dn_a_log"][0].astype(F32), HD).reshape(1, HW)
    w["dtb"] = jnp.repeat(p["dn_dt_bias"][0].astype(F32), HD).reshape(1, HW)
    w["dnw"] = jnp.tile(p["dn_out_norm"][0].astype(F32), HEADS).reshape(1, HW)
    w["sgnw"] = p["sg_norm"][0].astype(F32).reshape(1, HW)
    tri = jnp.tril(jnp.ones((SGC, SGC), dtype=bool))
    wt = jnp.where(tri, p["sg_w"][0].astype(F32), 0.0)
    w["wtril"] = _c(wt)
    w["wtril_t"] = _c(jnp.transpose(wt, (0, 2, 1)))
    w["sgb"] = jnp.repeat(jnp.transpose(p["sg_b"][0].astype(F32)), HD, axis=1)
    w["wout_ab"] = _c(p["ab_w_out"][0])
    w["pw"] = _c(p["pool_w"][0])
    w["ps"] = row(p["pool_scale"][0])
    w["fn"] = row(p["final_norm"])
    return w


def _local_step(x, tgt, w):
    def ffn_b(xin, nw, w_in, w_out, dy):
        dx, xn, act, dh, dnw = _ffn_bwd(xin, nw, w_in, w_out, dy)
        d_in = _mm_tn_win(xn, dh)
        d_out = _mm_tn(act, dy, FC, D, 512, 0.5, "mm_tn_wout")
        return dx, dnw, d_in, d_out

    x00 = x
    x01 = _ffn_fwd(x00, w["n1", 0], w["f1in", 0], w["f1out", 0])
    h, qkv, z, su, sv, b_rep, a_rep = _ab_proj(x01, w["mix", 0], w["wab_f"])
    qn, kn, v, beta, g = _dn_pre(qkv, b_rep, a_rep, w["cw"], w["alog"], w["dtb"])
    o, sall = _dn_fwd(qn, kn, v, beta, g)
    x02, cat = _ab_out(x01, o, z, su, sv, w["dnw"], w["sgnw"], w["wtril"], w["sgb"], w["wout_ab"])
    x10 = _ffn_fwd(x02, w["n2", 0], w["f2in", 0], w["f2out", 0])
    x11 = _ffn_fwd(x10, w["n1", 1], w["f1in", 1], w["f1out", 1])
    x12 = _pool_fwd(x11, w["mix", 1], w["pw"], w["ps"])
    x13 = _ffn_fwd(x12, w["n2", 1], w["f2in", 1], w["f2out", 1])
    loss, dx, d_fn = _loss_head(x13, w["fn"], tgt)

    dx, d_n2_1, d_f2in_1, d_f2out_1 = ffn_b(x12, w["n2", 1], w["f2in", 1], w["f2out", 1], dx)
    dx, d_mix_1, d_pw, d_ps = _pool_bwd(x11, w["mix", 1], w["pw"], w["ps"], dx)
    dx, d_n1_1, d_f1in_1, d_f1out_1 = ffn_b(x10, w["n1", 1], w["f1in", 1], w["f1out", 1], dx)
    dx, d_n2_0, d_f2in_0, d_f2out_0 = ffn_b(x02, w["n2", 0], w["f2in", 0], w["f2out", 0], dx)
    do, dz, dsu, dsv, d_dnw, d_sgnw, d_sgw, d_sgb = _ab_out_bwd(
        dx, o, z, su, sv, w["dnw"], w["sgnw"], w["wtril"], w["wtril_t"], w["sgb"], w["wout_ab"])
    d_wout_ab = _mm_tn(cat, dx, D, D, 512, 1.0, "mm_tn_about")
    dqn, dkn, dv, dbeta, dg = _dn_bwd(qn, kn, v, beta, g, sall, do)
    dc, dba, d_cw, d_alog, d_dtb = _dn_pre_bwd(qkv, b_rep, a_rep, w["cw"], w["alog"], w["dtb"], dqn, dkn, dv, dbeta, dg)
    dqkv = _conv_bwd(dc, w["cw"])
    dx, dcat, d_mix_0 = _ab_proj_bwd(x01, w["mix", 0], dqkv, dz, dsu, dsv, dba, w["wab_b"], dx)
    d_wab = _mm_tn(h, dcat, D, 640, 512, 1.0, "mm_tn_abin")
    dx, d_n1_0, d_f1in_0, d_f1out_0 = ffn_b(x00, w["n1", 0], w["f1in", 0], w["f1out", 0], dx)

    grads = {
        "ffn_norm1": jnp.concatenate([d_n1_0, d_n1_1], axis=0),
        "mix_norm": jnp.concatenate([d_mix_0, d_mix_1], axis=0),
        "ffn_norm2": jnp.concatenate([d_n2_0, d_n2_1], axis=0),
        "ffn1_w_in": jnp.stack([d_f1in_0, d_f1in_1]),
        "ffn1_w_out": jnp.stack([d_f1out_0, d_f1out_1]),
        "ffn2_w_in": jnp.stack([d_f2in_0, d_f2in_1]),
        "ffn2_w_out": jnp.stack([d_f2out_0, d_f2out_1]),
        "ab_w_in": jnp.concatenate([d_wab[:, 0:2048], d_wab[:, 3072:3080], d_wab[:, 2048:3072]], axis=1)[None],
        "dn_conv_w": d_cw.reshape(1, 4, QKV),
        "dn_a_log": d_alog[:, ::HD],
        "dn_dt_bias": d_dtb[:, ::HD],
        "dn_out_norm": d_dnw,
        "sg_norm": d_sgnw.reshape(1, HEADS, HD),
        "sg_w": d_sgw[None],
        "sg_b": jnp.transpose(d_sgb[:, 0:HEADS])[None],
        "ab_w_out": d_wout_ab[None],
        "pool_w": d_pw[None],
        "pool_scale": d_ps,
        "final_norm": d_fn.reshape(D),
    }
    return loss, dx, grads


def kernel(x, ffn_norm1, ffn1_w_in, ffn1_w_out, mix_norm, ffn_norm2, ffn2_w_in, ffn2_w_out, ab_w_in, dn_conv_w, dn_a_log, dn_dt_bias, dn_out_norm, sg_norm, sg_w, sg_b, ab_w_out, pool_w, pool_scale, final_norm, loss_target, m_ffn_norm1, m_ffn1_w_in, m_ffn1_w_out, m_mix_norm, m_ffn_norm2, m_ffn2_w_in, m_ffn2_w_out, m_ab_w_in, m_dn_conv_w, m_dn_a_log, m_dn_dt_bias, m_dn_out_norm, m_sg_norm, m_sg_w, m_sg_b, m_ab_w_out, m_pool_w, m_pool_scale, m_final_norm, v_ffn_norm1, v_ffn1_w_in, v_ffn1_w_out, v_mix_norm, v_ffn_norm2, v_ffn2_w_in, v_ffn2_w_out, v_ab_w_in, v_dn_conv_w, v_dn_a_log, v_dn_dt_bias, v_dn_out_norm, v_sg_norm, v_sg_w, v_sg_b, v_ab_w_out, v_pool_w, v_pool_scale, v_final_norm):
    wl = dict(ffn_norm1=ffn_norm1, ffn1_w_in=ffn1_w_in, ffn1_w_out=ffn1_w_out, mix_norm=mix_norm, ffn_norm2=ffn_norm2,
              ffn2_w_in=ffn2_w_in, ffn2_w_out=ffn2_w_out, ab_w_in=ab_w_in, dn_conv_w=dn_conv_w, dn_a_log=dn_a_log,
              dn_dt_bias=dn_dt_bias, dn_out_norm=dn_out_norm, sg_norm=sg_norm, sg_w=sg_w, sg_b=sg_b, ab_w_out=ab_w_out,
              pool_w=pool_w, pool_scale=pool_scale, final_norm=final_norm)
    ml = dict(ffn_norm1=m_ffn_norm1, ffn1_w_in=m_ffn1_w_in, ffn1_w_out=m_ffn1_w_out, mix_norm=m_mix_norm,
              ffn_norm2=m_ffn_norm2, ffn2_w_in=m_ffn2_w_in, ffn2_w_out=m_ffn2_w_out, ab_w_in=m_ab_w_in,
              dn_conv_w=m_dn_conv_w, dn_a_log=m_dn_a_log, dn_dt_bias=m_dn_dt_bias, dn_out_norm=m_dn_out_norm,
              sg_norm=m_sg_norm, sg_w=m_sg_w, sg_b=m_sg_b, ab_w_out=m_ab_w_out, pool_w=m_pool_w,
              pool_scale=m_pool_scale, final_norm=m_final_norm)
    vl = dict(ffn_norm1=v_ffn_norm1, ffn1_w_in=v_ffn1_w_in, ffn1_w_out=v_ffn1_w_out, mix_norm=v_mix_norm,
              ffn_norm2=v_ffn_norm2, ffn2_w_in=v_ffn2_w_in, ffn2_w_out=v_ffn2_w_out, ab_w_in=v_ab_w_in,
              dn_conv_w=v_dn_conv_w, dn_a_log=v_dn_a_log, dn_dt_bias=v_dn_dt_bias, dn_out_norm=v_dn_out_norm,
              sg_norm=v_sg_norm, sg_w=v_sg_w, sg_b=v_sg_b, ab_w_out=v_ab_w_out, pool_w=v_pool_w,
              pool_scale=v_pool_scale, final_norm=v_final_norm)

    gathered = _all_gather(_pack_ag(wl)).reshape(N_DEV, R_AG, ROW)
    full = _unpack_ag(gathered)
    for n, _ in SMALL:
        full[n] = wl[n]
    w = _prep_weights(full)

    loss_local, grad_x, grads = _local_step(x[0], loss_target[0], w)

    gbig = _pack_rows(jnp.concatenate([_c(_to_shards(grads[n], axis)) for n, _, axis in BIG], axis=1), R_BIG)
    gsmall = _pack_small(grads)
    land_b, land_s = _exchange_grads(gbig, gsmall)
    big = _adamw(land_b, _pack_big(wl), _pack_big(ml), _pack_big(vl), ADAM_ROWS, "adamw_sharded")
    small = _adamw(land_s, _pack_small(wl), _pack_small(ml), _pack_small(vl), R_SMALL, "adamw_replicated")
    outs = []
    for bp, sp in zip(big, small):
        d = _unpack_big(bp)
        d.update(_unpack_small(sp))
        outs.append(d)

    loss = lax.psum(loss_local[0, 0], ("x", "y", "c"))
    result = [loss, grad_x[None]]
    for d in outs:
        result += [d[n] for n in WEIGHT_ORDER]
    return tuple(result)


R_SMALL = 88
SMALL_ROWS = (
    ("ffn_norm1", (2, D), 2), ("mix_norm", (2, D), 2), ("ffn_norm2", (2, D), 2), ("final_norm", (D,), 1),
    ("sg_w", (1, 4, SGC, SGC), 64), ("sg_norm", (1, 4, HD), 1), ("sg_b", (1, 4, SGC), 1), ("dn_out_norm", (1, HD), 1),
    ("dn_a_log", (1, 4), 1), ("dn_dt_bias", (1, 4), 1), ("pool_scale", (1, D), 1), ("dn_conv_w", (1, 4, QKV), 8),
)


def _rows_of(a, rows):
    if a.shape[-1] == QKV:
        return jnp.pad(a.reshape(4, QKV), ((0, 0), (0, 2 * ROW - QKV))).reshape(8, ROW)
    n = _numel(a.shape)
    if n % ROW == 0:
        return a.reshape(n // ROW, ROW)
    return jnp.pad(a.reshape(1, n), ((0, 0), (0, ROW - n)))


def _from_rows(r, shape):
    if shape[-1] == QKV:
        return r.reshape(4, 2 * ROW)[:, 0:QKV].reshape(shape)
    n = _numel(shape)
    if n % ROW == 0:
        return r.reshape(shape)
    return r[:, 0:n].reshape(shape)


def _pack_small(vals):
    parts = [(_rows_of(vals[n].astype(F32), r) if n in vals else jnp.zeros((r, ROW), F32)) for n, _, r in SMALL_ROWS]
    used = sum(r for _, _, r in SMALL_ROWS)
    return jnp.concatenate(parts + [jnp.zeros((R_SMALL - used, ROW), F32)], axis=0)


def _unpack_small(packed):
    out, o = {}, 0
    for n, shape, r in SMALL_ROWS:
        out[n] = _from_rows(packed[o:o + r], shape)
        o += r
    return out


def _all_gather(arrs):
    na = len(arrs)

    def body(*refs):
        ins, outs = refs[0:na], refs[na:2 * na]
        send_sems, recv_sems, local_sems = refs[2 * na:]
        x, y, c = lax.axis_index("x"), lax.axis_index("y"), lax.axis_index("c")
        me, sibling = (x, y, c), (x, y, 1 - c)
        chips = [(1 - x, y), (x, 1 - y), (1 - x, 1 - y)]

        def slot(a, px, py, pc):
            return outs[a].at[4 * px + 2 * py + pc]

        def copy(a, k, block, to, src=None):
            return pltpu.make_async_remote_copy(
                src_ref=slot(a, *block) if src is None else src, dst_ref=slot(a, *block),
                send_sem=send_sems.at[7 * a + k], recv_sem=recv_sems.at[7 * a + k], device_id=to, device_id_type=MESH_T)

        mine = [pltpu.make_async_copy(ins[a], slot(a, *me), local_sems.at[a]) for a in range(na)]
        for cp in mine:
            cp.start()
        first = []
        for a in range(na):
            first.append(copy(a, 0, me, sibling, src=ins[a]))
            first += [copy(a, 1 + j, me, (*chip, c), src=ins[a]) for j, chip in enumerate(chips)]
        for cp in first:
            cp.start()
        passed = []
        for j, chip in enumerate(chips):
            for a in range(na):
                copy(a, 1 + j, (*chip, c), me).wait_recv()
                fwd = copy(a, 4 + j, (*chip, c), sibling)
                fwd.start()
                passed.append(fwd)
        for a in range(na):
            copy(a, 0, sibling, me).wait_recv()
            for j, chip in enumerate(chips):
                copy(a, 4 + j, (*chip, 1 - c), me).wait_recv()
        for cp in first + passed:
            cp.wait_send()
        for cp in mine:
            cp.wait()

    hbm = pl.BlockSpec(memory_space=pltpu.HBM)
    return pl.pallas_call(
        body,
        out_shape=[jax.ShapeDtypeStruct((N_DEV,) + a.shape, a.dtype) for a in arrs],
        in_specs=[hbm] * na,
        out_specs=[hbm] * na,
        scratch_shapes=[pltpu.SemaphoreType.DMA((7 * na,)), pltpu.SemaphoreType.DMA((7 * na,)),
                        pltpu.SemaphoreType.DMA((na,))],
        name="ag_weights",
    )(*arrs)


def _exchange_grads(sharded, repl):
    ns = len(sharded)
    na = ns + 1

    def body(*refs):
        ins, outs = refs[0:na], refs[na:2 * na]
        send_sems, recv_sems, local_sems = refs[2 * na:]
        x, y, c = lax.axis_index("x"), lax.axis_index("y"), lax.axis_index("c")
        me = 4 * x + 2 * y + c
        own = [pltpu.make_async_copy(ins[a].at[me] if a < ns else ins[a], outs[a].at[me], local_sems.at[a])
               for a in range(na)]
        for cp in own:
            cp.start()
        copies = []
        for k in range(1, N_DEV):
            px, py, pc = _peer(k, x, y, c)
            peer = 4 * px + 2 * py + pc
            for a in range(na):
                copies.append(pltpu.make_async_remote_copy(
                    src_ref=ins[a].at[peer] if a < ns else ins[a], dst_ref=outs[a].at[me],
                    send_sem=send_sems.at[na * (k - 1) + a], recv_sem=recv_sems.at[na * (k - 1) + a],
                    device_id=(px, py, pc), device_id_type=MESH_T))
        for cp in copies:
            cp.start()
        for cp in copies:
            cp.wait()
        for cp in own:
            cp.wait()

    hbm = pl.BlockSpec(memory_space=pltpu.HBM)
    return pl.pallas_call(
        body,
        out_shape=[jax.ShapeDtypeStruct(a.shape, a.dtype) for a in sharded]
        + [jax.ShapeDtypeStruct((N_DEV,) + repl.shape, repl.dtype)],
        in_specs=[hbm] * na,
        out_specs=[hbm] * na,
        scratch_shapes=[pltpu.SemaphoreType.DMA((7 * na,)), pltpu.SemaphoreType.DMA((7 * na,)),
                        pltpu.SemaphoreType.DMA((na,))],
        name="exchange_grads",
    )(*sharded, repl)


def _adamw(land, w, m, v, rb, name, f=None):
    nl, nr, nc = w.shape
    ns = land.shape[0]

    def body(l_ref, w_ref, m_ref, v_ref, g_ref, d_ref, m2_ref, v2_ref):
        g = l_ref[0].astype(F32)
        for s in range(1, ns):
            g = g + l_ref[s].astype(F32)
        m2 = ADAM_B1 * m_ref[...] + (1.0 - ADAM_B1) * g
        v2 = ADAM_B2 * v_ref[...] + (1.0 - ADAM_B2) * (g * g)
        m_hat = m2 / (1.0 - ADAM_B1 ** ADAM_STEP)
        v_hat = v2 / (1.0 - ADAM_B2 ** ADAM_STEP)
        g_ref[...] = g
        d_ref[...] = -ADAM_LR * (m_hat / (jnp.sqrt(v_hat) + ADAM_EPS) + ADAM_WD * w_ref[...])
        m2_ref[...] = m2
        v2_ref[...] = v2

    if f is None:
        lspec = pl.BlockSpec((ns, 1, rb, nc), lambda l, r: (0, l, r, 0))
    else:
        lspec = pl.BlockSpec((ns, None, 1, rb, nc), lambda l, r: (0, f, l, r, 0))
    wspec = pl.BlockSpec((1, rb, nc), lambda l, r: (l, r, 0))
    return pl.pallas_call(
        body,
        grid=(nl, nr // rb),
        in_specs=[lspec, wspec, wspec, wspec],
        out_specs=[wspec] * 4,
        out_shape=[jax.ShapeDtypeStruct((nl, nr, nc), F32)] * 4,
        name=name,
    )(land, w, m, v)


def _prep_weights(g, small):
    w = {}
    row = lambda a: a.reshape(1, -1).astype(F32)
    for l in range(2):
        w["n1", l] = row(small["ffn_norm1"][l])
        w["mix", l] = row(small["mix_norm"][l])
        w["n2", l] = row(small["ffn_norm2"][l])
    w["w_in"] = g["w_in"]
    w["w_out"] = g["w_out"]
    wi = jnp.transpose(g["ab_w_in"], (1, 0, 2)).reshape(D, 3080)
    main = [wi[:, 0:2048], wi[:, 2056:3080]]
    w["wab_f"] = jnp.concatenate(
        main + [jnp.repeat(wi[:, 2048:2052], HD, axis=1), jnp.repeat(wi[:, 2052:2056], HD, axis=1)], axis=1)
    w["wab_b"] = jnp.concatenate(main + [wi[:, 2048:2056], jnp.zeros((D, 120), wi.dtype)], axis=1)
    sm = g["small"]
    w["cw"] = jnp.transpose(sm[:, 1:5, 0:QKV // N_DEV], (1, 0, 2)).reshape(4, QKV)
    w["ps"] = sm[:, 0, 0:D // N_DEV].reshape(1, D)
    w["alog"] = jnp.repeat(small["dn_a_log"][0].astype(F32), HD).reshape(1, HW)
    w["dtb"] = jnp.repeat(small["dn_dt_bias"][0].astype(F32), HD).reshape(1, HW)
    w["dnw"] = jnp.tile(small["dn_out_norm"][0].astype(F32), HEADS).reshape(1, HW)
    w["sgnw"] = small["sg_norm"][0].astype(F32).reshape(1, HW)
    tri = jnp.tril(jnp.ones((SGC, SGC), dtype=bool))
    wt = jnp.where(tri, small["sg_w"][0].astype(F32), 0.0)
    w["wtril"] = _c(wt)
    w["wtril_t"] = _c(jnp.transpose(wt, (0, 2, 1)))
    w["sgb"] = jnp.repeat(jnp.transpose(small["sg_b"][0].astype(F32)), HD, axis=1)
    w["wout_ab"] = g["ab_w_out"].reshape(D, D)
    w["pw"] = jnp.transpose(g["pool_w"], (1, 0, 2, 3)).reshape(4, PG, PG)
    w["fn"] = row(small["final_norm"])
    return w


def _local_step(x, tgt, w):
    wi, wo = w["w_in"], w["w_out"]

    def ffn_f(xin, nw, f, l):
        return _ffn_fwd(xin, nw, wi, wo, f, l)

    def ffn_b(xin, nw, f, l, dy):
        dx, xn, act, dh, dnw = _ffn_bwd(xin, nw, wi, wo, f, l, dy)
        return dx, dnw, _mm_tn_win(xn, dh), _mm_tn_wout(act, dy)

    x00 = x
    x01 = ffn_f(x00, w["n1", 0], 0, 0)
    h, qkv, z, su, sv, b_rep, a_rep = _ab_proj(x01, w["mix", 0], w["wab_f"])
    qn, kn, v, beta, g = _dn_pre(qkv, b_rep, a_rep, w["cw"], w["alog"], w["dtb"])
    o, sall = _dn_fwd(qn, kn, v, beta, g)
    x02, cat = _ab_out(x01, o, z, su, sv, w["dnw"], w["sgnw"], w["wtril"], w["sgb"], w["wout_ab"])
    x10 = ffn_f(x02, w["n2", 0], 1, 0)
    x11 = ffn_f(x10, w["n1", 1], 0, 1)
    x12 = _pool_fwd(x11, w["mix", 1], w["pw"], w["ps"])
    x13 = ffn_f(x12, w["n2", 1], 1, 1)
    loss, dx, d_fn = _loss_head(x13, w["fn"], tgt)

    dx, d_n2_1, d_f2in_1, d_f2out_1 = ffn_b(x12, w["n2", 1], 1, 1, dx)
    dx, d_mix_1, d_pw, d_ps = _pool_bwd(x11, w["mix", 1], w["pw"], w["ps"], dx)
    dx, d_n1_1, d_f1in_1, d_f1out_1 = ffn_b(x10, w["n1", 1], 0, 1, dx)
    dx, d_n2_0, d_f2in_0, d_f2out_0 = ffn_b(x02, w["n2", 0], 1, 0, dx)
    do, dz, dsu, dsv, d_dnw, d_sgnw, d_sgw, d_sgb = _ab_out_bwd(
        dx, o, z, su, sv, w["dnw"], w["sgnw"], w["wtril"], w["wtril_t"], w["sgb"], w["wout_ab"])
    d_wout_ab = _mm_tn(cat, dx, D, D, 512, _MM, "mm_tn_about")
    dqn, dkn, dv, dbeta, dg = _dn_bwd(qn, kn, v, beta, g, sall, do)
    dc, dba, d_cw, d_alog, d_dtb = _dn_pre_bwd(qkv, b_rep, a_rep, w["cw"], w["alog"], w["dtb"], dqn, dkn, dv, dbeta, dg)
    dqkv = _conv_bwd(dc, w["cw"])
    dx, dcat, d_mix_0 = _ab_proj_bwd(x01, w["mix", 0], dqkv, dz, dsu, dsv, dba, w["wab_b"], dx)
    d_wab = _mm_tn(h, dcat, D, 640, 512, _MM, "mm_tn_abin")
    dx, d_n1_0, d_f1in_0, d_f1out_0 = ffn_b(x00, w["n1", 0], 0, 0, dx)

    stack4 = lambda a00, a01, a10, a11: jnp.stack([jnp.stack([a00, a01], axis=1), jnp.stack([a10, a11], axis=1)], axis=1)
    d_abin = jnp.concatenate([d_wab[:, 0:2048], d_wab[:, 3072:3080], d_wab[:, 2048:3072]], axis=1)
    sharded = {
        "w_in": stack4(d_f1in_0, d_f1in_1, d_f2in_0, d_f2in_1),
        "w_out": stack4(d_f1out_0, d_f1out_1, d_f2out_0, d_f2out_1),
        "ab_w_in": jnp.transpose(d_abin.reshape(D, N_DEV, 385), (1, 0, 2)),
        "ab_w_out": d_wout_ab.reshape(N_DEV, D // N_DEV, D),
        "pool_w": _c(jnp.transpose(d_pw.reshape(4, N_DEV, PG // N_DEV, PG), (1, 0, 2, 3))),
    }
    small = {
        "ffn_norm1": jnp.concatenate([d_n1_0, d_n1_1], axis=0),
        "mix_norm": jnp.concatenate([d_mix_0, d_mix_1], axis=0),
        "ffn_norm2": jnp.concatenate([d_n2_0, d_n2_1], axis=0),
        "dn_conv_w": d_cw.reshape(1, 4, QKV),
        "dn_a_log": d_alog[:, ::HD],
        "dn_dt_bias": d_dtb[:, ::HD],
        "dn_out_norm": d_dnw,
        "sg_norm": d_sgnw.reshape(1, HEADS, HD),
        "sg_w": d_sgw[None],
        "sg_b": jnp.transpose(d_sgb[:, 0:HEADS])[None],
        "pool_scale": d_ps,
        "final_norm": d_fn.reshape(D),
    }
    return loss, dx, sharded, small


def kernel(x, ffn_norm1, ffn1_w_in, ffn1_w_out, mix_norm, ffn_norm2, ffn2_w_in, ffn2_w_out, ab_w_in, dn_conv_w, dn_a_log, dn_dt_bias, dn_out_norm, sg_norm, sg_w, sg_b, ab_w_out, pool_w, pool_scale, final_norm, loss_target, m_ffn_norm1, m_ffn1_w_in, m_ffn1_w_out, m_mix_norm, m_ffn_norm2, m_ffn2_w_in, m_ffn2_w_out, m_ab_w_in, m_dn_conv_w, m_dn_a_log, m_dn_dt_bias, m_dn_out_norm, m_sg_norm, m_sg_w, m_sg_b, m_ab_w_out, m_pool_w, m_pool_scale, m_final_norm, v_ffn_norm1, v_ffn1_w_in, v_ffn1_w_out, v_mix_norm, v_ffn_norm2, v_ffn2_w_in, v_ffn2_w_out, v_ab_w_in, v_dn_conv_w, v_dn_a_log, v_dn_dt_bias, v_dn_out_norm, v_sg_norm, v_sg_w, v_sg_b, v_ab_w_out, v_pool_w, v_pool_scale, v_final_norm):
    wl = dict(ffn_norm1=ffn_norm1, ffn1_w_in=ffn1_w_in, ffn1_w_out=ffn1_w_out, mix_norm=mix_norm, ffn_norm2=ffn_norm2,
              ffn2_w_in=ffn2_w_in, ffn2_w_out=ffn2_w_out, ab_w_in=ab_w_in, dn_conv_w=dn_conv_w, dn_a_log=dn_a_log,
              dn_dt_bias=dn_dt_bias, dn_out_norm=dn_out_norm, sg_norm=sg_norm, sg_w=sg_w, sg_b=sg_b, ab_w_out=ab_w_out,
              pool_w=pool_w, pool_scale=pool_scale, final_norm=final_norm)
    ml = dict(ffn_norm1=m_ffn_norm1, ffn1_w_in=m_ffn1_w_in, ffn1_w_out=m_ffn1_w_out, mix_norm=m_mix_norm,
              ffn_norm2=m_ffn_norm2, ffn2_w_in=m_ffn2_w_in, ffn2_w_out=m_ffn2_w_out, ab_w_in=m_ab_w_in,
              dn_conv_w=m_dn_conv_w, dn_a_log=m_dn_a_log, dn_dt_bias=m_dn_dt_bias, dn_out_norm=m_dn_out_norm,
              sg_norm=m_sg_norm, sg_w=m_sg_w, sg_b=m_sg_b, ab_w_out=m_ab_w_out, pool_w=m_pool_w,
              pool_scale=m_pool_scale, final_norm=m_final_norm)
    vl = dict(ffn_norm1=v_ffn_norm1, ffn1_w_in=v_ffn1_w_in, ffn1_w_out=v_ffn1_w_out, mix_norm=v_mix_norm,
              ffn_norm2=v_ffn_norm2, ffn2_w_in=v_ffn2_w_in, ffn2_w_out=v_ffn2_w_out, ab_w_in=v_ab_w_in,
              dn_conv_w=v_dn_conv_w, dn_a_log=v_dn_a_log, dn_dt_bias=v_dn_dt_bias, dn_out_norm=v_dn_out_norm,
              sg_norm=v_sg_norm, sg_w=v_sg_w, sg_b=v_sg_b, ab_w_out=v_ab_w_out, pool_w=v_pool_w,
              pool_scale=v_pool_scale, final_norm=v_final_norm)
    small_names = [n for n, _, _ in SMALL_ROWS if n not in ("pool_scale", "dn_conv_w")]

    sm_shard = jnp.concatenate([
        jnp.pad(pool_scale, ((0, 0), (0, ROW - D // N_DEV))),
        jnp.pad(dn_conv_w[0], ((0, 0), (0, ROW - QKV // N_DEV))),
        jnp.zeros((3, ROW), F32)], axis=0)
    names = ("w_in", "w_out", "ab_w_in", "ab_w_out", "pool_w", "small")
    gathered = _all_gather([
        _c(jnp.stack([ffn1_w_in, ffn2_w_in])), _c(jnp.stack([ffn1_w_out, ffn2_w_out])), _c(ab_w_in[0]),
        _c(ab_w_out[0]), _c(pool_w[0]), sm_shard])
    w = _prep_weights(dict(zip(names, gathered)), wl)

    loss_local, grad_x, g_sh, g_sm = _local_step(x[0], loss_target[0], w)

    keys = ("w_in", "w_out", "ab_w_in", "ab_w_out", "pool_w")
    landed = _exchange_grads([g_sh[k] for k in keys], _pack_small(g_sm))
    land = dict(zip(keys, landed[:-1]))
    res = {}
    res["ffn1_w_in"] = _adamw(land["w_in"], ffn1_w_in, m_ffn1_w_in, v_ffn1_w_in, 256, "adamw_w_in", f=0)
    res["ffn2_w_in"] = _adamw(land["w_in"], ffn2_w_in, m_ffn2_w_in, v_ffn2_w_in, 256, "adamw_w_in", f=1)
    res["ffn1_w_out"] = _adamw(land["w_out"], ffn1_w_out, m_ffn1_w_out, v_ffn1_w_out, 176, "adamw_w_out", f=0)
    res["ffn2_w_out"] = _adamw(land["w_out"], ffn2_w_out, m_ffn2_w_out, v_ffn2_w_out, 176, "adamw_w_out", f=1)
    res["ab_w_in"] = _adamw(land["ab_w_in"][:, None], ab_w_in, m_ab_w_in, v_ab_w_in, 256, "adamw_ab_w_in")
    res["ab_w_out"] = _adamw(land["ab_w_out"][:, None], ab_w_out, m_ab_w_out, v_ab_w_out, 128, "adamw_ab_w_out")
    res["pool_w"] = [a[None] for a in _adamw(land["pool_w"], pool_w[0], m_pool_w[0], v_pool_w[0], 32, "adamw_pool_w")]
    sm = _adamw(landed[-1][:, None], _pack_small({n: wl[n] for n in small_names})[None],
                _pack_small({n: ml[n] for n in small_names})[None], _pack_small({n: vl[n] for n in small_names})[None],
                R_SMALL, "adamw_replicated")
    sm = [_unpack_small(a[0]) for a in sm]
    for n in small_names:
        res[n] = [d[n] for d in sm]
    me = 4 * lax.axis_index("x") + 2 * lax.axis_index("y") + lax.axis_index("c")
    g_ps = lax.dynamic_slice(sm[0]["pool_scale"], (0, me * (D // N_DEV)), (1, D // N_DEV))
    g_cw = lax.dynamic_slice(sm[0]["dn_conv_w"], (0, 0, me * (QKV // N_DEV)), (1, 4, QKV // N_DEV))
    pack2 = lambda ps, cw: jnp.concatenate([
        jnp.pad(ps, ((0, 0), (0, ROW - D // N_DEV))), jnp.pad(cw[0], ((0, 0), (0, ROW - QKV // N_DEV))),
        jnp.zeros((3, ROW), F32)], axis=0)[None]
    s2 = _adamw(pack2(g_ps, g_cw)[None], pack2(pool_scale, dn_conv_w), pack2(m_pool_scale, m_dn_conv_w),
                pack2(v_pool_scale, v_dn_conv_w), 8, "adamw_small_sharded")
    res["pool_scale"] = [a[0, 0:1, 0:D // N_DEV] for a in s2]
    res["dn_conv_w"] = [a[0, 1:5, 0:QKV // N_DEV][None] for a in s2]

    loss = lax.psum(loss_local[0, 0], ("x", "y", "c"))
    result = [loss, grad_x[None]]
    for i in range(4):
        result += [res[n][i] for n in WEIGHT_ORDER]
    return tuple(result)
```

```python
import jax
import jax.numpy as jnp
from jax import lax
from jax.experimental import pallas as pl
from jax.experimental.pallas import tpu as pltpu

F32 = jnp.float32
_MM = jnp.bfloat16

D = 1024
FF = 2816
EPS = 1e-6
HEADS = 4
HD = 128
DNC = 64
DN_STEP = 2
SGC = 128
QKV = 3 * HEADS * HD
HW = HEADS * HD
POOL_WINDOWS = (2, 4, 8, 16)
PG = D // 4
HALO = 16
N_DEV = 8
AB_IN = 3080
ROW = 1024

TM = 512
FC = 704
NJ = FF // FC
WO_ROWS = FF // N_DEV

ADAM_LR, ADAM_B1, ADAM_B2, ADAM_EPS, ADAM_WD, ADAM_STEP = 0.001, 0.9, 0.999, 1e-08, 0.01, 10

MESH_T = pl.DeviceIdType.MESH
NN = (((1,), (0,)), ((), ()))
NT = (((1,), (1,)), ((), ()))
TN = (((0,), (0,)), ((), ()))


def _c(a):
    return a.astype(_MM)


def _dg(a, b, dims):
    return lax.dot_general(a, b, dims, preferred_element_type=F32)


def _dot(a, b):
    return _dg(a, b, NN)


def _dot_nt(a, b):
    return _dg(a, b, NT)


def _dot_tn(a, b):
    return _dg(a, b, TN)


def _split2(a):
    hi = _c(a)
    return hi, _c(a - hi.astype(F32))


def _dot3(a, b, dims=NN):
    ah, al = _split2(a)
    bh, bl = _split2(b)
    return _dg(ah, bh, dims) + (_dg(ah, bl, dims) + _dg(al, bh, dims))


def _mask_dot(mask, x):
    x1 = _c(x)
    r = x - x1.astype(F32)
    x2 = _c(r)
    x3 = _c(r - x2.astype(F32))
    return _dot(mask, x1) + (_dot(mask, x2) + _dot(mask, x3))


def _sigmoid(x):
    return jax.nn.sigmoid(x)


def _gelu(x):
    return 0.5 * x * (1.0 + lax.erf(x * 0.7071067811865476))


def _gelu_grad(x):
    return 0.5 * (1.0 + lax.erf(x * 0.7071067811865476)) + x * jnp.exp(-0.5 * x * x) * 0.3989422804014327


def _accum(ref, val, step):
    @pl.when(step == 0)
    def _():
        ref[...] = val

    @pl.when(step > 0)
    def _():
        ref[...] += val


def _rstd(x):
    return lax.rsqrt(jnp.mean(x * x, axis=-1, keepdims=True) + EPS)


def _rms_bwd(dy, xhat, r, nw):
    dnw = jnp.sum(dy * xhat, axis=0, keepdims=True)
    dxh = dy * nw
    dx = r * (dxh - xhat * jnp.mean(dxh * xhat, axis=-1, keepdims=True))
    return dx, dnw


def _numel(shape):
    n = 1
    for s in shape:
        n *= s
    return n


def _peer(k, x, y, c):
    px = 1 - x if k & 4 else x
    py = 1 - y if k & 2 else y
    pc = 1 - c if k & 1 else c
    return px, py, pc


class _Comm:
    def __init__(self, kind, arrs, repl=()):
        self.kind = kind
        self.ns = len(arrs)
        self.arrs = list(arrs) + list(repl)
        self.na = len(self.arrs)

    @property
    def out_shape(self):
        out = []
        for i, a in enumerate(self.arrs):
            lead = (N_DEV,) if (self.kind == "gather" or i >= self.ns) else ()
            out.append(jax.ShapeDtypeStruct(lead + a.shape, a.dtype))
        return out

    @property
    def scratch(self):
        return [pltpu.SemaphoreType.DMA((7 * self.na,)), pltpu.SemaphoreType.DMA((7 * self.na,)),
                pltpu.SemaphoreType.DMA((self.na,))]

    def phases(self, ins, outs, sems):
        send_sems, recv_sems, local_sems = sems
        na = self.na
        x, y, c = lax.axis_index("x"), lax.axis_index("y"), lax.axis_index("c")
        if self.kind == "gather":
            me, sibling = (x, y, c), (x, y, 1 - c)
            chips = [(1 - x, y), (x, 1 - y), (1 - x, 1 - y)]

            def slot(a, px, py, pc):
                return outs[a].at[4 * px + 2 * py + pc]

            def copy(a, k, block, to, src=None):
                return pltpu.make_async_remote_copy(
                    src_ref=slot(a, *block) if src is None else src, dst_ref=slot(a, *block),
                    send_sem=send_sems.at[7 * a + k], recv_sem=recv_sems.at[7 * a + k],
                    device_id=to, device_id_type=MESH_T)

            mine = [pltpu.make_async_copy(ins[a], slot(a, *me), local_sems.at[a]) for a in range(na)]
            first, passed = [], []
            for a in range(na):
                first.append(copy(a, 0, me, sibling, src=ins[a]))
                first += [copy(a, 1 + j, me, (*chip, c), src=ins[a]) for j, chip in enumerate(chips)]
                passed += [copy(a, 4 + j, (*chip, c), sibling) for j, chip in enumerate(chips)]

            def start():
                for cp in mine + first:
                    cp.start()

            def middle():
                for a in range(na):
                    for j, chip in enumerate(chips):
                        copy(a, 1 + j, (*chip, c), me).wait_recv()
                        passed[3 * a + j].start()

            def finish():
                for a in range(na):
                    copy(a, 0, sibling, me).wait_recv()
                    for j, chip in enumerate(chips):
                        copy(a, 4 + j, (*chip, 1 - c), me).wait_recv()
                for cp in first + passed:
                    cp.wait_send()
                for cp in mine:
                    cp.wait()

            return start, middle, finish

        me = 4 * x + 2 * y + c
        ns = self.ns
        own = [pltpu.make_async_copy(ins[a].at[me] if a < ns else ins[a], outs[a].at[me], local_sems.at[a])
               for a in range(na)]
        copies = []
        for k in range(1, N_DEV):
            px, py, pc = _peer(k, x, y, c)
            peer = 4 * px + 2 * py + pc
            for a in range(na):
                copies.append(pltpu.make_async_remote_copy(
                    src_ref=ins[a].at[peer] if a < ns else ins[a], dst_ref=outs[a].at[me],
                    send_sem=send_sems.at[na * (k - 1) + a], recv_sem=recv_sems.at[na * (k - 1) + a],
                    device_id=(px, py, pc), device_id_type=MESH_T))

        def start():
            for cp in own + copies:
                cp.start()

        def middle():
            pass

        def finish():
            for cp in copies:
                cp.wait()
            for cp in own:
                cp.wait()

        return start, middle, finish


def _comm_call(comm, name):
    na = comm.na

    def body(*refs):
        start, middle, finish = comm.phases(refs[0:na], refs[na:2 * na], refs[2 * na:])
        start()
        middle()
        finish()

    hbm = pl.BlockSpec(memory_space=pltpu.HBM)
    return pl.pallas_call(
        body, out_shape=comm.out_shape, in_specs=[hbm] * na, out_specs=[hbm] * na, scratch_shapes=comm.scratch,
        name=name)(*comm.arrs)


def _carried_call(body, comm, n_in, n_out, n_scr, when, *, grid, in_specs, out_specs, out_shape, scratch_shapes,
                  operands, name):
    if comm is None:
        return pl.pallas_call(body, grid=grid, in_specs=in_specs, out_specs=out_specs, out_shape=out_shape,
                              scratch_shapes=scratch_shapes, name=name)(*operands), []
    na = comm.na

    def both(*refs):
        a = n_in + na
        b = a + n_out + na
        body(*refs[0:n_in], *refs[a:a + n_out], *refs[b:b + n_scr])
        start, middle, finish = comm.phases(refs[n_in:a], refs[a + n_out:b], refs[b + n_scr:])
        first, mid, last = when()
        pl.when(first)(start)
        pl.when(mid)(middle)
        pl.when(last)(finish)

    hbm = pl.BlockSpec(memory_space=pltpu.HBM)
    res = pl.pallas_call(
        both, grid=grid, in_specs=list(in_specs) + [hbm] * na, out_specs=list(out_specs) + [hbm] * na,
        out_shape=list(out_shape) + comm.out_shape, scratch_shapes=list(scratch_shapes) + comm.scratch,
        name=name)(*operands, *comm.arrs)
    return res[0:n_out], res[n_out:]


def _ffn_w_specs():
    return [
        pl.BlockSpec((None, D, FC), lambda i, j: (j, 0, 0)),
        pl.BlockSpec((None, D, FC), lambda i, j: (j + NJ, 0, 0)),
        pl.BlockSpec((2, WO_ROWS, D), lambda i, j: (j, 0, 0)),
    ]


def _ffn_when(nt):
    def when():
        i, j = pl.program_id(0), pl.program_id(1)
        return ((i == 0) & (j == 0), (i == nt // 2) & (j == 0), (i == nt - 1) & (j == NJ - 1))
    return when


def _ffn_fwd(x, nw, w_in, w_out, comm=None):
    t = x.shape[0]
    nt = t // TM

    def body(x_ref, nw_ref, wg_ref, wu_ref, wo3_ref, o_ref, xn_sc, acc_sc):
        j = pl.program_id(1)

        @pl.when(j == 0)
        def _():
            xv = x_ref[...]
            xn_sc[...] = _c(xv * _rstd(xv) * nw_ref[...])
            acc_sc[...] = jnp.zeros_like(acc_sc)

        xn = xn_sc[...]
        g = _dot(xn, wg_ref[...])
        u = _dot(xn, wu_ref[...])
        acc_sc[...] += _dot(_c(g * _sigmoid(g) * u), wo3_ref[...].reshape(FC, D))

        @pl.when(j == NJ - 1)
        def _():
            o_ref[...] = x_ref[...] + 0.5 * acc_sc[...]

    (out,), landed = _carried_call(
        body, comm, 5, 1, 2, _ffn_when(nt),
        grid=(nt, NJ),
        in_specs=[pl.BlockSpec((TM, D), lambda i, j: (i, 0)), pl.BlockSpec((1, D), lambda i, j: (0, 0))]
        + _ffn_w_specs(),
        out_specs=[pl.BlockSpec((TM, D), lambda i, j: (i, 0))],
        out_shape=[jax.ShapeDtypeStruct((t, D), F32)],
        scratch_shapes=[pltpu.VMEM((TM, D), _MM), pltpu.VMEM((TM, D), F32)],
        operands=(x, nw, w_in, w_in, w_out), name="ffn_fwd")
    return out, landed


def _ffn_bwd(x, nw, w_in, w_out, dy, comm=None):
    t = x.shape[0]
    nt = t // TM

    def body(x_ref, nw_ref, wg_ref, wu_ref, wo3_ref, dy_ref, dx_ref, xn_ref, a_ref, dh_ref, dnw_ref,
             r_sc, dyb_sc, acc_sc):
        wo = wo3_ref[...].reshape(FC, D)
        i = pl.program_id(0)
        j = pl.program_id(1)

        @pl.when(j == 0)
        def _():
            xv = x_ref[...]
            r = _rstd(xv)
            r_sc[...] = r
            xn_ref[...] = _c(xv * r * nw_ref[...])
            dyb_sc[...] = _c(0.5 * dy_ref[...])
            acc_sc[...] = jnp.zeros_like(acc_sc)

        xn = xn_ref[...]
        g = _dot(xn, wg_ref[...])
        u = _dot(xn, wu_ref[...])
        s = _sigmoid(g)
        sl = g * s
        a_ref[...] = _c(sl * u)
        da = _dot_nt(dyb_sc[...], wo)
        dg = _c(da * u * (s * (1.0 + g * (1.0 - s))))
        du = _c(da * sl)
        dh_ref[0] = dg
        dh_ref[1] = du
        acc_sc[...] += _dot_nt(dg, wg_ref[...]) + _dot_nt(du, wu_ref[...])

        @pl.when(j == NJ - 1)
        def _():
            r = r_sc[...]
            dx, dnw = _rms_bwd(acc_sc[...], x_ref[...] * r, r, nw_ref[...])
            dx_ref[...] = dy_ref[...] + dx
            _accum(dnw_ref, dnw, i)

    return _carried_call(
        body, comm, 6, 5, 3, _ffn_when(nt),
        grid=(nt, NJ),
        in_specs=[pl.BlockSpec((TM, D), lambda i, j: (i, 0)), pl.BlockSpec((1, D), lambda i, j: (0, 0))]
        + _ffn_w_specs() + [pl.BlockSpec((TM, D), lambda i, j: (i, 0))],
        out_specs=[
            pl.BlockSpec((TM, D), lambda i, j: (i, 0)),
            pl.BlockSpec((TM, D), lambda i, j: (i, 0)),
            pl.BlockSpec((None, TM, FC), lambda i, j: (j, i, 0)),
            pl.BlockSpec((None, 2, TM, FC), lambda i, j: (j, 0, i, 0)),
            pl.BlockSpec((1, D), lambda i, j: (0, 0)),
        ],
        out_shape=[
            jax.ShapeDtypeStruct((t, D), F32),
            jax.ShapeDtypeStruct((t, D), _MM),
            jax.ShapeDtypeStruct((NJ, t, FC), _MM),
            jax.ShapeDtypeStruct((NJ, 2, t, FC), _MM),
            jax.ShapeDtypeStruct((1, D), F32),
        ],
        scratch_shapes=[pltpu.VMEM((TM, 1), F32), pltpu.VMEM((TM, D), _MM), pltpu.VMEM((TM, D), F32)],
        operands=(x, nw, w_in, w_in, w_out, dy), name="ffn_bwd")


def _mm_tn(a, b, bm, bn, bt, out_dtype, name):
    t, m = a.shape
    n = b.shape[1]
    nt = t // bt

    def body(a_ref, b_ref, o_ref, acc_sc):
        k = pl.program_id(2)
        _accum(acc_sc, _dot_tn(_c(a_ref[...]), _c(b_ref[...])), k)

        @pl.when(k == nt - 1)
        def _():
            o_ref[...] = acc_sc[...].astype(out_dtype)

    return pl.pallas_call(
        body,
        grid=(m // bm, n // bn, nt),
        in_specs=[pl.BlockSpec((bt, bm), lambda i, j, k: (k, i)), pl.BlockSpec((bt, bn), lambda i, j, k: (k, j))],
        out_specs=pl.BlockSpec((bm, bn), lambda i, j, k: (i, j)),
        out_shape=jax.ShapeDtypeStruct((m, n), out_dtype),
        scratch_shapes=[pltpu.VMEM((bm, bn), F32)],
        name=name,
    )(a, b)


def _mm_tn_win(xn, dh):
    t = xn.shape[0]
    bt = min(512, t)
    nt = t // bt

    def body(a_ref, b_ref, o_ref, acc_sc):
        k = pl.program_id(2)
        _accum(acc_sc, _dot_tn(a_ref[...], b_ref[...]), k)

        @pl.when(k == nt - 1)
        def _():
            o_ref[...] = _c(acc_sc[...])

    return pl.pallas_call(
        body,
        grid=(2, NJ, nt),
        in_specs=[pl.BlockSpec((bt, D), lambda h, j, k: (k, 0)),
                  pl.BlockSpec((None, None, bt, FC), lambda h, j, k: (j, h, k, 0))],
        out_specs=pl.BlockSpec((None, D, FC), lambda h, j, k: (h * NJ + j, 0, 0)),
        out_shape=jax.ShapeDtypeStruct((N_DEV, D, FC), _MM),
        scratch_shapes=[pltpu.VMEM((D, FC), F32)],
        name="mm_tn_win",
    )(xn, dh)


def _mm_tn_wout(act, dy):
    t = dy.shape[0]
    bt = min(512, t)
    nt = t // bt

    def body(a_ref, b_ref, o_ref, acc_sc):
        k = pl.program_id(1)
        _accum(acc_sc, _dot_tn(a_ref[...], _c(b_ref[...])), k)

        @pl.when(k == nt - 1)
        def _():
            o_ref[...] = _c((0.5 * acc_sc[...]).reshape(2, WO_ROWS, D))

    return pl.pallas_call(
        body,
        grid=(NJ, nt),
        in_specs=[pl.BlockSpec((None, bt, FC), lambda j, k: (j, k, 0)), pl.BlockSpec((bt, D), lambda j, k: (k, 0))],
        out_specs=pl.BlockSpec((2, WO_ROWS, D), lambda j, k: (j, 0, 0)),
        out_shape=jax.ShapeDtypeStruct((N_DEV, WO_ROWS, D), _MM),
        scratch_shapes=[pltpu.VMEM((FC, D), F32)],
        name="mm_tn_wout",
    )(act, dy)


def _loss_head(x, nw, tgt):
    t = x.shape[0]

    def body(x_ref, nw_ref, t_ref, loss_ref, dx_ref, dnw_ref):
        i = pl.program_id(0)
        xv = x_ref[...]
        r = _rstd(xv)
        xh = xv * r
        e = xh * nw_ref[...] - t_ref[...]
        part = 0.5 * jnp.sum(jnp.mean(e * e, axis=-1, keepdims=True), axis=0, keepdims=True)
        _accum(loss_ref, jnp.broadcast_to(part, (1, 128)), i)
        dx, dnw = _rms_bwd(e * (1.0 / D), xh, r, nw_ref[...])
        dx_ref[...] = dx
        _accum(dnw_ref, dnw, i)

    return pl.pallas_call(
        body,
        grid=(t // TM,),
        in_specs=[pl.BlockSpec((TM, D), lambda i: (i, 0)), pl.BlockSpec((1, D), lambda i: (0, 0)),
                  pl.BlockSpec((TM, D), lambda i: (i, 0))],
        out_specs=[pl.BlockSpec((1, 128), lambda i: (0, 0)), pl.BlockSpec((TM, D), lambda i: (i, 0)),
                   pl.BlockSpec((1, D), lambda i: (0, 0))],
        out_shape=[jax.ShapeDtypeStruct((1, 128), F32), jax.ShapeDtypeStruct((t, D), F32),
                   jax.ShapeDtypeStruct((1, D), F32)],
        name="loss_head",
    )(x, nw, tgt)


PW_F = QKV + 5 * HW
PW_B = QKV + 3 * HW + 128


def _ab_proj(x1, nw, wab):
    t = x1.shape[0]

    def body(x_ref, nw_ref, w_ref, h_ref, qkv_ref, z_ref, su_ref, sv_ref, b_ref, a_ref):
        xv = x_ref[...]
        h = _c(xv * _rstd(xv) * nw_ref[...])
        h_ref[...] = h
        p = _dot(h, w_ref[...])
        qkv_ref[...] = p[:, 0:QKV]
        o = QKV
        for ref in (z_ref, su_ref, sv_ref, b_ref, a_ref):
            ref[...] = p[:, o:o + HW]
            o += HW

    row = lambda w: pl.BlockSpec((TM, w), lambda i: (i, 0))
    return pl.pallas_call(
        body,
        grid=(t // TM,),
        in_specs=[row(D), pl.BlockSpec((1, D), lambda i: (0, 0)), pl.BlockSpec((D, PW_F), lambda i: (0, 0))],
        out_specs=[row(D), row(QKV)] + [row(HW)] * 5,
        out_shape=[jax.ShapeDtypeStruct((t, D), _MM), jax.ShapeDtypeStruct((t, QKV), F32)]
        + [jax.ShapeDtypeStruct((t, HW), F32)] * 5,
        name="ab_proj",
    )(x1, nw, wab)


def _conv_rows(x, halo, cw):
    xe = jnp.concatenate([halo, x], axis=0)
    shifted = []
    c = None
    for k in range(4):
        s = 3 - k
        xs = (xe if s == 0 else pltpu.roll(xe, s, 0))[8:, :]
        shifted.append(xs)
        term = cw[k:k + 1, :] * xs
        c = term if c is None else c + term
    return c, shifted


def _head_rsq(a):
    parts = []
    for h in range(HEADS):
        ah = a[:, h * HD:(h + 1) * HD]
        r = lax.rsqrt(jnp.sum(ah * ah, axis=-1, keepdims=True) + EPS)
        parts.append(jnp.broadcast_to(r, ah.shape))
    return jnp.concatenate(parts, axis=-1)


def _head_sum(a):
    parts = []
    for h in range(HEADS):
        ah = a[:, h * HD:(h + 1) * HD]
        parts.append(jnp.broadcast_to(jnp.sum(ah, axis=-1, keepdims=True), ah.shape))
    return jnp.concatenate(parts, axis=-1)


def _softplus(x):
    return jnp.maximum(x, 0.0) + jnp.log1p(jnp.exp(-jnp.abs(x)))


def _halo_prev_spec(width, rows):
    per = TM // rows
    return pl.BlockSpec((rows, width), lambda i: (jnp.maximum(i * per - 1, 0), 0))


def _halo_next_spec(width, rows, t):
    per = TM // rows
    last = t // rows - 1
    return pl.BlockSpec((rows, width), lambda i: (jnp.minimum((i + 1) * per, last), 0))


def _dn_pre(qkv, b_rep, a_rep, cw, alog, dtb):
    t = qkv.shape[0]
    qscale = HD ** -0.5

    def body(x_ref, halo_ref, b_ref, a_ref, cw_ref, alog_ref, dt_ref, q_ref, k_ref, v_ref, beta_ref, g_ref):
        i = pl.program_id(0)
        halo = jnp.where(i == 0, 0.0, halo_ref[...])
        c, _ = _conv_rows(x_ref[...], halo, cw_ref[...])
        sc = c * _sigmoid(c)
        q = sc[:, 0:HW]
        k = sc[:, HW:2 * HW]
        q_ref[...] = q * _head_rsq(q) * qscale
        k_ref[...] = k * _head_rsq(k)
        v_ref[...] = sc[:, 2 * HW:]
        beta_ref[...] = _sigmoid(b_ref[...])
        g_ref[...] = -jnp.exp(alog_ref[...]) * _softplus(a_ref[...] + dt_ref[...])

    row = lambda w: pl.BlockSpec((TM, w), lambda i: (i, 0))
    full = lambda a: pl.BlockSpec(a.shape, lambda i: (0,) * a.ndim)
    return pl.pallas_call(
        body,
        grid=(t // TM,),
        in_specs=[row(QKV), _halo_prev_spec(QKV, 8), row(HW), row(HW), full(cw), full(alog), full(dtb)],
        out_specs=[row(HW)] * 5,
        out_shape=[jax.ShapeDtypeStruct((t, HW), F32)] * 5,
        name="dn_pre",
    )(qkv, qkv, b_rep, a_rep, cw, alog, dtb)


def _unit_lower_inv(lo, eye):
    p = eye - lo
    lp = lo
    for _ in range(5):
        lp = _dot3(lp, lp)
        p = p + _dot3(p, lp)
    return p


def _dn_chunk_local(q, k, v, b, gr):
    ri = lax.broadcasted_iota(jnp.int32, (DNC, DNC), 0)
    ci = lax.broadcasted_iota(jnp.int32, (DNC, DNC), 1)
    strict = ri > ci
    causal = ri >= ci
    ltri = _c(causal.astype(F32))
    eye = (ri == ci).astype(F32)
    diff = _mask_dot(ltri, jnp.where(strict, gr[:, 0:DNC], 0.0))
    dm = jnp.where(causal, jnp.exp(diff), 0.0)
    gc = _mask_dot(ltri, gr)
    gl = jnp.sum(gr, axis=0, keepdims=True)
    e = jnp.exp(gc)
    f = jnp.exp(gl - gc)
    kb = k * b
    m = _dot_nt(_c(kb), _c(k))
    lo = jnp.where(strict, m * dm, 0.0)
    a = _unit_lower_inv(lo, eye)
    vb = v * b
    kbe = kb * e
    u = _dot3(a, vb)
    w = _dot3(a, kbe)
    p = _dot_nt(_c(q), _c(k))
    return dict(strict=strict, dm=dm, e=e, f=f, gl=gl, kb=kb, m=m, a=a, vb=vb, kbe=kbe, u=u, w=w,
                p=p, attn=p * dm, qd=q * e, kt=k * f)


def _dn_when(n):
    def when():
        i = pl.program_id(0)
        return (i == 0, i == n // 2, i == n - 1)
    return when


def _dn_fwd(q, k, v, beta, g, comm=None):
    t = q.shape[0]
    rows = DN_STEP * DNC
    n = t // rows

    def body(q_ref, k_ref, v_ref, b_ref, g_ref, o_ref, sall_ref, s_sc):
        i = pl.program_id(0)

        @pl.when(i == 0)
        def _():
            s_sc[...] = jnp.zeros_like(s_sc)

        for h in range(HEADS):
            sl = slice(h * HD, (h + 1) * HD)
            cls = []
            for cc in range(DN_STEP):
                rs = slice(cc * DNC, (cc + 1) * DNC)
                cls.append(_dn_chunk_local(q_ref[rs, sl], k_ref[rs, sl], v_ref[rs, sl], b_ref[rs, sl], g_ref[rs, sl]))
            s = s_sc[h]
            for cc in range(DN_STEP):
                cl = cls[cc]
                sall_ref[cc, h] = s
                sb = _c(s)
                vn = cl["u"] - _dot(_c(cl["w"]), sb)
                o_ref[cc * DNC:(cc + 1) * DNC, sl] = _dot(_c(cl["qd"]), sb) + _dot(_c(cl["attn"]), _c(vn))
                s = s * jnp.exp(cl["gl"]) + _dot_tn(_c(cl["kt"]), _c(vn))
            s_sc[h] = s

    row = pl.BlockSpec((rows, HW), lambda i: (i, 0))
    return _carried_call(
        body, comm, 5, 2, 1, _dn_when(n),
        grid=(n,),
        in_specs=[row] * 5,
        out_specs=[row, pl.BlockSpec((DN_STEP, HEADS, HD, HD), lambda i: (i, 0, 0, 0))],
        out_shape=[jax.ShapeDtypeStruct((t, HW), F32), jax.ShapeDtypeStruct((t // DNC, HEADS, HD, HD), F32)],
        scratch_shapes=[pltpu.VMEM((HEADS, HD, HD), F32)],
        operands=(q, k, v, beta, g), name="dn_fwd")


def _dn_bwd(q, k, v, beta, g, sall, do, comm=None):
    t = q.shape[0]
    rows = DN_STEP * DNC
    n = t // rows

    def body(q_ref, k_ref, v_ref, b_ref, g_ref, sall_ref, do_ref, dq_ref, dk_ref, dv_ref, db_ref, dg_ref, ds_sc):
        i = pl.program_id(0)

        @pl.when(i == 0)
        def _():
            ds_sc[...] = jnp.zeros_like(ds_sc)

        ri = lax.broadcasted_iota(jnp.int32, (DNC, DNC), 0)
        ci = lax.broadcasted_iota(jnp.int32, (DNC, DNC), 1)
        upper = _c((ri <= ci).astype(F32))
        for h in range(HEADS):
            sl = slice(h * HD, (h + 1) * HD)
            cls = []
            for cc in range(DN_STEP):
                rs = slice(cc * DNC, (cc + 1) * DNC)
                cls.append(_dn_chunk_local(q_ref[rs, sl], k_ref[rs, sl], v_ref[rs, sl], b_ref[rs, sl], g_ref[rs, sl]))
            dsn = ds_sc[h]
            for cc in reversed(range(DN_STEP)):
                rs = slice(cc * DNC, (cc + 1) * DNC)
                cl = cls[cc]
                qh, kh, vh, bh = q_ref[rs, sl], k_ref[rs, sl], v_ref[rs, sl], b_ref[rs, sl]
                strict, dm, e, f, a = cl["strict"], cl["dm"], cl["e"], cl["f"], cl["a"]
                s = sall_ref[cc, h]
                sb = _c(s)
                dsb = _c(dsn)
                dob = _c(do_ref[rs, sl])
                egl = jnp.exp(cl["gl"])
                vn = cl["u"] - _dot(_c(cl["w"]), sb)
                vnb = _c(vn)
                dvn = _dot_tn(_c(cl["attn"]), dob) + _dot(_c(cl["kt"]), dsb)
                dvnb = _c(dvn)
                dqd = _dot_nt(dob, sb)
                dattn = _dot_nt(dob, vnb)
                dkt = _dot_nt(vnb, dsb)
                dgl = jnp.sum(jnp.sum(dsn * s, axis=1, keepdims=True), axis=0, keepdims=True) * egl[:, 0:1]
                dw = -_dot_nt(dvnb, sb)
                dsn = dsn * egl + _dot_tn(_c(cl["qd"]), dob) - _dot_tn(_c(cl["w"]), dvnb)
                dp = _c(dattn * dm)
                dq = dqd * e + _dot(dp, _c(kh))
                dk = _dot_tn(dp, _c(qh)) + dkt * f
                dd = dattn * cl["p"]
                de = jnp.sum(dqd * qh, axis=-1, keepdims=True)
                dff = jnp.sum(dkt * kh, axis=-1, keepdims=True) * f[:, 0:1]
                dgl = dgl + jnp.sum(dff, axis=0, keepdims=True)
                dvb = _dot3(a, dvn, TN)
                dkbe = _dot3(a, dw, TN)
                dlo = jnp.where(strict, -(_dot3(dvb, cl["u"], NT) + _dot3(dkbe, cl["w"], NT)), 0.0)
                dmm = _c(dlo * dm)
                dd = dd + dlo * cl["m"]
                dkb = _dot(dmm, _c(kh)) + dkbe * e
                dk = dk + _dot_tn(dmm, _c(cl["kb"])) + dkb * bh
                de = de + jnp.sum(dkbe * cl["kb"], axis=-1, keepdims=True)
                dbeta = jnp.sum(dkb * kh, axis=-1, keepdims=True) + jnp.sum(dvb * vh, axis=-1, keepdims=True)
                gm = dd * dm
                dgc = de * e[:, 0:1] - dff
                dg = (jnp.sum(jnp.where(strict, _mask_dot(upper, gm), 0.0), axis=-1, keepdims=True)
                      + _mask_dot(upper, jnp.broadcast_to(dgc, (DNC, HD)))[:, 0:1] + dgl)
                dq_ref[rs, sl] = dq
                dk_ref[rs, sl] = dk
                dv_ref[rs, sl] = dvb * bh
                db_ref[rs, sl] = jnp.broadcast_to(dbeta, (DNC, HD))
                dg_ref[rs, sl] = jnp.broadcast_to(dg, (DNC, HD))
            ds_sc[h] = dsn

    row = pl.BlockSpec((rows, HW), lambda i: (n - 1 - i, 0))
    return _carried_call(
        body, comm, 7, 5, 1, _dn_when(n),
        grid=(n,),
        in_specs=[row] * 5 + [pl.BlockSpec((DN_STEP, HEADS, HD, HD), lambda i: (n - 1 - i, 0, 0, 0)), row],
        out_specs=[row] * 5,
        out_shape=[jax.ShapeDtypeStruct((t, HW), F32)] * 5,
        scratch_shapes=[pltpu.VMEM((HEADS, HD, HD), F32)],
        operands=(q, k, v, beta, g, sall, do), name="dn_bwd")


def _group_norm(a, nw):
    rs = []
    for h in range(HEADS):
        ah = a[:, h * HD:(h + 1) * HD]
        rs.append(jnp.broadcast_to(_rstd(ah), ah.shape))
    r = jnp.concatenate(rs, axis=-1)
    xh = a * r
    return xh * nw, xh, r


def _group_norm_bwd(dy, xh, r, nw):
    dxh = dy * nw
    return r * (dxh - xh * (_head_sum(dxh * xh) * (1.0 / HD)))


def _sg_mix(wt_ref, svn_b, nchunk):
    rows = []
    for cidx in range(nchunk):
        cols = []
        for g in range(HEADS):
            blk = svn_b[cidx * SGC:(cidx + 1) * SGC, g * HD:(g + 1) * HD]
            cols.append(_dot(wt_ref[g], blk))
        rows.append(jnp.concatenate(cols, axis=-1))
    return jnp.concatenate(rows, axis=0)


def _ab_out(x1, o, z, su, sv, dnw, sgnw, wtril, sgb, wout):
    t = x1.shape[0]
    nchunk = TM // SGC

    def body(x_ref, o_ref, z_ref, su_ref, sv_ref, dnw_ref, sgnw_ref, wt_ref, sgb_ref, wo_ref, x2_ref, cat_ref):
        on, _, _ = _group_norm(o_ref[...], dnw_ref[...])
        zv = z_ref[...]
        cat_ref[:, 0:HW] = _c(on * (zv * _sigmoid(zv)))
        svn, _, _ = _group_norm(_gelu(sv_ref[...]), sgnw_ref[...])
        mixed = _sg_mix(wt_ref, _c(svn), nchunk) + jnp.tile(sgb_ref[...], (nchunk, 1))
        cat_ref[:, HW:] = _c(_gelu(su_ref[...]) * mixed)
        x2_ref[...] = x_ref[...] + _dot(cat_ref[...], wo_ref[...])

    row = lambda w: pl.BlockSpec((TM, w), lambda i: (i, 0))
    full = lambda a: pl.BlockSpec(a.shape, lambda i: (0,) * a.ndim)
    return pl.pallas_call(
        body,
        grid=(t // TM,),
        in_specs=[row(D)] + [row(HW)] * 4 + [full(dnw), full(sgnw), full(wtril), full(sgb), full(wout)],
        out_specs=[row(D), row(D)],
        out_shape=[jax.ShapeDtypeStruct((t, D), F32), jax.ShapeDtypeStruct((t, D), _MM)],
        name="ab_out",
    )(x1, o, z, su, sv, dnw, sgnw, wtril, sgb, wout)


def _ab_out_bwd(dx2, o, z, su, sv, dnw, sgnw, wtril, wtril_t, sgb, wout):
    t = dx2.shape[0]
    nchunk = TM // SGC

    def body(dx_ref, o_ref, z_ref, su_ref, sv_ref, dnw_ref, sgnw_ref, wt_ref, wtt_ref, sgb_ref, wo_ref,
             do_ref, dz_ref, dsu_ref, dsv_ref, ddnw_ref, dsgnw_ref, dsgw_ref, dsgb_ref):
        i = pl.program_id(0)
        dcat = _dot_nt(_c(dx_ref[...]), wo_ref[...])
        doa = dcat[:, 0:HW]
        dob = dcat[:, HW:]
        on, oh, ro = _group_norm(o_ref[...], dnw_ref[...])
        zv = z_ref[...]
        sz = _sigmoid(zv)
        dz_ref[...] = _c(doa * on * (sz * (1.0 + zv * (1.0 - sz))))
        don = doa * (zv * sz)
        do_ref[...] = _group_norm_bwd(don, oh, ro, dnw_ref[...])
        dd = jnp.sum(don * oh, axis=0, keepdims=True)
        _accum(ddnw_ref, dd[:, 0:HD] + dd[:, HD:2 * HD] + dd[:, 2 * HD:3 * HD] + dd[:, 3 * HD:], i)
        suv = su_ref[...]
        svv = sv_ref[...]
        svg = _gelu(svv)
        svn, sh, rs = _group_norm(svg, sgnw_ref[...])
        svn_b = _c(svn)
        mixed = _sg_mix(wt_ref, svn_b, nchunk) + jnp.tile(sgb_ref[...], (nchunk, 1))
        dsu_ref[...] = _c(dob * mixed * _gelu_grad(suv))
        dmixed = dob * _gelu(suv)
        dmb = _c(dmixed)
        tri = lax.broadcasted_iota(jnp.int32, (SGC, SGC), 0) >= lax.broadcasted_iota(jnp.int32, (SGC, SGC), 1)
        lane = lax.broadcasted_iota(jnp.int32, (SGC, HD), 1)
        rows = []
        dbias = jnp.zeros((SGC, HD), F32)
        for g in range(HEADS):
            gs = slice(g * HD, (g + 1) * HD)
            dwg = jnp.zeros((SGC, SGC), F32)
            col = jnp.zeros((SGC, 1), F32)
            for cidx in range(nchunk):
                cs = slice(cidx * SGC, (cidx + 1) * SGC)
                dwg = dwg + _dot_nt(dmb[cs, gs], svn_b[cs, gs])
                col = col + jnp.sum(dmixed[cs, gs], axis=-1, keepdims=True)
            _accum(dsgw_ref.at[g], jnp.where(tri, dwg, 0.0), i)
            dbias = dbias + jnp.where(lane == g, col, 0.0)
        _accum(dsgb_ref, dbias, i)
        for cidx in range(nchunk):
            cs = slice(cidx * SGC, (cidx + 1) * SGC)
            rows.append(jnp.concatenate(
                [_dot(wtt_ref[g], dmb[cs, g * HD:(g + 1) * HD]) for g in range(HEADS)], axis=-1))
        dsvn = jnp.concatenate(rows, axis=0)
        _accum(dsgnw_ref, jnp.sum(dsvn * sh, axis=0, keepdims=True), i)
        dsv_ref[...] = _c(_group_norm_bwd(dsvn, sh, rs, sgnw_ref[...]) * _gelu_grad(svv))

    row = lambda w: pl.BlockSpec((TM, w), lambda i: (i, 0))
    full = lambda a: pl.BlockSpec(a.shape, lambda i: (0,) * a.ndim)
    const = lambda shape: pl.BlockSpec(shape, lambda i: (0,) * len(shape))
    return pl.pallas_call(
        body,
        grid=(t // TM,),
        in_specs=[row(D)] + [row(HW)] * 4 + [full(dnw), full(sgnw), full(wtril), full(wtril_t), full(sgb), full(wout)],
        out_specs=[row(HW)] * 4 + [const((1, HD)), const((1, HW)), const((HEADS, SGC, SGC)), const((SGC, HD))],
        out_shape=[jax.ShapeDtypeStruct((t, HW), F32)] + [jax.ShapeDtypeStruct((t, HW), _MM)] * 3
        + [jax.ShapeDtypeStruct((1, HD), F32), jax.ShapeDtypeStruct((1, HW), F32),
           jax.ShapeDtypeStruct((HEADS, SGC, SGC), F32), jax.ShapeDtypeStruct((SGC, HD), F32)],
        name="ab_out_bwd",
    )(dx2, o, z, su, sv, dnw, sgnw, wtril, wtril_t, sgb, wout)


def _dn_pre_bwd(qkv, b_rep, a_rep, cw, alog, dtb, dqn, dkn, dv, dbeta, dg):
    t = qkv.shape[0]
    qscale = HD ** -0.5

    def body(x_ref, halo_ref, b_ref, a_ref, cw_ref, alog_ref, dt_ref, dq_ref, dk_ref, dv_ref, dbeta_ref, dg_ref,
             dc_ref, dba_ref, dcw_ref, dalog_ref, ddt_ref):
        i = pl.program_id(0)
        halo = jnp.where(i == 0, 0.0, halo_ref[...])
        c, shifted = _conv_rows(x_ref[...], halo, cw_ref[...])
        s = _sigmoid(c)
        sc = c * s
        q = sc[:, 0:HW]
        k = sc[:, HW:2 * HW]
        rq = _head_rsq(q)
        rk = _head_rsq(k)
        qu = q * rq
        ku = k * rk
        dqn = dq_ref[...]
        dkn = dk_ref[...]
        dq = qscale * rq * (dqn - qu * _head_sum(dqn * qu))
        dk = rk * (dkn - ku * _head_sum(dkn * ku))
        dsc = jnp.concatenate([dq, dk, dv_ref[...]], axis=-1)
        dc = dsc * (s * (1.0 + c * (1.0 - s)))
        dc_ref[...] = dc
        for kk in range(4):
            _accum(dcw_ref.at[kk], jnp.sum(dc * shifted[kk], axis=0, keepdims=True), i)
        beta = _sigmoid(b_ref[...])
        dbp = dbeta_ref[...] * beta * (1.0 - beta)
        nea = -jnp.exp(alog_ref[...])
        spin = a_ref[...] + dt_ref[...]
        dgv = dg_ref[...]
        dap = dgv * nea * _sigmoid(spin)
        _accum(dalog_ref, jnp.sum(dgv * nea * _softplus(spin), axis=0, keepdims=True), i)
        _accum(ddt_ref, jnp.sum(dap, axis=0, keepdims=True), i)
        lane = lax.broadcasted_iota(jnp.int32, (TM, HD), 1)
        dba = jnp.zeros((TM, HD), F32)
        for h in range(HEADS):
            dba = dba + jnp.where(lane == h, dbp[:, h * HD:(h + 1) * HD], 0.0)
            dba = dba + jnp.where(lane == HEADS + h, dap[:, h * HD:(h + 1) * HD], 0.0)
        dba_ref[...] = _c(dba)

    row = lambda w: pl.BlockSpec((TM, w), lambda i: (i, 0))
    full = lambda a: pl.BlockSpec(a.shape, lambda i: (0,) * a.ndim)
    const = lambda shape: pl.BlockSpec(shape, lambda i: (0,) * len(shape))
    return pl.pallas_call(
        body,
        grid=(t // TM,),
        in_specs=[row(QKV), _halo_prev_spec(QKV, 8), row(HW), row(HW), full(cw), full(alog), full(dtb)] + [row(HW)] * 5,
        out_specs=[row(QKV), row(HD), const((4, 1, QKV)), const((1, HW)), const((1, HW))],
        out_shape=[jax.ShapeDtypeStruct((t, QKV), F32), jax.ShapeDtypeStruct((t, HD), _MM),
                   jax.ShapeDtypeStruct((4, 1, QKV), F32), jax.ShapeDtypeStruct((1, HW), F32),
                   jax.ShapeDtypeStruct((1, HW), F32)],
        name="dn_pre_bwd",
    )(qkv, qkv, b_rep, a_rep, cw, alog, dtb, dqn, dkn, dv, dbeta, dg)


def _conv_bwd(dc, cw):
    t = dc.shape[0]
    nt = t // TM

    def body(dc_ref, halo_ref, cw_ref, dx_ref):
        i = pl.program_id(0)
        halo = jnp.where(i == nt - 1, 0.0, halo_ref[...])
        de = jnp.concatenate([dc_ref[...], halo], axis=0)
        cwv = cw_ref[...]
        acc = None
        for k in range(4):
            s = 3 - k
            ds = (de if s == 0 else pltpu.roll(de, TM + 8 - s, 0))[0:TM, :]
            term = cwv[k:k + 1, :] * ds
            acc = term if acc is None else acc + term
        dx_ref[...] = _c(acc)

    return pl.pallas_call(
        body,
        grid=(nt,),
        in_specs=[pl.BlockSpec((TM, QKV), lambda i: (i, 0)), _halo_next_spec(QKV, 8, t),
                  pl.BlockSpec(cw.shape, lambda i: (0, 0))],
        out_specs=pl.BlockSpec((TM, QKV), lambda i: (i, 0)),
        out_shape=jax.ShapeDtypeStruct((t, QKV), _MM),
        name="conv_bwd",
    )(dc, dc, cw)


def _ab_proj_bwd(x1, nw, dqkv, dz, dsu, dsv, dba, wab_b, dres):
    t = x1.shape[0]

    def body(x_ref, nw_ref, dqkv_ref, dz_ref, dsu_ref, dsv_ref, dba_ref, w_ref, dres_ref, dx_ref, dcat_ref, dnw_ref):
        i = pl.program_id(0)
        dcat_ref[:, 0:QKV] = dqkv_ref[...]
        o = QKV
        for ref in (dz_ref, dsu_ref, dsv_ref):
            dcat_ref[:, o:o + HW] = ref[...]
            o += HW
        dcat_ref[:, o:o + 128] = dba_ref[...]
        dh = _dot_nt(dcat_ref[...], w_ref[...])
        xv = x_ref[...]
        r = _rstd(xv)
        dx, dnw = _rms_bwd(dh, xv * r, r, nw_ref[...])
        dx_ref[...] = dres_ref[...] + dx
        _accum(dnw_ref, dnw, i)

    row = lambda w: pl.BlockSpec((TM, w), lambda i: (i, 0))
    return pl.pallas_call(
        body,
        grid=(t // TM,),
        in_specs=[row(D), pl.BlockSpec((1, D), lambda i: (0, 0)), row(QKV), row(HW), row(HW), row(HW), row(128),
                  pl.BlockSpec((D, PW_B), lambda i: (0, 0)), row(D)],
        out_specs=[row(D), row(PW_B), pl.BlockSpec((1, D), lambda i: (0, 0))],
        out_shape=[jax.ShapeDtypeStruct((t, D), F32), jax.ShapeDtypeStruct((t, PW_B), _MM),
                   jax.ShapeDtypeStruct((1, D), F32)],
        name="ab_proj_bwd",
    )(x1, nw, dqkv, dz, dsu, dsv, dba, wab_b, dres)


def _pool_counts(i):
    pos = (lax.broadcasted_iota(jnp.int32, (TM + HALO, 1), 0) + i * TM + 1).astype(F32)
    return [1.0 / jnp.minimum(pos, float(w)) for w in POOL_WINDOWS]


def _window_sum(ext, win, back):
    r = ext.shape[0]
    s = ext
    step = 1
    while step < win:
        s = s + pltpu.roll(s, step if back else r - step, 0)
        step *= 2
    return s


def _pooled(h_ext, invc, g):
    gs = slice(g * PG, (g + 1) * PG)
    he = h_ext[:, gs]
    ws = _window_sum(he, POOL_WINDOWS[g], True)[HALO:, :]
    return ws * invc[g][0:TM, :] - he[HALO:, :]


def _pool_fwd(x1, nw, pw, scale):
    t = x1.shape[0]

    def body(x_ref, halo_ref, nw_ref, pw_ref, sc_ref, x2_ref):
        i = pl.program_id(0)
        xv = x_ref[...]
        hv = halo_ref[...]
        nwv = nw_ref[...]
        h_ext = jnp.concatenate([jnp.where(i == 0, 0.0, hv * _rstd(hv) * nwv), xv * _rstd(xv) * nwv], axis=0)
        invc = _pool_counts(i)
        outs = [_dot(_c(_pooled(h_ext, invc, g)), pw_ref[g]) for g in range(4)]
        x2_ref[...] = xv + jnp.concatenate(outs, axis=-1) * sc_ref[...]

    return pl.pallas_call(
        body,
        grid=(t // TM,),
        in_specs=[pl.BlockSpec((TM, D), lambda i: (i, 0)), _halo_prev_spec(D, HALO),
                  pl.BlockSpec((1, D), lambda i: (0, 0)), pl.BlockSpec((4, PG, PG), lambda i: (0, 0, 0)),
                  pl.BlockSpec((1, D), lambda i: (0, 0))],
        out_specs=pl.BlockSpec((TM, D), lambda i: (i, 0)),
        out_shape=jax.ShapeDtypeStruct((t, D), F32),
        name="pool_fwd",
    )(x1, x1, nw, pw, scale)


def _pool_bwd(x1, nw, pw, scale, dx2):
    t = x1.shape[0]
    nt = t // TM

    def body(x_ref, halo_ref, nw_ref, pw_ref, sc_ref, dx2_ref, dnext_ref, dx_ref, dnw_ref, dpw_ref, dsc_ref):
        i = pl.program_id(0)
        xv = x_ref[...]
        hv = halo_ref[...]
        nwv = nw_ref[...]
        r = _rstd(xv)
        xh = xv * r
        h_ext = jnp.concatenate([jnp.where(i == 0, 0.0, hv * _rstd(hv) * nwv), xh * nwv], axis=0)
        invc = _pool_counts(i)
        dyv = dx2_ref[...]
        dout_ext = jnp.concatenate([dyv, jnp.where(i == nt - 1, 0.0, dnext_ref[...])], axis=0) * sc_ref[...]
        dh_parts = []
        dsc_parts = []
        for g in range(4):
            gs = slice(g * PG, (g + 1) * PG)
            pooled_b = _c(_pooled(h_ext, invc, g))
            dout_b = _c(dout_ext[:, gs])
            dsc_parts.append(jnp.sum(dyv[:, gs] * _dot(pooled_b, pw_ref[g]), axis=0, keepdims=True))
            _accum(dpw_ref.at[g], _dot_tn(pooled_b, dout_b[0:TM, :]), i)
            dpool_ext = _dot_nt(dout_b, pw_ref[g])
            lead = _window_sum(dpool_ext * invc[g], POOL_WINDOWS[g], False)[0:TM, :]
            dh_parts.append(lead - dpool_ext[0:TM, :])
        _accum(dsc_ref, jnp.concatenate(dsc_parts, axis=-1), i)
        dx, dnw = _rms_bwd(jnp.concatenate(dh_parts, axis=-1), xh, r, nwv)
        dx_ref[...] = dyv + dx
        _accum(dnw_ref, dnw, i)

    vec = pl.BlockSpec((1, D), lambda i: (0, 0))
    return pl.pallas_call(
        body,
        grid=(nt,),
        in_specs=[pl.BlockSpec((TM, D), lambda i: (i, 0)), _halo_prev_spec(D, HALO), vec,
                  pl.BlockSpec((4, PG, PG), lambda i: (0, 0, 0)), vec,
                  pl.BlockSpec((TM, D), lambda i: (i, 0)), _halo_next_spec(D, HALO, t)],
        out_specs=[pl.BlockSpec((TM, D), lambda i: (i, 0)), vec, pl.BlockSpec((4, PG, PG), lambda i: (0, 0, 0)), vec],
        out_shape=[jax.ShapeDtypeStruct((t, D), F32), jax.ShapeDtypeStruct((1, D), F32),
                   jax.ShapeDtypeStruct((4, PG, PG), F32), jax.ShapeDtypeStruct((1, D), F32)],
        name="pool_bwd",
    )(x1, x1, nw, pw, scale, dx2, dx2)


def _adamw(lands, w, m, v, rb, name):
    nl, nr = w.shape[0], w.shape[1]
    rest = w.shape[2:]
    ns = lands[0].shape[0]
    zeros = (0,) * len(rest)

    def body(*refs):
        l_refs = refs[0:nl]
        w_ref, m_ref, v_ref, g_ref, d_ref, m2_ref, v2_ref = refs[nl:]
        for l in range(nl):
            g = l_refs[l][0].astype(F32)
            for s in range(1, ns):
                g = g + l_refs[l][s].astype(F32)
            m2 = ADAM_B1 * m_ref[l] + (1.0 - ADAM_B1) * g
            v2 = ADAM_B2 * v_ref[l] + (1.0 - ADAM_B2) * (g * g)
            m_hat = m2 / (1.0 - ADAM_B1 ** ADAM_STEP)
            v_hat = v2 / (1.0 - ADAM_B2 ** ADAM_STEP)
            g_ref[l] = g
            d_ref[l] = -ADAM_LR * (m_hat / (jnp.sqrt(v_hat) + ADAM_EPS) + ADAM_WD * w_ref[l])
            m2_ref[l] = m2
            v2_ref[l] = v2

    lspec = pl.BlockSpec((ns, rb) + rest, lambda r: (0, r) + zeros)
    wspec = pl.BlockSpec((nl, rb) + rest, lambda r: (0, r) + zeros)
    return pl.pallas_call(
        body,
        grid=(nr // rb,),
        in_specs=[lspec] * nl + [wspec] * 3,
        out_specs=[wspec] * 4,
        out_shape=[jax.ShapeDtypeStruct(w.shape, F32)] * 4,
        name=name,
    )(*lands, w, m, v)


WEIGHT_ORDER = ("ffn_norm1", "ffn1_w_in", "ffn1_w_out", "mix_norm", "ffn_norm2", "ffn2_w_in", "ffn2_w_out", "ab_w_in",
                "dn_conv_w", "dn_a_log", "dn_dt_bias", "dn_out_norm", "sg_norm", "sg_w", "sg_b", "ab_w_out", "pool_w",
                "pool_scale", "final_norm")
R_SMALL = 88
SMALL_ROWS = (
    ("ffn_norm1", (2, D), 2), ("mix_norm", (2, D), 2), ("ffn_norm2", (2, D), 2), ("final_norm", (D,), 1),
    ("sg_w", (1, 4, SGC, SGC), 64), ("sg_norm", (1, 4, HD), 1), ("sg_b", (1, 4, SGC), 1), ("dn_out_norm", (1, HD), 1),
    ("dn_a_log", (1, 4), 1), ("dn_dt_bias", (1, 4), 1), ("pool_scale", (1, D), 1), ("dn_conv_w", (1, 4, QKV), 8),
)
SMALL_SHARDED = ("pool_scale", "dn_conv_w")


def _rows_of(a, rows):
    if a.shape[-1] == QKV:
        return jnp.pad(a.reshape(4, QKV), ((0, 0), (0, 2 * ROW - QKV))).reshape(8, ROW)
    n = _numel(a.shape)
    if n % ROW == 0:
        return a.reshape(n // ROW, ROW)
    return jnp.pad(a.reshape(1, n), ((0, 0), (0, ROW - n)))


def _from_rows(r, shape):
    if shape[-1] == QKV:
        return r.reshape(4, 2 * ROW)[:, 0:QKV].reshape(shape)
    n = _numel(shape)
    if n % ROW == 0:
        return r.reshape(shape)
    return r[:, 0:n].reshape(shape)


def _pack_small(vals):
    parts = [(_rows_of(vals[n].astype(F32), r) if n in vals else jnp.zeros((r, ROW), F32)) for n, _, r in SMALL_ROWS]
    used = sum(r for _, _, r in SMALL_ROWS)
    return jnp.concatenate(parts + [jnp.zeros((R_SMALL - used, ROW), F32)], axis=0)


def _unpack_small(packed):
    out, o = {}, 0
    for n, shape, r in SMALL_ROWS:
        out[n] = _from_rows(packed[o:o + r], shape)
        o += r
    return out


def _pack_small_shard(ps, cw):
    return jnp.concatenate([
        jnp.pad(ps, ((0, 0), (0, ROW - D // N_DEV))), jnp.pad(cw[0], ((0, 0), (0, ROW - QKV // N_DEV))),
        jnp.zeros((3, ROW), F32)], axis=0)


def _mixer_weights(g_in, g_out, g_small, small):
    w = {}
    wi = jnp.transpose(g_in, (1, 0, 2)).reshape(D, AB_IN)
    main = [wi[:, 0:2048], wi[:, 2056:AB_IN]]
    w["wab_f"] = jnp.concatenate(
        main + [jnp.repeat(wi[:, 2048:2052], HD, axis=1), jnp.repeat(wi[:, 2052:2056], HD, axis=1)], axis=1)
    w["wab_b"] = jnp.concatenate(main + [wi[:, 2048:2056], jnp.zeros((D, 120), wi.dtype)], axis=1)
    w["cw"] = jnp.transpose(g_small[:, 1:5, 0:QKV // N_DEV], (1, 0, 2)).reshape(4, QKV)
    w["ps"] = g_small[:, 0, 0:D // N_DEV].reshape(1, D)
    w["alog"] = jnp.repeat(small["dn_a_log"][0].astype(F32), HD).reshape(1, HW)
    w["dtb"] = jnp.repeat(small["dn_dt_bias"][0].astype(F32), HD).reshape(1, HW)
    w["dnw"] = jnp.tile(small["dn_out_norm"][0].astype(F32), HEADS).reshape(1, HW)
    w["sgnw"] = small["sg_norm"][0].astype(F32).reshape(1, HW)
    tri = jnp.tril(jnp.ones((SGC, SGC), dtype=bool))
    wt = jnp.where(tri, small["sg_w"][0].astype(F32), 0.0)
    w["wtril"] = _c(wt)
    w["wtril_t"] = _c(jnp.transpose(wt, (0, 2, 1)))
    w["sgb"] = jnp.repeat(jnp.transpose(small["sg_b"][0].astype(F32)), HD, axis=1)
    w["wout_ab"] = g_out.reshape(D, D)
    return w


def kernel(x, ffn_norm1, ffn1_w_in, ffn1_w_out, mix_norm, ffn_norm2, ffn2_w_in, ffn2_w_out, ab_w_in, dn_conv_w, dn_a_log, dn_dt_bias, dn_out_norm, sg_norm, sg_w, sg_b, ab_w_out, pool_w, pool_scale, final_norm, loss_target, m_ffn_norm1, m_ffn1_w_in, m_ffn1_w_out, m_mix_norm, m_ffn_norm2, m_ffn2_w_in, m_ffn2_w_out, m_ab_w_in, m_dn_conv_w, m_dn_a_log, m_dn_dt_bias, m_dn_out_norm, m_sg_norm, m_sg_w, m_sg_b, m_ab_w_out, m_pool_w, m_pool_scale, m_final_norm, v_ffn_norm1, v_ffn1_w_in, v_ffn1_w_out, v_mix_norm, v_ffn_norm2, v_ffn2_w_in, v_ffn2_w_out, v_ab_w_in, v_dn_conv_w, v_dn_a_log, v_dn_dt_bias, v_dn_out_norm, v_sg_norm, v_sg_w, v_sg_b, v_ab_w_out, v_pool_w, v_pool_scale, v_final_norm):
    wl = dict(ffn_norm1=ffn_norm1, mix_norm=mix_norm, ffn_norm2=ffn_norm2, dn_a_log=dn_a_log, dn_dt_bias=dn_dt_bias,
              dn_out_norm=dn_out_norm, sg_norm=sg_norm, sg_w=sg_w, sg_b=sg_b, final_norm=final_norm)
    ml = dict(ffn_norm1=m_ffn_norm1, mix_norm=m_mix_norm, ffn_norm2=m_ffn_norm2, dn_a_log=m_dn_a_log,
              dn_dt_bias=m_dn_dt_bias, dn_out_norm=m_dn_out_norm, sg_norm=m_sg_norm, sg_w=m_sg_w, sg_b=m_sg_b,
              final_norm=m_final_norm)
    vl = dict(ffn_norm1=v_ffn_norm1, mix_norm=v_mix_norm, ffn_norm2=v_ffn_norm2, dn_a_log=v_dn_a_log,
              dn_dt_bias=v_dn_dt_bias, dn_out_norm=v_dn_out_norm, sg_norm=v_sg_norm, sg_w=v_sg_w, sg_b=v_sg_b,
              final_norm=v_final_norm)
    row = lambda a: a.reshape(1, -1).astype(F32)
    n1 = [row(ffn_norm1[l]) for l in range(2)]
    n2 = [row(ffn_norm2[l]) for l in range(2)]
    mix = [row(mix_norm[l]) for l in range(2)]
    s_in = {(f, l): _c(wf[l]) for f, wf in enumerate((ffn1_w_in, ffn2_w_in)) for l in range(2)}
    s_out = {(f, l): _c(wf[l]) for f, wf in enumerate((ffn1_w_out, ffn2_w_out)) for l in range(2)}
    xs, tgt = x[0], loss_target[0]

    wi00, wo00 = _comm_call(_Comm("gather", [s_in[0, 0], s_out[0, 0]]), "gather_first")
    x01, (g_abin, g_about, g_small) = _ffn_fwd(
        xs, n1[0], wi00, wo00,
        comm=_Comm("gather", [_c(ab_w_in[0]), _c(ab_w_out[0]), _pack_small_shard(pool_scale, dn_conv_w)]))
    w = _mixer_weights(g_abin, g_about, g_small, wl)
    h, qkv, z, su, sv, b_rep, a_rep = _ab_proj(x01, mix[0], w["wab_f"])
    qn, kn, v, beta, g = _dn_pre(qkv, b_rep, a_rep, w["cw"], w["alog"], w["dtb"])
    (o, sall), rest = _dn_fwd(
        qn, kn, v, beta, g,
        comm=_Comm("gather", [s_in[1, 0], s_out[1, 0], s_in[0, 1], s_out[0, 1], s_in[1, 1], s_out[1, 1], _c(pool_w[0])]))
    wi10, wo10, wi01, wo01, wi11, wo11, g_pw = rest
    pw = jnp.transpose(g_pw, (1, 0, 2, 3)).reshape(4, PG, PG)
    x02, cat = _ab_out(x01, o, z, su, sv, w["dnw"], w["sgnw"], w["wtril"], w["sgb"], w["wout_ab"])
    x10, _ = _ffn_fwd(x02, n2[0], wi10, wo10)
    x11, _ = _ffn_fwd(x10, n1[1], wi01, wo01)
    x12 = _pool_fwd(x11, mix[1], pw, w["ps"])
    x13, _ = _ffn_fwd(x12, n2[1], wi11, wo11)
    loss_local, dx, d_fn = _loss_head(x13, row(final_norm), tgt)

    def ffn_b(xin, nw, w_in, w_out, dy, comm=None):
        (dxn, xn, act, dh, dnw), landed = _ffn_bwd(xin, nw, w_in, w_out, dy, comm)
        return dxn, dnw, [_mm_tn_win(xn, dh), _mm_tn_wout(act, dy)], landed

    dx, d_n2_1, g11, _ = ffn_b(x12, n2[1], wi11, wo11, dx)
    dx, d_mix_1, d_pw, d_ps = _pool_bwd(x11, mix[1], pw, w["ps"], dx)
    d_pw_sh = _c(jnp.transpose(d_pw.reshape(4, N_DEV, PG // N_DEV, PG), (1, 0, 2, 3)))
    dx, d_n1_1, g01, land11 = ffn_b(x10, n1[1], wi01, wo01, dx, _Comm("exchange", g11))
    dx, d_n2_0, g10, land01 = ffn_b(x02, n2[0], wi10, wo10, dx, _Comm("exchange", g01 + [d_pw_sh]))
    do, dz, dsu, dsv, d_dnw, d_sgnw, d_sgw, d_sgb = _ab_out_bwd(
        dx, o, z, su, sv, w["dnw"], w["sgnw"], w["wtril"], w["wtril_t"], w["sgb"], w["wout_ab"])
    d_about = _mm_tn(cat, dx, D, D, min(512, TM), _MM, "mm_tn_about").reshape(N_DEV, D // N_DEV, D)
    (dqn, dkn, dv, dbeta, dg), land10 = _dn_bwd(qn, kn, v, beta, g, sall, do, _Comm("exchange", g10))
    dc, dba, d_cw, d_alog, d_dtb = _dn_pre_bwd(qkv, b_rep, a_rep, w["cw"], w["alog"], w["dtb"], dqn, dkn, dv, dbeta, dg)
    dqkv = _conv_bwd(dc, w["cw"])
    dx, dcat, d_mix_0 = _ab_proj_bwd(x01, mix[0], dqkv, dz, dsu, dsv, dba, w["wab_b"], dx)
    d_wab = _mm_tn(h, dcat, D, 640, min(512, TM), _MM, "mm_tn_abin")
    d_abin = jnp.concatenate([d_wab[:, 0:2048], d_wab[:, 3072:3080], d_wab[:, 2048:3072]], axis=1)
    d_abin_sh = jnp.transpose(d_abin.reshape(D, N_DEV, AB_IN // N_DEV), (1, 0, 2))
    grad_x, d_n1_0, g00, land_ab = ffn_b(xs, n1[0], wi00, wo00, dx, _Comm("exchange", [d_abin_sh, d_about]))

    g_small = {
        "ffn_norm1": jnp.concatenate([d_n1_0, d_n1_1], axis=0),
        "mix_norm": jnp.concatenate([d_mix_0, d_mix_1], axis=0),
        "ffn_norm2": jnp.concatenate([d_n2_0, d_n2_1], axis=0),
        "dn_conv_w": d_cw.reshape(1, 4, QKV),
        "dn_a_log": d_alog[:, ::HD],
        "dn_dt_bias": d_dtb[:, ::HD],
        "dn_out_norm": d_dnw,
        "sg_norm": d_sgnw.reshape(1, HEADS, HD),
        "sg_w": d_sgw[None],
        "sg_b": jnp.transpose(d_sgb[:, 0:HEADS])[None],
        "pool_scale": d_ps,
        "final_norm": d_fn.reshape(D),
    }
    land00_in, land00_out, land_small = _comm_call(_Comm("exchange", g00, repl=[_pack_small(g_small)]), "exchange_last")

    res = {}
    res["ffn1_w_in"] = _adamw([land00_in, land01[0]], ffn1_w_in, m_ffn1_w_in, v_ffn1_w_in, 128, "adamw_w_in")
    res["ffn2_w_in"] = _adamw([land10[0], land11[0]], ffn2_w_in, m_ffn2_w_in, v_ffn2_w_in, 128, "adamw_w_in")
    res["ffn1_w_out"] = _adamw([land00_out, land01[1]], ffn1_w_out, m_ffn1_w_out, v_ffn1_w_out, 176, "adamw_w_out")
    res["ffn2_w_out"] = _adamw([land10[1], land11[1]], ffn2_w_out, m_ffn2_w_out, v_ffn2_w_out, 176, "adamw_w_out")
    res["ab_w_in"] = _adamw([land_ab[0]], ab_w_in, m_ab_w_in, v_ab_w_in, 256, "adamw_ab_w_in")
    res["ab_w_out"] = _adamw([land_ab[1]], ab_w_out, m_ab_w_out, v_ab_w_out, D // N_DEV, "adamw_ab_w_out")
    res["pool_w"] = _adamw([land01[2]], pool_w, m_pool_w, v_pool_w, 4, "adamw_pool_w")
    sm = _adamw([land_small], _pack_small(wl)[None], _pack_small(ml)[None], _pack_small(vl)[None], R_SMALL,
                "adamw_replicated")
    sm = [_unpack_small(a[0]) for a in sm]
    for n in wl:
        res[n] = [d[n] for d in sm]
    me = 4 * lax.axis_index("x") + 2 * lax.axis_index("y") + lax.axis_index("c")
    g_ps = lax.dynamic_slice(sm[0]["pool_scale"], (0, me * (D // N_DEV)), (1, D // N_DEV))
    g_cw = lax.dynamic_slice(sm[0]["dn_conv_w"], (0, 0, me * (QKV // N_DEV)), (1, 4, QKV // N_DEV))
    s2 = _adamw([_pack_small_shard(g_ps, g_cw)[None]], _pack_small_shard(pool_scale, dn_conv_w)[None],
                _pack_small_shard(m_pool_scale, m_dn_conv_w)[None], _pack_small_shard(v_pool_scale, v_dn_conv_w)[None],
                8, "adamw_small_sharded")
    res["pool_scale"] = [a[0, 0:1, 0:D // N_DEV] for a in s2]
    res["dn_conv_w"] = [a[0, 1:5, 0:QKV // N_DEV][None] for a in s2]

    loss = lax.psum(loss_local[0, 0], ("x", "y", "c"))
    result = [loss, grad_x[None]]
    for i in range(4):
        result += [res[n][i] for n in WEIGHT_ORDER]
    return tuple(result)
```

```python
import jax
import jax.numpy as jnp
from jax import lax
from jax.experimental import pallas as pl
from jax.experimental.pallas import tpu as pltpu

F32 = jnp.float32
_MM = jnp.bfloat16

D = 1024
FF = 2816
EPS = 1e-6
HEADS = 4
HD = 128
DNC = 64
DN_STEP = 2
SGC = 128
QKV = 3 * HEADS * HD
HW = HEADS * HD
POOL_WINDOWS = (2, 4, 8, 16)
PG = D // 4
HALO = 16
N_DEV = 8
AB_IN = 3080
ROW = 1024

TM = 512
BT = 1024
FC = 704
NJ = FF // FC
WO_ROWS = FF // N_DEV

ADAM_LR, ADAM_B1, ADAM_B2, ADAM_EPS, ADAM_WD, ADAM_STEP = 0.001, 0.9, 0.999, 1e-08, 0.01, 10

MESH_T = pl.DeviceIdType.MESH
NN = (((1,), (0,)), ((), ()))
NT = (((1,), (1,)), ((), ()))
TN = (((0,), (0,)), ((), ()))


def _c(a):
    return a.astype(_MM)


def _dg(a, b, dims):
    return lax.dot_general(a, b, dims, preferred_element_type=F32)


def _dot(a, b):
    return _dg(a, b, NN)


def _dot_nt(a, b):
    return _dg(a, b, NT)


def _dot_tn(a, b):
    return _dg(a, b, TN)


def _split2(a):
    hi = _c(a)
    return hi, _c(a - hi.astype(F32))


def _dot3(a, b, dims=NN):
    ah, al = _split2(a)
    bh, bl = _split2(b)
    return _dg(ah, bh, dims) + (_dg(ah, bl, dims) + _dg(al, bh, dims))


def _mask_dot(mask, x):
    x1 = _c(x)
    r = x - x1.astype(F32)
    x2 = _c(r)
    x3 = _c(r - x2.astype(F32))
    return _dot(mask, x1) + (_dot(mask, x2) + _dot(mask, x3))


def _sigmoid(x):
    return jax.nn.sigmoid(x)


def _gelu(x):
    return 0.5 * x * (1.0 + lax.erf(x * 0.7071067811865476))


def _gelu_grad(x):
    return 0.5 * (1.0 + lax.erf(x * 0.7071067811865476)) + x * jnp.exp(-0.5 * x * x) * 0.3989422804014327


def _accum(ref, val, step):
    @pl.when(step == 0)
    def _():
        ref[...] = val

    @pl.when(step > 0)
    def _():
        ref[...] += val


def _rstd(x):
    return lax.rsqrt(jnp.mean(x * x, axis=-1, keepdims=True) + EPS)


def _rms_bwd(dy, xhat, r, nw):
    dnw = jnp.sum(dy * xhat, axis=0, keepdims=True)
    dxh = dy * nw
    dx = r * (dxh - xhat * jnp.mean(dxh * xhat, axis=-1, keepdims=True))
    return dx, dnw


def _numel(shape):
    n = 1
    for s in shape:
        n *= s
    return n


def _peer(k, x, y, c):
    px = 1 - x if k & 4 else x
    py = 1 - y if k & 2 else y
    pc = 1 - c if k & 1 else c
    return px, py, pc


class _Comm:
    def __init__(self, kind, arrs, repl=()):
        self.kind = kind
        self.ns = len(arrs)
        self.arrs = list(arrs) + list(repl)
        self.na = len(self.arrs)

    @property
    def out_shape(self):
        out = []
        for i, a in enumerate(self.arrs):
            lead = (N_DEV,) if (self.kind == "gather" or i >= self.ns) else ()
            out.append(jax.ShapeDtypeStruct(lead + a.shape, a.dtype))
        return out

    @property
    def scratch(self):
        return [pltpu.SemaphoreType.DMA((7 * self.na,)), pltpu.SemaphoreType.DMA((7 * self.na,)),
                pltpu.SemaphoreType.DMA((self.na,))]

    def phases(self, ins, outs, sems):
        send_sems, recv_sems, local_sems = sems
        na = self.na
        x, y, c = lax.axis_index("x"), lax.axis_index("y"), lax.axis_index("c")
        if self.kind == "gather":
            me, sibling = (x, y, c), (x, y, 1 - c)
            chips = [(1 - x, y), (x, 1 - y), (1 - x, 1 - y)]

            def slot(a, px, py, pc):
                return outs[a].at[4 * px + 2 * py + pc]

            def copy(a, k, block, to, src=None):
                return pltpu.make_async_remote_copy(
                    src_ref=slot(a, *block) if src is None else src, dst_ref=slot(a, *block),
                    send_sem=send_sems.at[7 * a + k], recv_sem=recv_sems.at[7 * a + k],
                    device_id=to, device_id_type=MESH_T)

            mine = [pltpu.make_async_copy(ins[a], slot(a, *me), local_sems.at[a]) for a in range(na)]
            first, passed = [], []
            for a in range(na):
                first.append(copy(a, 0, me, sibling, src=ins[a]))
                first += [copy(a, 1 + j, me, (*chip, c), src=ins[a]) for j, chip in enumerate(chips)]
                passed += [copy(a, 4 + j, (*chip, c), sibling) for j, chip in enumerate(chips)]

            def start():
                for cp in mine + first:
                    cp.start()

            def middle():
                for a in range(na):
                    for j, chip in enumerate(chips):
                        copy(a, 1 + j, (*chip, c), me).wait_recv()
                        passed[3 * a + j].start()

            def finish():
                for a in range(na):
                    copy(a, 0, sibling, me).wait_recv()
                    for j, chip in enumerate(chips):
                        copy(a, 4 + j, (*chip, 1 - c), me).wait_recv()
                for cp in first + passed:
                    cp.wait_send()
                for cp in mine:
                    cp.wait()

            return start, middle, finish

        me = 4 * x + 2 * y + c
        ns = self.ns
        own = [pltpu.make_async_copy(ins[a].at[me] if a < ns else ins[a], outs[a].at[me], local_sems.at[a])
               for a in range(na)]
        copies = []
        for k in range(1, N_DEV):
            px, py, pc = _peer(k, x, y, c)
            peer = 4 * px + 2 * py + pc
            for a in range(na):
                copies.append(pltpu.make_async_remote_copy(
                    src_ref=ins[a].at[peer] if a < ns else ins[a], dst_ref=outs[a].at[me],
                    send_sem=send_sems.at[na * (k - 1) + a], recv_sem=recv_sems.at[na * (k - 1) + a],
                    device_id=(px, py, pc), device_id_type=MESH_T))

        def start():
            for cp in own + copies:
                cp.start()

        def middle():
            pass

        def finish():
            for cp in copies:
                cp.wait()
            for cp in own:
                cp.wait()

        return start, middle, finish


def _comm_call(comm, name):
    na = comm.na

    def body(*refs):
        start, middle, finish = comm.phases(refs[0:na], refs[na:2 * na], refs[2 * na:])
        start()
        middle()
        finish()

    hbm = pl.BlockSpec(memory_space=pltpu.HBM)
    return pl.pallas_call(
        body, out_shape=comm.out_shape, in_specs=[hbm] * na, out_specs=[hbm] * na, scratch_shapes=comm.scratch,
        name=name)(*comm.arrs)


def _carried_call(body, comm, n_in, n_out, n_scr, when, *, grid, in_specs, out_specs, out_shape, scratch_shapes,
                  operands, name):
    if comm is None:
        return pl.pallas_call(body, grid=grid, in_specs=in_specs, out_specs=out_specs, out_shape=out_shape,
                              scratch_shapes=scratch_shapes, name=name)(*operands), []
    na = comm.na

    def both(*refs):
        a = n_in + na
        b = a + n_out + na
        body(*refs[0:n_in], *refs[a:a + n_out], *refs[b:b + n_scr])
        start, middle, finish = comm.phases(refs[n_in:a], refs[a + n_out:b], refs[b + n_scr:])
        first, mid, last = when()
        pl.when(first)(start)
        pl.when(mid)(middle)
        pl.when(last)(finish)

    hbm = pl.BlockSpec(memory_space=pltpu.HBM)
    res = pl.pallas_call(
        both, grid=grid, in_specs=list(in_specs) + [hbm] * na, out_specs=list(out_specs) + [hbm] * na,
        out_shape=list(out_shape) + comm.out_shape, scratch_shapes=list(scratch_shapes) + comm.scratch,
        name=name)(*operands, *comm.arrs)
    return res[0:n_out], res[n_out:]


def _ffn_w_specs():
    return [
        pl.BlockSpec((None, D, FC), lambda i, j: (j, 0, 0)),
        pl.BlockSpec((None, D, FC), lambda i, j: (j + NJ, 0, 0)),
        pl.BlockSpec((2, WO_ROWS, D), lambda i, j: (j, 0, 0)),
    ]


def _ffn_when(nt):
    def when():
        i, j = pl.program_id(0), pl.program_id(1)
        return ((i == 0) & (j == 0), (i == nt // 2) & (j == 0), (i == nt - 1) & (j == NJ - 1))
    return when


def _ffn_fwd(x, nw, w_in, w_out, comm=None):
    t = x.shape[0]
    nt = t // TM

    def body(x_ref, nw_ref, wg_ref, wu_ref, wo3_ref, o_ref, gu_ref, xn_sc, acc_sc):
        j = pl.program_id(1)

        @pl.when(j == 0)
        def _():
            xv = x_ref[...]
            xn_sc[...] = _c(xv * _rstd(xv) * nw_ref[...])
            acc_sc[...] = jnp.zeros_like(acc_sc)

        xn = xn_sc[...]
        g = _dot(xn, wg_ref[...])
        u = _dot(xn, wu_ref[...])
        gu_ref[0] = _c(g)
        gu_ref[1] = _c(u)
        acc_sc[...] += _dot(_c(g * _sigmoid(g) * u), wo3_ref[...].reshape(FC, D))

        @pl.when(j == NJ - 1)
        def _():
            o_ref[...] = x_ref[...] + 0.5 * acc_sc[...]

    (out, gu), landed = _carried_call(
        body, comm, 5, 2, 2, _ffn_when(nt),
        grid=(nt, NJ),
        in_specs=[pl.BlockSpec((TM, D), lambda i, j: (i, 0)), pl.BlockSpec((1, D), lambda i, j: (0, 0))]
        + _ffn_w_specs(),
        out_specs=[pl.BlockSpec((TM, D), lambda i, j: (i, 0)),
                   pl.BlockSpec((None, 2, TM, FC), lambda i, j: (j, 0, i, 0))],
        out_shape=[jax.ShapeDtypeStruct((t, D), F32), jax.ShapeDtypeStruct((NJ, 2, t, FC), _MM)],
        scratch_shapes=[pltpu.VMEM((TM, D), _MM), pltpu.VMEM((TM, D), F32)],
        operands=(x, nw, w_in, w_in, w_out), name="ffn_fwd")
    return out, gu, landed


def _ffn_bwd(x, nw, w_in, w_out, gu, dy, comm=None):
    t = x.shape[0]
    nt = t // TM

    def body(x_ref, nw_ref, wg_ref, wu_ref, wo3_ref, gu_ref, dy_ref, dx_ref, xn_ref, a_ref, dh_ref, dnw_ref,
             r_sc, dyb_sc, acc_sc):
        wo = wo3_ref[...].reshape(FC, D)
        i = pl.program_id(0)
        j = pl.program_id(1)

        @pl.when(j == 0)
        def _():
            xv = x_ref[...]
            r = _rstd(xv)
            r_sc[...] = r
            xn_ref[...] = _c(xv * r * nw_ref[...])
            dyb_sc[...] = _c(0.5 * dy_ref[...])
            acc_sc[...] = jnp.zeros_like(acc_sc)

        g = gu_ref[0].astype(F32)
        u = gu_ref[1].astype(F32)
        s = _sigmoid(g)
        sl = g * s
        a_ref[...] = _c(sl * u)
        da = _dot_nt(dyb_sc[...], wo)
        dg = _c(da * u * (s * (1.0 + g * (1.0 - s))))
        du = _c(da * sl)
        dh_ref[0] = dg
        dh_ref[1] = du
        acc_sc[...] += _dot_nt(dg, wg_ref[...]) + _dot_nt(du, wu_ref[...])

        @pl.when(j == NJ - 1)
        def _():
            r = r_sc[...]
            dx, dnw = _rms_bwd(acc_sc[...], x_ref[...] * r, r, nw_ref[...])
            dx_ref[...] = dy_ref[...] + dx
            _accum(dnw_ref, dnw, i)

    return _carried_call(
        body, comm, 7, 5, 3, _ffn_when(nt),
        grid=(nt, NJ),
        in_specs=[pl.BlockSpec((TM, D), lambda i, j: (i, 0)), pl.BlockSpec((1, D), lambda i, j: (0, 0))]
        + _ffn_w_specs() + [pl.BlockSpec((None, 2, TM, FC), lambda i, j: (j, 0, i, 0)),
                            pl.BlockSpec((TM, D), lambda i, j: (i, 0))],
        out_specs=[
            pl.BlockSpec((TM, D), lambda i, j: (i, 0)),
            pl.BlockSpec((TM, D), lambda i, j: (i, 0)),
            pl.BlockSpec((None, TM, FC), lambda i, j: (j, i, 0)),
            pl.BlockSpec((None, 2, TM, FC), lambda i, j: (j, 0, i, 0)),
            pl.BlockSpec((1, D), lambda i, j: (0, 0)),
        ],
        out_shape=[
            jax.ShapeDtypeStruct((t, D), F32),
            jax.ShapeDtypeStruct((t, D), _MM),
            jax.ShapeDtypeStruct((NJ, t, FC), _MM),
            jax.ShapeDtypeStruct((NJ, 2, t, FC), _MM),
            jax.ShapeDtypeStruct((1, D), F32),
        ],
        scratch_shapes=[pltpu.VMEM((TM, 1), F32), pltpu.VMEM((TM, D), _MM), pltpu.VMEM((TM, D), F32)],
        operands=(x, nw, w_in, w_in, w_out, gu, dy), name="ffn_bwd")


def _mm_tn(a, b, bm, bn, bt, out_dtype, name):
    t, m = a.shape
    n = b.shape[1]
    nt = t // bt

    def body(a_ref, b_ref, o_ref, acc_sc):
        k = pl.program_id(2)
        _accum(acc_sc, _dot_tn(_c(a_ref[...]), _c(b_ref[...])), k)

        @pl.when(k == nt - 1)
        def _():
            o_ref[...] = acc_sc[...].astype(out_dtype)

    return pl.pallas_call(
        body,
        grid=(m // bm, n // bn, nt),
        in_specs=[pl.BlockSpec((bt, bm), lambda i, j, k: (k, i)), pl.BlockSpec((bt, bn), lambda i, j, k: (k, j))],
        out_specs=pl.BlockSpec((bm, bn), lambda i, j, k: (i, j)),
        out_shape=jax.ShapeDtypeStruct((m, n), out_dtype),
        scratch_shapes=[pltpu.VMEM((bm, bn), F32)],
        name=name,
    )(a, b)


def _mm_tn_win(xn, dh):
    t = xn.shape[0]
    bt = min(BT, t)
    nt = t // bt

    def body(a_ref, b_ref, o_ref, acc_sc):
        k = pl.program_id(2)
        _accum(acc_sc, _dot_tn(a_ref[...], b_ref[...]), k)

        @pl.when(k == nt - 1)
        def _():
            o_ref[...] = _c(acc_sc[...])

    return pl.pallas_call(
        body,
        grid=(2, NJ, nt),
        in_specs=[pl.BlockSpec((bt, D), lambda h, j, k: (k, 0)),
                  pl.BlockSpec((None, None, bt, FC), lambda h, j, k: (j, h, k, 0))],
        out_specs=pl.BlockSpec((None, D, FC), lambda h, j, k: (h * NJ + j, 0, 0)),
        out_shape=jax.ShapeDtypeStruct((N_DEV, D, FC), _MM),
        scratch_shapes=[pltpu.VMEM((D, FC), F32)],
        name="mm_tn_win",
    )(xn, dh)


def _mm_tn_wout(act, dy):
    t = dy.shape[0]
    bt = min(BT, t)
    nt = t // bt

    def body(a_ref, b_ref, o_ref, acc_sc):
        k = pl.program_id(1)
        _accum(acc_sc, _dot_tn(a_ref[...], _c(b_ref[...])), k)

        @pl.when(k == nt - 1)
        def _():
            o_ref[...] = _c((0.5 * acc_sc[...]).reshape(2, WO_ROWS, D))

    return pl.pallas_call(
        body,
        grid=(NJ, nt),
        in_specs=[pl.BlockSpec((None, bt, FC), lambda j, k: (j, k, 0)), pl.BlockSpec((bt, D), lambda j, k: (k, 0))],
        out_specs=pl.BlockSpec((2, WO_ROWS, D), lambda j, k: (j, 0, 0)),
        out_shape=jax.ShapeDtypeStruct((N_DEV, WO_ROWS, D), _MM),
        scratch_shapes=[pltpu.VMEM((FC, D), F32)],
        name="mm_tn_wout",
    )(act, dy)


def _loss_head(x, nw, tgt):
    t = x.shape[0]

    def body(x_ref, nw_ref, t_ref, loss_ref, dx_ref, dnw_ref):
        i = pl.program_id(0)
        xv = x_ref[...]
        r = _rstd(xv)
        xh = xv * r
        e = xh * nw_ref[...] - t_ref[...]
        part = 0.5 * jnp.sum(jnp.mean(e * e, axis=-1, keepdims=True), axis=0, keepdims=True)
        _accum(loss_ref, jnp.broadcast_to(part, (1, 128)), i)
        dx, dnw = _rms_bwd(e * (1.0 / D), xh, r, nw_ref[...])
        dx_ref[...] = dx
        _accum(dnw_ref, dnw, i)

    return pl.pallas_call(
        body,
        grid=(t // TM,),
        in_specs=[pl.BlockSpec((TM, D), lambda i: (i, 0)), pl.BlockSpec((1, D), lambda i: (0, 0)),
                  pl.BlockSpec((TM, D), lambda i: (i, 0))],
        out_specs=[pl.BlockSpec((1, 128), lambda i: (0, 0)), pl.BlockSpec((TM, D), lambda i: (i, 0)),
                   pl.BlockSpec((1, D), lambda i: (0, 0))],
        out_shape=[jax.ShapeDtypeStruct((1, 128), F32), jax.ShapeDtypeStruct((t, D), F32),
                   jax.ShapeDtypeStruct((1, D), F32)],
        name="loss_head",
    )(x, nw, tgt)


PW_F = QKV + 5 * HW
PW_B = QKV + 3 * HW + 128


def _ab_proj(x1, nw, wab):
    t = x1.shape[0]

    def body(x_ref, nw_ref, w_ref, h_ref, qkv_ref, z_ref, su_ref, sv_ref, b_ref, a_ref):
        xv = x_ref[...]
        h = _c(xv * _rstd(xv) * nw_ref[...])
        h_ref[...] = h
        p = _dot(h, w_ref[...])
        qkv_ref[...] = p[:, 0:QKV]
        o = QKV
        for ref in (z_ref, su_ref, sv_ref, b_ref, a_ref):
            ref[...] = p[:, o:o + HW]
            o += HW

    row = lambda w: pl.BlockSpec((TM, w), lambda i: (i, 0))
    return pl.pallas_call(
        body,
        grid=(t // TM,),
        in_specs=[row(D), pl.BlockSpec((1, D), lambda i: (0, 0)), pl.BlockSpec((D, PW_F), lambda i: (0, 0))],
        out_specs=[row(D), row(QKV)] + [row(HW)] * 5,
        out_shape=[jax.ShapeDtypeStruct((t, D), _MM), jax.ShapeDtypeStruct((t, QKV), F32)]
        + [jax.ShapeDtypeStruct((t, HW), F32)] * 5,
        name="ab_proj",
    )(x1, nw, wab)


def _conv_rows(x, halo, cw):
    xe = jnp.concatenate([halo, x], axis=0)
    shifted = []
    c = None
    for k in range(4):
        s = 3 - k
        xs = (xe if s == 0 else pltpu.roll(xe, s, 0))[8:, :]
        shifted.append(xs)
        term = cw[k:k + 1, :] * xs
        c = term if c is None else c + term
    return c, shifted


def _head_rsq(a):
    parts = []
    for h in range(HEADS):
        ah = a[:, h * HD:(h + 1) * HD]
        r = lax.rsqrt(jnp.sum(ah * ah, axis=-1, keepdims=True) + EPS)
        parts.append(jnp.broadcast_to(r, ah.shape))
    return jnp.concatenate(parts, axis=-1)


def _head_sum(a):
    parts = []
    for h in range(HEADS):
        ah = a[:, h * HD:(h + 1) * HD]
        parts.append(jnp.broadcast_to(jnp.sum(ah, axis=-1, keepdims=True), ah.shape))
    return jnp.concatenate(parts, axis=-1)


def _softplus(x):
    return jnp.maximum(x, 0.0) + jnp.log1p(jnp.exp(-jnp.abs(x)))


def _halo_prev_spec(width, rows):
    per = TM // rows
    return pl.BlockSpec((rows, width), lambda i: (jnp.maximum(i * per - 1, 0), 0))


def _halo_next_spec(width, rows, t):
    per = TM // rows
    last = t // rows - 1
    return pl.BlockSpec((rows, width), lambda i: (jnp.minimum((i + 1) * per, last), 0))


def _dn_pre(qkv, b_rep, a_rep, cw, alog, dtb):
    t = qkv.shape[0]
    qscale = HD ** -0.5

    def body(x_ref, halo_ref, b_ref, a_ref, cw_ref, alog_ref, dt_ref, q_ref, k_ref, v_ref, beta_ref, g_ref):
        i = pl.program_id(0)
        halo = jnp.where(i == 0, 0.0, halo_ref[...])
        c, _ = _conv_rows(x_ref[...], halo, cw_ref[...])
        sc = c * _sigmoid(c)
        q = sc[:, 0:HW]
        k = sc[:, HW:2 * HW]
        q_ref[...] = q * _head_rsq(q) * qscale
        k_ref[...] = k * _head_rsq(k)
        v_ref[...] = sc[:, 2 * HW:]
        beta_ref[...] = _sigmoid(b_ref[...])
        g_ref[...] = -jnp.exp(alog_ref[...]) * _softplus(a_ref[...] + dt_ref[...])

    row = lambda w: pl.BlockSpec((TM, w), lambda i: (i, 0))
    full = lambda a: pl.BlockSpec(a.shape, lambda i: (0,) * a.ndim)
    return pl.pallas_call(
        body,
        grid=(t // TM,),
        in_specs=[row(QKV), _halo_prev_spec(QKV, 8), row(HW), row(HW), full(cw), full(alog), full(dtb)],
        out_specs=[row(HW)] * 5,
        out_shape=[jax.ShapeDtypeStruct((t, HW), F32)] * 5,
        name="dn_pre",
    )(qkv, qkv, b_rep, a_rep, cw, alog, dtb)


def _unit_lower_inv(lo, eye):
    p = eye - lo
    lp = lo
    for _ in range(5):
        lp = _dot3(lp, lp)
        p = p + _dot3(p, lp)
    return p


def _dn_chunk_local(q, k, v, b, gr):
    ri = lax.broadcasted_iota(jnp.int32, (DNC, DNC), 0)
    ci = lax.broadcasted_iota(jnp.int32, (DNC, DNC), 1)
    strict = ri > ci
    causal = ri >= ci
    ltri = _c(causal.astype(F32))
    eye = (ri == ci).astype(F32)
    diff = _mask_dot(ltri, jnp.where(strict, gr[:, 0:DNC], 0.0))
    dm = jnp.where(causal, jnp.exp(diff), 0.0)
    gc = _mask_dot(ltri, gr)
    gl = jnp.sum(gr, axis=0, keepdims=True)
    e = jnp.exp(gc)
    f = jnp.exp(gl - gc)
    kb = k * b
    m = _dot_nt(_c(kb), _c(k))
    lo = jnp.where(strict, m * dm, 0.0)
    a = _unit_lower_inv(lo, eye)
    vb = v * b
    kbe = kb * e
    u = _dot3(a, vb)
    w = _dot3(a, kbe)
    p = _dot_nt(_c(q), _c(k))
    return dict(strict=strict, dm=dm, e=e, f=f, gl=gl, kb=kb, m=m, a=a, vb=vb, kbe=kbe, u=u, w=w,
                p=p, attn=p * dm, qd=q * e, kt=k * f)


def _dn_when(n):
    def when():
        i = pl.program_id(0)
        return (i == 0, i == n // 2, i == n - 1)
    return when


def _dn_fwd(q, k, v, beta, g, comm=None):
    t = q.shape[0]
    rows = DN_STEP * DNC
    n = t // rows

    def body(q_ref, k_ref, v_ref, b_ref, g_ref, o_ref, sall_ref, s_sc):
        i = pl.program_id(0)

        @pl.when(i == 0)
        def _():
            s_sc[...] = jnp.zeros_like(s_sc)

        for h in range(HEADS):
            sl = slice(h * HD, (h + 1) * HD)
            cls = []
            for cc in range(DN_STEP):
                rs = slice(cc * DNC, (cc + 1) * DNC)
                cls.append(_dn_chunk_local(q_ref[rs, sl], k_ref[rs, sl], v_ref[rs, sl], b_ref[rs, sl], g_ref[rs, sl]))
            s = s_sc[h]
            for cc in range(DN_STEP):
                cl = cls[cc]
                sall_ref[cc, h] = s
                sb = _c(s)
                vn = cl["u"] - _dot(_c(cl["w"]), sb)
                o_ref[cc * DNC:(cc + 1) * DNC, sl] = _dot(_c(cl["qd"]), sb) + _dot(_c(cl["attn"]), _c(vn))
                s = s * jnp.exp(cl["gl"]) + _dot_tn(_c(cl["kt"]), _c(vn))
            s_sc[h] = s

    row = pl.BlockSpec((rows, HW), lambda i: (i, 0))
    return _carried_call(
        body, comm, 5, 2, 1, _dn_when(n),
        grid=(n,),
        in_specs=[row] * 5,
        out_specs=[row, pl.BlockSpec((DN_STEP, HEADS, HD, HD), lambda i: (i, 0, 0, 0))],
        out_shape=[jax.ShapeDtypeStruct((t, HW), F32), jax.ShapeDtypeStruct((t // DNC, HEADS, HD, HD), F32)],
        scratch_shapes=[pltpu.VMEM((HEADS, HD, HD), F32)],
        operands=(q, k, v, beta, g), name="dn_fwd")


def _dn_bwd(q, k, v, beta, g, sall, do, comm=None):
    t = q.shape[0]
    rows = DN_STEP * DNC
    n = t // rows

    def body(q_ref, k_ref, v_ref, b_ref, g_ref, sall_ref, do_ref, dq_ref, dk_ref, dv_ref, db_ref, dg_ref, ds_sc):
        i = pl.program_id(0)

        @pl.when(i == 0)
        def _():
            ds_sc[...] = jnp.zeros_like(ds_sc)

        ri = lax.broadcasted_iota(jnp.int32, (DNC, DNC), 0)
        ci = lax.broadcasted_iota(jnp.int32, (DNC, DNC), 1)
        upper = _c((ri <= ci).astype(F32))
        for h in range(HEADS):
            sl = slice(h * HD, (h + 1) * HD)
            cls = []
            for cc in range(DN_STEP):
                rs = slice(cc * DNC, (cc + 1) * DNC)
                cls.append(_dn_chunk_local(q_ref[rs, sl], k_ref[rs, sl], v_ref[rs, sl], b_ref[rs, sl], g_ref[rs, sl]))
            dsn = ds_sc[h]
            for cc in reversed(range(DN_STEP)):
                rs = slice(cc * DNC, (cc + 1) * DNC)
                cl = cls[cc]
                qh, kh, vh, bh = q_ref[rs, sl], k_ref[rs, sl], v_ref[rs, sl], b_ref[rs, sl]
                strict, dm, e, f, a = cl["strict"], cl["dm"], cl["e"], cl["f"], cl["a"]
                s = sall_ref[cc, h]
                sb = _c(s)
                dsb = _c(dsn)
                dob = _c(do_ref[rs, sl])
                egl = jnp.exp(cl["gl"])
                vn = cl["u"] - _dot(_c(cl["w"]), sb)
                vnb = _c(vn)
                dvn = _dot_tn(_c(cl["attn"]), dob) + _dot(_c(cl["kt"]), dsb)
                dvnb = _c(dvn)
                dqd = _dot_nt(dob, sb)
                dattn = _dot_nt(dob, vnb)
                dkt = _dot_nt(vnb, dsb)
                dgl = jnp.sum(jnp.sum(dsn * s, axis=1, keepdims=True), axis=0, keepdims=True) * egl[:, 0:1]
                dw = -_dot_nt(dvnb, sb)
                dsn = dsn * egl + _dot_tn(_c(cl["qd"]), dob) - _dot_tn(_c(cl["w"]), dvnb)
                dp = _c(dattn * dm)
                dq = dqd * e + _dot(dp, _c(kh))
                dk = _dot_tn(dp, _c(qh)) + dkt * f
                dd = dattn * cl["p"]
                de = jnp.sum(dqd * qh, axis=-1, keepdims=True)
                dff = jnp.sum(dkt * kh, axis=-1, keepdims=True) * f[:, 0:1]
                dgl = dgl + jnp.sum(dff, axis=0, keepdims=True)
                dvb = _dot3(a, dvn, TN)
                dkbe = _dot3(a, dw, TN)
                dlo = jnp.where(strict, -(_dot3(dvb, cl["u"], NT) + _dot3(dkbe, cl["w"], NT)), 0.0)
                dmm = _c(dlo * dm)
                dd = dd + dlo * cl["m"]
                dkb = _dot(dmm, _c(kh)) + dkbe * e
                dk = dk + _dot_tn(dmm, _c(cl["kb"])) + dkb * bh
                de = de + jnp.sum(dkbe * cl["kb"], axis=-1, keepdims=True)
                dbeta = jnp.sum(dkb * kh, axis=-1, keepdims=True) + jnp.sum(dvb * vh, axis=-1, keepdims=True)
                gm = dd * dm
                dgc = de * e[:, 0:1] - dff
                dg = (jnp.sum(jnp.where(strict, _mask_dot(upper, gm), 0.0), axis=-1, keepdims=True)
                      + _mask_dot(upper, jnp.broadcast_to(dgc, (DNC, HD)))[:, 0:1] + dgl)
                dq_ref[rs, sl] = dq
                dk_ref[rs, sl] = dk
                dv_ref[rs, sl] = dvb * bh
                db_ref[rs, sl] = jnp.broadcast_to(dbeta, (DNC, HD))
                dg_ref[rs, sl] = jnp.broadcast_to(dg, (DNC, HD))
            ds_sc[h] = dsn

    row = pl.BlockSpec((rows, HW), lambda i: (n - 1 - i, 0))
    return _carried_call(
        body, comm, 7, 5, 1, _dn_when(n),
        grid=(n,),
        in_specs=[row] * 5 + [pl.BlockSpec((DN_STEP, HEADS, HD, HD), lambda i: (n - 1 - i, 0, 0, 0)), row],
        out_specs=[row] * 5,
        out_shape=[jax.ShapeDtypeStruct((t, HW), F32)] * 5,
        scratch_shapes=[pltpu.VMEM((HEADS, HD, HD), F32)],
        operands=(q, k, v, beta, g, sall, do), name="dn_bwd")


def _group_norm(a, nw):
    rs = []
    for h in range(HEADS):
        ah = a[:, h * HD:(h + 1) * HD]
        rs.append(jnp.broadcast_to(_rstd(ah), ah.shape))
    r = jnp.concatenate(rs, axis=-1)
    xh = a * r
    return xh * nw, xh, r


def _group_norm_bwd(dy, xh, r, nw):
    dxh = dy * nw
    return r * (dxh - xh * (_head_sum(dxh * xh) * (1.0 / HD)))


def _sg_mix(wt_ref, svn_b, nchunk):
    rows = []
    for cidx in range(nchunk):
        cols = []
        for g in range(HEADS):
            blk = svn_b[cidx * SGC:(cidx + 1) * SGC, g * HD:(g + 1) * HD]
            cols.append(_dot(wt_ref[g], blk))
        rows.append(jnp.concatenate(cols, axis=-1))
    return jnp.concatenate(rows, axis=0)


def _ab_out(x1, o, z, su, sv, dnw, sgnw, wtril, sgb, wout):
    t = x1.shape[0]
    nchunk = TM // SGC

    def body(x_ref, o_ref, z_ref, su_ref, sv_ref, dnw_ref, sgnw_ref, wt_ref, sgb_ref, wo_ref, x2_ref, cat_ref):
        on, _, _ = _group_norm(o_ref[...], dnw_ref[...])
        zv = z_ref[...]
        cat_ref[:, 0:HW] = _c(on * (zv * _sigmoid(zv)))
        svn, _, _ = _group_norm(_gelu(sv_ref[...]), sgnw_ref[...])
        mixed = _sg_mix(wt_ref, _c(svn), nchunk) + jnp.tile(sgb_ref[...], (nchunk, 1))
        cat_ref[:, HW:] = _c(_gelu(su_ref[...]) * mixed)
        x2_ref[...] = x_ref[...] + _dot(cat_ref[...], wo_ref[...])

    row = lambda w: pl.BlockSpec((TM, w), lambda i: (i, 0))
    full = lambda a: pl.BlockSpec(a.shape, lambda i: (0,) * a.ndim)
    return pl.pallas_call(
        body,
        grid=(t // TM,),
        in_specs=[row(D)] + [row(HW)] * 4 + [full(dnw), full(sgnw), full(wtril), full(sgb), full(wout)],
        out_specs=[row(D), row(D)],
        out_shape=[jax.ShapeDtypeStruct((t, D), F32), jax.ShapeDtypeStruct((t, D), _MM)],
        name="ab_out",
    )(x1, o, z, su, sv, dnw, sgnw, wtril, sgb, wout)


def _ab_out_bwd(dx2, o, z, su, sv, dnw, sgnw, wtril, wtril_t, sgb, wout):
    t = dx2.shape[0]
    nchunk = TM // SGC

    def body(dx_ref, o_ref, z_ref, su_ref, sv_ref, dnw_ref, sgnw_ref, wt_ref, wtt_ref, sgb_ref, wo_ref,
             do_ref, dz_ref, dsu_ref, dsv_ref, ddnw_ref, dsgnw_ref, dsgw_ref, dsgb_ref):
        i = pl.program_id(0)
        dcat = _dot_nt(_c(dx_ref[...]), wo_ref[...])
        doa = dcat[:, 0:HW]
        dob = dcat[:, HW:]
        on, oh, ro = _group_norm(o_ref[...], dnw_ref[...])
        zv = z_ref[...]
        sz = _sigmoid(zv)
        dz_ref[...] = _c(doa * on * (sz * (1.0 + zv * (1.0 - sz))))
        don = doa * (zv * sz)
        do_ref[...] = _group_norm_bwd(don, oh, ro, dnw_ref[...])
        dd = jnp.sum(don * oh, axis=0, keepdims=True)
        _accum(ddnw_ref, dd[:, 0:HD] + dd[:, HD:2 * HD] + dd[:, 2 * HD:3 * HD] + dd[:, 3 * HD:], i)
        suv = su_ref[...]
        svv = sv_ref[...]
        svg = _gelu(svv)
        svn, sh, rs = _group_norm(svg, sgnw_ref[...])
        svn_b = _c(svn)
        mixed = _sg_mix(wt_ref, svn_b, nchunk) + jnp.tile(sgb_ref[...], (nchunk, 1))
        dsu_ref[...] = _c(dob * mixed * _gelu_grad(suv))
        dmixed = dob * _gelu(suv)
        dmb = _c(dmixed)
        tri = lax.broadcasted_iota(jnp.int32, (SGC, SGC), 0) >= lax.broadcasted_iota(jnp.int32, (SGC, SGC), 1)
        lane = lax.broadcasted_iota(jnp.int32, (SGC, HD), 1)
        rows = []
        dbias = jnp.zeros((SGC, HD), F32)
        for g in range(HEADS):
            gs = slice(g * HD, (g + 1) * HD)
            dwg = jnp.zeros((SGC, SGC), F32)
            col = jnp.zeros((SGC, 1), F32)
            for cidx in range(nchunk):
                cs = slice(cidx * SGC, (cidx + 1) * SGC)
                dwg = dwg + _dot_nt(dmb[cs, gs], svn_b[cs, gs])
                col = col + jnp.sum(dmixed[cs, gs], axis=-1, keepdims=True)
            _accum(dsgw_ref.at[g], jnp.where(tri, dwg, 0.0), i)
            dbias = dbias + jnp.where(lane == g, col, 0.0)
        _accum(dsgb_ref, dbias, i)
        for cidx in range(nchunk):
            cs = slice(cidx * SGC, (cidx + 1) * SGC)
            rows.append(jnp.concatenate(
                [_dot(wtt_ref[g], dmb[cs, g * HD:(g + 1) * HD]) for g in range(HEADS)], axis=-1))
        dsvn = jnp.concatenate(rows, axis=0)
        _accum(dsgnw_ref, jnp.sum(dsvn * sh, axis=0, keepdims=True), i)
        dsv_ref[...] = _c(_group_norm_bwd(dsvn, sh, rs, sgnw_ref[...]) * _gelu_grad(svv))

    row = lambda w: pl.BlockSpec((TM, w), lambda i: (i, 0))
    full = lambda a: pl.BlockSpec(a.shape, lambda i: (0,) * a.ndim)
    const = lambda shape: pl.BlockSpec(shape, lambda i: (0,) * len(shape))
    return pl.pallas_call(
        body,
        grid=(t // TM,),
        in_specs=[row(D)] + [row(HW)] * 4 + [full(dnw), full(sgnw), full(wtril), full(wtril_t), full(sgb), full(wout)],
        out_specs=[row(HW)] * 4 + [const((1, HD)), const((1, HW)), const((HEADS, SGC, SGC)), const((SGC, HD))],
        out_shape=[jax.ShapeDtypeStruct((t, HW), F32)] + [jax.ShapeDtypeStruct((t, HW), _MM)] * 3
        + [jax.ShapeDtypeStruct((1, HD), F32), jax.ShapeDtypeStruct((1, HW), F32),
           jax.ShapeDtypeStruct((HEADS, SGC, SGC), F32), jax.ShapeDtypeStruct((SGC, HD), F32)],
        name="ab_out_bwd",
    )(dx2, o, z, su, sv, dnw, sgnw, wtril, wtril_t, sgb, wout)


def _dn_pre_bwd(qkv, b_rep, a_rep, cw, alog, dtb, dqn, dkn, dv, dbeta, dg):
    t = qkv.shape[0]
    qscale = HD ** -0.5

    def body(x_ref, halo_ref, b_ref, a_ref, cw_ref, alog_ref, dt_ref, dq_ref, dk_ref, dv_ref, dbeta_ref, dg_ref,
             dc_ref, dba_ref, dcw_ref, dalog_ref, ddt_ref):
        i = pl.program_id(0)
        halo = jnp.where(i == 0, 0.0, halo_ref[...])
        c, shifted = _conv_rows(x_ref[...], halo, cw_ref[...])
        s = _sigmoid(c)
        sc = c * s
        q = sc[:, 0:HW]
        k = sc[:, HW:2 * HW]
        rq = _head_rsq(q)
        rk = _head_rsq(k)
        qu = q * rq
        ku = k * rk
        dqn = dq_ref[...]
        dkn = dk_ref[...]
        dq = qscale * rq * (dqn - qu * _head_sum(dqn * qu))
        dk = rk * (dkn - ku * _head_sum(dkn * ku))
        dsc = jnp.concatenate([dq, dk, dv_ref[...]], axis=-1)
        dc = dsc * (s * (1.0 + c * (1.0 - s)))
        dc_ref[...] = dc
        for kk in range(4):
            _accum(dcw_ref.at[kk], jnp.sum(dc * shifted[kk], axis=0, keepdims=True), i)
        beta = _sigmoid(b_ref[...])
        dbp = dbeta_ref[...] * beta * (1.0 - beta)
        nea = -jnp.exp(alog_ref[...])
        spin = a_ref[...] + dt_ref[...]
        dgv = dg_ref[...]
        dap = dgv * nea * _sigmoid(spin)
        _accum(dalog_ref, jnp.sum(dgv * nea * _softplus(spin), axis=0, keepdims=True), i)
        _accum(ddt_ref, jnp.sum(dap, axis=0, keepdims=True), i)
        lane = lax.broadcasted_iota(jnp.int32, (TM, HD), 1)
        dba = jnp.zeros((TM, HD), F32)
        for h in range(HEADS):
            dba = dba + jnp.where(lane == h, dbp[:, h * HD:(h + 1) * HD], 0.0)
            dba = dba + jnp.where(lane == HEADS + h, dap[:, h * HD:(h + 1) * HD], 0.0)
        dba_ref[...] = _c(dba)

    row = lambda w: pl.BlockSpec((TM, w), lambda i: (i, 0))
    full = lambda a: pl.BlockSpec(a.shape, lambda i: (0,) * a.ndim)
    const = lambda shape: pl.BlockSpec(shape, lambda i: (0,) * len(shape))
    return pl.pallas_call(
        body,
        grid=(t // TM,),
        in_specs=[row(QKV), _halo_prev_spec(QKV, 8), row(HW), row(HW), full(cw), full(alog), full(dtb)] + [row(HW)] * 5,
        out_specs=[row(QKV), row(HD), const((4, 1, QKV)), const((1, HW)), const((1, HW))],
        out_shape=[jax.ShapeDtypeStruct((t, QKV), F32), jax.ShapeDtypeStruct((t, HD), _MM),
                   jax.ShapeDtypeStruct((4, 1, QKV), F32), jax.ShapeDtypeStruct((1, HW), F32),
                   jax.ShapeDtypeStruct((1, HW), F32)],
        name="dn_pre_bwd",
    )(qkv, qkv, b_rep, a_rep, cw, alog, dtb, dqn, dkn, dv, dbeta, dg)


def _conv_bwd(dc, cw):
    t = dc.shape[0]
    nt = t // TM

    def body(dc_ref, halo_ref, cw_ref, dx_ref):
        i = pl.program_id(0)
        halo = jnp.where(i == nt - 1, 0.0, halo_ref[...])
        de = jnp.concatenate([dc_ref[...], halo], axis=0)
        cwv = cw_ref[...]
        acc = None
        for k in range(4):
            s = 3 - k
            ds = (de if s == 0 else pltpu.roll(de, TM + 8 - s, 0))[0:TM, :]
            term = cwv[k:k + 1, :] * ds
            acc = term if acc is None else acc + term
        dx_ref[...] = _c(acc)

    return pl.pallas_call(
        body,
        grid=(nt,),
        in_specs=[pl.BlockSpec((TM, QKV), lambda i: (i, 0)), _halo_next_spec(QKV, 8, t),
                  pl.BlockSpec(cw.shape, lambda i: (0, 0))],
        out_specs=pl.BlockSpec((TM, QKV), lambda i: (i, 0)),
        out_shape=jax.ShapeDtypeStruct((t, QKV), _MM),
        name="conv_bwd",
    )(dc, dc, cw)


def _ab_proj_bwd(x1, nw, dqkv, dz, dsu, dsv, dba, wab_b, dres):
    t = x1.shape[0]

    def body(x_ref, nw_ref, dqkv_ref, dz_ref, dsu_ref, dsv_ref, dba_ref, w_ref, dres_ref, dx_ref, dcat_ref, dnw_ref):
        i = pl.program_id(0)
        dcat_ref[:, 0:QKV] = dqkv_ref[...]
        o = QKV
        for ref in (dz_ref, dsu_ref, dsv_ref):
            dcat_ref[:, o:o + HW] = ref[...]
            o += HW
        dcat_ref[:, o:o + 128] = dba_ref[...]
        dh = _dot_nt(dcat_ref[...], w_ref[...])
        xv = x_ref[...]
        r = _rstd(xv)
        dx, dnw = _rms_bwd(dh, xv * r, r, nw_ref[...])
        dx_ref[...] = dres_ref[...] + dx
        _accum(dnw_ref, dnw, i)

    row = lambda w: pl.BlockSpec((TM, w), lambda i: (i, 0))
    return pl.pallas_call(
        body,
        grid=(t // TM,),
        in_specs=[row(D), pl.BlockSpec((1, D), lambda i: (0, 0)), row(QKV), row(HW), row(HW), row(HW), row(128),
                  pl.BlockSpec((D, PW_B), lambda i: (0, 0)), row(D)],
        out_specs=[row(D), row(PW_B), pl.BlockSpec((1, D), lambda i: (0, 0))],
        out_shape=[jax.ShapeDtypeStruct((t, D), F32), jax.ShapeDtypeStruct((t, PW_B), _MM),
                   jax.ShapeDtypeStruct((1, D), F32)],
        name="ab_proj_bwd",
    )(x1, nw, dqkv, dz, dsu, dsv, dba, wab_b, dres)


def _pool_counts(i):
    pos = (lax.broadcasted_iota(jnp.int32, (TM + HALO, 1), 0) + i * TM + 1).astype(F32)
    return [1.0 / jnp.minimum(pos, float(w)) for w in POOL_WINDOWS]


def _window_sum(ext, win, back):
    r = ext.shape[0]
    s = ext
    step = 1
    while step < win:
        s = s + pltpu.roll(s, step if back else r - step, 0)
        step *= 2
    return s


def _pooled(h_ext, invc, g):
    gs = slice(g * PG, (g + 1) * PG)
    he = h_ext[:, gs]
    ws = _window_sum(he, POOL_WINDOWS[g], True)[HALO:, :]
    return ws * invc[g][0:TM, :] - he[HALO:, :]


def _pool_fwd(x1, nw, pw, scale):
    t = x1.shape[0]

    def body(x_ref, halo_ref, nw_ref, pw_ref, sc_ref, x2_ref):
        i = pl.program_id(0)
        xv = x_ref[...]
        hv = halo_ref[...]
        nwv = nw_ref[...]
        h_ext = jnp.concatenate([jnp.where(i == 0, 0.0, hv * _rstd(hv) * nwv), xv * _rstd(xv) * nwv], axis=0)
        invc = _pool_counts(i)
        outs = [_dot(_c(_pooled(h_ext, invc, g)), pw_ref[g]) for g in range(4)]
        x2_ref[...] = xv + jnp.concatenate(outs, axis=-1) * sc_ref[...]

    return pl.pallas_call(
        body,
        grid=(t // TM,),
        in_specs=[pl.BlockSpec((TM, D), lambda i: (i, 0)), _halo_prev_spec(D, HALO),
                  pl.BlockSpec((1, D), lambda i: (0, 0)), pl.BlockSpec((4, PG, PG), lambda i: (0, 0, 0)),
                  pl.BlockSpec((1, D), lambda i: (0, 0))],
        out_specs=pl.BlockSpec((TM, D), lambda i: (i, 0)),
        out_shape=jax.ShapeDtypeStruct((t, D), F32),
        name="pool_fwd",
    )(x1, x1, nw, pw, scale)


def _pool_bwd(x1, nw, pw, scale, dx2):
    t = x1.shape[0]
    nt = t // TM

    def body(x_ref, halo_ref, nw_ref, pw_ref, sc_ref, dx2_ref, dnext_ref, dx_ref, dnw_ref, dpw_ref, dsc_ref):
        i = pl.program_id(0)
        xv = x_ref[...]
        hv = halo_ref[...]
        nwv = nw_ref[...]
        r = _rstd(xv)
        xh = xv * r
        h_ext = jnp.concatenate([jnp.where(i == 0, 0.0, hv * _rstd(hv) * nwv), xh * nwv], axis=0)
        invc = _pool_counts(i)
        dyv = dx2_ref[...]
        dout_ext = jnp.concatenate([dyv, jnp.where(i == nt - 1, 0.0, dnext_ref[...])], axis=0) * sc_ref[...]
        dh_parts = []
        dsc_parts = []
        for g in range(4):
            gs = slice(g * PG, (g + 1) * PG)
            pooled_b = _c(_pooled(h_ext, invc, g))
            dout_b = _c(dout_ext[:, gs])
            dsc_parts.append(jnp.sum(dyv[:, gs] * _dot(pooled_b, pw_ref[g]), axis=0, keepdims=True))
            _accum(dpw_ref.at[g], _dot_tn(pooled_b, dout_b[0:TM, :]), i)
            dpool_ext = _dot_nt(dout_b, pw_ref[g])
            lead = _window_sum(dpool_ext * invc[g], POOL_WINDOWS[g], False)[0:TM, :]
            dh_parts.append(lead - dpool_ext[0:TM, :])
        _accum(dsc_ref, jnp.concatenate(dsc_parts, axis=-1), i)
        dx, dnw = _rms_bwd(jnp.concatenate(dh_parts, axis=-1), xh, r, nwv)
        dx_ref[...] = dyv + dx
        _accum(dnw_ref, dnw, i)

    vec = pl.BlockSpec((1, D), lambda i: (0, 0))
    return pl.pallas_call(
        body,
        grid=(nt,),
        in_specs=[pl.BlockSpec((TM, D), lambda i: (i, 0)), _halo_prev_spec(D, HALO), vec,
                  pl.BlockSpec((4, PG, PG), lambda i: (0, 0, 0)), vec,
                  pl.BlockSpec((TM, D), lambda i: (i, 0)), _halo_next_spec(D, HALO, t)],
        out_specs=[pl.BlockSpec((TM, D), lambda i: (i, 0)), vec, pl.BlockSpec((4, PG, PG), lambda i: (0, 0, 0)), vec],
        out_shape=[jax.ShapeDtypeStruct((t, D), F32), jax.ShapeDtypeStruct((1, D), F32),
                   jax.ShapeDtypeStruct((4, PG, PG), F32), jax.ShapeDtypeStruct((1, D), F32)],
        name="pool_bwd",
    )(x1, x1, nw, pw, scale, dx2, dx2)


def _adamw(lands, w, m, v, rb, name):
    nl, nr = w.shape[0], w.shape[1]
    rest = w.shape[2:]
    ns = lands[0].shape[0]
    zeros = (0,) * len(rest)

    def body(*refs):
        l_refs = refs[0:nl]
        w_ref, m_ref, v_ref, g_ref, d_ref, m2_ref, v2_ref = refs[nl:]
        for l in range(nl):
            g = l_refs[l][0].astype(F32)
            for s in range(1, ns):
                g = g + l_refs[l][s].astype(F32)
            m2 = ADAM_B1 * m_ref[l] + (1.0 - ADAM_B1) * g
            v2 = ADAM_B2 * v_ref[l] + (1.0 - ADAM_B2) * (g * g)
            m_hat = m2 / (1.0 - ADAM_B1 ** ADAM_STEP)
            v_hat = v2 / (1.0 - ADAM_B2 ** ADAM_STEP)
            g_ref[l] = g
            d_ref[l] = -ADAM_LR * (m_hat / (jnp.sqrt(v_hat) + ADAM_EPS) + ADAM_WD * w_ref[l])
            m2_ref[l] = m2
            v2_ref[l] = v2

    lspec = pl.BlockSpec((ns, rb) + rest, lambda r: (0, r) + zeros)
    wspec = pl.BlockSpec((nl, rb) + rest, lambda r: (0, r) + zeros)
    return pl.pallas_call(
        body,
        grid=(nr // rb,),
        in_specs=[lspec] * nl + [wspec] * 3,
        out_specs=[wspec] * 4,
        out_shape=[jax.ShapeDtypeStruct(w.shape, F32)] * 4,
        name=name,
    )(*lands, w, m, v)


WEIGHT_ORDER = ("ffn_norm1", "ffn1_w_in", "ffn1_w_out", "mix_norm", "ffn_norm2", "ffn2_w_in", "ffn2_w_out", "ab_w_in",
                "dn_conv_w", "dn_a_log", "dn_dt_bias", "dn_out_norm", "sg_norm", "sg_w", "sg_b", "ab_w_out", "pool_w",
                "pool_scale", "final_norm")
R_SMALL = 88
SMALL_ROWS = (
    ("ffn_norm1", (2, D), 2), ("mix_norm", (2, D), 2), ("ffn_norm2", (2, D), 2), ("final_norm", (D,), 1),
    ("sg_w", (1, 4, SGC, SGC), 64), ("sg_norm", (1, 4, HD), 1), ("sg_b", (1, 4, SGC), 1), ("dn_out_norm", (1, HD), 1),
    ("dn_a_log", (1, 4), 1), ("dn_dt_bias", (1, 4), 1), ("pool_scale", (1, D), 1), ("dn_conv_w", (1, 4, QKV), 8),
)
SMALL_SHARDED = ("pool_scale", "dn_conv_w")


def _rows_of(a, rows):
    if a.shape[-1] == QKV:
        return jnp.pad(a.reshape(4, QKV), ((0, 0), (0, 2 * ROW - QKV))).reshape(8, ROW)
    n = _numel(a.shape)
    if n % ROW == 0:
        return a.reshape(n // ROW, ROW)
    return jnp.pad(a.reshape(1, n), ((0, 0), (0, ROW - n)))


def _from_rows(r, shape):
    if shape[-1] == QKV:
        return r.reshape(4, 2 * ROW)[:, 0:QKV].reshape(shape)
    n = _numel(shape)
    if n % ROW == 0:
        return r.reshape(shape)
    return r[:, 0:n].reshape(shape)


def _pack_small(vals):
    parts = [(_rows_of(vals[n].astype(F32), r) if n in vals else jnp.zeros((r, ROW), F32)) for n, _, r in SMALL_ROWS]
    used = sum(r for _, _, r in SMALL_ROWS)
    return jnp.concatenate(parts + [jnp.zeros((R_SMALL - used, ROW), F32)], axis=0)


def _unpack_small(packed):
    out, o = {}, 0
    for n, shape, r in SMALL_ROWS:
        out[n] = _from_rows(packed[o:o + r], shape)
        o += r
    return out


def _pack_small_shard(ps, cw):
    return jnp.concatenate([
        jnp.pad(ps, ((0, 0), (0, ROW - D // N_DEV))), jnp.pad(cw[0], ((0, 0), (0, ROW - QKV // N_DEV))),
        jnp.zeros((3, ROW), F32)], axis=0)


def _mixer_weights(g_in, g_out, g_small, small):
    w = {}
    wi = jnp.transpose(g_in, (1, 0, 2)).reshape(D, AB_IN)
    main = [wi[:, 0:2048], wi[:, 2056:AB_IN]]
    w["wab_f"] = jnp.concatenate(
        main + [jnp.repeat(wi[:, 2048:2052], HD, axis=1), jnp.repeat(wi[:, 2052:2056], HD, axis=1)], axis=1)
    w["wab_b"] = jnp.concatenate(main + [wi[:, 2048:2056], jnp.zeros((D, 120), wi.dtype)], axis=1)
    w["cw"] = jnp.transpose(g_small[:, 1:5, 0:QKV // N_DEV], (1, 0, 2)).reshape(4, QKV)
    w["ps"] = g_small[:, 0, 0:D // N_DEV].reshape(1, D)
    w["alog"] = jnp.repeat(small["dn_a_log"][0].astype(F32), HD).reshape(1, HW)
    w["dtb"] = jnp.repeat(small["dn_dt_bias"][0].astype(F32), HD).reshape(1, HW)
    w["dnw"] = jnp.tile(small["dn_out_norm"][0].astype(F32), HEADS).reshape(1, HW)
    w["sgnw"] = small["sg_norm"][0].astype(F32).reshape(1, HW)
    tri = jnp.tril(jnp.ones((SGC, SGC), dtype=bool))
    wt = jnp.where(tri, small["sg_w"][0].astype(F32), 0.0)
    w["wtril"] = _c(wt)
    w["wtril_t"] = _c(jnp.transpose(wt, (0, 2, 1)))
    w["sgb"] = jnp.repeat(jnp.transpose(small["sg_b"][0].astype(F32)), HD, axis=1)
    w["wout_ab"] = g_out.reshape(D, D)
    return w


def kernel(x, ffn_norm1, ffn1_w_in, ffn1_w_out, mix_norm, ffn_norm2, ffn2_w_in, ffn2_w_out, ab_w_in, dn_conv_w, dn_a_log, dn_dt_bias, dn_out_norm, sg_norm, sg_w, sg_b, ab_w_out, pool_w, pool_scale, final_norm, loss_target, m_ffn_norm1, m_ffn1_w_in, m_ffn1_w_out, m_mix_norm, m_ffn_norm2, m_ffn2_w_in, m_ffn2_w_out, m_ab_w_in, m_dn_conv_w, m_dn_a_log, m_dn_dt_bias, m_dn_out_norm, m_sg_norm, m_sg_w, m_sg_b, m_ab_w_out, m_pool_w, m_pool_scale, m_final_norm, v_ffn_norm1, v_ffn1_w_in, v_ffn1_w_out, v_mix_norm, v_ffn_norm2, v_ffn2_w_in, v_ffn2_w_out, v_ab_w_in, v_dn_conv_w, v_dn_a_log, v_dn_dt_bias, v_dn_out_norm, v_sg_norm, v_sg_w, v_sg_b, v_ab_w_out, v_pool_w, v_pool_scale, v_final_norm):
    wl = dict(ffn_norm1=ffn_norm1, mix_norm=mix_norm, ffn_norm2=ffn_norm2, dn_a_log=dn_a_log, dn_dt_bias=dn_dt_bias,
              dn_out_norm=dn_out_norm, sg_norm=sg_norm, sg_w=sg_w, sg_b=sg_b, final_norm=final_norm)
    ml = dict(ffn_norm1=m_ffn_norm1, mix_norm=m_mix_norm, ffn_norm2=m_ffn_norm2, dn_a_log=m_dn_a_log,
              dn_dt_bias=m_dn_dt_bias, dn_out_norm=m_dn_out_norm, sg_norm=m_sg_norm, sg_w=m_sg_w, sg_b=m_sg_b,
              final_norm=m_final_norm)
    vl = dict(ffn_norm1=v_ffn_norm1, mix_norm=v_mix_norm, ffn_norm2=v_ffn_norm2, dn_a_log=v_dn_a_log,
              dn_dt_bias=v_dn_dt_bias, dn_out_norm=v_dn_out_norm, sg_norm=v_sg_norm, sg_w=v_sg_w, sg_b=v_sg_b,
              final_norm=v_final_norm)
    row = lambda a: a.reshape(1, -1).astype(F32)
    n1 = [row(ffn_norm1[l]) for l in range(2)]
    n2 = [row(ffn_norm2[l]) for l in range(2)]
    mix = [row(mix_norm[l]) for l in range(2)]
    s_in = {(f, l): _c(wf[l]) for f, wf in enumerate((ffn1_w_in, ffn2_w_in)) for l in range(2)}
    s_out = {(f, l): _c(wf[l]) for f, wf in enumerate((ffn1_w_out, ffn2_w_out)) for l in range(2)}
    xs, tgt = x[0], loss_target[0]

    wi00, wo00 = _comm_call(_Comm("gather", [s_in[0, 0], s_out[0, 0]]), "gather_first")
    x01, gu00, (g_abin, g_about, g_small) = _ffn_fwd(
        xs, n1[0], wi00, wo00,
        comm=_Comm("gather", [_c(ab_w_in[0]), _c(ab_w_out[0]), _pack_small_shard(pool_scale, dn_conv_w)]))
    w = _mixer_weights(g_abin, g_about, g_small, wl)
    h, qkv, z, su, sv, b_rep, a_rep = _ab_proj(x01, mix[0], w["wab_f"])
    qn, kn, v, beta, g = _dn_pre(qkv, b_rep, a_rep, w["cw"], w["alog"], w["dtb"])
    (o, sall), rest = _dn_fwd(
        qn, kn, v, beta, g,
        comm=_Comm("gather", [s_in[1, 0], s_out[1, 0], s_in[0, 1], s_out[0, 1], s_in[1, 1], s_out[1, 1], _c(pool_w[0])]))
    wi10, wo10, wi01, wo01, wi11, wo11, g_pw = rest
    pw = jnp.transpose(g_pw, (1, 0, 2, 3)).reshape(4, PG, PG)
    x02, cat = _ab_out(x01, o, z, su, sv, w["dnw"], w["sgnw"], w["wtril"], w["sgb"], w["wout_ab"])
    x10, gu10, _ = _ffn_fwd(x02, n2[0], wi10, wo10)
    x11, gu01, _ = _ffn_fwd(x10, n1[1], wi01, wo01)
    x12 = _pool_fwd(x11, mix[1], pw, w["ps"])
    x13, gu11, _ = _ffn_fwd(x12, n2[1], wi11, wo11)
    loss_local, dx, d_fn = _loss_head(x13, row(final_norm), tgt)

    bt = min(BT, xs.shape[0])

    def ffn_b(xin, nw, w_in, w_out, gu, dy, comm=None):
        (dxn, xn, act, dh, dnw), landed = _ffn_bwd(xin, nw, w_in, w_out, gu, dy, comm)
        return dxn, dnw, [_mm_tn_win(xn, dh), _mm_tn_wout(act, dy)], landed

    dx, d_n2_1, g11, _ = ffn_b(x12, n2[1], wi11, wo11, gu11, dx)
    dx, d_mix_1, d_pw, d_ps = _pool_bwd(x11, mix[1], pw, w["ps"], dx)
    d_pw_sh = _c(jnp.transpose(d_pw.reshape(4, N_DEV, PG // N_DEV, PG), (1, 0, 2, 3)))
    dx, d_n1_1, g01, land11 = ffn_b(x10, n1[1], wi01, wo01, gu01, dx, _Comm("exchange", g11))
    dx, d_n2_0, g10, land01 = ffn_b(x02, n2[0], wi10, wo10, gu10, dx, _Comm("exchange", g01 + [d_pw_sh]))
    do, dz, dsu, dsv, d_dnw, d_sgnw, d_sgw, d_sgb = _ab_out_bwd(
        dx, o, z, su, sv, w["dnw"], w["sgnw"], w["wtril"], w["wtril_t"], w["sgb"], w["wout_ab"])
    d_about = _mm_tn(cat, dx, D, D, bt, _MM, "mm_tn_about").reshape(N_DEV, D // N_DEV, D)
    (dqn, dkn, dv, dbeta, dg), land10 = _dn_bwd(qn, kn, v, beta, g, sall, do, _Comm("exchange", g10))
    dc, dba, d_cw, d_alog, d_dtb = _dn_pre_bwd(qkv, b_rep, a_rep, w["cw"], w["alog"], w["dtb"], dqn, dkn, dv, dbeta, dg)
    dqkv = _conv_bwd(dc, w["cw"])
    dx, dcat, d_mix_0 = _ab_proj_bwd(x01, mix[0], dqkv, dz, dsu, dsv, dba, w["wab_b"], dx)
    d_wab = _mm_tn(h, dcat, D, 640, bt, _MM, "mm_tn_abin")
    d_abin = jnp.concatenate([d_wab[:, 0:2048], d_wab[:, 3072:3080], d_wab[:, 2048:3072]], axis=1)
    d_abin_sh = jnp.transpose(d_abin.reshape(D, N_DEV, AB_IN // N_DEV), (1, 0, 2))
    grad_x, d_n1_0, g00, land_ab = ffn_b(xs, n1[0], wi00, wo00, gu00, dx, _Comm("exchange", [d_abin_sh, d_about]))

    g_small = {
        "ffn_norm1": jnp.concatenate([d_n1_0, d_n1_1], axis=0),
        "mix_norm": jnp.concatenate([d_mix_0, d_mix_1], axis=0),
        "ffn_norm2": jnp.concatenate([d_n2_0, d_n2_1], axis=0),
        "dn_conv_w": d_cw.reshape(1, 4, QKV),
        "dn_a_log": d_alog[:, ::HD],
        "dn_dt_bias": d_dtb[:, ::HD],
        "dn_out_norm": d_dnw,
        "sg_norm": d_sgnw.reshape(1, HEADS, HD),
        "sg_w": d_sgw[None],
        "sg_b": jnp.transpose(d_sgb[:, 0:HEADS])[None],
        "pool_scale": d_ps,
        "final_norm": d_fn.reshape(D),
    }
    land00_in, land00_out, land_small = _comm_call(_Comm("exchange", g00, repl=[_pack_small(g_small)]), "exchange_last")

    res = {}
    res["ffn1_w_in"] = _adamw([land00_in, land01[0]], ffn1_w_in, m_ffn1_w_in, v_ffn1_w_in, 128, "adamw_w_in")
    res["ffn2_w_in"] = _adamw([land10[0], land11[0]], ffn2_w_in, m_ffn2_w_in, v_ffn2_w_in, 128, "adamw_w_in")
    res["ffn1_w_out"] = _adamw([land00_out, land01[1]], ffn1_w_out, m_ffn1_w_out, v_ffn1_w_out, 176, "adamw_w_out")
    res["ffn2_w_out"] = _adamw([land10[1], land11[1]], ffn2_w_out, m_ffn2_w_out, v_ffn2_w_out, 176, "adamw_w_out")
    res["ab_w_in"] = _adamw([land_ab[0]], ab_w_in, m_ab_w_in, v_ab_w_in, 256, "adamw_ab_w_in")
    res["ab_w_out"] = _adamw([land_ab[1]], ab_w_out, m_ab_w_out, v_ab_w_out, D // N_DEV, "adamw_ab_w_out")
    res["pool_w"] = _adamw([land01[2]], pool_w, m_pool_w, v_pool_w, 4, "adamw_pool_w")
    sm = _adamw([land_small], _pack_small(wl)[None], _pack_small(ml)[None], _pack_small(vl)[None], R_SMALL,
                "adamw_replicated")
    sm = [_unpack_small(a[0]) for a in sm]
    for n in wl:
        res[n] = [d[n] for d in sm]
    me = 4 * lax.axis_index("x") + 2 * lax.axis_index("y") + lax.axis_index("c")
    g_ps = lax.dynamic_slice(sm[0]["pool_scale"], (0, me * (D // N_DEV)), (1, D // N_DEV))
    g_cw = lax.dynamic_slice(sm[0]["dn_conv_w"], (0, 0, me * (QKV // N_DEV)), (1, 4, QKV // N_DEV))
    s2 = _adamw([_pack_small_shard(g_ps, g_cw)[None]], _pack_small_shard(pool_scale, dn_conv_w)[None],
                _pack_small_shard(m_pool_scale, m_dn_conv_w)[None], _pack_small_shard(v_pool_scale, v_dn_conv_w)[None],
                8, "adamw_small_sharded")
    res["pool_scale"] = [a[0, 0:1, 0:D // N_DEV] for a in s2]
    res["dn_conv_w"] = [a[0, 1:5, 0:QKV // N_DEV][None] for a in s2]

    loss = lax.psum(loss_local[0, 0], ("x", "y", "c"))
    result = [loss, grad_x[None]]
    for i in range(4):
        result += [res[n][i] for n in WEIGHT_ORDER]
    return tuple(result)
```

```python
import jax
import jax.numpy as jnp
from jax import lax
from jax.experimental import pallas as pl
from jax.experimental.pallas import tpu as pltpu

F32 = jnp.float32
_MM = jnp.bfloat16

D = 1024
FF = 2816
EPS = 1e-6
HEADS = 4
HD = 128
DNC = 64
DN_STEP = 2
SGC = 128
QKV = 3 * HEADS * HD
HW = HEADS * HD
POOL_WINDOWS = (2, 4, 8, 16)
PG = D // 4
HALO = 16
N_DEV = 8
AB_IN = 3080
ROW = 1024

TM = 512
BT = 1024
FC = 704
NJ = FF // FC
WO_ROWS = FF // N_DEV

ADAM_LR, ADAM_B1, ADAM_B2, ADAM_EPS, ADAM_WD, ADAM_STEP = 0.001, 0.9, 0.999, 1e-08, 0.01, 10

MESH_T = pl.DeviceIdType.MESH
NN = (((1,), (0,)), ((), ()))
NT = (((1,), (1,)), ((), ()))
TN = (((0,), (0,)), ((), ()))


def _c(a):
    return a.astype(_MM)


def _dg(a, b, dims):
    return lax.dot_general(a, b, dims, preferred_element_type=F32)


def _dot(a, b):
    return _dg(a, b, NN)


def _dot_nt(a, b):
    return _dg(a, b, NT)


def _dot_tn(a, b):
    return _dg(a, b, TN)


def _split2(a):
    hi = _c(a)
    return hi, _c(a - hi.astype(F32))


def _dot3(a, b, dims=NN):
    ah, al = _split2(a)
    bh, bl = _split2(b)
    return _dg(ah, bh, dims) + (_dg(ah, bl, dims) + _dg(al, bh, dims))


def _mask_dot(mask, x):
    x1 = _c(x)
    r = x - x1.astype(F32)
    x2 = _c(r)
    x3 = _c(r - x2.astype(F32))
    return _dot(mask, x1) + (_dot(mask, x2) + _dot(mask, x3))


def _sigmoid(x):
    return jax.nn.sigmoid(x)


def _gelu(x):
    return 0.5 * x * (1.0 + lax.erf(x * 0.7071067811865476))


def _gelu_grad(x):
    return 0.5 * (1.0 + lax.erf(x * 0.7071067811865476)) + x * jnp.exp(-0.5 * x * x) * 0.3989422804014327


def _accum(ref, val, step):
    @pl.when(step == 0)
    def _():
        ref[...] = val

    @pl.when(step > 0)
    def _():
        ref[...] += val


def _rstd(x):
    return lax.rsqrt(jnp.mean(x * x, axis=-1, keepdims=True) + EPS)


def _rms_bwd(dy, xhat, r, nw):
    dnw = jnp.sum(dy * xhat, axis=0, keepdims=True)
    dxh = dy * nw
    dx = r * (dxh - xhat * jnp.mean(dxh * xhat, axis=-1, keepdims=True))
    return dx, dnw


def _numel(shape):
    n = 1
    for s in shape:
        n *= s
    return n


def _peer(k, x, y, c):
    px = 1 - x if k & 4 else x
    py = 1 - y if k & 2 else y
    pc = 1 - c if k & 1 else c
    return px, py, pc


class _Comm:
    def __init__(self, kind, arrs, repl=()):
        self.kind = kind
        self.ns = len(arrs)
        self.arrs = list(arrs) + list(repl)
        self.na = len(self.arrs)

    @property
    def out_shape(self):
        out = []
        for i, a in enumerate(self.arrs):
            lead = (N_DEV,) if (self.kind == "gather" or i >= self.ns) else ()
            out.append(jax.ShapeDtypeStruct(lead + a.shape, a.dtype))
        return out

    @property
    def scratch(self):
        return [pltpu.SemaphoreType.DMA((7 * self.na,)), pltpu.SemaphoreType.DMA((7 * self.na,)),
                pltpu.SemaphoreType.DMA((self.na,))]

    def phases(self, ins, outs, sems):
        send_sems, recv_sems, local_sems = sems
        na = self.na
        x, y, c = lax.axis_index("x"), lax.axis_index("y"), lax.axis_index("c")
        if self.kind == "gather":
            me, sibling = (x, y, c), (x, y, 1 - c)
            chips = [(1 - x, y), (x, 1 - y), (1 - x, 1 - y)]

            def slot(a, px, py, pc):
                return outs[a].at[4 * px + 2 * py + pc]

            def copy(a, k, block, to, src=None):
                return pltpu.make_async_remote_copy(
                    src_ref=slot(a, *block) if src is None else src, dst_ref=slot(a, *block),
                    send_sem=send_sems.at[7 * a + k], recv_sem=recv_sems.at[7 * a + k],
                    device_id=to, device_id_type=MESH_T)

            mine = [pltpu.make_async_copy(ins[a], slot(a, *me), local_sems.at[a]) for a in range(na)]
            first, passed = [], []
            for a in range(na):
                first.append(copy(a, 0, me, sibling, src=ins[a]))
                first += [copy(a, 1 + j, me, (*chip, c), src=ins[a]) for j, chip in enumerate(chips)]
                passed += [copy(a, 4 + j, (*chip, c), sibling) for j, chip in enumerate(chips)]

            def start():
                for cp in mine + first:
                    cp.start()

            def middle():
                for a in range(na):
                    for j, chip in enumerate(chips):
                        copy(a, 1 + j, (*chip, c), me).wait_recv()
                        passed[3 * a + j].start()

            def finish():
                for a in range(na):
                    copy(a, 0, sibling, me).wait_recv()
                    for j, chip in enumerate(chips):
                        copy(a, 4 + j, (*chip, 1 - c), me).wait_recv()
                for cp in first + passed:
                    cp.wait_send()
                for cp in mine:
                    cp.wait()

            return start, middle, finish

        me = 4 * x + 2 * y + c
        ns = self.ns
        own = [pltpu.make_async_copy(ins[a].at[me] if a < ns else ins[a], outs[a].at[me], local_sems.at[a])
               for a in range(na)]
        copies = []
        for k in range(1, N_DEV):
            px, py, pc = _peer(k, x, y, c)
            peer = 4 * px + 2 * py + pc
            for a in range(na):
                copies.append(pltpu.make_async_remote_copy(
                    src_ref=ins[a].at[peer] if a < ns else ins[a], dst_ref=outs[a].at[me],
                    send_sem=send_sems.at[na * (k - 1) + a], recv_sem=recv_sems.at[na * (k - 1) + a],
                    device_id=(px, py, pc), device_id_type=MESH_T))

        def start():
            for cp in own + copies:
                cp.start()

        def middle():
            pass

        def finish():
            for cp in copies:
                cp.wait()
            for cp in own:
                cp.wait()

        return start, middle, finish


def _comm_call(comm, name):
    na = comm.na

    def body(*refs):
        start, middle, finish = comm.phases(refs[0:na], refs[na:2 * na], refs[2 * na:])
        start()
        middle()
        finish()

    hbm = pl.BlockSpec(memory_space=pltpu.HBM)
    return pl.pallas_call(
        body, out_shape=comm.out_shape, in_specs=[hbm] * na, out_specs=[hbm] * na, scratch_shapes=comm.scratch,
        name=name)(*comm.arrs)


def _carried_call(body, comm, n_in, n_out, n_scr, when, *, grid, in_specs, out_specs, out_shape, scratch_shapes,
                  operands, name):
    if comm is None:
        return pl.pallas_call(body, grid=grid, in_specs=in_specs, out_specs=out_specs, out_shape=out_shape,
                              scratch_shapes=scratch_shapes, name=name)(*operands), []
    na = comm.na

    def both(*refs):
        a = n_in + na
        b = a + n_out + na
        body(*refs[0:n_in], *refs[a:a + n_out], *refs[b:b + n_scr])
        start, middle, finish = comm.phases(refs[n_in:a], refs[a + n_out:b], refs[b + n_scr:])
        first, mid, last = when()
        pl.when(first)(start)
        pl.when(mid)(middle)
        pl.when(last)(finish)

    hbm = pl.BlockSpec(memory_space=pltpu.HBM)
    res = pl.pallas_call(
        both, grid=grid, in_specs=list(in_specs) + [hbm] * na, out_specs=list(out_specs) + [hbm] * na,
        out_shape=list(out_shape) + comm.out_shape, scratch_shapes=list(scratch_shapes) + comm.scratch,
        name=name)(*operands, *comm.arrs)
    return res[0:n_out], res[n_out:]


def _ffn_w_specs():
    return [
        pl.BlockSpec((None, D, FC), lambda i, j: (j, 0, 0)),
        pl.BlockSpec((None, D, FC), lambda i, j: (j + NJ, 0, 0)),
        pl.BlockSpec((2, WO_ROWS, D), lambda i, j: (j, 0, 0)),
    ]


def _ffn_when(nt):
    def when():
        i, j = pl.program_id(0), pl.program_id(1)
        return ((i == 0) & (j == 0), (i == nt // 2) & (j == 0), (i == nt - 1) & (j == NJ - 1))
    return when


def _ffn_fwd(x, nw, w_in, w_out, comm=None):
    t = x.shape[0]
    nt = t // TM

    def body(x_ref, nw_ref, wg_ref, wu_ref, wo3_ref, o_ref, gu_ref, xn_sc, acc_sc):
        j = pl.program_id(1)

        @pl.when(j == 0)
        def _():
            xv = x_ref[...]
            xn_sc[...] = _c(xv * _rstd(xv) * nw_ref[...])
            acc_sc[...] = jnp.zeros_like(acc_sc)

        xn = xn_sc[...]
        g = _dot(xn, wg_ref[...])
        u = _dot(xn, wu_ref[...])
        gu_ref[0] = _c(g)
        gu_ref[1] = _c(u)
        acc_sc[...] += _dot(_c(g * _sigmoid(g) * u), wo3_ref[...].reshape(FC, D))

        @pl.when(j == NJ - 1)
        def _():
            o_ref[...] = x_ref[...] + 0.5 * acc_sc[...]

    (out, gu), landed = _carried_call(
        body, comm, 5, 2, 2, _ffn_when(nt),
        grid=(nt, NJ),
        in_specs=[pl.BlockSpec((TM, D), lambda i, j: (i, 0)), pl.BlockSpec((1, D), lambda i, j: (0, 0))]
        + _ffn_w_specs(),
        out_specs=[pl.BlockSpec((TM, D), lambda i, j: (i, 0)),
                   pl.BlockSpec((None, 2, TM, FC), lambda i, j: (j, 0, i, 0))],
        out_shape=[jax.ShapeDtypeStruct((t, D), F32), jax.ShapeDtypeStruct((NJ, 2, t, FC), _MM)],
        scratch_shapes=[pltpu.VMEM((TM, D), _MM), pltpu.VMEM((TM, D), F32)],
        operands=(x, nw, w_in, w_in, w_out), name="ffn_fwd")
    return out, gu, landed


def _ffn_bwd(x, nw, w_in, w_out, gu, dy, comm=None):
    t = x.shape[0]
    nt = t // TM

    def body(x_ref, nw_ref, wg_ref, wu_ref, wo3_ref, gu_ref, dy_ref, dx_ref, xn_ref, a_ref, dh_ref, dnw_ref,
             r_sc, dyb_sc, acc_sc):
        wo = wo3_ref[...].reshape(FC, D)
        i = pl.program_id(0)
        j = pl.program_id(1)

        @pl.when(j == 0)
        def _():
            xv = x_ref[...]
            r = _rstd(xv)
            r_sc[...] = r
            xn_ref[...] = _c(xv * r * nw_ref[...])
            dyb_sc[...] = _c(0.5 * dy_ref[...])
            acc_sc[...] = jnp.zeros_like(acc_sc)

        g = gu_ref[0].astype(F32)
        u = gu_ref[1].astype(F32)
        s = _sigmoid(g)
        sl = g * s
        a_ref[...] = _c(sl * u)
        da = _dot_nt(dyb_sc[...], wo)
        dg = _c(da * u * (s * (1.0 + g * (1.0 - s))))
        du = _c(da * sl)
        dh_ref[0] = dg
        dh_ref[1] = du
        acc_sc[...] += _dot_nt(dg, wg_ref[...]) + _dot_nt(du, wu_ref[...])

        @pl.when(j == NJ - 1)
        def _():
            r = r_sc[...]
            dx, dnw = _rms_bwd(acc_sc[...], x_ref[...] * r, r, nw_ref[...])
            dx_ref[...] = dy_ref[...] + dx
            _accum(dnw_ref, dnw, i)

    return _carried_call(
        body, comm, 7, 5, 3, _ffn_when(nt),
        grid=(nt, NJ),
        in_specs=[pl.BlockSpec((TM, D), lambda i, j: (i, 0)), pl.BlockSpec((1, D), lambda i, j: (0, 0))]
        + _ffn_w_specs() + [pl.BlockSpec((None, 2, TM, FC), lambda i, j: (j, 0, i, 0)),
                            pl.BlockSpec((TM, D), lambda i, j: (i, 0))],
        out_specs=[
            pl.BlockSpec((TM, D), lambda i, j: (i, 0)),
            pl.BlockSpec((TM, D), lambda i, j: (i, 0)),
            pl.BlockSpec((None, TM, FC), lambda i, j: (j, i, 0)),
            pl.BlockSpec((None, 2, TM, FC), lambda i, j: (j, 0, i, 0)),
            pl.BlockSpec((1, D), lambda i, j: (0, 0)),
        ],
        out_shape=[
            jax.ShapeDtypeStruct((t, D), F32),
            jax.ShapeDtypeStruct((t, D), _MM),
            jax.ShapeDtypeStruct((NJ, t, FC), _MM),
            jax.ShapeDtypeStruct((NJ, 2, t, FC), _MM),
            jax.ShapeDtypeStruct((1, D), F32),
        ],
        scratch_shapes=[pltpu.VMEM((TM, 1), F32), pltpu.VMEM((TM, D), _MM), pltpu.VMEM((TM, D), F32)],
        operands=(x, nw, w_in, w_in, w_out, gu, dy), name="ffn_bwd")


def _mm_tn(a, b, bm, bn, bt, out_dtype, name):
    t, m = a.shape
    n = b.shape[1]
    nt = t // bt

    def body(a_ref, b_ref, o_ref, acc_sc):
        k = pl.program_id(2)
        _accum(acc_sc, _dot_tn(_c(a_ref[...]), _c(b_ref[...])), k)

        @pl.when(k == nt - 1)
        def _():
            o_ref[...] = acc_sc[...].astype(out_dtype)

    return pl.pallas_call(
        body,
        grid=(m // bm, n // bn, nt),
        in_specs=[pl.BlockSpec((bt, bm), lambda i, j, k: (k, i)), pl.BlockSpec((bt, bn), lambda i, j, k: (k, j))],
        out_specs=pl.BlockSpec((bm, bn), lambda i, j, k: (i, j)),
        out_shape=jax.ShapeDtypeStruct((m, n), out_dtype),
        scratch_shapes=[pltpu.VMEM((bm, bn), F32)],
        name=name,
    )(a, b)


def _mm_tn_win(xn, dh):
    t = xn.shape[0]
    bt = min(BT, t)
    nt = t // bt

    def body(a_ref, b_ref, o_ref, acc_sc):
        k = pl.program_id(2)
        _accum(acc_sc, _dot_tn(a_ref[...], b_ref[...]), k)

        @pl.when(k == nt - 1)
        def _():
            o_ref[...] = _c(acc_sc[...])

    return pl.pallas_call(
        body,
        grid=(2, NJ, nt),
        in_specs=[pl.BlockSpec((bt, D), lambda h, j, k: (k, 0)),
                  pl.BlockSpec((None, None, bt, FC), lambda h, j, k: (j, h, k, 0))],
        out_specs=pl.BlockSpec((None, D, FC), lambda h, j, k: (h * NJ + j, 0, 0)),
        out_shape=jax.ShapeDtypeStruct((N_DEV, D, FC), _MM),
        scratch_shapes=[pltpu.VMEM((D, FC), F32)],
        name="mm_tn_win",
    )(xn, dh)


def _mm_tn_wout(act, dy):
    t = dy.shape[0]
    bt = min(BT, t)
    nt = t // bt

    def body(a_ref, b_ref, o_ref, acc_sc):
        k = pl.program_id(1)
        _accum(acc_sc, _dot_tn(a_ref[...], _c(b_ref[...])), k)

        @pl.when(k == nt - 1)
        def _():
            o_ref[...] = _c((0.5 * acc_sc[...]).reshape(2, WO_ROWS, D))

    return pl.pallas_call(
        body,
        grid=(NJ, nt),
        in_specs=[pl.BlockSpec((None, bt, FC), lambda j, k: (j, k, 0)), pl.BlockSpec((bt, D), lambda j, k: (k, 0))],
        out_specs=pl.BlockSpec((2, WO_ROWS, D), lambda j, k: (j, 0, 0)),
        out_shape=jax.ShapeDtypeStruct((N_DEV, WO_ROWS, D), _MM),
        scratch_shapes=[pltpu.VMEM((FC, D), F32)],
        name="mm_tn_wout",
    )(act, dy)


def _loss_head(x, nw, tgt):
    t = x.shape[0]

    def body(x_ref, nw_ref, t_ref, loss_ref, dx_ref, dnw_ref):
        i = pl.program_id(0)
        xv = x_ref[...]
        r = _rstd(xv)
        xh = xv * r
        e = xh * nw_ref[...] - t_ref[...]
        part = 0.5 * jnp.sum(jnp.mean(e * e, axis=-1, keepdims=True), axis=0, keepdims=True)
        _accum(loss_ref, jnp.broadcast_to(part, (1, 128)), i)
        dx, dnw = _rms_bwd(e * (1.0 / D), xh, r, nw_ref[...])
        dx_ref[...] = dx
        _accum(dnw_ref, dnw, i)

    return pl.pallas_call(
        body,
        grid=(t // TM,),
        in_specs=[pl.BlockSpec((TM, D), lambda i: (i, 0)), pl.BlockSpec((1, D), lambda i: (0, 0)),
                  pl.BlockSpec((TM, D), lambda i: (i, 0))],
        out_specs=[pl.BlockSpec((1, 128), lambda i: (0, 0)), pl.BlockSpec((TM, D), lambda i: (i, 0)),
                   pl.BlockSpec((1, D), lambda i: (0, 0))],
        out_shape=[jax.ShapeDtypeStruct((1, 128), F32), jax.ShapeDtypeStruct((t, D), F32),
                   jax.ShapeDtypeStruct((1, D), F32)],
        name="loss_head",
    )(x, nw, tgt)


PW_F = QKV + 5 * HW
PW_B = QKV + 3 * HW + 128


def _ab_proj(x1, nw, wab):
    t = x1.shape[0]

    def body(x_ref, nw_ref, w_ref, h_ref, qkv_ref, z_ref, su_ref, sv_ref, b_ref, a_ref):
        xv = x_ref[...]
        h = _c(xv * _rstd(xv) * nw_ref[...])
        h_ref[...] = h
        p = _dot(h, w_ref[...])
        qkv_ref[...] = p[:, 0:QKV]
        o = QKV
        for ref in (z_ref, su_ref, sv_ref, b_ref, a_ref):
            ref[...] = p[:, o:o + HW]
            o += HW

    row = lambda w: pl.BlockSpec((TM, w), lambda i: (i, 0))
    return pl.pallas_call(
        body,
        grid=(t // TM,),
        in_specs=[row(D), pl.BlockSpec((1, D), lambda i: (0, 0)), pl.BlockSpec((D, PW_F), lambda i: (0, 0))],
        out_specs=[row(D), row(QKV)] + [row(HW)] * 5,
        out_shape=[jax.ShapeDtypeStruct((t, D), _MM), jax.ShapeDtypeStruct((t, QKV), F32)]
        + [jax.ShapeDtypeStruct((t, HW), F32)] * 5,
        name="ab_proj",
    )(x1, nw, wab)


def _conv_rows(x, halo, cw):
    xe = jnp.concatenate([halo, x], axis=0)
    shifted = []
    c = None
    for k in range(4):
        s = 3 - k
        xs = (xe if s == 0 else pltpu.roll(xe, s, 0))[8:, :]
        shifted.append(xs)
        term = cw[k:k + 1, :] * xs
        c = term if c is None else c + term
    return c, shifted


def _head_rsq(a):
    parts = []
    for h in range(HEADS):
        ah = a[:, h * HD:(h + 1) * HD]
        r = lax.rsqrt(jnp.sum(ah * ah, axis=-1, keepdims=True) + EPS)
        parts.append(jnp.broadcast_to(r, ah.shape))
    return jnp.concatenate(parts, axis=-1)


def _head_sum(a):
    parts = []
    for h in range(HEADS):
        ah = a[:, h * HD:(h + 1) * HD]
        parts.append(jnp.broadcast_to(jnp.sum(ah, axis=-1, keepdims=True), ah.shape))
    return jnp.concatenate(parts, axis=-1)


def _softplus(x):
    return jnp.maximum(x, 0.0) + jnp.log1p(jnp.exp(-jnp.abs(x)))


def _halo_prev_spec(width, rows):
    per = TM // rows
    return pl.BlockSpec((rows, width), lambda i: (jnp.maximum(i * per - 1, 0), 0))


def _halo_next_spec(width, rows, t):
    per = TM // rows
    last = t // rows - 1
    return pl.BlockSpec((rows, width), lambda i: (jnp.minimum((i + 1) * per, last), 0))


def _dn_pre(qkv, b_rep, a_rep, cw, alog, dtb):
    t = qkv.shape[0]
    qscale = HD ** -0.5

    def body(x_ref, halo_ref, b_ref, a_ref, cw_ref, alog_ref, dt_ref, q_ref, k_ref, v_ref, beta_ref, g_ref):
        i = pl.program_id(0)
        halo = jnp.where(i == 0, 0.0, halo_ref[...])
        c, _ = _conv_rows(x_ref[...], halo, cw_ref[...])
        sc = c * _sigmoid(c)
        q = sc[:, 0:HW]
        k = sc[:, HW:2 * HW]
        q_ref[...] = q * _head_rsq(q) * qscale
        k_ref[...] = k * _head_rsq(k)
        v_ref[...] = sc[:, 2 * HW:]
        beta_ref[...] = _sigmoid(b_ref[...])
        g_ref[...] = -jnp.exp(alog_ref[...]) * _softplus(a_ref[...] + dt_ref[...])

    row = lambda w: pl.BlockSpec((TM, w), lambda i: (i, 0))
    full = lambda a: pl.BlockSpec(a.shape, lambda i: (0,) * a.ndim)
    return pl.pallas_call(
        body,
        grid=(t // TM,),
        in_specs=[row(QKV), _halo_prev_spec(QKV, 8), row(HW), row(HW), full(cw), full(alog), full(dtb)],
        out_specs=[row(HW)] * 5,
        out_shape=[jax.ShapeDtypeStruct((t, HW), F32)] * 5,
        name="dn_pre",
    )(qkv, qkv, b_rep, a_rep, cw, alog, dtb)


def _unit_lower_inv(los, eye):
    ps = [eye - lo for lo in los]
    lps = list(los)
    for _ in range(5):
        lps = [_dot(_c(lp), _c(lp)) for lp in lps]
        ps = [p + _dot(_c(p), _c(lp)) for p, lp in zip(ps, lps)]
    rs = [eye - (p + _dot3(lo, p)) for lo, p in zip(los, ps)]
    return [p + _dot(_c(p), _c(r)) for p, r in zip(ps, rs)]


def _dn_masks():
    ri = lax.broadcasted_iota(jnp.int32, (DNC, DNC), 0)
    ci = lax.broadcasted_iota(jnp.int32, (DNC, DNC), 1)
    return dict(strict=ri > ci, causal=ri >= ci, eye=(ri == ci).astype(F32),
                ltri=_c((ri >= ci).astype(F32)), upper=_c((ri <= ci).astype(F32)))


def _dn_decay(gr, mk):
    rhs = jnp.concatenate([gr, jnp.where(mk["strict"], gr[:, 0:DNC], 0.0)], axis=1)
    cs = _mask_dot(mk["ltri"], rhs)
    gc = cs[:, 0:HD]
    dm = jnp.where(mk["causal"], jnp.exp(cs[:, HD:HD + DNC]), 0.0)
    gl = jnp.sum(gr, axis=0, keepdims=True)
    return dm, jnp.exp(gc), jnp.exp(gl - gc), gl


def _dn_when(n):
    def when():
        i = pl.program_id(0)
        return (i == 0, i == n // 2, i == n - 1)
    return when


def _dn_fwd(q, k, v, beta, g, comm=None):
    t = q.shape[0]
    rows = DN_STEP * DNC
    n = t // rows

    def body(q_ref, k_ref, v_ref, b_ref, g_ref, o_ref, sall_ref, aall_ref, u_ref, w_ref, s_sc):
        i = pl.program_id(0)

        @pl.when(i == 0)
        def _():
            s_sc[...] = jnp.zeros_like(s_sc)

        mk = _dn_masks()
        idx = [(cc, h) for cc in range(DN_STEP) for h in range(HEADS)]
        at = lambda cc, h: (slice(cc * DNC, (cc + 1) * DNC), slice(h * HD, (h + 1) * HD))
        qs = [q_ref[at(*i)] for i in idx]
        ks = [k_ref[at(*i)] for i in idx]
        bs = [b_ref[at(*i)] for i in idx]
        dec = [_dn_decay(g_ref[at(*i)], mk) for i in idx]
        kbs = [k_ * b_ for k_, b_ in zip(ks, bs)]
        los = [jnp.where(mk["strict"], _dot_nt(_c(kb), _c(k_)) * d[0], 0.0) for kb, k_, d in zip(kbs, ks, dec)]
        inv = _unit_lower_inv(los, mk["eye"])
        uws = [_dot3(a, jnp.concatenate([v_ref[at(*i)] * b_, kb * d[1]], axis=1))
               for a, i, b_, kb, d in zip(inv, idx, bs, kbs, dec)]
        attn = [_c(_dot_nt(_c(q_), _c(k_)) * d[0]) for q_, k_, d in zip(qs, ks, dec)]
        for n_, (cc, h) in enumerate(idx):
            aall_ref[cc, h] = inv[n_]
            u_ref[at(cc, h)] = uws[n_][:, 0:HD]
            w_ref[at(cc, h)] = uws[n_][:, HD:]
        ss = [s_sc[h] for h in range(HEADS)]
        for cc in range(DN_STEP):
            base = cc * HEADS
            for h in range(HEADS):
                sall_ref[cc, h] = ss[h]
            ws = [_dot(_c(jnp.concatenate([uws[base + h][:, HD:], qs[base + h] * dec[base + h][1]], axis=0)),
                       _c(ss[h])) for h in range(HEADS)]
            vn = [_c(uws[base + h][:, 0:HD] - ws[h][0:DNC]) for h in range(HEADS)]
            for h in range(HEADS):
                o_ref[at(cc, h)] = ws[h][DNC:] + _dot(attn[base + h], vn[h])
            ss = [ss[h] * jnp.exp(dec[base + h][3]) + _dot_tn(_c(ks[base + h] * dec[base + h][2]), vn[h])
                  for h in range(HEADS)]
        for h in range(HEADS):
            s_sc[h] = ss[h]

    row = pl.BlockSpec((rows, HW), lambda i: (i, 0))
    return _carried_call(
        body, comm, 5, 5, 1, _dn_when(n),
        grid=(n,),
        in_specs=[row] * 5,
        out_specs=[row, pl.BlockSpec((DN_STEP, HEADS, HD, HD), lambda i: (i, 0, 0, 0)),
                   pl.BlockSpec((DN_STEP, HEADS, DNC, DNC), lambda i: (i, 0, 0, 0)), row, row],
        out_shape=[jax.ShapeDtypeStruct((t, HW), F32), jax.ShapeDtypeStruct((t // DNC, HEADS, HD, HD), F32),
                   jax.ShapeDtypeStruct((t // DNC, HEADS, DNC, DNC), F32), jax.ShapeDtypeStruct((t, HW), F32),
                   jax.ShapeDtypeStruct((t, HW), F32)],
        scratch_shapes=[pltpu.VMEM((HEADS, HD, HD), F32)],
        operands=(q, k, v, beta, g), name="dn_fwd")


def _dn_bwd(q, k, v, beta, g, sall, aall, u, w, do, comm=None):
    t = q.shape[0]
    rows = DN_STEP * DNC
    n = t // rows

    def body(q_ref, k_ref, v_ref, b_ref, g_ref, sall_ref, aall_ref, u_ref, w_ref, do_ref,
             dq_ref, dk_ref, dv_ref, db_ref, dg_ref, ds_sc):
        i = pl.program_id(0)

        @pl.when(i == 0)
        def _():
            ds_sc[...] = jnp.zeros_like(ds_sc)

        mk = _dn_masks()
        strict = mk["strict"]
        hs = range(HEADS)
        at = lambda cc, h: (slice(cc * DNC, (cc + 1) * DNC), slice(h * HD, (h + 1) * HD))
        rowsum = lambda a: jnp.sum(a, axis=-1, keepdims=True)
        dsn = [ds_sc[h] for h in hs]
        for cc in reversed(range(DN_STEP)):
            q = [q_ref[at(cc, h)] for h in hs]
            k = [k_ref[at(cc, h)] for h in hs]
            b = [b_ref[at(cc, h)] for h in hs]
            u = [u_ref[at(cc, h)] for h in hs]
            w = [w_ref[at(cc, h)] for h in hs]
            do = [do_ref[at(cc, h)] for h in hs]
            s = [sall_ref[cc, h] for h in hs]
            dec = [_dn_decay(g_ref[at(cc, h)], mk) for h in hs]
            dm, e, f = [d[0] for d in dec], [d[1] for d in dec], [d[2] for d in dec]
            egl = [jnp.exp(d[3]) for d in dec]
            kb = [k[h] * b[h] for h in hs]
            kc = [_c(k[h]) for h in hs]
            sb = [_c(s[h]) for h in hs]
            dob = [_c(do[h]) for h in hs]
            m = [_dot_nt(_c(kb[h]), kc[h]) for h in hs]
            p = [_dot_nt(_c(q[h]), kc[h]) for h in hs]
            vnb = [_c(u[h] - _dot(_c(w[h]), sb[h])) for h in hs]
            dsb = [_c(dsn[h]) for h in hs]
            dvn = [_dot_tn(_c(p[h] * dm[h]), dob[h]) + _dot(_c(k[h] * f[h]), dsb[h]) for h in hs]
            dov = [_c(jnp.concatenate([do[h], dvn[h]], axis=0)) for h in hs]
            t1 = [_dot_nt(dov[h], sb[h]) for h in hs]
            dattn = [_dot_nt(dob[h], vnb[h]) for h in hs]
            dkt = [_dot_nt(vnb[h], dsb[h]) for h in hs]
            dgl = [jnp.sum(jnp.sum(dsn[h] * s[h], axis=1, keepdims=True), axis=0, keepdims=True) * egl[h][:, 0:1]
                   for h in hs]
            dsn = [dsn[h] * egl[h] + _dot_tn(_c(jnp.concatenate([q[h] * e[h], -w[h]], axis=0)), dov[h]) for h in hs]
            dqd = [t1[h][0:DNC] for h in hs]
            dw = [-t1[h][DNC:] for h in hs]
            ab = [_dot3(aall_ref[cc, h], jnp.concatenate([dvn[h], dw[h]], axis=1), TN) for h in hs]
            dlo = [jnp.where(strict, -_dot3(ab[h], jnp.concatenate([u[h], w[h]], axis=1), NT), 0.0) for h in hs]
            dpm = [_c(jnp.concatenate([dattn[h] * dm[h], dlo[h] * dm[h]], axis=0)) for h in hs]
            t2 = [_dot(dpm[h], kc[h]) for h in hs]
            t4 = [_dot_tn(dpm[h], _c(jnp.concatenate([q[h], kb[h]], axis=0))) for h in hs]
            dff = [rowsum(dkt[h] * k[h]) * f[h][:, 0:1] for h in hs]
            de = [rowsum(dqd[h] * q[h]) + rowsum(ab[h][:, HD:] * kb[h]) for h in hs]
            dd = [(dattn[h] * p[h] + dlo[h] * m[h]) * dm[h] for h in hs]
            t3 = [_mask_dot(mk["upper"], jnp.concatenate(
                [jnp.broadcast_to(de[h] * e[h][:, 0:1] - dff[h], (DNC, HD)), dd[h]], axis=1)) for h in hs]
            for h in hs:
                dvb, dkbe = ab[h][:, 0:HD], ab[h][:, HD:]
                dkb = t2[h][DNC:] + dkbe * e[h]
                dbeta = rowsum(dkb * k[h]) + rowsum(dvb * v_ref[at(cc, h)])
                dg = (rowsum(jnp.where(strict, t3[h][:, HD:HD + DNC], 0.0)) + t3[h][:, 0:1]
                      + dgl[h] + jnp.sum(dff[h], axis=0, keepdims=True))
                dq_ref[at(cc, h)] = dqd[h] * e[h] + t2[h][0:DNC]
                dk_ref[at(cc, h)] = t4[h] + dkt[h] * f[h] + dkb * b[h]
                dv_ref[at(cc, h)] = dvb * b[h]
                db_ref[at(cc, h)] = jnp.broadcast_to(dbeta, (DNC, HD))
                dg_ref[at(cc, h)] = jnp.broadcast_to(dg, (DNC, HD))
        for h in hs:
            ds_sc[h] = dsn[h]

    row = pl.BlockSpec((rows, HW), lambda i: (n - 1 - i, 0))
    return _carried_call(
        body, comm, 10, 5, 1, _dn_when(n),
        grid=(n,),
        in_specs=[row] * 5 + [pl.BlockSpec((DN_STEP, HEADS, HD, HD), lambda i: (n - 1 - i, 0, 0, 0)),
                              pl.BlockSpec((DN_STEP, HEADS, DNC, DNC), lambda i: (n - 1 - i, 0, 0, 0)), row, row, row],
        out_specs=[row] * 5,
        out_shape=[jax.ShapeDtypeStruct((t, HW), F32)] * 5,
        scratch_shapes=[pltpu.VMEM((HEADS, HD, HD), F32)],
        operands=(q, k, v, beta, g, sall, aall, u, w, do), name="dn_bwd")


def _group_norm(a, nw):
    rs = []
    for h in range(HEADS):
        ah = a[:, h * HD:(h + 1) * HD]
        rs.append(jnp.broadcast_to(_rstd(ah), ah.shape))
    r = jnp.concatenate(rs, axis=-1)
    xh = a * r
    return xh * nw, xh, r


def _group_norm_bwd(dy, xh, r, nw):
    dxh = dy * nw
    return r * (dxh - xh * (_head_sum(dxh * xh) * (1.0 / HD)))


def _sg_mix(wt_ref, svn_b, nchunk):
    rows = []
    for cidx in range(nchunk):
        cols = []
        for g in range(HEADS):
            blk = svn_b[cidx * SGC:(cidx + 1) * SGC, g * HD:(g + 1) * HD]
            cols.append(_dot(wt_ref[g], blk))
        rows.append(jnp.concatenate(cols, axis=-1))
    return jnp.concatenate(rows, axis=0)


def _ab_out(x1, o, z, su, sv, dnw, sgnw, wtril, sgb, wout):
    t = x1.shape[0]
    nchunk = TM // SGC

    def body(x_ref, o_ref, z_ref, su_ref, sv_ref, dnw_ref, sgnw_ref, wt_ref, sgb_ref, wo_ref, x2_ref, cat_ref):
        on, _, _ = _group_norm(o_ref[...], dnw_ref[...])
        zv = z_ref[...]
        cat_ref[:, 0:HW] = _c(on * (zv * _sigmoid(zv)))
        svn, _, _ = _group_norm(_gelu(sv_ref[...]), sgnw_ref[...])
        mixed = _sg_mix(wt_ref, _c(svn), nchunk) + jnp.tile(sgb_ref[...], (nchunk, 1))
        cat_ref[:, HW:] = _c(_gelu(su_ref[...]) * mixed)
        x2_ref[...] = x_ref[...] + _dot(cat_ref[...], wo_ref[...])

    row = lambda w: pl.BlockSpec((TM, w), lambda i: (i, 0))
    full = lambda a: pl.BlockSpec(a.shape, lambda i: (0,) * a.ndim)
    return pl.pallas_call(
        body,
        grid=(t // TM,),
        in_specs=[row(D)] + [row(HW)] * 4 + [full(dnw), full(sgnw), full(wtril), full(sgb), full(wout)],
        out_specs=[row(D), row(D)],
        out_shape=[jax.ShapeDtypeStruct((t, D), F32), jax.ShapeDtypeStruct((t, D), _MM)],
        name="ab_out",
    )(x1, o, z, su, sv, dnw, sgnw, wtril, sgb, wout)


def _ab_out_bwd(dx2, o, z, su, sv, dnw, sgnw, wtril, wtril_t, sgb, wout):
    t = dx2.shape[0]
    nchunk = TM // SGC

    def body(dx_ref, o_ref, z_ref, su_ref, sv_ref, dnw_ref, sgnw_ref, wt_ref, wtt_ref, sgb_ref, wo_ref,
             do_ref, dz_ref, dsu_ref, dsv_ref, ddnw_ref, dsgnw_ref, dsgw_ref, dsgb_ref):
        i = pl.program_id(0)
        dcat = _dot_nt(_c(dx_ref[...]), wo_ref[...])
        doa = dcat[:, 0:HW]
        dob = dcat[:, HW:]
        on, oh, ro = _group_norm(o_ref[...], dnw_ref[...])
        zv = z_ref[...]
        sz = _sigmoid(zv)
        dz_ref[...] = _c(doa * on * (sz * (1.0 + zv * (1.0 - sz))))
        don = doa * (zv * sz)
        do_ref[...] = _group_norm_bwd(don, oh, ro, dnw_ref[...])
        dd = jnp.sum(don * oh, axis=0, keepdims=True)
        _accum(ddnw_ref, dd[:, 0:HD] + dd[:, HD:2 * HD] + dd[:, 2 * HD:3 * HD] + dd[:, 3 * HD:], i)
        suv = su_ref[...]
        svv = sv_ref[...]
        svg = _gelu(svv)
        svn, sh, rs = _group_norm(svg, sgnw_ref[...])
        svn_b = _c(svn)
        mixed = _sg_mix(wt_ref, svn_b, nchunk) + jnp.tile(sgb_ref[...], (nchunk, 1))
        dsu_ref[...] = _c(dob * mixed * _gelu_grad(suv))
        dmixed = dob * _gelu(suv)
        dmb = _c(dmixed)
        tri = lax.broadcasted_iota(jnp.int32, (SGC, SGC), 0) >= lax.broadcasted_iota(jnp.int32, (SGC, SGC), 1)
        lane = lax.broadcasted_iota(jnp.int32, (SGC, HD), 1)
        rows = []
        dbias = jnp.zeros((SGC, HD), F32)
        for g in range(HEADS):
            gs = slice(g * HD, (g + 1) * HD)
            dwg = jnp.zeros((SGC, SGC), F32)
            col = jnp.zeros((SGC, 1), F32)
            for cidx in range(nchunk):
                cs = slice(cidx * SGC, (cidx + 1) * SGC)
                dwg = dwg + _dot_nt(dmb[cs, gs], svn_b[cs, gs])
                col = col + jnp.sum(dmixed[cs, gs], axis=-1, keepdims=True)
            _accum(dsgw_ref.at[g], jnp.where(tri, dwg, 0.0), i)
            dbias = dbias + jnp.where(lane == g, col, 0.0)
        _accum(dsgb_ref, dbias, i)
        for cidx in range(nchunk):
            cs = slice(cidx * SGC, (cidx + 1) * SGC)
            rows.append(jnp.concatenate(
                [_dot(wtt_ref[g], dmb[cs, g * HD:(g + 1) * HD]) for g in range(HEADS)], axis=-1))
        dsvn = jnp.concatenate(rows, axis=0)
        _accum(dsgnw_ref, jnp.sum(dsvn * sh, axis=0, keepdims=True), i)
        dsv_ref[...] = _c(_group_norm_bwd(dsvn, sh, rs, sgnw_ref[...]) * _gelu_grad(svv))

    row = lambda w: pl.BlockSpec((TM, w), lambda i: (i, 0))
    full = lambda a: pl.BlockSpec(a.shape, lambda i: (0,) * a.ndim)
    const = lambda shape: pl.BlockSpec(shape, lambda i: (0,) * len(shape))
    return pl.pallas_call(
        body,
        grid=(t // TM,),
        in_specs=[row(D)] + [row(HW)] * 4 + [full(dnw), full(sgnw), full(wtril), full(wtril_t), full(sgb), full(wout)],
        out_specs=[row(HW)] * 4 + [const((1, HD)), const((1, HW)), const((HEADS, SGC, SGC)), const((SGC, HD))],
        out_shape=[jax.ShapeDtypeStruct((t, HW), F32)] + [jax.ShapeDtypeStruct((t, HW), _MM)] * 3
        + [jax.ShapeDtypeStruct((1, HD), F32), jax.ShapeDtypeStruct((1, HW), F32),
           jax.ShapeDtypeStruct((HEADS, SGC, SGC), F32), jax.ShapeDtypeStruct((SGC, HD), F32)],
        name="ab_out_bwd",
    )(dx2, o, z, su, sv, dnw, sgnw, wtril, wtril_t, sgb, wout)


def _dn_pre_bwd(qkv, b_rep, a_rep, cw, alog, dtb, dqn, dkn, dv, dbeta, dg):
    t = qkv.shape[0]
    qscale = HD ** -0.5

    def body(x_ref, halo_ref, b_ref, a_ref, cw_ref, alog_ref, dt_ref, dq_ref, dk_ref, dv_ref, dbeta_ref, dg_ref,
             dc_ref, dba_ref, dcw_ref, dalog_ref, ddt_ref):
        i = pl.program_id(0)
        halo = jnp.where(i == 0, 0.0, halo_ref[...])
        c, shifted = _conv_rows(x_ref[...], halo, cw_ref[...])
        s = _sigmoid(c)
        sc = c * s
        q = sc[:, 0:HW]
        k = sc[:, HW:2 * HW]
        rq = _head_rsq(q)
        rk = _head_rsq(k)
        qu = q * rq
        ku = k * rk
        dqn = dq_ref[...]
        dkn = dk_ref[...]
        dq = qscale * rq * (dqn - qu * _head_sum(dqn * qu))
        dk = rk * (dkn - ku * _head_sum(dkn * ku))
        dsc = jnp.concatenate([dq, dk, dv_ref[...]], axis=-1)
        dc = dsc * (s * (1.0 + c * (1.0 - s)))
        dc_ref[...] = dc
        for kk in range(4):
            _accum(dcw_ref.at[kk], jnp.sum(dc * shifted[kk], axis=0, keepdims=True), i)
        beta = _sigmoid(b_ref[...])
        dbp = dbeta_ref[...] * beta * (1.0 - beta)
        nea = -jnp.exp(alog_ref[...])
        spin = a_ref[...] + dt_ref[...]
        dgv = dg_ref[...]
        dap = dgv * nea * _sigmoid(spin)
        _accum(dalog_ref, jnp.sum(dgv * nea * _softplus(spin), axis=0, keepdims=True), i)
        _accum(ddt_ref, jnp.sum(dap, axis=0, keepdims=True), i)
        lane = lax.broadcasted_iota(jnp.int32, (TM, HD), 1)
        dba = jnp.zeros((TM, HD), F32)
        for h in range(HEADS):
            dba = dba + jnp.where(lane == h, dbp[:, h * HD:(h + 1) * HD], 0.0)
            dba = dba + jnp.where(lane == HEADS + h, dap[:, h * HD:(h + 1) * HD], 0.0)
        dba_ref[...] = _c(dba)

    row = lambda w: pl.BlockSpec((TM, w), lambda i: (i, 0))
    full = lambda a: pl.BlockSpec(a.shape, lambda i: (0,) * a.ndim)
    const = lambda shape: pl.BlockSpec(shape, lambda i: (0,) * len(shape))
    return pl.pallas_call(
        body,
        grid=(t // TM,),
        in_specs=[row(QKV), _halo_prev_spec(QKV, 8), row(HW), row(HW), full(cw), full(alog), full(dtb)] + [row(HW)] * 5,
        out_specs=[row(QKV), row(HD), const((4, 1, QKV)), const((1, HW)), const((1, HW))],
        out_shape=[jax.ShapeDtypeStruct((t, QKV), F32), jax.ShapeDtypeStruct((t, HD), _MM),
                   jax.ShapeDtypeStruct((4, 1, QKV), F32), jax.ShapeDtypeStruct((1, HW), F32),
                   jax.ShapeDtypeStruct((1, HW), F32)],
        name="dn_pre_bwd",
    )(qkv, qkv, b_rep, a_rep, cw, alog, dtb, dqn, dkn, dv, dbeta, dg)


def _conv_bwd(dc, cw):
    t = dc.shape[0]
    nt = t // TM

    def body(dc_ref, halo_ref, cw_ref, dx_ref):
        i = pl.program_id(0)
        halo = jnp.where(i == nt - 1, 0.0, halo_ref[...])
        de = jnp.concatenate([dc_ref[...], halo], axis=0)
        cwv = cw_ref[...]
        acc = None
        for k in range(4):
            s = 3 - k
            ds = (de if s == 0 else pltpu.roll(de, TM + 8 - s, 0))[0:TM, :]
            term = cwv[k:k + 1, :] * ds
            acc = term if acc is None else acc + term
        dx_ref[...] = _c(acc)

    return pl.pallas_call(
        body,
        grid=(nt,),
        in_specs=[pl.BlockSpec((TM, QKV), lambda i: (i, 0)), _halo_next_spec(QKV, 8, t),
                  pl.BlockSpec(cw.shape, lambda i: (0, 0))],
        out_specs=pl.BlockSpec((TM, QKV), lambda i: (i, 0)),
        out_shape=jax.ShapeDtypeStruct((t, QKV), _MM),
        name="conv_bwd",
    )(dc, dc, cw)


def _ab_proj_bwd(x1, nw, dqkv, dz, dsu, dsv, dba, wab_b, dres):
    t = x1.shape[0]

    def body(x_ref, nw_ref, dqkv_ref, dz_ref, dsu_ref, dsv_ref, dba_ref, w_ref, dres_ref, dx_ref, dcat_ref, dnw_ref):
        i = pl.program_id(0)
        dcat_ref[:, 0:QKV] = dqkv_ref[...]
        o = QKV
        for ref in (dz_ref, dsu_ref, dsv_ref):
            dcat_ref[:, o:o + HW] = ref[...]
            o += HW
        dcat_ref[:, o:o + 128] = dba_ref[...]
        dh = _dot_nt(dcat_ref[...], w_ref[...])
        xv = x_ref[...]
        r = _rstd(xv)
        dx, dnw = _rms_bwd(dh, xv * r, r, nw_ref[...])
        dx_ref[...] = dres_ref[...] + dx
        _accum(dnw_ref, dnw, i)

    row = lambda w: pl.BlockSpec((TM, w), lambda i: (i, 0))
    return pl.pallas_call(
        body,
        grid=(t // TM,),
        in_specs=[row(D), pl.BlockSpec((1, D), lambda i: (0, 0)), row(QKV), row(HW), row(HW), row(HW), row(128),
                  pl.BlockSpec((D, PW_B), lambda i: (0, 0)), row(D)],
        out_specs=[row(D), row(PW_B), pl.BlockSpec((1, D), lambda i: (0, 0))],
        out_shape=[jax.ShapeDtypeStruct((t, D), F32), jax.ShapeDtypeStruct((t, PW_B), _MM),
                   jax.ShapeDtypeStruct((1, D), F32)],
        name="ab_proj_bwd",
    )(x1, nw, dqkv, dz, dsu, dsv, dba, wab_b, dres)


def _pool_counts(i):
    pos = (lax.broadcasted_iota(jnp.int32, (TM + HALO, 1), 0) + i * TM + 1).astype(F32)
    return [1.0 / jnp.minimum(pos, float(w)) for w in POOL_WINDOWS]


def _window_sum(ext, win, back):
    r = ext.shape[0]
    s = ext
    step = 1
    while step < win:
        s = s + pltpu.roll(s, step if back else r - step, 0)
        step *= 2
    return s


def _pooled(h_ext, invc, g):
    gs = slice(g * PG, (g + 1) * PG)
    he = h_ext[:, gs]
    ws = _window_sum(he, POOL_WINDOWS[g], True)[HALO:, :]
    return ws * invc[g][0:TM, :] - he[HALO:, :]


def _pool_fwd(x1, nw, pw, scale):
    t = x1.shape[0]

    def body(x_ref, halo_ref, nw_ref, pw_ref, sc_ref, x2_ref):
        i = pl.program_id(0)
        xv = x_ref[...]
        hv = halo_ref[...]
        nwv = nw_ref[...]
        h_ext = jnp.concatenate([jnp.where(i == 0, 0.0, hv * _rstd(hv) * nwv), xv * _rstd(xv) * nwv], axis=0)
        invc = _pool_counts(i)
        outs = [_dot(_c(_pooled(h_ext, invc, g)), pw_ref[g]) for g in range(4)]
        x2_ref[...] = xv + jnp.concatenate(outs, axis=-1) * sc_ref[...]

    return pl.pallas_call(
        body,
        grid=(t // TM,),
        in_specs=[pl.BlockSpec((TM, D), lambda i: (i, 0)), _halo_prev_spec(D, HALO),
                  pl.BlockSpec((1, D), lambda i: (0, 0)), pl.BlockSpec((4, PG, PG), lambda i: (0, 0, 0)),
                  pl.BlockSpec((1, D), lambda i: (0, 0))],
        out_specs=pl.BlockSpec((TM, D), lambda i: (i, 0)),
        out_shape=jax.ShapeDtypeStruct((t, D), F32),
        name="pool_fwd",
    )(x1, x1, nw, pw, scale)


def _pool_bwd(x1, nw, pw, scale, dx2):
    t = x1.shape[0]
    nt = t // TM

    def body(x_ref, halo_ref, nw_ref, pw_ref, sc_ref, dx2_ref, dnext_ref, dx_ref, dnw_ref, dpw_ref, dsc_ref):
        i = pl.program_id(0)
        xv = x_ref[...]
        hv = halo_ref[...]
        nwv = nw_ref[...]
        r = _rstd(xv)
        xh = xv * r
        h_ext = jnp.concatenate([jnp.where(i == 0, 0.0, hv * _rstd(hv) * nwv), xh * nwv], axis=0)
        invc = _pool_counts(i)
        dyv = dx2_ref[...]
        dout_ext = jnp.concatenate([dyv, jnp.where(i == nt - 1, 0.0, dnext_ref[...])], axis=0) * sc_ref[...]
        dh_parts = []
        dsc_parts = []
        for g in range(4):
            gs = slice(g * PG, (g + 1) * PG)
            pooled_b = _c(_pooled(h_ext, invc, g))
            dout_b = _c(dout_ext[:, gs])
            dsc_parts.append(jnp.sum(dyv[:, gs] * _dot(pooled_b, pw_ref[g]), axis=0, keepdims=True))
            _accum(dpw_ref.at[g], _dot_tn(pooled_b, dout_b[0:TM, :]), i)
            dpool_ext = _dot_nt(dout_b, pw_ref[g])
            lead = _window_sum(dpool_ext * invc[g], POOL_WINDOWS[g], False)[0:TM, :]
            dh_parts.append(lead - dpool_ext[0:TM, :])
        _accum(dsc_ref, jnp.concatenate(dsc_parts, axis=-1), i)
        dx, dnw = _rms_bwd(jnp.concatenate(dh_parts, axis=-1), xh, r, nwv)
        dx_ref[...] = dyv + dx
        _accum(dnw_ref, dnw, i)

    vec = pl.BlockSpec((1, D), lambda i: (0, 0))
    return pl.pallas_call(
        body,
        grid=(nt,),
        in_specs=[pl.BlockSpec((TM, D), lambda i: (i, 0)), _halo_prev_spec(D, HALO), vec,
                  pl.BlockSpec((4, PG, PG), lambda i: (0, 0, 0)), vec,
                  pl.BlockSpec((TM, D), lambda i: (i, 0)), _halo_next_spec(D, HALO, t)],
        out_specs=[pl.BlockSpec((TM, D), lambda i: (i, 0)), vec, pl.BlockSpec((4, PG, PG), lambda i: (0, 0, 0)), vec],
        out_shape=[jax.ShapeDtypeStruct((t, D), F32), jax.ShapeDtypeStruct((1, D), F32),
                   jax.ShapeDtypeStruct((4, PG, PG), F32), jax.ShapeDtypeStruct((1, D), F32)],
        name="pool_bwd",
    )(x1, x1, nw, pw, scale, dx2, dx2)


def _adamw(lands, w, m, v, rb, name):
    nl, nr = w.shape[0], w.shape[1]
    rest = w.shape[2:]
    ns = lands[0].shape[0]
    zeros = (0,) * len(rest)

    def body(*refs):
        l_refs = refs[0:nl]
        w_ref, m_ref, v_ref, g_ref, d_ref, m2_ref, v2_ref = refs[nl:]
        for l in range(nl):
            g = l_refs[l][0].astype(F32)
            for s in range(1, ns):
                g = g + l_refs[l][s].astype(F32)
            m2 = ADAM_B1 * m_ref[l] + (1.0 - ADAM_B1) * g
            v2 = ADAM_B2 * v_ref[l] + (1.0 - ADAM_B2) * (g * g)
            m_hat = m2 / (1.0 - ADAM_B1 ** ADAM_STEP)
            v_hat = v2 / (1.0 - ADAM_B2 ** ADAM_STEP)
            g_ref[l] = g
            d_ref[l] = -ADAM_LR * (m_hat / (jnp.sqrt(v_hat) + ADAM_EPS) + ADAM_WD * w_ref[l])
            m2_ref[l] = m2
            v2_ref[l] = v2

    lspec = pl.BlockSpec((ns, rb) + rest, lambda r: (0, r) + zeros)
    wspec = pl.BlockSpec((nl, rb) + rest, lambda r: (0, r) + zeros)
    return pl.pallas_call(
        body,
        grid=(nr // rb,),
        in_specs=[lspec] * nl + [wspec] * 3,
        out_specs=[wspec] * 4,
        out_shape=[jax.ShapeDtypeStruct(w.shape, F32)] * 4,
        name=name,
    )(*lands, w, m, v)


WEIGHT_ORDER = ("ffn_norm1", "ffn1_w_in", "ffn1_w_out", "mix_norm", "ffn_norm2", "ffn2_w_in", "ffn2_w_out", "ab_w_in",
                "dn_conv_w", "dn_a_log", "dn_dt_bias", "dn_out_norm", "sg_norm", "sg_w", "sg_b", "ab_w_out", "pool_w",
                "pool_scale", "final_norm")
R_SMALL = 88
SMALL_ROWS = (
    ("ffn_norm1", (2, D), 2), ("mix_norm", (2, D), 2), ("ffn_norm2", (2, D), 2), ("final_norm", (D,), 1),
    ("sg_w", (1, 4, SGC, SGC), 64), ("sg_norm", (1, 4, HD), 1), ("sg_b", (1, 4, SGC), 1), ("dn_out_norm", (1, HD), 1),
    ("dn_a_log", (1, 4), 1), ("dn_dt_bias", (1, 4), 1), ("pool_scale", (1, D), 1), ("dn_conv_w", (1, 4, QKV), 8),
)
SMALL_SHARDED = ("pool_scale", "dn_conv_w")


def _rows_of(a, rows):
    if a.shape[-1] == QKV:
        return jnp.pad(a.reshape(4, QKV), ((0, 0), (0, 2 * ROW - QKV))).reshape(8, ROW)
    n = _numel(a.shape)
    if n % ROW == 0:
        return a.reshape(n // ROW, ROW)
    return jnp.pad(a.reshape(1, n), ((0, 0), (0, ROW - n)))


def _from_rows(r, shape):
    if shape[-1] == QKV:
        return r.reshape(4, 2 * ROW)[:, 0:QKV].reshape(shape)
    n = _numel(shape)
    if n % ROW == 0:
        return r.reshape(shape)
    return r[:, 0:n].reshape(shape)


def _pack_small(vals):
    parts = [(_rows_of(vals[n].astype(F32), r) if n in vals else jnp.zeros((r, ROW), F32)) for n, _, r in SMALL_ROWS]
    used = sum(r for _, _, r in SMALL_ROWS)
    return jnp.concatenate(parts + [jnp.zeros((R_SMALL - used, ROW), F32)], axis=0)


def _unpack_small(packed):
    out, o = {}, 0
    for n, shape, r in SMALL_ROWS:
        out[n] = _from_rows(packed[o:o + r], shape)
        o += r
    return out


def _pack_small_shard(ps, cw):
    return jnp.concatenate([
        jnp.pad(ps, ((0, 0), (0, ROW - D // N_DEV))), jnp.pad(cw[0], ((0, 0), (0, ROW - QKV // N_DEV))),
        jnp.zeros((3, ROW), F32)], axis=0)


def _mixer_weights(g_in, g_out, g_small, small):
    w = {}
    wi = jnp.transpose(g_in, (1, 0, 2)).reshape(D, AB_IN)
    main = [wi[:, 0:2048], wi[:, 2056:AB_IN]]
    w["wab_f"] = jnp.concatenate(
        main + [jnp.repeat(wi[:, 2048:2052], HD, axis=1), jnp.repeat(wi[:, 2052:2056], HD, axis=1)], axis=1)
    w["wab_b"] = jnp.concatenate(main + [wi[:, 2048:2056], jnp.zeros((D, 120), wi.dtype)], axis=1)
    w["cw"] = jnp.transpose(g_small[:, 1:5, 0:QKV // N_DEV], (1, 0, 2)).reshape(4, QKV)
    w["ps"] = g_small[:, 0, 0:D // N_DEV].reshape(1, D)
    w["alog"] = jnp.repeat(small["dn_a_log"][0].astype(F32), HD).reshape(1, HW)
    w["dtb"] = jnp.repeat(small["dn_dt_bias"][0].astype(F32), HD).reshape(1, HW)
    w["dnw"] = jnp.tile(small["dn_out_norm"][0].astype(F32), HEADS).reshape(1, HW)
    w["sgnw"] = small["sg_norm"][0].astype(F32).reshape(1, HW)
    tri = jnp.tril(jnp.ones((SGC, SGC), dtype=bool))
    wt = jnp.where(tri, small["sg_w"][0].astype(F32), 0.0)
    w["wtril"] = _c(wt)
    w["wtril_t"] = _c(jnp.transpose(wt, (0, 2, 1)))
    w["sgb"] = jnp.repeat(jnp.transpose(small["sg_b"][0].astype(F32)), HD, axis=1)
    w["wout_ab"] = g_out.reshape(D, D)
    return w


def kernel(x, ffn_norm1, ffn1_w_in, ffn1_w_out, mix_norm, ffn_norm2, ffn2_w_in, ffn2_w_out, ab_w_in, dn_conv_w, dn_a_log, dn_dt_bias, dn_out_norm, sg_norm, sg_w, sg_b, ab_w_out, pool_w, pool_scale, final_norm, loss_target, m_ffn_norm1, m_ffn1_w_in, m_ffn1_w_out, m_mix_norm, m_ffn_norm2, m_ffn2_w_in, m_ffn2_w_out, m_ab_w_in, m_dn_conv_w, m_dn_a_log, m_dn_dt_bias, m_dn_out_norm, m_sg_norm, m_sg_w, m_sg_b, m_ab_w_out, m_pool_w, m_pool_scale, m_final_norm, v_ffn_norm1, v_ffn1_w_in, v_ffn1_w_out, v_mix_norm, v_ffn_norm2, v_ffn2_w_in, v_ffn2_w_out, v_ab_w_in, v_dn_conv_w, v_dn_a_log, v_dn_dt_bias, v_dn_out_norm, v_sg_norm, v_sg_w, v_sg_b, v_ab_w_out, v_pool_w, v_pool_scale, v_final_norm):
    wl = dict(ffn_norm1=ffn_norm1, mix_norm=mix_norm, ffn_norm2=ffn_norm2, dn_a_log=dn_a_log, dn_dt_bias=dn_dt_bias,
              dn_out_norm=dn_out_norm, sg_norm=sg_norm, sg_w=sg_w, sg_b=sg_b, final_norm=final_norm)
    ml = dict(ffn_norm1=m_ffn_norm1, mix_norm=m_mix_norm, ffn_norm2=m_ffn_norm2, dn_a_log=m_dn_a_log,
              dn_dt_bias=m_dn_dt_bias, dn_out_norm=m_dn_out_norm, sg_norm=m_sg_norm, sg_w=m_sg_w, sg_b=m_sg_b,
              final_norm=m_final_norm)
    vl = dict(ffn_norm1=v_ffn_norm1, mix_norm=v_mix_norm, ffn_norm2=v_ffn_norm2, dn_a_log=v_dn_a_log,
              dn_dt_bias=v_dn_dt_bias, dn_out_norm=v_dn_out_norm, sg_norm=v_sg_norm, sg_w=v_sg_w, sg_b=v_sg_b,
              final_norm=v_final_norm)
    row = lambda a: a.reshape(1, -1).astype(F32)
    n1 = [row(ffn_norm1[l]) for l in range(2)]
    n2 = [row(ffn_norm2[l]) for l in range(2)]
    mix = [row(mix_norm[l]) for l in range(2)]
    s_in = {(f, l): _c(wf[l]) for f, wf in enumerate((ffn1_w_in, ffn2_w_in)) for l in range(2)}
    s_out = {(f, l): _c(wf[l]) for f, wf in enumerate((ffn1_w_out, ffn2_w_out)) for l in range(2)}
    xs, tgt = x[0], loss_target[0]

    wi00, wo00 = _comm_call(_Comm("gather", [s_in[0, 0], s_out[0, 0]]), "gather_first")
    x01, gu00, (g_abin, g_about, g_small, wi10, wo10) = _ffn_fwd(
        xs, n1[0], wi00, wo00,
        comm=_Comm("gather", [_c(ab_w_in[0]), _c(ab_w_out[0]), _pack_small_shard(pool_scale, dn_conv_w),
                              s_in[1, 0], s_out[1, 0]]))
    w = _mixer_weights(g_abin, g_about, g_small, wl)
    h, qkv, z, su, sv, b_rep, a_rep = _ab_proj(x01, mix[0], w["wab_f"])
    qn, kn, v, beta, g = _dn_pre(qkv, b_rep, a_rep, w["cw"], w["alog"], w["dtb"])
    (o, sall, aall, dn_u, dn_w), (wi01, wo01, g_pw) = _dn_fwd(
        qn, kn, v, beta, g, comm=_Comm("gather", [s_in[0, 1], s_out[0, 1], _c(pool_w[0])]))
    pw = jnp.transpose(g_pw, (1, 0, 2, 3)).reshape(4, PG, PG)
    x02, cat = _ab_out(x01, o, z, su, sv, w["dnw"], w["sgnw"], w["wtril"], w["sgb"], w["wout_ab"])
    x10, gu10, (wi11, wo11) = _ffn_fwd(x02, n2[0], wi10, wo10, comm=_Comm("gather", [s_in[1, 1], s_out[1, 1]]))
    x11, gu01, _ = _ffn_fwd(x10, n1[1], wi01, wo01)
    x12 = _pool_fwd(x11, mix[1], pw, w["ps"])
    x13, gu11, _ = _ffn_fwd(x12, n2[1], wi11, wo11)
    loss_local, dx, d_fn = _loss_head(x13, row(final_norm), tgt)

    bt = min(BT, xs.shape[0])

    def ffn_b(xin, nw, w_in, w_out, gu, dy, comm=None):
        (dxn, xn, act, dh, dnw), landed = _ffn_bwd(xin, nw, w_in, w_out, gu, dy, comm)
        return dxn, dnw, [_mm_tn_win(xn, dh), _mm_tn_wout(act, dy)], landed

    dx, d_n2_1, g11, _ = ffn_b(x12, n2[1], wi11, wo11, gu11, dx)
    dx, d_mix_1, d_pw, d_ps = _pool_bwd(x11, mix[1], pw, w["ps"], dx)
    d_pw_sh = _c(jnp.transpose(d_pw.reshape(4, N_DEV, PG // N_DEV, PG), (1, 0, 2, 3)))
    dx, d_n1_1, g01, land11 = ffn_b(x10, n1[1], wi01, wo01, gu01, dx, _Comm("exchange", g11))
    dx, d_n2_0, g10, land01 = ffn_b(x02, n2[0], wi10, wo10, gu10, dx, _Comm("exchange", g01 + [d_pw_sh]))
    do, dz, dsu, dsv, d_dnw, d_sgnw, d_sgw, d_sgb = _ab_out_bwd(
        dx, o, z, su, sv, w["dnw"], w["sgnw"], w["wtril"], w["wtril_t"], w["sgb"], w["wout_ab"])
    d_about = _mm_tn(cat, dx, D, D, bt, _MM, "mm_tn_about").reshape(N_DEV, D // N_DEV, D)
    (dqn, dkn, dv, dbeta, dg), _ = _dn_bwd(qn, kn, v, beta, g, sall, aall, dn_u, dn_w, do)
    dc, dba, d_cw, d_alog, d_dtb = _dn_pre_bwd(qkv, b_rep, a_rep, w["cw"], w["alog"], w["dtb"], dqn, dkn, dv, dbeta, dg)
    dqkv = _conv_bwd(dc, w["cw"])
    dx, dcat, d_mix_0 = _ab_proj_bwd(x01, mix[0], dqkv, dz, dsu, dsv, dba, w["wab_b"], dx)
    d_wab = _mm_tn(h, dcat, D, 640, bt, _MM, "mm_tn_abin")
    d_abin = jnp.concatenate([d_wab[:, 0:2048], d_wab[:, 3072:3080], d_wab[:, 2048:3072]], axis=1)
    d_abin_sh = jnp.transpose(d_abin.reshape(D, N_DEV, AB_IN // N_DEV), (1, 0, 2))
    grad_x, d_n1_0, g00, land_mid = ffn_b(xs, n1[0], wi00, wo00, gu00, dx,
                                          _Comm("exchange", g10 + [d_abin_sh, d_about]))
    land10, land_ab = land_mid[0:2], land_mid[2:4]

    g_small = {
        "ffn_norm1": jnp.concatenate([d_n1_0, d_n1_1], axis=0),
        "mix_norm": jnp.concatenate([d_mix_0, d_mix_1], axis=0),
        "ffn_norm2": jnp.concatenate([d_n2_0, d_n2_1], axis=0),
        "dn_conv_w": d_cw.reshape(1, 4, QKV),
        "dn_a_log": d_alog[:, ::HD],
        "dn_dt_bias": d_dtb[:, ::HD],
        "dn_out_norm": d_dnw,
        "sg_norm": d_sgnw.reshape(1, HEADS, HD),
        "sg_w": d_sgw[None],
        "sg_b": jnp.transpose(d_sgb[:, 0:HEADS])[None],
        "pool_scale": d_ps,
        "final_norm": d_fn.reshape(D),
    }
    land00_in, land00_out, land_small = _comm_call(_Comm("exchange", g00, repl=[_pack_small(g_small)]), "exchange_last")

    res = {}
    res["ffn1_w_in"] = _adamw([land00_in, land01[0]], ffn1_w_in, m_ffn1_w_in, v_ffn1_w_in, 128, "adamw_w_in")
    res["ffn2_w_in"] = _adamw([land10[0], land11[0]], ffn2_w_in, m_ffn2_w_in, v_ffn2_w_in, 128, "adamw_w_in")
    res["ffn1_w_out"] = _adamw([land00_out, land01[1]], ffn1_w_out, m_ffn1_w_out, v_ffn1_w_out, 176, "adamw_w_out")
    res["ffn2_w_out"] = _adamw([land10[1], land11[1]], ffn2_w_out, m_ffn2_w_out, v_ffn2_w_out, 176, "adamw_w_out")
    res["ab_w_in"] = _adamw([land_ab[0]], ab_w_in, m_ab_w_in, v_ab_w_in, 256, "adamw_ab_w_in")
    res["ab_w_out"] = _adamw([land_ab[1]], ab_w_out, m_ab_w_out, v_ab_w_out, D // N_DEV, "adamw_ab_w_out")
    res["pool_w"] = _adamw([land01[2]], pool_w, m_pool_w, v_pool_w, 4, "adamw_pool_w")
    sm = _adamw([land_small], _pack_small(wl)[None], _pack_small(ml)[None], _pack_small(vl)[None], R_SMALL,
                "adamw_replicated")
    sm = [_unpack_small(a[0]) for a in sm]
    for n in wl:
        res[n] = [d[n] for d in sm]
    me = 4 * lax.axis_index("x") + 2 * lax.axis_index("y") + lax.axis_index("c")
    g_ps = lax.dynamic_slice(sm[0]["pool_scale"], (0, me * (D // N_DEV)), (1, D // N_DEV))
    g_cw = lax.dynamic_slice(sm[0]["dn_conv_w"], (0, 0, me * (QKV // N_DEV)), (1, 4, QKV // N_DEV))
    s2 = _adamw([_pack_small_shard(g_ps, g_cw)[None]], _pack_small_shard(pool_scale, dn_conv_w)[None],
                _pack_small_shard(m_pool_scale, m_dn_conv_w)[None], _pack_small_shard(v_pool_scale, v_dn_conv_w)[None],
                8, "adamw_small_sharded")
    res["pool_scale"] = [a[0, 0:1, 0:D // N_DEV] for a in s2]
    res["dn_conv_w"] = [a[0, 1:5, 0:QKV // N_DEV][None] for a in s2]

    loss = lax.psum(loss_local[0, 0], ("x", "y", "c"))
    result = [loss, grad_x[None]]
    for i in range(4):
        result += [res[n][i] for n in WEIGHT_ORDER]
    return tuple(result)
```

```python
import jax
import jax.numpy as jnp
from jax import lax
from jax.experimental import pallas as pl
from jax.experimental.pallas import tpu as pltpu

F32 = jnp.float32
_MM = jnp.bfloat16

D = 1024
FF = 2816
EPS = 1e-6
HEADS = 4
HD = 128
DNC = 64
DN_STEP = 2
SGC = 128
QKV = 3 * HEADS * HD
HW = HEADS * HD
POOL_WINDOWS = (2, 4, 8, 16)
PG = D // 4
HALO = 16
N_DEV = 8
AB_IN = 3080
ROW = 1024

TM = 512
BT = 1024
FC = 704
NJ = FF // FC
WO_ROWS = FF // N_DEV

ADAM_LR, ADAM_B1, ADAM_B2, ADAM_EPS, ADAM_WD, ADAM_STEP = 0.001, 0.9, 0.999, 1e-08, 0.01, 10

MESH_T = pl.DeviceIdType.MESH
NN = (((1,), (0,)), ((), ()))
NT = (((1,), (1,)), ((), ()))
TN = (((0,), (0,)), ((), ()))


def _c(a):
    return a.astype(_MM)


def _dg(a, b, dims):
    return lax.dot_general(a, b, dims, preferred_element_type=F32)


def _dot(a, b):
    return _dg(a, b, NN)


def _dot_nt(a, b):
    return _dg(a, b, NT)


def _dot_tn(a, b):
    return _dg(a, b, TN)


def _split2(a):
    hi = _c(a)
    return hi, _c(a - hi.astype(F32))


def _dot3(a, b, dims=NN):
    ah, al = _split2(a)
    bh, bl = _split2(b)
    return _dg(ah, bh, dims) + (_dg(ah, bl, dims) + _dg(al, bh, dims))


def _mask_dot(mask, x):
    x1 = _c(x)
    r = x - x1.astype(F32)
    x2 = _c(r)
    x3 = _c(r - x2.astype(F32))
    return _dot(mask, x1) + (_dot(mask, x2) + _dot(mask, x3))


def _sigmoid(x):
    return jax.nn.sigmoid(x)


def _gelu(x):
    return 0.5 * x * (1.0 + lax.erf(x * 0.7071067811865476))


def _gelu_grad(x):
    return 0.5 * (1.0 + lax.erf(x * 0.7071067811865476)) + x * jnp.exp(-0.5 * x * x) * 0.3989422804014327


def _accum(ref, val, step):
    @pl.when(step == 0)
    def _():
        ref[...] = val

    @pl.when(step > 0)
    def _():
        ref[...] += val


def _rstd(x):
    return lax.rsqrt(jnp.mean(x * x, axis=-1, keepdims=True) + EPS)


def _rms_bwd(dy, xhat, r, nw):
    dnw = jnp.sum(dy * xhat, axis=0, keepdims=True)
    dxh = dy * nw
    dx = r * (dxh - xhat * jnp.mean(dxh * xhat, axis=-1, keepdims=True))
    return dx, dnw


def _numel(shape):
    n = 1
    for s in shape:
        n *= s
    return n


def _peer(k, x, y, c):
    px = 1 - x if k & 4 else x
    py = 1 - y if k & 2 else y
    pc = 1 - c if k & 1 else c
    return px, py, pc


class _Comm:
    def __init__(self, kind, arrs, repl=()):
        self.kind = kind
        self.ns = len(arrs)
        self.arrs = list(arrs) + list(repl)
        self.na = len(self.arrs)

    @property
    def out_shape(self):
        out = []
        for i, a in enumerate(self.arrs):
            lead = (N_DEV,) if (self.kind == "gather" or i >= self.ns) else ()
            out.append(jax.ShapeDtypeStruct(lead + a.shape, a.dtype))
        return out

    @property
    def scratch(self):
        return [pltpu.SemaphoreType.DMA((7 * self.na,)), pltpu.SemaphoreType.DMA((7 * self.na,)),
                pltpu.SemaphoreType.DMA((self.na,))]

    def phases(self, ins, outs, sems):
        send_sems, recv_sems, local_sems = sems
        na = self.na
        x, y, c = lax.axis_index("x"), lax.axis_index("y"), lax.axis_index("c")
        if self.kind == "gather":
            me, sibling = (x, y, c), (x, y, 1 - c)
            chips = [(1 - x, y), (x, 1 - y), (1 - x, 1 - y)]

            def slot(a, px, py, pc):
                return outs[a].at[4 * px + 2 * py + pc]

            def copy(a, k, block, to, src=None):
                return pltpu.make_async_remote_copy(
                    src_ref=slot(a, *block) if src is None else src, dst_ref=slot(a, *block),
                    send_sem=send_sems.at[7 * a + k], recv_sem=recv_sems.at[7 * a + k],
                    device_id=to, device_id_type=MESH_T)

            mine = [pltpu.make_async_copy(ins[a], slot(a, *me), local_sems.at[a]) for a in range(na)]
            first, passed = [], []
            for a in range(na):
                first.append(copy(a, 0, me, sibling, src=ins[a]))
                first += [copy(a, 1 + j, me, (*chip, c), src=ins[a]) for j, chip in enumerate(chips)]
                passed += [copy(a, 4 + j, (*chip, c), sibling) for j, chip in enumerate(chips)]

            def start():
                for cp in mine + first:
                    cp.start()

            def middle():
                for a in range(na):
                    for j, chip in enumerate(chips):
                        copy(a, 1 + j, (*chip, c), me).wait_recv()
                        passed[3 * a + j].start()

            def finish():
                for a in range(na):
                    copy(a, 0, sibling, me).wait_recv()
                    for j, chip in enumerate(chips):
                        copy(a, 4 + j, (*chip, 1 - c), me).wait_recv()
                for cp in first + passed:
                    cp.wait_send()
                for cp in mine:
                    cp.wait()

            return start, middle, finish

        me = 4 * x + 2 * y + c
        ns = self.ns
        own = [pltpu.make_async_copy(ins[a].at[me] if a < ns else ins[a], outs[a].at[me], local_sems.at[a])
               for a in range(na)]
        copies = []
        for k in range(1, N_DEV):
            px, py, pc = _peer(k, x, y, c)
            peer = 4 * px + 2 * py + pc
            for a in range(na):
                copies.append(pltpu.make_async_remote_copy(
                    src_ref=ins[a].at[peer] if a < ns else ins[a], dst_ref=outs[a].at[me],
                    send_sem=send_sems.at[na * (k - 1) + a], recv_sem=recv_sems.at[na * (k - 1) + a],
                    device_id=(px, py, pc), device_id_type=MESH_T))

        def start():
            for cp in own + copies:
                cp.start()

        def middle():
            pass

        def finish():
            for cp in copies:
                cp.wait()
            for cp in own:
                cp.wait()

        return start, middle, finish


def _comm_call(comm, name):
    na = comm.na

    def body(*refs):
        start, middle, finish = comm.phases(refs[0:na], refs[na:2 * na], refs[2 * na:])
        start()
        middle()
        finish()

    hbm = pl.BlockSpec(memory_space=pltpu.HBM)
    return pl.pallas_call(
        body, out_shape=comm.out_shape, in_specs=[hbm] * na, out_specs=[hbm] * na, scratch_shapes=comm.scratch,
        name=name)(*comm.arrs)


def _carried_call(body, comm, n_in, n_out, n_scr, when, *, grid, in_specs, out_specs, out_shape, scratch_shapes,
                  operands, name):
    if comm is None:
        return pl.pallas_call(body, grid=grid, in_specs=in_specs, out_specs=out_specs, out_shape=out_shape,
                              scratch_shapes=scratch_shapes, name=name)(*operands), []
    na = comm.na

    def both(*refs):
        a = n_in + na
        b = a + n_out + na
        body(*refs[0:n_in], *refs[a:a + n_out], *refs[b:b + n_scr])
        start, middle, finish = comm.phases(refs[n_in:a], refs[a + n_out:b], refs[b + n_scr:])
        first, mid, last = when()
        pl.when(first)(start)
        pl.when(mid)(middle)
        pl.when(last)(finish)

    hbm = pl.BlockSpec(memory_space=pltpu.HBM)
    res = pl.pallas_call(
        both, grid=grid, in_specs=list(in_specs) + [hbm] * na, out_specs=list(out_specs) + [hbm] * na,
        out_shape=list(out_shape) + comm.out_shape, scratch_shapes=list(scratch_shapes) + comm.scratch,
        name=name)(*operands, *comm.arrs)
    return res[0:n_out], res[n_out:]


def _ffn_w_specs():
    return [
        pl.BlockSpec((None, D, FC), lambda i, j: (j, 0, 0)),
        pl.BlockSpec((None, D, FC), lambda i, j: (j + NJ, 0, 0)),
        pl.BlockSpec((2, WO_ROWS, D), lambda i, j: (j, 0, 0)),
    ]


def _ffn_when(nt):
    def when():
        i, j = pl.program_id(0), pl.program_id(1)
        return ((i == 0) & (j == 0), (i == (3 * nt) // 4) & (j == 0), (i == nt - 1) & (j == NJ - 1))
    return when


def _ffn_fwd(x, nw, w_in, w_out, comm=None):
    t = x.shape[0]
    nt = t // TM

    def body(x_ref, nw_ref, wg_ref, wu_ref, wo3_ref, o_ref, gu_ref, xn_sc, acc_sc):
        j = pl.program_id(1)

        @pl.when(j == 0)
        def _():
            xv = x_ref[...]
            xn_sc[...] = _c(xv * _rstd(xv) * nw_ref[...])
            acc_sc[...] = jnp.zeros_like(acc_sc)

        xn = xn_sc[...]
        g = _dot(xn, wg_ref[...])
        u = _dot(xn, wu_ref[...])
        gu_ref[0] = _c(g)
        gu_ref[1] = _c(u)
        acc_sc[...] += _dot(_c(g * _sigmoid(g) * u), wo3_ref[...].reshape(FC, D))

        @pl.when(j == NJ - 1)
        def _():
            o_ref[...] = x_ref[...] + 0.5 * acc_sc[...]

    (out, gu), landed = _carried_call(
        body, comm, 5, 2, 2, _ffn_when(nt),
        grid=(nt, NJ),
        in_specs=[pl.BlockSpec((TM, D), lambda i, j: (i, 0)), pl.BlockSpec((1, D), lambda i, j: (0, 0))]
        + _ffn_w_specs(),
        out_specs=[pl.BlockSpec((TM, D), lambda i, j: (i, 0)),
                   pl.BlockSpec((None, 2, TM, FC), lambda i, j: (j, 0, i, 0))],
        out_shape=[jax.ShapeDtypeStruct((t, D), F32), jax.ShapeDtypeStruct((NJ, 2, t, FC), _MM)],
        scratch_shapes=[pltpu.VMEM((TM, D), _MM), pltpu.VMEM((TM, D), F32)],
        operands=(x, nw, w_in, w_in, w_out), name="ffn_fwd")
    return out, gu, landed


def _ffn_bwd(x, nw, w_in, w_out, gu, dy, comm=None):
    t = x.shape[0]
    nt = t // TM

    def body(x_ref, nw_ref, wg_ref, wu_ref, wo3_ref, gu_ref, dy_ref, dx_ref, xn_ref, a_ref, dh_ref, dnw_ref,
             r_sc, dyb_sc, acc_sc):
        wo = wo3_ref[...].reshape(FC, D)
        i = pl.program_id(0)
        j = pl.program_id(1)

        @pl.when(j == 0)
        def _():
            xv = x_ref[...]
            r = _rstd(xv)
            r_sc[...] = r
            xn_ref[...] = _c(xv * r * nw_ref[...])
            dyb_sc[...] = _c(0.5 * dy_ref[...])
            acc_sc[...] = jnp.zeros_like(acc_sc)

        g = gu_ref[0].astype(F32)
        u = gu_ref[1].astype(F32)
        s = _sigmoid(g)
        sl = g * s
        a_ref[...] = _c(sl * u)
        da = _dot_nt(dyb_sc[...], wo)
        dg = _c(da * u * (s * (1.0 + g * (1.0 - s))))
        du = _c(da * sl)
        dh_ref[0] = dg
        dh_ref[1] = du
        acc_sc[...] += _dot_nt(dg, wg_ref[...]) + _dot_nt(du, wu_ref[...])

        @pl.when(j == NJ - 1)
        def _():
            r = r_sc[...]
            dx, dnw = _rms_bwd(acc_sc[...], x_ref[...] * r, r, nw_ref[...])
            dx_ref[...] = dy_ref[...] + dx
            _accum(dnw_ref, dnw, i)

    return _carried_call(
        body, comm, 7, 5, 3, _ffn_when(nt),
        grid=(nt, NJ),
        in_specs=[pl.BlockSpec((TM, D), lambda i, j: (i, 0)), pl.BlockSpec((1, D), lambda i, j: (0, 0))]
        + _ffn_w_specs() + [pl.BlockSpec((None, 2, TM, FC), lambda i, j: (j, 0, i, 0)),
                            pl.BlockSpec((TM, D), lambda i, j: (i, 0))],
        out_specs=[
            pl.BlockSpec((TM, D), lambda i, j: (i, 0)),
            pl.BlockSpec((TM, D), lambda i, j: (i, 0)),
            pl.BlockSpec((None, TM, FC), lambda i, j: (j, i, 0)),
            pl.BlockSpec((None, 2, TM, FC), lambda i, j: (j, 0, i, 0)),
            pl.BlockSpec((1, D), lambda i, j: (0, 0)),
        ],
        out_shape=[
            jax.ShapeDtypeStruct((t, D), F32),
            jax.ShapeDtypeStruct((t, D), _MM),
            jax.ShapeDtypeStruct((NJ, t, FC), _MM),
            jax.ShapeDtypeStruct((NJ, 2, t, FC), _MM),
            jax.ShapeDtypeStruct((1, D), F32),
        ],
        scratch_shapes=[pltpu.VMEM((TM, 1), F32), pltpu.VMEM((TM, D), _MM), pltpu.VMEM((TM, D), F32)],
        operands=(x, nw, w_in, w_in, w_out, gu, dy), name="ffn_bwd")


def _mm_tn(a, b, bm, bn, bt, out_dtype, name):
    t, m = a.shape
    n = b.shape[1]
    nt = t // bt

    def body(a_ref, b_ref, o_ref, acc_sc):
        k = pl.program_id(2)
        _accum(acc_sc, _dot_tn(_c(a_ref[...]), _c(b_ref[...])), k)

        @pl.when(k == nt - 1)
        def _():
            o_ref[...] = acc_sc[...].astype(out_dtype)

    return pl.pallas_call(
        body,
        grid=(m // bm, n // bn, nt),
        in_specs=[pl.BlockSpec((bt, bm), lambda i, j, k: (k, i)), pl.BlockSpec((bt, bn), lambda i, j, k: (k, j))],
        out_specs=pl.BlockSpec((bm, bn), lambda i, j, k: (i, j)),
        out_shape=jax.ShapeDtypeStruct((m, n), out_dtype),
        scratch_shapes=[pltpu.VMEM((bm, bn), F32)],
        name=name,
    )(a, b)


def _mm_tn_win(xn, dh, comm=None):
    t = xn.shape[0]
    bt = min(BT, t)
    nt = t // bt

    def body(a_ref, b_ref, o_ref, acc_sc):
        k = pl.program_id(2)
        _accum(acc_sc, _dot_tn(a_ref[...], b_ref[...]), k)

        @pl.when(k == nt - 1)
        def _():
            o_ref[...] = _c(acc_sc[...])

    def when():
        h, j, k = pl.program_id(0), pl.program_id(1), pl.program_id(2)
        start = (h == 0) & (j == 0) & (k == 0)
        return start, start, (h == 1) & (j == NJ - 1) & (k == nt - 1)

    (out,), landed = _carried_call(
        body, comm, 2, 1, 1, when,
        grid=(2, NJ, nt),
        in_specs=[pl.BlockSpec((bt, D), lambda h, j, k: (k, 0)),
                  pl.BlockSpec((None, None, bt, FC), lambda h, j, k: (j, h, k, 0))],
        out_specs=[pl.BlockSpec((None, D, FC), lambda h, j, k: (h * NJ + j, 0, 0))],
        out_shape=[jax.ShapeDtypeStruct((N_DEV, D, FC), _MM)],
        scratch_shapes=[pltpu.VMEM((D, FC), F32)],
        operands=(xn, dh), name="mm_tn_win")
    return out, landed


def _mm_tn_wout(act, dy):
    t = dy.shape[0]
    bt = min(BT, t)
    nt = t // bt

    def body(a_ref, b_ref, o_ref, acc_sc):
        k = pl.program_id(1)
        _accum(acc_sc, _dot_tn(a_ref[...], _c(b_ref[...])), k)

        @pl.when(k == nt - 1)
        def _():
            o_ref[...] = _c((0.5 * acc_sc[...]).reshape(2, WO_ROWS, D))

    return pl.pallas_call(
        body,
        grid=(NJ, nt),
        in_specs=[pl.BlockSpec((None, bt, FC), lambda j, k: (j, k, 0)), pl.BlockSpec((bt, D), lambda j, k: (k, 0))],
        out_specs=pl.BlockSpec((2, WO_ROWS, D), lambda j, k: (j, 0, 0)),
        out_shape=jax.ShapeDtypeStruct((N_DEV, WO_ROWS, D), _MM),
        scratch_shapes=[pltpu.VMEM((FC, D), F32)],
        name="mm_tn_wout",
    )(act, dy)


def _loss_head(x, nw, tgt):
    t = x.shape[0]

    def body(x_ref, nw_ref, t_ref, loss_ref, dx_ref, dnw_ref):
        i = pl.program_id(0)
        xv = x_ref[...]
        r = _rstd(xv)
        xh = xv * r
        e = xh * nw_ref[...] - t_ref[...]
        part = 0.5 * jnp.sum(jnp.mean(e * e, axis=-1, keepdims=True), axis=0, keepdims=True)
        _accum(loss_ref, jnp.broadcast_to(part, (1, 128)), i)
        dx, dnw = _rms_bwd(e * (1.0 / D), xh, r, nw_ref[...])
        dx_ref[...] = dx
        _accum(dnw_ref, dnw, i)

    return pl.pallas_call(
        body,
        grid=(t // TM,),
        in_specs=[pl.BlockSpec((TM, D), lambda i: (i, 0)), pl.BlockSpec((1, D), lambda i: (0, 0)),
                  pl.BlockSpec((TM, D), lambda i: (i, 0))],
        out_specs=[pl.BlockSpec((1, 128), lambda i: (0, 0)), pl.BlockSpec((TM, D), lambda i: (i, 0)),
                   pl.BlockSpec((1, D), lambda i: (0, 0))],
        out_shape=[jax.ShapeDtypeStruct((1, 128), F32), jax.ShapeDtypeStruct((t, D), F32),
                   jax.ShapeDtypeStruct((1, D), F32)],
        name="loss_head",
    )(x, nw, tgt)


PW_F = QKV + 5 * HW
PW_B = QKV + 3 * HW + 128


def _ab_proj(x1, nw, wab):
    t = x1.shape[0]

    def body(x_ref, nw_ref, w_ref, h_ref, qkv_ref, z_ref, su_ref, sv_ref, b_ref, a_ref):
        xv = x_ref[...]
        h = _c(xv * _rstd(xv) * nw_ref[...])
        h_ref[...] = h
        p = _dot(h, w_ref[...])
        qkv_ref[...] = p[:, 0:QKV]
        o = QKV
        for ref in (z_ref, su_ref, sv_ref, b_ref, a_ref):
            ref[...] = p[:, o:o + HW]
            o += HW

    row = lambda w: pl.BlockSpec((TM, w), lambda i: (i, 0))
    return pl.pallas_call(
        body,
        grid=(t // TM,),
        in_specs=[row(D), pl.BlockSpec((1, D), lambda i: (0, 0)), pl.BlockSpec((D, PW_F), lambda i: (0, 0))],
        out_specs=[row(D), row(QKV)] + [row(HW)] * 5,
        out_shape=[jax.ShapeDtypeStruct((t, D), _MM), jax.ShapeDtypeStruct((t, QKV), F32)]
        + [jax.ShapeDtypeStruct((t, HW), F32)] * 5,
        name="ab_proj",
    )(x1, nw, wab)


def _conv_rows(x, halo, cw):
    xe = jnp.concatenate([halo, x], axis=0)
    shifted = []
    c = None
    for k in range(4):
        s = 3 - k
        xs = (xe if s == 0 else pltpu.roll(xe, s, 0))[8:, :]
        shifted.append(xs)
        term = cw[k:k + 1, :] * xs
        c = term if c is None else c + term
    return c, shifted


def _head_rsq(a):
    parts = []
    for h in range(HEADS):
        ah = a[:, h * HD:(h + 1) * HD]
        r = lax.rsqrt(jnp.sum(ah * ah, axis=-1, keepdims=True) + EPS)
        parts.append(jnp.broadcast_to(r, ah.shape))
    return jnp.concatenate(parts, axis=-1)


def _head_sum(a):
    parts = []
    for h in range(HEADS):
        ah = a[:, h * HD:(h + 1) * HD]
        parts.append(jnp.broadcast_to(jnp.sum(ah, axis=-1, keepdims=True), ah.shape))
    return jnp.concatenate(parts, axis=-1)


def _softplus(x):
    return jnp.maximum(x, 0.0) + jnp.log1p(jnp.exp(-jnp.abs(x)))


def _halo_prev_spec(width, rows):
    per = TM // rows
    return pl.BlockSpec((rows, width), lambda i: (jnp.maximum(i * per - 1, 0), 0))


def _halo_next_spec(width, rows, t):
    per = TM // rows
    last = t // rows - 1
    return pl.BlockSpec((rows, width), lambda i: (jnp.minimum((i + 1) * per, last), 0))


def _dn_pre(qkv, b_rep, a_rep, cw, alog, dtb):
    t = qkv.shape[0]
    qscale = HD ** -0.5

    def body(x_ref, halo_ref, b_ref, a_ref, cw_ref, alog_ref, dt_ref, q_ref, k_ref, v_ref, beta_ref, g_ref):
        i = pl.program_id(0)
        halo = jnp.where(i == 0, 0.0, halo_ref[...])
        c, _ = _conv_rows(x_ref[...], halo, cw_ref[...])
        sc = c * _sigmoid(c)
        q = sc[:, 0:HW]
        k = sc[:, HW:2 * HW]
        q_ref[...] = q * _head_rsq(q) * qscale
        k_ref[...] = k * _head_rsq(k)
        v_ref[...] = sc[:, 2 * HW:]
        beta_ref[...] = _sigmoid(b_ref[...])
        g_ref[...] = -jnp.exp(alog_ref[...]) * _softplus(a_ref[...] + dt_ref[...])

    row = lambda w: pl.BlockSpec((TM, w), lambda i: (i, 0))
    full = lambda a: pl.BlockSpec(a.shape, lambda i: (0,) * a.ndim)
    return pl.pallas_call(
        body,
        grid=(t // TM,),
        in_specs=[row(QKV), _halo_prev_spec(QKV, 8), row(HW), row(HW), full(cw), full(alog), full(dtb)],
        out_specs=[row(HW)] * 5,
        out_shape=[jax.ShapeDtypeStruct((t, HW), F32)] * 5,
        name="dn_pre",
    )(qkv, qkv, b_rep, a_rep, cw, alog, dtb)


def _unit_lower_inv(los, eye):
    ps = [eye - lo for lo in los]
    lps = list(los)
    for _ in range(5):
        lps = [_dot(_c(lp), _c(lp)) for lp in lps]
        ps = [p + _dot(_c(p), _c(lp)) for p, lp in zip(ps, lps)]
    rs = [eye - (p + _dot3(lo, p)) for lo, p in zip(los, ps)]
    return [p + _dot(_c(p), _c(r)) for p, r in zip(ps, rs)]


def _dn_masks():
    ri = lax.broadcasted_iota(jnp.int32, (DNC, DNC), 0)
    ci = lax.broadcasted_iota(jnp.int32, (DNC, DNC), 1)
    return dict(strict=ri > ci, causal=ri >= ci, eye=(ri == ci).astype(F32),
                ltri=_c((ri >= ci).astype(F32)), upper=_c((ri <= ci).astype(F32)))


def _dn_decay(gr, mk):
    rhs = jnp.concatenate([gr, jnp.where(mk["strict"], gr[:, 0:DNC], 0.0)], axis=1)
    cs = _mask_dot(mk["ltri"], rhs)
    gc = cs[:, 0:HD]
    dm = jnp.where(mk["causal"], jnp.exp(cs[:, HD:HD + DNC]), 0.0)
    gl = jnp.sum(gr, axis=0, keepdims=True)
    return dm, jnp.exp(gc), jnp.exp(gl - gc), gl


def _dn_when(n):
    def when():
        i = pl.program_id(0)
        return (i == 0, i == n // 2, i == n - 1)
    return when


def _dn_fwd(q, k, v, beta, g, comm=None):
    t = q.shape[0]
    rows = DN_STEP * DNC
    n = t // rows

    def body(q_ref, k_ref, v_ref, b_ref, g_ref, o_ref, sall_ref, aall_ref, u_ref, w_ref, s_sc):
        i = pl.program_id(0)

        @pl.when(i == 0)
        def _():
            s_sc[...] = jnp.zeros_like(s_sc)

        mk = _dn_masks()
        idx = [(cc, h) for cc in range(DN_STEP) for h in range(HEADS)]
        at = lambda cc, h: (slice(cc * DNC, (cc + 1) * DNC), slice(h * HD, (h + 1) * HD))
        qs = [q_ref[at(*i)] for i in idx]
        ks = [k_ref[at(*i)] for i in idx]
        bs = [b_ref[at(*i)] for i in idx]
        dec = [_dn_decay(g_ref[at(*i)], mk) for i in idx]
        kbs = [k_ * b_ for k_, b_ in zip(ks, bs)]
        los = [jnp.where(mk["strict"], _dot_nt(_c(kb), _c(k_)) * d[0], 0.0) for kb, k_, d in zip(kbs, ks, dec)]
        inv = _unit_lower_inv(los, mk["eye"])
        uws = [_dot3(a, jnp.concatenate([v_ref[at(*i)] * b_, kb * d[1]], axis=1))
               for a, i, b_, kb, d in zip(inv, idx, bs, kbs, dec)]
        attn = [_c(_dot_nt(_c(q_), _c(k_)) * d[0]) for q_, k_, d in zip(qs, ks, dec)]
        for n_, (cc, h) in enumerate(idx):
            aall_ref[cc, h] = inv[n_]
            u_ref[at(cc, h)] = uws[n_][:, 0:HD]
            w_ref[at(cc, h)] = uws[n_][:, HD:]
        ss = [s_sc[h] for h in range(HEADS)]
        for cc in range(DN_STEP):
            base = cc * HEADS
            for h in range(HEADS):
                sall_ref[cc, h] = ss[h]
            ws = [_dot(_c(jnp.concatenate([uws[base + h][:, HD:], qs[base + h] * dec[base + h][1]], axis=0)),
                       _c(ss[h])) for h in range(HEADS)]
            vn = [_c(uws[base + h][:, 0:HD] - ws[h][0:DNC]) for h in range(HEADS)]
            for h in range(HEADS):
                o_ref[at(cc, h)] = ws[h][DNC:] + _dot(attn[base + h], vn[h])
            ss = [ss[h] * jnp.exp(dec[base + h][3]) + _dot_tn(_c(ks[base + h] * dec[base + h][2]), vn[h])
                  for h in range(HEADS)]
        for h in range(HEADS):
            s_sc[h] = ss[h]

    row = pl.BlockSpec((rows, HW), lambda i: (i, 0))
    return _carried_call(
        body, comm, 5, 5, 1, _dn_when(n),
        grid=(n,),
        in_specs=[row] * 5,
        out_specs=[row, pl.BlockSpec((DN_STEP, HEADS, HD, HD), lambda i: (i, 0, 0, 0)),
                   pl.BlockSpec((DN_STEP, HEADS, DNC, DNC), lambda i: (i, 0, 0, 0)), row, row],
        out_shape=[jax.ShapeDtypeStruct((t, HW), F32), jax.ShapeDtypeStruct((t // DNC, HEADS, HD, HD), F32),
                   jax.ShapeDtypeStruct((t // DNC, HEADS, DNC, DNC), F32), jax.ShapeDtypeStruct((t, HW), F32),
                   jax.ShapeDtypeStruct((t, HW), F32)],
        scratch_shapes=[pltpu.VMEM((HEADS, HD, HD), F32)],
        operands=(q, k, v, beta, g), name="dn_fwd")


def _dn_bwd(q, k, v, beta, g, sall, aall, u, w, do, comm=None):
    t = q.shape[0]
    rows = DN_STEP * DNC
    n = t // rows

    def body(q_ref, k_ref, v_ref, b_ref, g_ref, sall_ref, aall_ref, u_ref, w_ref, do_ref,
             dq_ref, dk_ref, dv_ref, db_ref, dg_ref, ds_sc):
        i = pl.program_id(0)

        @pl.when(i == 0)
        def _():
            ds_sc[...] = jnp.zeros_like(ds_sc)

        mk = _dn_masks()
        strict = mk["strict"]
        hs = range(HEADS)
        at = lambda cc, h: (slice(cc * DNC, (cc + 1) * DNC), slice(h * HD, (h + 1) * HD))
        rowsum = lambda a: jnp.sum(a, axis=-1, keepdims=True)
        dsn = [ds_sc[h] for h in hs]
        for cc in reversed(range(DN_STEP)):
            q = [q_ref[at(cc, h)] for h in hs]
            k = [k_ref[at(cc, h)] for h in hs]
            b = [b_ref[at(cc, h)] for h in hs]
            u = [u_ref[at(cc, h)] for h in hs]
            w = [w_ref[at(cc, h)] for h in hs]
            do = [do_ref[at(cc, h)] for h in hs]
            s = [sall_ref[cc, h] for h in hs]
            dec = [_dn_decay(g_ref[at(cc, h)], mk) for h in hs]
            dm, e, f = [d[0] for d in dec], [d[1] for d in dec], [d[2] for d in dec]
            egl = [jnp.exp(d[3]) for d in dec]
            kb = [k[h] * b[h] for h in hs]
            kc = [_c(k[h]) for h in hs]
            sb = [_c(s[h]) for h in hs]
            dob = [_c(do[h]) for h in hs]
            m = [_dot_nt(_c(kb[h]), kc[h]) for h in hs]
            p = [_dot_nt(_c(q[h]), kc[h]) for h in hs]
            vnb = [_c(u[h] - _dot(_c(w[h]), sb[h])) for h in hs]
            dsb = [_c(dsn[h]) for h in hs]
            dvn = [_dot_tn(_c(p[h] * dm[h]), dob[h]) + _dot(_c(k[h] * f[h]), dsb[h]) for h in hs]
            dov = [_c(jnp.concatenate([do[h], dvn[h]], axis=0)) for h in hs]
            t1 = [_dot_nt(dov[h], sb[h]) for h in hs]
            dattn = [_dot_nt(dob[h], vnb[h]) for h in hs]
            dkt = [_dot_nt(vnb[h], dsb[h]) for h in hs]
            dgl = [jnp.sum(jnp.sum(dsn[h] * s[h], axis=1, keepdims=True), axis=0, keepdims=True) * egl[h][:, 0:1]
                   for h in hs]
            dsn = [dsn[h] * egl[h] + _dot_tn(_c(jnp.concatenate([q[h] * e[h], -w[h]], axis=0)), dov[h]) for h in hs]
            dqd = [t1[h][0:DNC] for h in hs]
            dw = [-t1[h][DNC:] for h in hs]
            ab = [_dot3(aall_ref[cc, h], jnp.concatenate([dvn[h], dw[h]], axis=1), TN) for h in hs]
            dlo = [jnp.where(strict, -_dot3(ab[h], jnp.concatenate([u[h], w[h]], axis=1), NT), 0.0) for h in hs]
            dpm = [_c(jnp.concatenate([dattn[h] * dm[h], dlo[h] * dm[h]], axis=0)) for h in hs]
            t2 = [_dot(dpm[h], kc[h]) for h in hs]
            t4 = [_dot_tn(dpm[h], _c(jnp.concatenate([q[h], kb[h]], axis=0))) for h in hs]
            dff = [rowsum(dkt[h] * k[h]) * f[h][:, 0:1] for h in hs]
            de = [rowsum(dqd[h] * q[h]) + rowsum(ab[h][:, HD:] * kb[h]) for h in hs]
            dd = [(dattn[h] * p[h] + dlo[h] * m[h]) * dm[h] for h in hs]
            t3 = [_mask_dot(mk["upper"], jnp.concatenate(
                [jnp.broadcast_to(de[h] * e[h][:, 0:1] - dff[h], (DNC, HD)), dd[h]], axis=1)) for h in hs]
            for h in hs:
                dvb, dkbe = ab[h][:, 0:HD], ab[h][:, HD:]
                dkb = t2[h][DNC:] + dkbe * e[h]
                dbeta = rowsum(dkb * k[h]) + rowsum(dvb * v_ref[at(cc, h)])
                dg = (rowsum(jnp.where(strict, t3[h][:, HD:HD + DNC], 0.0)) + t3[h][:, 0:1]
                      + dgl[h] + jnp.sum(dff[h], axis=0, keepdims=True))
                dq_ref[at(cc, h)] = dqd[h] * e[h] + t2[h][0:DNC]
                dk_ref[at(cc, h)] = t4[h] + dkt[h] * f[h] + dkb * b[h]
                dv_ref[at(cc, h)] = dvb * b[h]
                db_ref[at(cc, h)] = jnp.broadcast_to(dbeta, (DNC, HD))
                dg_ref[at(cc, h)] = jnp.broadcast_to(dg, (DNC, HD))
        for h in hs:
            ds_sc[h] = dsn[h]

    row = pl.BlockSpec((rows, HW), lambda i: (n - 1 - i, 0))
    return _carried_call(
        body, comm, 10, 5, 1, _dn_when(n),
        grid=(n,),
        in_specs=[row] * 5 + [pl.BlockSpec((DN_STEP, HEADS, HD, HD), lambda i: (n - 1 - i, 0, 0, 0)),
                              pl.BlockSpec((DN_STEP, HEADS, DNC, DNC), lambda i: (n - 1 - i, 0, 0, 0)), row, row, row],
        out_specs=[row] * 5,
        out_shape=[jax.ShapeDtypeStruct((t, HW), F32)] * 5,
        scratch_shapes=[pltpu.VMEM((HEADS, HD, HD), F32)],
        operands=(q, k, v, beta, g, sall, aall, u, w, do), name="dn_bwd")


def _group_norm(a, nw):
    rs = []
    for h in range(HEADS):
        ah = a[:, h * HD:(h + 1) * HD]
        rs.append(jnp.broadcast_to(_rstd(ah), ah.shape))
    r = jnp.concatenate(rs, axis=-1)
    xh = a * r
    return xh * nw, xh, r


def _group_norm_bwd(dy, xh, r, nw):
    dxh = dy * nw
    return r * (dxh - xh * (_head_sum(dxh * xh) * (1.0 / HD)))


def _sg_mix(wt_ref, svn_b, nchunk):
    rows = []
    for cidx in range(nchunk):
        cols = []
        for g in range(HEADS):
            blk = svn_b[cidx * SGC:(cidx + 1) * SGC, g * HD:(g + 1) * HD]
            cols.append(_dot(wt_ref[g], blk))
        rows.append(jnp.concatenate(cols, axis=-1))
    return jnp.concatenate(rows, axis=0)


def _ab_out(x1, o, z, su, sv, dnw, sgnw, wtril, sgb, wout):
    t = x1.shape[0]
    nchunk = TM // SGC

    def body(x_ref, o_ref, z_ref, su_ref, sv_ref, dnw_ref, sgnw_ref, wt_ref, sgb_ref, wo_ref, x2_ref, cat_ref):
        on, _, _ = _group_norm(o_ref[...], dnw_ref[...])
        zv = z_ref[...]
        cat_ref[:, 0:HW] = _c(on * (zv * _sigmoid(zv)))
        svn, _, _ = _group_norm(_gelu(sv_ref[...]), sgnw_ref[...])
        mixed = _sg_mix(wt_ref, _c(svn), nchunk) + jnp.tile(sgb_ref[...], (nchunk, 1))
        cat_ref[:, HW:] = _c(_gelu(su_ref[...]) * mixed)
        x2_ref[...] = x_ref[...] + _dot(cat_ref[...], wo_ref[...])

    row = lambda w: pl.BlockSpec((TM, w), lambda i: (i, 0))
    full = lambda a: pl.BlockSpec(a.shape, lambda i: (0,) * a.ndim)
    return pl.pallas_call(
        body,
        grid=(t // TM,),
        in_specs=[row(D)] + [row(HW)] * 4 + [full(dnw), full(sgnw), full(wtril), full(sgb), full(wout)],
        out_specs=[row(D), row(D)],
        out_shape=[jax.ShapeDtypeStruct((t, D), F32), jax.ShapeDtypeStruct((t, D), _MM)],
        name="ab_out",
    )(x1, o, z, su, sv, dnw, sgnw, wtril, sgb, wout)


def _ab_out_bwd(dx2, o, z, su, sv, dnw, sgnw, wtril, wtril_t, sgb, wout):
    t = dx2.shape[0]
    nchunk = TM // SGC

    def body(dx_ref, o_ref, z_ref, su_ref, sv_ref, dnw_ref, sgnw_ref, wt_ref, wtt_ref, sgb_ref, wo_ref,
             do_ref, dz_ref, dsu_ref, dsv_ref, ddnw_ref, dsgnw_ref, dsgw_ref, dsgb_ref):
        i = pl.program_id(0)
        dcat = _dot_nt(_c(dx_ref[...]), wo_ref[...])
        doa = dcat[:, 0:HW]
        dob = dcat[:, HW:]
        on, oh, ro = _group_norm(o_ref[...], dnw_ref[...])
        zv = z_ref[...]
        sz = _sigmoid(zv)
        dz_ref[...] = _c(doa * on * (sz * (1.0 + zv * (1.0 - sz))))
        don = doa * (zv * sz)
        do_ref[...] = _group_norm_bwd(don, oh, ro, dnw_ref[...])
        dd = jnp.sum(don * oh, axis=0, keepdims=True)
        _accum(ddnw_ref, dd[:, 0:HD] + dd[:, HD:2 * HD] + dd[:, 2 * HD:3 * HD] + dd[:, 3 * HD:], i)
        suv = su_ref[...]
        svv = sv_ref[...]
        svg = _gelu(svv)
        svn, sh, rs = _group_norm(svg, sgnw_ref[...])
        svn_b = _c(svn)
        mixed = _sg_mix(wt_ref, svn_b, nchunk) + jnp.tile(sgb_ref[...], (nchunk, 1))
        dsu_ref[...] = _c(dob * mixed * _gelu_grad(suv))
        dmixed = dob * _gelu(suv)
        dmb = _c(dmixed)
        tri = lax.broadcasted_iota(jnp.int32, (SGC, SGC), 0) >= lax.broadcasted_iota(jnp.int32, (SGC, SGC), 1)
        lane = lax.broadcasted_iota(jnp.int32, (SGC, HD), 1)
        rows = []
        dbias = jnp.zeros((SGC, HD), F32)
        for g in range(HEADS):
            gs = slice(g * HD, (g + 1) * HD)
            dwg = jnp.zeros((SGC, SGC), F32)
            col = jnp.zeros((SGC, 1), F32)
            for cidx in range(nchunk):
                cs = slice(cidx * SGC, (cidx + 1) * SGC)
                dwg = dwg + _dot_nt(dmb[cs, gs], svn_b[cs, gs])
                col = col + jnp.sum(dmixed[cs, gs], axis=-1, keepdims=True)
            _accum(dsgw_ref.at[g], jnp.where(tri, dwg, 0.0), i)
            dbias = dbias + jnp.where(lane == g, col, 0.0)
        _accum(dsgb_ref, dbias, i)
        for cidx in range(nchunk):
            cs = slice(cidx * SGC, (cidx + 1) * SGC)
            rows.append(jnp.concatenate(
                [_dot(wtt_ref[g], dmb[cs, g * HD:(g + 1) * HD]) for g in range(HEADS)], axis=-1))
        dsvn = jnp.concatenate(rows, axis=0)
        _accum(dsgnw_ref, jnp.sum(dsvn * sh, axis=0, keepdims=True), i)
        dsv_ref[...] = _c(_group_norm_bwd(dsvn, sh, rs, sgnw_ref[...]) * _gelu_grad(svv))

    row = lambda w: pl.BlockSpec((TM, w), lambda i: (i, 0))
    full = lambda a: pl.BlockSpec(a.shape, lambda i: (0,) * a.ndim)
    const = lambda shape: pl.BlockSpec(shape, lambda i: (0,) * len(shape))
    return pl.pallas_call(
        body,
        grid=(t // TM,),
        in_specs=[row(D)] + [row(HW)] * 4 + [full(dnw), full(sgnw), full(wtril), full(wtril_t), full(sgb), full(wout)],
        out_specs=[row(HW)] * 4 + [const((1, HD)), const((1, HW)), const((HEADS, SGC, SGC)), const((SGC, HD))],
        out_shape=[jax.ShapeDtypeStruct((t, HW), F32)] + [jax.ShapeDtypeStruct((t, HW), _MM)] * 3
        + [jax.ShapeDtypeStruct((1, HD), F32), jax.ShapeDtypeStruct((1, HW), F32),
           jax.ShapeDtypeStruct((HEADS, SGC, SGC), F32), jax.ShapeDtypeStruct((SGC, HD), F32)],
        name="ab_out_bwd",
    )(dx2, o, z, su, sv, dnw, sgnw, wtril, wtril_t, sgb, wout)


def _dn_pre_bwd(qkv, b_rep, a_rep, cw, alog, dtb, dqn, dkn, dv, dbeta, dg):
    t = qkv.shape[0]
    qscale = HD ** -0.5

    def body(x_ref, halo_ref, b_ref, a_ref, cw_ref, alog_ref, dt_ref, dq_ref, dk_ref, dv_ref, dbeta_ref, dg_ref,
             dc_ref, dba_ref, dcw_ref, dalog_ref, ddt_ref):
        i = pl.program_id(0)
        halo = jnp.where(i == 0, 0.0, halo_ref[...])
        c, shifted = _conv_rows(x_ref[...], halo, cw_ref[...])
        s = _sigmoid(c)
        sc = c * s
        q = sc[:, 0:HW]
        k = sc[:, HW:2 * HW]
        rq = _head_rsq(q)
        rk = _head_rsq(k)
        qu = q * rq
        ku = k * rk
        dqn = dq_ref[...]
        dkn = dk_ref[...]
        dq = qscale * rq * (dqn - qu * _head_sum(dqn * qu))
        dk = rk * (dkn - ku * _head_sum(dkn * ku))
        dsc = jnp.concatenate([dq, dk, dv_ref[...]], axis=-1)
        dc = dsc * (s * (1.0 + c * (1.0 - s)))
        dc_ref[...] = dc
        for kk in range(4):
            _accum(dcw_ref.at[kk], jnp.sum(dc * shifted[kk], axis=0, keepdims=True), i)
        beta = _sigmoid(b_ref[...])
        dbp = dbeta_ref[...] * beta * (1.0 - beta)
        nea = -jnp.exp(alog_ref[...])
        spin = a_ref[...] + dt_ref[...]
        dgv = dg_ref[...]
        dap = dgv * nea * _sigmoid(spin)
        _accum(dalog_ref, jnp.sum(dgv * nea * _softplus(spin), axis=0, keepdims=True), i)
        _accum(ddt_ref, jnp.sum(dap, axis=0, keepdims=True), i)
        lane = lax.broadcasted_iota(jnp.int32, (TM, HD), 1)
        dba = jnp.zeros((TM, HD), F32)
        for h in range(HEADS):
            dba = dba + jnp.where(lane == h, dbp[:, h * HD:(h + 1) * HD], 0.0)
            dba = dba + jnp.where(lane == HEADS + h, dap[:, h * HD:(h + 1) * HD], 0.0)
        dba_ref[...] = _c(dba)

    row = lambda w: pl.BlockSpec((TM, w), lambda i: (i, 0))
    full = lambda a: pl.BlockSpec(a.shape, lambda i: (0,) * a.ndim)
    const = lambda shape: pl.BlockSpec(shape, lambda i: (0,) * len(shape))
    return pl.pallas_call(
        body,
        grid=(t // TM,),
        in_specs=[row(QKV), _halo_prev_spec(QKV, 8), row(HW), row(HW), full(cw), full(alog), full(dtb)] + [row(HW)] * 5,
        out_specs=[row(QKV), row(HD), const((4, 1, QKV)), const((1, HW)), const((1, HW))],
        out_shape=[jax.ShapeDtypeStruct((t, QKV), F32), jax.ShapeDtypeStruct((t, HD), _MM),
                   jax.ShapeDtypeStruct((4, 1, QKV), F32), jax.ShapeDtypeStruct((1, HW), F32),
                   jax.ShapeDtypeStruct((1, HW), F32)],
        name="dn_pre_bwd",
    )(qkv, qkv, b_rep, a_rep, cw, alog, dtb, dqn, dkn, dv, dbeta, dg)


def _conv_bwd(dc, cw):
    t = dc.shape[0]
    nt = t // TM

    def body(dc_ref, halo_ref, cw_ref, dx_ref):
        i = pl.program_id(0)
        halo = jnp.where(i == nt - 1, 0.0, halo_ref[...])
        de = jnp.concatenate([dc_ref[...], halo], axis=0)
        cwv = cw_ref[...]
        acc = None
        for k in range(4):
            s = 3 - k
            ds = (de if s == 0 else pltpu.roll(de, TM + 8 - s, 0))[0:TM, :]
            term = cwv[k:k + 1, :] * ds
            acc = term if acc is None else acc + term
        dx_ref[...] = _c(acc)

    return pl.pallas_call(
        body,
        grid=(nt,),
        in_specs=[pl.BlockSpec((TM, QKV), lambda i: (i, 0)), _halo_next_spec(QKV, 8, t),
                  pl.BlockSpec(cw.shape, lambda i: (0, 0))],
        out_specs=pl.BlockSpec((TM, QKV), lambda i: (i, 0)),
        out_shape=jax.ShapeDtypeStruct((t, QKV), _MM),
        name="conv_bwd",
    )(dc, dc, cw)


def _ab_proj_bwd(x1, nw, dqkv, dz, dsu, dsv, dba, wab_b, dres):
    t = x1.shape[0]

    def body(x_ref, nw_ref, dqkv_ref, dz_ref, dsu_ref, dsv_ref, dba_ref, w_ref, dres_ref, dx_ref, dcat_ref, dnw_ref):
        i = pl.program_id(0)
        dcat_ref[:, 0:QKV] = dqkv_ref[...]
        o = QKV
        for ref in (dz_ref, dsu_ref, dsv_ref):
            dcat_ref[:, o:o + HW] = ref[...]
            o += HW
        dcat_ref[:, o:o + 128] = dba_ref[...]
        dh = _dot_nt(dcat_ref[...], w_ref[...])
        xv = x_ref[...]
        r = _rstd(xv)
        dx, dnw = _rms_bwd(dh, xv * r, r, nw_ref[...])
        dx_ref[...] = dres_ref[...] + dx
        _accum(dnw_ref, dnw, i)

    row = lambda w: pl.BlockSpec((TM, w), lambda i: (i, 0))
    return pl.pallas_call(
        body,
        grid=(t // TM,),
        in_specs=[row(D), pl.BlockSpec((1, D), lambda i: (0, 0)), row(QKV), row(HW), row(HW), row(HW), row(128),
                  pl.BlockSpec((D, PW_B), lambda i: (0, 0)), row(D)],
        out_specs=[row(D), row(PW_B), pl.BlockSpec((1, D), lambda i: (0, 0))],
        out_shape=[jax.ShapeDtypeStruct((t, D), F32), jax.ShapeDtypeStruct((t, PW_B), _MM),
                   jax.ShapeDtypeStruct((1, D), F32)],
        name="ab_proj_bwd",
    )(x1, nw, dqkv, dz, dsu, dsv, dba, wab_b, dres)


def _pool_counts(i):
    pos = (lax.broadcasted_iota(jnp.int32, (TM + HALO, 1), 0) + i * TM + 1).astype(F32)
    return [1.0 / jnp.minimum(pos, float(w)) for w in POOL_WINDOWS]


def _window_sum(ext, win, back):
    r = ext.shape[0]
    s = ext
    step = 1
    while step < win:
        s = s + pltpu.roll(s, step if back else r - step, 0)
        step *= 2
    return s


def _pooled(h_ext, invc, g):
    gs = slice(g * PG, (g + 1) * PG)
    he = h_ext[:, gs]
    ws = _window_sum(he, POOL_WINDOWS[g], True)[HALO:, :]
    return ws * invc[g][0:TM, :] - he[HALO:, :]


def _pool_fwd(x1, nw, pw, scale):
    t = x1.shape[0]

    def body(x_ref, halo_ref, nw_ref, pw_ref, sc_ref, x2_ref):
        i = pl.program_id(0)
        xv = x_ref[...]
        hv = halo_ref[...]
        nwv = nw_ref[...]
        h_ext = jnp.concatenate([jnp.where(i == 0, 0.0, hv * _rstd(hv) * nwv), xv * _rstd(xv) * nwv], axis=0)
        invc = _pool_counts(i)
        outs = [_dot(_c(_pooled(h_ext, invc, g)), pw_ref[g]) for g in range(4)]
        x2_ref[...] = xv + jnp.concatenate(outs, axis=-1) * sc_ref[...]

    return pl.pallas_call(
        body,
        grid=(t // TM,),
        in_specs=[pl.BlockSpec((TM, D), lambda i: (i, 0)), _halo_prev_spec(D, HALO),
                  pl.BlockSpec((1, D), lambda i: (0, 0)), pl.BlockSpec((4, PG, PG), lambda i: (0, 0, 0)),
                  pl.BlockSpec((1, D), lambda i: (0, 0))],
        out_specs=pl.BlockSpec((TM, D), lambda i: (i, 0)),
        out_shape=jax.ShapeDtypeStruct((t, D), F32),
        name="pool_fwd",
    )(x1, x1, nw, pw, scale)


def _pool_bwd(x1, nw, pw, scale, dx2):
    t = x1.shape[0]
    nt = t // TM

    def body(x_ref, halo_ref, nw_ref, pw_ref, sc_ref, dx2_ref, dnext_ref, dx_ref, dnw_ref, dpw_ref, dsc_ref):
        i = pl.program_id(0)
        xv = x_ref[...]
        hv = halo_ref[...]
        nwv = nw_ref[...]
        r = _rstd(xv)
        xh = xv * r
        h_ext = jnp.concatenate([jnp.where(i == 0, 0.0, hv * _rstd(hv) * nwv), xh * nwv], axis=0)
        invc = _pool_counts(i)
        dyv = dx2_ref[...]
        dout_ext = jnp.concatenate([dyv, jnp.where(i == nt - 1, 0.0, dnext_ref[...])], axis=0) * sc_ref[...]
        dh_parts = []
        dsc_parts = []
        for g in range(4):
            gs = slice(g * PG, (g + 1) * PG)
            pooled_b = _c(_pooled(h_ext, invc, g))
            dout_b = _c(dout_ext[:, gs])
            dsc_parts.append(jnp.sum(dyv[:, gs] * _dot(pooled_b, pw_ref[g]), axis=0, keepdims=True))
            _accum(dpw_ref.at[g], _dot_tn(pooled_b, dout_b[0:TM, :]), i)
            dpool_ext = _dot_nt(dout_b, pw_ref[g])
            lead = _window_sum(dpool_ext * invc[g], POOL_WINDOWS[g], False)[0:TM, :]
            dh_parts.append(lead - dpool_ext[0:TM, :])
        _accum(dsc_ref, jnp.concatenate(dsc_parts, axis=-1), i)
        dx, dnw = _rms_bwd(jnp.concatenate(dh_parts, axis=-1), xh, r, nwv)
        dx_ref[...] = dyv + dx
        _accum(dnw_ref, dnw, i)

    vec = pl.BlockSpec((1, D), lambda i: (0, 0))
    return pl.pallas_call(
        body,
        grid=(nt,),
        in_specs=[pl.BlockSpec((TM, D), lambda i: (i, 0)), _halo_prev_spec(D, HALO), vec,
                  pl.BlockSpec((4, PG, PG), lambda i: (0, 0, 0)), vec,
                  pl.BlockSpec((TM, D), lambda i: (i, 0)), _halo_next_spec(D, HALO, t)],
        out_specs=[pl.BlockSpec((TM, D), lambda i: (i, 0)), vec, pl.BlockSpec((4, PG, PG), lambda i: (0, 0, 0)), vec],
        out_shape=[jax.ShapeDtypeStruct((t, D), F32), jax.ShapeDtypeStruct((1, D), F32),
                   jax.ShapeDtypeStruct((4, PG, PG), F32), jax.ShapeDtypeStruct((1, D), F32)],
        name="pool_bwd",
    )(x1, x1, nw, pw, scale, dx2, dx2)


def _adamw(lands, w, m, v, rb, name):
    nl, nr = w.shape[0], w.shape[1]
    rest = w.shape[2:]
    ns = lands[0].shape[0]
    zeros = (0,) * len(rest)

    def body(*refs):
        l_refs = refs[0:nl]
        w_ref, m_ref, v_ref, g_ref, d_ref, m2_ref, v2_ref = refs[nl:]
        for l in range(nl):
            g = l_refs[l][0].astype(F32)
            for s in range(1, ns):
                g = g + l_refs[l][s].astype(F32)
            m2 = ADAM_B1 * m_ref[l] + (1.0 - ADAM_B1) * g
            v2 = ADAM_B2 * v_ref[l] + (1.0 - ADAM_B2) * (g * g)
            m_hat = m2 / (1.0 - ADAM_B1 ** ADAM_STEP)
            v_hat = v2 / (1.0 - ADAM_B2 ** ADAM_STEP)
            g_ref[l] = g
            d_ref[l] = -ADAM_LR * (m_hat / (jnp.sqrt(v_hat) + ADAM_EPS) + ADAM_WD * w_ref[l])
            m2_ref[l] = m2
            v2_ref[l] = v2

    lspec = pl.BlockSpec((ns, rb) + rest, lambda r: (0, r) + zeros)
    wspec = pl.BlockSpec((nl, rb) + rest, lambda r: (0, r) + zeros)
    return pl.pallas_call(
        body,
        grid=(nr // rb,),
        in_specs=[lspec] * nl + [wspec] * 3,
        out_specs=[wspec] * 4,
        out_shape=[jax.ShapeDtypeStruct(w.shape, F32)] * 4,
        name=name,
    )(*lands, w, m, v)


WEIGHT_ORDER = ("ffn_norm1", "ffn1_w_in", "ffn1_w_out", "mix_norm", "ffn_norm2", "ffn2_w_in", "ffn2_w_out", "ab_w_in",
                "dn_conv_w", "dn_a_log", "dn_dt_bias", "dn_out_norm", "sg_norm", "sg_w", "sg_b", "ab_w_out", "pool_w",
                "pool_scale", "final_norm")
R_SMALL = 88
SMALL_ROWS = (
    ("ffn_norm1", (2, D), 2), ("mix_norm", (2, D), 2), ("ffn_norm2", (2, D), 2), ("final_norm", (D,), 1),
    ("sg_w", (1, 4, SGC, SGC), 64), ("sg_norm", (1, 4, HD), 1), ("sg_b", (1, 4, SGC), 1), ("dn_out_norm", (1, HD), 1),
    ("dn_a_log", (1, 4), 1), ("dn_dt_bias", (1, 4), 1), ("pool_scale", (1, D), 1), ("dn_conv_w", (1, 4, QKV), 8),
)
SMALL_SHARDED = ("pool_scale", "dn_conv_w")


def _rows_of(a, rows):
    if a.shape[-1] == QKV:
        return jnp.pad(a.reshape(4, QKV), ((0, 0), (0, 2 * ROW - QKV))).reshape(8, ROW)
    n = _numel(a.shape)
    if n % ROW == 0:
        return a.reshape(n // ROW, ROW)
    return jnp.pad(a.reshape(1, n), ((0, 0), (0, ROW - n)))


def _from_rows(r, shape):
    if shape[-1] == QKV:
        return r.reshape(4, 2 * ROW)[:, 0:QKV].reshape(shape)
    n = _numel(shape)
    if n % ROW == 0:
        return r.reshape(shape)
    return r[:, 0:n].reshape(shape)


def _pack_small(vals):
    parts = [(_rows_of(vals[n].astype(F32), r) if n in vals else jnp.zeros((r, ROW), F32)) for n, _, r in SMALL_ROWS]
    used = sum(r for _, _, r in SMALL_ROWS)
    return jnp.concatenate(parts + [jnp.zeros((R_SMALL - used, ROW), F32)], axis=0)


def _unpack_small(packed):
    out, o = {}, 0
    for n, shape, r in SMALL_ROWS:
        out[n] = _from_rows(packed[o:o + r], shape)
        o += r
    return out


def _pack_small_shard(ps, cw):
    return jnp.concatenate([
        jnp.pad(ps, ((0, 0), (0, ROW - D // N_DEV))), jnp.pad(cw[0], ((0, 0), (0, ROW - QKV // N_DEV))),
        jnp.zeros((3, ROW), F32)], axis=0)


def _mixer_weights(g_in, g_out, g_small, small):
    w = {}
    wi = jnp.transpose(g_in, (1, 0, 2)).reshape(D, AB_IN)
    main = [wi[:, 0:2048], wi[:, 2056:AB_IN]]
    w["wab_f"] = jnp.concatenate(
        main + [jnp.repeat(wi[:, 2048:2052], HD, axis=1), jnp.repeat(wi[:, 2052:2056], HD, axis=1)], axis=1)
    w["wab_b"] = jnp.concatenate(main + [wi[:, 2048:2056], jnp.zeros((D, 120), wi.dtype)], axis=1)
    w["cw"] = jnp.transpose(g_small[:, 1:5, 0:QKV // N_DEV], (1, 0, 2)).reshape(4, QKV)
    w["ps"] = g_small[:, 0, 0:D // N_DEV].reshape(1, D)
    w["alog"] = jnp.repeat(small["dn_a_log"][0].astype(F32), HD).reshape(1, HW)
    w["dtb"] = jnp.repeat(small["dn_dt_bias"][0].astype(F32), HD).reshape(1, HW)
    w["dnw"] = jnp.tile(small["dn_out_norm"][0].astype(F32), HEADS).reshape(1, HW)
    w["sgnw"] = small["sg_norm"][0].astype(F32).reshape(1, HW)
    tri = jnp.tril(jnp.ones((SGC, SGC), dtype=bool))
    wt = jnp.where(tri, small["sg_w"][0].astype(F32), 0.0)
    w["wtril"] = _c(wt)
    w["wtril_t"] = _c(jnp.transpose(wt, (0, 2, 1)))
    w["sgb"] = jnp.repeat(jnp.transpose(small["sg_b"][0].astype(F32)), HD, axis=1)
    w["wout_ab"] = g_out.reshape(D, D)
    return w


def kernel(x, ffn_norm1, ffn1_w_in, ffn1_w_out, mix_norm, ffn_norm2, ffn2_w_in, ffn2_w_out, ab_w_in, dn_conv_w, dn_a_log, dn_dt_bias, dn_out_norm, sg_norm, sg_w, sg_b, ab_w_out, pool_w, pool_scale, final_norm, loss_target, m_ffn_norm1, m_ffn1_w_in, m_ffn1_w_out, m_mix_norm, m_ffn_norm2, m_ffn2_w_in, m_ffn2_w_out, m_ab_w_in, m_dn_conv_w, m_dn_a_log, m_dn_dt_bias, m_dn_out_norm, m_sg_norm, m_sg_w, m_sg_b, m_ab_w_out, m_pool_w, m_pool_scale, m_final_norm, v_ffn_norm1, v_ffn1_w_in, v_ffn1_w_out, v_mix_norm, v_ffn_norm2, v_ffn2_w_in, v_ffn2_w_out, v_ab_w_in, v_dn_conv_w, v_dn_a_log, v_dn_dt_bias, v_dn_out_norm, v_sg_norm, v_sg_w, v_sg_b, v_ab_w_out, v_pool_w, v_pool_scale, v_final_norm):
    wl = dict(ffn_norm1=ffn_norm1, mix_norm=mix_norm, ffn_norm2=ffn_norm2, dn_a_log=dn_a_log, dn_dt_bias=dn_dt_bias,
              dn_out_norm=dn_out_norm, sg_norm=sg_norm, sg_w=sg_w, sg_b=sg_b, final_norm=final_norm)
    ml = dict(ffn_norm1=m_ffn_norm1, mix_norm=m_mix_norm, ffn_norm2=m_ffn_norm2, dn_a_log=m_dn_a_log,
              dn_dt_bias=m_dn_dt_bias, dn_out_norm=m_dn_out_norm, sg_norm=m_sg_norm, sg_w=m_sg_w, sg_b=m_sg_b,
              final_norm=m_final_norm)
    vl = dict(ffn_norm1=v_ffn_norm1, mix_norm=v_mix_norm, ffn_norm2=v_ffn_norm2, dn_a_log=v_dn_a_log,
              dn_dt_bias=v_dn_dt_bias, dn_out_norm=v_dn_out_norm, sg_norm=v_sg_norm, sg_w=v_sg_w, sg_b=v_sg_b,
              final_norm=v_final_norm)
    row = lambda a: a.reshape(1, -1).astype(F32)
    n1 = [row(ffn_norm1[l]) for l in range(2)]
    n2 = [row(ffn_norm2[l]) for l in range(2)]
    mix = [row(mix_norm[l]) for l in range(2)]
    s_in = {(f, l): _c(wf[l]) for f, wf in enumerate((ffn1_w_in, ffn2_w_in)) for l in range(2)}
    s_out = {(f, l): _c(wf[l]) for f, wf in enumerate((ffn1_w_out, ffn2_w_out)) for l in range(2)}
    xs, tgt = x[0], loss_target[0]

    wi00, wo00 = _comm_call(_Comm("gather", [s_in[0, 0], s_out[0, 0]]), "gather_first")
    x01, gu00, (g_abin, g_about, g_small, wi10) = _ffn_fwd(
        xs, n1[0], wi00, wo00,
        comm=_Comm("gather", [_c(ab_w_in[0]), _c(ab_w_out[0]), _pack_small_shard(pool_scale, dn_conv_w), s_in[1, 0]]))
    w = _mixer_weights(g_abin, g_about, g_small, wl)
    h, qkv, z, su, sv, b_rep, a_rep = _ab_proj(x01, mix[0], w["wab_f"])
    qn, kn, v, beta, g = _dn_pre(qkv, b_rep, a_rep, w["cw"], w["alog"], w["dtb"])
    (o, sall, aall, dn_u, dn_w), (wo10, wi01, g_pw) = _dn_fwd(
        qn, kn, v, beta, g, comm=_Comm("gather", [s_out[1, 0], s_in[0, 1], _c(pool_w[0])]))
    pw = jnp.transpose(g_pw, (1, 0, 2, 3)).reshape(4, PG, PG)
    x02, cat = _ab_out(x01, o, z, su, sv, w["dnw"], w["sgnw"], w["wtril"], w["sgb"], w["wout_ab"])
    x10, gu10, (wo01, wi11) = _ffn_fwd(x02, n2[0], wi10, wo10, comm=_Comm("gather", [s_out[0, 1], s_in[1, 1]]))
    x11, gu01, (wo11,) = _ffn_fwd(x10, n1[1], wi01, wo01, comm=_Comm("gather", [s_out[1, 1]]))
    x12 = _pool_fwd(x11, mix[1], pw, w["ps"])
    x13, gu11, _ = _ffn_fwd(x12, n2[1], wi11, wo11)
    loss_local, dx, d_fn = _loss_head(x13, row(final_norm), tgt)

    bt = min(BT, xs.shape[0])

    def ffn_b(xin, nw, w_in, w_out, gu, dy, comm=None):
        (dxn, xn, act, dh, dnw), landed = _ffn_bwd(xin, nw, w_in, w_out, gu, dy, comm)
        return dxn, dnw, (xn, act, dh), landed

    def ffn_g(kept, dy):
        return [_mm_tn_win(kept[0], kept[2])[0], _mm_tn_wout(kept[1], dy)]

    dy = dx
    dx, d_n2_1, kept, _ = ffn_b(x12, n2[1], wi11, wo11, gu11, dy)
    g11 = ffn_g(kept, dy)
    dx, d_mix_1, d_pw, d_ps = _pool_bwd(x11, mix[1], pw, w["ps"], dx)
    d_pw_sh = _c(jnp.transpose(d_pw.reshape(4, N_DEV, PG // N_DEV, PG), (1, 0, 2, 3)))
    dy = dx
    dx, d_n1_1, kept, land11 = ffn_b(x10, n1[1], wi01, wo01, gu01, dy, _Comm("exchange", g11))
    g01 = ffn_g(kept, dy)
    dy = dx
    dx, d_n2_0, kept, land01 = ffn_b(x02, n2[0], wi10, wo10, gu10, dy, _Comm("exchange", g01 + [d_pw_sh]))
    g10 = ffn_g(kept, dy)
    do, dz, dsu, dsv, d_dnw, d_sgnw, d_sgw, d_sgb = _ab_out_bwd(
        dx, o, z, su, sv, w["dnw"], w["sgnw"], w["wtril"], w["wtril_t"], w["sgb"], w["wout_ab"])
    d_about = _mm_tn(cat, dx, D, D, bt, _MM, "mm_tn_about").reshape(N_DEV, D // N_DEV, D)
    (dqn, dkn, dv, dbeta, dg), _ = _dn_bwd(qn, kn, v, beta, g, sall, aall, dn_u, dn_w, do)
    dc, dba, d_cw, d_alog, d_dtb = _dn_pre_bwd(qkv, b_rep, a_rep, w["cw"], w["alog"], w["dtb"], dqn, dkn, dv, dbeta, dg)
    dqkv = _conv_bwd(dc, w["cw"])
    dx, dcat, d_mix_0 = _ab_proj_bwd(x01, mix[0], dqkv, dz, dsu, dsv, dba, w["wab_b"], dx)
    d_wab = _mm_tn(h, dcat, D, 640, bt, _MM, "mm_tn_abin")
    d_abin = jnp.concatenate([d_wab[:, 0:2048], d_wab[:, 3072:3080], d_wab[:, 2048:3072]], axis=1)
    d_abin_sh = jnp.transpose(d_abin.reshape(D, N_DEV, AB_IN // N_DEV), (1, 0, 2))
    dy = dx
    grad_x, d_n1_0, kept, land_mid = ffn_b(xs, n1[0], wi00, wo00, gu00, dy,
                                           _Comm("exchange", g10 + [d_abin_sh, d_about]))
    land10, land_ab = land_mid[0:2], land_mid[2:4]

    g_small = {
        "ffn_norm1": jnp.concatenate([d_n1_0, d_n1_1], axis=0),
        "mix_norm": jnp.concatenate([d_mix_0, d_mix_1], axis=0),
        "ffn_norm2": jnp.concatenate([d_n2_0, d_n2_1], axis=0),
        "dn_conv_w": d_cw.reshape(1, 4, QKV),
        "dn_a_log": d_alog[:, ::HD],
        "dn_dt_bias": d_dtb[:, ::HD],
        "dn_out_norm": d_dnw,
        "sg_norm": d_sgnw.reshape(1, HEADS, HD),
        "sg_w": d_sgw[None],
        "sg_b": jnp.transpose(d_sgb[:, 0:HEADS])[None],
        "pool_scale": d_ps,
        "final_norm": d_fn.reshape(D),
    }
    g00_out = _mm_tn_wout(kept[1], dy)
    g00_in, (land00_out, land_small) = _mm_tn_win(
        kept[0], kept[2], comm=_Comm("exchange", [g00_out], repl=[_pack_small(g_small)]))
    (land00_in,) = _comm_call(_Comm("exchange", [g00_in]), "exchange_last")

    res = {}
    res["ffn1_w_in"] = _adamw([land00_in, land01[0]], ffn1_w_in, m_ffn1_w_in, v_ffn1_w_in, 128, "adamw_w_in")
    res["ffn2_w_in"] = _adamw([land10[0], land11[0]], ffn2_w_in, m_ffn2_w_in, v_ffn2_w_in, 128, "adamw_w_in")
    res["ffn1_w_out"] = _adamw([land00_out, land01[1]], ffn1_w_out, m_ffn1_w_out, v_ffn1_w_out, 176, "adamw_w_out")
    res["ffn2_w_out"] = _adamw([land10[1], land11[1]], ffn2_w_out, m_ffn2_w_out, v_ffn2_w_out, 176, "adamw_w_out")
    res["ab_w_in"] = _adamw([land_ab[0]], ab_w_in, m_ab_w_in, v_ab_w_in, 256, "adamw_ab_w_in")
    res["ab_w_out"] = _adamw([land_ab[1]], ab_w_out, m_ab_w_out, v_ab_w_out, D // N_DEV, "adamw_ab_w_out")
    res["pool_w"] = _adamw([land01[2]], pool_w, m_pool_w, v_pool_w, 4, "adamw_pool_w")
    sm = _adamw([land_small], _pack_small(wl)[None], _pack_small(ml)[None], _pack_small(vl)[None], R_SMALL,
                "adamw_replicated")
    sm = [_unpack_small(a[0]) for a in sm]
    for n in wl:
        res[n] = [d[n] for d in sm]
    me = 4 * lax.axis_index("x") + 2 * lax.axis_index("y") + lax.axis_index("c")
    g_ps = lax.dynamic_slice(sm[0]["pool_scale"], (0, me * (D // N_DEV)), (1, D // N_DEV))
    g_cw = lax.dynamic_slice(sm[0]["dn_conv_w"], (0, 0, me * (QKV // N_DEV)), (1, 4, QKV // N_DEV))
    s2 = _adamw([_pack_small_shard(g_ps, g_cw)[None]], _pack_small_shard(pool_scale, dn_conv_w)[None],
                _pack_small_shard(m_pool_scale, m_dn_conv_w)[None], _pack_small_shard(v_pool_scale, v_dn_conv_w)[None],
                8, "adamw_small_sharded")
    res["pool_scale"] = [a[0, 0:1, 0:D // N_DEV] for a in s2]
    res["dn_conv_w"] = [a[0, 1:5, 0:QKV // N_DEV][None] for a in s2]

    loss = lax.psum(loss_local[0, 0], ("x", "y", "c"))
    result = [loss, grad_x[None]]
    for i in range(4):
        result += [res[n][i] for n in WEIGHT_ORDER]
    return tuple(result)
```

```python
import jax
import jax.numpy as jnp
from jax import lax
from jax.experimental import pallas as pl
from jax.experimental.pallas import tpu as pltpu

F32 = jnp.float32
_MM = jnp.bfloat16

D = 1024
FF = 2816
EPS = 1e-6
HEADS = 4
HD = 128
DNC = 64
DN_STEP = 8
SGC = 128
QKV = 3 * HEADS * HD
HW = HEADS * HD
POOL_WINDOWS = (2, 4, 8, 16)
PG = D // 4
HALO = 16
N_DEV = 8
AB_IN = 3080
ROW = 1024

TM = 512
BT = 1024
FT = 512
FC = 704
NJ = FF // FC
WO_ROWS = FF // N_DEV

ADAM_LR, ADAM_B1, ADAM_B2, ADAM_EPS, ADAM_WD, ADAM_STEP = 0.001, 0.9, 0.999, 1e-08, 0.01, 10

MESH_T = pl.DeviceIdType.MESH
NN = (((1,), (0,)), ((), ()))
NT = (((1,), (1,)), ((), ()))
TN = (((0,), (0,)), ((), ()))


def _c(a):
    return a.astype(_MM)


def _dg(a, b, dims):
    return lax.dot_general(a, b, dims, preferred_element_type=F32)


def _dot(a, b):
    return _dg(a, b, NN)


def _dot_nt(a, b):
    return _dg(a, b, NT)


def _dot_tn(a, b):
    return _dg(a, b, TN)


def _split2(a):
    hi = _c(a)
    return hi, _c(a - hi.astype(F32))


def _dot3(a, b, dims=NN):
    ah, al = _split2(a)
    bh, bl = _split2(b)
    return _dg(ah, bh, dims) + (_dg(ah, bl, dims) + _dg(al, bh, dims))


def _mask_dot(mask, x):
    x1 = _c(x)
    r = x - x1.astype(F32)
    x2 = _c(r)
    x3 = _c(r - x2.astype(F32))
    return _dot(mask, x1) + (_dot(mask, x2) + _dot(mask, x3))


def _sigmoid(x):
    return jax.nn.sigmoid(x)


def _gelu(x):
    return 0.5 * x * (1.0 + lax.erf(x * 0.7071067811865476))


def _gelu_grad(x):
    return 0.5 * (1.0 + lax.erf(x * 0.7071067811865476)) + x * jnp.exp(-0.5 * x * x) * 0.3989422804014327


def _accum(ref, val, step):
    @pl.when(step == 0)
    def _():
        ref[...] = val

    @pl.when(step > 0)
    def _():
        ref[...] += val


def _rstd(x):
    return lax.rsqrt(jnp.mean(x * x, axis=-1, keepdims=True) + EPS)


def _rms_bwd(dy, xhat, r, nw):
    dnw = jnp.sum(dy * xhat, axis=0, keepdims=True)
    dxh = dy * nw
    dx = r * (dxh - xhat * jnp.mean(dxh * xhat, axis=-1, keepdims=True))
    return dx, dnw


def _numel(shape):
    n = 1
    for s in shape:
        n *= s
    return n


def _peer(k, x, y, c):
    px = 1 - x if k & 4 else x
    py = 1 - y if k & 2 else y
    pc = 1 - c if k & 1 else c
    return px, py, pc


class _Comm:
    def __init__(self, kind, arrs, repl=()):
        self.kind = kind
        self.ns = len(arrs)
        self.arrs = list(arrs) + list(repl)
        self.na = len(self.arrs)

    @property
    def out_shape(self):
        out = []
        for i, a in enumerate(self.arrs):
            lead = (N_DEV,) if (self.kind == "gather" or i >= self.ns) else ()
            out.append(jax.ShapeDtypeStruct(lead + a.shape, a.dtype))
        return out

    @property
    def scratch(self):
        return [pltpu.SemaphoreType.DMA((7 * self.na,)), pltpu.SemaphoreType.DMA((7 * self.na,)),
                pltpu.SemaphoreType.DMA((self.na,))]

    def phases(self, ins, outs, sems):
        send_sems, recv_sems, local_sems = sems
        na = self.na
        x, y, c = lax.axis_index("x"), lax.axis_index("y"), lax.axis_index("c")
        if self.kind == "gather":
            me, sibling = (x, y, c), (x, y, 1 - c)
            chips = [(1 - x, y), (x, 1 - y), (1 - x, 1 - y)]

            def slot(a, px, py, pc):
                return outs[a].at[4 * px + 2 * py + pc]

            def copy(a, k, block, to, src=None):
                return pltpu.make_async_remote_copy(
                    src_ref=slot(a, *block) if src is None else src, dst_ref=slot(a, *block),
                    send_sem=send_sems.at[7 * a + k], recv_sem=recv_sems.at[7 * a + k],
                    device_id=to, device_id_type=MESH_T)

            mine = [pltpu.make_async_copy(ins[a], slot(a, *me), local_sems.at[a]) for a in range(na)]
            first, passed = [], []
            for a in range(na):
                first.append(copy(a, 0, me, sibling, src=ins[a]))
                first += [copy(a, 1 + j, me, (*chip, c), src=ins[a]) for j, chip in enumerate(chips)]
                passed += [copy(a, 4 + j, (*chip, c), sibling) for j, chip in enumerate(chips)]

            def start():
                for cp in mine + first:
                    cp.start()

            def middle():
                for a in range(na):
                    for j, chip in enumerate(chips):
                        copy(a, 1 + j, (*chip, c), me).wait_recv()
                        passed[3 * a + j].start()

            def finish():
                for a in range(na):
                    copy(a, 0, sibling, me).wait_recv()
                    for j, chip in enumerate(chips):
                        copy(a, 4 + j, (*chip, 1 - c), me).wait_recv()
                for cp in first + passed:
                    cp.wait_send()
                for cp in mine:
                    cp.wait()

            return start, middle, finish

        me = 4 * x + 2 * y + c
        ns = self.ns
        own = [pltpu.make_async_copy(ins[a].at[me] if a < ns else ins[a], outs[a].at[me], local_sems.at[a])
               for a in range(na)]
        copies = []
        for k in range(1, N_DEV):
            px, py, pc = _peer(k, x, y, c)
            peer = 4 * px + 2 * py + pc
            for a in range(na):
                copies.append(pltpu.make_async_remote_copy(
                    src_ref=ins[a].at[peer] if a < ns else ins[a], dst_ref=outs[a].at[me],
                    send_sem=send_sems.at[na * (k - 1) + a], recv_sem=recv_sems.at[na * (k - 1) + a],
                    device_id=(px, py, pc), device_id_type=MESH_T))

        def start():
            for cp in own + copies:
                cp.start()

        def middle():
            pass

        def finish():
            for cp in copies:
                cp.wait()
            for cp in own:
                cp.wait()

        return start, middle, finish


def _comm_call(comm, name):
    na = comm.na

    def body(*refs):
        start, middle, finish = comm.phases(refs[0:na], refs[na:2 * na], refs[2 * na:])
        start()
        middle()
        finish()

    hbm = pl.BlockSpec(memory_space=pltpu.HBM)
    return pl.pallas_call(
        body, out_shape=comm.out_shape, in_specs=[hbm] * na, out_specs=[hbm] * na, scratch_shapes=comm.scratch,
        name=name)(*comm.arrs)


def _carried_call(body, comm, n_in, n_out, n_scr, when, *, grid, in_specs, out_specs, out_shape, scratch_shapes,
                  operands, name):
    if comm is None:
        return pl.pallas_call(body, grid=grid, in_specs=in_specs, out_specs=out_specs, out_shape=out_shape,
                              scratch_shapes=scratch_shapes, name=name)(*operands), []
    na = comm.na

    def both(*refs):
        a = n_in + na
        b = a + n_out + na
        body(*refs[0:n_in], *refs[a:a + n_out], *refs[b:b + n_scr])
        start, middle, finish = comm.phases(refs[n_in:a], refs[a + n_out:b], refs[b + n_scr:])
        first, mid, last = when()
        pl.when(first)(start)
        pl.when(mid)(middle)
        pl.when(last)(finish)

    hbm = pl.BlockSpec(memory_space=pltpu.HBM)
    res = pl.pallas_call(
        both, grid=grid, in_specs=list(in_specs) + [hbm] * na, out_specs=list(out_specs) + [hbm] * na,
        out_shape=list(out_shape) + comm.out_shape, scratch_shapes=list(scratch_shapes) + comm.scratch,
        name=name)(*operands, *comm.arrs)
    return res[0:n_out], res[n_out:]


def _ffn_w_specs():
    return [
        pl.BlockSpec((None, D, FC), lambda i, j: (j, 0, 0)),
        pl.BlockSpec((None, D, FC), lambda i, j: (j + NJ, 0, 0)),
        pl.BlockSpec((2, WO_ROWS, D), lambda i, j: (j, 0, 0)),
    ]


def _ffn_when(nt):
    def when():
        i, j = pl.program_id(0), pl.program_id(1)
        return ((i == 0) & (j == 0), (i == (3 * nt) // 4) & (j == 0), (i == nt - 1) & (j == NJ - 1))
    return when


def _ffn_fwd(x, nw, w_in, w_out, comm=None):
    t = x.shape[0]
    tm = min(FT, t)
    nt = t // tm

    def body(x_ref, nw_ref, wg_ref, wu_ref, wo3_ref, o_ref, gu_ref, xn_sc, acc_sc):
        j = pl.program_id(1)

        @pl.when(j == 0)
        def _():
            xv = x_ref[...]
            xn_sc[...] = _c(xv * _rstd(xv) * nw_ref[...])
            acc_sc[...] = jnp.zeros_like(acc_sc)

        xn = xn_sc[...]
        g = _dot(xn, wg_ref[...])
        u = _dot(xn, wu_ref[...])
        gu_ref[0] = _c(g)
        gu_ref[1] = _c(u)
        acc_sc[...] += _dot(_c(g * _sigmoid(g) * u), wo3_ref[...].reshape(FC, D))

        @pl.when(j == NJ - 1)
        def _():
            o_ref[...] = x_ref[...] + 0.5 * acc_sc[...]

    (out, gu), landed = _carried_call(
        body, comm, 5, 2, 2, _ffn_when(nt),
        grid=(nt, NJ),
        in_specs=[pl.BlockSpec((tm, D), lambda i, j: (i, 0)), pl.BlockSpec((1, D), lambda i, j: (0, 0))]
        + _ffn_w_specs(),
        out_specs=[pl.BlockSpec((tm, D), lambda i, j: (i, 0)),
                   pl.BlockSpec((None, 2, tm, FC), lambda i, j: (j, 0, i, 0))],
        out_shape=[jax.ShapeDtypeStruct((t, D), F32), jax.ShapeDtypeStruct((NJ, 2, t, FC), _MM)],
        scratch_shapes=[pltpu.VMEM((tm, D), _MM), pltpu.VMEM((tm, D), F32)],
        operands=(x, nw, w_in, w_in, w_out), name="ffn_fwd")
    return out, gu, landed


def _ffn_bwd(x, nw, w_in, w_out, gu, dy, comm=None):
    t = x.shape[0]
    nt = t // TM

    def body(x_ref, nw_ref, wg_ref, wu_ref, wo3_ref, gu_ref, dy_ref, dx_ref, xn_ref, a_ref, dh_ref, dnw_ref,
             r_sc, dyb_sc, acc_sc):
        wo = wo3_ref[...].reshape(FC, D)
        i = pl.program_id(0)
        j = pl.program_id(1)

        @pl.when(j == 0)
        def _():
            xv = x_ref[...]
            r = _rstd(xv)
            r_sc[...] = r
            xn_ref[...] = _c(xv * r * nw_ref[...])
            dyb_sc[...] = _c(0.5 * dy_ref[...])
            acc_sc[...] = jnp.zeros_like(acc_sc)

        g = gu_ref[0].astype(F32)
        u = gu_ref[1].astype(F32)
        s = _sigmoid(g)
        sl = g * s
        a_ref[...] = _c(sl * u)
        da = _dot_nt(dyb_sc[...], wo)
        dg = _c(da * u * (s * (1.0 + g * (1.0 - s))))
        du = _c(da * sl)
        dh_ref[0] = dg
        dh_ref[1] = du
        acc_sc[...] += _dot_nt(dg, wg_ref[...]) + _dot_nt(du, wu_ref[...])

        @pl.when(j == NJ - 1)
        def _():
            r = r_sc[...]
            dx, dnw = _rms_bwd(acc_sc[...], x_ref[...] * r, r, nw_ref[...])
            dx_ref[...] = dy_ref[...] + dx
            _accum(dnw_ref, dnw, i)

    return _carried_call(
        body, comm, 7, 5, 3, _ffn_when(nt),
        grid=(nt, NJ),
        in_specs=[pl.BlockSpec((TM, D), lambda i, j: (i, 0)), pl.BlockSpec((1, D), lambda i, j: (0, 0))]
        + _ffn_w_specs() + [pl.BlockSpec((None, 2, TM, FC), lambda i, j: (j, 0, i, 0)),
                            pl.BlockSpec((TM, D), lambda i, j: (i, 0))],
        out_specs=[
            pl.BlockSpec((TM, D), lambda i, j: (i, 0)),
            pl.BlockSpec((TM, D), lambda i, j: (i, 0)),
            pl.BlockSpec((None, TM, FC), lambda i, j: (j, i, 0)),
            pl.BlockSpec((None, 2, TM, FC), lambda i, j: (j, 0, i, 0)),
            pl.BlockSpec((1, D), lambda i, j: (0, 0)),
        ],
        out_shape=[
            jax.ShapeDtypeStruct((t, D), F32),
            jax.ShapeDtypeStruct((t, D), _MM),
            jax.ShapeDtypeStruct((NJ, t, FC), _MM),
            jax.ShapeDtypeStruct((NJ, 2, t, FC), _MM),
            jax.ShapeDtypeStruct((1, D), F32),
        ],
        scratch_shapes=[pltpu.VMEM((TM, 1), F32), pltpu.VMEM((TM, D), _MM), pltpu.VMEM((TM, D), F32)],
        operands=(x, nw, w_in, w_in, w_out, gu, dy), name="ffn_bwd")


def _mm_tn(a, b, bm, bn, bt, out_dtype, name):
    t, m = a.shape
    n = b.shape[1]
    nt = t // bt

    def body(a_ref, b_ref, o_ref, acc_sc):
        k = pl.program_id(2)
        _accum(acc_sc, _dot_tn(_c(a_ref[...]), _c(b_ref[...])), k)

        @pl.when(k == nt - 1)
        def _():
            o_ref[...] = acc_sc[...].astype(out_dtype)

    return pl.pallas_call(
        body,
        grid=(m // bm, n // bn, nt),
        in_specs=[pl.BlockSpec((bt, bm), lambda i, j, k: (k, i)), pl.BlockSpec((bt, bn), lambda i, j, k: (k, j))],
        out_specs=pl.BlockSpec((bm, bn), lambda i, j, k: (i, j)),
        out_shape=jax.ShapeDtypeStruct((m, n), out_dtype),
        scratch_shapes=[pltpu.VMEM((bm, bn), F32)],
        name=name,
    )(a, b)


def _mm_tn_win(xn, dh, comm=None, half=None):
    t = xn.shape[0]
    bt = min(BT, t)
    nt = t // bt
    m, mb = (D, 0) if half is None else (D // 2, half)

    def body(a_ref, b_ref, o_ref, acc_sc):
        k = pl.program_id(2)
        _accum(acc_sc, _dot_tn(a_ref[...], b_ref[...]), k)

        @pl.when(k == nt - 1)
        def _():
            o_ref[...] = _c(acc_sc[...])

    def when():
        h, j, k = pl.program_id(0), pl.program_id(1), pl.program_id(2)
        start = (h == 0) & (j == 0) & (k == 0)
        return start, start, (h == 1) & (j == NJ - 1) & (k == nt - 1)

    (out,), landed = _carried_call(
        body, comm, 2, 1, 1, when,
        grid=(2, NJ, nt),
        in_specs=[pl.BlockSpec((bt, m), lambda h, j, k: (k, mb)),
                  pl.BlockSpec((None, None, bt, FC), lambda h, j, k: (j, h, k, 0))],
        out_specs=[pl.BlockSpec((None, m, FC), lambda h, j, k: (h * NJ + j, 0, 0))],
        out_shape=[jax.ShapeDtypeStruct((N_DEV, m, FC), _MM)],
        scratch_shapes=[pltpu.VMEM((m, FC), F32)],
        operands=(xn, dh), name="mm_tn_win")
    return out, landed


def _mm_tn_wout(act, dy):
    t = dy.shape[0]
    bt = min(BT, t)
    nt = t // bt

    def body(a_ref, b_ref, o_ref, acc_sc):
        k = pl.program_id(1)
        _accum(acc_sc, _dot_tn(a_ref[...], _c(b_ref[...])), k)

        @pl.when(k == nt - 1)
        def _():
            o_ref[...] = _c((0.5 * acc_sc[...]).reshape(2, WO_ROWS, D))

    return pl.pallas_call(
        body,
        grid=(NJ, nt),
        in_specs=[pl.BlockSpec((None, bt, FC), lambda j, k: (j, k, 0)), pl.BlockSpec((bt, D), lambda j, k: (k, 0))],
        out_specs=pl.BlockSpec((2, WO_ROWS, D), lambda j, k: (j, 0, 0)),
        out_shape=jax.ShapeDtypeStruct((N_DEV, WO_ROWS, D), _MM),
        scratch_shapes=[pltpu.VMEM((FC, D), F32)],
        name="mm_tn_wout",
    )(act, dy)


def _loss_head(x, nw, tgt):
    t = x.shape[0]

    def body(x_ref, nw_ref, t_ref, loss_ref, dx_ref, dnw_ref):
        i = pl.program_id(0)
        xv = x_ref[...]
        r = _rstd(xv)
        xh = xv * r
        e = xh * nw_ref[...] - t_ref[...]
        part = 0.5 * jnp.sum(jnp.mean(e * e, axis=-1, keepdims=True), axis=0, keepdims=True)
        _accum(loss_ref, jnp.broadcast_to(part, (1, 128)), i)
        dx, dnw = _rms_bwd(e * (1.0 / D), xh, r, nw_ref[...])
        dx_ref[...] = dx
        _accum(dnw_ref, dnw, i)

    return pl.pallas_call(
        body,
        grid=(t // TM,),
        in_specs=[pl.BlockSpec((TM, D), lambda i: (i, 0)), pl.BlockSpec((1, D), lambda i: (0, 0)),
                  pl.BlockSpec((TM, D), lambda i: (i, 0))],
        out_specs=[pl.BlockSpec((1, 128), lambda i: (0, 0)), pl.BlockSpec((TM, D), lambda i: (i, 0)),
                   pl.BlockSpec((1, D), lambda i: (0, 0))],
        out_shape=[jax.ShapeDtypeStruct((1, 128), F32), jax.ShapeDtypeStruct((t, D), F32),
                   jax.ShapeDtypeStruct((1, D), F32)],
        name="loss_head",
    )(x, nw, tgt)


PW_F = QKV + 5 * HW
PW_B = QKV + 3 * HW + 128


def _ab_proj(x1, nw, wab):
    t = x1.shape[0]

    def body(x_ref, nw_ref, w_ref, h_ref, qkv_ref, z_ref, su_ref, sv_ref, b_ref, a_ref):
        xv = x_ref[...]
        h = _c(xv * _rstd(xv) * nw_ref[...])
        h_ref[...] = h
        p = _dot(h, w_ref[...])
        qkv_ref[...] = p[:, 0:QKV]
        o = QKV
        for ref in (z_ref, su_ref, sv_ref, b_ref, a_ref):
            ref[...] = p[:, o:o + HW]
            o += HW

    row = lambda w: pl.BlockSpec((TM, w), lambda i: (i, 0))
    return pl.pallas_call(
        body,
        grid=(t // TM,),
        in_specs=[row(D), pl.BlockSpec((1, D), lambda i: (0, 0)), pl.BlockSpec((D, PW_F), lambda i: (0, 0))],
        out_specs=[row(D), row(QKV)] + [row(HW)] * 5,
        out_shape=[jax.ShapeDtypeStruct((t, D), _MM), jax.ShapeDtypeStruct((t, QKV), F32)]
        + [jax.ShapeDtypeStruct((t, HW), F32)] * 5,
        name="ab_proj",
    )(x1, nw, wab)


def _conv_rows(x, halo, cw):
    xe = jnp.concatenate([halo, x], axis=0)
    shifted = []
    c = None
    for k in range(4):
        s = 3 - k
        xs = (xe if s == 0 else pltpu.roll(xe, s, 0))[8:, :]
        shifted.append(xs)
        term = cw[k:k + 1, :] * xs
        c = term if c is None else c + term
    return c, shifted


def _head_rsq(a):
    parts = []
    for h in range(HEADS):
        ah = a[:, h * HD:(h + 1) * HD]
        r = lax.rsqrt(jnp.sum(ah * ah, axis=-1, keepdims=True) + EPS)
        parts.append(jnp.broadcast_to(r, ah.shape))
    return jnp.concatenate(parts, axis=-1)


def _head_sum(a):
    parts = []
    for h in range(HEADS):
        ah = a[:, h * HD:(h + 1) * HD]
        parts.append(jnp.broadcast_to(jnp.sum(ah, axis=-1, keepdims=True), ah.shape))
    return jnp.concatenate(parts, axis=-1)


def _softplus(x):
    return jnp.maximum(x, 0.0) + jnp.log1p(jnp.exp(-jnp.abs(x)))


def _halo_prev_spec(width, rows):
    per = TM // rows
    return pl.BlockSpec((rows, width), lambda i: (jnp.maximum(i * per - 1, 0), 0))


def _halo_next_spec(width, rows, t):
    per = TM // rows
    last = t // rows - 1
    return pl.BlockSpec((rows, width), lambda i: (jnp.minimum((i + 1) * per, last), 0))


def _dn_pre(qkv, b_rep, a_rep, cw, alog, dtb):
    t = qkv.shape[0]
    qscale = HD ** -0.5

    def body(x_ref, halo_ref, b_ref, a_ref, cw_ref, alog_ref, dt_ref, q_ref, k_ref, v_ref, beta_ref, g_ref):
        i = pl.program_id(0)
        halo = jnp.where(i == 0, 0.0, halo_ref[...])
        c, _ = _conv_rows(x_ref[...], halo, cw_ref[...])
        sc = c * _sigmoid(c)
        q = sc[:, 0:HW]
        k = sc[:, HW:2 * HW]
        q_ref[...] = q * _head_rsq(q) * qscale
        k_ref[...] = k * _head_rsq(k)
        v_ref[...] = sc[:, 2 * HW:]
        beta_ref[...] = _sigmoid(b_ref[...])
        g_ref[...] = -jnp.exp(alog_ref[...]) * _softplus(a_ref[...] + dt_ref[...])

    row = lambda w: pl.BlockSpec((TM, w), lambda i: (i, 0))
    full = lambda a: pl.BlockSpec(a.shape, lambda i: (0,) * a.ndim)
    return pl.pallas_call(
        body,
        grid=(t // TM,),
        in_specs=[row(QKV), _halo_prev_spec(QKV, 8), row(HW), row(HW), full(cw), full(alog), full(dtb)],
        out_specs=[row(HW)] * 5,
        out_shape=[jax.ShapeDtypeStruct((t, HW), F32)] * 5,
        name="dn_pre",
    )(qkv, qkv, b_rep, a_rep, cw, alog, dtb)


def _unit_lower_inv(los, eye):
    ps = [eye - lo for lo in los]
    lps = list(los)
    for _ in range(5):
        lps = [_dot(_c(lp), _c(lp)) for lp in lps]
        ps = [p + _dot(_c(p), _c(lp)) for p, lp in zip(ps, lps)]
    rs = [eye - (p + _dot3(lo, p)) for lo, p in zip(los, ps)]
    return [p + _dot(_c(p), _c(r)) for p, r in zip(ps, rs)]


def _dn_masks():
    ri = lax.broadcasted_iota(jnp.int32, (DNC, DNC), 0)
    ci = lax.broadcasted_iota(jnp.int32, (DNC, DNC), 1)
    return dict(strict=ri > ci, causal=ri >= ci, eye=(ri == ci).astype(F32),
                ltri=_c((ri >= ci).astype(F32)), upper=_c((ri <= ci).astype(F32)))


def _dn_decay(gr, mk):
    rhs = jnp.concatenate([gr, jnp.where(mk["strict"], gr[:, 0:DNC], 0.0)], axis=1)
    cs = _mask_dot(mk["ltri"], rhs)
    gc = cs[:, 0:HD]
    dm = jnp.where(mk["causal"], jnp.exp(cs[:, HD:HD + DNC]), 0.0)
    gl = jnp.sum(gr, axis=0, keepdims=True)
    return dm, jnp.exp(gc), jnp.exp(gl - gc), gl


def _dn_when(n):
    def when():
        i = pl.program_id(0)
        return (i == 0, i == n // 2, i == n - 1)
    return when


def _dn_fwd(q, k, v, beta, g, comm=None):
    t = q.shape[0]
    rows = DN_STEP * DNC
    n = t // rows

    def body(q_ref, k_ref, v_ref, b_ref, g_ref, o_ref, sall_ref, aall_ref, u_ref, w_ref, s_sc):
        i = pl.program_id(0)

        @pl.when(i == 0)
        def _():
            s_sc[...] = jnp.zeros_like(s_sc)

        mk = _dn_masks()
        idx = [(cc, h) for cc in range(DN_STEP) for h in range(HEADS)]
        at = lambda cc, h: (slice(cc * DNC, (cc + 1) * DNC), slice(h * HD, (h + 1) * HD))
        qs = [q_ref[at(*i)] for i in idx]
        ks = [k_ref[at(*i)] for i in idx]
        bs = [b_ref[at(*i)] for i in idx]
        dec = [_dn_decay(g_ref[at(*i)], mk) for i in idx]
        kbs = [k_ * b_ for k_, b_ in zip(ks, bs)]
        los = [jnp.where(mk["strict"], _dot_nt(_c(kb), _c(k_)) * d[0], 0.0) for kb, k_, d in zip(kbs, ks, dec)]
        inv = _unit_lower_inv(los, mk["eye"])
        uws = [_dot3(a, jnp.concatenate([v_ref[at(*i)] * b_, kb * d[1]], axis=1))
               for a, i, b_, kb, d in zip(inv, idx, bs, kbs, dec)]
        attn = [_c(_dot_nt(_c(q_), _c(k_)) * d[0]) for q_, k_, d in zip(qs, ks, dec)]
        for n_, (cc, h) in enumerate(idx):
            aall_ref[cc, h] = inv[n_]
            u_ref[at(cc, h)] = uws[n_][:, 0:HD]
            w_ref[at(cc, h)] = uws[n_][:, HD:]
        ss = [s_sc[h] for h in range(HEADS)]
        for cc in range(DN_STEP):
            base = cc * HEADS
            for h in range(HEADS):
                sall_ref[cc, h] = ss[h]
            ws = [_dot(_c(jnp.concatenate([uws[base + h][:, HD:], qs[base + h] * dec[base + h][1]], axis=0)),
                       _c(ss[h])) for h in range(HEADS)]
            vn = [_c(uws[base + h][:, 0:HD] - ws[h][0:DNC]) for h in range(HEADS)]
            for h in range(HEADS):
                o_ref[at(cc, h)] = ws[h][DNC:] + _dot(attn[base + h], vn[h])
            ss = [ss[h] * jnp.exp(dec[base + h][3]) + _dot_tn(_c(ks[base + h] * dec[base + h][2]), vn[h])
                  for h in range(HEADS)]
        for h in range(HEADS):
            s_sc[h] = ss[h]

    row = pl.BlockSpec((rows, HW), lambda i: (i, 0))
    return _carried_call(
        body, comm, 5, 5, 1, _dn_when(n),
        grid=(n,),
        in_specs=[row] * 5,
        out_specs=[row, pl.BlockSpec((DN_STEP, HEADS, HD, HD), lambda i: (i, 0, 0, 0)),
                   pl.BlockSpec((DN_STEP, HEADS, DNC, DNC), lambda i: (i, 0, 0, 0)), row, row],
        out_shape=[jax.ShapeDtypeStruct((t, HW), F32), jax.ShapeDtypeStruct((t // DNC, HEADS, HD, HD), F32),
                   jax.ShapeDtypeStruct((t // DNC, HEADS, DNC, DNC), F32), jax.ShapeDtypeStruct((t, HW), F32),
                   jax.ShapeDtypeStruct((t, HW), F32)],
        scratch_shapes=[pltpu.VMEM((HEADS, HD, HD), F32)],
        operands=(q, k, v, beta, g), name="dn_fwd")


def _dn_bwd(q, k, v, beta, g, sall, aall, u, w, do, comm=None):
    t = q.shape[0]
    rows = DN_STEP * DNC
    n = t // rows

    def body(q_ref, k_ref, v_ref, b_ref, g_ref, sall_ref, aall_ref, u_ref, w_ref, do_ref,
             dq_ref, dk_ref, dv_ref, db_ref, dg_ref, ds_sc):
        i = pl.program_id(0)

        @pl.when(i == 0)
        def _():
            ds_sc[...] = jnp.zeros_like(ds_sc)

        mk = _dn_masks()
        strict = mk["strict"]
        hs = range(HEADS)
        at = lambda cc, h: (slice(cc * DNC, (cc + 1) * DNC), slice(h * HD, (h + 1) * HD))
        rowsum = lambda a: jnp.sum(a, axis=-1, keepdims=True)
        dsn = [ds_sc[h] for h in hs]
        for cc in reversed(range(DN_STEP)):
            q = [q_ref[at(cc, h)] for h in hs]
            k = [k_ref[at(cc, h)] for h in hs]
            b = [b_ref[at(cc, h)] for h in hs]
            u = [u_ref[at(cc, h)] for h in hs]
            w = [w_ref[at(cc, h)] for h in hs]
            do = [do_ref[at(cc, h)] for h in hs]
            s = [sall_ref[cc, h] for h in hs]
            dec = [_dn_decay(g_ref[at(cc, h)], mk) for h in hs]
            dm, e, f = [d[0] for d in dec], [d[1] for d in dec], [d[2] for d in dec]
            egl = [jnp.exp(d[3]) for d in dec]
            kb = [k[h] * b[h] for h in hs]
            kc = [_c(k[h]) for h in hs]
            sb = [_c(s[h]) for h in hs]
            dob = [_c(do[h]) for h in hs]
            m = [_dot_nt(_c(kb[h]), kc[h]) for h in hs]
            p = [_dot_nt(_c(q[h]), kc[h]) for h in hs]
            vnb = [_c(u[h] - _dot(_c(w[h]), sb[h])) for h in hs]
            dsb = [_c(dsn[h]) for h in hs]
            dvn = [_dot_tn(_c(p[h] * dm[h]), dob[h]) + _dot(_c(k[h] * f[h]), dsb[h]) for h in hs]
            dov = [_c(jnp.concatenate([do[h], dvn[h]], axis=0)) for h in hs]
            t1 = [_dot_nt(dov[h], sb[h]) for h in hs]
            dattn = [_dot_nt(dob[h], vnb[h]) for h in hs]
            dkt = [_dot_nt(vnb[h], dsb[h]) for h in hs]
            dgl = [jnp.sum(jnp.sum(dsn[h] * s[h], axis=1, keepdims=True), axis=0, keepdims=True) * egl[h][:, 0:1]
                   for h in hs]
            dsn = [dsn[h] * egl[h] + _dot_tn(_c(jnp.concatenate([q[h] * e[h], -w[h]], axis=0)), dov[h]) for h in hs]
            dqd = [t1[h][0:DNC] for h in hs]
            dw = [-t1[h][DNC:] for h in hs]
            ab = [_dot3(aall_ref[cc, h], jnp.concatenate([dvn[h], dw[h]], axis=1), TN) for h in hs]
            dlo = [jnp.where(strict, -_dot3(ab[h], jnp.concatenate([u[h], w[h]], axis=1), NT), 0.0) for h in hs]
            dpm = [_c(jnp.concatenate([dattn[h] * dm[h], dlo[h] * dm[h]], axis=0)) for h in hs]
            t2 = [_dot(dpm[h], kc[h]) for h in hs]
            t4 = [_dot_tn(dpm[h], _c(jnp.concatenate([q[h], kb[h]], axis=0))) for h in hs]
            dff = [rowsum(dkt[h] * k[h]) * f[h][:, 0:1] for h in hs]
            de = [rowsum(dqd[h] * q[h]) + rowsum(ab[h][:, HD:] * kb[h]) for h in hs]
            dd = [(dattn[h] * p[h] + dlo[h] * m[h]) * dm[h] for h in hs]
            t3 = [_mask_dot(mk["upper"], jnp.concatenate(
                [jnp.broadcast_to(de[h] * e[h][:, 0:1] - dff[h], (DNC, HD)), dd[h]], axis=1)) for h in hs]
            for h in hs:
                dvb, dkbe = ab[h][:, 0:HD], ab[h][:, HD:]
                dkb = t2[h][DNC:] + dkbe * e[h]
                dbeta = rowsum(dkb * k[h]) + rowsum(dvb * v_ref[at(cc, h)])
                dg = (rowsum(jnp.where(strict, t3[h][:, HD:HD + DNC], 0.0)) + t3[h][:, 0:1]
                      + dgl[h] + jnp.sum(dff[h], axis=0, keepdims=True))
                dq_ref[at(cc, h)] = dqd[h] * e[h] + t2[h][0:DNC]
                dk_ref[at(cc, h)] = t4[h] + dkt[h] * f[h] + dkb * b[h]
                dv_ref[at(cc, h)] = dvb * b[h]
                db_ref[at(cc, h)] = jnp.broadcast_to(dbeta, (DNC, HD))
                dg_ref[at(cc, h)] = jnp.broadcast_to(dg, (DNC, HD))
        for h in hs:
            ds_sc[h] = dsn[h]

    row = pl.BlockSpec((rows, HW), lambda i: (n - 1 - i, 0))
    return _carried_call(
        body, comm, 10, 5, 1, _dn_when(n),
        grid=(n,),
        in_specs=[row] * 5 + [pl.BlockSpec((DN_STEP, HEADS, HD, HD), lambda i: (n - 1 - i, 0, 0, 0)),
                              pl.BlockSpec((DN_STEP, HEADS, DNC, DNC), lambda i: (n - 1 - i, 0, 0, 0)), row, row, row],
        out_specs=[row] * 5,
        out_shape=[jax.ShapeDtypeStruct((t, HW), F32)] * 5,
        scratch_shapes=[pltpu.VMEM((HEADS, HD, HD), F32)],
        operands=(q, k, v, beta, g, sall, aall, u, w, do), name="dn_bwd")


def _group_norm(a, nw):
    rs = []
    for h in range(HEADS):
        ah = a[:, h * HD:(h + 1) * HD]
        rs.append(jnp.broadcast_to(_rstd(ah), ah.shape))
    r = jnp.concatenate(rs, axis=-1)
    xh = a * r
    return xh * nw, xh, r


def _group_norm_bwd(dy, xh, r, nw):
    dxh = dy * nw
    return r * (dxh - xh * (_head_sum(dxh * xh) * (1.0 / HD)))


def _sg_mix(wt_ref, svn_b, nchunk):
    rows = []
    for cidx in range(nchunk):
        cols = []
        for g in range(HEADS):
            blk = svn_b[cidx * SGC:(cidx + 1) * SGC, g * HD:(g + 1) * HD]
            cols.append(_dot(wt_ref[g], blk))
        rows.append(jnp.concatenate(cols, axis=-1))
    return jnp.concatenate(rows, axis=0)


def _ab_out(x1, o, z, su, sv, dnw, sgnw, wtril, sgb, wout):
    t = x1.shape[0]
    nchunk = TM // SGC

    def body(x_ref, o_ref, z_ref, su_ref, sv_ref, dnw_ref, sgnw_ref, wt_ref, sgb_ref, wo_ref, x2_ref, cat_ref):
        on, _, _ = _group_norm(o_ref[...], dnw_ref[...])
        zv = z_ref[...]
        cat_ref[:, 0:HW] = _c(on * (zv * _sigmoid(zv)))
        svn, _, _ = _group_norm(_gelu(sv_ref[...]), sgnw_ref[...])
        mixed = _sg_mix(wt_ref, _c(svn), nchunk) + jnp.tile(sgb_ref[...], (nchunk, 1))
        cat_ref[:, HW:] = _c(_gelu(su_ref[...]) * mixed)
        x2_ref[...] = x_ref[...] + _dot(cat_ref[...], wo_ref[...])

    row = lambda w: pl.BlockSpec((TM, w), lambda i: (i, 0))
    full = lambda a: pl.BlockSpec(a.shape, lambda i: (0,) * a.ndim)
    return pl.pallas_call(
        body,
        grid=(t // TM,),
        in_specs=[row(D)] + [row(HW)] * 4 + [full(dnw), full(sgnw), full(wtril), full(sgb), full(wout)],
        out_specs=[row(D), row(D)],
        out_shape=[jax.ShapeDtypeStruct((t, D), F32), jax.ShapeDtypeStruct((t, D), _MM)],
        name="ab_out",
    )(x1, o, z, su, sv, dnw, sgnw, wtril, sgb, wout)


def _ab_out_bwd(dx2, o, z, su, sv, dnw, sgnw, wtril, wtril_t, sgb, wout):
    t = dx2.shape[0]
    nchunk = TM // SGC

    def body(dx_ref, o_ref, z_ref, su_ref, sv_ref, dnw_ref, sgnw_ref, wt_ref, wtt_ref, sgb_ref, wo_ref,
             do_ref, dz_ref, dsu_ref, dsv_ref, ddnw_ref, dsgnw_ref, dsgw_ref, dsgb_ref):
        i = pl.program_id(0)
        dcat = _dot_nt(_c(dx_ref[...]), wo_ref[...])
        doa = dcat[:, 0:HW]
        dob = dcat[:, HW:]
        on, oh, ro = _group_norm(o_ref[...], dnw_ref[...])
        zv = z_ref[...]
        sz = _sigmoid(zv)
        dz_ref[...] = _c(doa * on * (sz * (1.0 + zv * (1.0 - sz))))
        don = doa * (zv * sz)
        do_ref[...] = _group_norm_bwd(don, oh, ro, dnw_ref[...])
        dd = jnp.sum(don * oh, axis=0, keepdims=True)
        _accum(ddnw_ref, dd[:, 0:HD] + dd[:, HD:2 * HD] + dd[:, 2 * HD:3 * HD] + dd[:, 3 * HD:], i)
        suv = su_ref[...]
        svv = sv_ref[...]
        svg = _gelu(svv)
        svn, sh, rs = _group_norm(svg, sgnw_ref[...])
        svn_b = _c(svn)
        mixed = _sg_mix(wt_ref, svn_b, nchunk) + jnp.tile(sgb_ref[...], (nchunk, 1))
        dsu_ref[...] = _c(dob * mixed * _gelu_grad(suv))
        dmixed = dob * _gelu(suv)
        dmb = _c(dmixed)
        tri = lax.broadcasted_iota(jnp.int32, (SGC, SGC), 0) >= lax.broadcasted_iota(jnp.int32, (SGC, SGC), 1)
        lane = lax.broadcasted_iota(jnp.int32, (SGC, HD), 1)
        rows = []
        dbias = jnp.zeros((SGC, HD), F32)
        for g in range(HEADS):
            gs = slice(g * HD, (g + 1) * HD)
            dwg = jnp.zeros((SGC, SGC), F32)
            col = jnp.zeros((SGC, 1), F32)
            for cidx in range(nchunk):
                cs = slice(cidx * SGC, (cidx + 1) * SGC)
                dwg = dwg + _dot_nt(dmb[cs, gs], svn_b[cs, gs])
                col = col + jnp.sum(dmixed[cs, gs], axis=-1, keepdims=True)
            _accum(dsgw_ref.at[g], jnp.where(tri, dwg, 0.0), i)
            dbias = dbias + jnp.where(lane == g, col, 0.0)
        _accum(dsgb_ref, dbias, i)
        for cidx in range(nchunk):
            cs = slice(cidx * SGC, (cidx + 1) * SGC)
            rows.append(jnp.concatenate(
                [_dot(wtt_ref[g], dmb[cs, g * HD:(g + 1) * HD]) for g in range(HEADS)], axis=-1))
        dsvn = jnp.concatenate(rows, axis=0)
        _accum(dsgnw_ref, jnp.sum(dsvn * sh, axis=0, keepdims=True), i)
        dsv_ref[...] = _c(_group_norm_bwd(dsvn, sh, rs, sgnw_ref[...]) * _gelu_grad(svv))

    row = lambda w: pl.BlockSpec((TM, w), lambda i: (i, 0))
    full = lambda a: pl.BlockSpec(a.shape, lambda i: (0,) * a.ndim)
    const = lambda shape: pl.BlockSpec(shape, lambda i: (0,) * len(shape))
    return pl.pallas_call(
        body,
        grid=(t // TM,),
        in_specs=[row(D)] + [row(HW)] * 4 + [full(dnw), full(sgnw), full(wtril), full(wtril_t), full(sgb), full(wout)],
        out_specs=[row(HW)] * 4 + [const((1, HD)), const((1, HW)), const((HEADS, SGC, SGC)), const((SGC, HD))],
        out_shape=[jax.ShapeDtypeStruct((t, HW), F32)] + [jax.ShapeDtypeStruct((t, HW), _MM)] * 3
        + [jax.ShapeDtypeStruct((1, HD), F32), jax.ShapeDtypeStruct((1, HW), F32),
           jax.ShapeDtypeStruct((HEADS, SGC, SGC), F32), jax.ShapeDtypeStruct((SGC, HD), F32)],
        name="ab_out_bwd",
    )(dx2, o, z, su, sv, dnw, sgnw, wtril, wtril_t, sgb, wout)


def _dn_pre_bwd(qkv, b_rep, a_rep, cw, alog, dtb, dqn, dkn, dv, dbeta, dg):
    t = qkv.shape[0]
    qscale = HD ** -0.5

    def body(x_ref, halo_ref, b_ref, a_ref, cw_ref, alog_ref, dt_ref, dq_ref, dk_ref, dv_ref, dbeta_ref, dg_ref,
             dc_ref, dba_ref, dcw_ref, dalog_ref, ddt_ref):
        i = pl.program_id(0)
        halo = jnp.where(i == 0, 0.0, halo_ref[...])
        c, shifted = _conv_rows(x_ref[...], halo, cw_ref[...])
        s = _sigmoid(c)
        sc = c * s
        q = sc[:, 0:HW]
        k = sc[:, HW:2 * HW]
        rq = _head_rsq(q)
        rk = _head_rsq(k)
        qu = q * rq
        ku = k * rk
        dqn = dq_ref[...]
        dkn = dk_ref[...]
        dq = qscale * rq * (dqn - qu * _head_sum(dqn * qu))
        dk = rk * (dkn - ku * _head_sum(dkn * ku))
        dsc = jnp.concatenate([dq, dk, dv_ref[...]], axis=-1)
        dc = dsc * (s * (1.0 + c * (1.0 - s)))
        dc_ref[...] = dc
        for kk in range(4):
            _accum(dcw_ref.at[kk], jnp.sum(dc * shifted[kk], axis=0, keepdims=True), i)
        beta = _sigmoid(b_ref[...])
        dbp = dbeta_ref[...] * beta * (1.0 - beta)
        nea = -jnp.exp(alog_ref[...])
        spin = a_ref[...] + dt_ref[...]
        dgv = dg_ref[...]
        dap = dgv * nea * _sigmoid(spin)
        _accum(dalog_ref, jnp.sum(dgv * nea * _softplus(spin), axis=0, keepdims=True), i)
        _accum(ddt_ref, jnp.sum(dap, axis=0, keepdims=True), i)
        lane = lax.broadcasted_iota(jnp.int32, (TM, HD), 1)
        dba = jnp.zeros((TM, HD), F32)
        for h in range(HEADS):
            dba = dba + jnp.where(lane == h, dbp[:, h * HD:(h + 1) * HD], 0.0)
            dba = dba + jnp.where(lane == HEADS + h, dap[:, h * HD:(h + 1) * HD], 0.0)
        dba_ref[...] = _c(dba)

    row = lambda w: pl.BlockSpec((TM, w), lambda i: (i, 0))
    full = lambda a: pl.BlockSpec(a.shape, lambda i: (0,) * a.ndim)
    const = lambda shape: pl.BlockSpec(shape, lambda i: (0,) * len(shape))
    return pl.pallas_call(
        body,
        grid=(t // TM,),
        in_specs=[row(QKV), _halo_prev_spec(QKV, 8), row(HW), row(HW), full(cw), full(alog), full(dtb)] + [row(HW)] * 5,
        out_specs=[row(QKV), row(HD), const((4, 1, QKV)), const((1, HW)), const((1, HW))],
        out_shape=[jax.ShapeDtypeStruct((t, QKV), F32), jax.ShapeDtypeStruct((t, HD), _MM),
                   jax.ShapeDtypeStruct((4, 1, QKV), F32), jax.ShapeDtypeStruct((1, HW), F32),
                   jax.ShapeDtypeStruct((1, HW), F32)],
        name="dn_pre_bwd",
    )(qkv, qkv, b_rep, a_rep, cw, alog, dtb, dqn, dkn, dv, dbeta, dg)


def _conv_bwd(dc, cw):
    t = dc.shape[0]
    nt = t // TM

    def body(dc_ref, halo_ref, cw_ref, dx_ref):
        i = pl.program_id(0)
        halo = jnp.where(i == nt - 1, 0.0, halo_ref[...])
        de = jnp.concatenate([dc_ref[...], halo], axis=0)
        cwv = cw_ref[...]
        acc = None
        for k in range(4):
            s = 3 - k
            ds = (de if s == 0 else pltpu.roll(de, TM + 8 - s, 0))[0:TM, :]
            term = cwv[k:k + 1, :] * ds
            acc = term if acc is None else acc + term
        dx_ref[...] = _c(acc)

    return pl.pallas_call(
        body,
        grid=(nt,),
        in_specs=[pl.BlockSpec((TM, QKV), lambda i: (i, 0)), _halo_next_spec(QKV, 8, t),
                  pl.BlockSpec(cw.shape, lambda i: (0, 0))],
        out_specs=pl.BlockSpec((TM, QKV), lambda i: (i, 0)),
        out_shape=jax.ShapeDtypeStruct((t, QKV), _MM),
        name="conv_bwd",
    )(dc, dc, cw)


def _ab_proj_bwd(x1, nw, dqkv, dz, dsu, dsv, dba, wab_b, dres):
    t = x1.shape[0]

    def body(x_ref, nw_ref, dqkv_ref, dz_ref, dsu_ref, dsv_ref, dba_ref, w_ref, dres_ref, dx_ref, dcat_ref, dnw_ref):
        i = pl.program_id(0)
        dcat_ref[:, 0:QKV] = dqkv_ref[...]
        o = QKV
        for ref in (dz_ref, dsu_ref, dsv_ref):
            dcat_ref[:, o:o + HW] = ref[...]
            o += HW
        dcat_ref[:, o:o + 128] = dba_ref[...]
        dh = _dot_nt(dcat_ref[...], w_ref[...])
        xv = x_ref[...]
        r = _rstd(xv)
        dx, dnw = _rms_bwd(dh, xv * r, r, nw_ref[...])
        dx_ref[...] = dres_ref[...] + dx
        _accum(dnw_ref, dnw, i)

    row = lambda w: pl.BlockSpec((TM, w), lambda i: (i, 0))
    return pl.pallas_call(
        body,
        grid=(t // TM,),
        in_specs=[row(D), pl.BlockSpec((1, D), lambda i: (0, 0)), row(QKV), row(HW), row(HW), row(HW), row(128),
                  pl.BlockSpec((D, PW_B), lambda i: (0, 0)), row(D)],
        out_specs=[row(D), row(PW_B), pl.BlockSpec((1, D), lambda i: (0, 0))],
        out_shape=[jax.ShapeDtypeStruct((t, D), F32), jax.ShapeDtypeStruct((t, PW_B), _MM),
                   jax.ShapeDtypeStruct((1, D), F32)],
        name="ab_proj_bwd",
    )(x1, nw, dqkv, dz, dsu, dsv, dba, wab_b, dres)


def _pool_counts(i):
    pos = (lax.broadcasted_iota(jnp.int32, (TM + HALO, 1), 0) + i * TM + 1).astype(F32)
    return [1.0 / jnp.minimum(pos, float(w)) for w in POOL_WINDOWS]


def _window_sum(ext, win, back):
    r = ext.shape[0]
    s = ext
    step = 1
    while step < win:
        s = s + pltpu.roll(s, step if back else r - step, 0)
        step *= 2
    return s


def _pooled(h_ext, invc, g):
    gs = slice(g * PG, (g + 1) * PG)
    he = h_ext[:, gs]
    ws = _window_sum(he, POOL_WINDOWS[g], True)[HALO:, :]
    return ws * invc[g][0:TM, :] - he[HALO:, :]


def _pool_fwd(x1, nw, pw, scale):
    t = x1.shape[0]

    def body(x_ref, halo_ref, nw_ref, pw_ref, sc_ref, x2_ref):
        i = pl.program_id(0)
        xv = x_ref[...]
        hv = halo_ref[...]
        nwv = nw_ref[...]
        h_ext = jnp.concatenate([jnp.where(i == 0, 0.0, hv * _rstd(hv) * nwv), xv * _rstd(xv) * nwv], axis=0)
        invc = _pool_counts(i)
        outs = [_dot(_c(_pooled(h_ext, invc, g)), pw_ref[g]) for g in range(4)]
        x2_ref[...] = xv + jnp.concatenate(outs, axis=-1) * sc_ref[...]

    return pl.pallas_call(
        body,
        grid=(t // TM,),
        in_specs=[pl.BlockSpec((TM, D), lambda i: (i, 0)), _halo_prev_spec(D, HALO),
                  pl.BlockSpec((1, D), lambda i: (0, 0)), pl.BlockSpec((4, PG, PG), lambda i: (0, 0, 0)),
                  pl.BlockSpec((1, D), lambda i: (0, 0))],
        out_specs=pl.BlockSpec((TM, D), lambda i: (i, 0)),
        out_shape=jax.ShapeDtypeStruct((t, D), F32),
        name="pool_fwd",
    )(x1, x1, nw, pw, scale)


def _pool_bwd(x1, nw, pw, scale, dx2):
    t = x1.shape[0]
    nt = t // TM

    def body(x_ref, halo_ref, nw_ref, pw_ref, sc_ref, dx2_ref, dnext_ref, dx_ref, dnw_ref, dpw_ref, dsc_ref):
        i = pl.program_id(0)
        xv = x_ref[...]
        hv = halo_ref[...]
        nwv = nw_ref[...]
        r = _rstd(xv)
        xh = xv * r
        h_ext = jnp.concatenate([jnp.where(i == 0, 0.0, hv * _rstd(hv) * nwv), xh * nwv], axis=0)
        invc = _pool_counts(i)
        dyv = dx2_ref[...]
        dout_ext = jnp.concatenate([dyv, jnp.where(i == nt - 1, 0.0, dnext_ref[...])], axis=0) * sc_ref[...]
        dh_parts = []
        dsc_parts = []
        for g in range(4):
            gs = slice(g * PG, (g + 1) * PG)
            pooled_b = _c(_pooled(h_ext, invc, g))
            dout_b = _c(dout_ext[:, gs])
            dsc_parts.append(jnp.sum(dyv[:, gs] * _dot(pooled_b, pw_ref[g]), axis=0, keepdims=True))
            _accum(dpw_ref.at[g], _dot_tn(pooled_b, dout_b[0:TM, :]), i)
            dpool_ext = _dot_nt(dout_b, pw_ref[g])
            lead = _window_sum(dpool_ext * invc[g], POOL_WINDOWS[g], False)[0:TM, :]
            dh_parts.append(lead - dpool_ext[0:TM, :])
        _accum(dsc_ref, jnp.concatenate(dsc_parts, axis=-1), i)
        dx, dnw = _rms_bwd(jnp.concatenate(dh_parts, axis=-1), xh, r, nwv)
        dx_ref[...] = dyv + dx
        _accum(dnw_ref, dnw, i)

    vec = pl.BlockSpec((1, D), lambda i: (0, 0))
    return pl.pallas_call(
        body,
        grid=(nt,),
        in_specs=[pl.BlockSpec((TM, D), lambda i: (i, 0)), _halo_prev_spec(D, HALO), vec,
                  pl.BlockSpec((4, PG, PG), lambda i: (0, 0, 0)), vec,
                  pl.BlockSpec((TM, D), lambda i: (i, 0)), _halo_next_spec(D, HALO, t)],
        out_specs=[pl.BlockSpec((TM, D), lambda i: (i, 0)), vec, pl.BlockSpec((4, PG, PG), lambda i: (0, 0, 0)), vec],
        out_shape=[jax.ShapeDtypeStruct((t, D), F32), jax.ShapeDtypeStruct((1, D), F32),
                   jax.ShapeDtypeStruct((4, PG, PG), F32), jax.ShapeDtypeStruct((1, D), F32)],
        name="pool_bwd",
    )(x1, x1, nw, pw, scale, dx2, dx2)


def _adamw(lands, w, m, v, rb, name):
    nl, nr = w.shape[0], w.shape[1]
    rest = w.shape[2:]
    ns = lands[0].shape[0]
    zeros = (0,) * len(rest)

    def body(*refs):
        l_refs = refs[0:nl]
        w_ref, m_ref, v_ref, g_ref, d_ref, m2_ref, v2_ref = refs[nl:]
        for l in range(nl):
            g = l_refs[l][0].astype(F32)
            for s in range(1, ns):
                g = g + l_refs[l][s].astype(F32)
            m2 = ADAM_B1 * m_ref[l] + (1.0 - ADAM_B1) * g
            v2 = ADAM_B2 * v_ref[l] + (1.0 - ADAM_B2) * (g * g)
            m_hat = m2 / (1.0 - ADAM_B1 ** ADAM_STEP)
            v_hat = v2 / (1.0 - ADAM_B2 ** ADAM_STEP)
            g_ref[l] = g
            d_ref[l] = -ADAM_LR * (m_hat / (jnp.sqrt(v_hat) + ADAM_EPS) + ADAM_WD * w_ref[l])
            m2_ref[l] = m2
            v2_ref[l] = v2

    lspec = pl.BlockSpec((ns, rb) + rest, lambda r: (0, r) + zeros)
    wspec = pl.BlockSpec((nl, rb) + rest, lambda r: (0, r) + zeros)
    return pl.pallas_call(
        body,
        grid=(nr // rb,),
        in_specs=[lspec] * nl + [wspec] * 3,
        out_specs=[wspec] * 4,
        out_shape=[jax.ShapeDtypeStruct(w.shape, F32)] * 4,
        name=name,
    )(*lands, w, m, v)


WEIGHT_ORDER = ("ffn_norm1", "ffn1_w_in", "ffn1_w_out", "mix_norm", "ffn_norm2", "ffn2_w_in", "ffn2_w_out", "ab_w_in",
                "dn_conv_w", "dn_a_log", "dn_dt_bias", "dn_out_norm", "sg_norm", "sg_w", "sg_b", "ab_w_out", "pool_w",
                "pool_scale", "final_norm")
R_SMALL = 88
SMALL_ROWS = (
    ("ffn_norm1", (2, D), 2), ("mix_norm", (2, D), 2), ("ffn_norm2", (2, D), 2), ("final_norm", (D,), 1),
    ("sg_w", (1, 4, SGC, SGC), 64), ("sg_norm", (1, 4, HD), 1), ("sg_b", (1, 4, SGC), 1), ("dn_out_norm", (1, HD), 1),
    ("dn_a_log", (1, 4), 1), ("dn_dt_bias", (1, 4), 1), ("pool_scale", (1, D), 1), ("dn_conv_w", (1, 4, QKV), 8),
)
SMALL_SHARDED = ("pool_scale", "dn_conv_w")


def _rows_of(a, rows):
    if a.shape[-1] == QKV:
        return jnp.pad(a.reshape(4, QKV), ((0, 0), (0, 2 * ROW - QKV))).reshape(8, ROW)
    n = _numel(a.shape)
    if n % ROW == 0:
        return a.reshape(n // ROW, ROW)
    return jnp.pad(a.reshape(1, n), ((0, 0), (0, ROW - n)))


def _from_rows(r, shape):
    if shape[-1] == QKV:
        return r.reshape(4, 2 * ROW)[:, 0:QKV].reshape(shape)
    n = _numel(shape)
    if n % ROW == 0:
        return r.reshape(shape)
    return r[:, 0:n].reshape(shape)


def _pack_small(vals):
    parts = [(_rows_of(vals[n].astype(F32), r) if n in vals else jnp.zeros((r, ROW), F32)) for n, _, r in SMALL_ROWS]
    used = sum(r for _, _, r in SMALL_ROWS)
    return jnp.concatenate(parts + [jnp.zeros((R_SMALL - used, ROW), F32)], axis=0)


def _unpack_small(packed):
    out, o = {}, 0
    for n, shape, r in SMALL_ROWS:
        out[n] = _from_rows(packed[o:o + r], shape)
        o += r
    return out


def _pack_small_shard(ps, cw):
    return jnp.concatenate([
        jnp.pad(ps, ((0, 0), (0, ROW - D // N_DEV))), jnp.pad(cw[0], ((0, 0), (0, ROW - QKV // N_DEV))),
        jnp.zeros((3, ROW), F32)], axis=0)


def _mixer_weights(g_in, g_out, g_small, small):
    w = {}
    wi = jnp.transpose(g_in, (1, 0, 2)).reshape(D, AB_IN)
    main = [wi[:, 0:2048], wi[:, 2056:AB_IN]]
    w["wab_f"] = jnp.concatenate(
        main + [jnp.repeat(wi[:, 2048:2052], HD, axis=1), jnp.repeat(wi[:, 2052:2056], HD, axis=1)], axis=1)
    w["wab_b"] = jnp.concatenate(main + [wi[:, 2048:2056], jnp.zeros((D, 120), wi.dtype)], axis=1)
    w["cw"] = jnp.transpose(g_small[:, 1:5, 0:QKV // N_DEV], (1, 0, 2)).reshape(4, QKV)
    w["ps"] = g_small[:, 0, 0:D // N_DEV].reshape(1, D)
    w["alog"] = jnp.repeat(small["dn_a_log"][0].astype(F32), HD).reshape(1, HW)
    w["dtb"] = jnp.repeat(small["dn_dt_bias"][0].astype(F32), HD).reshape(1, HW)
    w["dnw"] = jnp.tile(small["dn_out_norm"][0].astype(F32), HEADS).reshape(1, HW)
    w["sgnw"] = small["sg_norm"][0].astype(F32).reshape(1, HW)
    tri = jnp.tril(jnp.ones((SGC, SGC), dtype=bool))
    wt = jnp.where(tri, small["sg_w"][0].astype(F32), 0.0)
    w["wtril"] = _c(wt)
    w["wtril_t"] = _c(jnp.transpose(wt, (0, 2, 1)))
    w["sgb"] = jnp.repeat(jnp.transpose(small["sg_b"][0].astype(F32)), HD, axis=1)
    w["wout_ab"] = g_out.reshape(D, D)
    return w


def kernel(x, ffn_norm1, ffn1_w_in, ffn1_w_out, mix_norm, ffn_norm2, ffn2_w_in, ffn2_w_out, ab_w_in, dn_conv_w, dn_a_log, dn_dt_bias, dn_out_norm, sg_norm, sg_w, sg_b, ab_w_out, pool_w, pool_scale, final_norm, loss_target, m_ffn_norm1, m_ffn1_w_in, m_ffn1_w_out, m_mix_norm, m_ffn_norm2, m_ffn2_w_in, m_ffn2_w_out, m_ab_w_in, m_dn_conv_w, m_dn_a_log, m_dn_dt_bias, m_dn_out_norm, m_sg_norm, m_sg_w, m_sg_b, m_ab_w_out, m_pool_w, m_pool_scale, m_final_norm, v_ffn_norm1, v_ffn1_w_in, v_ffn1_w_out, v_mix_norm, v_ffn_norm2, v_ffn2_w_in, v_ffn2_w_out, v_ab_w_in, v_dn_conv_w, v_dn_a_log, v_dn_dt_bias, v_dn_out_norm, v_sg_norm, v_sg_w, v_sg_b, v_ab_w_out, v_pool_w, v_pool_scale, v_final_norm):
    wl = dict(ffn_norm1=ffn_norm1, mix_norm=mix_norm, ffn_norm2=ffn_norm2, dn_a_log=dn_a_log, dn_dt_bias=dn_dt_bias,
              dn_out_norm=dn_out_norm, sg_norm=sg_norm, sg_w=sg_w, sg_b=sg_b, final_norm=final_norm)
    ml = dict(ffn_norm1=m_ffn_norm1, mix_norm=m_mix_norm, ffn_norm2=m_ffn_norm2, dn_a_log=m_dn_a_log,
              dn_dt_bias=m_dn_dt_bias, dn_out_norm=m_dn_out_norm, sg_norm=m_sg_norm, sg_w=m_sg_w, sg_b=m_sg_b,
              final_norm=m_final_norm)
    vl = dict(ffn_norm1=v_ffn_norm1, mix_norm=v_mix_norm, ffn_norm2=v_ffn_norm2, dn_a_log=v_dn_a_log,
              dn_dt_bias=v_dn_dt_bias, dn_out_norm=v_dn_out_norm, sg_norm=v_sg_norm, sg_w=v_sg_w, sg_b=v_sg_b,
              final_norm=v_final_norm)
    row = lambda a: a.reshape(1, -1).astype(F32)
    n1 = [row(ffn_norm1[l]) for l in range(2)]
    n2 = [row(ffn_norm2[l]) for l in range(2)]
    mix = [row(mix_norm[l]) for l in range(2)]
    s_in = {(f, l): _c(wf[l]) for f, wf in enumerate((ffn1_w_in, ffn2_w_in)) for l in range(2)}
    s_out = {(f, l): _c(wf[l]) for f, wf in enumerate((ffn1_w_out, ffn2_w_out)) for l in range(2)}
    xs, tgt = x[0], loss_target[0]

    wi00, wo00 = _comm_call(_Comm("gather", [s_in[0, 0], s_out[0, 0]]), "gather_first")
    x01, gu00, (g_abin, g_about, g_small, wi10) = _ffn_fwd(
        xs, n1[0], wi00, wo00,
        comm=_Comm("gather", [_c(ab_w_in[0]), _c(ab_w_out[0]), _pack_small_shard(pool_scale, dn_conv_w), s_in[1, 0]]))
    w = _mixer_weights(g_abin, g_about, g_small, wl)
    h, qkv, z, su, sv, b_rep, a_rep = _ab_proj(x01, mix[0], w["wab_f"])
    qn, kn, v, beta, g = _dn_pre(qkv, b_rep, a_rep, w["cw"], w["alog"], w["dtb"])
    (o, sall, aall, dn_u, dn_w), (wo10, g_pw) = _dn_fwd(
        qn, kn, v, beta, g, comm=_Comm("gather", [s_out[1, 0], _c(pool_w[0])]))
    pw = jnp.transpose(g_pw, (1, 0, 2, 3)).reshape(4, PG, PG)
    x02, cat = _ab_out(x01, o, z, su, sv, w["dnw"], w["sgnw"], w["wtril"], w["sgb"], w["wout_ab"])
    x10, gu10, (wi01, wo01) = _ffn_fwd(x02, n2[0], wi10, wo10, comm=_Comm("gather", [s_in[0, 1], s_out[0, 1]]))
    x11, gu01, (wi11, wo11) = _ffn_fwd(x10, n1[1], wi01, wo01, comm=_Comm("gather", [s_in[1, 1], s_out[1, 1]]))
    x12 = _pool_fwd(x11, mix[1], pw, w["ps"])
    x13, gu11, _ = _ffn_fwd(x12, n2[1], wi11, wo11)
    loss_local, dx, d_fn = _loss_head(x13, row(final_norm), tgt)

    bt = min(BT, xs.shape[0])

    def ffn_b(xin, nw, w_in, w_out, gu, dy, comm=None):
        (dxn, xn, act, dh, dnw), landed = _ffn_bwd(xin, nw, w_in, w_out, gu, dy, comm)
        return dxn, dnw, (xn, act, dh), landed

    def ffn_g(kept, dy):
        return [_mm_tn_win(kept[0], kept[2])[0], _mm_tn_wout(kept[1], dy)]

    dy = dx
    dx, d_n2_1, kept, _ = ffn_b(x12, n2[1], wi11, wo11, gu11, dy)
    g11 = ffn_g(kept, dy)
    dx, d_mix_1, d_pw, d_ps = _pool_bwd(x11, mix[1], pw, w["ps"], dx)
    d_pw_sh = _c(jnp.transpose(d_pw.reshape(4, N_DEV, PG // N_DEV, PG), (1, 0, 2, 3)))
    dy = dx
    dx, d_n1_1, kept, land11 = ffn_b(x10, n1[1], wi01, wo01, gu01, dy, _Comm("exchange", g11))
    g01 = ffn_g(kept, dy)
    dy = dx
    dx, d_n2_0, kept, land01 = ffn_b(x02, n2[0], wi10, wo10, gu10, dy, _Comm("exchange", g01 + [d_pw_sh]))
    g10 = ffn_g(kept, dy)
    do, dz, dsu, dsv, d_dnw, d_sgnw, d_sgw, d_sgb = _ab_out_bwd(
        dx, o, z, su, sv, w["dnw"], w["sgnw"], w["wtril"], w["wtril_t"], w["sgb"], w["wout_ab"])
    d_about = _mm_tn(cat, dx, D, D, bt, _MM, "mm_tn_about").reshape(N_DEV, D // N_DEV, D)
    (dqn, dkn, dv, dbeta, dg), _ = _dn_bwd(qn, kn, v, beta, g, sall, aall, dn_u, dn_w, do)
    dc, dba, d_cw, d_alog, d_dtb = _dn_pre_bwd(qkv, b_rep, a_rep, w["cw"], w["alog"], w["dtb"], dqn, dkn, dv, dbeta, dg)
    dqkv = _conv_bwd(dc, w["cw"])
    dx, dcat, d_mix_0 = _ab_proj_bwd(x01, mix[0], dqkv, dz, dsu, dsv, dba, w["wab_b"], dx)
    d_wab = _mm_tn(h, dcat, D, 640, bt, _MM, "mm_tn_abin")
    d_abin = jnp.concatenate([d_wab[:, 0:2048], d_wab[:, 3072:3080], d_wab[:, 2048:3072]], axis=1)
    d_abin_sh = jnp.transpose(d_abin.reshape(D, N_DEV, AB_IN // N_DEV), (1, 0, 2))
    dy = dx
    grad_x, d_n1_0, kept, land_mid = ffn_b(xs, n1[0], wi00, wo00, gu00, dy,
                                           _Comm("exchange", g10 + [d_abin_sh, d_about]))
    land10, land_ab = land_mid[0:2], land_mid[2:4]

    g_small = {
        "ffn_norm1": jnp.concatenate([d_n1_0, d_n1_1], axis=0),
        "mix_norm": jnp.concatenate([d_mix_0, d_mix_1], axis=0),
        "ffn_norm2": jnp.concatenate([d_n2_0, d_n2_1], axis=0),
        "dn_conv_w": d_cw.reshape(1, 4, QKV),
        "dn_a_log": d_alog[:, ::HD],
        "dn_dt_bias": d_dtb[:, ::HD],
        "dn_out_norm": d_dnw,
        "sg_norm": d_sgnw.reshape(1, HEADS, HD),
        "sg_w": d_sgw[None],
        "sg_b": jnp.transpose(d_sgb[:, 0:HEADS])[None],
        "pool_scale": d_ps,
        "final_norm": d_fn.reshape(D),
    }
    g00_out = _mm_tn_wout(kept[1], dy)
    g00_in_a, (land00_out, land_small) = _mm_tn_win(
        kept[0], kept[2], comm=_Comm("exchange", [g00_out], repl=[_pack_small(g_small)]), half=0)
    g00_in_b, (land_a,) = _mm_tn_win(kept[0], kept[2], comm=_Comm("exchange", [g00_in_a]), half=1)
    (land_b,) = _comm_call(_Comm("exchange", [g00_in_b]), "exchange_last")
    land00_in = jnp.concatenate([land_a, land_b], axis=1)

    res = {}
    res["ffn1_w_in"] = _adamw([land00_in, land01[0]], ffn1_w_in, m_ffn1_w_in, v_ffn1_w_in, 128, "adamw_w_in")
    res["ffn2_w_in"] = _adamw([land10[0], land11[0]], ffn2_w_in, m_ffn2_w_in, v_ffn2_w_in, 128, "adamw_w_in")
    res["ffn1_w_out"] = _adamw([land00_out, land01[1]], ffn1_w_out, m_ffn1_w_out, v_ffn1_w_out, 176, "adamw_w_out")
    res["ffn2_w_out"] = _adamw([land10[1], land11[1]], ffn2_w_out, m_ffn2_w_out, v_ffn2_w_out, 176, "adamw_w_out")
    res["ab_w_in"] = _adamw([land_ab[0]], ab_w_in, m_ab_w_in, v_ab_w_in, 256, "adamw_ab_w_in")
    res["ab_w_out"] = _adamw([land_ab[1]], ab_w_out, m_ab_w_out, v_ab_w_out, D // N_DEV, "adamw_ab_w_out")
    res["pool_w"] = _adamw([land01[2]], pool_w, m_pool_w, v_pool_w, 4, "adamw_pool_w")
    sm = _adamw([land_small], _pack_small(wl)[None], _pack_small(ml)[None], _pack_small(vl)[None], R_SMALL,
                "adamw_replicated")
    sm = [_unpack_small(a[0]) for a in sm]
    for n in wl:
        res[n] = [d[n] for d in sm]
    me = 4 * lax.axis_index("x") + 2 * lax.axis_index("y") + lax.axis_index("c")
    g_ps = lax.dynamic_slice(sm[0]["pool_scale"], (0, me * (D // N_DEV)), (1, D // N_DEV))
    g_cw = lax.dynamic_slice(sm[0]["dn_conv_w"], (0, 0, me * (QKV // N_DEV)), (1, 4, QKV // N_DEV))
    s2 = _adamw([_pack_small_shard(g_ps, g_cw)[None]], _pack_small_shard(pool_scale, dn_conv_w)[None],
                _pack_small_shard(m_pool_scale, m_dn_conv_w)[None], _pack_small_shard(v_pool_scale, v_dn_conv_w)[None],
                8, "adamw_small_sharded")
    res["pool_scale"] = [a[0, 0:1, 0:D // N_DEV] for a in s2]
    res["dn_conv_w"] = [a[0, 1:5, 0:QKV // N_DEV][None] for a in s2]

    loss = lax.psum(loss_local[0, 0], ("x", "y", "c"))
    result = [loss, grad_x[None]]
    for i in range(4):
        result += [res[n][i] for n in WEIGHT_ORDER]
    return tuple(result)
```

```python
import jax
import jax.numpy as jnp
from jax import lax
from jax.experimental import pallas as pl
from jax.experimental.pallas import tpu as pltpu

F32 = jnp.float32
_MM = jnp.bfloat16

D = 1024
FF = 2816
EPS = 1e-6
HEADS = 4
HD = 128
DNC = 64
DN_STEP = 8
SGC = 128
QKV = 3 * HEADS * HD
HW = HEADS * HD
POOL_WINDOWS = (2, 4, 8, 16)
PG = D // 4
HALO = 16
N_DEV = 8
AB_IN = 3080
ROW = 1024

TM = 512
BT = 1024
FT = 512
FC = 704
NJ = FF // FC
WO_ROWS = FF // N_DEV

ADAM_LR, ADAM_B1, ADAM_B2, ADAM_EPS, ADAM_WD, ADAM_STEP = 0.001, 0.9, 0.999, 1e-08, 0.01, 10

MESH_T = pl.DeviceIdType.MESH
NN = (((1,), (0,)), ((), ()))
NT = (((1,), (1,)), ((), ()))
TN = (((0,), (0,)), ((), ()))


def _c(a):
    return a.astype(_MM)


def _dg(a, b, dims):
    return lax.dot_general(a, b, dims, preferred_element_type=F32)


def _dot(a, b):
    return _dg(a, b, NN)


def _dot_nt(a, b):
    return _dg(a, b, NT)


def _dot_tn(a, b):
    return _dg(a, b, TN)


def _split2(a):
    hi = _c(a)
    return hi, _c(a - hi.astype(F32))


def _dot3(a, b, dims=NN):
    ah, al = _split2(a)
    bh, bl = _split2(b)
    return _dg(ah, bh, dims) + (_dg(ah, bl, dims) + _dg(al, bh, dims))


def _mask_dot(mask, x):
    x1 = _c(x)
    r = x - x1.astype(F32)
    x2 = _c(r)
    x3 = _c(r - x2.astype(F32))
    return _dot(mask, x1) + (_dot(mask, x2) + _dot(mask, x3))


def _sigmoid(x):
    return jax.nn.sigmoid(x)


def _gelu(x):
    return 0.5 * x * (1.0 + lax.erf(x * 0.7071067811865476))


def _gelu_grad(x):
    return 0.5 * (1.0 + lax.erf(x * 0.7071067811865476)) + x * jnp.exp(-0.5 * x * x) * 0.3989422804014327


def _accum(ref, val, step):
    @pl.when(step == 0)
    def _():
        ref[...] = val

    @pl.when(step > 0)
    def _():
        ref[...] += val


def _rstd(x):
    return lax.rsqrt(jnp.mean(x * x, axis=-1, keepdims=True) + EPS)


def _rms_bwd(dy, xhat, r, nw):
    dnw = jnp.sum(dy * xhat, axis=0, keepdims=True)
    dxh = dy * nw
    dx = r * (dxh - xhat * jnp.mean(dxh * xhat, axis=-1, keepdims=True))
    return dx, dnw


def _numel(shape):
    n = 1
    for s in shape:
        n *= s
    return n


def _peer(k, x, y, c):
    px = 1 - x if k & 4 else x
    py = 1 - y if k & 2 else y
    pc = 1 - c if k & 1 else c
    return px, py, pc


class _Comm:
    def __init__(self, kind, arrs, repl=()):
        self.kind = kind
        self.ns = len(arrs)
        self.arrs = list(arrs) + list(repl)
        self.na = len(self.arrs)

    @property
    def out_shape(self):
        out = []
        for i, a in enumerate(self.arrs):
            lead = (N_DEV,) if (self.kind == "gather" or i >= self.ns) else ()
            out.append(jax.ShapeDtypeStruct(lead + a.shape, a.dtype))
        return out

    @property
    def scratch(self):
        return [pltpu.SemaphoreType.DMA((7 * self.na,)), pltpu.SemaphoreType.DMA((7 * self.na,)),
                pltpu.SemaphoreType.DMA((self.na,))]

    def phases(self, ins, outs, sems):
        send_sems, recv_sems, local_sems = sems
        na = self.na
        x, y, c = lax.axis_index("x"), lax.axis_index("y"), lax.axis_index("c")
        if self.kind == "gather":
            me, sibling = (x, y, c), (x, y, 1 - c)
            chips = [(1 - x, y), (x, 1 - y), (1 - x, 1 - y)]

            def slot(a, px, py, pc):
                return outs[a].at[4 * px + 2 * py + pc]

            def copy(a, k, block, to, src=None):
                return pltpu.make_async_remote_copy(
                    src_ref=slot(a, *block) if src is None else src, dst_ref=slot(a, *block),
                    send_sem=send_sems.at[7 * a + k], recv_sem=recv_sems.at[7 * a + k],
                    device_id=to, device_id_type=MESH_T)

            mine = [pltpu.make_async_copy(ins[a], slot(a, *me), local_sems.at[a]) for a in range(na)]
            first, passed = [], []
            for a in range(na):
                first.append(copy(a, 0, me, sibling, src=ins[a]))
                first += [copy(a, 1 + j, me, (*chip, c), src=ins[a]) for j, chip in enumerate(chips)]
                passed += [copy(a, 4 + j, (*chip, c), sibling) for j, chip in enumerate(chips)]

            def start():
                for cp in mine + first:
                    cp.start()

            def middle():
                for a in range(na):
                    for j, chip in enumerate(chips):
                        copy(a, 1 + j, (*chip, c), me).wait_recv()
                        passed[3 * a + j].start()

            def finish():
                for a in range(na):
                    copy(a, 0, sibling, me).wait_recv()
                    for j, chip in enumerate(chips):
                        copy(a, 4 + j, (*chip, 1 - c), me).wait_recv()
                for cp in first + passed:
                    cp.wait_send()
                for cp in mine:
                    cp.wait()

            return start, middle, finish

        me = 4 * x + 2 * y + c
        ns = self.ns
        own = [pltpu.make_async_copy(ins[a].at[me] if a < ns else ins[a], outs[a].at[me], local_sems.at[a])
               for a in range(na)]
        copies = []
        for k in range(1, N_DEV):
            px, py, pc = _peer(k, x, y, c)
            peer = 4 * px + 2 * py + pc
            for a in range(na):
                copies.append(pltpu.make_async_remote_copy(
                    src_ref=ins[a].at[peer] if a < ns else ins[a], dst_ref=outs[a].at[me],
                    send_sem=send_sems.at[na * (k - 1) + a], recv_sem=recv_sems.at[na * (k - 1) + a],
                    device_id=(px, py, pc), device_id_type=MESH_T))

        def start():
            for cp in own + copies:
                cp.start()

        def middle():
            pass

        def finish():
            for cp in copies:
                cp.wait()
            for cp in own:
                cp.wait()

        return start, middle, finish


def _comm_call(comm, name):
    na = comm.na

    def body(*refs):
        start, middle, finish = comm.phases(refs[0:na], refs[na:2 * na], refs[2 * na:])
        start()
        middle()
        finish()

    hbm = pl.BlockSpec(memory_space=pltpu.HBM)
    return pl.pallas_call(
        body, out_shape=comm.out_shape, in_specs=[hbm] * na, out_specs=[hbm] * na, scratch_shapes=comm.scratch,
        name=name)(*comm.arrs)


def _carried_call(body, comm, n_in, n_out, n_scr, when, *, grid, in_specs, out_specs, out_shape, scratch_shapes,
                  operands, name):
    if comm is None:
        return pl.pallas_call(body, grid=grid, in_specs=in_specs, out_specs=out_specs, out_shape=out_shape,
                              scratch_shapes=scratch_shapes, name=name)(*operands), []
    na = comm.na

    def both(*refs):
        a = n_in + na
        b = a + n_out + na
        body(*refs[0:n_in], *refs[a:a + n_out], *refs[b:b + n_scr])
        start, middle, finish = comm.phases(refs[n_in:a], refs[a + n_out:b], refs[b + n_scr:])
        first, mid, last = when()
        pl.when(first)(start)
        pl.when(mid)(middle)
        pl.when(last)(finish)

    hbm = pl.BlockSpec(memory_space=pltpu.HBM)
    res = pl.pallas_call(
        both, grid=grid, in_specs=list(in_specs) + [hbm] * na, out_specs=list(out_specs) + [hbm] * na,
        out_shape=list(out_shape) + comm.out_shape, scratch_shapes=list(scratch_shapes) + comm.scratch,
        name=name)(*operands, *comm.arrs)
    return res[0:n_out], res[n_out:]


def _ffn_w_specs():
    return [
        pl.BlockSpec((None, D, FC), lambda i, j: (j, 0, 0)),
        pl.BlockSpec((None, D, FC), lambda i, j: (j + NJ, 0, 0)),
        pl.BlockSpec((2, WO_ROWS, D), lambda i, j: (j, 0, 0)),
    ]


def _ffn_when(nt):
    def when():
        i, j = pl.program_id(0), pl.program_id(1)
        return ((i == 0) & (j == 0), (i == (3 * nt) // 4) & (j == 0), (i == nt - 1) & (j == NJ - 1))
    return when


def _ffn_fwd(x, nw, w_in, w_out, comm=None):
    t = x.shape[0]
    tm = min(FT, t)
    nt = t // tm

    def body(x_ref, nw_ref, wg_ref, wu_ref, wo3_ref, o_ref, gu_ref, xn_sc, acc_sc):
        j = pl.program_id(1)

        @pl.when(j == 0)
        def _():
            xv = x_ref[...]
            xn_sc[...] = _c(xv * _rstd(xv) * nw_ref[...])
            acc_sc[...] = jnp.zeros_like(acc_sc)

        xn = xn_sc[...]
        g = _dot(xn, wg_ref[...])
        u = _dot(xn, wu_ref[...])
        gu_ref[0] = _c(g)
        gu_ref[1] = _c(u)
        acc_sc[...] += _dot(_c(g * _sigmoid(g) * u), wo3_ref[...].reshape(FC, D))

        @pl.when(j == NJ - 1)
        def _():
            o_ref[...] = x_ref[...] + 0.5 * acc_sc[...]

    (out, gu), landed = _carried_call(
        body, comm, 5, 2, 2, _ffn_when(nt),
        grid=(nt, NJ),
        in_specs=[pl.BlockSpec((tm, D), lambda i, j: (i, 0)), pl.BlockSpec((1, D), lambda i, j: (0, 0))]
        + _ffn_w_specs(),
        out_specs=[pl.BlockSpec((tm, D), lambda i, j: (i, 0)),
                   pl.BlockSpec((None, 2, tm, FC), lambda i, j: (j, 0, i, 0))],
        out_shape=[jax.ShapeDtypeStruct((t, D), F32), jax.ShapeDtypeStruct((NJ, 2, t, FC), _MM)],
        scratch_shapes=[pltpu.VMEM((tm, D), _MM), pltpu.VMEM((tm, D), F32)],
        operands=(x, nw, w_in, w_in, w_out), name="ffn_fwd")
    return out, gu, landed


def _ffn_bwd(x, nw, w_in, w_out, gu, dy, comm=None):
    t = x.shape[0]
    nt = t // TM

    def body(x_ref, nw_ref, wg_ref, wu_ref, wo3_ref, gu_ref, dy_ref, dx_ref, xn_ref, a_ref, dh_ref, dnw_ref,
             r_sc, dyb_sc, acc_sc):
        wo = wo3_ref[...].reshape(FC, D)
        i = pl.program_id(0)
        j = pl.program_id(1)

        @pl.when(j == 0)
        def _():
            xv = x_ref[...]
            r = _rstd(xv)
            r_sc[...] = r
            xn_ref[...] = _c(xv * r * nw_ref[...])
            dyb_sc[...] = _c(0.5 * dy_ref[...])
            acc_sc[...] = jnp.zeros_like(acc_sc)

        g = gu_ref[0].astype(F32)
        u = gu_ref[1].astype(F32)
        s = _sigmoid(g)
        sl = g * s
        a_ref[...] = _c(sl * u)
        da = _dot_nt(dyb_sc[...], wo)
        dg = _c(da * u * (s * (1.0 + g * (1.0 - s))))
        du = _c(da * sl)
        dh_ref[0] = dg
        dh_ref[1] = du
        acc_sc[...] += _dot_nt(dg, wg_ref[...]) + _dot_nt(du, wu_ref[...])

        @pl.when(j == NJ - 1)
        def _():
            r = r_sc[...]
            dx, dnw = _rms_bwd(acc_sc[...], x_ref[...] * r, r, nw_ref[...])
            dx_ref[...] = dy_ref[...] + dx
            _accum(dnw_ref, dnw, i)

    return _carried_call(
        body, comm, 7, 5, 3, _ffn_when(nt),
        grid=(nt, NJ),
        in_specs=[pl.BlockSpec((TM, D), lambda i, j: (i, 0)), pl.BlockSpec((1, D), lambda i, j: (0, 0))]
        + _ffn_w_specs() + [pl.BlockSpec((None, 2, TM, FC), lambda i, j: (j, 0, i, 0)),
                            pl.BlockSpec((TM, D), lambda i, j: (i, 0))],
        out_specs=[
            pl.BlockSpec((TM, D), lambda i, j: (i, 0)),
            pl.BlockSpec((TM, D), lambda i, j: (i, 0)),
            pl.BlockSpec((None, TM, FC), lambda i, j: (j, i, 0)),
            pl.BlockSpec((None, 2, TM, FC), lambda i, j: (j, 0, i, 0)),
            pl.BlockSpec((1, D), lambda i, j: (0, 0)),
        ],
        out_shape=[
            jax.ShapeDtypeStruct((t, D), F32),
            jax.ShapeDtypeStruct((t, D), _MM),
            jax.ShapeDtypeStruct((NJ, t, FC), _MM),
            jax.ShapeDtypeStruct((NJ, 2, t, FC), _MM),
            jax.ShapeDtypeStruct((1, D), F32),
        ],
        scratch_shapes=[pltpu.VMEM((TM, 1), F32), pltpu.VMEM((TM, D), _MM), pltpu.VMEM((TM, D), F32)],
        operands=(x, nw, w_in, w_in, w_out, gu, dy), name="ffn_bwd")


def _mm_tn(a, b, bm, bn, bt, out_dtype, name):
    t, m = a.shape
    n = b.shape[1]
    nt = t // bt

    def body(a_ref, b_ref, o_ref, acc_sc):
        k = pl.program_id(2)
        _accum(acc_sc, _dot_tn(_c(a_ref[...]), _c(b_ref[...])), k)

        @pl.when(k == nt - 1)
        def _():
            o_ref[...] = acc_sc[...].astype(out_dtype)

    return pl.pallas_call(
        body,
        grid=(m // bm, n // bn, nt),
        in_specs=[pl.BlockSpec((bt, bm), lambda i, j, k: (k, i)), pl.BlockSpec((bt, bn), lambda i, j, k: (k, j))],
        out_specs=pl.BlockSpec((bm, bn), lambda i, j, k: (i, j)),
        out_shape=jax.ShapeDtypeStruct((m, n), out_dtype),
        scratch_shapes=[pltpu.VMEM((bm, bn), F32)],
        name=name,
    )(a, b)


def _mm_tn_win(xn, dh, comm=None):
    t = xn.shape[0]
    bt = min(BT, t)
    nt = t // bt

    def body(a_ref, b_ref, o_ref, acc_sc):
        k = pl.program_id(2)
        _accum(acc_sc, _dot_tn(b_ref[...], a_ref[...]), k)

        @pl.when(k == nt - 1)
        def _():
            o_ref[...] = _c(acc_sc[...])

    def when():
        h, j, k = pl.program_id(0), pl.program_id(1), pl.program_id(2)
        start = (h == 0) & (j == 0) & (k == 0)
        return start, start, (h == 1) & (j == NJ - 1) & (k == nt - 1)

    (out,), landed = _carried_call(
        body, comm, 2, 1, 1, when,
        grid=(2, NJ, nt),
        in_specs=[pl.BlockSpec((bt, D), lambda h, j, k: (k, 0)),
                  pl.BlockSpec((None, None, bt, FC), lambda h, j, k: (j, h, k, 0))],
        out_specs=[pl.BlockSpec((None, FC, D), lambda h, j, k: (h * NJ + j, 0, 0))],
        out_shape=[jax.ShapeDtypeStruct((N_DEV, FC, D), _MM)],
        scratch_shapes=[pltpu.VMEM((FC, D), F32)],
        operands=(xn, dh), name="mm_tn_win")
    return out, landed


def _mm_tn_wout(act, dy):
    t = dy.shape[0]
    bt = min(BT, t)
    nt = t // bt

    def body(a_ref, b_ref, o_ref, acc_sc):
        k = pl.program_id(1)
        _accum(acc_sc, _dot_tn(a_ref[...], _c(b_ref[...])), k)

        @pl.when(k == nt - 1)
        def _():
            o_ref[...] = _c((0.5 * acc_sc[...]).reshape(2, WO_ROWS, D))

    return pl.pallas_call(
        body,
        grid=(NJ, nt),
        in_specs=[pl.BlockSpec((None, bt, FC), lambda j, k: (j, k, 0)), pl.BlockSpec((bt, D), lambda j, k: (k, 0))],
        out_specs=pl.BlockSpec((2, WO_ROWS, D), lambda j, k: (j, 0, 0)),
        out_shape=jax.ShapeDtypeStruct((N_DEV, WO_ROWS, D), _MM),
        scratch_shapes=[pltpu.VMEM((FC, D), F32)],
        name="mm_tn_wout",
    )(act, dy)


def _loss_head(x, nw, tgt):
    t = x.shape[0]

    def body(x_ref, nw_ref, t_ref, loss_ref, dx_ref, dnw_ref):
        i = pl.program_id(0)
        xv = x_ref[...]
        r = _rstd(xv)
        xh = xv * r
        e = xh * nw_ref[...] - t_ref[...]
        part = 0.5 * jnp.sum(jnp.mean(e * e, axis=-1, keepdims=True), axis=0, keepdims=True)
        _accum(loss_ref, jnp.broadcast_to(part, (1, 128)), i)
        dx, dnw = _rms_bwd(e * (1.0 / D), xh, r, nw_ref[...])
        dx_ref[...] = dx
        _accum(dnw_ref, dnw, i)

    return pl.pallas_call(
        body,
        grid=(t // TM,),
        in_specs=[pl.BlockSpec((TM, D), lambda i: (i, 0)), pl.BlockSpec((1, D), lambda i: (0, 0)),
                  pl.BlockSpec((TM, D), lambda i: (i, 0))],
        out_specs=[pl.BlockSpec((1, 128), lambda i: (0, 0)), pl.BlockSpec((TM, D), lambda i: (i, 0)),
                   pl.BlockSpec((1, D), lambda i: (0, 0))],
        out_shape=[jax.ShapeDtypeStruct((1, 128), F32), jax.ShapeDtypeStruct((t, D), F32),
                   jax.ShapeDtypeStruct((1, D), F32)],
        name="loss_head",
    )(x, nw, tgt)


PW_F = QKV + 5 * HW
PW_B = QKV + 3 * HW + 128


def _ab_proj(x1, nw, wab):
    t = x1.shape[0]

    def body(x_ref, nw_ref, w_ref, h_ref, qkv_ref, z_ref, su_ref, sv_ref, b_ref, a_ref):
        xv = x_ref[...]
        h = _c(xv * _rstd(xv) * nw_ref[...])
        h_ref[...] = h
        p = _dot(h, w_ref[...])
        qkv_ref[...] = p[:, 0:QKV]
        o = QKV
        for ref in (z_ref, su_ref, sv_ref, b_ref, a_ref):
            ref[...] = p[:, o:o + HW]
            o += HW

    row = lambda w: pl.BlockSpec((TM, w), lambda i: (i, 0))
    return pl.pallas_call(
        body,
        grid=(t // TM,),
        in_specs=[row(D), pl.BlockSpec((1, D), lambda i: (0, 0)), pl.BlockSpec((D, PW_F), lambda i: (0, 0))],
        out_specs=[row(D), row(QKV)] + [row(HW)] * 5,
        out_shape=[jax.ShapeDtypeStruct((t, D), _MM), jax.ShapeDtypeStruct((t, QKV), F32)]
        + [jax.ShapeDtypeStruct((t, HW), F32)] * 5,
        name="ab_proj",
    )(x1, nw, wab)


def _conv_rows(x, halo, cw):
    xe = jnp.concatenate([halo, x], axis=0)
    shifted = []
    c = None
    for k in range(4):
        s = 3 - k
        xs = (xe if s == 0 else pltpu.roll(xe, s, 0))[8:, :]
        shifted.append(xs)
        term = cw[k:k + 1, :] * xs
        c = term if c is None else c + term
    return c, shifted


def _head_rsq(a):
    parts = []
    for h in range(HEADS):
        ah = a[:, h * HD:(h + 1) * HD]
        r = lax.rsqrt(jnp.sum(ah * ah, axis=-1, keepdims=True) + EPS)
        parts.append(jnp.broadcast_to(r, ah.shape))
    return jnp.concatenate(parts, axis=-1)


def _head_sum(a):
    parts = []
    for h in range(HEADS):
        ah = a[:, h * HD:(h + 1) * HD]
        parts.append(jnp.broadcast_to(jnp.sum(ah, axis=-1, keepdims=True), ah.shape))
    return jnp.concatenate(parts, axis=-1)


def _softplus(x):
    return jnp.maximum(x, 0.0) + jnp.log1p(jnp.exp(-jnp.abs(x)))


def _halo_prev_spec(width, rows):
    per = TM // rows
    return pl.BlockSpec((rows, width), lambda i: (jnp.maximum(i * per - 1, 0), 0))


def _halo_next_spec(width, rows, t):
    per = TM // rows
    last = t // rows - 1
    return pl.BlockSpec((rows, width), lambda i: (jnp.minimum((i + 1) * per, last), 0))


def _dn_pre(qkv, b_rep, a_rep, cw, alog, dtb):
    t = qkv.shape[0]
    qscale = HD ** -0.5

    def body(x_ref, halo_ref, b_ref, a_ref, cw_ref, alog_ref, dt_ref, q_ref, k_ref, v_ref, beta_ref, g_ref):
        i = pl.program_id(0)
        halo = jnp.where(i == 0, 0.0, halo_ref[...])
        c, _ = _conv_rows(x_ref[...], halo, cw_ref[...])
        sc = c * _sigmoid(c)
        q = sc[:, 0:HW]
        k = sc[:, HW:2 * HW]
        q_ref[...] = q * _head_rsq(q) * qscale
        k_ref[...] = k * _head_rsq(k)
        v_ref[...] = sc[:, 2 * HW:]
        beta_ref[...] = _sigmoid(b_ref[...])
        g_ref[...] = -jnp.exp(alog_ref[...]) * _softplus(a_ref[...] + dt_ref[...])

    row = lambda w: pl.BlockSpec((TM, w), lambda i: (i, 0))
    full = lambda a: pl.BlockSpec(a.shape, lambda i: (0,) * a.ndim)
    return pl.pallas_call(
        body,
        grid=(t // TM,),
        in_specs=[row(QKV), _halo_prev_spec(QKV, 8), row(HW), row(HW), full(cw), full(alog), full(dtb)],
        out_specs=[row(HW)] * 5,
        out_shape=[jax.ShapeDtypeStruct((t, HW), F32)] * 5,
        name="dn_pre",
    )(qkv, qkv, b_rep, a_rep, cw, alog, dtb)


def _unit_lower_inv(los, eye):
    ps = [eye - lo for lo in los]
    lps = list(los)
    for _ in range(5):
        lps = [_dot(_c(lp), _c(lp)) for lp in lps]
        ps = [p + _dot(_c(p), _c(lp)) for p, lp in zip(ps, lps)]
    rs = [eye - (p + _dot3(lo, p)) for lo, p in zip(los, ps)]
    return [p + _dot(_c(p), _c(r)) for p, r in zip(ps, rs)]


def _dn_masks():
    ri = lax.broadcasted_iota(jnp.int32, (DNC, DNC), 0)
    ci = lax.broadcasted_iota(jnp.int32, (DNC, DNC), 1)
    return dict(strict=ri > ci, causal=ri >= ci, eye=(ri == ci).astype(F32),
                ltri=_c((ri >= ci).astype(F32)), upper=_c((ri <= ci).astype(F32)))


def _dn_decay(gr, mk):
    rhs = jnp.concatenate([gr, jnp.where(mk["strict"], gr[:, 0:DNC], 0.0)], axis=1)
    cs = _mask_dot(mk["ltri"], rhs)
    gc = cs[:, 0:HD]
    dm = jnp.where(mk["causal"], jnp.exp(cs[:, HD:HD + DNC]), 0.0)
    gl = jnp.sum(gr, axis=0, keepdims=True)
    return dm, jnp.exp(gc), jnp.exp(gl - gc), gl


def _dn_when(n):
    def when():
        i = pl.program_id(0)
        return (i == 0, i == n // 2, i == n - 1)
    return when


def _dn_fwd(q, k, v, beta, g, comm=None):
    t = q.shape[0]
    rows = DN_STEP * DNC
    n = t // rows

    def body(q_ref, k_ref, v_ref, b_ref, g_ref, o_ref, sall_ref, aall_ref, u_ref, w_ref, s_sc):
        i = pl.program_id(0)

        @pl.when(i == 0)
        def _():
            s_sc[...] = jnp.zeros_like(s_sc)

        mk = _dn_masks()
        idx = [(cc, h) for cc in range(DN_STEP) for h in range(HEADS)]
        at = lambda cc, h: (slice(cc * DNC, (cc + 1) * DNC), slice(h * HD, (h + 1) * HD))
        qs = [q_ref[at(*i)] for i in idx]
        ks = [k_ref[at(*i)] for i in idx]
        bs = [b_ref[at(*i)] for i in idx]
        dec = [_dn_decay(g_ref[at(*i)], mk) for i in idx]
        kbs = [k_ * b_ for k_, b_ in zip(ks, bs)]
        los = [jnp.where(mk["strict"], _dot_nt(_c(kb), _c(k_)) * d[0], 0.0) for kb, k_, d in zip(kbs, ks, dec)]
        inv = _unit_lower_inv(los, mk["eye"])
        uws = [_dot3(a, jnp.concatenate([v_ref[at(*i)] * b_, kb * d[1]], axis=1))
               for a, i, b_, kb, d in zip(inv, idx, bs, kbs, dec)]
        attn = [_c(_dot_nt(_c(q_), _c(k_)) * d[0]) for q_, k_, d in zip(qs, ks, dec)]
        for n_, (cc, h) in enumerate(idx):
            aall_ref[cc, h] = inv[n_]
            u_ref[at(cc, h)] = uws[n_][:, 0:HD]
            w_ref[at(cc, h)] = uws[n_][:, HD:]
        ss = [s_sc[h] for h in range(HEADS)]
        for cc in range(DN_STEP):
            base = cc * HEADS
            for h in range(HEADS):
                sall_ref[cc, h] = ss[h]
            ws = [_dot(_c(jnp.concatenate([uws[base + h][:, HD:], qs[base + h] * dec[base + h][1]], axis=0)),
                       _c(ss[h])) for h in range(HEADS)]
            vn = [_c(uws[base + h][:, 0:HD] - ws[h][0:DNC]) for h in range(HEADS)]
            for h in range(HEADS):
                o_ref[at(cc, h)] = ws[h][DNC:] + _dot(attn[base + h], vn[h])
            ss = [ss[h] * jnp.exp(dec[base + h][3]) + _dot_tn(_c(ks[base + h] * dec[base + h][2]), vn[h])
                  for h in range(HEADS)]
        for h in range(HEADS):
            s_sc[h] = ss[h]

    row = pl.BlockSpec((rows, HW), lambda i: (i, 0))
    return _carried_call(
        body, comm, 5, 5, 1, _dn_when(n),
        grid=(n,),
        in_specs=[row] * 5,
        out_specs=[row, pl.BlockSpec((DN_STEP, HEADS, HD, HD), lambda i: (i, 0, 0, 0)),
                   pl.BlockSpec((DN_STEP, HEADS, DNC, DNC), lambda i: (i, 0, 0, 0)), row, row],
        out_shape=[jax.ShapeDtypeStruct((t, HW), F32), jax.ShapeDtypeStruct((t // DNC, HEADS, HD, HD), F32),
                   jax.ShapeDtypeStruct((t // DNC, HEADS, DNC, DNC), F32), jax.ShapeDtypeStruct((t, HW), F32),
                   jax.ShapeDtypeStruct((t, HW), F32)],
        scratch_shapes=[pltpu.VMEM((HEADS, HD, HD), F32)],
        operands=(q, k, v, beta, g), name="dn_fwd")


def _dn_bwd(q, k, v, beta, g, sall, aall, u, w, do, comm=None):
    t = q.shape[0]
    rows = DN_STEP * DNC
    n = t // rows

    def body(q_ref, k_ref, v_ref, b_ref, g_ref, sall_ref, aall_ref, u_ref, w_ref, do_ref,
             dq_ref, dk_ref, dv_ref, db_ref, dg_ref, ds_sc):
        i = pl.program_id(0)

        @pl.when(i == 0)
        def _():
            ds_sc[...] = jnp.zeros_like(ds_sc)

        mk = _dn_masks()
        strict = mk["strict"]
        hs = range(HEADS)
        at = lambda cc, h: (slice(cc * DNC, (cc + 1) * DNC), slice(h * HD, (h + 1) * HD))
        rowsum = lambda a: jnp.sum(a, axis=-1, keepdims=True)
        dsn = [ds_sc[h] for h in hs]
        for cc in reversed(range(DN_STEP)):
            q = [q_ref[at(cc, h)] for h in hs]
            k = [k_ref[at(cc, h)] for h in hs]
            b = [b_ref[at(cc, h)] for h in hs]
            u = [u_ref[at(cc, h)] for h in hs]
            w = [w_ref[at(cc, h)] for h in hs]
            do = [do_ref[at(cc, h)] for h in hs]
            s = [sall_ref[cc, h] for h in hs]
            dec = [_dn_decay(g_ref[at(cc, h)], mk) for h in hs]
            dm, e, f = [d[0] for d in dec], [d[1] for d in dec], [d[2] for d in dec]
            egl = [jnp.exp(d[3]) for d in dec]
            kb = [k[h] * b[h] for h in hs]
            kc = [_c(k[h]) for h in hs]
            sb = [_c(s[h]) for h in hs]
            dob = [_c(do[h]) for h in hs]
            m = [_dot_nt(_c(kb[h]), kc[h]) for h in hs]
            p = [_dot_nt(_c(q[h]), kc[h]) for h in hs]
            vnb = [_c(u[h] - _dot(_c(w[h]), sb[h])) for h in hs]
            dsb = [_c(dsn[h]) for h in hs]
            dvn = [_dot_tn(_c(p[h] * dm[h]), dob[h]) + _dot(_c(k[h] * f[h]), dsb[h]) for h in hs]
            dov = [_c(jnp.concatenate([do[h], dvn[h]], axis=0)) for h in hs]
            t1 = [_dot_nt(dov[h], sb[h]) for h in hs]
            dattn = [_dot_nt(dob[h], vnb[h]) for h in hs]
            dkt = [_dot_nt(vnb[h], dsb[h]) for h in hs]
            dgl = [jnp.sum(jnp.sum(dsn[h] * s[h], axis=1, keepdims=True), axis=0, keepdims=True) * egl[h][:, 0:1]
                   for h in hs]
            dsn = [dsn[h] * egl[h] + _dot_tn(_c(jnp.concatenate([q[h] * e[h], -w[h]], axis=0)), dov[h]) for h in hs]
            dqd = [t1[h][0:DNC] for h in hs]
            dw = [-t1[h][DNC:] for h in hs]
            ab = [_dot3(aall_ref[cc, h], jnp.concatenate([dvn[h], dw[h]], axis=1), TN) for h in hs]
            dlo = [jnp.where(strict, -_dot3(ab[h], jnp.concatenate([u[h], w[h]], axis=1), NT), 0.0) for h in hs]
            dpm = [_c(jnp.concatenate([dattn[h] * dm[h], dlo[h] * dm[h]], axis=0)) for h in hs]
            t2 = [_dot(dpm[h], kc[h]) for h in hs]
            t4 = [_dot_tn(dpm[h], _c(jnp.concatenate([q[h], kb[h]], axis=0))) for h in hs]
            dff = [rowsum(dkt[h] * k[h]) * f[h][:, 0:1] for h in hs]
            de = [rowsum(dqd[h] * q[h]) + rowsum(ab[h][:, HD:] * kb[h]) for h in hs]
            dd = [(dattn[h] * p[h] + dlo[h] * m[h]) * dm[h] for h in hs]
            t3 = [_mask_dot(mk["upper"], jnp.concatenate(
                [jnp.broadcast_to(de[h] * e[h][:, 0:1] - dff[h], (DNC, HD)), dd[h]], axis=1)) for h in hs]
            for h in hs:
                dvb, dkbe = ab[h][:, 0:HD], ab[h][:, HD:]
                dkb = t2[h][DNC:] + dkbe * e[h]
                dbeta = rowsum(dkb * k[h]) + rowsum(dvb * v_ref[at(cc, h)])
                dg = (rowsum(jnp.where(strict, t3[h][:, HD:HD + DNC], 0.0)) + t3[h][:, 0:1]
                      + dgl[h] + jnp.sum(dff[h], axis=0, keepdims=True))
                dq_ref[at(cc, h)] = dqd[h] * e[h] + t2[h][0:DNC]
                dk_ref[at(cc, h)] = t4[h] + dkt[h] * f[h] + dkb * b[h]
                dv_ref[at(cc, h)] = dvb * b[h]
                db_ref[at(cc, h)] = jnp.broadcast_to(dbeta, (DNC, HD))
                dg_ref[at(cc, h)] = jnp.broadcast_to(dg, (DNC, HD))
        for h in hs:
            ds_sc[h] = dsn[h]

    row = pl.BlockSpec((rows, HW), lambda i: (n - 1 - i, 0))
    return _carried_call(
        body, comm, 10, 5, 1, _dn_when(n),
        grid=(n,),
        in_specs=[row] * 5 + [pl.BlockSpec((DN_STEP, HEADS, HD, HD), lambda i: (n - 1 - i, 0, 0, 0)),
                              pl.BlockSpec((DN_STEP, HEADS, DNC, DNC), lambda i: (n - 1 - i, 0, 0, 0)), row, row, row],
        out_specs=[row] * 5,
        out_shape=[jax.ShapeDtypeStruct((t, HW), F32)] * 5,
        scratch_shapes=[pltpu.VMEM((HEADS, HD, HD), F32)],
        operands=(q, k, v, beta, g, sall, aall, u, w, do), name="dn_bwd")


def _group_norm(a, nw):
    rs = []
    for h in range(HEADS):
        ah = a[:, h * HD:(h + 1) * HD]
        rs.append(jnp.broadcast_to(_rstd(ah), ah.shape))
    r = jnp.concatenate(rs, axis=-1)
    xh = a * r
    return xh * nw, xh, r


def _group_norm_bwd(dy, xh, r, nw):
    dxh = dy * nw
    return r * (dxh - xh * (_head_sum(dxh * xh) * (1.0 / HD)))


def _sg_mix(wt_ref, svn_b, nchunk):
    rows = []
    for cidx in range(nchunk):
        cols = []
        for g in range(HEADS):
            blk = svn_b[cidx * SGC:(cidx + 1) * SGC, g * HD:(g + 1) * HD]
            cols.append(_dot(wt_ref[g], blk))
        rows.append(jnp.concatenate(cols, axis=-1))
    return jnp.concatenate(rows, axis=0)


def _ab_out(x1, o, z, su, sv, dnw, sgnw, wtril, sgb, wout):
    t = x1.shape[0]
    nchunk = TM // SGC

    def body(x_ref, o_ref, z_ref, su_ref, sv_ref, dnw_ref, sgnw_ref, wt_ref, sgb_ref, wo_ref, x2_ref, cat_ref):
        on, _, _ = _group_norm(o_ref[...], dnw_ref[...])
        zv = z_ref[...]
        cat_ref[:, 0:HW] = _c(on * (zv * _sigmoid(zv)))
        svn, _, _ = _group_norm(_gelu(sv_ref[...]), sgnw_ref[...])
        mixed = _sg_mix(wt_ref, _c(svn), nchunk) + jnp.tile(sgb_ref[...], (nchunk, 1))
        cat_ref[:, HW:] = _c(_gelu(su_ref[...]) * mixed)
        x2_ref[...] = x_ref[...] + _dot(cat_ref[...], wo_ref[...])

    row = lambda w: pl.BlockSpec((TM, w), lambda i: (i, 0))
    full = lambda a: pl.BlockSpec(a.shape, lambda i: (0,) * a.ndim)
    return pl.pallas_call(
        body,
        grid=(t // TM,),
        in_specs=[row(D)] + [row(HW)] * 4 + [full(dnw), full(sgnw), full(wtril), full(sgb), full(wout)],
        out_specs=[row(D), row(D)],
        out_shape=[jax.ShapeDtypeStruct((t, D), F32), jax.ShapeDtypeStruct((t, D), _MM)],
        name="ab_out",
    )(x1, o, z, su, sv, dnw, sgnw, wtril, sgb, wout)


def _ab_out_bwd(dx2, o, z, su, sv, dnw, sgnw, wtril, wtril_t, sgb, wout):
    t = dx2.shape[0]
    nchunk = TM // SGC

    def body(dx_ref, o_ref, z_ref, su_ref, sv_ref, dnw_ref, sgnw_ref, wt_ref, wtt_ref, sgb_ref, wo_ref,
             do_ref, dz_ref, dsu_ref, dsv_ref, ddnw_ref, dsgnw_ref, dsgw_ref, dsgb_ref):
        i = pl.program_id(0)
        dcat = _dot_nt(_c(dx_ref[...]), wo_ref[...])
        doa = dcat[:, 0:HW]
        dob = dcat[:, HW:]
        on, oh, ro = _group_norm(o_ref[...], dnw_ref[...])
        zv = z_ref[...]
        sz = _sigmoid(zv)
        dz_ref[...] = _c(doa * on * (sz * (1.0 + zv * (1.0 - sz))))
        don = doa * (zv * sz)
        do_ref[...] = _group_norm_bwd(don, oh, ro, dnw_ref[...])
        dd = jnp.sum(don * oh, axis=0, keepdims=True)
        _accum(ddnw_ref, dd[:, 0:HD] + dd[:, HD:2 * HD] + dd[:, 2 * HD:3 * HD] + dd[:, 3 * HD:], i)
        suv = su_ref[...]
        svv = sv_ref[...]
        svg = _gelu(svv)
        svn, sh, rs = _group_norm(svg, sgnw_ref[...])
        svn_b = _c(svn)
        mixed = _sg_mix(wt_ref, svn_b, nchunk) + jnp.tile(sgb_ref[...], (nchunk, 1))
        dsu_ref[...] = _c(dob * mixed * _gelu_grad(suv))
        dmixed = dob * _gelu(suv)
        dmb = _c(dmixed)
        tri = lax.broadcasted_iota(jnp.int32, (SGC, SGC), 0) >= lax.broadcasted_iota(jnp.int32, (SGC, SGC), 1)
        lane = lax.broadcasted_iota(jnp.int32, (SGC, HD), 1)
        rows = []
        dbias = jnp.zeros((SGC, HD), F32)
        for g in range(HEADS):
            gs = slice(g * HD, (g + 1) * HD)
            dwg = jnp.zeros((SGC, SGC), F32)
            col = jnp.zeros((SGC, 1), F32)
            for cidx in range(nchunk):
                cs = slice(cidx * SGC, (cidx + 1) * SGC)
                dwg = dwg + _dot_nt(dmb[cs, gs], svn_b[cs, gs])
                col = col + jnp.sum(dmixed[cs, gs], axis=-1, keepdims=True)
            _accum(dsgw_ref.at[g], jnp.where(tri, dwg, 0.0), i)
            dbias = dbias + jnp.where(lane == g, col, 0.0)
        _accum(dsgb_ref, dbias, i)
        for cidx in range(nchunk):
            cs = slice(cidx * SGC, (cidx + 1) * SGC)
            rows.append(jnp.concatenate(
                [_dot(wtt_ref[g], dmb[cs, g * HD:(g + 1) * HD]) for g in range(HEADS)], axis=-1))
        dsvn = jnp.concatenate(rows, axis=0)
        _accum(dsgnw_ref, jnp.sum(dsvn * sh, axis=0, keepdims=True), i)
        dsv_ref[...] = _c(_group_norm_bwd(dsvn, sh, rs, sgnw_ref[...]) * _gelu_grad(svv))

    row = lambda w: pl.BlockSpec((TM, w), lambda i: (i, 0))
    full = lambda a: pl.BlockSpec(a.shape, lambda i: (0,) * a.ndim)
    const = lambda shape: pl.BlockSpec(shape, lambda i: (0,) * len(shape))
    return pl.pallas_call(
        body,
        grid=(t // TM,),
        in_specs=[row(D)] + [row(HW)] * 4 + [full(dnw), full(sgnw), full(wtril), full(wtril_t), full(sgb), full(wout)],
        out_specs=[row(HW)] * 4 + [const((1, HD)), const((1, HW)), const((HEADS, SGC, SGC)), const((SGC, HD))],
        out_shape=[jax.ShapeDtypeStruct((t, HW), F32)] + [jax.ShapeDtypeStruct((t, HW), _MM)] * 3
        + [jax.ShapeDtypeStruct((1, HD), F32), jax.ShapeDtypeStruct((1, HW), F32),
           jax.ShapeDtypeStruct((HEADS, SGC, SGC), F32), jax.ShapeDtypeStruct((SGC, HD), F32)],
        name="ab_out_bwd",
    )(dx2, o, z, su, sv, dnw, sgnw, wtril, wtril_t, sgb, wout)


def _dn_pre_bwd(qkv, b_rep, a_rep, cw, alog, dtb, dqn, dkn, dv, dbeta, dg):
    t = qkv.shape[0]
    qscale = HD ** -0.5

    def body(x_ref, halo_ref, b_ref, a_ref, cw_ref, alog_ref, dt_ref, dq_ref, dk_ref, dv_ref, dbeta_ref, dg_ref,
             dc_ref, dba_ref, dcw_ref, dalog_ref, ddt_ref):
        i = pl.program_id(0)
        halo = jnp.where(i == 0, 0.0, halo_ref[...])
        c, shifted = _conv_rows(x_ref[...], halo, cw_ref[...])
        s = _sigmoid(c)
        sc = c * s
        q = sc[:, 0:HW]
        k = sc[:, HW:2 * HW]
        rq = _head_rsq(q)
        rk = _head_rsq(k)
        qu = q * rq
        ku = k * rk
        dqn = dq_ref[...]
        dkn = dk_ref[...]
        dq = qscale * rq * (dqn - qu * _head_sum(dqn * qu))
        dk = rk * (dkn - ku * _head_sum(dkn * ku))
        dsc = jnp.concatenate([dq, dk, dv_ref[...]], axis=-1)
        dc = dsc * (s * (1.0 + c * (1.0 - s)))
        dc_ref[...] = dc
        for kk in range(4):
            _accum(dcw_ref.at[kk], jnp.sum(dc * shifted[kk], axis=0, keepdims=True), i)
        beta = _sigmoid(b_ref[...])
        dbp = dbeta_ref[...] * beta * (1.0 - beta)
        nea = -jnp.exp(alog_ref[...])
        spin = a_ref[...] + dt_ref[...]
        dgv = dg_ref[...]
        dap = dgv * nea * _sigmoid(spin)
        _accum(dalog_ref, jnp.sum(dgv * nea * _softplus(spin), axis=0, keepdims=True), i)
        _accum(ddt_ref, jnp.sum(dap, axis=0, keepdims=True), i)
        lane = lax.broadcasted_iota(jnp.int32, (TM, HD), 1)
        dba = jnp.zeros((TM, HD), F32)
        for h in range(HEADS):
            dba = dba + jnp.where(lane == h, dbp[:, h * HD:(h + 1) * HD], 0.0)
            dba = dba + jnp.where(lane == HEADS + h, dap[:, h * HD:(h + 1) * HD], 0.0)
        dba_ref[...] = _c(dba)

    row = lambda w: pl.BlockSpec((TM, w), lambda i: (i, 0))
    full = lambda a: pl.BlockSpec(a.shape, lambda i: (0,) * a.ndim)
    const = lambda shape: pl.BlockSpec(shape, lambda i: (0,) * len(shape))
    return pl.pallas_call(
        body,
        grid=(t // TM,),
        in_specs=[row(QKV), _halo_prev_spec(QKV, 8), row(HW), row(HW), full(cw), full(alog), full(dtb)] + [row(HW)] * 5,
        out_specs=[row(QKV), row(HD), const((4, 1, QKV)), const((1, HW)), const((1, HW))],
        out_shape=[jax.ShapeDtypeStruct((t, QKV), F32), jax.ShapeDtypeStruct((t, HD), _MM),
                   jax.ShapeDtypeStruct((4, 1, QKV), F32), jax.ShapeDtypeStruct((1, HW), F32),
                   jax.ShapeDtypeStruct((1, HW), F32)],
        name="dn_pre_bwd",
    )(qkv, qkv, b_rep, a_rep, cw, alog, dtb, dqn, dkn, dv, dbeta, dg)


def _conv_bwd(dc, cw):
    t = dc.shape[0]
    nt = t // TM

    def body(dc_ref, halo_ref, cw_ref, dx_ref):
        i = pl.program_id(0)
        halo = jnp.where(i == nt - 1, 0.0, halo_ref[...])
        de = jnp.concatenate([dc_ref[...], halo], axis=0)
        cwv = cw_ref[...]
        acc = None
        for k in range(4):
            s = 3 - k
            ds = (de if s == 0 else pltpu.roll(de, TM + 8 - s, 0))[0:TM, :]
            term = cwv[k:k + 1, :] * ds
            acc = term if acc is None else acc + term
        dx_ref[...] = _c(acc)

    return pl.pallas_call(
        body,
        grid=(nt,),
        in_specs=[pl.BlockSpec((TM, QKV), lambda i: (i, 0)), _halo_next_spec(QKV, 8, t),
                  pl.BlockSpec(cw.shape, lambda i: (0, 0))],
        out_specs=pl.BlockSpec((TM, QKV), lambda i: (i, 0)),
        out_shape=jax.ShapeDtypeStruct((t, QKV), _MM),
        name="conv_bwd",
    )(dc, dc, cw)


def _ab_proj_bwd(x1, nw, dqkv, dz, dsu, dsv, dba, wab_b, dres):
    t = x1.shape[0]

    def body(x_ref, nw_ref, dqkv_ref, dz_ref, dsu_ref, dsv_ref, dba_ref, w_ref, dres_ref, dx_ref, dcat_ref, dnw_ref):
        i = pl.program_id(0)
        dcat_ref[:, 0:QKV] = dqkv_ref[...]
        o = QKV
        for ref in (dz_ref, dsu_ref, dsv_ref):
            dcat_ref[:, o:o + HW] = ref[...]
            o += HW
        dcat_ref[:, o:o + 128] = dba_ref[...]
        dh = _dot_nt(dcat_ref[...], w_ref[...])
        xv = x_ref[...]
        r = _rstd(xv)
        dx, dnw = _rms_bwd(dh, xv * r, r, nw_ref[...])
        dx_ref[...] = dres_ref[...] + dx
        _accum(dnw_ref, dnw, i)

    row = lambda w: pl.BlockSpec((TM, w), lambda i: (i, 0))
    return pl.pallas_call(
        body,
        grid=(t // TM,),
        in_specs=[row(D), pl.BlockSpec((1, D), lambda i: (0, 0)), row(QKV), row(HW), row(HW), row(HW), row(128),
                  pl.BlockSpec((D, PW_B), lambda i: (0, 0)), row(D)],
        out_specs=[row(D), row(PW_B), pl.BlockSpec((1, D), lambda i: (0, 0))],
        out_shape=[jax.ShapeDtypeStruct((t, D), F32), jax.ShapeDtypeStruct((t, PW_B), _MM),
                   jax.ShapeDtypeStruct((1, D), F32)],
        name="ab_proj_bwd",
    )(x1, nw, dqkv, dz, dsu, dsv, dba, wab_b, dres)


def _pool_counts(i):
    pos = (lax.broadcasted_iota(jnp.int32, (TM + HALO, 1), 0) + i * TM + 1).astype(F32)
    return [1.0 / jnp.minimum(pos, float(w)) for w in POOL_WINDOWS]


def _window_sum(ext, win, back):
    r = ext.shape[0]
    s = ext
    step = 1
    while step < win:
        s = s + pltpu.roll(s, step if back else r - step, 0)
        step *= 2
    return s


def _pooled(h_ext, invc, g):
    gs = slice(g * PG, (g + 1) * PG)
    he = h_ext[:, gs]
    ws = _window_sum(he, POOL_WINDOWS[g], True)[HALO:, :]
    return ws * invc[g][0:TM, :] - he[HALO:, :]


def _pool_fwd(x1, nw, pw, scale):
    t = x1.shape[0]

    def body(x_ref, halo_ref, nw_ref, pw_ref, sc_ref, x2_ref):
        i = pl.program_id(0)
        xv = x_ref[...]
        hv = halo_ref[...]
        nwv = nw_ref[...]
        h_ext = jnp.concatenate([jnp.where(i == 0, 0.0, hv * _rstd(hv) * nwv), xv * _rstd(xv) * nwv], axis=0)
        invc = _pool_counts(i)
        outs = [_dot(_c(_pooled(h_ext, invc, g)), pw_ref[g]) for g in range(4)]
        x2_ref[...] = xv + jnp.concatenate(outs, axis=-1) * sc_ref[...]

    return pl.pallas_call(
        body,
        grid=(t // TM,),
        in_specs=[pl.BlockSpec((TM, D), lambda i: (i, 0)), _halo_prev_spec(D, HALO),
                  pl.BlockSpec((1, D), lambda i: (0, 0)), pl.BlockSpec((4, PG, PG), lambda i: (0, 0, 0)),
                  pl.BlockSpec((1, D), lambda i: (0, 0))],
        out_specs=pl.BlockSpec((TM, D), lambda i: (i, 0)),
        out_shape=jax.ShapeDtypeStruct((t, D), F32),
        name="pool_fwd",
    )(x1, x1, nw, pw, scale)


def _pool_bwd(x1, nw, pw, scale, dx2):
    t = x1.shape[0]
    nt = t // TM

    def body(x_ref, halo_ref, nw_ref, pw_ref, sc_ref, dx2_ref, dnext_ref, dx_ref, dnw_ref, dpw_ref, dsc_ref):
        i = pl.program_id(0)
        xv = x_ref[...]
        hv = halo_ref[...]
        nwv = nw_ref[...]
        r = _rstd(xv)
        xh = xv * r
        h_ext = jnp.concatenate([jnp.where(i == 0, 0.0, hv * _rstd(hv) * nwv), xh * nwv], axis=0)
        invc = _pool_counts(i)
        dyv = dx2_ref[...]
        dout_ext = jnp.concatenate([dyv, jnp.where(i == nt - 1, 0.0, dnext_ref[...])], axis=0) * sc_ref[...]
        dh_parts = []
        dsc_parts = []
        for g in range(4):
            gs = slice(g * PG, (g + 1) * PG)
            pooled_b = _c(_pooled(h_ext, invc, g))
            dout_b = _c(dout_ext[:, gs])
            dsc_parts.append(jnp.sum(dyv[:, gs] * _dot(pooled_b, pw_ref[g]), axis=0, keepdims=True))
            _accum(dpw_ref.at[g], _dot_tn(pooled_b, dout_b[0:TM, :]), i)
            dpool_ext = _dot_nt(dout_b, pw_ref[g])
            lead = _window_sum(dpool_ext * invc[g], POOL_WINDOWS[g], False)[0:TM, :]
            dh_parts.append(lead - dpool_ext[0:TM, :])
        _accum(dsc_ref, jnp.concatenate(dsc_parts, axis=-1), i)
        dx, dnw = _rms_bwd(jnp.concatenate(dh_parts, axis=-1), xh, r, nwv)
        dx_ref[...] = dyv + dx
        _accum(dnw_ref, dnw, i)

    vec = pl.BlockSpec((1, D), lambda i: (0, 0))
    return pl.pallas_call(
        body,
        grid=(nt,),
        in_specs=[pl.BlockSpec((TM, D), lambda i: (i, 0)), _halo_prev_spec(D, HALO), vec,
                  pl.BlockSpec((4, PG, PG), lambda i: (0, 0, 0)), vec,
                  pl.BlockSpec((TM, D), lambda i: (i, 0)), _halo_next_spec(D, HALO, t)],
        out_specs=[pl.BlockSpec((TM, D), lambda i: (i, 0)), vec, pl.BlockSpec((4, PG, PG), lambda i: (0, 0, 0)), vec],
        out_shape=[jax.ShapeDtypeStruct((t, D), F32), jax.ShapeDtypeStruct((1, D), F32),
                   jax.ShapeDtypeStruct((4, PG, PG), F32), jax.ShapeDtypeStruct((1, D), F32)],
        name="pool_bwd",
    )(x1, x1, nw, pw, scale, dx2, dx2)


def _adamw(lands, w, m, v, rb, name):
    nl, nr = w.shape[0], w.shape[1]
    rest = w.shape[2:]
    ns = lands[0].shape[0]
    zeros = (0,) * len(rest)

    def body(*refs):
        l_refs = refs[0:nl]
        w_ref, m_ref, v_ref, g_ref, d_ref, m2_ref, v2_ref = refs[nl:]
        for l in range(nl):
            g = l_refs[l][0].astype(F32)
            for s in range(1, ns):
                g = g + l_refs[l][s].astype(F32)
            m2 = ADAM_B1 * m_ref[l] + (1.0 - ADAM_B1) * g
            v2 = ADAM_B2 * v_ref[l] + (1.0 - ADAM_B2) * (g * g)
            m_hat = m2 / (1.0 - ADAM_B1 ** ADAM_STEP)
            v_hat = v2 / (1.0 - ADAM_B2 ** ADAM_STEP)
            g_ref[l] = g
            d_ref[l] = -ADAM_LR * (m_hat / (jnp.sqrt(v_hat) + ADAM_EPS) + ADAM_WD * w_ref[l])
            m2_ref[l] = m2
            v2_ref[l] = v2

    lspec = pl.BlockSpec((ns, rb) + rest, lambda r: (0, r) + zeros)
    wspec = pl.BlockSpec((nl, rb) + rest, lambda r: (0, r) + zeros)
    return pl.pallas_call(
        body,
        grid=(nr // rb,),
        in_specs=[lspec] * nl + [wspec] * 3,
        out_specs=[wspec] * 4,
        out_shape=[jax.ShapeDtypeStruct(w.shape, F32)] * 4,
        name=name,
    )(*lands, w, m, v)


WEIGHT_ORDER = ("ffn_norm1", "ffn1_w_in", "ffn1_w_out", "mix_norm", "ffn_norm2", "ffn2_w_in", "ffn2_w_out", "ab_w_in",
                "dn_conv_w", "dn_a_log", "dn_dt_bias", "dn_out_norm", "sg_norm", "sg_w", "sg_b", "ab_w_out", "pool_w",
                "pool_scale", "final_norm")
R_SMALL = 88
SMALL_ROWS = (
    ("ffn_norm1", (2, D), 2), ("mix_norm", (2, D), 2), ("ffn_norm2", (2, D), 2), ("final_norm", (D,), 1),
    ("sg_w", (1, 4, SGC, SGC), 64), ("sg_norm", (1, 4, HD), 1), ("sg_b", (1, 4, SGC), 1), ("dn_out_norm", (1, HD), 1),
    ("dn_a_log", (1, 4), 1), ("dn_dt_bias", (1, 4), 1), ("pool_scale", (1, D), 1), ("dn_conv_w", (1, 4, QKV), 8),
)
SMALL_SHARDED = ("pool_scale", "dn_conv_w")


def _rows_of(a, rows):
    if a.shape[-1] == QKV:
        return jnp.pad(a.reshape(4, QKV), ((0, 0), (0, 2 * ROW - QKV))).reshape(8, ROW)
    n = _numel(a.shape)
    if n % ROW == 0:
        return a.reshape(n // ROW, ROW)
    return jnp.pad(a.reshape(1, n), ((0, 0), (0, ROW - n)))


def _from_rows(r, shape):
    if shape[-1] == QKV:
        return r.reshape(4, 2 * ROW)[:, 0:QKV].reshape(shape)
    n = _numel(shape)
    if n % ROW == 0:
        return r.reshape(shape)
    return r[:, 0:n].reshape(shape)


def _pack_small(vals):
    parts = [(_rows_of(vals[n].astype(F32), r) if n in vals else jnp.zeros((r, ROW), F32)) for n, _, r in SMALL_ROWS]
    used = sum(r for _, _, r in SMALL_ROWS)
    return jnp.concatenate(parts + [jnp.zeros((R_SMALL - used, ROW), F32)], axis=0)


def _unpack_small(packed):
    out, o = {}, 0
    for n, shape, r in SMALL_ROWS:
        out[n] = _from_rows(packed[o:o + r], shape)
        o += r
    return out


def _pack_small_shard(ps, cw):
    return jnp.concatenate([
        jnp.pad(ps, ((0, 0), (0, ROW - D // N_DEV))), jnp.pad(cw[0], ((0, 0), (0, ROW - QKV // N_DEV))),
        jnp.zeros((3, ROW), F32)], axis=0)


def _mixer_weights(g_in, g_out, g_small, small):
    w = {}
    wi = jnp.transpose(g_in, (1, 0, 2)).reshape(D, AB_IN)
    main = [wi[:, 0:2048], wi[:, 2056:AB_IN]]
    w["wab_f"] = jnp.concatenate(
        main + [jnp.repeat(wi[:, 2048:2052], HD, axis=1), jnp.repeat(wi[:, 2052:2056], HD, axis=1)], axis=1)
    w["wab_b"] = jnp.concatenate(main + [wi[:, 2048:2056], jnp.zeros((D, 120), wi.dtype)], axis=1)
    w["cw"] = jnp.transpose(g_small[:, 1:5, 0:QKV // N_DEV], (1, 0, 2)).reshape(4, QKV)
    w["ps"] = g_small[:, 0, 0:D // N_DEV].reshape(1, D)
    w["alog"] = jnp.repeat(small["dn_a_log"][0].astype(F32), HD).reshape(1, HW)
    w["dtb"] = jnp.repeat(small["dn_dt_bias"][0].astype(F32), HD).reshape(1, HW)
    w["dnw"] = jnp.tile(small["dn_out_norm"][0].astype(F32), HEADS).reshape(1, HW)
    w["sgnw"] = small["sg_norm"][0].astype(F32).reshape(1, HW)
    tri = jnp.tril(jnp.ones((SGC, SGC), dtype=bool))
    wt = jnp.where(tri, small["sg_w"][0].astype(F32), 0.0)
    w["wtril"] = _c(wt)
    w["wtril_t"] = _c(jnp.transpose(wt, (0, 2, 1)))
    w["sgb"] = jnp.repeat(jnp.transpose(small["sg_b"][0].astype(F32)), HD, axis=1)
    w["wout_ab"] = g_out.reshape(D, D)
    return w


def kernel(x, ffn_norm1, ffn1_w_in, ffn1_w_out, mix_norm, ffn_norm2, ffn2_w_in, ffn2_w_out, ab_w_in, dn_conv_w, dn_a_log, dn_dt_bias, dn_out_norm, sg_norm, sg_w, sg_b, ab_w_out, pool_w, pool_scale, final_norm, loss_target, m_ffn_norm1, m_ffn1_w_in, m_ffn1_w_out, m_mix_norm, m_ffn_norm2, m_ffn2_w_in, m_ffn2_w_out, m_ab_w_in, m_dn_conv_w, m_dn_a_log, m_dn_dt_bias, m_dn_out_norm, m_sg_norm, m_sg_w, m_sg_b, m_ab_w_out, m_pool_w, m_pool_scale, m_final_norm, v_ffn_norm1, v_ffn1_w_in, v_ffn1_w_out, v_mix_norm, v_ffn_norm2, v_ffn2_w_in, v_ffn2_w_out, v_ab_w_in, v_dn_conv_w, v_dn_a_log, v_dn_dt_bias, v_dn_out_norm, v_sg_norm, v_sg_w, v_sg_b, v_ab_w_out, v_pool_w, v_pool_scale, v_final_norm):
    wl = dict(ffn_norm1=ffn_norm1, mix_norm=mix_norm, ffn_norm2=ffn_norm2, dn_a_log=dn_a_log, dn_dt_bias=dn_dt_bias,
              dn_out_norm=dn_out_norm, sg_norm=sg_norm, sg_w=sg_w, sg_b=sg_b, final_norm=final_norm)
    ml = dict(ffn_norm1=m_ffn_norm1, mix_norm=m_mix_norm, ffn_norm2=m_ffn_norm2, dn_a_log=m_dn_a_log,
              dn_dt_bias=m_dn_dt_bias, dn_out_norm=m_dn_out_norm, sg_norm=m_sg_norm, sg_w=m_sg_w, sg_b=m_sg_b,
              final_norm=m_final_norm)
    vl = dict(ffn_norm1=v_ffn_norm1, mix_norm=v_mix_norm, ffn_norm2=v_ffn_norm2, dn_a_log=v_dn_a_log,
              dn_dt_bias=v_dn_dt_bias, dn_out_norm=v_dn_out_norm, sg_norm=v_sg_norm, sg_w=v_sg_w, sg_b=v_sg_b,
              final_norm=v_final_norm)
    row = lambda a: a.reshape(1, -1).astype(F32)
    n1 = [row(ffn_norm1[l]) for l in range(2)]
    n2 = [row(ffn_norm2[l]) for l in range(2)]
    mix = [row(mix_norm[l]) for l in range(2)]
    s_in = {(f, l): _c(wf[l]) for f, wf in enumerate((ffn1_w_in, ffn2_w_in)) for l in range(2)}
    s_out = {(f, l): _c(wf[l]) for f, wf in enumerate((ffn1_w_out, ffn2_w_out)) for l in range(2)}
    xs, tgt = x[0], loss_target[0]

    wi00, wo00 = _comm_call(_Comm("gather", [s_in[0, 0], s_out[0, 0]]), "gather_first")
    x01, gu00, (g_abin, g_about, g_small, wi10) = _ffn_fwd(
        xs, n1[0], wi00, wo00,
        comm=_Comm("gather", [_c(ab_w_in[0]), _c(ab_w_out[0]), _pack_small_shard(pool_scale, dn_conv_w), s_in[1, 0]]))
    w = _mixer_weights(g_abin, g_about, g_small, wl)
    h, qkv, z, su, sv, b_rep, a_rep = _ab_proj(x01, mix[0], w["wab_f"])
    qn, kn, v, beta, g = _dn_pre(qkv, b_rep, a_rep, w["cw"], w["alog"], w["dtb"])
    (o, sall, aall, dn_u, dn_w), (wo10, g_pw) = _dn_fwd(
        qn, kn, v, beta, g, comm=_Comm("gather", [s_out[1, 0], _c(pool_w[0])]))
    pw = jnp.transpose(g_pw, (1, 0, 2, 3)).reshape(4, PG, PG)
    x02, cat = _ab_out(x01, o, z, su, sv, w["dnw"], w["sgnw"], w["wtril"], w["sgb"], w["wout_ab"])
    x10, gu10, (wi01, wo01) = _ffn_fwd(x02, n2[0], wi10, wo10, comm=_Comm("gather", [s_in[0, 1], s_out[0, 1]]))
    x11, gu01, (wi11, wo11) = _ffn_fwd(x10, n1[1], wi01, wo01, comm=_Comm("gather", [s_in[1, 1], s_out[1, 1]]))
    x12 = _pool_fwd(x11, mix[1], pw, w["ps"])
    x13, gu11, _ = _ffn_fwd(x12, n2[1], wi11, wo11)
    loss_local, dx, d_fn = _loss_head(x13, row(final_norm), tgt)

    bt = min(BT, xs.shape[0])

    def ffn_b(xin, nw, w_in, w_out, gu, dy, comm=None):
        (dxn, xn, act, dh, dnw), landed = _ffn_bwd(xin, nw, w_in, w_out, gu, dy, comm)
        return dxn, dnw, (xn, act, dh), landed

    def ffn_g(kept, dy):
        return [_mm_tn_win(kept[0], kept[2])[0], _mm_tn_wout(kept[1], dy)]

    dy = dx
    dx, d_n2_1, kept, _ = ffn_b(x12, n2[1], wi11, wo11, gu11, dy)
    g11 = ffn_g(kept, dy)
    dx, d_mix_1, d_pw, d_ps = _pool_bwd(x11, mix[1], pw, w["ps"], dx)
    d_pw_sh = _c(jnp.transpose(d_pw.reshape(4, N_DEV, PG // N_DEV, PG), (1, 0, 2, 3)))
    dy = dx
    dx, d_n1_1, kept, land11 = ffn_b(x10, n1[1], wi01, wo01, gu01, dy, _Comm("exchange", g11))
    g01 = ffn_g(kept, dy)
    dy = dx
    dx, d_n2_0, kept, land01 = ffn_b(x02, n2[0], wi10, wo10, gu10, dy, _Comm("exchange", g01 + [d_pw_sh]))
    g10 = ffn_g(kept, dy)
    do, dz, dsu, dsv, d_dnw, d_sgnw, d_sgw, d_sgb = _ab_out_bwd(
        dx, o, z, su, sv, w["dnw"], w["sgnw"], w["wtril"], w["wtril_t"], w["sgb"], w["wout_ab"])
    d_about = _mm_tn(cat, dx, D, D, bt, _MM, "mm_tn_about").reshape(N_DEV, D // N_DEV, D)
    (dqn, dkn, dv, dbeta, dg), _ = _dn_bwd(qn, kn, v, beta, g, sall, aall, dn_u, dn_w, do)
    dc, dba, d_cw, d_alog, d_dtb = _dn_pre_bwd(qkv, b_rep, a_rep, w["cw"], w["alog"], w["dtb"], dqn, dkn, dv, dbeta, dg)
    dqkv = _conv_bwd(dc, w["cw"])
    dx, dcat, d_mix_0 = _ab_proj_bwd(x01, mix[0], dqkv, dz, dsu, dsv, dba, w["wab_b"], dx)
    d_wab = _mm_tn(h, dcat, D, 640, bt, _MM, "mm_tn_abin")
    d_abin = jnp.concatenate([d_wab[:, 0:2048], d_wab[:, 3072:3080], d_wab[:, 2048:3072]], axis=1)
    d_abin_sh = jnp.transpose(d_abin.reshape(D, N_DEV, AB_IN // N_DEV), (1, 0, 2))
    dy = dx
    grad_x, d_n1_0, kept, land_mid = ffn_b(xs, n1[0], wi00, wo00, gu00, dy,
                                           _Comm("exchange", g10 + [d_abin_sh, d_about]))
    land10, land_ab = land_mid[0:2], land_mid[2:4]

    g_small = {
        "ffn_norm1": jnp.concatenate([d_n1_0, d_n1_1], axis=0),
        "mix_norm": jnp.concatenate([d_mix_0, d_mix_1], axis=0),
        "ffn_norm2": jnp.concatenate([d_n2_0, d_n2_1], axis=0),
        "dn_conv_w": d_cw.reshape(1, 4, QKV),
        "dn_a_log": d_alog[:, ::HD],
        "dn_dt_bias": d_dtb[:, ::HD],
        "dn_out_norm": d_dnw,
        "sg_norm": d_sgnw.reshape(1, HEADS, HD),
        "sg_w": d_sgw[None],
        "sg_b": jnp.transpose(d_sgb[:, 0:HEADS])[None],
        "pool_scale": d_ps,
        "final_norm": d_fn.reshape(D),
    }
    g00_out = _mm_tn_wout(kept[1], dy)
    g00_in, (land00_out, land_small) = _mm_tn_win(
        kept[0], kept[2], comm=_Comm("exchange", [g00_out], repl=[_pack_small(g_small)]))
    (land00_in,) = _comm_call(_Comm("exchange", [g00_in]), "exchange_last")

    res = {}
    tr = lambda a: jnp.swapaxes(a, 1, 2)
    res["ffn1_w_in"] = [tr(a) for a in _adamw([land00_in, land01[0]], tr(ffn1_w_in), tr(m_ffn1_w_in), tr(v_ffn1_w_in),
                                              176, "adamw_w_in")]
    res["ffn2_w_in"] = [tr(a) for a in _adamw([land10[0], land11[0]], tr(ffn2_w_in), tr(m_ffn2_w_in), tr(v_ffn2_w_in),
                                              176, "adamw_w_in")]
    res["ffn1_w_out"] = _adamw([land00_out, land01[1]], ffn1_w_out, m_ffn1_w_out, v_ffn1_w_out, 176, "adamw_w_out")
    res["ffn2_w_out"] = _adamw([land10[1], land11[1]], ffn2_w_out, m_ffn2_w_out, v_ffn2_w_out, 176, "adamw_w_out")
    res["ab_w_in"] = _adamw([land_ab[0]], ab_w_in, m_ab_w_in, v_ab_w_in, 256, "adamw_ab_w_in")
    res["ab_w_out"] = _adamw([land_ab[1]], ab_w_out, m_ab_w_out, v_ab_w_out, D // N_DEV, "adamw_ab_w_out")
    res["pool_w"] = _adamw([land01[2]], pool_w, m_pool_w, v_pool_w, 4, "adamw_pool_w")
    sm = _adamw([land_small], _pack_small(wl)[None], _pack_small(ml)[None], _pack_small(vl)[None], R_SMALL,
                "adamw_replicated")
    sm = [_unpack_small(a[0]) for a in sm]
    for n in wl:
        res[n] = [d[n] for d in sm]
    me = 4 * lax.axis_index("x") + 2 * lax.axis_index("y") + lax.axis_index("c")
    g_ps = lax.dynamic_slice(sm[0]["pool_scale"], (0, me * (D // N_DEV)), (1, D // N_DEV))
    g_cw = lax.dynamic_slice(sm[0]["dn_conv_w"], (0, 0, me * (QKV // N_DEV)), (1, 4, QKV // N_DEV))
    s2 = _adamw([_pack_small_shard(g_ps, g_cw)[None]], _pack_small_shard(pool_scale, dn_conv_w)[None],
                _pack_small_shard(m_pool_scale, m_dn_conv_w)[None], _pack_small_shard(v_pool_scale, v_dn_conv_w)[None],
                8, "adamw_small_sharded")
    res["pool_scale"] = [a[0, 0:1, 0:D // N_DEV] for a in s2]
    res["dn_conv_w"] = [a[0, 1:5, 0:QKV // N_DEV][None] for a in s2]

    loss = lax.psum(loss_local[0, 0], ("x", "y", "c"))
    result = [loss, grad_x[None]]
    for i in range(4):
        result += [res[n][i] for n in WEIGHT_ORDER]
    return tuple(result)
```

```python
import jax
import jax.numpy as jnp
from jax import lax
from jax.experimental import pallas as pl
from jax.experimental.pallas import tpu as pltpu

F32 = jnp.float32
_MM = jnp.bfloat16

D = 1024
FF = 2816
EPS = 1e-6
HEADS = 4
HD = 128
DNC = 64
DN_STEP = 8
SGC = 128
QKV = 3 * HEADS * HD
HW = HEADS * HD
POOL_WINDOWS = (2, 4, 8, 16)
PG = D // 4
HALO = 16
N_DEV = 8
AB_IN = 3080
ROW = 1024

TM = 512
BT = 1024
FT = 512
FC = 704
NJ = FF // FC
WO_ROWS = FF // N_DEV

ADAM_LR, ADAM_B1, ADAM_B2, ADAM_EPS, ADAM_WD, ADAM_STEP = 0.001, 0.9, 0.999, 1e-08, 0.01, 10

MESH_T = pl.DeviceIdType.MESH
NN = (((1,), (0,)), ((), ()))
NT = (((1,), (1,)), ((), ()))
TN = (((0,), (0,)), ((), ()))


def _c(a):
    return a.astype(_MM)


def _dg(a, b, dims):
    return lax.dot_general(a, b, dims, preferred_element_type=F32)


def _dot(a, b):
    return _dg(a, b, NN)


def _dot_nt(a, b):
    return _dg(a, b, NT)


def _dot_tn(a, b):
    return _dg(a, b, TN)


def _split2(a):
    hi = _c(a)
    return hi, _c(a - hi.astype(F32))


def _dot3(a, b, dims=NN):
    ah, al = _split2(a)
    bh, bl = _split2(b)
    return _dg(ah, bh, dims) + (_dg(ah, bl, dims) + _dg(al, bh, dims))


def _mask_dot(mask, x):
    x1 = _c(x)
    r = x - x1.astype(F32)
    x2 = _c(r)
    x3 = _c(r - x2.astype(F32))
    return _dot(mask, x1) + (_dot(mask, x2) + _dot(mask, x3))


def _sigmoid(x):
    return jax.nn.sigmoid(x)


def _gelu(x):
    return 0.5 * x * (1.0 + lax.erf(x * 0.7071067811865476))


def _gelu_grad(x):
    return 0.5 * (1.0 + lax.erf(x * 0.7071067811865476)) + x * jnp.exp(-0.5 * x * x) * 0.3989422804014327


def _accum(ref, val, step):
    @pl.when(step == 0)
    def _():
        ref[...] = val

    @pl.when(step > 0)
    def _():
        ref[...] += val


def _rstd(x):
    return lax.rsqrt(jnp.mean(x * x, axis=-1, keepdims=True) + EPS)


def _rms_bwd(dy, xhat, r, nw):
    dnw = jnp.sum(dy * xhat, axis=0, keepdims=True)
    dxh = dy * nw
    dx = r * (dxh - xhat * jnp.mean(dxh * xhat, axis=-1, keepdims=True))
    return dx, dnw


def _numel(shape):
    n = 1
    for s in shape:
        n *= s
    return n


def _peer(k, x, y, c):
    px = 1 - x if k & 4 else x
    py = 1 - y if k & 2 else y
    pc = 1 - c if k & 1 else c
    return px, py, pc


class _Comm:
    def __init__(self, kind, arrs, repl=()):
        self.kind = kind
        self.ns = len(arrs)
        self.arrs = list(arrs) + list(repl)
        self.na = len(self.arrs)

    @property
    def out_shape(self):
        out = []
        for i, a in enumerate(self.arrs):
            lead = (N_DEV,) if (self.kind == "gather" or i >= self.ns) else ()
            out.append(jax.ShapeDtypeStruct(lead + a.shape, a.dtype))
        return out

    @property
    def scratch(self):
        return [pltpu.SemaphoreType.DMA((7 * self.na,)), pltpu.SemaphoreType.DMA((7 * self.na,)),
                pltpu.SemaphoreType.DMA((self.na,))]

    def phases(self, ins, outs, sems):
        send_sems, recv_sems, local_sems = sems
        na = self.na
        x, y, c = lax.axis_index("x"), lax.axis_index("y"), lax.axis_index("c")
        if self.kind == "gather":
            me, sibling = (x, y, c), (x, y, 1 - c)
            chips = [(1 - x, y), (x, 1 - y), (1 - x, 1 - y)]

            def slot(a, px, py, pc):
                return outs[a].at[4 * px + 2 * py + pc]

            def copy(a, k, block, to, src=None):
                return pltpu.make_async_remote_copy(
                    src_ref=slot(a, *block) if src is None else src, dst_ref=slot(a, *block),
                    send_sem=send_sems.at[7 * a + k], recv_sem=recv_sems.at[7 * a + k],
                    device_id=to, device_id_type=MESH_T)

            mine = [pltpu.make_async_copy(ins[a], slot(a, *me), local_sems.at[a]) for a in range(na)]
            first, passed = [], []
            for a in range(na):
                first.append(copy(a, 0, me, sibling, src=ins[a]))
                first += [copy(a, 1 + j, me, (*chip, c), src=ins[a]) for j, chip in enumerate(chips)]
                passed += [copy(a, 4 + j, (*chip, c), sibling) for j, chip in enumerate(chips)]

            def start():
                for cp in mine + first:
                    cp.start()

            def middle():
                for a in range(na):
                    for j, chip in enumerate(chips):
                        copy(a, 1 + j, (*chip, c), me).wait_recv()
                        passed[3 * a + j].start()

            def finish():
                for a in range(na):
                    copy(a, 0, sibling, me).wait_recv()
                    for j, chip in enumerate(chips):
                        copy(a, 4 + j, (*chip, 1 - c), me).wait_recv()
                for cp in first + passed:
                    cp.wait_send()
                for cp in mine:
                    cp.wait()

            return start, middle, finish

        me = 4 * x + 2 * y + c
        ns = self.ns
        own = [pltpu.make_async_copy(ins[a].at[me] if a < ns else ins[a], outs[a].at[me], local_sems.at[a])
               for a in range(na)]
        copies = []
        for k in range(1, N_DEV):
            px, py, pc = _peer(k, x, y, c)
            peer = 4 * px + 2 * py + pc
            for a in range(na):
                copies.append(pltpu.make_async_remote_copy(
                    src_ref=ins[a].at[peer] if a < ns else ins[a], dst_ref=outs[a].at[me],
                    send_sem=send_sems.at[na * (k - 1) + a], recv_sem=recv_sems.at[na * (k - 1) + a],
                    device_id=(px, py, pc), device_id_type=MESH_T))

        def start():
            for cp in own + copies:
                cp.start()

        def middle():
            pass

        def finish():
            for cp in copies:
                cp.wait()
            for cp in own:
                cp.wait()

        return start, middle, finish


def _comm_call(comm, name):
    na = comm.na

    def body(*refs):
        start, middle, finish = comm.phases(refs[0:na], refs[na:2 * na], refs[2 * na:])
        start()
        middle()
        finish()

    hbm = pl.BlockSpec(memory_space=pltpu.HBM)
    return pl.pallas_call(
        body, out_shape=comm.out_shape, in_specs=[hbm] * na, out_specs=[hbm] * na, scratch_shapes=comm.scratch,
        name=name)(*comm.arrs)


def _carried_call(body, comm, n_in, n_out, n_scr, when, *, grid, in_specs, out_specs, out_shape, scratch_shapes,
                  operands, name):
    if comm is None:
        return pl.pallas_call(body, grid=grid, in_specs=in_specs, out_specs=out_specs, out_shape=out_shape,
                              scratch_shapes=scratch_shapes, name=name)(*operands), []
    na = comm.na

    def both(*refs):
        a = n_in + na
        b = a + n_out + na
        body(*refs[0:n_in], *refs[a:a + n_out], *refs[b:b + n_scr])
        start, middle, finish = comm.phases(refs[n_in:a], refs[a + n_out:b], refs[b + n_scr:])
        first, mid, last = when()
        pl.when(first)(start)
        pl.when(mid)(middle)
        pl.when(last)(finish)

    hbm = pl.BlockSpec(memory_space=pltpu.HBM)
    res = pl.pallas_call(
        both, grid=grid, in_specs=list(in_specs) + [hbm] * na, out_specs=list(out_specs) + [hbm] * na,
        out_shape=list(out_shape) + comm.out_shape, scratch_shapes=list(scratch_shapes) + comm.scratch,
        name=name)(*operands, *comm.arrs)
    return res[0:n_out], res[n_out:]


def _ffn_w_specs():
    return [
        pl.BlockSpec((None, D, FC), lambda i, j: (j, 0, 0)),
        pl.BlockSpec((None, D, FC), lambda i, j: (j + NJ, 0, 0)),
        pl.BlockSpec((2, WO_ROWS, D), lambda i, j: (j, 0, 0)),
    ]


def _ffn_when(nt):
    def when():
        i, j = pl.program_id(0), pl.program_id(1)
        return ((i == 0) & (j == 0), (i == (3 * nt) // 4) & (j == 0), (i == nt - 1) & (j == NJ - 1))
    return when


def _ffn_fwd(x, nw, w_in, w_out, comm=None):
    t = x.shape[0]
    tm = min(FT, t)
    nt = t // tm

    def body(x_ref, nw_ref, wg_ref, wu_ref, wo3_ref, o_ref, gu_ref, xn_sc, acc_sc):
        j = pl.program_id(1)

        @pl.when(j == 0)
        def _():
            xv = x_ref[...]
            xn_sc[...] = _c(xv * _rstd(xv) * nw_ref[...])
            acc_sc[...] = jnp.zeros_like(acc_sc)

        xn = xn_sc[...]
        g = _dot(xn, wg_ref[...])
        u = _dot(xn, wu_ref[...])
        gu_ref[0] = _c(g)
        gu_ref[1] = _c(u)
        acc_sc[...] += _dot(_c(g * _sigmoid(g) * u), wo3_ref[...].reshape(FC, D))

        @pl.when(j == NJ - 1)
        def _():
            o_ref[...] = x_ref[...] + 0.5 * acc_sc[...]

    (out, gu), landed = _carried_call(
        body, comm, 5, 2, 2, _ffn_when(nt),
        grid=(nt, NJ),
        in_specs=[pl.BlockSpec((tm, D), lambda i, j: (i, 0)), pl.BlockSpec((1, D), lambda i, j: (0, 0))]
        + _ffn_w_specs(),
        out_specs=[pl.BlockSpec((tm, D), lambda i, j: (i, 0)),
                   pl.BlockSpec((None, 2, tm, FC), lambda i, j: (j, 0, i, 0))],
        out_shape=[jax.ShapeDtypeStruct((t, D), F32), jax.ShapeDtypeStruct((NJ, 2, t, FC), _MM)],
        scratch_shapes=[pltpu.VMEM((tm, D), _MM), pltpu.VMEM((tm, D), F32)],
        operands=(x, nw, w_in, w_in, w_out), name="ffn_fwd")
    return out, gu, landed


def _ffn_bwd(x, nw, w_in, w_out, gu, dy, comm=None):
    t = x.shape[0]
    nt = t // TM

    def body(x_ref, nw_ref, wg_ref, wu_ref, wo3_ref, gu_ref, dy_ref, dx_ref, xn_ref, a_ref, dh_ref, dnw_ref,
             r_sc, dyb_sc, acc_sc):
        wo = wo3_ref[...].reshape(FC, D)
        i = pl.program_id(0)
        j = pl.program_id(1)

        @pl.when(j == 0)
        def _():
            xv = x_ref[...]
            r = _rstd(xv)
            r_sc[...] = r
            xn_ref[...] = _c(xv * r * nw_ref[...])
            dyb_sc[...] = _c(0.5 * dy_ref[...])
            acc_sc[...] = jnp.zeros_like(acc_sc)

        g = gu_ref[0].astype(F32)
        u = gu_ref[1].astype(F32)
        s = _sigmoid(g)
        sl = g * s
        a_ref[...] = _c(sl * u)
        da = _dot_nt(dyb_sc[...], wo)
        dg = _c(da * u * (s * (1.0 + g * (1.0 - s))))
        du = _c(da * sl)
        dh_ref[0] = dg
        dh_ref[1] = du
        acc_sc[...] += _dot_nt(dg, wg_ref[...]) + _dot_nt(du, wu_ref[...])

        @pl.when(j == NJ - 1)
        def _():
            r = r_sc[...]
            dx, dnw = _rms_bwd(acc_sc[...], x_ref[...] * r, r, nw_ref[...])
            dx_ref[...] = dy_ref[...] + dx
            _accum(dnw_ref, dnw, i)

    return _carried_call(
        body, comm, 7, 5, 3, _ffn_when(nt),
        grid=(nt, NJ),
        in_specs=[pl.BlockSpec((TM, D), lambda i, j: (i, 0)), pl.BlockSpec((1, D), lambda i, j: (0, 0))]
        + _ffn_w_specs() + [pl.BlockSpec((None, 2, TM, FC), lambda i, j: (j, 0, i, 0)),
                            pl.BlockSpec((TM, D), lambda i, j: (i, 0))],
        out_specs=[
            pl.BlockSpec((TM, D), lambda i, j: (i, 0)),
            pl.BlockSpec((TM, D), lambda i, j: (i, 0)),
            pl.BlockSpec((None, TM, FC), lambda i, j: (j, i, 0)),
            pl.BlockSpec((None, 2, TM, FC), lambda i, j: (j, 0, i, 0)),
            pl.BlockSpec((1, D), lambda i, j: (0, 0)),
        ],
        out_shape=[
            jax.ShapeDtypeStruct((t, D), F32),
            jax.ShapeDtypeStruct((t, D), _MM),
            jax.ShapeDtypeStruct((NJ, t, FC), _MM),
            jax.ShapeDtypeStruct((NJ, 2, t, FC), _MM),
            jax.ShapeDtypeStruct((1, D), F32),
        ],
        scratch_shapes=[pltpu.VMEM((TM, 1), F32), pltpu.VMEM((TM, D), _MM), pltpu.VMEM((TM, D), F32)],
        operands=(x, nw, w_in, w_in, w_out, gu, dy), name="ffn_bwd")


def _mm_tn(a, b, bm, bn, bt, out_dtype, name):
    t, m = a.shape
    n = b.shape[1]
    nt = t // bt

    def body(a_ref, b_ref, o_ref, acc_sc):
        k = pl.program_id(2)
        _accum(acc_sc, _dot_tn(_c(a_ref[...]), _c(b_ref[...])), k)

        @pl.when(k == nt - 1)
        def _():
            o_ref[...] = acc_sc[...].astype(out_dtype)

    return pl.pallas_call(
        body,
        grid=(m // bm, n // bn, nt),
        in_specs=[pl.BlockSpec((bt, bm), lambda i, j, k: (k, i)), pl.BlockSpec((bt, bn), lambda i, j, k: (k, j))],
        out_specs=pl.BlockSpec((bm, bn), lambda i, j, k: (i, j)),
        out_shape=jax.ShapeDtypeStruct((m, n), out_dtype),
        scratch_shapes=[pltpu.VMEM((bm, bn), F32)],
        name=name,
    )(a, b)


def _mm_tn_win(xn, dh, comm=None):
    t = xn.shape[0]
    bt = min(BT, t)
    nt = t // bt

    def body(a_ref, b_ref, o_ref, acc_sc):
        k = pl.program_id(2)
        _accum(acc_sc, _dot_tn(b_ref[...], a_ref[...]), k)

        @pl.when(k == nt - 1)
        def _():
            o_ref[...] = _c(acc_sc[...])

    def when():
        h, j, k = pl.program_id(0), pl.program_id(1), pl.program_id(2)
        start = (h == 0) & (j == 0) & (k == 0)
        return start, start, (h == 1) & (j == NJ - 1) & (k == nt - 1)

    (out,), landed = _carried_call(
        body, comm, 2, 1, 1, when,
        grid=(2, NJ, nt),
        in_specs=[pl.BlockSpec((bt, D), lambda h, j, k: (k, 0)),
                  pl.BlockSpec((None, None, bt, FC), lambda h, j, k: (j, h, k, 0))],
        out_specs=[pl.BlockSpec((None, FC, D), lambda h, j, k: (h * NJ + j, 0, 0))],
        out_shape=[jax.ShapeDtypeStruct((N_DEV, FC, D), _MM)],
        scratch_shapes=[pltpu.VMEM((FC, D), F32)],
        operands=(xn, dh), name="mm_tn_win")
    return out, landed


def _mm_tn_wout(act, dy):
    t = dy.shape[0]
    bt = min(BT, t)
    nt = t // bt

    def body(a_ref, b_ref, o_ref, acc_sc):
        k = pl.program_id(1)
        _accum(acc_sc, _dot_tn(a_ref[...], _c(b_ref[...])), k)

        @pl.when(k == nt - 1)
        def _():
            o_ref[...] = _c((0.5 * acc_sc[...]).reshape(2, WO_ROWS, D))

    return pl.pallas_call(
        body,
        grid=(NJ, nt),
        in_specs=[pl.BlockSpec((None, bt, FC), lambda j, k: (j, k, 0)), pl.BlockSpec((bt, D), lambda j, k: (k, 0))],
        out_specs=pl.BlockSpec((2, WO_ROWS, D), lambda j, k: (j, 0, 0)),
        out_shape=jax.ShapeDtypeStruct((N_DEV, WO_ROWS, D), _MM),
        scratch_shapes=[pltpu.VMEM((FC, D), F32)],
        name="mm_tn_wout",
    )(act, dy)


def _loss_head(x, nw, tgt):
    t = x.shape[0]

    def body(x_ref, nw_ref, t_ref, loss_ref, dx_ref, dnw_ref):
        i = pl.program_id(0)
        xv = x_ref[...]
        r = _rstd(xv)
        xh = xv * r
        e = xh * nw_ref[...] - t_ref[...]
        part = 0.5 * jnp.sum(jnp.mean(e * e, axis=-1, keepdims=True), axis=0, keepdims=True)
        _accum(loss_ref, jnp.broadcast_to(part, (1, 128)), i)
        dx, dnw = _rms_bwd(e * (1.0 / D), xh, r, nw_ref[...])
        dx_ref[...] = dx
        _accum(dnw_ref, dnw, i)

    return pl.pallas_call(
        body,
        grid=(t // TM,),
        in_specs=[pl.BlockSpec((TM, D), lambda i: (i, 0)), pl.BlockSpec((1, D), lambda i: (0, 0)),
                  pl.BlockSpec((TM, D), lambda i: (i, 0))],
        out_specs=[pl.BlockSpec((1, 128), lambda i: (0, 0)), pl.BlockSpec((TM, D), lambda i: (i, 0)),
                   pl.BlockSpec((1, D), lambda i: (0, 0))],
        out_shape=[jax.ShapeDtypeStruct((1, 128), F32), jax.ShapeDtypeStruct((t, D), F32),
                   jax.ShapeDtypeStruct((1, D), F32)],
        name="loss_head",
    )(x, nw, tgt)


PW_F = QKV + 5 * HW
PW_B = QKV + 3 * HW + 128
AB_MAIN = QKV + HW
AB_GATES = 2 * HEADS


def _ab_proj(x1, nw, wab):
    t = x1.shape[0]

    def body(x_ref, nw_ref, w_ref, h_ref, qkv_ref, z_ref, su_ref, sv_ref, b_ref, a_ref):
        xv = x_ref[...]
        h = _c(xv * _rstd(xv) * nw_ref[...])
        h_ref[...] = h
        p = _dot(h, w_ref[...])
        qkv_ref[...] = p[:, 0:QKV]
        o = QKV
        for ref in (z_ref, su_ref, sv_ref, b_ref, a_ref):
            ref[...] = p[:, o:o + HW]
            o += HW

    row = lambda w: pl.BlockSpec((TM, w), lambda i: (i, 0))
    return pl.pallas_call(
        body,
        grid=(t // TM,),
        in_specs=[row(D), pl.BlockSpec((1, D), lambda i: (0, 0)), pl.BlockSpec((D, PW_F), lambda i: (0, 0))],
        out_specs=[row(D), row(QKV)] + [row(HW)] * 5,
        out_shape=[jax.ShapeDtypeStruct((t, D), _MM), jax.ShapeDtypeStruct((t, QKV), F32)]
        + [jax.ShapeDtypeStruct((t, HW), F32)] * 5,
        name="ab_proj",
    )(x1, nw, wab)


def _conv_rows(x, halo, cw):
    xe = jnp.concatenate([halo, x], axis=0)
    shifted = []
    c = None
    for k in range(4):
        s = 3 - k
        xs = (xe if s == 0 else pltpu.roll(xe, s, 0))[8:, :]
        shifted.append(xs)
        term = cw[k:k + 1, :] * xs
        c = term if c is None else c + term
    return c, shifted


def _head_rsq(a):
    parts = []
    for h in range(HEADS):
        ah = a[:, h * HD:(h + 1) * HD]
        r = lax.rsqrt(jnp.sum(ah * ah, axis=-1, keepdims=True) + EPS)
        parts.append(jnp.broadcast_to(r, ah.shape))
    return jnp.concatenate(parts, axis=-1)


def _head_sum(a):
    parts = []
    for h in range(HEADS):
        ah = a[:, h * HD:(h + 1) * HD]
        parts.append(jnp.broadcast_to(jnp.sum(ah, axis=-1, keepdims=True), ah.shape))
    return jnp.concatenate(parts, axis=-1)


def _softplus(x):
    return jnp.maximum(x, 0.0) + jnp.log1p(jnp.exp(-jnp.abs(x)))


def _halo_prev_spec(width, rows):
    per = TM // rows
    return pl.BlockSpec((rows, width), lambda i: (jnp.maximum(i * per - 1, 0), 0))


def _halo_next_spec(width, rows, t):
    per = TM // rows
    last = t // rows - 1
    return pl.BlockSpec((rows, width), lambda i: (jnp.minimum((i + 1) * per, last), 0))


def _dn_pre(qkv, b_rep, a_rep, cw, alog, dtb):
    t = qkv.shape[0]
    qscale = HD ** -0.5

    def body(x_ref, halo_ref, b_ref, a_ref, cw_ref, alog_ref, dt_ref, q_ref, k_ref, v_ref, beta_ref, g_ref):
        i = pl.program_id(0)
        halo = jnp.where(i == 0, 0.0, halo_ref[...])
        c, _ = _conv_rows(x_ref[...], halo, cw_ref[...])
        sc = c * _sigmoid(c)
        q = sc[:, 0:HW]
        k = sc[:, HW:2 * HW]
        q_ref[...] = q * _head_rsq(q) * qscale
        k_ref[...] = k * _head_rsq(k)
        v_ref[...] = sc[:, 2 * HW:]
        beta_ref[...] = _sigmoid(b_ref[...])
        g_ref[...] = -jnp.exp(alog_ref[...]) * _softplus(a_ref[...] + dt_ref[...])

    row = lambda w: pl.BlockSpec((TM, w), lambda i: (i, 0))
    full = lambda a: pl.BlockSpec(a.shape, lambda i: (0,) * a.ndim)
    return pl.pallas_call(
        body,
        grid=(t // TM,),
        in_specs=[row(QKV), _halo_prev_spec(QKV, 8), row(HW), row(HW), full(cw), full(alog), full(dtb)],
        out_specs=[row(HW)] * 5,
        out_shape=[jax.ShapeDtypeStruct((t, HW), F32)] * 5,
        name="dn_pre",
    )(qkv, qkv, b_rep, a_rep, cw, alog, dtb)


def _unit_lower_inv(los, eye):
    ps = [eye - lo for lo in los]
    lps = list(los)
    for _ in range(5):
        lps = [_dot(_c(lp), _c(lp)) for lp in lps]
        ps = [p + _dot(_c(p), _c(lp)) for p, lp in zip(ps, lps)]
    rs = [eye - (p + _dot3(lo, p)) for lo, p in zip(los, ps)]
    return [p + _dot(_c(p), _c(r)) for p, r in zip(ps, rs)]


def _dn_masks():
    ri = lax.broadcasted_iota(jnp.int32, (DNC, DNC), 0)
    ci = lax.broadcasted_iota(jnp.int32, (DNC, DNC), 1)
    return dict(strict=ri > ci, causal=ri >= ci, eye=(ri == ci).astype(F32),
                ltri=_c((ri >= ci).astype(F32)), upper=_c((ri <= ci).astype(F32)))


def _dn_decay(gr, mk):
    rhs = jnp.concatenate([gr, jnp.where(mk["strict"], gr[:, 0:DNC], 0.0)], axis=1)
    cs = _mask_dot(mk["ltri"], rhs)
    gc = cs[:, 0:HD]
    dm = jnp.where(mk["causal"], jnp.exp(cs[:, HD:HD + DNC]), 0.0)
    gl = jnp.sum(gr, axis=0, keepdims=True)
    return dm, jnp.exp(gc), jnp.exp(gl - gc), gl


def _dn_when(n):
    def when():
        i = pl.program_id(0)
        return (i == 0, i == n // 2, i == n - 1)
    return when


def _dn_fwd(q, k, v, beta, g, comm=None):
    t = q.shape[0]
    rows = DN_STEP * DNC
    n = t // rows

    def body(q_ref, k_ref, v_ref, b_ref, g_ref, o_ref, sall_ref, aall_ref, u_ref, w_ref, s_sc):
        i = pl.program_id(0)

        @pl.when(i == 0)
        def _():
            s_sc[...] = jnp.zeros_like(s_sc)

        mk = _dn_masks()
        idx = [(cc, h) for cc in range(DN_STEP) for h in range(HEADS)]
        at = lambda cc, h: (slice(cc * DNC, (cc + 1) * DNC), slice(h * HD, (h + 1) * HD))
        qs = [q_ref[at(*i)] for i in idx]
        ks = [k_ref[at(*i)] for i in idx]
        bs = [b_ref[at(*i)] for i in idx]
        dec = [_dn_decay(g_ref[at(*i)], mk) for i in idx]
        kbs = [k_ * b_ for k_, b_ in zip(ks, bs)]
        los = [jnp.where(mk["strict"], _dot_nt(_c(kb), _c(k_)) * d[0], 0.0) for kb, k_, d in zip(kbs, ks, dec)]
        inv = _unit_lower_inv(los, mk["eye"])
        uws = [_dot3(a, jnp.concatenate([v_ref[at(*i)] * b_, kb * d[1]], axis=1))
               for a, i, b_, kb, d in zip(inv, idx, bs, kbs, dec)]
        attn = [_c(_dot_nt(_c(q_), _c(k_)) * d[0]) for q_, k_, d in zip(qs, ks, dec)]
        for n_, (cc, h) in enumerate(idx):
            aall_ref[cc, h] = inv[n_]
            u_ref[at(cc, h)] = uws[n_][:, 0:HD]
            w_ref[at(cc, h)] = uws[n_][:, HD:]
        ss = [s_sc[h] for h in range(HEADS)]
        for cc in range(DN_STEP):
            base = cc * HEADS
            for h in range(HEADS):
                sall_ref[cc, h] = ss[h]
            ws = [_dot(_c(jnp.concatenate([uws[base + h][:, HD:], qs[base + h] * dec[base + h][1]], axis=0)),
                       _c(ss[h])) for h in range(HEADS)]
            vn = [_c(uws[base + h][:, 0:HD] - ws[h][0:DNC]) for h in range(HEADS)]
            for h in range(HEADS):
                o_ref[at(cc, h)] = ws[h][DNC:] + _dot(attn[base + h], vn[h])
            ss = [ss[h] * jnp.exp(dec[base + h][3]) + _dot_tn(_c(ks[base + h] * dec[base + h][2]), vn[h])
                  for h in range(HEADS)]
        for h in range(HEADS):
            s_sc[h] = ss[h]

    row = pl.BlockSpec((rows, HW), lambda i: (i, 0))
    return _carried_call(
        body, comm, 5, 5, 1, _dn_when(n),
        grid=(n,),
        in_specs=[row] * 5,
        out_specs=[row, pl.BlockSpec((DN_STEP, HEADS, HD, HD), lambda i: (i, 0, 0, 0)),
                   pl.BlockSpec((DN_STEP, HEADS, DNC, DNC), lambda i: (i, 0, 0, 0)), row, row],
        out_shape=[jax.ShapeDtypeStruct((t, HW), F32), jax.ShapeDtypeStruct((t // DNC, HEADS, HD, HD), F32),
                   jax.ShapeDtypeStruct((t // DNC, HEADS, DNC, DNC), F32), jax.ShapeDtypeStruct((t, HW), F32),
                   jax.ShapeDtypeStruct((t, HW), F32)],
        scratch_shapes=[pltpu.VMEM((HEADS, HD, HD), F32)],
        operands=(q, k, v, beta, g), name="dn_fwd")


def _dn_bwd(q, k, v, beta, g, sall, aall, u, w, do, comm=None):
    t = q.shape[0]
    rows = DN_STEP * DNC
    n = t // rows

    def body(q_ref, k_ref, v_ref, b_ref, g_ref, sall_ref, aall_ref, u_ref, w_ref, do_ref,
             dq_ref, dk_ref, dv_ref, db_ref, dg_ref, ds_sc):
        i = pl.program_id(0)

        @pl.when(i == 0)
        def _():
            ds_sc[...] = jnp.zeros_like(ds_sc)

        mk = _dn_masks()
        strict = mk["strict"]
        hs = range(HEADS)
        at = lambda cc, h: (slice(cc * DNC, (cc + 1) * DNC), slice(h * HD, (h + 1) * HD))
        rowsum = lambda a: jnp.sum(a, axis=-1, keepdims=True)
        dsn = [ds_sc[h] for h in hs]
        for cc in reversed(range(DN_STEP)):
            q = [q_ref[at(cc, h)] for h in hs]
            k = [k_ref[at(cc, h)] for h in hs]
            b = [b_ref[at(cc, h)] for h in hs]
            u = [u_ref[at(cc, h)] for h in hs]
            w = [w_ref[at(cc, h)] for h in hs]
            do = [do_ref[at(cc, h)] for h in hs]
            s = [sall_ref[cc, h] for h in hs]
            dec = [_dn_decay(g_ref[at(cc, h)], mk) for h in hs]
            dm, e, f = [d[0] for d in dec], [d[1] for d in dec], [d[2] for d in dec]
            egl = [jnp.exp(d[3]) for d in dec]
            kb = [k[h] * b[h] for h in hs]
            kc = [_c(k[h]) for h in hs]
            sb = [_c(s[h]) for h in hs]
            dob = [_c(do[h]) for h in hs]
            m = [_dot_nt(_c(kb[h]), kc[h]) for h in hs]
            p = [_dot_nt(_c(q[h]), kc[h]) for h in hs]
            vnb = [_c(u[h] - _dot(_c(w[h]), sb[h])) for h in hs]
            dsb = [_c(dsn[h]) for h in hs]
            dvn = [_dot_tn(_c(p[h] * dm[h]), dob[h]) + _dot(_c(k[h] * f[h]), dsb[h]) for h in hs]
            dov = [_c(jnp.concatenate([do[h], dvn[h]], axis=0)) for h in hs]
            t1 = [_dot_nt(dov[h], sb[h]) for h in hs]
            dattn = [_dot_nt(dob[h], vnb[h]) for h in hs]
            dkt = [_dot_nt(vnb[h], dsb[h]) for h in hs]
            dgl = [jnp.sum(jnp.sum(dsn[h] * s[h], axis=1, keepdims=True), axis=0, keepdims=True) * egl[h][:, 0:1]
                   for h in hs]
            dsn = [dsn[h] * egl[h] + _dot_tn(_c(jnp.concatenate([q[h] * e[h], -w[h]], axis=0)), dov[h]) for h in hs]
            dqd = [t1[h][0:DNC] for h in hs]
            dw = [-t1[h][DNC:] for h in hs]
            ab = [_dot3(aall_ref[cc, h], jnp.concatenate([dvn[h], dw[h]], axis=1), TN) for h in hs]
            dlo = [jnp.where(strict, -_dot3(ab[h], jnp.concatenate([u[h], w[h]], axis=1), NT), 0.0) for h in hs]
            dpm = [_c(jnp.concatenate([dattn[h] * dm[h], dlo[h] * dm[h]], axis=0)) for h in hs]
            t2 = [_dot(dpm[h], kc[h]) for h in hs]
            t4 = [_dot_tn(dpm[h], _c(jnp.concatenate([q[h], kb[h]], axis=0))) for h in hs]
            dff = [rowsum(dkt[h] * k[h]) * f[h][:, 0:1] for h in hs]
            de = [rowsum(dqd[h] * q[h]) + rowsum(ab[h][:, HD:] * kb[h]) for h in hs]
            dd = [(dattn[h] * p[h] + dlo[h] * m[h]) * dm[h] for h in hs]
            t3 = [_mask_dot(mk["upper"], jnp.concatenate(
                [jnp.broadcast_to(de[h] * e[h][:, 0:1] - dff[h], (DNC, HD)), dd[h]], axis=1)) for h in hs]
            for h in hs:
                dvb, dkbe = ab[h][:, 0:HD], ab[h][:, HD:]
                dkb = t2[h][DNC:] + dkbe * e[h]
                dbeta = rowsum(dkb * k[h]) + rowsum(dvb * v_ref[at(cc, h)])
                dg = (rowsum(jnp.where(strict, t3[h][:, HD:HD + DNC], 0.0)) + t3[h][:, 0:1]
                      + dgl[h] + jnp.sum(dff[h], axis=0, keepdims=True))
                dq_ref[at(cc, h)] = dqd[h] * e[h] + t2[h][0:DNC]
                dk_ref[at(cc, h)] = t4[h] + dkt[h] * f[h] + dkb * b[h]
                dv_ref[at(cc, h)] = dvb * b[h]
                db_ref[at(cc, h)] = jnp.broadcast_to(dbeta, (DNC, HD))
                dg_ref[at(cc, h)] = jnp.broadcast_to(dg, (DNC, HD))
        for h in hs:
            ds_sc[h] = dsn[h]

    row = pl.BlockSpec((rows, HW), lambda i: (n - 1 - i, 0))
    return _carried_call(
        body, comm, 10, 5, 1, _dn_when(n),
        grid=(n,),
        in_specs=[row] * 5 + [pl.BlockSpec((DN_STEP, HEADS, HD, HD), lambda i: (n - 1 - i, 0, 0, 0)),
                              pl.BlockSpec((DN_STEP, HEADS, DNC, DNC), lambda i: (n - 1 - i, 0, 0, 0)), row, row, row],
        out_specs=[row] * 5,
        out_shape=[jax.ShapeDtypeStruct((t, HW), F32)] * 5,
        scratch_shapes=[pltpu.VMEM((HEADS, HD, HD), F32)],
        operands=(q, k, v, beta, g, sall, aall, u, w, do), name="dn_bwd")


def _group_norm(a, nw):
    rs = []
    for h in range(HEADS):
        ah = a[:, h * HD:(h + 1) * HD]
        rs.append(jnp.broadcast_to(_rstd(ah), ah.shape))
    r = jnp.concatenate(rs, axis=-1)
    xh = a * r
    return xh * nw, xh, r


def _group_norm_bwd(dy, xh, r, nw):
    dxh = dy * nw
    return r * (dxh - xh * (_head_sum(dxh * xh) * (1.0 / HD)))


def _sg_mix(wt_ref, svn_b, nchunk):
    rows = []
    for cidx in range(nchunk):
        cols = []
        for g in range(HEADS):
            blk = svn_b[cidx * SGC:(cidx + 1) * SGC, g * HD:(g + 1) * HD]
            cols.append(_dot(wt_ref[g], blk))
        rows.append(jnp.concatenate(cols, axis=-1))
    return jnp.concatenate(rows, axis=0)


def _ab_out(x1, o, z, su, sv, dnw, sgnw, wtril, sgb, wout):
    t = x1.shape[0]
    nchunk = TM // SGC

    def body(x_ref, o_ref, z_ref, su_ref, sv_ref, dnw_ref, sgnw_ref, wt_ref, sgb_ref, wo_ref, x2_ref, cat_ref):
        on, _, _ = _group_norm(o_ref[...], dnw_ref[...])
        zv = z_ref[...]
        cat_ref[:, 0:HW] = _c(on * (zv * _sigmoid(zv)))
        svn, _, _ = _group_norm(_gelu(sv_ref[...]), sgnw_ref[...])
        mixed = _sg_mix(wt_ref, _c(svn), nchunk) + jnp.tile(sgb_ref[...], (nchunk, 1))
        cat_ref[:, HW:] = _c(_gelu(su_ref[...]) * mixed)
        x2_ref[...] = x_ref[...] + _dot(cat_ref[...], wo_ref[...])

    row = lambda w: pl.BlockSpec((TM, w), lambda i: (i, 0))
    full = lambda a: pl.BlockSpec(a.shape, lambda i: (0,) * a.ndim)
    return pl.pallas_call(
        body,
        grid=(t // TM,),
        in_specs=[row(D)] + [row(HW)] * 4 + [full(dnw), full(sgnw), full(wtril), full(sgb), full(wout)],
        out_specs=[row(D), row(D)],
        out_shape=[jax.ShapeDtypeStruct((t, D), F32), jax.ShapeDtypeStruct((t, D), _MM)],
        name="ab_out",
    )(x1, o, z, su, sv, dnw, sgnw, wtril, sgb, wout)


def _ab_out_bwd(dx2, o, z, su, sv, dnw, sgnw, wtril, wtril_t, sgb, wout):
    t = dx2.shape[0]
    nchunk = TM // SGC

    def body(dx_ref, o_ref, z_ref, su_ref, sv_ref, dnw_ref, sgnw_ref, wt_ref, wtt_ref, sgb_ref, wo_ref,
             do_ref, dz_ref, dsu_ref, dsv_ref, ddnw_ref, dsgnw_ref, dsgw_ref, dsgb_ref):
        i = pl.program_id(0)
        dcat = _dot_nt(_c(dx_ref[...]), wo_ref[...])
        doa = dcat[:, 0:HW]
        dob = dcat[:, HW:]
        on, oh, ro = _group_norm(o_ref[...], dnw_ref[...])
        zv = z_ref[...]
        sz = _sigmoid(zv)
        dz_ref[...] = _c(doa * on * (sz * (1.0 + zv * (1.0 - sz))))
        don = doa * (zv * sz)
        do_ref[...] = _group_norm_bwd(don, oh, ro, dnw_ref[...])
        dd = jnp.sum(don * oh, axis=0, keepdims=True)
        _accum(ddnw_ref, dd[:, 0:HD] + dd[:, HD:2 * HD] + dd[:, 2 * HD:3 * HD] + dd[:, 3 * HD:], i)
        suv = su_ref[...]
        svv = sv_ref[...]
        svg = _gelu(svv)
        svn, sh, rs = _group_norm(svg, sgnw_ref[...])
        svn_b = _c(svn)
        mixed = _sg_mix(wt_ref, svn_b, nchunk) + jnp.tile(sgb_ref[...], (nchunk, 1))
        dsu_ref[...] = _c(dob * mixed * _gelu_grad(suv))
        dmixed = dob * _gelu(suv)
        dmb = _c(dmixed)
        tri = lax.broadcasted_iota(jnp.int32, (SGC, SGC), 0) >= lax.broadcasted_iota(jnp.int32, (SGC, SGC), 1)
        lane = lax.broadcasted_iota(jnp.int32, (SGC, HD), 1)
        rows = []
        dbias = jnp.zeros((SGC, HD), F32)
        for g in range(HEADS):
            gs = slice(g * HD, (g + 1) * HD)
            dwg = jnp.zeros((SGC, SGC), F32)
            col = jnp.zeros((SGC, 1), F32)
            for cidx in range(nchunk):
                cs = slice(cidx * SGC, (cidx + 1) * SGC)
                dwg = dwg + _dot_nt(dmb[cs, gs], svn_b[cs, gs])
                col = col + jnp.sum(dmixed[cs, gs], axis=-1, keepdims=True)
            _accum(dsgw_ref.at[g], jnp.where(tri, dwg, 0.0), i)
            dbias = dbias + jnp.where(lane == g, col, 0.0)
        _accum(dsgb_ref, dbias, i)
        for cidx in range(nchunk):
            cs = slice(cidx * SGC, (cidx + 1) * SGC)
            rows.append(jnp.concatenate(
                [_dot(wtt_ref[g], dmb[cs, g * HD:(g + 1) * HD]) for g in range(HEADS)], axis=-1))
        dsvn = jnp.concatenate(rows, axis=0)
        _accum(dsgnw_ref, jnp.sum(dsvn * sh, axis=0, keepdims=True), i)
        dsv_ref[...] = _c(_group_norm_bwd(dsvn, sh, rs, sgnw_ref[...]) * _gelu_grad(svv))

    row = lambda w: pl.BlockSpec((TM, w), lambda i: (i, 0))
    full = lambda a: pl.BlockSpec(a.shape, lambda i: (0,) * a.ndim)
    const = lambda shape: pl.BlockSpec(shape, lambda i: (0,) * len(shape))
    return pl.pallas_call(
        body,
        grid=(t // TM,),
        in_specs=[row(D)] + [row(HW)] * 4 + [full(dnw), full(sgnw), full(wtril), full(wtril_t), full(sgb), full(wout)],
        out_specs=[row(HW)] * 4 + [const((1, HD)), const((1, HW)), const((HEADS, SGC, SGC)), const((SGC, HD))],
        out_shape=[jax.ShapeDtypeStruct((t, HW), F32)] + [jax.ShapeDtypeStruct((t, HW), _MM)] * 3
        + [jax.ShapeDtypeStruct((1, HD), F32), jax.ShapeDtypeStruct((1, HW), F32),
           jax.ShapeDtypeStruct((HEADS, SGC, SGC), F32), jax.ShapeDtypeStruct((SGC, HD), F32)],
        name="ab_out_bwd",
    )(dx2, o, z, su, sv, dnw, sgnw, wtril, wtril_t, sgb, wout)


def _dn_pre_bwd(qkv, b_rep, a_rep, cw, alog, dtb, dqn, dkn, dv, dbeta, dg):
    t = qkv.shape[0]
    qscale = HD ** -0.5

    def body(x_ref, halo_ref, b_ref, a_ref, cw_ref, alog_ref, dt_ref, dq_ref, dk_ref, dv_ref, dbeta_ref, dg_ref,
             dc_ref, dba_ref, dcw_ref, dalog_ref, ddt_ref):
        i = pl.program_id(0)
        halo = jnp.where(i == 0, 0.0, halo_ref[...])
        c, shifted = _conv_rows(x_ref[...], halo, cw_ref[...])
        s = _sigmoid(c)
        sc = c * s
        q = sc[:, 0:HW]
        k = sc[:, HW:2 * HW]
        rq = _head_rsq(q)
        rk = _head_rsq(k)
        qu = q * rq
        ku = k * rk
        dqn = dq_ref[...]
        dkn = dk_ref[...]
        dq = qscale * rq * (dqn - qu * _head_sum(dqn * qu))
        dk = rk * (dkn - ku * _head_sum(dkn * ku))
        dsc = jnp.concatenate([dq, dk, dv_ref[...]], axis=-1)
        dc = dsc * (s * (1.0 + c * (1.0 - s)))
        dc_ref[...] = _c(dc)
        for kk in range(4):
            _accum(dcw_ref.at[kk], jnp.sum(dc * shifted[kk], axis=0, keepdims=True), i)
        beta = _sigmoid(b_ref[...])
        dbp = dbeta_ref[...] * beta * (1.0 - beta)
        nea = -jnp.exp(alog_ref[...])
        spin = a_ref[...] + dt_ref[...]
        dgv = dg_ref[...]
        dap = dgv * nea * _sigmoid(spin)
        _accum(dalog_ref, jnp.sum(dgv * nea * _softplus(spin), axis=0, keepdims=True), i)
        _accum(ddt_ref, jnp.sum(dap, axis=0, keepdims=True), i)
        lane = lax.broadcasted_iota(jnp.int32, (TM, HD), 1)
        dba = jnp.zeros((TM, HD), F32)
        for h in range(HEADS):
            dba = dba + jnp.where(lane == h, dbp[:, h * HD:(h + 1) * HD], 0.0)
            dba = dba + jnp.where(lane == HEADS + h, dap[:, h * HD:(h + 1) * HD], 0.0)
        dba_ref[...] = _c(dba)

    row = lambda w: pl.BlockSpec((TM, w), lambda i: (i, 0))
    full = lambda a: pl.BlockSpec(a.shape, lambda i: (0,) * a.ndim)
    const = lambda shape: pl.BlockSpec(shape, lambda i: (0,) * len(shape))
    return pl.pallas_call(
        body,
        grid=(t // TM,),
        in_specs=[row(QKV), _halo_prev_spec(QKV, 8), row(HW), row(HW), full(cw), full(alog), full(dtb)] + [row(HW)] * 5,
        out_specs=[row(QKV), row(HD), const((4, 1, QKV)), const((1, HW)), const((1, HW))],
        out_shape=[jax.ShapeDtypeStruct((t, QKV), _MM), jax.ShapeDtypeStruct((t, HD), _MM),
                   jax.ShapeDtypeStruct((4, 1, QKV), F32), jax.ShapeDtypeStruct((1, HW), F32),
                   jax.ShapeDtypeStruct((1, HW), F32)],
        name="dn_pre_bwd",
    )(qkv, qkv, b_rep, a_rep, cw, alog, dtb, dqn, dkn, dv, dbeta, dg)


def _conv_bwd(dc, cw):
    t = dc.shape[0]
    nt = t // TM

    def body(dc_ref, halo_ref, cw_ref, dx_ref):
        i = pl.program_id(0)
        halo = jnp.where(i == nt - 1, 0.0, halo_ref[...].astype(F32))
        de = jnp.concatenate([dc_ref[...].astype(F32), halo], axis=0)
        cwv = cw_ref[...]
        acc = None
        for k in range(4):
            s = 3 - k
            ds = (de if s == 0 else pltpu.roll(de, TM + HALO - s, 0))[0:TM, :]
            term = cwv[k:k + 1, :] * ds
            acc = term if acc is None else acc + term
        dx_ref[...] = _c(acc)

    return pl.pallas_call(
        body,
        grid=(nt,),
        in_specs=[pl.BlockSpec((TM, QKV), lambda i: (i, 0)), _halo_next_spec(QKV, HALO, t),
                  pl.BlockSpec(cw.shape, lambda i: (0, 0))],
        out_specs=pl.BlockSpec((TM, QKV), lambda i: (i, 0)),
        out_shape=jax.ShapeDtypeStruct((t, QKV), _MM),
        name="conv_bwd",
    )(dc, dc, cw)


def _ab_proj_bwd(x1, nw, dqkv, dz, dsu, dsv, dba, wab_b, dres):
    t = x1.shape[0]

    def body(x_ref, nw_ref, dqkv_ref, dz_ref, dsu_ref, dsv_ref, dba_ref, w_ref, dres_ref, dx_ref, dcat_ref, dnw_ref):
        i = pl.program_id(0)
        dcat_ref[:, 0:QKV] = dqkv_ref[...]
        o = QKV
        for ref in (dz_ref, dsu_ref, dsv_ref):
            dcat_ref[:, o:o + HW] = ref[...]
            o += HW
        dcat_ref[:, o:o + 128] = dba_ref[...]
        dh = _dot_nt(dcat_ref[...], w_ref[...])
        xv = x_ref[...]
        r = _rstd(xv)
        dx, dnw = _rms_bwd(dh, xv * r, r, nw_ref[...])
        dx_ref[...] = dres_ref[...] + dx
        _accum(dnw_ref, dnw, i)

    row = lambda w: pl.BlockSpec((TM, w), lambda i: (i, 0))
    return pl.pallas_call(
        body,
        grid=(t // TM,),
        in_specs=[row(D), pl.BlockSpec((1, D), lambda i: (0, 0)), row(QKV), row(HW), row(HW), row(HW), row(128),
                  pl.BlockSpec((D, PW_B), lambda i: (0, 0)), row(D)],
        out_specs=[row(D), row(PW_B), pl.BlockSpec((1, D), lambda i: (0, 0))],
        out_shape=[jax.ShapeDtypeStruct((t, D), F32), jax.ShapeDtypeStruct((t, PW_B), _MM),
                   jax.ShapeDtypeStruct((1, D), F32)],
        name="ab_proj_bwd",
    )(x1, nw, dqkv, dz, dsu, dsv, dba, wab_b, dres)


def _pool_counts(i):
    pos = (lax.broadcasted_iota(jnp.int32, (TM + HALO, 1), 0) + i * TM + 1).astype(F32)
    return [1.0 / jnp.minimum(pos, float(w)) for w in POOL_WINDOWS]


def _window_sum(ext, win, back):
    r = ext.shape[0]
    s = ext
    step = 1
    while step < win:
        s = s + pltpu.roll(s, step if back else r - step, 0)
        step *= 2
    return s


def _pooled(h_ext, invc, g):
    gs = slice(g * PG, (g + 1) * PG)
    he = h_ext[:, gs]
    ws = _window_sum(he, POOL_WINDOWS[g], True)[HALO:, :]
    return ws * invc[g][0:TM, :] - he[HALO:, :]


def _pool_fwd(x1, nw, pw, scale):
    t = x1.shape[0]

    def body(x_ref, halo_ref, nw_ref, pw_ref, sc_ref, x2_ref):
        i = pl.program_id(0)
        xv = x_ref[...]
        hv = halo_ref[...]
        nwv = nw_ref[...]
        h_ext = jnp.concatenate([jnp.where(i == 0, 0.0, hv * _rstd(hv) * nwv), xv * _rstd(xv) * nwv], axis=0)
        invc = _pool_counts(i)
        outs = [_dot(_c(_pooled(h_ext, invc, g)), pw_ref[g]) for g in range(4)]
        x2_ref[...] = xv + jnp.concatenate(outs, axis=-1) * sc_ref[...]

    return pl.pallas_call(
        body,
        grid=(t // TM,),
        in_specs=[pl.BlockSpec((TM, D), lambda i: (i, 0)), _halo_prev_spec(D, HALO),
                  pl.BlockSpec((1, D), lambda i: (0, 0)), pl.BlockSpec((4, PG, PG), lambda i: (0, 0, 0)),
                  pl.BlockSpec((1, D), lambda i: (0, 0))],
        out_specs=pl.BlockSpec((TM, D), lambda i: (i, 0)),
        out_shape=jax.ShapeDtypeStruct((t, D), F32),
        name="pool_fwd",
    )(x1, x1, nw, pw, scale)


def _pool_bwd(x1, nw, pw, scale, dx2):
    t = x1.shape[0]
    nt = t // TM

    def body(x_ref, halo_ref, nw_ref, pw_ref, sc_ref, dx2_ref, dnext_ref, dx_ref, dnw_ref, dpw_ref, dsc_ref):
        i = pl.program_id(0)
        xv = x_ref[...]
        hv = halo_ref[...]
        nwv = nw_ref[...]
        r = _rstd(xv)
        xh = xv * r
        h_ext = jnp.concatenate([jnp.where(i == 0, 0.0, hv * _rstd(hv) * nwv), xh * nwv], axis=0)
        invc = _pool_counts(i)
        dyv = dx2_ref[...]
        dout_ext = jnp.concatenate([dyv, jnp.where(i == nt - 1, 0.0, dnext_ref[...])], axis=0) * sc_ref[...]
        dh_parts = []
        dsc_parts = []
        for g in range(4):
            gs = slice(g * PG, (g + 1) * PG)
            pooled_b = _c(_pooled(h_ext, invc, g))
            dout_b = _c(dout_ext[:, gs])
            dsc_parts.append(jnp.sum(dyv[:, gs] * _dot(pooled_b, pw_ref[g]), axis=0, keepdims=True))
            _accum(dpw_ref.at[g], _dot_tn(pooled_b, dout_b[0:TM, :]), i)
            dpool_ext = _dot_nt(dout_b, pw_ref[g])
            lead = _window_sum(dpool_ext * invc[g], POOL_WINDOWS[g], False)[0:TM, :]
            dh_parts.append(lead - dpool_ext[0:TM, :])
        _accum(dsc_ref, jnp.concatenate(dsc_parts, axis=-1), i)
        dx, dnw = _rms_bwd(jnp.concatenate(dh_parts, axis=-1), xh, r, nwv)
        dx_ref[...] = dyv + dx
        _accum(dnw_ref, dnw, i)

    vec = pl.BlockSpec((1, D), lambda i: (0, 0))
    return pl.pallas_call(
        body,
        grid=(nt,),
        in_specs=[pl.BlockSpec((TM, D), lambda i: (i, 0)), _halo_prev_spec(D, HALO), vec,
                  pl.BlockSpec((4, PG, PG), lambda i: (0, 0, 0)), vec,
                  pl.BlockSpec((TM, D), lambda i: (i, 0)), _halo_next_spec(D, HALO, t)],
        out_specs=[pl.BlockSpec((TM, D), lambda i: (i, 0)), vec, pl.BlockSpec((4, PG, PG), lambda i: (0, 0, 0)), vec],
        out_shape=[jax.ShapeDtypeStruct((t, D), F32), jax.ShapeDtypeStruct((1, D), F32),
                   jax.ShapeDtypeStruct((4, PG, PG), F32), jax.ShapeDtypeStruct((1, D), F32)],
        name="pool_bwd",
    )(x1, x1, nw, pw, scale, dx2, dx2)


def _adamw(lands, w, m, v, rb, name):
    nl, nr = w.shape[0], w.shape[1]
    rest = w.shape[2:]
    ns = lands[0].shape[0]
    zeros = (0,) * len(rest)

    def body(*refs):
        l_refs = refs[0:nl]
        w_ref, m_ref, v_ref, g_ref, d_ref, m2_ref, v2_ref = refs[nl:]
        for l in range(nl):
            g = l_refs[l][0].astype(F32)
            for s in range(1, ns):
                g = g + l_refs[l][s].astype(F32)
            m2 = ADAM_B1 * m_ref[l] + (1.0 - ADAM_B1) * g
            v2 = ADAM_B2 * v_ref[l] + (1.0 - ADAM_B2) * (g * g)
            m_hat = m2 / (1.0 - ADAM_B1 ** ADAM_STEP)
            v_hat = v2 / (1.0 - ADAM_B2 ** ADAM_STEP)
            g_ref[l] = g
            d_ref[l] = -ADAM_LR * (m_hat / (jnp.sqrt(v_hat) + ADAM_EPS) + ADAM_WD * w_ref[l])
            m2_ref[l] = m2
            v2_ref[l] = v2

    lspec = pl.BlockSpec((ns, rb) + rest, lambda r: (0, r) + zeros)
    wspec = pl.BlockSpec((nl, rb) + rest, lambda r: (0, r) + zeros)
    return pl.pallas_call(
        body,
        grid=(nr // rb,),
        in_specs=[lspec] * nl + [wspec] * 3,
        out_specs=[wspec] * 4,
        out_shape=[jax.ShapeDtypeStruct(w.shape, F32)] * 4,
        name=name,
    )(*lands, w, m, v)


WEIGHT_ORDER = ("ffn_norm1", "ffn1_w_in", "ffn1_w_out", "mix_norm", "ffn_norm2", "ffn2_w_in", "ffn2_w_out", "ab_w_in",
                "dn_conv_w", "dn_a_log", "dn_dt_bias", "dn_out_norm", "sg_norm", "sg_w", "sg_b", "ab_w_out", "pool_w",
                "pool_scale", "final_norm")
R_SMALL = 88
SMALL_ROWS = (
    ("ffn_norm1", (2, D), 2), ("mix_norm", (2, D), 2), ("ffn_norm2", (2, D), 2), ("final_norm", (D,), 1),
    ("sg_w", (1, 4, SGC, SGC), 64), ("sg_norm", (1, 4, HD), 1), ("sg_b", (1, 4, SGC), 1), ("dn_out_norm", (1, HD), 1),
    ("dn_a_log", (1, 4), 1), ("dn_dt_bias", (1, 4), 1), ("pool_scale", (1, D), 1), ("dn_conv_w", (1, 4, QKV), 8),
)
SMALL_SHARDED = ("pool_scale", "dn_conv_w")


def _rows_of(a, rows):
    if a.shape[-1] == QKV:
        return jnp.pad(a.reshape(4, QKV), ((0, 0), (0, 2 * ROW - QKV))).reshape(8, ROW)
    n = _numel(a.shape)
    if n % ROW == 0:
        return a.reshape(n // ROW, ROW)
    return jnp.pad(a.reshape(1, n), ((0, 0), (0, ROW - n)))


def _from_rows(r, shape):
    if shape[-1] == QKV:
        return r.reshape(4, 2 * ROW)[:, 0:QKV].reshape(shape)
    n = _numel(shape)
    if n % ROW == 0:
        return r.reshape(shape)
    return r[:, 0:n].reshape(shape)


def _pack_small(vals):
    parts = [(_rows_of(vals[n].astype(F32), r) if n in vals else jnp.zeros((r, ROW), F32)) for n, _, r in SMALL_ROWS]
    used = sum(r for _, _, r in SMALL_ROWS)
    return jnp.concatenate(parts + [jnp.zeros((R_SMALL - used, ROW), F32)], axis=0)


def _unpack_small(packed):
    out, o = {}, 0
    for n, shape, r in SMALL_ROWS:
        out[n] = _from_rows(packed[o:o + r], shape)
        o += r
    return out


def _pack_small_shard(ps, cw):
    return jnp.concatenate([
        jnp.pad(ps, ((0, 0), (0, ROW - D // N_DEV))), jnp.pad(cw[0], ((0, 0), (0, ROW - QKV // N_DEV))),
        jnp.zeros((3, ROW), F32)], axis=0)


def _mixer_weights(g_in, g_out, g_small, small):
    w = {}
    wi = jnp.transpose(g_in, (1, 0, 2)).reshape(D, AB_IN)
    gates = wi[:, AB_MAIN:AB_MAIN + AB_GATES]
    main = [wi[:, 0:AB_MAIN], wi[:, AB_MAIN + AB_GATES:AB_IN]]
    w["wab_f"] = jnp.concatenate(
        main + [jnp.repeat(gates[:, 0:HEADS], HD, axis=1), jnp.repeat(gates[:, HEADS:], HD, axis=1)], axis=1)
    w["wab_b"] = jnp.concatenate(main + [gates, jnp.zeros((D, 128 - AB_GATES), wi.dtype)], axis=1)
    w["cw"] = jnp.transpose(g_small[:, 1:5, 0:QKV // N_DEV], (1, 0, 2)).reshape(4, QKV)
    w["ps"] = g_small[:, 0, 0:D // N_DEV].reshape(1, D)
    w["alog"] = jnp.repeat(small["dn_a_log"][0].astype(F32), HD).reshape(1, HW)
    w["dtb"] = jnp.repeat(small["dn_dt_bias"][0].astype(F32), HD).reshape(1, HW)
    w["dnw"] = jnp.tile(small["dn_out_norm"][0].astype(F32), HEADS).reshape(1, HW)
    w["sgnw"] = small["sg_norm"][0].astype(F32).reshape(1, HW)
    tri = jnp.tril(jnp.ones((SGC, SGC), dtype=bool))
    wt = jnp.where(tri, small["sg_w"][0].astype(F32), 0.0)
    w["wtril"] = _c(wt)
    w["wtril_t"] = _c(jnp.transpose(wt, (0, 2, 1)))
    w["sgb"] = jnp.repeat(jnp.transpose(small["sg_b"][0].astype(F32)), HD, axis=1)
    w["wout_ab"] = g_out.reshape(D, D)
    return w


def kernel(x, ffn_norm1, ffn1_w_in, ffn1_w_out, mix_norm, ffn_norm2, ffn2_w_in, ffn2_w_out, ab_w_in, dn_conv_w, dn_a_log, dn_dt_bias, dn_out_norm, sg_norm, sg_w, sg_b, ab_w_out, pool_w, pool_scale, final_norm, loss_target, m_ffn_norm1, m_ffn1_w_in, m_ffn1_w_out, m_mix_norm, m_ffn_norm2, m_ffn2_w_in, m_ffn2_w_out, m_ab_w_in, m_dn_conv_w, m_dn_a_log, m_dn_dt_bias, m_dn_out_norm, m_sg_norm, m_sg_w, m_sg_b, m_ab_w_out, m_pool_w, m_pool_scale, m_final_norm, v_ffn_norm1, v_ffn1_w_in, v_ffn1_w_out, v_mix_norm, v_ffn_norm2, v_ffn2_w_in, v_ffn2_w_out, v_ab_w_in, v_dn_conv_w, v_dn_a_log, v_dn_dt_bias, v_dn_out_norm, v_sg_norm, v_sg_w, v_sg_b, v_ab_w_out, v_pool_w, v_pool_scale, v_final_norm):
    wl = dict(ffn_norm1=ffn_norm1, mix_norm=mix_norm, ffn_norm2=ffn_norm2, dn_a_log=dn_a_log, dn_dt_bias=dn_dt_bias,
              dn_out_norm=dn_out_norm, sg_norm=sg_norm, sg_w=sg_w, sg_b=sg_b, final_norm=final_norm)
    ml = dict(ffn_norm1=m_ffn_norm1, mix_norm=m_mix_norm, ffn_norm2=m_ffn_norm2, dn_a_log=m_dn_a_log,
              dn_dt_bias=m_dn_dt_bias, dn_out_norm=m_dn_out_norm, sg_norm=m_sg_norm, sg_w=m_sg_w, sg_b=m_sg_b,
              final_norm=m_final_norm)
    vl = dict(ffn_norm1=v_ffn_norm1, mix_norm=v_mix_norm, ffn_norm2=v_ffn_norm2, dn_a_log=v_dn_a_log,
              dn_dt_bias=v_dn_dt_bias, dn_out_norm=v_dn_out_norm, sg_norm=v_sg_norm, sg_w=v_sg_w, sg_b=v_sg_b,
              final_norm=v_final_norm)
    row = lambda a: a.reshape(1, -1).astype(F32)
    n1 = [row(ffn_norm1[l]) for l in range(2)]
    n2 = [row(ffn_norm2[l]) for l in range(2)]
    mix = [row(mix_norm[l]) for l in range(2)]
    s_in = {(f, l): _c(wf[l]) for f, wf in enumerate((ffn1_w_in, ffn2_w_in)) for l in range(2)}
    s_out = {(f, l): _c(wf[l]) for f, wf in enumerate((ffn1_w_out, ffn2_w_out)) for l in range(2)}
    xs, tgt = x[0], loss_target[0]

    wi00, wo00 = _comm_call(_Comm("gather", [s_in[0, 0], s_out[0, 0]]), "gather_first")
    x01, gu00, (g_abin, g_about, g_small, wi10) = _ffn_fwd(
        xs, n1[0], wi00, wo00,
        comm=_Comm("gather", [_c(ab_w_in[0]), _c(ab_w_out[0]), _pack_small_shard(pool_scale, dn_conv_w), s_in[1, 0]]))
    w = _mixer_weights(g_abin, g_about, g_small, wl)
    h, qkv, z, su, sv, b_rep, a_rep = _ab_proj(x01, mix[0], w["wab_f"])
    qn, kn, v, beta, g = _dn_pre(qkv, b_rep, a_rep, w["cw"], w["alog"], w["dtb"])
    (o, sall, aall, dn_u, dn_w), (wo10, g_pw) = _dn_fwd(
        qn, kn, v, beta, g, comm=_Comm("gather", [s_out[1, 0], _c(pool_w[0])]))
    pw = jnp.transpose(g_pw, (1, 0, 2, 3)).reshape(4, PG, PG)
    x02, cat = _ab_out(x01, o, z, su, sv, w["dnw"], w["sgnw"], w["wtril"], w["sgb"], w["wout_ab"])
    x10, gu10, (wi01, wo01) = _ffn_fwd(x02, n2[0], wi10, wo10, comm=_Comm("gather", [s_in[0, 1], s_out[0, 1]]))
    x11, gu01, (wi11, wo11) = _ffn_fwd(x10, n1[1], wi01, wo01, comm=_Comm("gather", [s_in[1, 1], s_out[1, 1]]))
    x12 = _pool_fwd(x11, mix[1], pw, w["ps"])
    x13, gu11, _ = _ffn_fwd(x12, n2[1], wi11, wo11)
    loss_local, dx, d_fn = _loss_head(x13, row(final_norm), tgt)

    bt = min(BT, xs.shape[0])

    def ffn_b(xin, nw, w_in, w_out, gu, dy, comm=None):
        (dxn, xn, act, dh, dnw), landed = _ffn_bwd(xin, nw, w_in, w_out, gu, dy, comm)
        return dxn, dnw, (xn, act, dh), landed

    def ffn_g(kept, dy):
        return [_mm_tn_win(kept[0], kept[2])[0], _mm_tn_wout(kept[1], dy)]

    dy = dx
    dx, d_n2_1, kept, _ = ffn_b(x12, n2[1], wi11, wo11, gu11, dy)
    g11 = ffn_g(kept, dy)
    dx, d_mix_1, d_pw, d_ps = _pool_bwd(x11, mix[1], pw, w["ps"], dx)
    d_pw_sh = _c(jnp.transpose(d_pw.reshape(4, N_DEV, PG // N_DEV, PG), (1, 0, 2, 3)))
    dy = dx
    dx, d_n1_1, kept, land11 = ffn_b(x10, n1[1], wi01, wo01, gu01, dy, _Comm("exchange", g11))
    g01 = ffn_g(kept, dy)
    dy = dx
    dx, d_n2_0, kept, land01 = ffn_b(x02, n2[0], wi10, wo10, gu10, dy, _Comm("exchange", g01 + [d_pw_sh]))
    g10 = ffn_g(kept, dy)
    do, dz, dsu, dsv, d_dnw, d_sgnw, d_sgw, d_sgb = _ab_out_bwd(
        dx, o, z, su, sv, w["dnw"], w["sgnw"], w["wtril"], w["wtril_t"], w["sgb"], w["wout_ab"])
    d_about = _mm_tn(cat, dx, D, D, bt, _MM, "mm_tn_about").reshape(N_DEV, D // N_DEV, D)
    (dqn, dkn, dv, dbeta, dg), _ = _dn_bwd(qn, kn, v, beta, g, sall, aall, dn_u, dn_w, do)
    dc, dba, d_cw, d_alog, d_dtb = _dn_pre_bwd(qkv, b_rep, a_rep, w["cw"], w["alog"], w["dtb"], dqn, dkn, dv, dbeta, dg)
    dqkv = _conv_bwd(dc, w["cw"])
    dx, dcat, d_mix_0 = _ab_proj_bwd(x01, mix[0], dqkv, dz, dsu, dsv, dba, w["wab_b"], dx)
    d_wab = _mm_tn(h, dcat, D, 640, bt, _MM, "mm_tn_abin")
    rest = PW_B - 128
    d_abin = jnp.concatenate([d_wab[:, 0:AB_MAIN], d_wab[:, rest:rest + AB_GATES], d_wab[:, AB_MAIN:rest]], axis=1)
    d_abin_sh = jnp.transpose(d_abin.reshape(D, N_DEV, AB_IN // N_DEV), (1, 0, 2))
    dy = dx
    grad_x, d_n1_0, kept, land_mid = ffn_b(xs, n1[0], wi00, wo00, gu00, dy,
                                           _Comm("exchange", g10 + [d_abin_sh, d_about]))
    land10, land_ab = land_mid[0:2], land_mid[2:4]

    g_small = {
        "ffn_norm1": jnp.concatenate([d_n1_0, d_n1_1], axis=0),
        "mix_norm": jnp.concatenate([d_mix_0, d_mix_1], axis=0),
        "ffn_norm2": jnp.concatenate([d_n2_0, d_n2_1], axis=0),
        "dn_conv_w": d_cw.reshape(1, 4, QKV),
        "dn_a_log": d_alog[:, ::HD],
        "dn_dt_bias": d_dtb[:, ::HD],
        "dn_out_norm": d_dnw,
        "sg_norm": d_sgnw.reshape(1, HEADS, HD),
        "sg_w": d_sgw[None],
        "sg_b": jnp.transpose(d_sgb[:, 0:HEADS])[None],
        "pool_scale": d_ps,
        "final_norm": d_fn.reshape(D),
    }
    g00_out = _mm_tn_wout(kept[1], dy)
    g00_in, (land00_out, land_small) = _mm_tn_win(
        kept[0], kept[2], comm=_Comm("exchange", [g00_out], repl=[_pack_small(g_small)]))
    (land00_in,) = _comm_call(_Comm("exchange", [g00_in]), "exchange_last")

    res = {}
    tr = lambda a: jnp.swapaxes(a, 1, 2)
    res["ffn1_w_in"] = [tr(a) for a in _adamw([land00_in, land01[0]], tr(ffn1_w_in), tr(m_ffn1_w_in), tr(v_ffn1_w_in),
                                              176, "adamw_w_in")]
    res["ffn2_w_in"] = [tr(a) for a in _adamw([land10[0], land11[0]], tr(ffn2_w_in), tr(m_ffn2_w_in), tr(v_ffn2_w_in),
                                              176, "adamw_w_in")]
    res["ffn1_w_out"] = _adamw([land00_out, land01[1]], ffn1_w_out, m_ffn1_w_out, v_ffn1_w_out, 176, "adamw_w_out")
    res["ffn2_w_out"] = _adamw([land10[1], land11[1]], ffn2_w_out, m_ffn2_w_out, v_ffn2_w_out, 176, "adamw_w_out")
    res["ab_w_in"] = _adamw([land_ab[0]], ab_w_in, m_ab_w_in, v_ab_w_in, 256, "adamw_ab_w_in")
    res["ab_w_out"] = _adamw([land_ab[1]], ab_w_out, m_ab_w_out, v_ab_w_out, D // N_DEV, "adamw_ab_w_out")
    res["pool_w"] = _adamw([land01[2]], pool_w, m_pool_w, v_pool_w, 4, "adamw_pool_w")
    sm = _adamw([land_small], _pack_small(wl)[None], _pack_small(ml)[None], _pack_small(vl)[None], R_SMALL,
                "adamw_replicated")
    sm = [_unpack_small(a[0]) for a in sm]
    for n in wl:
        res[n] = [d[n] for d in sm]
    me = 4 * lax.axis_index("x") + 2 * lax.axis_index("y") + lax.axis_index("c")
    g_ps = lax.dynamic_slice(sm[0]["pool_scale"], (0, me * (D // N_DEV)), (1, D // N_DEV))
    g_cw = lax.dynamic_slice(sm[0]["dn_conv_w"], (0, 0, me * (QKV // N_DEV)), (1, 4, QKV // N_DEV))
    s2 = _adamw([_pack_small_shard(g_ps, g_cw)[None]], _pack_small_shard(pool_scale, dn_conv_w)[None],
                _pack_small_shard(m_pool_scale, m_dn_conv_w)[None], _pack_small_shard(v_pool_scale, v_dn_conv_w)[None],
                8, "adamw_small_sharded")
    res["pool_scale"] = [a[0, 0:1, 0:D // N_DEV] for a in s2]
    res["dn_conv_w"] = [a[0, 1:5, 0:QKV // N_DEV][None] for a in s2]

    loss = lax.psum(loss_local[0, 0], ("x", "y", "c"))
    result = [loss, grad_x[None]]
    for i in range(4):
        result += [res[n][i] for n in WEIGHT_ORDER]
    return tuple(result)
```

```python
import jax
import jax.numpy as jnp
from jax import lax
from jax.experimental import pallas as pl
from jax.experimental.pallas import tpu as pltpu

F32 = jnp.float32
_MM = jnp.bfloat16

D = 1024
FF = 2816
EPS = 1e-6
HEADS = 4
HD = 128
DNC = 64
DN_STEP = 8
SGC = 128
QKV = 3 * HEADS * HD
HW = HEADS * HD
POOL_WINDOWS = (2, 4, 8, 16)
PG = D // 4
HALO = 16
N_DEV = 8
AB_IN = 3080
ROW = 1024

TM = 512
BT = 2048
BT_WIN = 4096
FT = 512
FC = 704
NJ = FF // FC
WO_ROWS = FF // N_DEV

ADAM_LR, ADAM_B1, ADAM_B2, ADAM_EPS, ADAM_WD, ADAM_STEP = 0.001, 0.9, 0.999, 1e-08, 0.01, 10

MESH_T = pl.DeviceIdType.MESH
NN = (((1,), (0,)), ((), ()))
NT = (((1,), (1,)), ((), ()))
TN = (((0,), (0,)), ((), ()))


def _c(a):
    return a.astype(_MM)


def _dg(a, b, dims):
    return lax.dot_general(a, b, dims, preferred_element_type=F32)


def _dot(a, b):
    return _dg(a, b, NN)


def _dot_nt(a, b):
    return _dg(a, b, NT)


def _dot_tn(a, b):
    return _dg(a, b, TN)


def _split2(a):
    hi = _c(a)
    return hi, _c(a - hi.astype(F32))


def _dot3(a, b, dims=NN):
    ah, al = _split2(a)
    bh, bl = _split2(b)
    return _dg(ah, bh, dims) + (_dg(ah, bl, dims) + _dg(al, bh, dims))


def _mask_dot(mask, x):
    x1 = _c(x)
    r = x - x1.astype(F32)
    x2 = _c(r)
    x3 = _c(r - x2.astype(F32))
    return _dot(mask, x1) + (_dot(mask, x2) + _dot(mask, x3))


def _sigmoid(x):
    return jax.nn.sigmoid(x)


def _gelu(x):
    return 0.5 * x * (1.0 + lax.erf(x * 0.7071067811865476))


def _gelu_grad(x):
    return 0.5 * (1.0 + lax.erf(x * 0.7071067811865476)) + x * jnp.exp(-0.5 * x * x) * 0.3989422804014327


def _accum(ref, val, step):
    @pl.when(step == 0)
    def _():
        ref[...] = val

    @pl.when(step > 0)
    def _():
        ref[...] += val


def _rstd(x):
    return lax.rsqrt(jnp.mean(x * x, axis=-1, keepdims=True) + EPS)


def _rms_bwd(dy, xhat, r, nw):
    dnw = jnp.sum(dy * xhat, axis=0, keepdims=True)
    dxh = dy * nw
    dx = r * (dxh - xhat * jnp.mean(dxh * xhat, axis=-1, keepdims=True))
    return dx, dnw


def _numel(shape):
    n = 1
    for s in shape:
        n *= s
    return n


def _peer(k, x, y, c):
    px = 1 - x if k & 4 else x
    py = 1 - y if k & 2 else y
    pc = 1 - c if k & 1 else c
    return px, py, pc


class _Comm:
    def __init__(self, kind, arrs, repl=()):
        self.kind = kind
        self.ns = len(arrs)
        self.arrs = list(arrs) + list(repl)
        self.na = len(self.arrs)

    @property
    def out_shape(self):
        out = []
        for i, a in enumerate(self.arrs):
            lead = (N_DEV,) if (self.kind == "gather" or i >= self.ns) else ()
            out.append(jax.ShapeDtypeStruct(lead + a.shape, a.dtype))
        return out

    @property
    def scratch(self):
        return [pltpu.SemaphoreType.DMA((7 * self.na,)), pltpu.SemaphoreType.DMA((7 * self.na,)),
                pltpu.SemaphoreType.DMA((self.na,))]

    def phases(self, ins, outs, sems):
        send_sems, recv_sems, local_sems = sems
        na = self.na
        x, y, c = lax.axis_index("x"), lax.axis_index("y"), lax.axis_index("c")
        if self.kind == "gather":
            me, sibling = (x, y, c), (x, y, 1 - c)
            chips = [(1 - x, y), (x, 1 - y), (1 - x, 1 - y)]

            def slot(a, px, py, pc):
                return outs[a].at[4 * px + 2 * py + pc]

            def copy(a, k, block, to, src=None):
                return pltpu.make_async_remote_copy(
                    src_ref=slot(a, *block) if src is None else src, dst_ref=slot(a, *block),
                    send_sem=send_sems.at[7 * a + k], recv_sem=recv_sems.at[7 * a + k],
                    device_id=to, device_id_type=MESH_T)

            mine = [pltpu.make_async_copy(ins[a], slot(a, *me), local_sems.at[a]) for a in range(na)]
            first, passed = [], []
            for a in range(na):
                first.append(copy(a, 0, me, sibling, src=ins[a]))
                first += [copy(a, 1 + j, me, (*chip, c), src=ins[a]) for j, chip in enumerate(chips)]
                passed += [copy(a, 4 + j, (*chip, c), sibling) for j, chip in enumerate(chips)]

            def start():
                for cp in mine + first:
                    cp.start()

            def middle():
                for a in range(na):
                    for j, chip in enumerate(chips):
                        copy(a, 1 + j, (*chip, c), me).wait_recv()
                        passed[3 * a + j].start()

            def finish():
                for a in range(na):
                    copy(a, 0, sibling, me).wait_recv()
                    for j, chip in enumerate(chips):
                        copy(a, 4 + j, (*chip, 1 - c), me).wait_recv()
                for cp in first + passed:
                    cp.wait_send()
                for cp in mine:
                    cp.wait()

            return start, middle, finish

        me = 4 * x + 2 * y + c
        ns = self.ns
        own = [pltpu.make_async_copy(ins[a].at[me] if a < ns else ins[a], outs[a].at[me], local_sems.at[a])
               for a in range(na)]
        copies = []
        for k in range(1, N_DEV):
            px, py, pc = _peer(k, x, y, c)
            peer = 4 * px + 2 * py + pc
            for a in range(na):
                copies.append(pltpu.make_async_remote_copy(
                    src_ref=ins[a].at[peer] if a < ns else ins[a], dst_ref=outs[a].at[me],
                    send_sem=send_sems.at[na * (k - 1) + a], recv_sem=recv_sems.at[na * (k - 1) + a],
                    device_id=(px, py, pc), device_id_type=MESH_T))

        def start():
            for cp in own + copies:
                cp.start()

        def middle():
            pass

        def finish():
            for cp in copies:
                cp.wait()
            for cp in own:
                cp.wait()

        return start, middle, finish


def _comm_call(comm, name):
    na = comm.na

    def body(*refs):
        start, middle, finish = comm.phases(refs[0:na], refs[na:2 * na], refs[2 * na:])
        start()
        middle()
        finish()

    hbm = pl.BlockSpec(memory_space=pltpu.HBM)
    return pl.pallas_call(
        body, out_shape=comm.out_shape, in_specs=[hbm] * na, out_specs=[hbm] * na, scratch_shapes=comm.scratch,
        name=name)(*comm.arrs)


def _carried_call(body, comm, n_in, n_out, n_scr, when, *, grid, in_specs, out_specs, out_shape, scratch_shapes,
                  operands, name):
    if comm is None:
        return pl.pallas_call(body, grid=grid, in_specs=in_specs, out_specs=out_specs, out_shape=out_shape,
                              scratch_shapes=scratch_shapes, name=name)(*operands), []
    na = comm.na

    def both(*refs):
        a = n_in + na
        b = a + n_out + na
        body(*refs[0:n_in], *refs[a:a + n_out], *refs[b:b + n_scr])
        start, middle, finish = comm.phases(refs[n_in:a], refs[a + n_out:b], refs[b + n_scr:])
        first, mid, last = when()
        pl.when(first)(start)
        pl.when(mid)(middle)
        pl.when(last)(finish)

    hbm = pl.BlockSpec(memory_space=pltpu.HBM)
    res = pl.pallas_call(
        both, grid=grid, in_specs=list(in_specs) + [hbm] * na, out_specs=list(out_specs) + [hbm] * na,
        out_shape=list(out_shape) + comm.out_shape, scratch_shapes=list(scratch_shapes) + comm.scratch,
        name=name)(*operands, *comm.arrs)
    return res[0:n_out], res[n_out:]


def _ffn_w_specs():
    return [
        pl.BlockSpec((None, D, FC), lambda i, j: (j, 0, 0)),
        pl.BlockSpec((None, D, FC), lambda i, j: (j + NJ, 0, 0)),
        pl.BlockSpec((2, WO_ROWS, D), lambda i, j: (j, 0, 0)),
    ]


def _ffn_when(nt):
    def when():
        i, j = pl.program_id(0), pl.program_id(1)
        return ((i == 0) & (j == 0), (i == (3 * nt) // 4) & (j == 0), (i == nt - 1) & (j == NJ - 1))
    return when


def _ffn_fwd(x, nw, w_in, w_out, comm=None):
    t = x.shape[0]
    tm = min(FT, t)
    nt = t // tm

    def body(x_ref, nw_ref, wg_ref, wu_ref, wo3_ref, o_ref, gu_ref, xn_sc, acc_sc):
        j = pl.program_id(1)

        @pl.when(j == 0)
        def _():
            xv = x_ref[...]
            xn_sc[...] = _c(xv * _rstd(xv) * nw_ref[...])
            acc_sc[...] = jnp.zeros_like(acc_sc)

        xn = xn_sc[...]
        g = _dot(xn, wg_ref[...])
        u = _dot(xn, wu_ref[...])
        gu_ref[0] = _c(g)
        gu_ref[1] = _c(u)
        acc_sc[...] += _dot(_c(g * _sigmoid(g) * u), wo3_ref[...].reshape(FC, D))

        @pl.when(j == NJ - 1)
        def _():
            o_ref[...] = x_ref[...] + 0.5 * acc_sc[...]

    (out, gu), landed = _carried_call(
        body, comm, 5, 2, 2, _ffn_when(nt),
        grid=(nt, NJ),
        in_specs=[pl.BlockSpec((tm, D), lambda i, j: (i, 0)), pl.BlockSpec((1, D), lambda i, j: (0, 0))]
        + _ffn_w_specs(),
        out_specs=[pl.BlockSpec((tm, D), lambda i, j: (i, 0)),
                   pl.BlockSpec((None, 2, tm, FC), lambda i, j: (j, 0, i, 0))],
        out_shape=[jax.ShapeDtypeStruct((t, D), F32), jax.ShapeDtypeStruct((NJ, 2, t, FC), _MM)],
        scratch_shapes=[pltpu.VMEM((tm, D), _MM), pltpu.VMEM((tm, D), F32)],
        operands=(x, nw, w_in, w_in, w_out), name="ffn_fwd")
    return out, gu, landed


def _ffn_bwd(x, nw, w_in, w_out, gu, dy, comm=None):
    t = x.shape[0]
    nt = t // TM

    def body(x_ref, nw_ref, wg_ref, wu_ref, wo3_ref, gu_ref, dy_ref, dx_ref, xn_ref, a_ref, dh_ref, dnw_ref,
             r_sc, dyb_sc, acc_sc):
        wo = wo3_ref[...].reshape(FC, D)
        i = pl.program_id(0)
        j = pl.program_id(1)

        @pl.when(j == 0)
        def _():
            xv = x_ref[...]
            r = _rstd(xv)
            r_sc[...] = r
            xn_ref[...] = _c(xv * r * nw_ref[...])
            dyb_sc[...] = _c(0.5 * dy_ref[...])
            acc_sc[...] = jnp.zeros_like(acc_sc)

        g = gu_ref[0].astype(F32)
        u = gu_ref[1].astype(F32)
        s = _sigmoid(g)
        sl = g * s
        a_ref[...] = _c(sl * u)
        da = _dot_nt(dyb_sc[...], wo)
        dg = _c(da * u * (s * (1.0 + g * (1.0 - s))))
        du = _c(da * sl)
        dh_ref[0] = dg
        dh_ref[1] = du
        acc_sc[...] += _dot_nt(dg, wg_ref[...]) + _dot_nt(du, wu_ref[...])

        @pl.when(j == NJ - 1)
        def _():
            r = r_sc[...]
            dx, dnw = _rms_bwd(acc_sc[...], x_ref[...] * r, r, nw_ref[...])
            dx_ref[...] = dy_ref[...] + dx
            _accum(dnw_ref, dnw, i)

    return _carried_call(
        body, comm, 7, 5, 3, _ffn_when(nt),
        grid=(nt, NJ),
        in_specs=[pl.BlockSpec((TM, D), lambda i, j: (i, 0)), pl.BlockSpec((1, D), lambda i, j: (0, 0))]
        + _ffn_w_specs() + [pl.BlockSpec((None, 2, TM, FC), lambda i, j: (j, 0, i, 0)),
                            pl.BlockSpec((TM, D), lambda i, j: (i, 0))],
        out_specs=[
            pl.BlockSpec((TM, D), lambda i, j: (i, 0)),
            pl.BlockSpec((TM, D), lambda i, j: (i, 0)),
            pl.BlockSpec((None, TM, FC), lambda i, j: (j, i, 0)),
            pl.BlockSpec((None, 2, TM, FC), lambda i, j: (j, 0, i, 0)),
            pl.BlockSpec((1, D), lambda i, j: (0, 0)),
        ],
        out_shape=[
            jax.ShapeDtypeStruct((t, D), F32),
            jax.ShapeDtypeStruct((t, D), _MM),
            jax.ShapeDtypeStruct((NJ, t, FC), _MM),
            jax.ShapeDtypeStruct((NJ, 2, t, FC), _MM),
            jax.ShapeDtypeStruct((1, D), F32),
        ],
        scratch_shapes=[pltpu.VMEM((TM, 1), F32), pltpu.VMEM((TM, D), _MM), pltpu.VMEM((TM, D), F32)],
        operands=(x, nw, w_in, w_in, w_out, gu, dy), name="ffn_bwd")


def _mm_tn(a, b, bm, bn, bt, out_dtype, name):
    t, m = a.shape
    n = b.shape[1]
    nt = t // bt

    def body(a_ref, b_ref, o_ref, acc_sc):
        k = pl.program_id(2)
        _accum(acc_sc, _dot_tn(_c(a_ref[...]), _c(b_ref[...])), k)

        @pl.when(k == nt - 1)
        def _():
            o_ref[...] = acc_sc[...].astype(out_dtype)

    return pl.pallas_call(
        body,
        grid=(m // bm, n // bn, nt),
        in_specs=[pl.BlockSpec((bt, bm), lambda i, j, k: (k, i)), pl.BlockSpec((bt, bn), lambda i, j, k: (k, j))],
        out_specs=pl.BlockSpec((bm, bn), lambda i, j, k: (i, j)),
        out_shape=jax.ShapeDtypeStruct((m, n), out_dtype),
        scratch_shapes=[pltpu.VMEM((bm, bn), F32)],
        name=name,
    )(a, b)


def _mm_tn_win(xn, dh, comm=None):
    t = xn.shape[0]
    bt = min(BT_WIN, t)
    nt = t // bt

    def body(a_ref, b_ref, o_ref, acc_sc):
        k = pl.program_id(2)
        _accum(acc_sc, _dot_tn(b_ref[...], a_ref[...]), k)

        @pl.when(k == nt - 1)
        def _():
            o_ref[...] = _c(acc_sc[...])

    def when():
        h, j, k = pl.program_id(0), pl.program_id(1), pl.program_id(2)
        start = (h == 0) & (j == 0) & (k == 0)
        return start, start, (h == 1) & (j == NJ - 1) & (k == nt - 1)

    (out,), landed = _carried_call(
        body, comm, 2, 1, 1, when,
        grid=(2, NJ, nt),
        in_specs=[pl.BlockSpec((bt, D), lambda h, j, k: (k, 0)),
                  pl.BlockSpec((None, None, bt, FC), lambda h, j, k: (j, h, k, 0))],
        out_specs=[pl.BlockSpec((None, FC, D), lambda h, j, k: (h * NJ + j, 0, 0))],
        out_shape=[jax.ShapeDtypeStruct((N_DEV, FC, D), _MM)],
        scratch_shapes=[pltpu.VMEM((FC, D), F32)],
        operands=(xn, dh), name="mm_tn_win")
    return out, landed


def _mm_tn_wout(act, dy):
    t = dy.shape[0]
    bt = min(BT, t)
    nt = t // bt

    def body(a_ref, b_ref, o_ref, acc_sc):
        k = pl.program_id(1)
        _accum(acc_sc, _dot_tn(a_ref[...], _c(b_ref[...])), k)

        @pl.when(k == nt - 1)
        def _():
            o_ref[...] = _c((0.5 * acc_sc[...]).reshape(2, WO_ROWS, D))

    return pl.pallas_call(
        body,
        grid=(NJ, nt),
        in_specs=[pl.BlockSpec((None, bt, FC), lambda j, k: (j, k, 0)), pl.BlockSpec((bt, D), lambda j, k: (k, 0))],
        out_specs=pl.BlockSpec((2, WO_ROWS, D), lambda j, k: (j, 0, 0)),
        out_shape=jax.ShapeDtypeStruct((N_DEV, WO_ROWS, D), _MM),
        scratch_shapes=[pltpu.VMEM((FC, D), F32)],
        name="mm_tn_wout",
    )(act, dy)


def _loss_head(x, nw, tgt):
    t = x.shape[0]

    def body(x_ref, nw_ref, t_ref, loss_ref, dx_ref, dnw_ref):
        i = pl.program_id(0)
        xv = x_ref[...]
        r = _rstd(xv)
        xh = xv * r
        e = xh * nw_ref[...] - t_ref[...]
        part = 0.5 * jnp.sum(jnp.mean(e * e, axis=-1, keepdims=True), axis=0, keepdims=True)
        _accum(loss_ref, jnp.broadcast_to(part, (1, 128)), i)
        dx, dnw = _rms_bwd(e * (1.0 / D), xh, r, nw_ref[...])
        dx_ref[...] = dx
        _accum(dnw_ref, dnw, i)

    return pl.pallas_call(
        body,
        grid=(t // TM,),
        in_specs=[pl.BlockSpec((TM, D), lambda i: (i, 0)), pl.BlockSpec((1, D), lambda i: (0, 0)),
                  pl.BlockSpec((TM, D), lambda i: (i, 0))],
        out_specs=[pl.BlockSpec((1, 128), lambda i: (0, 0)), pl.BlockSpec((TM, D), lambda i: (i, 0)),
                   pl.BlockSpec((1, D), lambda i: (0, 0))],
        out_shape=[jax.ShapeDtypeStruct((1, 128), F32), jax.ShapeDtypeStruct((t, D), F32),
                   jax.ShapeDtypeStruct((1, D), F32)],
        name="loss_head",
    )(x, nw, tgt)


PW_F = QKV + 5 * HW
PW_B = QKV + 3 * HW + 128
AB_MAIN = QKV + HW
AB_GATES = 2 * HEADS


def _ab_proj(x1, nw, wab):
    t = x1.shape[0]

    def body(x_ref, nw_ref, w_ref, h_ref, qkv_ref, z_ref, su_ref, sv_ref, b_ref, a_ref):
        xv = x_ref[...]
        h = _c(xv * _rstd(xv) * nw_ref[...])
        h_ref[...] = h
        p = _dot(h, w_ref[...])
        qkv_ref[...] = p[:, 0:QKV]
        o = QKV
        for ref in (z_ref, su_ref, sv_ref, b_ref, a_ref):
            ref[...] = p[:, o:o + HW]
            o += HW

    row = lambda w: pl.BlockSpec((TM, w), lambda i: (i, 0))
    return pl.pallas_call(
        body,
        grid=(t // TM,),
        in_specs=[row(D), pl.BlockSpec((1, D), lambda i: (0, 0)), pl.BlockSpec((D, PW_F), lambda i: (0, 0))],
        out_specs=[row(D), row(QKV)] + [row(HW)] * 5,
        out_shape=[jax.ShapeDtypeStruct((t, D), _MM), jax.ShapeDtypeStruct((t, QKV), F32)]
        + [jax.ShapeDtypeStruct((t, HW), F32)] * 5,
        name="ab_proj",
    )(x1, nw, wab)


def _conv_rows(x, halo, cw):
    xe = jnp.concatenate([halo, x], axis=0)
    shifted = []
    c = None
    for k in range(4):
        s = 3 - k
        xs = (xe if s == 0 else pltpu.roll(xe, s, 0))[8:, :]
        shifted.append(xs)
        term = cw[k:k + 1, :] * xs
        c = term if c is None else c + term
    return c, shifted


def _head_rsq(a):
    parts = []
    for h in range(HEADS):
        ah = a[:, h * HD:(h + 1) * HD]
        r = lax.rsqrt(jnp.sum(ah * ah, axis=-1, keepdims=True) + EPS)
        parts.append(jnp.broadcast_to(r, ah.shape))
    return jnp.concatenate(parts, axis=-1)


def _head_sum(a):
    parts = []
    for h in range(HEADS):
        ah = a[:, h * HD:(h + 1) * HD]
        parts.append(jnp.broadcast_to(jnp.sum(ah, axis=-1, keepdims=True), ah.shape))
    return jnp.concatenate(parts, axis=-1)


def _softplus(x):
    return jnp.maximum(x, 0.0) + jnp.log1p(jnp.exp(-jnp.abs(x)))


def _halo_prev_spec(width, rows):
    per = TM // rows
    return pl.BlockSpec((rows, width), lambda i: (jnp.maximum(i * per - 1, 0), 0))


def _halo_next_spec(width, rows, t):
    per = TM // rows
    last = t // rows - 1
    return pl.BlockSpec((rows, width), lambda i: (jnp.minimum((i + 1) * per, last), 0))


def _dn_pre(qkv, b_rep, a_rep, cw, alog, dtb):
    t = qkv.shape[0]
    qscale = HD ** -0.5

    def body(x_ref, halo_ref, b_ref, a_ref, cw_ref, alog_ref, dt_ref, q_ref, k_ref, v_ref, beta_ref, g_ref):
        i = pl.program_id(0)
        halo = jnp.where(i == 0, 0.0, halo_ref[...])
        c, _ = _conv_rows(x_ref[...], halo, cw_ref[...])
        sc = c * _sigmoid(c)
        q = sc[:, 0:HW]
        k = sc[:, HW:2 * HW]
        q_ref[...] = q * _head_rsq(q) * qscale
        k_ref[...] = k * _head_rsq(k)
        v_ref[...] = sc[:, 2 * HW:]
        beta_ref[...] = _sigmoid(b_ref[...])
        g_ref[...] = -jnp.exp(alog_ref[...]) * _softplus(a_ref[...] + dt_ref[...])

    row = lambda w: pl.BlockSpec((TM, w), lambda i: (i, 0))
    full = lambda a: pl.BlockSpec(a.shape, lambda i: (0,) * a.ndim)
    return pl.pallas_call(
        body,
        grid=(t // TM,),
        in_specs=[row(QKV), _halo_prev_spec(QKV, 8), row(HW), row(HW), full(cw), full(alog), full(dtb)],
        out_specs=[row(HW)] * 5,
        out_shape=[jax.ShapeDtypeStruct((t, HW), F32)] * 5,
        name="dn_pre",
    )(qkv, qkv, b_rep, a_rep, cw, alog, dtb)


def _unit_lower_inv(los, eye):
    ps = [eye - lo for lo in los]
    lps = list(los)
    for _ in range(5):
        lps = [_dot(_c(lp), _c(lp)) for lp in lps]
        ps = [p + _dot(_c(p), _c(lp)) for p, lp in zip(ps, lps)]
    rs = [eye - (p + _dot3(lo, p)) for lo, p in zip(los, ps)]
    return [p + _dot(_c(p), _c(r)) for p, r in zip(ps, rs)]


def _dn_masks():
    ri = lax.broadcasted_iota(jnp.int32, (DNC, DNC), 0)
    ci = lax.broadcasted_iota(jnp.int32, (DNC, DNC), 1)
    return dict(strict=ri > ci, causal=ri >= ci, eye=(ri == ci).astype(F32),
                ltri=_c((ri >= ci).astype(F32)), upper=_c((ri <= ci).astype(F32)))


def _dn_decay(gr, mk):
    rhs = jnp.concatenate([gr, jnp.where(mk["strict"], gr[:, 0:DNC], 0.0)], axis=1)
    cs = _mask_dot(mk["ltri"], rhs)
    gc = cs[:, 0:HD]
    dm = jnp.where(mk["causal"], jnp.exp(cs[:, HD:HD + DNC]), 0.0)
    gl = jnp.sum(gr, axis=0, keepdims=True)
    return dm, jnp.exp(gc), jnp.exp(gl - gc), gl


def _dn_when(n):
    def when():
        i = pl.program_id(0)
        return (i == 0, i == n // 2, i == n - 1)
    return when


def _dn_fwd(q, k, v, beta, g, comm=None):
    t = q.shape[0]
    rows = DN_STEP * DNC
    n = t // rows

    def body(q_ref, k_ref, v_ref, b_ref, g_ref, o_ref, sall_ref, aall_ref, u_ref, w_ref, s_sc):
        i = pl.program_id(0)

        @pl.when(i == 0)
        def _():
            s_sc[...] = jnp.zeros_like(s_sc)

        mk = _dn_masks()
        idx = [(cc, h) for cc in range(DN_STEP) for h in range(HEADS)]
        at = lambda cc, h: (slice(cc * DNC, (cc + 1) * DNC), slice(h * HD, (h + 1) * HD))
        qs = [q_ref[at(*i)] for i in idx]
        ks = [k_ref[at(*i)] for i in idx]
        bs = [b_ref[at(*i)] for i in idx]
        dec = [_dn_decay(g_ref[at(*i)], mk) for i in idx]
        kbs = [k_ * b_ for k_, b_ in zip(ks, bs)]
        los = [jnp.where(mk["strict"], _dot_nt(_c(kb), _c(k_)) * d[0], 0.0) for kb, k_, d in zip(kbs, ks, dec)]
        inv = _unit_lower_inv(los, mk["eye"])
        uws = [_dot3(a, jnp.concatenate([v_ref[at(*i)] * b_, kb * d[1]], axis=1))
               for a, i, b_, kb, d in zip(inv, idx, bs, kbs, dec)]
        attn = [_c(_dot_nt(_c(q_), _c(k_)) * d[0]) for q_, k_, d in zip(qs, ks, dec)]
        for n_, (cc, h) in enumerate(idx):
            aall_ref[cc, h] = inv[n_]
            u_ref[at(cc, h)] = uws[n_][:, 0:HD]
            w_ref[at(cc, h)] = uws[n_][:, HD:]
        ss = [s_sc[h] for h in range(HEADS)]
        for cc in range(DN_STEP):
            base = cc * HEADS
            for h in range(HEADS):
                sall_ref[cc, h] = ss[h]
            ws = [_dot(_c(jnp.concatenate([uws[base + h][:, HD:], qs[base + h] * dec[base + h][1]], axis=0)),
                       _c(ss[h])) for h in range(HEADS)]
            vn = [_c(uws[base + h][:, 0:HD] - ws[h][0:DNC]) for h in range(HEADS)]
            for h in range(HEADS):
                o_ref[at(cc, h)] = ws[h][DNC:] + _dot(attn[base + h], vn[h])
            ss = [ss[h] * jnp.exp(dec[base + h][3]) + _dot_tn(_c(ks[base + h] * dec[base + h][2]), vn[h])
                  for h in range(HEADS)]
        for h in range(HEADS):
            s_sc[h] = ss[h]

    row = pl.BlockSpec((rows, HW), lambda i: (i, 0))
    return _carried_call(
        body, comm, 5, 5, 1, _dn_when(n),
        grid=(n,),
        in_specs=[row] * 5,
        out_specs=[row, pl.BlockSpec((DN_STEP, HEADS, HD, HD), lambda i: (i, 0, 0, 0)),
                   pl.BlockSpec((DN_STEP, HEADS, DNC, DNC), lambda i: (i, 0, 0, 0)), row, row],
        out_shape=[jax.ShapeDtypeStruct((t, HW), F32), jax.ShapeDtypeStruct((t // DNC, HEADS, HD, HD), F32),
                   jax.ShapeDtypeStruct((t // DNC, HEADS, DNC, DNC), F32), jax.ShapeDtypeStruct((t, HW), F32),
                   jax.ShapeDtypeStruct((t, HW), F32)],
        scratch_shapes=[pltpu.VMEM((HEADS, HD, HD), F32)],
        operands=(q, k, v, beta, g), name="dn_fwd")


def _dn_bwd(q, k, v, beta, g, sall, aall, u, w, do, comm=None):
    t = q.shape[0]
    rows = DN_STEP * DNC
    n = t // rows

    def body(q_ref, k_ref, v_ref, b_ref, g_ref, sall_ref, aall_ref, u_ref, w_ref, do_ref,
             dq_ref, dk_ref, dv_ref, db_ref, dg_ref, ds_sc):
        i = pl.program_id(0)

        @pl.when(i == 0)
        def _():
            ds_sc[...] = jnp.zeros_like(ds_sc)

        mk = _dn_masks()
        strict = mk["strict"]
        hs = range(HEADS)
        at = lambda cc, h: (slice(cc * DNC, (cc + 1) * DNC), slice(h * HD, (h + 1) * HD))
        rowsum = lambda a: jnp.sum(a, axis=-1, keepdims=True)
        dsn = [ds_sc[h] for h in hs]
        for cc in reversed(range(DN_STEP)):
            q = [q_ref[at(cc, h)] for h in hs]
            k = [k_ref[at(cc, h)] for h in hs]
            b = [b_ref[at(cc, h)] for h in hs]
            u = [u_ref[at(cc, h)] for h in hs]
            w = [w_ref[at(cc, h)] for h in hs]
            do = [do_ref[at(cc, h)] for h in hs]
            s = [sall_ref[cc, h] for h in hs]
            dec = [_dn_decay(g_ref[at(cc, h)], mk) for h in hs]
            dm, e, f = [d[0] for d in dec], [d[1] for d in dec], [d[2] for d in dec]
            egl = [jnp.exp(d[3]) for d in dec]
            kb = [k[h] * b[h] for h in hs]
            kc = [_c(k[h]) for h in hs]
            sb = [_c(s[h]) for h in hs]
            dob = [_c(do[h]) for h in hs]
            m = [_dot_nt(_c(kb[h]), kc[h]) for h in hs]
            p = [_dot_nt(_c(q[h]), kc[h]) for h in hs]
            vnb = [_c(u[h] - _dot(_c(w[h]), sb[h])) for h in hs]
            dsb = [_c(dsn[h]) for h in hs]
            dvn = [_dot_tn(_c(p[h] * dm[h]), dob[h]) + _dot(_c(k[h] * f[h]), dsb[h]) for h in hs]
            dov = [_c(jnp.concatenate([do[h], dvn[h]], axis=0)) for h in hs]
            t1 = [_dot_nt(dov[h], sb[h]) for h in hs]
            dattn = [_dot_nt(dob[h], vnb[h]) for h in hs]
            dkt = [_dot_nt(vnb[h], dsb[h]) for h in hs]
            dgl = [jnp.sum(jnp.sum(dsn[h] * s[h], axis=1, keepdims=True), axis=0, keepdims=True) * egl[h][:, 0:1]
                   for h in hs]
            dsn = [dsn[h] * egl[h] + _dot_tn(_c(jnp.concatenate([q[h] * e[h], -w[h]], axis=0)), dov[h]) for h in hs]
            dqd = [t1[h][0:DNC] for h in hs]
            dw = [-t1[h][DNC:] for h in hs]
            ab = [_dot3(aall_ref[cc, h], jnp.concatenate([dvn[h], dw[h]], axis=1), TN) for h in hs]
            dlo = [jnp.where(strict, -_dot3(ab[h], jnp.concatenate([u[h], w[h]], axis=1), NT), 0.0) for h in hs]
            dpm = [_c(jnp.concatenate([dattn[h] * dm[h], dlo[h] * dm[h]], axis=0)) for h in hs]
            t2 = [_dot(dpm[h], kc[h]) for h in hs]
            t4 = [_dot_tn(dpm[h], _c(jnp.concatenate([q[h], kb[h]], axis=0))) for h in hs]
            dff = [rowsum(dkt[h] * k[h]) * f[h][:, 0:1] for h in hs]
            de = [rowsum(dqd[h] * q[h]) + rowsum(ab[h][:, HD:] * kb[h]) for h in hs]
            dd = [(dattn[h] * p[h] + dlo[h] * m[h]) * dm[h] for h in hs]
            t3 = [_mask_dot(mk["upper"], jnp.concatenate(
                [jnp.broadcast_to(de[h] * e[h][:, 0:1] - dff[h], (DNC, HD)), dd[h]], axis=1)) for h in hs]
            for h in hs:
                dvb, dkbe = ab[h][:, 0:HD], ab[h][:, HD:]
                dkb = t2[h][DNC:] + dkbe * e[h]
                dbeta = rowsum(dkb * k[h]) + rowsum(dvb * v_ref[at(cc, h)])
                dg = (rowsum(jnp.where(strict, t3[h][:, HD:HD + DNC], 0.0)) + t3[h][:, 0:1]
                      + dgl[h] + jnp.sum(dff[h], axis=0, keepdims=True))
                dq_ref[at(cc, h)] = dqd[h] * e[h] + t2[h][0:DNC]
                dk_ref[at(cc, h)] = t4[h] + dkt[h] * f[h] + dkb * b[h]
                dv_ref[at(cc, h)] = dvb * b[h]
                db_ref[at(cc, h)] = jnp.broadcast_to(dbeta, (DNC, HD))
                dg_ref[at(cc, h)] = jnp.broadcast_to(dg, (DNC, HD))
        for h in hs:
            ds_sc[h] = dsn[h]

    row = pl.BlockSpec((rows, HW), lambda i: (n - 1 - i, 0))
    return _carried_call(
        body, comm, 10, 5, 1, _dn_when(n),
        grid=(n,),
        in_specs=[row] * 5 + [pl.BlockSpec((DN_STEP, HEADS, HD, HD), lambda i: (n - 1 - i, 0, 0, 0)),
                              pl.BlockSpec((DN_STEP, HEADS, DNC, DNC), lambda i: (n - 1 - i, 0, 0, 0)), row, row, row],
        out_specs=[row] * 5,
        out_shape=[jax.ShapeDtypeStruct((t, HW), F32)] * 5,
        scratch_shapes=[pltpu.VMEM((HEADS, HD, HD), F32)],
        operands=(q, k, v, beta, g, sall, aall, u, w, do), name="dn_bwd")


def _group_norm(a, nw):
    rs = []
    for h in range(HEADS):
        ah = a[:, h * HD:(h + 1) * HD]
        rs.append(jnp.broadcast_to(_rstd(ah), ah.shape))
    r = jnp.concatenate(rs, axis=-1)
    xh = a * r
    return xh * nw, xh, r


def _group_norm_bwd(dy, xh, r, nw):
    dxh = dy * nw
    return r * (dxh - xh * (_head_sum(dxh * xh) * (1.0 / HD)))


def _sg_mix(wt_ref, svn_b, nchunk):
    rows = []
    for cidx in range(nchunk):
        cols = []
        for g in range(HEADS):
            blk = svn_b[cidx * SGC:(cidx + 1) * SGC, g * HD:(g + 1) * HD]
            cols.append(_dot(wt_ref[g], blk))
        rows.append(jnp.concatenate(cols, axis=-1))
    return jnp.concatenate(rows, axis=0)


def _ab_out(x1, o, z, su, sv, dnw, sgnw, wtril, sgb, wout):
    t = x1.shape[0]
    nchunk = TM // SGC

    def body(x_ref, o_ref, z_ref, su_ref, sv_ref, dnw_ref, sgnw_ref, wt_ref, sgb_ref, wo_ref, x2_ref, cat_ref):
        on, _, _ = _group_norm(o_ref[...], dnw_ref[...])
        zv = z_ref[...]
        cat_ref[:, 0:HW] = _c(on * (zv * _sigmoid(zv)))
        svn, _, _ = _group_norm(_gelu(sv_ref[...]), sgnw_ref[...])
        mixed = _sg_mix(wt_ref, _c(svn), nchunk) + jnp.tile(sgb_ref[...], (nchunk, 1))
        cat_ref[:, HW:] = _c(_gelu(su_ref[...]) * mixed)
        x2_ref[...] = x_ref[...] + _dot(cat_ref[...], wo_ref[...])

    row = lambda w: pl.BlockSpec((TM, w), lambda i: (i, 0))
    full = lambda a: pl.BlockSpec(a.shape, lambda i: (0,) * a.ndim)
    return pl.pallas_call(
        body,
        grid=(t // TM,),
        in_specs=[row(D)] + [row(HW)] * 4 + [full(dnw), full(sgnw), full(wtril), full(sgb), full(wout)],
        out_specs=[row(D), row(D)],
        out_shape=[jax.ShapeDtypeStruct((t, D), F32), jax.ShapeDtypeStruct((t, D), _MM)],
        name="ab_out",
    )(x1, o, z, su, sv, dnw, sgnw, wtril, sgb, wout)


def _ab_out_bwd(dx2, o, z, su, sv, dnw, sgnw, wtril, wtril_t, sgb, wout):
    t = dx2.shape[0]
    nchunk = TM // SGC

    def body(dx_ref, o_ref, z_ref, su_ref, sv_ref, dnw_ref, sgnw_ref, wt_ref, wtt_ref, sgb_ref, wo_ref,
             do_ref, dz_ref, dsu_ref, dsv_ref, ddnw_ref, dsgnw_ref, dsgw_ref, dsgb_ref):
        i = pl.program_id(0)
        dcat = _dot_nt(_c(dx_ref[...]), wo_ref[...])
        doa = dcat[:, 0:HW]
        dob = dcat[:, HW:]
        on, oh, ro = _group_norm(o_ref[...], dnw_ref[...])
        zv = z_ref[...]
        sz = _sigmoid(zv)
        dz_ref[...] = _c(doa * on * (sz * (1.0 + zv * (1.0 - sz))))
        don = doa * (zv * sz)
        do_ref[...] = _group_norm_bwd(don, oh, ro, dnw_ref[...])
        dd = jnp.sum(don * oh, axis=0, keepdims=True)
        _accum(ddnw_ref, dd[:, 0:HD] + dd[:, HD:2 * HD] + dd[:, 2 * HD:3 * HD] + dd[:, 3 * HD:], i)
        suv = su_ref[...]
        svv = sv_ref[...]
        svg = _gelu(svv)
        svn, sh, rs = _group_norm(svg, sgnw_ref[...])
        svn_b = _c(svn)
        mixed = _sg_mix(wt_ref, svn_b, nchunk) + jnp.tile(sgb_ref[...], (nchunk, 1))
        dsu_ref[...] = _c(dob * mixed * _gelu_grad(suv))
        dmixed = dob * _gelu(suv)
        dmb = _c(dmixed)
        tri = lax.broadcasted_iota(jnp.int32, (SGC, SGC), 0) >= lax.broadcasted_iota(jnp.int32, (SGC, SGC), 1)
        lane = lax.broadcasted_iota(jnp.int32, (SGC, HD), 1)
        rows = []
        dbias = jnp.zeros((SGC, HD), F32)
        for g in range(HEADS):
            gs = slice(g * HD, (g + 1) * HD)
            dwg = jnp.zeros((SGC, SGC), F32)
            col = jnp.zeros((SGC, 1), F32)
            for cidx in range(nchunk):
                cs = slice(cidx * SGC, (cidx + 1) * SGC)
                dwg = dwg + _dot_nt(dmb[cs, gs], svn_b[cs, gs])
                col = col + jnp.sum(dmixed[cs, gs], axis=-1, keepdims=True)
            _accum(dsgw_ref.at[g], jnp.where(tri, dwg, 0.0), i)
            dbias = dbias + jnp.where(lane == g, col, 0.0)
        _accum(dsgb_ref, dbias, i)
        for cidx in range(nchunk):
            cs = slice(cidx * SGC, (cidx + 1) * SGC)
            rows.append(jnp.concatenate(
                [_dot(wtt_ref[g], dmb[cs, g * HD:(g + 1) * HD]) for g in range(HEADS)], axis=-1))
        dsvn = jnp.concatenate(rows, axis=0)
        _accum(dsgnw_ref, jnp.sum(dsvn * sh, axis=0, keepdims=True), i)
        dsv_ref[...] = _c(_group_norm_bwd(dsvn, sh, rs, sgnw_ref[...]) * _gelu_grad(svv))

    row = lambda w: pl.BlockSpec((TM, w), lambda i: (i, 0))
    full = lambda a: pl.BlockSpec(a.shape, lambda i: (0,) * a.ndim)
    const = lambda shape: pl.BlockSpec(shape, lambda i: (0,) * len(shape))
    return pl.pallas_call(
        body,
        grid=(t // TM,),
        in_specs=[row(D)] + [row(HW)] * 4 + [full(dnw), full(sgnw), full(wtril), full(wtril_t), full(sgb), full(wout)],
        out_specs=[row(HW)] * 4 + [const((1, HD)), const((1, HW)), const((HEADS, SGC, SGC)), const((SGC, HD))],
        out_shape=[jax.ShapeDtypeStruct((t, HW), F32)] + [jax.ShapeDtypeStruct((t, HW), _MM)] * 3
        + [jax.ShapeDtypeStruct((1, HD), F32), jax.ShapeDtypeStruct((1, HW), F32),
           jax.ShapeDtypeStruct((HEADS, SGC, SGC), F32), jax.ShapeDtypeStruct((SGC, HD), F32)],
        name="ab_out_bwd",
    )(dx2, o, z, su, sv, dnw, sgnw, wtril, wtril_t, sgb, wout)


def _dn_pre_bwd(qkv, b_rep, a_rep, cw, alog, dtb, dqn, dkn, dv, dbeta, dg):
    t = qkv.shape[0]
    qscale = HD ** -0.5

    def body(x_ref, halo_ref, b_ref, a_ref, cw_ref, alog_ref, dt_ref, dq_ref, dk_ref, dv_ref, dbeta_ref, dg_ref,
             dc_ref, dba_ref, dcw_ref, dalog_ref, ddt_ref):
        i = pl.program_id(0)
        halo = jnp.where(i == 0, 0.0, halo_ref[...])
        c, shifted = _conv_rows(x_ref[...], halo, cw_ref[...])
        s = _sigmoid(c)
        sc = c * s
        q = sc[:, 0:HW]
        k = sc[:, HW:2 * HW]
        rq = _head_rsq(q)
        rk = _head_rsq(k)
        qu = q * rq
        ku = k * rk
        dqn = dq_ref[...]
        dkn = dk_ref[...]
        dq = qscale * rq * (dqn - qu * _head_sum(dqn * qu))
        dk = rk * (dkn - ku * _head_sum(dkn * ku))
        dsc = jnp.concatenate([dq, dk, dv_ref[...]], axis=-1)
        dc = dsc * (s * (1.0 + c * (1.0 - s)))
        dc_ref[...] = _c(dc)
        for kk in range(4):
            _accum(dcw_ref.at[kk], jnp.sum(dc * shifted[kk], axis=0, keepdims=True), i)
        beta = _sigmoid(b_ref[...])
        dbp = dbeta_ref[...] * beta * (1.0 - beta)
        nea = -jnp.exp(alog_ref[...])
        spin = a_ref[...] + dt_ref[...]
        dgv = dg_ref[...]
        dap = dgv * nea * _sigmoid(spin)
        _accum(dalog_ref, jnp.sum(dgv * nea * _softplus(spin), axis=0, keepdims=True), i)
        _accum(ddt_ref, jnp.sum(dap, axis=0, keepdims=True), i)
        lane = lax.broadcasted_iota(jnp.int32, (TM, HD), 1)
        dba = jnp.zeros((TM, HD), F32)
        for h in range(HEADS):
            dba = dba + jnp.where(lane == h, dbp[:, h * HD:(h + 1) * HD], 0.0)
            dba = dba + jnp.where(lane == HEADS + h, dap[:, h * HD:(h + 1) * HD], 0.0)
        dba_ref[...] = _c(dba)

    row = lambda w: pl.BlockSpec((TM, w), lambda i: (i, 0))
    full = lambda a: pl.BlockSpec(a.shape, lambda i: (0,) * a.ndim)
    const = lambda shape: pl.BlockSpec(shape, lambda i: (0,) * len(shape))
    return pl.pallas_call(
        body,
        grid=(t // TM,),
        in_specs=[row(QKV), _halo_prev_spec(QKV, 8), row(HW), row(HW), full(cw), full(alog), full(dtb)] + [row(HW)] * 5,
        out_specs=[row(QKV), row(HD), const((4, 1, QKV)), const((1, HW)), const((1, HW))],
        out_shape=[jax.ShapeDtypeStruct((t, QKV), _MM), jax.ShapeDtypeStruct((t, HD), _MM),
                   jax.ShapeDtypeStruct((4, 1, QKV), F32), jax.ShapeDtypeStruct((1, HW), F32),
                   jax.ShapeDtypeStruct((1, HW), F32)],
        name="dn_pre_bwd",
    )(qkv, qkv, b_rep, a_rep, cw, alog, dtb, dqn, dkn, dv, dbeta, dg)


def _conv_bwd(dc, cw):
    t = dc.shape[0]
    nt = t // TM

    def body(dc_ref, halo_ref, cw_ref, dx_ref):
        i = pl.program_id(0)
        halo = jnp.where(i == nt - 1, 0.0, halo_ref[...].astype(F32))
        de = jnp.concatenate([dc_ref[...].astype(F32), halo], axis=0)
        cwv = cw_ref[...]
        acc = None
        for k in range(4):
            s = 3 - k
            ds = (de if s == 0 else pltpu.roll(de, TM + HALO - s, 0))[0:TM, :]
            term = cwv[k:k + 1, :] * ds
            acc = term if acc is None else acc + term
        dx_ref[...] = _c(acc)

    return pl.pallas_call(
        body,
        grid=(nt,),
        in_specs=[pl.BlockSpec((TM, QKV), lambda i: (i, 0)), _halo_next_spec(QKV, HALO, t),
                  pl.BlockSpec(cw.shape, lambda i: (0, 0))],
        out_specs=pl.BlockSpec((TM, QKV), lambda i: (i, 0)),
        out_shape=jax.ShapeDtypeStruct((t, QKV), _MM),
        name="conv_bwd",
    )(dc, dc, cw)


def _ab_proj_bwd(x1, nw, dqkv, dz, dsu, dsv, dba, wab_b, dres):
    t = x1.shape[0]

    def body(x_ref, nw_ref, dqkv_ref, dz_ref, dsu_ref, dsv_ref, dba_ref, w_ref, dres_ref, dx_ref, dcat_ref, dnw_ref):
        i = pl.program_id(0)
        dcat_ref[:, 0:QKV] = dqkv_ref[...]
        o = QKV
        for ref in (dz_ref, dsu_ref, dsv_ref):
            dcat_ref[:, o:o + HW] = ref[...]
            o += HW
        dcat_ref[:, o:o + 128] = dba_ref[...]
        dh = _dot_nt(dcat_ref[...], w_ref[...])
        xv = x_ref[...]
        r = _rstd(xv)
        dx, dnw = _rms_bwd(dh, xv * r, r, nw_ref[...])
        dx_ref[...] = dres_ref[...] + dx
        _accum(dnw_ref, dnw, i)

    row = lambda w: pl.BlockSpec((TM, w), lambda i: (i, 0))
    return pl.pallas_call(
        body,
        grid=(t // TM,),
        in_specs=[row(D), pl.BlockSpec((1, D), lambda i: (0, 0)), row(QKV), row(HW), row(HW), row(HW), row(128),
                  pl.BlockSpec((D, PW_B), lambda i: (0, 0)), row(D)],
        out_specs=[row(D), row(PW_B), pl.BlockSpec((1, D), lambda i: (0, 0))],
        out_shape=[jax.ShapeDtypeStruct((t, D), F32), jax.ShapeDtypeStruct((t, PW_B), _MM),
                   jax.ShapeDtypeStruct((1, D), F32)],
        name="ab_proj_bwd",
    )(x1, nw, dqkv, dz, dsu, dsv, dba, wab_b, dres)


def _pool_counts(i):
    pos = (lax.broadcasted_iota(jnp.int32, (TM + HALO, 1), 0) + i * TM + 1).astype(F32)
    return [1.0 / jnp.minimum(pos, float(w)) for w in POOL_WINDOWS]


def _window_sum(ext, win, back):
    r = ext.shape[0]
    s = ext
    step = 1
    while step < win:
        s = s + pltpu.roll(s, step if back else r - step, 0)
        step *= 2
    return s


def _pooled(h_ext, invc, g):
    gs = slice(g * PG, (g + 1) * PG)
    he = h_ext[:, gs]
    ws = _window_sum(he, POOL_WINDOWS[g], True)[HALO:, :]
    return ws * invc[g][0:TM, :] - he[HALO:, :]


def _pool_fwd(x1, nw, pw, scale):
    t = x1.shape[0]

    def body(x_ref, halo_ref, nw_ref, pw_ref, sc_ref, x2_ref):
        i = pl.program_id(0)
        xv = x_ref[...]
        hv = halo_ref[...]
        nwv = nw_ref[...]
        h_ext = jnp.concatenate([jnp.where(i == 0, 0.0, hv * _rstd(hv) * nwv), xv * _rstd(xv) * nwv], axis=0)
        invc = _pool_counts(i)
        outs = [_dot(_c(_pooled(h_ext, invc, g)), pw_ref[g]) for g in range(4)]
        x2_ref[...] = xv + jnp.concatenate(outs, axis=-1) * sc_ref[...]

    return pl.pallas_call(
        body,
        grid=(t // TM,),
        in_specs=[pl.BlockSpec((TM, D), lambda i: (i, 0)), _halo_prev_spec(D, HALO),
                  pl.BlockSpec((1, D), lambda i: (0, 0)), pl.BlockSpec((4, PG, PG), lambda i: (0, 0, 0)),
                  pl.BlockSpec((1, D), lambda i: (0, 0))],
        out_specs=pl.BlockSpec((TM, D), lambda i: (i, 0)),
        out_shape=jax.ShapeDtypeStruct((t, D), F32),
        name="pool_fwd",
    )(x1, x1, nw, pw, scale)


def _pool_bwd(x1, nw, pw, scale, dx2):
    t = x1.shape[0]
    nt = t // TM

    def body(x_ref, halo_ref, nw_ref, pw_ref, sc_ref, dx2_ref, dnext_ref, dx_ref, dnw_ref, dpw_ref, dsc_ref):
        i = pl.program_id(0)
        xv = x_ref[...]
        hv = halo_ref[...]
        nwv = nw_ref[...]
        r = _rstd(xv)
        xh = xv * r
        h_ext = jnp.concatenate([jnp.where(i == 0, 0.0, hv * _rstd(hv) * nwv), xh * nwv], axis=0)
        invc = _pool_counts(i)
        dyv = dx2_ref[...]
        dout_ext = jnp.concatenate([dyv, jnp.where(i == nt - 1, 0.0, dnext_ref[...])], axis=0) * sc_ref[...]
        dh_parts = []
        dsc_parts = []
        for g in range(4):
            gs = slice(g * PG, (g + 1) * PG)
            pooled_b = _c(_pooled(h_ext, invc, g))
            dout_b = _c(dout_ext[:, gs])
            dsc_parts.append(jnp.sum(dyv[:, gs] * _dot(pooled_b, pw_ref[g]), axis=0, keepdims=True))
            _accum(dpw_ref.at[g], _dot_tn(pooled_b, dout_b[0:TM, :]), i)
            dpool_ext = _dot_nt(dout_b, pw_ref[g])
            lead = _window_sum(dpool_ext * invc[g], POOL_WINDOWS[g], False)[0:TM, :]
            dh_parts.append(lead - dpool_ext[0:TM, :])
        _accum(dsc_ref, jnp.concatenate(dsc_parts, axis=-1), i)
        dx, dnw = _rms_bwd(jnp.concatenate(dh_parts, axis=-1), xh, r, nwv)
        dx_ref[...] = dyv + dx
        _accum(dnw_ref, dnw, i)

    vec = pl.BlockSpec((1, D), lambda i: (0, 0))
    return pl.pallas_call(
        body,
        grid=(nt,),
        in_specs=[pl.BlockSpec((TM, D), lambda i: (i, 0)), _halo_prev_spec(D, HALO), vec,
                  pl.BlockSpec((4, PG, PG), lambda i: (0, 0, 0)), vec,
                  pl.BlockSpec((TM, D), lambda i: (i, 0)), _halo_next_spec(D, HALO, t)],
        out_specs=[pl.BlockSpec((TM, D), lambda i: (i, 0)), vec, pl.BlockSpec((4, PG, PG), lambda i: (0, 0, 0)), vec],
        out_shape=[jax.ShapeDtypeStruct((t, D), F32), jax.ShapeDtypeStruct((1, D), F32),
                   jax.ShapeDtypeStruct((4, PG, PG), F32), jax.ShapeDtypeStruct((1, D), F32)],
        name="pool_bwd",
    )(x1, x1, nw, pw, scale, dx2, dx2)


def _adamw(lands, w, m, v, rb, name):
    nl, nr = w.shape[0], w.shape[1]
    rest = w.shape[2:]
    ns = lands[0].shape[0]
    zeros = (0,) * len(rest)

    def body(*refs):
        l_refs = refs[0:nl]
        w_ref, m_ref, v_ref, g_ref, d_ref, m2_ref, v2_ref = refs[nl:]
        for l in range(nl):
            g = l_refs[l][0].astype(F32)
            for s in range(1, ns):
                g = g + l_refs[l][s].astype(F32)
            m2 = ADAM_B1 * m_ref[l] + (1.0 - ADAM_B1) * g
            v2 = ADAM_B2 * v_ref[l] + (1.0 - ADAM_B2) * (g * g)
            m_hat = m2 / (1.0 - ADAM_B1 ** ADAM_STEP)
            v_hat = v2 / (1.0 - ADAM_B2 ** ADAM_STEP)
            g_ref[l] = g
            d_ref[l] = -ADAM_LR * (m_hat / (jnp.sqrt(v_hat) + ADAM_EPS) + ADAM_WD * w_ref[l])
            m2_ref[l] = m2
            v2_ref[l] = v2

    lspec = pl.BlockSpec((ns, rb) + rest, lambda r: (0, r) + zeros)
    wspec = pl.BlockSpec((nl, rb) + rest, lambda r: (0, r) + zeros)
    return pl.pallas_call(
        body,
        grid=(nr // rb,),
        in_specs=[lspec] * nl + [wspec] * 3,
        out_specs=[wspec] * 4,
        out_shape=[jax.ShapeDtypeStruct(w.shape, F32)] * 4,
        name=name,
    )(*lands, w, m, v)


WEIGHT_ORDER = ("ffn_norm1", "ffn1_w_in", "ffn1_w_out", "mix_norm", "ffn_norm2", "ffn2_w_in", "ffn2_w_out", "ab_w_in",
                "dn_conv_w", "dn_a_log", "dn_dt_bias", "dn_out_norm", "sg_norm", "sg_w", "sg_b", "ab_w_out", "pool_w",
                "pool_scale", "final_norm")
R_SMALL = 88
SMALL_ROWS = (
    ("ffn_norm1", (2, D), 2), ("mix_norm", (2, D), 2), ("ffn_norm2", (2, D), 2), ("final_norm", (D,), 1),
    ("sg_w", (1, 4, SGC, SGC), 64), ("sg_norm", (1, 4, HD), 1), ("sg_b", (1, 4, SGC), 1), ("dn_out_norm", (1, HD), 1),
    ("dn_a_log", (1, 4), 1), ("dn_dt_bias", (1, 4), 1), ("pool_scale", (1, D), 1), ("dn_conv_w", (1, 4, QKV), 8),
)
SMALL_SHARDED = ("pool_scale", "dn_conv_w")


def _rows_of(a, rows):
    if a.shape[-1] == QKV:
        return jnp.pad(a.reshape(4, QKV), ((0, 0), (0, 2 * ROW - QKV))).reshape(8, ROW)
    n = _numel(a.shape)
    if n % ROW == 0:
        return a.reshape(n // ROW, ROW)
    return jnp.pad(a.reshape(1, n), ((0, 0), (0, ROW - n)))


def _from_rows(r, shape):
    if shape[-1] == QKV:
        return r.reshape(4, 2 * ROW)[:, 0:QKV].reshape(shape)
    n = _numel(shape)
    if n % ROW == 0:
        return r.reshape(shape)
    return r[:, 0:n].reshape(shape)


def _pack_small(vals):
    parts = [(_rows_of(vals[n].astype(F32), r) if n in vals else jnp.zeros((r, ROW), F32)) for n, _, r in SMALL_ROWS]
    used = sum(r for _, _, r in SMALL_ROWS)
    return jnp.concatenate(parts + [jnp.zeros((R_SMALL - used, ROW), F32)], axis=0)


def _unpack_small(packed):
    out, o = {}, 0
    for n, shape, r in SMALL_ROWS:
        out[n] = _from_rows(packed[o:o + r], shape)
        o += r
    return out


def _pack_small_shard(ps, cw):
    return jnp.concatenate([
        jnp.pad(ps, ((0, 0), (0, ROW - D // N_DEV))), jnp.pad(cw[0], ((0, 0), (0, ROW - QKV // N_DEV))),
        jnp.zeros((3, ROW), F32)], axis=0)


def _mixer_weights(g_in, g_out, g_small, small):
    w = {}
    wi = jnp.transpose(g_in, (1, 0, 2)).reshape(D, AB_IN)
    gates = wi[:, AB_MAIN:AB_MAIN + AB_GATES]
    main = [wi[:, 0:AB_MAIN], wi[:, AB_MAIN + AB_GATES:AB_IN]]
    w["wab_f"] = jnp.concatenate(
        main + [jnp.repeat(gates[:, 0:HEADS], HD, axis=1), jnp.repeat(gates[:, HEADS:], HD, axis=1)], axis=1)
    w["wab_b"] = jnp.concatenate(main + [gates, jnp.zeros((D, 128 - AB_GATES), wi.dtype)], axis=1)
    w["cw"] = jnp.transpose(g_small[:, 1:5, 0:QKV // N_DEV], (1, 0, 2)).reshape(4, QKV)
    w["ps"] = g_small[:, 0, 0:D // N_DEV].reshape(1, D)
    w["alog"] = jnp.repeat(small["dn_a_log"][0].astype(F32), HD).reshape(1, HW)
    w["dtb"] = jnp.repeat(small["dn_dt_bias"][0].astype(F32), HD).reshape(1, HW)
    w["dnw"] = jnp.tile(small["dn_out_norm"][0].astype(F32), HEADS).reshape(1, HW)
    w["sgnw"] = small["sg_norm"][0].astype(F32).reshape(1, HW)
    tri = jnp.tril(jnp.ones((SGC, SGC), dtype=bool))
    wt = jnp.where(tri, small["sg_w"][0].astype(F32), 0.0)
    w["wtril"] = _c(wt)
    w["wtril_t"] = _c(jnp.transpose(wt, (0, 2, 1)))
    w["sgb"] = jnp.repeat(jnp.transpose(small["sg_b"][0].astype(F32)), HD, axis=1)
    w["wout_ab"] = g_out.reshape(D, D)
    return w


def kernel(x, ffn_norm1, ffn1_w_in, ffn1_w_out, mix_norm, ffn_norm2, ffn2_w_in, ffn2_w_out, ab_w_in, dn_conv_w, dn_a_log, dn_dt_bias, dn_out_norm, sg_norm, sg_w, sg_b, ab_w_out, pool_w, pool_scale, final_norm, loss_target, m_ffn_norm1, m_ffn1_w_in, m_ffn1_w_out, m_mix_norm, m_ffn_norm2, m_ffn2_w_in, m_ffn2_w_out, m_ab_w_in, m_dn_conv_w, m_dn_a_log, m_dn_dt_bias, m_dn_out_norm, m_sg_norm, m_sg_w, m_sg_b, m_ab_w_out, m_pool_w, m_pool_scale, m_final_norm, v_ffn_norm1, v_ffn1_w_in, v_ffn1_w_out, v_mix_norm, v_ffn_norm2, v_ffn2_w_in, v_ffn2_w_out, v_ab_w_in, v_dn_conv_w, v_dn_a_log, v_dn_dt_bias, v_dn_out_norm, v_sg_norm, v_sg_w, v_sg_b, v_ab_w_out, v_pool_w, v_pool_scale, v_final_norm):
    wl = dict(ffn_norm1=ffn_norm1, mix_norm=mix_norm, ffn_norm2=ffn_norm2, dn_a_log=dn_a_log, dn_dt_bias=dn_dt_bias,
              dn_out_norm=dn_out_norm, sg_norm=sg_norm, sg_w=sg_w, sg_b=sg_b, final_norm=final_norm)
    ml = dict(ffn_norm1=m_ffn_norm1, mix_norm=m_mix_norm, ffn_norm2=m_ffn_norm2, dn_a_log=m_dn_a_log,
              dn_dt_bias=m_dn_dt_bias, dn_out_norm=m_dn_out_norm, sg_norm=m_sg_norm, sg_w=m_sg_w, sg_b=m_sg_b,
              final_norm=m_final_norm)
    vl = dict(ffn_norm1=v_ffn_norm1, mix_norm=v_mix_norm, ffn_norm2=v_ffn_norm2, dn_a_log=v_dn_a_log,
              dn_dt_bias=v_dn_dt_bias, dn_out_norm=v_dn_out_norm, sg_norm=v_sg_norm, sg_w=v_sg_w, sg_b=v_sg_b,
              final_norm=v_final_norm)
    row = lambda a: a.reshape(1, -1).astype(F32)
    n1 = [row(ffn_norm1[l]) for l in range(2)]
    n2 = [row(ffn_norm2[l]) for l in range(2)]
    mix = [row(mix_norm[l]) for l in range(2)]
    s_in = {(f, l): _c(wf[l]) for f, wf in enumerate((ffn1_w_in, ffn2_w_in)) for l in range(2)}
    s_out = {(f, l): _c(wf[l]) for f, wf in enumerate((ffn1_w_out, ffn2_w_out)) for l in range(2)}
    xs, tgt = x[0], loss_target[0]

    wi00, wo00 = _comm_call(_Comm("gather", [s_in[0, 0], s_out[0, 0]]), "gather_first")
    x01, gu00, (g_abin, g_about, g_small, wi10) = _ffn_fwd(
        xs, n1[0], wi00, wo00,
        comm=_Comm("gather", [_c(ab_w_in[0]), _c(ab_w_out[0]), _pack_small_shard(pool_scale, dn_conv_w), s_in[1, 0]]))
    w = _mixer_weights(g_abin, g_about, g_small, wl)
    h, qkv, z, su, sv, b_rep, a_rep = _ab_proj(x01, mix[0], w["wab_f"])
    qn, kn, v, beta, g = _dn_pre(qkv, b_rep, a_rep, w["cw"], w["alog"], w["dtb"])
    (o, sall, aall, dn_u, dn_w), (wo10, g_pw) = _dn_fwd(
        qn, kn, v, beta, g, comm=_Comm("gather", [s_out[1, 0], _c(pool_w[0])]))
    pw = jnp.transpose(g_pw, (1, 0, 2, 3)).reshape(4, PG, PG)
    x02, cat = _ab_out(x01, o, z, su, sv, w["dnw"], w["sgnw"], w["wtril"], w["sgb"], w["wout_ab"])
    x10, gu10, (wi01, wo01) = _ffn_fwd(x02, n2[0], wi10, wo10, comm=_Comm("gather", [s_in[0, 1], s_out[0, 1]]))
    x11, gu01, (wi11, wo11) = _ffn_fwd(x10, n1[1], wi01, wo01, comm=_Comm("gather", [s_in[1, 1], s_out[1, 1]]))
    x12 = _pool_fwd(x11, mix[1], pw, w["ps"])
    x13, gu11, _ = _ffn_fwd(x12, n2[1], wi11, wo11)
    loss_local, dx, d_fn = _loss_head(x13, row(final_norm), tgt)

    bt = min(BT, xs.shape[0])

    def ffn_b(xin, nw, w_in, w_out, gu, dy, comm=None):
        (dxn, xn, act, dh, dnw), landed = _ffn_bwd(xin, nw, w_in, w_out, gu, dy, comm)
        return dxn, dnw, (xn, act, dh), landed

    def ffn_g(kept, dy):
        return [_mm_tn_win(kept[0], kept[2])[0], _mm_tn_wout(kept[1], dy)]

    dy = dx
    dx, d_n2_1, kept, _ = ffn_b(x12, n2[1], wi11, wo11, gu11, dy)
    g11 = ffn_g(kept, dy)
    dx, d_mix_1, d_pw, d_ps = _pool_bwd(x11, mix[1], pw, w["ps"], dx)
    d_pw_sh = _c(jnp.transpose(d_pw.reshape(4, N_DEV, PG // N_DEV, PG), (1, 0, 2, 3)))
    dy = dx
    dx, d_n1_1, kept, land11 = ffn_b(x10, n1[1], wi01, wo01, gu01, dy, _Comm("exchange", g11))
    g01 = ffn_g(kept, dy)
    dy = dx
    dx, d_n2_0, kept, land01 = ffn_b(x02, n2[0], wi10, wo10, gu10, dy, _Comm("exchange", g01 + [d_pw_sh]))
    g10 = ffn_g(kept, dy)
    do, dz, dsu, dsv, d_dnw, d_sgnw, d_sgw, d_sgb = _ab_out_bwd(
        dx, o, z, su, sv, w["dnw"], w["sgnw"], w["wtril"], w["wtril_t"], w["sgb"], w["wout_ab"])
    d_about = _mm_tn(cat, dx, D, D, bt, _MM, "mm_tn_about").reshape(N_DEV, D // N_DEV, D)
    (dqn, dkn, dv, dbeta, dg), _ = _dn_bwd(qn, kn, v, beta, g, sall, aall, dn_u, dn_w, do)
    dc, dba, d_cw, d_alog, d_dtb = _dn_pre_bwd(qkv, b_rep, a_rep, w["cw"], w["alog"], w["dtb"], dqn, dkn, dv, dbeta, dg)
    dqkv = _conv_bwd(dc, w["cw"])
    dx, dcat, d_mix_0 = _ab_proj_bwd(x01, mix[0], dqkv, dz, dsu, dsv, dba, w["wab_b"], dx)
    d_wab = _mm_tn(h, dcat, D, 640, bt, _MM, "mm_tn_abin")
    rest = PW_B - 128
    d_abin = jnp.concatenate([d_wab[:, 0:AB_MAIN], d_wab[:, rest:rest + AB_GATES], d_wab[:, AB_MAIN:rest]], axis=1)
    d_abin_sh = jnp.transpose(d_abin.reshape(D, N_DEV, AB_IN // N_DEV), (1, 0, 2))
    dy = dx
    grad_x, d_n1_0, kept, land_mid = ffn_b(xs, n1[0], wi00, wo00, gu00, dy,
                                           _Comm("exchange", g10 + [d_abin_sh, d_about]))
    land10, land_ab = land_mid[0:2], land_mid[2:4]

    g_small = {
        "ffn_norm1": jnp.concatenate([d_n1_0, d_n1_1], axis=0),
        "mix_norm": jnp.concatenate([d_mix_0, d_mix_1], axis=0),
        "ffn_norm2": jnp.concatenate([d_n2_0, d_n2_1], axis=0),
        "dn_conv_w": d_cw.reshape(1, 4, QKV),
        "dn_a_log": d_alog[:, ::HD],
        "dn_dt_bias": d_dtb[:, ::HD],
        "dn_out_norm": d_dnw,
        "sg_norm": d_sgnw.reshape(1, HEADS, HD),
        "sg_w": d_sgw[None],
        "sg_b": jnp.transpose(d_sgb[:, 0:HEADS])[None],
        "pool_scale": d_ps,
        "final_norm": d_fn.reshape(D),
    }
    g00_out = _mm_tn_wout(kept[1], dy)
    g00_in, (land00_out, land_small) = _mm_tn_win(
        kept[0], kept[2], comm=_Comm("exchange", [g00_out], repl=[_pack_small(g_small)]))
    (land00_in,) = _comm_call(_Comm("exchange", [g00_in]), "exchange_last")

    res = {}
    tr = lambda a: jnp.swapaxes(a, 1, 2)
    res["ffn1_w_in"] = [tr(a) for a in _adamw([land00_in, land01[0]], tr(ffn1_w_in), tr(m_ffn1_w_in), tr(v_ffn1_w_in),
                                              176, "adamw_w_in")]
    res["ffn2_w_in"] = [tr(a) for a in _adamw([land10[0], land11[0]], tr(ffn2_w_in), tr(m_ffn2_w_in), tr(v_ffn2_w_in),
                                              176, "adamw_w_in")]
    res["ffn1_w_out"] = _adamw([land00_out, land01[1]], ffn1_w_out, m_ffn1_w_out, v_ffn1_w_out, 176, "adamw_w_out")
    res["ffn2_w_out"] = _adamw([land10[1], land11[1]], ffn2_w_out, m_ffn2_w_out, v_ffn2_w_out, 176, "adamw_w_out")
    res["ab_w_in"] = _adamw([land_ab[0]], ab_w_in, m_ab_w_in, v_ab_w_in, 256, "adamw_ab_w_in")
    res["ab_w_out"] = _adamw([land_ab[1]], ab_w_out, m_ab_w_out, v_ab_w_out, D // N_DEV, "adamw_ab_w_out")
    res["pool_w"] = _adamw([land01[2]], pool_w, m_pool_w, v_pool_w, 4, "adamw_pool_w")
    sm = _adamw([land_small], _pack_small(wl)[None], _pack_small(ml)[None], _pack_small(vl)[None], R_SMALL,
                "adamw_replicated")
    sm = [_unpack_small(a[0]) for a in sm]
    for n in wl:
        res[n] = [d[n] for d in sm]
    me = 4 * lax.axis_index("x") + 2 * lax.axis_index("y") + lax.axis_index("c")
    g_ps = lax.dynamic_slice(sm[0]["pool_scale"], (0, me * (D // N_DEV)), (1, D // N_DEV))
    g_cw = lax.dynamic_slice(sm[0]["dn_conv_w"], (0, 0, me * (QKV // N_DEV)), (1, 4, QKV // N_DEV))
    s2 = _adamw([_pack_small_shard(g_ps, g_cw)[None]], _pack_small_shard(pool_scale, dn_conv_w)[None],
                _pack_small_shard(m_pool_scale, m_dn_conv_w)[None], _pack_small_shard(v_pool_scale, v_dn_conv_w)[None],
                8, "adamw_small_sharded")
    res["pool_scale"] = [a[0, 0:1, 0:D // N_DEV] for a in s2]
    res["dn_conv_w"] = [a[0, 1:5, 0:QKV // N_DEV][None] for a in s2]

    loss = lax.psum(loss_local[0, 0], ("x", "y", "c"))
    result = [loss, grad_x[None]]
    for i in range(4):
        result += [res[n][i] for n in WEIGHT_ORDER]
    return tuple(result)
```

```python
import jax
import jax.numpy as jnp
from jax import lax
from jax.experimental import pallas as pl
from jax.experimental.pallas import tpu as pltpu

F32 = jnp.float32
_MM = jnp.bfloat16

D = 1024
FF = 2816
EPS = 1e-6
HEADS = 4
HD = 128
DNC = 64
DN_STEP = 8
SGC = 128
QKV = 3 * HEADS * HD
HW = HEADS * HD
POOL_WINDOWS = (2, 4, 8, 16)
PG = D // 4
HALO = 16
N_DEV = 8
AB_IN = 3080
ROW = 1024

TM = 512
BT = 2048
BT_WIN = 4096
FT = 512
FWD_CHUNKS = 2
BWD_CHUNKS = 1
FC = 704
NJ = FF // FC
WO_ROWS = FF // N_DEV

ADAM_LR, ADAM_B1, ADAM_B2, ADAM_EPS, ADAM_WD, ADAM_STEP = 0.001, 0.9, 0.999, 1e-08, 0.01, 10

MESH_T = pl.DeviceIdType.MESH
NN = (((1,), (0,)), ((), ()))
NT = (((1,), (1,)), ((), ()))
TN = (((0,), (0,)), ((), ()))


def _c(a):
    return a.astype(_MM)


def _dg(a, b, dims):
    return lax.dot_general(a, b, dims, preferred_element_type=F32)


def _dot(a, b):
    return _dg(a, b, NN)


def _dot_nt(a, b):
    return _dg(a, b, NT)


def _dot_tn(a, b):
    return _dg(a, b, TN)


def _split2(a):
    hi = _c(a)
    return hi, _c(a - hi.astype(F32))


def _dot3(a, b, dims=NN):
    ah, al = _split2(a)
    bh, bl = _split2(b)
    return _dg(ah, bh, dims) + (_dg(ah, bl, dims) + _dg(al, bh, dims))


def _mask_dot(mask, x):
    x1 = _c(x)
    r = x - x1.astype(F32)
    x2 = _c(r)
    x3 = _c(r - x2.astype(F32))
    return _dot(mask, x1) + (_dot(mask, x2) + _dot(mask, x3))


def _sigmoid(x):
    return jax.nn.sigmoid(x)


def _gelu(x):
    return 0.5 * x * (1.0 + lax.erf(x * 0.7071067811865476))


def _gelu_grad(x):
    return 0.5 * (1.0 + lax.erf(x * 0.7071067811865476)) + x * jnp.exp(-0.5 * x * x) * 0.3989422804014327


def _accum(ref, val, step):
    @pl.when(step == 0)
    def _():
        ref[...] = val

    @pl.when(step > 0)
    def _():
        ref[...] += val


def _rstd(x):
    return lax.rsqrt(jnp.mean(x * x, axis=-1, keepdims=True) + EPS)


def _rms_bwd(dy, xhat, r, nw):
    dnw = jnp.sum(dy * xhat, axis=0, keepdims=True)
    dxh = dy * nw
    dx = r * (dxh - xhat * jnp.mean(dxh * xhat, axis=-1, keepdims=True))
    return dx, dnw


def _numel(shape):
    n = 1
    for s in shape:
        n *= s
    return n


def _peer(k, x, y, c):
    px = 1 - x if k & 4 else x
    py = 1 - y if k & 2 else y
    pc = 1 - c if k & 1 else c
    return px, py, pc


class _Comm:
    def __init__(self, kind, arrs, repl=()):
        self.kind = kind
        self.ns = len(arrs)
        self.arrs = list(arrs) + list(repl)
        self.na = len(self.arrs)

    @property
    def out_shape(self):
        out = []
        for i, a in enumerate(self.arrs):
            lead = (N_DEV,) if (self.kind == "gather" or i >= self.ns) else ()
            out.append(jax.ShapeDtypeStruct(lead + a.shape, a.dtype))
        return out

    @property
    def scratch(self):
        return [pltpu.SemaphoreType.DMA((7 * self.na,)), pltpu.SemaphoreType.DMA((7 * self.na,)),
                pltpu.SemaphoreType.DMA((self.na,))]

    def phases(self, ins, outs, sems):
        send_sems, recv_sems, local_sems = sems
        na = self.na
        x, y, c = lax.axis_index("x"), lax.axis_index("y"), lax.axis_index("c")
        if self.kind == "gather":
            me, sibling = (x, y, c), (x, y, 1 - c)
            chips = [(1 - x, y), (x, 1 - y), (1 - x, 1 - y)]

            def slot(a, px, py, pc):
                return outs[a].at[4 * px + 2 * py + pc]

            def copy(a, k, block, to, src=None):
                return pltpu.make_async_remote_copy(
                    src_ref=slot(a, *block) if src is None else src, dst_ref=slot(a, *block),
                    send_sem=send_sems.at[7 * a + k], recv_sem=recv_sems.at[7 * a + k],
                    device_id=to, device_id_type=MESH_T)

            mine = [pltpu.make_async_copy(ins[a], slot(a, *me), local_sems.at[a]) for a in range(na)]
            first, passed = [], []
            for a in range(na):
                first.append(copy(a, 0, me, sibling, src=ins[a]))
                first += [copy(a, 1 + j, me, (*chip, c), src=ins[a]) for j, chip in enumerate(chips)]
                passed += [copy(a, 4 + j, (*chip, c), sibling) for j, chip in enumerate(chips)]

            def start():
                for cp in mine + first:
                    cp.start()

            def middle():
                for a in range(na):
                    for j, chip in enumerate(chips):
                        copy(a, 1 + j, (*chip, c), me).wait_recv()
                        passed[3 * a + j].start()

            def finish():
                for a in range(na):
                    copy(a, 0, sibling, me).wait_recv()
                    for j, chip in enumerate(chips):
                        copy(a, 4 + j, (*chip, 1 - c), me).wait_recv()
                for cp in first + passed:
                    cp.wait_send()
                for cp in mine:
                    cp.wait()

            return start, middle, finish

        me = 4 * x + 2 * y + c
        ns = self.ns
        own = [pltpu.make_async_copy(ins[a].at[me] if a < ns else ins[a], outs[a].at[me], local_sems.at[a])
               for a in range(na)]
        copies = []
        for k in range(1, N_DEV):
            px, py, pc = _peer(k, x, y, c)
            peer = 4 * px + 2 * py + pc
            for a in range(na):
                copies.append(pltpu.make_async_remote_copy(
                    src_ref=ins[a].at[peer] if a < ns else ins[a], dst_ref=outs[a].at[me],
                    send_sem=send_sems.at[na * (k - 1) + a], recv_sem=recv_sems.at[na * (k - 1) + a],
                    device_id=(px, py, pc), device_id_type=MESH_T))

        def start():
            for cp in own + copies:
                cp.start()

        def middle():
            pass

        def finish():
            for cp in copies:
                cp.wait()
            for cp in own:
                cp.wait()

        return start, middle, finish


def _comm_call(comm, name):
    na = comm.na

    def body(*refs):
        start, middle, finish = comm.phases(refs[0:na], refs[na:2 * na], refs[2 * na:])
        start()
        middle()
        finish()

    hbm = pl.BlockSpec(memory_space=pltpu.HBM)
    return pl.pallas_call(
        body, out_shape=comm.out_shape, in_specs=[hbm] * na, out_specs=[hbm] * na, scratch_shapes=comm.scratch,
        name=name)(*comm.arrs)


def _carried_call(body, comm, n_in, n_out, n_scr, when, *, grid, in_specs, out_specs, out_shape, scratch_shapes,
                  operands, name):
    if comm is None:
        return pl.pallas_call(body, grid=grid, in_specs=in_specs, out_specs=out_specs, out_shape=out_shape,
                              scratch_shapes=scratch_shapes, name=name)(*operands), []
    na = comm.na

    def both(*refs):
        a = n_in + na
        b = a + n_out + na
        body(*refs[0:n_in], *refs[a:a + n_out], *refs[b:b + n_scr])
        start, middle, finish = comm.phases(refs[n_in:a], refs[a + n_out:b], refs[b + n_scr:])
        first, mid, last = when()
        pl.when(first)(start)
        pl.when(mid)(middle)
        pl.when(last)(finish)

    hbm = pl.BlockSpec(memory_space=pltpu.HBM)
    res = pl.pallas_call(
        both, grid=grid, in_specs=list(in_specs) + [hbm] * na, out_specs=list(out_specs) + [hbm] * na,
        out_shape=list(out_shape) + comm.out_shape, scratch_shapes=list(scratch_shapes) + comm.scratch,
        name=name)(*operands, *comm.arrs)
    return res[0:n_out], res[n_out:]


def _ffn_fwd(x, nw, w_in, w_out, comm=None):
    t = x.shape[0]
    tm = min(FT, t)
    nt = t // tm

    nj = NJ // FWD_CHUNKS

    def body(x_ref, nw_ref, wg_ref, wu_ref, wo3_ref, o_ref, gu_ref, xn_sc, acc_sc):
        j = pl.program_id(1)

        @pl.when(j == 0)
        def _():
            xv = x_ref[...]
            xn_sc[...] = _c(xv * _rstd(xv) * nw_ref[...])
            acc_sc[...] = jnp.zeros_like(acc_sc)

        xn = xn_sc[...]
        part = None
        for cc in range(FWD_CHUNKS):
            g = _dot(xn, wg_ref[cc])
            u = _dot(xn, wu_ref[cc])
            gu_ref[cc, 0] = _c(g)
            gu_ref[cc, 1] = _c(u)
            p = _dot(_c(g * _sigmoid(g) * u), wo3_ref[2 * cc:2 * cc + 2].reshape(FC, D))
            part = p if part is None else part + p
        acc_sc[...] += part

        @pl.when(j == nj - 1)
        def _():
            o_ref[...] = x_ref[...] + 0.5 * acc_sc[...]

    def when():
        i, j = pl.program_id(0), pl.program_id(1)
        return ((i == 0) & (j == 0), (i == (3 * nt) // 4) & (j == 0), (i == nt - 1) & (j == nj - 1))

    (out, gu), landed = _carried_call(
        body, comm, 5, 2, 2, when,
        grid=(nt, nj),
        in_specs=[pl.BlockSpec((tm, D), lambda i, j: (i, 0)), pl.BlockSpec((1, D), lambda i, j: (0, 0)),
                  pl.BlockSpec((FWD_CHUNKS, D, FC), lambda i, j: (j, 0, 0)),
                  pl.BlockSpec((FWD_CHUNKS, D, FC), lambda i, j: (j + nj, 0, 0)),
                  pl.BlockSpec((2 * FWD_CHUNKS, WO_ROWS, D), lambda i, j: (j, 0, 0))],
        out_specs=[pl.BlockSpec((tm, D), lambda i, j: (i, 0)),
                   pl.BlockSpec((FWD_CHUNKS, 2, tm, FC), lambda i, j: (j, 0, i, 0))],
        out_shape=[jax.ShapeDtypeStruct((t, D), F32), jax.ShapeDtypeStruct((NJ, 2, t, FC), _MM)],
        scratch_shapes=[pltpu.VMEM((tm, D), _MM), pltpu.VMEM((tm, D), F32)],
        operands=(x, nw, w_in, w_in, w_out), name="ffn_fwd")
    return out, gu, landed


def _ffn_bwd(x, nw, w_in, w_out, gu, dy, comm=None):
    t = x.shape[0]
    nt = t // TM
    nc = BWD_CHUNKS
    nj = NJ // nc

    def body(x_ref, nw_ref, wg_ref, wu_ref, wo3_ref, gu_ref, dy_ref, dx_ref, xn_ref, a_ref, dh_ref, dnw_ref,
             r_sc, dyb_sc, acc_sc):
        i = pl.program_id(0)
        j = pl.program_id(1)

        @pl.when(j == 0)
        def _():
            xv = x_ref[...]
            r = _rstd(xv)
            r_sc[...] = r
            xn_ref[...] = _c(xv * r * nw_ref[...])
            dyb_sc[...] = _c(0.5 * dy_ref[...])
            acc_sc[...] = jnp.zeros_like(acc_sc)

        part = None
        for cc in range(nc):
            g = gu_ref[cc, 0].astype(F32)
            u = gu_ref[cc, 1].astype(F32)
            s = _sigmoid(g)
            sl = g * s
            a_ref[cc] = _c(sl * u)
            da = _dot_nt(dyb_sc[...], wo3_ref[2 * cc:2 * cc + 2].reshape(FC, D))
            dg = _c(da * u * (s * (1.0 + g * (1.0 - s))))
            du = _c(da * sl)
            dh_ref[cc, 0] = dg
            dh_ref[cc, 1] = du
            p = _dot_nt(dg, wg_ref[cc]) + _dot_nt(du, wu_ref[cc])
            part = p if part is None else part + p
        acc_sc[...] += part

        @pl.when(j == nj - 1)
        def _():
            r = r_sc[...]
            dx, dnw = _rms_bwd(acc_sc[...], x_ref[...] * r, r, nw_ref[...])
            dx_ref[...] = dy_ref[...] + dx
            _accum(dnw_ref, dnw, i)

    def when():
        i, j = pl.program_id(0), pl.program_id(1)
        return ((i == 0) & (j == 0), (i == (3 * nt) // 4) & (j == 0), (i == nt - 1) & (j == nj - 1))

    return _carried_call(
        body, comm, 7, 5, 3, when,
        grid=(nt, nj),
        in_specs=[pl.BlockSpec((TM, D), lambda i, j: (i, 0)), pl.BlockSpec((1, D), lambda i, j: (0, 0)),
                  pl.BlockSpec((nc, D, FC), lambda i, j: (j, 0, 0)),
                  pl.BlockSpec((nc, D, FC), lambda i, j: (j + nj, 0, 0)),
                  pl.BlockSpec((2 * nc, WO_ROWS, D), lambda i, j: (j, 0, 0)),
                  pl.BlockSpec((nc, 2, TM, FC), lambda i, j: (j, 0, i, 0)),
                  pl.BlockSpec((TM, D), lambda i, j: (i, 0))],
        out_specs=[
            pl.BlockSpec((TM, D), lambda i, j: (i, 0)),
            pl.BlockSpec((TM, D), lambda i, j: (i, 0)),
            pl.BlockSpec((nc, TM, FC), lambda i, j: (j, i, 0)),
            pl.BlockSpec((nc, 2, TM, FC), lambda i, j: (j, 0, i, 0)),
            pl.BlockSpec((1, D), lambda i, j: (0, 0)),
        ],
        out_shape=[
            jax.ShapeDtypeStruct((t, D), F32),
            jax.ShapeDtypeStruct((t, D), _MM),
            jax.ShapeDtypeStruct((NJ, t, FC), _MM),
            jax.ShapeDtypeStruct((NJ, 2, t, FC), _MM),
            jax.ShapeDtypeStruct((1, D), F32),
        ],
        scratch_shapes=[pltpu.VMEM((TM, 1), F32), pltpu.VMEM((TM, D), _MM), pltpu.VMEM((TM, D), F32)],
        operands=(x, nw, w_in, w_in, w_out, gu, dy), name="ffn_bwd")


def _mm_tn(a, b, bm, bn, bt, out_dtype, name):
    t, m = a.shape
    n = b.shape[1]
    nt = t // bt

    def body(a_ref, b_ref, o_ref, acc_sc):
        k = pl.program_id(2)
        _accum(acc_sc, _dot_tn(_c(a_ref[...]), _c(b_ref[...])), k)

        @pl.when(k == nt - 1)
        def _():
            o_ref[...] = acc_sc[...].astype(out_dtype)

    return pl.pallas_call(
        body,
        grid=(m // bm, n // bn, nt),
        in_specs=[pl.BlockSpec((bt, bm), lambda i, j, k: (k, i)), pl.BlockSpec((bt, bn), lambda i, j, k: (k, j))],
        out_specs=pl.BlockSpec((bm, bn), lambda i, j, k: (i, j)),
        out_shape=jax.ShapeDtypeStruct((m, n), out_dtype),
        scratch_shapes=[pltpu.VMEM((bm, bn), F32)],
        name=name,
    )(a, b)


def _mm_tn_win(xn, dh, comm=None):
    t = xn.shape[0]
    bt = min(BT_WIN, t)
    nt = t // bt

    def body(a_ref, b_ref, o_ref, acc_sc):
        k = pl.program_id(2)
        _accum(acc_sc, _dot_tn(b_ref[...], a_ref[...]), k)

        @pl.when(k == nt - 1)
        def _():
            o_ref[...] = _c(acc_sc[...])

    def when():
        h, j, k = pl.program_id(0), pl.program_id(1), pl.program_id(2)
        start = (h == 0) & (j == 0) & (k == 0)
        return start, start, (h == 1) & (j == NJ - 1) & (k == nt - 1)

    (out,), landed = _carried_call(
        body, comm, 2, 1, 1, when,
        grid=(2, NJ, nt),
        in_specs=[pl.BlockSpec((bt, D), lambda h, j, k: (k, 0)),
                  pl.BlockSpec((None, None, bt, FC), lambda h, j, k: (j, h, k, 0))],
        out_specs=[pl.BlockSpec((None, FC, D), lambda h, j, k: (h * NJ + j, 0, 0))],
        out_shape=[jax.ShapeDtypeStruct((N_DEV, FC, D), _MM)],
        scratch_shapes=[pltpu.VMEM((FC, D), F32)],
        operands=(xn, dh), name="mm_tn_win")
    return out, landed


def _mm_tn_wout(act, dy):
    t = dy.shape[0]
    bt = min(BT, t)
    nt = t // bt

    def body(a_ref, b_ref, o_ref, acc_sc):
        k = pl.program_id(1)
        _accum(acc_sc, _dot_tn(a_ref[...], _c(b_ref[...])), k)

        @pl.when(k == nt - 1)
        def _():
            o_ref[...] = _c((0.5 * acc_sc[...]).reshape(2, WO_ROWS, D))

    return pl.pallas_call(
        body,
        grid=(NJ, nt),
        in_specs=[pl.BlockSpec((None, bt, FC), lambda j, k: (j, k, 0)), pl.BlockSpec((bt, D), lambda j, k: (k, 0))],
        out_specs=pl.BlockSpec((2, WO_ROWS, D), lambda j, k: (j, 0, 0)),
        out_shape=jax.ShapeDtypeStruct((N_DEV, WO_ROWS, D), _MM),
        scratch_shapes=[pltpu.VMEM((FC, D), F32)],
        name="mm_tn_wout",
    )(act, dy)


def _loss_head(x, nw, tgt):
    t = x.shape[0]

    def body(x_ref, nw_ref, t_ref, loss_ref, dx_ref, dnw_ref):
        i = pl.program_id(0)
        xv = x_ref[...]
        r = _rstd(xv)
        xh = xv * r
        e = xh * nw_ref[...] - t_ref[...]
        part = 0.5 * jnp.sum(jnp.mean(e * e, axis=-1, keepdims=True), axis=0, keepdims=True)
        _accum(loss_ref, jnp.broadcast_to(part, (1, 128)), i)
        dx, dnw = _rms_bwd(e * (1.0 / D), xh, r, nw_ref[...])
        dx_ref[...] = dx
        _accum(dnw_ref, dnw, i)

    return pl.pallas_call(
        body,
        grid=(t // TM,),
        in_specs=[pl.BlockSpec((TM, D), lambda i: (i, 0)), pl.BlockSpec((1, D), lambda i: (0, 0)),
                  pl.BlockSpec((TM, D), lambda i: (i, 0))],
        out_specs=[pl.BlockSpec((1, 128), lambda i: (0, 0)), pl.BlockSpec((TM, D), lambda i: (i, 0)),
                   pl.BlockSpec((1, D), lambda i: (0, 0))],
        out_shape=[jax.ShapeDtypeStruct((1, 128), F32), jax.ShapeDtypeStruct((t, D), F32),
                   jax.ShapeDtypeStruct((1, D), F32)],
        name="loss_head",
    )(x, nw, tgt)


PW_F = QKV + 5 * HW
PW_B = QKV + 3 * HW + 128
AB_MAIN = QKV + HW
AB_GATES = 2 * HEADS


def _ab_proj(x1, nw, wab):
    t = x1.shape[0]

    def body(x_ref, nw_ref, w_ref, h_ref, qkv_ref, z_ref, su_ref, sv_ref, b_ref, a_ref):
        xv = x_ref[...]
        h = _c(xv * _rstd(xv) * nw_ref[...])
        h_ref[...] = h
        p = _dot(h, w_ref[...])
        qkv_ref[...] = p[:, 0:QKV]
        o = QKV
        for ref in (z_ref, su_ref, sv_ref, b_ref, a_ref):
            ref[...] = p[:, o:o + HW]
            o += HW

    row = lambda w: pl.BlockSpec((TM, w), lambda i: (i, 0))
    return pl.pallas_call(
        body,
        grid=(t // TM,),
        in_specs=[row(D), pl.BlockSpec((1, D), lambda i: (0, 0)), pl.BlockSpec((D, PW_F), lambda i: (0, 0))],
        out_specs=[row(D), row(QKV)] + [row(HW)] * 5,
        out_shape=[jax.ShapeDtypeStruct((t, D), _MM), jax.ShapeDtypeStruct((t, QKV), F32)]
        + [jax.ShapeDtypeStruct((t, HW), F32)] * 5,
        name="ab_proj",
    )(x1, nw, wab)


def _conv_rows(x, halo, cw):
    xe = jnp.concatenate([halo, x], axis=0)
    shifted = []
    c = None
    for k in range(4):
        s = 3 - k
        xs = (xe if s == 0 else pltpu.roll(xe, s, 0))[8:, :]
        shifted.append(xs)
        term = cw[k:k + 1, :] * xs
        c = term if c is None else c + term
    return c, shifted


def _head_rsq(a):
    parts = []
    for h in range(HEADS):
        ah = a[:, h * HD:(h + 1) * HD]
        r = lax.rsqrt(jnp.sum(ah * ah, axis=-1, keepdims=True) + EPS)
        parts.append(jnp.broadcast_to(r, ah.shape))
    return jnp.concatenate(parts, axis=-1)


def _head_sum(a):
    parts = []
    for h in range(HEADS):
        ah = a[:, h * HD:(h + 1) * HD]
        parts.append(jnp.broadcast_to(jnp.sum(ah, axis=-1, keepdims=True), ah.shape))
    return jnp.concatenate(parts, axis=-1)


def _softplus(x):
    return jnp.maximum(x, 0.0) + jnp.log1p(jnp.exp(-jnp.abs(x)))


def _halo_prev_spec(width, rows):
    per = TM // rows
    return pl.BlockSpec((rows, width), lambda i: (jnp.maximum(i * per - 1, 0), 0))


def _halo_next_spec(width, rows, t):
    per = TM // rows
    last = t // rows - 1
    return pl.BlockSpec((rows, width), lambda i: (jnp.minimum((i + 1) * per, last), 0))


def _dn_pre(qkv, b_rep, a_rep, cw, alog, dtb):
    t = qkv.shape[0]
    qscale = HD ** -0.5

    def body(x_ref, halo_ref, b_ref, a_ref, cw_ref, alog_ref, dt_ref, q_ref, k_ref, v_ref, beta_ref, g_ref):
        i = pl.program_id(0)
        halo = jnp.where(i == 0, 0.0, halo_ref[...])
        c, _ = _conv_rows(x_ref[...], halo, cw_ref[...])
        sc = c * _sigmoid(c)
        q = sc[:, 0:HW]
        k = sc[:, HW:2 * HW]
        q_ref[...] = q * _head_rsq(q) * qscale
        k_ref[...] = k * _head_rsq(k)
        v_ref[...] = sc[:, 2 * HW:]
        beta_ref[...] = _sigmoid(b_ref[...])
        g_ref[...] = -jnp.exp(alog_ref[...]) * _softplus(a_ref[...] + dt_ref[...])

    row = lambda w: pl.BlockSpec((TM, w), lambda i: (i, 0))
    full = lambda a: pl.BlockSpec(a.shape, lambda i: (0,) * a.ndim)
    return pl.pallas_call(
        body,
        grid=(t // TM,),
        in_specs=[row(QKV), _halo_prev_spec(QKV, 8), row(HW), row(HW), full(cw), full(alog), full(dtb)],
        out_specs=[row(HW)] * 5,
        out_shape=[jax.ShapeDtypeStruct((t, HW), F32)] * 5,
        name="dn_pre",
    )(qkv, qkv, b_rep, a_rep, cw, alog, dtb)


def _unit_lower_inv(los, eye):
    ps = [eye - lo for lo in los]
    lps = list(los)
    for _ in range(5):
        lps = [_dot(_c(lp), _c(lp)) for lp in lps]
        ps = [p + _dot(_c(p), _c(lp)) for p, lp in zip(ps, lps)]
    rs = [eye - (p + _dot3(lo, p)) for lo, p in zip(los, ps)]
    return [p + _dot(_c(p), _c(r)) for p, r in zip(ps, rs)]


def _dn_masks():
    ri = lax.broadcasted_iota(jnp.int32, (DNC, DNC), 0)
    ci = lax.broadcasted_iota(jnp.int32, (DNC, DNC), 1)
    return dict(strict=ri > ci, causal=ri >= ci, eye=(ri == ci).astype(F32),
                ltri=_c((ri >= ci).astype(F32)), upper=_c((ri <= ci).astype(F32)))


def _dn_decay(gr, mk):
    rhs = jnp.concatenate([gr, jnp.where(mk["strict"], gr[:, 0:DNC], 0.0)], axis=1)
    cs = _mask_dot(mk["ltri"], rhs)
    gc = cs[:, 0:HD]
    dm = jnp.where(mk["causal"], jnp.exp(cs[:, HD:HD + DNC]), 0.0)
    gl = jnp.sum(gr, axis=0, keepdims=True)
    return dm, jnp.exp(gc), jnp.exp(gl - gc), gl


def _dn_when(n):
    def when():
        i = pl.program_id(0)
        return (i == 0, i == n // 2, i == n - 1)
    return when


def _dn_fwd(q, k, v, beta, g, comm=None):
    t = q.shape[0]
    rows = DN_STEP * DNC
    n = t // rows

    def body(q_ref, k_ref, v_ref, b_ref, g_ref, o_ref, sall_ref, aall_ref, u_ref, w_ref, s_sc):
        i = pl.program_id(0)

        @pl.when(i == 0)
        def _():
            s_sc[...] = jnp.zeros_like(s_sc)

        mk = _dn_masks()
        idx = [(cc, h) for cc in range(DN_STEP) for h in range(HEADS)]
        at = lambda cc, h: (slice(cc * DNC, (cc + 1) * DNC), slice(h * HD, (h + 1) * HD))
        qs = [q_ref[at(*i)] for i in idx]
        ks = [k_ref[at(*i)] for i in idx]
        bs = [b_ref[at(*i)] for i in idx]
        dec = [_dn_decay(g_ref[at(*i)], mk) for i in idx]
        kbs = [k_ * b_ for k_, b_ in zip(ks, bs)]
        los = [jnp.where(mk["strict"], _dot_nt(_c(kb), _c(k_)) * d[0], 0.0) for kb, k_, d in zip(kbs, ks, dec)]
        inv = _unit_lower_inv(los, mk["eye"])
        uws = [_dot3(a, jnp.concatenate([v_ref[at(*i)] * b_, kb * d[1]], axis=1))
               for a, i, b_, kb, d in zip(inv, idx, bs, kbs, dec)]
        attn = [_c(_dot_nt(_c(q_), _c(k_)) * d[0]) for q_, k_, d in zip(qs, ks, dec)]
        for n_, (cc, h) in enumerate(idx):
            aall_ref[cc, h] = inv[n_]
            u_ref[at(cc, h)] = uws[n_][:, 0:HD]
            w_ref[at(cc, h)] = uws[n_][:, HD:]
        ss = [s_sc[h] for h in range(HEADS)]
        for cc in range(DN_STEP):
            base = cc * HEADS
            for h in range(HEADS):
                sall_ref[cc, h] = ss[h]
            ws = [_dot(_c(jnp.concatenate([uws[base + h][:, HD:], qs[base + h] * dec[base + h][1]], axis=0)),
                       _c(ss[h])) for h in range(HEADS)]
            vn = [_c(uws[base + h][:, 0:HD] - ws[h][0:DNC]) for h in range(HEADS)]
            for h in range(HEADS):
                o_ref[at(cc, h)] = ws[h][DNC:] + _dot(attn[base + h], vn[h])
            ss = [ss[h] * jnp.exp(dec[base + h][3]) + _dot_tn(_c(ks[base + h] * dec[base + h][2]), vn[h])
                  for h in range(HEADS)]
        for h in range(HEADS):
            s_sc[h] = ss[h]

    row = pl.BlockSpec((rows, HW), lambda i: (i, 0))
    return _carried_call(
        body, comm, 5, 5, 1, _dn_when(n),
        grid=(n,),
        in_specs=[row] * 5,
        out_specs=[row, pl.BlockSpec((DN_STEP, HEADS, HD, HD), lambda i: (i, 0, 0, 0)),
                   pl.BlockSpec((DN_STEP, HEADS, DNC, DNC), lambda i: (i, 0, 0, 0)), row, row],
        out_shape=[jax.ShapeDtypeStruct((t, HW), F32), jax.ShapeDtypeStruct((t // DNC, HEADS, HD, HD), F32),
                   jax.ShapeDtypeStruct((t // DNC, HEADS, DNC, DNC), F32), jax.ShapeDtypeStruct((t, HW), F32),
                   jax.ShapeDtypeStruct((t, HW), F32)],
        scratch_shapes=[pltpu.VMEM((HEADS, HD, HD), F32)],
        operands=(q, k, v, beta, g), name="dn_fwd")


def _dn_bwd(q, k, v, beta, g, sall, aall, u, w, do, comm=None):
    t = q.shape[0]
    rows = DN_STEP * DNC
    n = t // rows

    def body(q_ref, k_ref, v_ref, b_ref, g_ref, sall_ref, aall_ref, u_ref, w_ref, do_ref,
             dq_ref, dk_ref, dv_ref, db_ref, dg_ref, ds_sc):
        i = pl.program_id(0)

        @pl.when(i == 0)
        def _():
            ds_sc[...] = jnp.zeros_like(ds_sc)

        mk = _dn_masks()
        strict = mk["strict"]
        hs = range(HEADS)
        at = lambda cc, h: (slice(cc * DNC, (cc + 1) * DNC), slice(h * HD, (h + 1) * HD))
        rowsum = lambda a: jnp.sum(a, axis=-1, keepdims=True)
        dsn = [ds_sc[h] for h in hs]
        for cc in reversed(range(DN_STEP)):
            q = [q_ref[at(cc, h)] for h in hs]
            k = [k_ref[at(cc, h)] for h in hs]
            b = [b_ref[at(cc, h)] for h in hs]
            u = [u_ref[at(cc, h)] for h in hs]
            w = [w_ref[at(cc, h)] for h in hs]
            do = [do_ref[at(cc, h)] for h in hs]
            s = [sall_ref[cc, h] for h in hs]
            dec = [_dn_decay(g_ref[at(cc, h)], mk) for h in hs]
            dm, e, f = [d[0] for d in dec], [d[1] for d in dec], [d[2] for d in dec]
            egl = [jnp.exp(d[3]) for d in dec]
            kb = [k[h] * b[h] for h in hs]
            kc = [_c(k[h]) for h in hs]
            sb = [_c(s[h]) for h in hs]
            dob = [_c(do[h]) for h in hs]
            m = [_dot_nt(_c(kb[h]), kc[h]) for h in hs]
            p = [_dot_nt(_c(q[h]), kc[h]) for h in hs]
            vnb = [_c(u[h] - _dot(_c(w[h]), sb[h])) for h in hs]
            dsb = [_c(dsn[h]) for h in hs]
            dvn = [_dot_tn(_c(p[h] * dm[h]), dob[h]) + _dot(_c(k[h] * f[h]), dsb[h]) for h in hs]
            dov = [_c(jnp.concatenate([do[h], dvn[h]], axis=0)) for h in hs]
            t1 = [_dot_nt(dov[h], sb[h]) for h in hs]
            dattn = [_dot_nt(dob[h], vnb[h]) for h in hs]
            dkt = [_dot_nt(vnb[h], dsb[h]) for h in hs]
            dgl = [jnp.sum(jnp.sum(dsn[h] * s[h], axis=1, keepdims=True), axis=0, keepdims=True) * egl[h][:, 0:1]
                   for h in hs]
            dsn = [dsn[h] * egl[h] + _dot_tn(_c(jnp.concatenate([q[h] * e[h], -w[h]], axis=0)), dov[h]) for h in hs]
            dqd = [t1[h][0:DNC] for h in hs]
            dw = [-t1[h][DNC:] for h in hs]
            ab = [_dot3(aall_ref[cc, h], jnp.concatenate([dvn[h], dw[h]], axis=1), TN) for h in hs]
            dlo = [jnp.where(strict, -_dot3(ab[h], jnp.concatenate([u[h], w[h]], axis=1), NT), 0.0) for h in hs]
            dpm = [_c(jnp.concatenate([dattn[h] * dm[h], dlo[h] * dm[h]], axis=0)) for h in hs]
            t2 = [_dot(dpm[h], kc[h]) for h in hs]
            t4 = [_dot_tn(dpm[h], _c(jnp.concatenate([q[h], kb[h]], axis=0))) for h in hs]
            dff = [rowsum(dkt[h] * k[h]) * f[h][:, 0:1] for h in hs]
            de = [rowsum(dqd[h] * q[h]) + rowsum(ab[h][:, HD:] * kb[h]) for h in hs]
            dd = [(dattn[h] * p[h] + dlo[h] * m[h]) * dm[h] for h in hs]
            t3 = [_mask_dot(mk["upper"], jnp.concatenate(
                [jnp.broadcast_to(de[h] * e[h][:, 0:1] - dff[h], (DNC, HD)), dd[h]], axis=1)) for h in hs]
            for h in hs:
                dvb, dkbe = ab[h][:, 0:HD], ab[h][:, HD:]
                dkb = t2[h][DNC:] + dkbe * e[h]
                dbeta = rowsum(dkb * k[h]) + rowsum(dvb * v_ref[at(cc, h)])
                dg = (rowsum(jnp.where(strict, t3[h][:, HD:HD + DNC], 0.0)) + t3[h][:, 0:1]
                      + dgl[h] + jnp.sum(dff[h], axis=0, keepdims=True))
                dq_ref[at(cc, h)] = dqd[h] * e[h] + t2[h][0:DNC]
                dk_ref[at(cc, h)] = t4[h] + dkt[h] * f[h] + dkb * b[h]
                dv_ref[at(cc, h)] = dvb * b[h]
                db_ref[at(cc, h)] = jnp.broadcast_to(dbeta, (DNC, HD))
                dg_ref[at(cc, h)] = jnp.broadcast_to(dg, (DNC, HD))
        for h in hs:
            ds_sc[h] = dsn[h]

    row = pl.BlockSpec((rows, HW), lambda i: (n - 1 - i, 0))
    return _carried_call(
        body, comm, 10, 5, 1, _dn_when(n),
        grid=(n,),
        in_specs=[row] * 5 + [pl.BlockSpec((DN_STEP, HEADS, HD, HD), lambda i: (n - 1 - i, 0, 0, 0)),
                              pl.BlockSpec((DN_STEP, HEADS, DNC, DNC), lambda i: (n - 1 - i, 0, 0, 0)), row, row, row],
        out_specs=[row] * 5,
        out_shape=[jax.ShapeDtypeStruct((t, HW), F32)] * 5,
        scratch_shapes=[pltpu.VMEM((HEADS, HD, HD), F32)],
        operands=(q, k, v, beta, g, sall, aall, u, w, do), name="dn_bwd")


def _group_norm(a, nw):
    rs = []
    for h in range(HEADS):
        ah = a[:, h * HD:(h + 1) * HD]
        rs.append(jnp.broadcast_to(_rstd(ah), ah.shape))
    r = jnp.concatenate(rs, axis=-1)
    xh = a * r
    return xh * nw, xh, r


def _group_norm_bwd(dy, xh, r, nw):
    dxh = dy * nw
    return r * (dxh - xh * (_head_sum(dxh * xh) * (1.0 / HD)))


def _sg_mix(wt_ref, svn_b, nchunk):
    rows = []
    for cidx in range(nchunk):
        cols = []
        for g in range(HEADS):
            blk = svn_b[cidx * SGC:(cidx + 1) * SGC, g * HD:(g + 1) * HD]
            cols.append(_dot(wt_ref[g], blk))
        rows.append(jnp.concatenate(cols, axis=-1))
    return jnp.concatenate(rows, axis=0)


def _ab_out(x1, o, z, su, sv, dnw, sgnw, wtril, sgb, wout):
    t = x1.shape[0]
    nchunk = TM // SGC

    def body(x_ref, o_ref, z_ref, su_ref, sv_ref, dnw_ref, sgnw_ref, wt_ref, sgb_ref, wo_ref, x2_ref, cat_ref):
        on, _, _ = _group_norm(o_ref[...], dnw_ref[...])
        zv = z_ref[...]
        cat_ref[:, 0:HW] = _c(on * (zv * _sigmoid(zv)))
        svn, _, _ = _group_norm(_gelu(sv_ref[...]), sgnw_ref[...])
        mixed = _sg_mix(wt_ref, _c(svn), nchunk) + jnp.tile(sgb_ref[...], (nchunk, 1))
        cat_ref[:, HW:] = _c(_gelu(su_ref[...]) * mixed)
        x2_ref[...] = x_ref[...] + _dot(cat_ref[...], wo_ref[...])

    row = lambda w: pl.BlockSpec((TM, w), lambda i: (i, 0))
    full = lambda a: pl.BlockSpec(a.shape, lambda i: (0,) * a.ndim)
    return pl.pallas_call(
        body,
        grid=(t // TM,),
        in_specs=[row(D)] + [row(HW)] * 4 + [full(dnw), full(sgnw), full(wtril), full(sgb), full(wout)],
        out_specs=[row(D), row(D)],
        out_shape=[jax.ShapeDtypeStruct((t, D), F32), jax.ShapeDtypeStruct((t, D), _MM)],
        name="ab_out",
    )(x1, o, z, su, sv, dnw, sgnw, wtril, sgb, wout)


def _ab_out_bwd(dx2, o, z, su, sv, dnw, sgnw, wtril, wtril_t, sgb, wout):
    t = dx2.shape[0]
    nchunk = TM // SGC

    def body(dx_ref, o_ref, z_ref, su_ref, sv_ref, dnw_ref, sgnw_ref, wt_ref, wtt_ref, sgb_ref, wo_ref,
             do_ref, dz_ref, dsu_ref, dsv_ref, ddnw_ref, dsgnw_ref, dsgw_ref, dsgb_ref):
        i = pl.program_id(0)
        dcat = _dot_nt(_c(dx_ref[...]), wo_ref[...])
        doa = dcat[:, 0:HW]
        dob = dcat[:, HW:]
        on, oh, ro = _group_norm(o_ref[...], dnw_ref[...])
        zv = z_ref[...]
        sz = _sigmoid(zv)
        dz_ref[...] = _c(doa * on * (sz * (1.0 + zv * (1.0 - sz))))
        don = doa * (zv * sz)
        do_ref[...] = _group_norm_bwd(don, oh, ro, dnw_ref[...])
        dd = jnp.sum(don * oh, axis=0, keepdims=True)
        _accum(ddnw_ref, dd[:, 0:HD] + dd[:, HD:2 * HD] + dd[:, 2 * HD:3 * HD] + dd[:, 3 * HD:], i)
        suv = su_ref[...]
        svv = sv_ref[...]
        svg = _gelu(svv)
        svn, sh, rs = _group_norm(svg, sgnw_ref[...])
        svn_b = _c(svn)
        mixed = _sg_mix(wt_ref, svn_b, nchunk) + jnp.tile(sgb_ref[...], (nchunk, 1))
        dsu_ref[...] = _c(dob * mixed * _gelu_grad(suv))
        dmixed = dob * _gelu(suv)
        dmb = _c(dmixed)
        tri = lax.broadcasted_iota(jnp.int32, (SGC, SGC), 0) >= lax.broadcasted_iota(jnp.int32, (SGC, SGC), 1)
        lane = lax.broadcasted_iota(jnp.int32, (SGC, HD), 1)
        rows = []
        dbias = jnp.zeros((SGC, HD), F32)
        for g in range(HEADS):
            gs = slice(g * HD, (g + 1) * HD)
            dwg = jnp.zeros((SGC, SGC), F32)
            col = jnp.zeros((SGC, 1), F32)
            for cidx in range(nchunk):
                cs = slice(cidx * SGC, (cidx + 1) * SGC)
                dwg = dwg + _dot_nt(dmb[cs, gs], svn_b[cs, gs])
                col = col + jnp.sum(dmixed[cs, gs], axis=-1, keepdims=True)
            _accum(dsgw_ref.at[g], jnp.where(tri, dwg, 0.0), i)
            dbias = dbias + jnp.where(lane == g, col, 0.0)
        _accum(dsgb_ref, dbias, i)
        for cidx in range(nchunk):
            cs = slice(cidx * SGC, (cidx + 1) * SGC)
            rows.append(jnp.concatenate(
                [_dot(wtt_ref[g], dmb[cs, g * HD:(g + 1) * HD]) for g in range(HEADS)], axis=-1))
        dsvn = jnp.concatenate(rows, axis=0)
        _accum(dsgnw_ref, jnp.sum(dsvn * sh, axis=0, keepdims=True), i)
        dsv_ref[...] = _c(_group_norm_bwd(dsvn, sh, rs, sgnw_ref[...]) * _gelu_grad(svv))

    row = lambda w: pl.BlockSpec((TM, w), lambda i: (i, 0))
    full = lambda a: pl.BlockSpec(a.shape, lambda i: (0,) * a.ndim)
    const = lambda shape: pl.BlockSpec(shape, lambda i: (0,) * len(shape))
    return pl.pallas_call(
        body,
        grid=(t // TM,),
        in_specs=[row(D)] + [row(HW)] * 4 + [full(dnw), full(sgnw), full(wtril), full(wtril_t), full(sgb), full(wout)],
        out_specs=[row(HW)] * 4 + [const((1, HD)), const((1, HW)), const((HEADS, SGC, SGC)), const((SGC, HD))],
        out_shape=[jax.ShapeDtypeStruct((t, HW), F32)] + [jax.ShapeDtypeStruct((t, HW), _MM)] * 3
        + [jax.ShapeDtypeStruct((1, HD), F32), jax.ShapeDtypeStruct((1, HW), F32),
           jax.ShapeDtypeStruct((HEADS, SGC, SGC), F32), jax.ShapeDtypeStruct((SGC, HD), F32)],
        name="ab_out_bwd",
    )(dx2, o, z, su, sv, dnw, sgnw, wtril, wtril_t, sgb, wout)


def _dn_pre_bwd(qkv, b_rep, a_rep, cw, alog, dtb, dqn, dkn, dv, dbeta, dg):
    t = qkv.shape[0]
    qscale = HD ** -0.5

    def body(x_ref, halo_ref, b_ref, a_ref, cw_ref, alog_ref, dt_ref, dq_ref, dk_ref, dv_ref, dbeta_ref, dg_ref,
             dc_ref, dba_ref, dcw_ref, dalog_ref, ddt_ref):
        i = pl.program_id(0)
        halo = jnp.where(i == 0, 0.0, halo_ref[...])
        c, shifted = _conv_rows(x_ref[...], halo, cw_ref[...])
        s = _sigmoid(c)
        sc = c * s
        q = sc[:, 0:HW]
        k = sc[:, HW:2 * HW]
        rq = _head_rsq(q)
        rk = _head_rsq(k)
        qu = q * rq
        ku = k * rk
        dqn = dq_ref[...]
        dkn = dk_ref[...]
        dq = qscale * rq * (dqn - qu * _head_sum(dqn * qu))
        dk = rk * (dkn - ku * _head_sum(dkn * ku))
        dsc = jnp.concatenate([dq, dk, dv_ref[...]], axis=-1)
        dc = dsc * (s * (1.0 + c * (1.0 - s)))
        dc_ref[...] = _c(dc)
        for kk in range(4):
            _accum(dcw_ref.at[kk], jnp.sum(dc * shifted[kk], axis=0, keepdims=True), i)
        beta = _sigmoid(b_ref[...])
        dbp = dbeta_ref[...] * beta * (1.0 - beta)
        nea = -jnp.exp(alog_ref[...])
        spin = a_ref[...] + dt_ref[...]
        dgv = dg_ref[...]
        dap = dgv * nea * _sigmoid(spin)
        _accum(dalog_ref, jnp.sum(dgv * nea * _softplus(spin), axis=0, keepdims=True), i)
        _accum(ddt_ref, jnp.sum(dap, axis=0, keepdims=True), i)
        lane = lax.broadcasted_iota(jnp.int32, (TM, HD), 1)
        dba = jnp.zeros((TM, HD), F32)
        for h in range(HEADS):
            dba = dba + jnp.where(lane == h, dbp[:, h * HD:(h + 1) * HD], 0.0)
            dba = dba + jnp.where(lane == HEADS + h, dap[:, h * HD:(h + 1) * HD], 0.0)
        dba_ref[...] = _c(dba)

    row = lambda w: pl.BlockSpec((TM, w), lambda i: (i, 0))
    full = lambda a: pl.BlockSpec(a.shape, lambda i: (0,) * a.ndim)
    const = lambda shape: pl.BlockSpec(shape, lambda i: (0,) * len(shape))
    return pl.pallas_call(
        body,
        grid=(t // TM,),
        in_specs=[row(QKV), _halo_prev_spec(QKV, 8), row(HW), row(HW), full(cw), full(alog), full(dtb)] + [row(HW)] * 5,
        out_specs=[row(QKV), row(HD), const((4, 1, QKV)), const((1, HW)), const((1, HW))],
        out_shape=[jax.ShapeDtypeStruct((t, QKV), _MM), jax.ShapeDtypeStruct((t, HD), _MM),
                   jax.ShapeDtypeStruct((4, 1, QKV), F32), jax.ShapeDtypeStruct((1, HW), F32),
                   jax.ShapeDtypeStruct((1, HW), F32)],
        name="dn_pre_bwd",
    )(qkv, qkv, b_rep, a_rep, cw, alog, dtb, dqn, dkn, dv, dbeta, dg)


def _conv_bwd(dc, cw):
    t = dc.shape[0]
    nt = t // TM

    def body(dc_ref, halo_ref, cw_ref, dx_ref):
        i = pl.program_id(0)
        halo = jnp.where(i == nt - 1, 0.0, halo_ref[...].astype(F32))
        de = jnp.concatenate([dc_ref[...].astype(F32), halo], axis=0)
        cwv = cw_ref[...]
        acc = None
        for k in range(4):
            s = 3 - k
            ds = (de if s == 0 else pltpu.roll(de, TM + HALO - s, 0))[0:TM, :]
            term = cwv[k:k + 1, :] * ds
            acc = term if acc is None else acc + term
        dx_ref[...] = _c(acc)

    return pl.pallas_call(
        body,
        grid=(nt,),
        in_specs=[pl.BlockSpec((TM, QKV), lambda i: (i, 0)), _halo_next_spec(QKV, HALO, t),
                  pl.BlockSpec(cw.shape, lambda i: (0, 0))],
        out_specs=pl.BlockSpec((TM, QKV), lambda i: (i, 0)),
        out_shape=jax.ShapeDtypeStruct((t, QKV), _MM),
        name="conv_bwd",
    )(dc, dc, cw)


def _ab_proj_bwd(x1, nw, dqkv, dz, dsu, dsv, dba, wab_b, dres):
    t = x1.shape[0]

    def body(x_ref, nw_ref, dqkv_ref, dz_ref, dsu_ref, dsv_ref, dba_ref, w_ref, dres_ref, dx_ref, dcat_ref, dnw_ref):
        i = pl.program_id(0)
        dcat_ref[:, 0:QKV] = dqkv_ref[...]
        o = QKV
        for ref in (dz_ref, dsu_ref, dsv_ref):
            dcat_ref[:, o:o + HW] = ref[...]
            o += HW
        dcat_ref[:, o:o + 128] = dba_ref[...]
        dh = _dot_nt(dcat_ref[...], w_ref[...])
        xv = x_ref[...]
        r = _rstd(xv)
        dx, dnw = _rms_bwd(dh, xv * r, r, nw_ref[...])
        dx_ref[...] = dres_ref[...] + dx
        _accum(dnw_ref, dnw, i)

    row = lambda w: pl.BlockSpec((TM, w), lambda i: (i, 0))
    return pl.pallas_call(
        body,
        grid=(t // TM,),
        in_specs=[row(D), pl.BlockSpec((1, D), lambda i: (0, 0)), row(QKV), row(HW), row(HW), row(HW), row(128),
                  pl.BlockSpec((D, PW_B), lambda i: (0, 0)), row(D)],
        out_specs=[row(D), row(PW_B), pl.BlockSpec((1, D), lambda i: (0, 0))],
        out_shape=[jax.ShapeDtypeStruct((t, D), F32), jax.ShapeDtypeStruct((t, PW_B), _MM),
                   jax.ShapeDtypeStruct((1, D), F32)],
        name="ab_proj_bwd",
    )(x1, nw, dqkv, dz, dsu, dsv, dba, wab_b, dres)


def _pool_counts(i):
    pos = (lax.broadcasted_iota(jnp.int32, (TM + HALO, 1), 0) + i * TM + 1).astype(F32)
    return [1.0 / jnp.minimum(pos, float(w)) for w in POOL_WINDOWS]


def _window_sum(ext, win, back):
    r = ext.shape[0]
    s = ext
    step = 1
    while step < win:
        s = s + pltpu.roll(s, step if back else r - step, 0)
        step *= 2
    return s


def _pooled(h_ext, invc, g):
    gs = slice(g * PG, (g + 1) * PG)
    he = h_ext[:, gs]
    ws = _window_sum(he, POOL_WINDOWS[g], True)[HALO:, :]
    return ws * invc[g][0:TM, :] - he[HALO:, :]


def _pool_fwd(x1, nw, pw, scale):
    t = x1.shape[0]

    def body(x_ref, halo_ref, nw_ref, pw_ref, sc_ref, x2_ref):
        i = pl.program_id(0)
        xv = x_ref[...]
        hv = halo_ref[...]
        nwv = nw_ref[...]
        h_ext = jnp.concatenate([jnp.where(i == 0, 0.0, hv * _rstd(hv) * nwv), xv * _rstd(xv) * nwv], axis=0)
        invc = _pool_counts(i)
        outs = [_dot(_c(_pooled(h_ext, invc, g)), pw_ref[g]) for g in range(4)]
        x2_ref[...] = xv + jnp.concatenate(outs, axis=-1) * sc_ref[...]

    return pl.pallas_call(
        body,
        grid=(t // TM,),
        in_specs=[pl.BlockSpec((TM, D), lambda i: (i, 0)), _halo_prev_spec(D, HALO),
                  pl.BlockSpec((1, D), lambda i: (0, 0)), pl.BlockSpec((4, PG, PG), lambda i: (0, 0, 0)),
                  pl.BlockSpec((1, D), lambda i: (0, 0))],
        out_specs=pl.BlockSpec((TM, D), lambda i: (i, 0)),
        out_shape=jax.ShapeDtypeStruct((t, D), F32),
        name="pool_fwd",
    )(x1, x1, nw, pw, scale)


def _pool_bwd(x1, nw, pw, scale, dx2):
    t = x1.shape[0]
    nt = t // TM

    def body(x_ref, halo_ref, nw_ref, pw_ref, sc_ref, dx2_ref, dnext_ref, dx_ref, dnw_ref, dpw_ref, dsc_ref):
        i = pl.program_id(0)
        xv = x_ref[...]
        hv = halo_ref[...]
        nwv = nw_ref[...]
        r = _rstd(xv)
        xh = xv * r
        h_ext = jnp.concatenate([jnp.where(i == 0, 0.0, hv * _rstd(hv) * nwv), xh * nwv], axis=0)
        invc = _pool_counts(i)
        dyv = dx2_ref[...]
        dout_ext = jnp.concatenate([dyv, jnp.where(i == nt - 1, 0.0, dnext_ref[...])], axis=0) * sc_ref[...]
        dh_parts = []
        dsc_parts = []
        for g in range(4):
            gs = slice(g * PG, (g + 1) * PG)
            pooled_b = _c(_pooled(h_ext, invc, g))
            dout_b = _c(dout_ext[:, gs])
            dsc_parts.append(jnp.sum(dyv[:, gs] * _dot(pooled_b, pw_ref[g]), axis=0, keepdims=True))
            _accum(dpw_ref.at[g], _dot_tn(pooled_b, dout_b[0:TM, :]), i)
            dpool_ext = _dot_nt(dout_b, pw_ref[g])
            lead = _window_sum(dpool_ext * invc[g], POOL_WINDOWS[g], False)[0:TM, :]
            dh_parts.append(lead - dpool_ext[0:TM, :])
        _accum(dsc_ref, jnp.concatenate(dsc_parts, axis=-1), i)
        dx, dnw = _rms_bwd(jnp.concatenate(dh_parts, axis=-1), xh, r, nwv)
        dx_ref[...] = dyv + dx
        _accum(dnw_ref, dnw, i)

    vec = pl.BlockSpec((1, D), lambda i: (0, 0))
    return pl.pallas_call(
        body,
        grid=(nt,),
        in_specs=[pl.BlockSpec((TM, D), lambda i: (i, 0)), _halo_prev_spec(D, HALO), vec,
                  pl.BlockSpec((4, PG, PG), lambda i: (0, 0, 0)), vec,
                  pl.BlockSpec((TM, D), lambda i: (i, 0)), _halo_next_spec(D, HALO, t)],
        out_specs=[pl.BlockSpec((TM, D), lambda i: (i, 0)), vec, pl.BlockSpec((4, PG, PG), lambda i: (0, 0, 0)), vec],
        out_shape=[jax.ShapeDtypeStruct((t, D), F32), jax.ShapeDtypeStruct((1, D), F32),
                   jax.ShapeDtypeStruct((4, PG, PG), F32), jax.ShapeDtypeStruct((1, D), F32)],
        name="pool_bwd",
    )(x1, x1, nw, pw, scale, dx2, dx2)


def _adamw(lands, w, m, v, rb, name):
    nl, nr = w.shape[0], w.shape[1]
    rest = w.shape[2:]
    ns = lands[0].shape[0]
    zeros = (0,) * len(rest)

    def body(*refs):
        l_refs = refs[0:nl]
        w_ref, m_ref, v_ref, g_ref, d_ref, m2_ref, v2_ref = refs[nl:]
        for l in range(nl):
            g = l_refs[l][0].astype(F32)
            for s in range(1, ns):
                g = g + l_refs[l][s].astype(F32)
            m2 = ADAM_B1 * m_ref[l] + (1.0 - ADAM_B1) * g
            v2 = ADAM_B2 * v_ref[l] + (1.0 - ADAM_B2) * (g * g)
            m_hat = m2 / (1.0 - ADAM_B1 ** ADAM_STEP)
            v_hat = v2 / (1.0 - ADAM_B2 ** ADAM_STEP)
            g_ref[l] = g
            d_ref[l] = -ADAM_LR * (m_hat / (jnp.sqrt(v_hat) + ADAM_EPS) + ADAM_WD * w_ref[l])
            m2_ref[l] = m2
            v2_ref[l] = v2

    lspec = pl.BlockSpec((ns, rb) + rest, lambda r: (0, r) + zeros)
    wspec = pl.BlockSpec((nl, rb) + rest, lambda r: (0, r) + zeros)
    return pl.pallas_call(
        body,
        grid=(nr // rb,),
        in_specs=[lspec] * nl + [wspec] * 3,
        out_specs=[wspec] * 4,
        out_shape=[jax.ShapeDtypeStruct(w.shape, F32)] * 4,
        name=name,
    )(*lands, w, m, v)


WEIGHT_ORDER = ("ffn_norm1", "ffn1_w_in", "ffn1_w_out", "mix_norm", "ffn_norm2", "ffn2_w_in", "ffn2_w_out", "ab_w_in",
                "dn_conv_w", "dn_a_log", "dn_dt_bias", "dn_out_norm", "sg_norm", "sg_w", "sg_b", "ab_w_out", "pool_w",
                "pool_scale", "final_norm")
R_SMALL = 88
SMALL_ROWS = (
    ("ffn_norm1", (2, D), 2), ("mix_norm", (2, D), 2), ("ffn_norm2", (2, D), 2), ("final_norm", (D,), 1),
    ("sg_w", (1, 4, SGC, SGC), 64), ("sg_norm", (1, 4, HD), 1), ("sg_b", (1, 4, SGC), 1), ("dn_out_norm", (1, HD), 1),
    ("dn_a_log", (1, 4), 1), ("dn_dt_bias", (1, 4), 1), ("pool_scale", (1, D), 1), ("dn_conv_w", (1, 4, QKV), 8),
)
SMALL_SHARDED = ("pool_scale", "dn_conv_w")


def _rows_of(a, rows):
    if a.shape[-1] == QKV:
        return jnp.pad(a.reshape(4, QKV), ((0, 0), (0, 2 * ROW - QKV))).reshape(8, ROW)
    n = _numel(a.shape)
    if n % ROW == 0:
        return a.reshape(n // ROW, ROW)
    return jnp.pad(a.reshape(1, n), ((0, 0), (0, ROW - n)))


def _from_rows(r, shape):
    if shape[-1] == QKV:
        return r.reshape(4, 2 * ROW)[:, 0:QKV].reshape(shape)
    n = _numel(shape)
    if n % ROW == 0:
        return r.reshape(shape)
    return r[:, 0:n].reshape(shape)


def _pack_small(vals):
    parts = [(_rows_of(vals[n].astype(F32), r) if n in vals else jnp.zeros((r, ROW), F32)) for n, _, r in SMALL_ROWS]
    used = sum(r for _, _, r in SMALL_ROWS)
    return jnp.concatenate(parts + [jnp.zeros((R_SMALL - used, ROW), F32)], axis=0)


def _unpack_small(packed):
    out, o = {}, 0
    for n, shape, r in SMALL_ROWS:
        out[n] = _from_rows(packed[o:o + r], shape)
        o += r
    return out


def _pack_small_shard(ps, cw):
    return jnp.concatenate([
        jnp.pad(ps, ((0, 0), (0, ROW - D // N_DEV))), jnp.pad(cw[0], ((0, 0), (0, ROW - QKV // N_DEV))),
        jnp.zeros((3, ROW), F32)], axis=0)


def _mixer_weights(g_in, g_out, g_small, small):
    w = {}
    wi = jnp.transpose(g_in, (1, 0, 2)).reshape(D, AB_IN)
    gates = wi[:, AB_MAIN:AB_MAIN + AB_GATES]
    main = [wi[:, 0:AB_MAIN], wi[:, AB_MAIN + AB_GATES:AB_IN]]
    w["wab_f"] = jnp.concatenate(
        main + [jnp.repeat(gates[:, 0:HEADS], HD, axis=1), jnp.repeat(gates[:, HEADS:], HD, axis=1)], axis=1)
    w["wab_b"] = jnp.concatenate(main + [gates, jnp.zeros((D, 128 - AB_GATES), wi.dtype)], axis=1)
    w["cw"] = jnp.transpose(g_small[:, 1:5, 0:QKV // N_DEV], (1, 0, 2)).reshape(4, QKV)
    w["ps"] = g_small[:, 0, 0:D // N_DEV].reshape(1, D)
    w["alog"] = jnp.repeat(small["dn_a_log"][0].astype(F32), HD).reshape(1, HW)
    w["dtb"] = jnp.repeat(small["dn_dt_bias"][0].astype(F32), HD).reshape(1, HW)
    w["dnw"] = jnp.tile(small["dn_out_norm"][0].astype(F32), HEADS).reshape(1, HW)
    w["sgnw"] = small["sg_norm"][0].astype(F32).reshape(1, HW)
    tri = jnp.tril(jnp.ones((SGC, SGC), dtype=bool))
    wt = jnp.where(tri, small["sg_w"][0].astype(F32), 0.0)
    w["wtril"] = _c(wt)
    w["wtril_t"] = _c(jnp.transpose(wt, (0, 2, 1)))
    w["sgb"] = jnp.repeat(jnp.transpose(small["sg_b"][0].astype(F32)), HD, axis=1)
    w["wout_ab"] = g_out.reshape(D, D)
    return w


def kernel(x, ffn_norm1, ffn1_w_in, ffn1_w_out, mix_norm, ffn_norm2, ffn2_w_in, ffn2_w_out, ab_w_in, dn_conv_w, dn_a_log, dn_dt_bias, dn_out_norm, sg_norm, sg_w, sg_b, ab_w_out, pool_w, pool_scale, final_norm, loss_target, m_ffn_norm1, m_ffn1_w_in, m_ffn1_w_out, m_mix_norm, m_ffn_norm2, m_ffn2_w_in, m_ffn2_w_out, m_ab_w_in, m_dn_conv_w, m_dn_a_log, m_dn_dt_bias, m_dn_out_norm, m_sg_norm, m_sg_w, m_sg_b, m_ab_w_out, m_pool_w, m_pool_scale, m_final_norm, v_ffn_norm1, v_ffn1_w_in, v_ffn1_w_out, v_mix_norm, v_ffn_norm2, v_ffn2_w_in, v_ffn2_w_out, v_ab_w_in, v_dn_conv_w, v_dn_a_log, v_dn_dt_bias, v_dn_out_norm, v_sg_norm, v_sg_w, v_sg_b, v_ab_w_out, v_pool_w, v_pool_scale, v_final_norm):
    wl = dict(ffn_norm1=ffn_norm1, mix_norm=mix_norm, ffn_norm2=ffn_norm2, dn_a_log=dn_a_log, dn_dt_bias=dn_dt_bias,
              dn_out_norm=dn_out_norm, sg_norm=sg_norm, sg_w=sg_w, sg_b=sg_b, final_norm=final_norm)
    ml = dict(ffn_norm1=m_ffn_norm1, mix_norm=m_mix_norm, ffn_norm2=m_ffn_norm2, dn_a_log=m_dn_a_log,
              dn_dt_bias=m_dn_dt_bias, dn_out_norm=m_dn_out_norm, sg_norm=m_sg_norm, sg_w=m_sg_w, sg_b=m_sg_b,
              final_norm=m_final_norm)
    vl = dict(ffn_norm1=v_ffn_norm1, mix_norm=v_mix_norm, ffn_norm2=v_ffn_norm2, dn_a_log=v_dn_a_log,
              dn_dt_bias=v_dn_dt_bias, dn_out_norm=v_dn_out_norm, sg_norm=v_sg_norm, sg_w=v_sg_w, sg_b=v_sg_b,
              final_norm=v_final_norm)
    row = lambda a: a.reshape(1, -1).astype(F32)
    n1 = [row(ffn_norm1[l]) for l in range(2)]
    n2 = [row(ffn_norm2[l]) for l in range(2)]
    mix = [row(mix_norm[l]) for l in range(2)]
    s_in = {(f, l): _c(wf[l]) for f, wf in enumerate((ffn1_w_in, ffn2_w_in)) for l in range(2)}
    s_out = {(f, l): _c(wf[l]) for f, wf in enumerate((ffn1_w_out, ffn2_w_out)) for l in range(2)}
    xs, tgt = x[0], loss_target[0]

    wi00, wo00 = _comm_call(_Comm("gather", [s_in[0, 0], s_out[0, 0]]), "gather_first")
    x01, gu00, (g_abin, g_about, g_small, wi10) = _ffn_fwd(
        xs, n1[0], wi00, wo00,
        comm=_Comm("gather", [_c(ab_w_in[0]), _c(ab_w_out[0]), _pack_small_shard(pool_scale, dn_conv_w), s_in[1, 0]]))
    w = _mixer_weights(g_abin, g_about, g_small, wl)
    h, qkv, z, su, sv, b_rep, a_rep = _ab_proj(x01, mix[0], w["wab_f"])
    qn, kn, v, beta, g = _dn_pre(qkv, b_rep, a_rep, w["cw"], w["alog"], w["dtb"])
    (o, sall, aall, dn_u, dn_w), (wo10, g_pw) = _dn_fwd(
        qn, kn, v, beta, g, comm=_Comm("gather", [s_out[1, 0], _c(pool_w[0])]))
    pw = jnp.transpose(g_pw, (1, 0, 2, 3)).reshape(4, PG, PG)
    x02, cat = _ab_out(x01, o, z, su, sv, w["dnw"], w["sgnw"], w["wtril"], w["sgb"], w["wout_ab"])
    x10, gu10, (wi01, wo01) = _ffn_fwd(x02, n2[0], wi10, wo10, comm=_Comm("gather", [s_in[0, 1], s_out[0, 1]]))
    x11, gu01, (wi11, wo11) = _ffn_fwd(x10, n1[1], wi01, wo01, comm=_Comm("gather", [s_in[1, 1], s_out[1, 1]]))
    x12 = _pool_fwd(x11, mix[1], pw, w["ps"])
    x13, gu11, _ = _ffn_fwd(x12, n2[1], wi11, wo11)
    loss_local, dx, d_fn = _loss_head(x13, row(final_norm), tgt)

    bt = min(BT, xs.shape[0])

    def ffn_b(xin, nw, w_in, w_out, gu, dy, comm=None):
        (dxn, xn, act, dh, dnw), landed = _ffn_bwd(xin, nw, w_in, w_out, gu, dy, comm)
        return dxn, dnw, (xn, act, dh), landed

    def ffn_g(kept, dy):
        return [_mm_tn_win(kept[0], kept[2])[0], _mm_tn_wout(kept[1], dy)]

    dy = dx
    dx, d_n2_1, kept, _ = ffn_b(x12, n2[1], wi11, wo11, gu11, dy)
    g11 = ffn_g(kept, dy)
    dx, d_mix_1, d_pw, d_ps = _pool_bwd(x11, mix[1], pw, w["ps"], dx)
    d_pw_sh = _c(jnp.transpose(d_pw.reshape(4, N_DEV, PG // N_DEV, PG), (1, 0, 2, 3)))
    dy = dx
    dx, d_n1_1, kept, land11 = ffn_b(x10, n1[1], wi01, wo01, gu01, dy, _Comm("exchange", g11))
    g01 = ffn_g(kept, dy)
    dy = dx
    dx, d_n2_0, kept, land01 = ffn_b(x02, n2[0], wi10, wo10, gu10, dy, _Comm("exchange", g01 + [d_pw_sh]))
    g10 = ffn_g(kept, dy)
    do, dz, dsu, dsv, d_dnw, d_sgnw, d_sgw, d_sgb = _ab_out_bwd(
        dx, o, z, su, sv, w["dnw"], w["sgnw"], w["wtril"], w["wtril_t"], w["sgb"], w["wout_ab"])
    d_about = _mm_tn(cat, dx, D, D, bt, _MM, "mm_tn_about").reshape(N_DEV, D // N_DEV, D)
    (dqn, dkn, dv, dbeta, dg), _ = _dn_bwd(qn, kn, v, beta, g, sall, aall, dn_u, dn_w, do)
    dc, dba, d_cw, d_alog, d_dtb = _dn_pre_bwd(qkv, b_rep, a_rep, w["cw"], w["alog"], w["dtb"], dqn, dkn, dv, dbeta, dg)
    dqkv = _conv_bwd(dc, w["cw"])
    dx, dcat, d_mix_0 = _ab_proj_bwd(x01, mix[0], dqkv, dz, dsu, dsv, dba, w["wab_b"], dx)
    d_wab = _mm_tn(h, dcat, D, 640, bt, _MM, "mm_tn_abin")
    rest = PW_B - 128
    d_abin = jnp.concatenate([d_wab[:, 0:AB_MAIN], d_wab[:, rest:rest + AB_GATES], d_wab[:, AB_MAIN:rest]], axis=1)
    d_abin_sh = jnp.transpose(d_abin.reshape(D, N_DEV, AB_IN // N_DEV), (1, 0, 2))
    dy = dx
    grad_x, d_n1_0, kept, land_mid = ffn_b(xs, n1[0], wi00, wo00, gu00, dy,
                                           _Comm("exchange", g10 + [d_abin_sh, d_about]))
    land10, land_ab = land_mid[0:2], land_mid[2:4]

    g_small = {
        "ffn_norm1": jnp.concatenate([d_n1_0, d_n1_1], axis=0),
        "mix_norm": jnp.concatenate([d_mix_0, d_mix_1], axis=0),
        "ffn_norm2": jnp.concatenate([d_n2_0, d_n2_1], axis=0),
        "dn_conv_w": d_cw.reshape(1, 4, QKV),
        "dn_a_log": d_alog[:, ::HD],
        "dn_dt_bias": d_dtb[:, ::HD],
        "dn_out_norm": d_dnw,
        "sg_norm": d_sgnw.reshape(1, HEADS, HD),
        "sg_w": d_sgw[None],
        "sg_b": jnp.transpose(d_sgb[:, 0:HEADS])[None],
        "pool_scale": d_ps,
        "final_norm": d_fn.reshape(D),
    }
    g00_out = _mm_tn_wout(kept[1], dy)
    g00_in, (land00_out, land_small) = _mm_tn_win(
        kept[0], kept[2], comm=_Comm("exchange", [g00_out], repl=[_pack_small(g_small)]))
    (land00_in,) = _comm_call(_Comm("exchange", [g00_in]), "exchange_last")

    res = {}
    tr = lambda a: jnp.swapaxes(a, 1, 2)
    res["ffn1_w_in"] = [tr(a) for a in _adamw([land00_in, land01[0]], tr(ffn1_w_in), tr(m_ffn1_w_in), tr(v_ffn1_w_in),
                                              176, "adamw_w_in")]
    res["ffn2_w_in"] = [tr(a) for a in _adamw([land10[0], land11[0]], tr(ffn2_w_in), tr(m_ffn2_w_in), tr(v_ffn2_w_in),
                                              176, "adamw_w_in")]
    res["ffn1_w_out"] = _adamw([land00_out, land01[1]], ffn1_w_out, m_ffn1_w_out, v_ffn1_w_out, 176, "adamw_w_out")
    res["ffn2_w_out"] = _adamw([land10[1], land11[1]], ffn2_w_out, m_ffn2_w_out, v_ffn2_w_out, 176, "adamw_w_out")
    res["ab_w_in"] = _adamw([land_ab[0]], ab_w_in, m_ab_w_in, v_ab_w_in, 256, "adamw_ab_w_in")
    res["ab_w_out"] = _adamw([land_ab[1]], ab_w_out, m_ab_w_out, v_ab_w_out, D // N_DEV, "adamw_ab_w_out")
    res["pool_w"] = _adamw([land01[2]], pool_w, m_pool_w, v_pool_w, 4, "adamw_pool_w")
    sm = _adamw([land_small], _pack_small(wl)[None], _pack_small(ml)[None], _pack_small(vl)[None], R_SMALL,
                "adamw_replicated")
    sm = [_unpack_small(a[0]) for a in sm]
    for n in wl:
        res[n] = [d[n] for d in sm]
    me = 4 * lax.axis_index("x") + 2 * lax.axis_index("y") + lax.axis_index("c")
    g_ps = lax.dynamic_slice(sm[0]["pool_scale"], (0, me * (D // N_DEV)), (1, D // N_DEV))
    g_cw = lax.dynamic_slice(sm[0]["dn_conv_w"], (0, 0, me * (QKV // N_DEV)), (1, 4, QKV // N_DEV))
    s2 = _adamw([_pack_small_shard(g_ps, g_cw)[None]], _pack_small_shard(pool_scale, dn_conv_w)[None],
                _pack_small_shard(m_pool_scale, m_dn_conv_w)[None], _pack_small_shard(v_pool_scale, v_dn_conv_w)[None],
                8, "adamw_small_sharded")
    res["pool_scale"] = [a[0, 0:1, 0:D // N_DEV] for a in s2]
    res["dn_conv_w"] = [a[0, 1:5, 0:QKV // N_DEV][None] for a in s2]

    loss = lax.psum(loss_local[0, 0], ("x", "y", "c"))
    result = [loss, grad_x[None]]
    for i in range(4):
        result += [res[n][i] for n in WEIGHT_ORDER]
    return tuple(result)
```

```python
import jax
import jax.numpy as jnp
from jax import lax
from jax.experimental import pallas as pl
from jax.experimental.pallas import tpu as pltpu

F32 = jnp.float32
_MM = jnp.bfloat16

D = 1024
FF = 2816
EPS = 1e-6
HEADS = 4
HD = 128
DNC = 64
DN_STEP = 8
SGC = 128
QKV = 3 * HEADS * HD
HW = HEADS * HD
POOL_WINDOWS = (2, 4, 8, 16)
PG = D // 4
HALO = 16
N_DEV = 8
AB_IN = 3080
ROW = 1024

TM = 512
BT = 2048
BT_WIN = 4096
FT = 512
FWD_CHUNKS = 2
BWD_CHUNKS = 1
FC = 704
NJ = FF // FC
WO_ROWS = FF // N_DEV

ADAM_LR, ADAM_B1, ADAM_B2, ADAM_EPS, ADAM_WD, ADAM_STEP = 0.001, 0.9, 0.999, 1e-08, 0.01, 10

MESH_T = pl.DeviceIdType.MESH
NN = (((1,), (0,)), ((), ()))
NT = (((1,), (1,)), ((), ()))
TN = (((0,), (0,)), ((), ()))


def _c(a):
    return a.astype(_MM)


def _dg(a, b, dims):
    return lax.dot_general(a, b, dims, preferred_element_type=F32)


def _dot(a, b):
    return _dg(a, b, NN)


def _dot_nt(a, b):
    return _dg(a, b, NT)


def _dot_tn(a, b):
    return _dg(a, b, TN)


def _split2(a):
    hi = _c(a)
    return hi, _c(a - hi.astype(F32))


def _dot3(a, b, dims=NN):
    ah, al = _split2(a)
    bh, bl = _split2(b)
    return _dg(ah, bh, dims) + (_dg(ah, bl, dims) + _dg(al, bh, dims))


def _mask_dot(mask, x):
    x1 = _c(x)
    r = x - x1.astype(F32)
    x2 = _c(r)
    x3 = _c(r - x2.astype(F32))
    return _dot(mask, x1) + (_dot(mask, x2) + _dot(mask, x3))


def _sigmoid(x):
    return jax.nn.sigmoid(x)


def _gelu(x):
    return 0.5 * x * (1.0 + lax.erf(x * 0.7071067811865476))


def _gelu_grad(x):
    return 0.5 * (1.0 + lax.erf(x * 0.7071067811865476)) + x * jnp.exp(-0.5 * x * x) * 0.3989422804014327


def _accum(ref, val, step):
    @pl.when(step == 0)
    def _():
        ref[...] = val

    @pl.when(step > 0)
    def _():
        ref[...] += val


def _rstd(x):
    return lax.rsqrt(jnp.mean(x * x, axis=-1, keepdims=True) + EPS)


def _rms_bwd(dy, xhat, r, nw):
    dnw = jnp.sum(dy * xhat, axis=0, keepdims=True)
    dxh = dy * nw
    dx = r * (dxh - xhat * jnp.mean(dxh * xhat, axis=-1, keepdims=True))
    return dx, dnw


def _numel(shape):
    n = 1
    for s in shape:
        n *= s
    return n


def _peer(k, x, y, c):
    px = 1 - x if k & 4 else x
    py = 1 - y if k & 2 else y
    pc = 1 - c if k & 1 else c
    return px, py, pc


class _Comm:
    def __init__(self, kind, arrs, repl=()):
        self.kind = kind
        self.ns = len(arrs)
        self.arrs = list(arrs) + list(repl)
        self.na = len(self.arrs)

    @property
    def out_shape(self):
        out = []
        for i, a in enumerate(self.arrs):
            lead = (N_DEV,) if (self.kind == "gather" or i >= self.ns) else ()
            out.append(jax.ShapeDtypeStruct(lead + a.shape, a.dtype))
        return out

    @property
    def scratch(self):
        return [pltpu.SemaphoreType.DMA((7 * self.na,)), pltpu.SemaphoreType.DMA((7 * self.na,)),
                pltpu.SemaphoreType.DMA((self.na,))]

    def phases(self, ins, outs, sems):
        send_sems, recv_sems, local_sems = sems
        na = self.na
        x, y, c = lax.axis_index("x"), lax.axis_index("y"), lax.axis_index("c")
        if self.kind == "gather":
            me, sibling = (x, y, c), (x, y, 1 - c)
            chips = [(1 - x, y), (x, 1 - y), (1 - x, 1 - y)]

            def slot(a, px, py, pc):
                return outs[a].at[4 * px + 2 * py + pc]

            def copy(a, k, block, to, src=None):
                return pltpu.make_async_remote_copy(
                    src_ref=slot(a, *block) if src is None else src, dst_ref=slot(a, *block),
                    send_sem=send_sems.at[7 * a + k], recv_sem=recv_sems.at[7 * a + k],
                    device_id=to, device_id_type=MESH_T)

            mine = [pltpu.make_async_copy(ins[a], slot(a, *me), local_sems.at[a]) for a in range(na)]
            first, passed = [], []
            for a in range(na):
                first.append(copy(a, 0, me, sibling, src=ins[a]))
                first += [copy(a, 1 + j, me, (*chip, c), src=ins[a]) for j, chip in enumerate(chips)]
                passed += [copy(a, 4 + j, (*chip, c), sibling) for j, chip in enumerate(chips)]

            def start():
                for cp in mine + first:
                    cp.start()

            def middle():
                for a in range(na):
                    for j, chip in enumerate(chips):
                        copy(a, 1 + j, (*chip, c), me).wait_recv()
                        passed[3 * a + j].start()

            def finish():
                for a in range(na):
                    copy(a, 0, sibling, me).wait_recv()
                    for j, chip in enumerate(chips):
                        copy(a, 4 + j, (*chip, 1 - c), me).wait_recv()
                for cp in first + passed:
                    cp.wait_send()
                for cp in mine:
                    cp.wait()

            return start, middle, finish

        me = 4 * x + 2 * y + c
        ns = self.ns
        own = [pltpu.make_async_copy(ins[a].at[me] if a < ns else ins[a], outs[a].at[me], local_sems.at[a])
               for a in range(na)]
        copies = []
        for k in range(1, N_DEV):
            px, py, pc = _peer(k, x, y, c)
            peer = 4 * px + 2 * py + pc
            for a in range(na):
                copies.append(pltpu.make_async_remote_copy(
                    src_ref=ins[a].at[peer] if a < ns else ins[a], dst_ref=outs[a].at[me],
                    send_sem=send_sems.at[na * (k - 1) + a], recv_sem=recv_sems.at[na * (k - 1) + a],
                    device_id=(px, py, pc), device_id_type=MESH_T))

        def start():
            for cp in own + copies:
                cp.start()

        def middle():
            pass

        def finish():
            for cp in copies:
                cp.wait()
            for cp in own:
                cp.wait()

        return start, middle, finish


def _comm_call(comm, name):
    na = comm.na

    def body(*refs):
        start, middle, finish = comm.phases(refs[0:na], refs[na:2 * na], refs[2 * na:])
        start()
        middle()
        finish()

    hbm = pl.BlockSpec(memory_space=pltpu.HBM)
    return pl.pallas_call(
        body, out_shape=comm.out_shape, in_specs=[hbm] * na, out_specs=[hbm] * na, scratch_shapes=comm.scratch,
        name=name)(*comm.arrs)


def _carried_call(body, comm, n_in, n_out, n_scr, when, *, grid, in_specs, out_specs, out_shape, scratch_shapes,
                  operands, name):
    if comm is None:
        return pl.pallas_call(body, grid=grid, in_specs=in_specs, out_specs=out_specs, out_shape=out_shape,
                              scratch_shapes=scratch_shapes, name=name)(*operands), []
    na = comm.na

    def both(*refs):
        a = n_in + na
        b = a + n_out + na
        body(*refs[0:n_in], *refs[a:a + n_out], *refs[b:b + n_scr])
        start, middle, finish = comm.phases(refs[n_in:a], refs[a + n_out:b], refs[b + n_scr:])
        first, mid, last = when()
        pl.when(first)(start)
        pl.when(mid)(middle)
        pl.when(last)(finish)

    hbm = pl.BlockSpec(memory_space=pltpu.HBM)
    res = pl.pallas_call(
        both, grid=grid, in_specs=list(in_specs) + [hbm] * na, out_specs=list(out_specs) + [hbm] * na,
        out_shape=list(out_shape) + comm.out_shape, scratch_shapes=list(scratch_shapes) + comm.scratch,
        name=name)(*operands, *comm.arrs)
    return res[0:n_out], res[n_out:]


def _ffn_fwd(x, nw, w_in, w_out, comm=None):
    t = x.shape[0]
    tm = min(FT, t)
    nt = t // tm

    nj = NJ // FWD_CHUNKS

    def body(x_ref, nw_ref, wg_ref, wu_ref, wo3_ref, o_ref, gu_ref, xn_sc, acc_sc):
        j = pl.program_id(1)

        @pl.when(j == 0)
        def _():
            xv = x_ref[...]
            xn_sc[...] = _c(xv * _rstd(xv) * nw_ref[...])
            acc_sc[...] = jnp.zeros_like(acc_sc)

        xn = xn_sc[...]
        part = None
        for cc in range(FWD_CHUNKS):
            g = _dot(xn, wg_ref[cc])
            u = _dot(xn, wu_ref[cc])
            gu_ref[cc, 0] = _c(g)
            gu_ref[cc, 1] = _c(u)
            p = _dot(_c(g * _sigmoid(g) * u), wo3_ref[2 * cc:2 * cc + 2].reshape(FC, D))
            part = p if part is None else part + p
        acc_sc[...] += part

        @pl.when(j == nj - 1)
        def _():
            o_ref[...] = x_ref[...] + 0.5 * acc_sc[...]

    def when():
        i, j = pl.program_id(0), pl.program_id(1)
        return ((i == 0) & (j == 0), (i == (3 * nt) // 4) & (j == 0), (i == nt - 1) & (j == nj - 1))

    (out, gu), landed = _carried_call(
        body, comm, 5, 2, 2, when,
        grid=(nt, nj),
        in_specs=[pl.BlockSpec((tm, D), lambda i, j: (i, 0)), pl.BlockSpec((1, D), lambda i, j: (0, 0)),
                  pl.BlockSpec((FWD_CHUNKS, D, FC), lambda i, j: (j, 0, 0)),
                  pl.BlockSpec((FWD_CHUNKS, D, FC), lambda i, j: (j + nj, 0, 0)),
                  pl.BlockSpec((2 * FWD_CHUNKS, WO_ROWS, D), lambda i, j: (j, 0, 0))],
        out_specs=[pl.BlockSpec((tm, D), lambda i, j: (i, 0)),
                   pl.BlockSpec((FWD_CHUNKS, 2, tm, FC), lambda i, j: (j, 0, i, 0))],
        out_shape=[jax.ShapeDtypeStruct((t, D), F32), jax.ShapeDtypeStruct((NJ, 2, t, FC), _MM)],
        scratch_shapes=[pltpu.VMEM((tm, D), _MM), pltpu.VMEM((tm, D), F32)],
        operands=(x, nw, w_in, w_in, w_out), name="ffn_fwd")
    return out, gu, landed


def _ffn_bwd(x, nw, w_in, w_out, gu, dy, comm=None):
    t = x.shape[0]
    nt = t // TM
    nc = BWD_CHUNKS
    nj = NJ // nc

    def body(x_ref, nw_ref, wg_ref, wu_ref, wo3_ref, gu_ref, dy_ref, dx_ref, xn_ref, a_ref, dh_ref, dnw_ref, dyb_sc,
             r_sc, acc_sc):
        i = pl.program_id(0)
        j = pl.program_id(1)

        @pl.when(j == 0)
        def _():
            xv = x_ref[...]
            r = _rstd(xv)
            r_sc[...] = r
            xn_ref[...] = _c(xv * r * nw_ref[...])
            dyb_sc[...] = _c(0.5 * dy_ref[...])
            acc_sc[...] = jnp.zeros_like(acc_sc)

        part = None
        for cc in range(nc):
            g = gu_ref[cc, 0].astype(F32)
            u = gu_ref[cc, 1].astype(F32)
            s = _sigmoid(g)
            sl = g * s
            a_ref[cc] = _c(sl * u)
            da = _dot_nt(dyb_sc[...], wo3_ref[2 * cc:2 * cc + 2].reshape(FC, D))
            dg = _c(da * u * (s * (1.0 + g * (1.0 - s))))
            du = _c(da * sl)
            dh_ref[cc, 0] = dg
            dh_ref[cc, 1] = du
            p = _dot_nt(dg, wg_ref[cc]) + _dot_nt(du, wu_ref[cc])
            part = p if part is None else part + p
        acc_sc[...] += part

        @pl.when(j == nj - 1)
        def _():
            r = r_sc[...]
            dx, dnw = _rms_bwd(acc_sc[...], x_ref[...] * r, r, nw_ref[...])
            dx_ref[...] = dy_ref[...] + dx
            _accum(dnw_ref, dnw, i)

    def when():
        i, j = pl.program_id(0), pl.program_id(1)
        return ((i == 0) & (j == 0), (i == (3 * nt) // 4) & (j == 0), (i == nt - 1) & (j == nj - 1))

    return _carried_call(
        body, comm, 7, 6, 2, when,
        grid=(nt, nj),
        in_specs=[pl.BlockSpec((TM, D), lambda i, j: (i, 0)), pl.BlockSpec((1, D), lambda i, j: (0, 0)),
                  pl.BlockSpec((nc, D, FC), lambda i, j: (j, 0, 0)),
                  pl.BlockSpec((nc, D, FC), lambda i, j: (j + nj, 0, 0)),
                  pl.BlockSpec((2 * nc, WO_ROWS, D), lambda i, j: (j, 0, 0)),
                  pl.BlockSpec((nc, 2, TM, FC), lambda i, j: (j, 0, i, 0)),
                  pl.BlockSpec((TM, D), lambda i, j: (i, 0))],
        out_specs=[
            pl.BlockSpec((TM, D), lambda i, j: (i, 0)),
            pl.BlockSpec((TM, D), lambda i, j: (i, 0)),
            pl.BlockSpec((nc, TM, FC), lambda i, j: (j, i, 0)),
            pl.BlockSpec((nc, 2, TM, FC), lambda i, j: (j, 0, i, 0)),
            pl.BlockSpec((1, D), lambda i, j: (0, 0)),
            pl.BlockSpec((TM, D), lambda i, j: (i, 0)),
        ],
        out_shape=[
            jax.ShapeDtypeStruct((t, D), F32),
            jax.ShapeDtypeStruct((t, D), _MM),
            jax.ShapeDtypeStruct((NJ, t, FC), _MM),
            jax.ShapeDtypeStruct((NJ, 2, t, FC), _MM),
            jax.ShapeDtypeStruct((1, D), F32),
            jax.ShapeDtypeStruct((t, D), _MM),
        ],
        scratch_shapes=[pltpu.VMEM((TM, 1), F32), pltpu.VMEM((TM, D), F32)],
        operands=(x, nw, w_in, w_in, w_out, gu, dy), name="ffn_bwd")


def _mm_tn(a, b, bm, bn, bt, out_dtype, name):
    t, m = a.shape
    n = b.shape[1]
    nt = t // bt

    def body(a_ref, b_ref, o_ref, acc_sc):
        k = pl.program_id(2)
        _accum(acc_sc, _dot_tn(_c(a_ref[...]), _c(b_ref[...])), k)

        @pl.when(k == nt - 1)
        def _():
            o_ref[...] = acc_sc[...].astype(out_dtype)

    return pl.pallas_call(
        body,
        grid=(m // bm, n // bn, nt),
        in_specs=[pl.BlockSpec((bt, bm), lambda i, j, k: (k, i)), pl.BlockSpec((bt, bn), lambda i, j, k: (k, j))],
        out_specs=pl.BlockSpec((bm, bn), lambda i, j, k: (i, j)),
        out_shape=jax.ShapeDtypeStruct((m, n), out_dtype),
        scratch_shapes=[pltpu.VMEM((bm, bn), F32)],
        name=name,
    )(a, b)


def _mm_tn_win(xn, dh, comm=None):
    t = xn.shape[0]
    bt = min(BT_WIN, t)
    nt = t // bt

    def body(a_ref, b_ref, o_ref, acc_sc):
        k = pl.program_id(2)
        _accum(acc_sc, _dot_tn(b_ref[...], a_ref[...]), k)

        @pl.when(k == nt - 1)
        def _():
            o_ref[...] = _c(acc_sc[...])

    def when():
        h, j, k = pl.program_id(0), pl.program_id(1), pl.program_id(2)
        start = (h == 0) & (j == 0) & (k == 0)
        return start, start, (h == 1) & (j == NJ - 1) & (k == nt - 1)

    (out,), landed = _carried_call(
        body, comm, 2, 1, 1, when,
        grid=(2, NJ, nt),
        in_specs=[pl.BlockSpec((bt, D), lambda h, j, k: (k, 0)),
                  pl.BlockSpec((None, None, bt, FC), lambda h, j, k: (j, h, k, 0))],
        out_specs=[pl.BlockSpec((None, FC, D), lambda h, j, k: (h * NJ + j, 0, 0))],
        out_shape=[jax.ShapeDtypeStruct((N_DEV, FC, D), _MM)],
        scratch_shapes=[pltpu.VMEM((FC, D), F32)],
        operands=(xn, dh), name="mm_tn_win")
    return out, landed


def _mm_tn_wout(act, dy):
    t = dy.shape[0]
    bt = min(BT_WIN, t)
    nt = t // bt

    def body(a_ref, b_ref, o_ref, acc_sc):
        k = pl.program_id(1)
        _accum(acc_sc, _dot_tn(a_ref[...], b_ref[...]), k)

        @pl.when(k == nt - 1)
        def _():
            o_ref[...] = _c(acc_sc[...].reshape(2, WO_ROWS, D))

    return pl.pallas_call(
        body,
        grid=(NJ, nt),
        in_specs=[pl.BlockSpec((None, bt, FC), lambda j, k: (j, k, 0)), pl.BlockSpec((bt, D), lambda j, k: (k, 0))],
        out_specs=pl.BlockSpec((2, WO_ROWS, D), lambda j, k: (j, 0, 0)),
        out_shape=jax.ShapeDtypeStruct((N_DEV, WO_ROWS, D), _MM),
        scratch_shapes=[pltpu.VMEM((FC, D), F32)],
        name="mm_tn_wout",
    )(act, dy)


def _loss_head(x, nw, tgt):
    t = x.shape[0]

    def body(x_ref, nw_ref, t_ref, loss_ref, dx_ref, dnw_ref):
        i = pl.program_id(0)
        xv = x_ref[...]
        r = _rstd(xv)
        xh = xv * r
        e = xh * nw_ref[...] - t_ref[...]
        part = 0.5 * jnp.sum(jnp.mean(e * e, axis=-1, keepdims=True), axis=0, keepdims=True)
        _accum(loss_ref, jnp.broadcast_to(part, (1, 128)), i)
        dx, dnw = _rms_bwd(e * (1.0 / D), xh, r, nw_ref[...])
        dx_ref[...] = dx
        _accum(dnw_ref, dnw, i)

    return pl.pallas_call(
        body,
        grid=(t // TM,),
        in_specs=[pl.BlockSpec((TM, D), lambda i: (i, 0)), pl.BlockSpec((1, D), lambda i: (0, 0)),
                  pl.BlockSpec((TM, D), lambda i: (i, 0))],
        out_specs=[pl.BlockSpec((1, 128), lambda i: (0, 0)), pl.BlockSpec((TM, D), lambda i: (i, 0)),
                   pl.BlockSpec((1, D), lambda i: (0, 0))],
        out_shape=[jax.ShapeDtypeStruct((1, 128), F32), jax.ShapeDtypeStruct((t, D), F32),
                   jax.ShapeDtypeStruct((1, D), F32)],
        name="loss_head",
    )(x, nw, tgt)


PW_F = QKV + 5 * HW
PW_B = QKV + 3 * HW + 128
AB_MAIN = QKV + HW
AB_GATES = 2 * HEADS


def _ab_proj(x1, nw, wab):
    t = x1.shape[0]

    def body(x_ref, nw_ref, w_ref, h_ref, qkv_ref, z_ref, su_ref, sv_ref, b_ref, a_ref):
        xv = x_ref[...]
        h = _c(xv * _rstd(xv) * nw_ref[...])
        h_ref[...] = h
        p = _dot(h, w_ref[...])
        qkv_ref[...] = p[:, 0:QKV]
        o = QKV
        for ref in (z_ref, su_ref, sv_ref, b_ref, a_ref):
            ref[...] = p[:, o:o + HW]
            o += HW

    row = lambda w: pl.BlockSpec((TM, w), lambda i: (i, 0))
    return pl.pallas_call(
        body,
        grid=(t // TM,),
        in_specs=[row(D), pl.BlockSpec((1, D), lambda i: (0, 0)), pl.BlockSpec((D, PW_F), lambda i: (0, 0))],
        out_specs=[row(D), row(QKV)] + [row(HW)] * 5,
        out_shape=[jax.ShapeDtypeStruct((t, D), _MM), jax.ShapeDtypeStruct((t, QKV), F32)]
        + [jax.ShapeDtypeStruct((t, HW), F32)] * 5,
        name="ab_proj",
    )(x1, nw, wab)


def _conv_rows(x, halo, cw):
    xe = jnp.concatenate([halo, x], axis=0)
    shifted = []
    c = None
    for k in range(4):
        s = 3 - k
        xs = (xe if s == 0 else pltpu.roll(xe, s, 0))[8:, :]
        shifted.append(xs)
        term = cw[k:k + 1, :] * xs
        c = term if c is None else c + term
    return c, shifted


def _head_rsq(a):
    parts = []
    for h in range(HEADS):
        ah = a[:, h * HD:(h + 1) * HD]
        r = lax.rsqrt(jnp.sum(ah * ah, axis=-1, keepdims=True) + EPS)
        parts.append(jnp.broadcast_to(r, ah.shape))
    return jnp.concatenate(parts, axis=-1)


def _head_sum(a):
    parts = []
    for h in range(HEADS):
        ah = a[:, h * HD:(h + 1) * HD]
        parts.append(jnp.broadcast_to(jnp.sum(ah, axis=-1, keepdims=True), ah.shape))
    return jnp.concatenate(parts, axis=-1)


def _softplus(x):
    return jnp.maximum(x, 0.0) + jnp.log1p(jnp.exp(-jnp.abs(x)))


def _halo_prev_spec(width, rows):
    per = TM // rows
    return pl.BlockSpec((rows, width), lambda i: (jnp.maximum(i * per - 1, 0), 0))


def _halo_next_spec(width, rows, t):
    per = TM // rows
    last = t // rows - 1
    return pl.BlockSpec((rows, width), lambda i: (jnp.minimum((i + 1) * per, last), 0))


def _dn_pre(qkv, b_rep, a_rep, cw, alog, dtb):
    t = qkv.shape[0]
    qscale = HD ** -0.5

    def body(x_ref, halo_ref, b_ref, a_ref, cw_ref, alog_ref, dt_ref, q_ref, k_ref, v_ref, beta_ref, g_ref):
        i = pl.program_id(0)
        halo = jnp.where(i == 0, 0.0, halo_ref[...])
        c, _ = _conv_rows(x_ref[...], halo, cw_ref[...])
        sc = c * _sigmoid(c)
        q = sc[:, 0:HW]
        k = sc[:, HW:2 * HW]
        q_ref[...] = q * _head_rsq(q) * qscale
        k_ref[...] = k * _head_rsq(k)
        v_ref[...] = sc[:, 2 * HW:]
        beta_ref[...] = _sigmoid(b_ref[...])
        g_ref[...] = -jnp.exp(alog_ref[...]) * _softplus(a_ref[...] + dt_ref[...])

    row = lambda w: pl.BlockSpec((TM, w), lambda i: (i, 0))
    full = lambda a: pl.BlockSpec(a.shape, lambda i: (0,) * a.ndim)
    return pl.pallas_call(
        body,
        grid=(t // TM,),
        in_specs=[row(QKV), _halo_prev_spec(QKV, 8), row(HW), row(HW), full(cw), full(alog), full(dtb)],
        out_specs=[row(HW)] * 5,
        out_shape=[jax.ShapeDtypeStruct((t, HW), F32)] * 5,
        name="dn_pre",
    )(qkv, qkv, b_rep, a_rep, cw, alog, dtb)


def _unit_lower_inv(los, eye):
    ps = [eye - lo for lo in los]
    lps = list(los)
    for _ in range(5):
        lps = [_dot(_c(lp), _c(lp)) for lp in lps]
        ps = [p + _dot(_c(p), _c(lp)) for p, lp in zip(ps, lps)]
    rs = [eye - (p + _dot3(lo, p)) for lo, p in zip(los, ps)]
    return [p + _dot(_c(p), _c(r)) for p, r in zip(ps, rs)]


def _dn_masks():
    ri = lax.broadcasted_iota(jnp.int32, (DNC, DNC), 0)
    ci = lax.broadcasted_iota(jnp.int32, (DNC, DNC), 1)
    return dict(strict=ri > ci, causal=ri >= ci, eye=(ri == ci).astype(F32),
                ltri=_c((ri >= ci).astype(F32)), upper=_c((ri <= ci).astype(F32)))


def _dn_decay(gr, mk):
    rhs = jnp.concatenate([gr, jnp.where(mk["strict"], gr[:, 0:DNC], 0.0)], axis=1)
    cs = _mask_dot(mk["ltri"], rhs)
    gc = cs[:, 0:HD]
    dm = jnp.where(mk["causal"], jnp.exp(cs[:, HD:HD + DNC]), 0.0)
    gl = jnp.sum(gr, axis=0, keepdims=True)
    return dm, jnp.exp(gc), jnp.exp(gl - gc), gl


def _dn_when(n):
    def when():
        i = pl.program_id(0)
        return (i == 0, i == n // 2, i == n - 1)
    return when


def _dn_fwd(q, k, v, beta, g, comm=None):
    t = q.shape[0]
    rows = DN_STEP * DNC
    n = t // rows

    def body(q_ref, k_ref, v_ref, b_ref, g_ref, o_ref, sall_ref, aall_ref, u_ref, w_ref, s_sc):
        i = pl.program_id(0)

        @pl.when(i == 0)
        def _():
            s_sc[...] = jnp.zeros_like(s_sc)

        mk = _dn_masks()
        idx = [(cc, h) for cc in range(DN_STEP) for h in range(HEADS)]
        at = lambda cc, h: (slice(cc * DNC, (cc + 1) * DNC), slice(h * HD, (h + 1) * HD))
        qs = [q_ref[at(*i)] for i in idx]
        ks = [k_ref[at(*i)] for i in idx]
        bs = [b_ref[at(*i)] for i in idx]
        dec = [_dn_decay(g_ref[at(*i)], mk) for i in idx]
        kbs = [k_ * b_ for k_, b_ in zip(ks, bs)]
        los = [jnp.where(mk["strict"], _dot_nt(_c(kb), _c(k_)) * d[0], 0.0) for kb, k_, d in zip(kbs, ks, dec)]
        inv = _unit_lower_inv(los, mk["eye"])
        uws = [_dot3(a, jnp.concatenate([v_ref[at(*i)] * b_, kb * d[1]], axis=1))
               for a, i, b_, kb, d in zip(inv, idx, bs, kbs, dec)]
        attn = [_c(_dot_nt(_c(q_), _c(k_)) * d[0]) for q_, k_, d in zip(qs, ks, dec)]
        for n_, (cc, h) in enumerate(idx):
            aall_ref[cc, h] = inv[n_]
            u_ref[at(cc, h)] = uws[n_][:, 0:HD]
            w_ref[at(cc, h)] = uws[n_][:, HD:]
        ss = [s_sc[h] for h in range(HEADS)]
        for cc in range(DN_STEP):
            base = cc * HEADS
            for h in range(HEADS):
                sall_ref[cc, h] = ss[h]
            ws = [_dot(_c(jnp.concatenate([uws[base + h][:, HD:], qs[base + h] * dec[base + h][1]], axis=0)),
                       _c(ss[h])) for h in range(HEADS)]
            vn = [_c(uws[base + h][:, 0:HD] - ws[h][0:DNC]) for h in range(HEADS)]
            for h in range(HEADS):
                o_ref[at(cc, h)] = ws[h][DNC:] + _dot(attn[base + h], vn[h])
            ss = [ss[h] * jnp.exp(dec[base + h][3]) + _dot_tn(_c(ks[base + h] * dec[base + h][2]), vn[h])
                  for h in range(HEADS)]
        for h in range(HEADS):
            s_sc[h] = ss[h]

    row = pl.BlockSpec((rows, HW), lambda i: (i, 0))
    return _carried_call(
        body, comm, 5, 5, 1, _dn_when(n),
        grid=(n,),
        in_specs=[row] * 5,
        out_specs=[row, pl.BlockSpec((DN_STEP, HEADS, HD, HD), lambda i: (i, 0, 0, 0)),
                   pl.BlockSpec((DN_STEP, HEADS, DNC, DNC), lambda i: (i, 0, 0, 0)), row, row],
        out_shape=[jax.ShapeDtypeStruct((t, HW), F32), jax.ShapeDtypeStruct((t // DNC, HEADS, HD, HD), F32),
                   jax.ShapeDtypeStruct((t // DNC, HEADS, DNC, DNC), F32), jax.ShapeDtypeStruct((t, HW), F32),
                   jax.ShapeDtypeStruct((t, HW), F32)],
        scratch_shapes=[pltpu.VMEM((HEADS, HD, HD), F32)],
        operands=(q, k, v, beta, g), name="dn_fwd")


def _dn_bwd(q, k, v, beta, g, sall, aall, u, w, do, comm=None):
    t = q.shape[0]
    rows = DN_STEP * DNC
    n = t // rows

    def body(q_ref, k_ref, v_ref, b_ref, g_ref, sall_ref, aall_ref, u_ref, w_ref, do_ref,
             dq_ref, dk_ref, dv_ref, db_ref, dg_ref, ds_sc):
        i = pl.program_id(0)

        @pl.when(i == 0)
        def _():
            ds_sc[...] = jnp.zeros_like(ds_sc)

        mk = _dn_masks()
        strict = mk["strict"]
        hs = range(HEADS)
        at = lambda cc, h: (slice(cc * DNC, (cc + 1) * DNC), slice(h * HD, (h + 1) * HD))
        rowsum = lambda a: jnp.sum(a, axis=-1, keepdims=True)
        dsn = [ds_sc[h] for h in hs]
        for cc in reversed(range(DN_STEP)):
            q = [q_ref[at(cc, h)] for h in hs]
            k = [k_ref[at(cc, h)] for h in hs]
            b = [b_ref[at(cc, h)] for h in hs]
            u = [u_ref[at(cc, h)] for h in hs]
            w = [w_ref[at(cc, h)] for h in hs]
            do = [do_ref[at(cc, h)] for h in hs]
            s = [sall_ref[cc, h] for h in hs]
            dec = [_dn_decay(g_ref[at(cc, h)], mk) for h in hs]
            dm, e, f = [d[0] for d in dec], [d[1] for d in dec], [d[2] for d in dec]
            egl = [jnp.exp(d[3]) for d in dec]
            kb = [k[h] * b[h] for h in hs]
            kc = [_c(k[h]) for h in hs]
            sb = [_c(s[h]) for h in hs]
            dob = [_c(do[h]) for h in hs]
            m = [_dot_nt(_c(kb[h]), kc[h]) for h in hs]
            p = [_dot_nt(_c(q[h]), kc[h]) for h in hs]
            vnb = [_c(u[h] - _dot(_c(w[h]), sb[h])) for h in hs]
            dsb = [_c(dsn[h]) for h in hs]
            dvn = [_dot_tn(_c(p[h] * dm[h]), dob[h]) + _dot(_c(k[h] * f[h]), dsb[h]) for h in hs]
            dov = [_c(jnp.concatenate([do[h], dvn[h]], axis=0)) for h in hs]
            t1 = [_dot_nt(dov[h], sb[h]) for h in hs]
            dattn = [_dot_nt(dob[h], vnb[h]) for h in hs]
            dkt = [_dot_nt(vnb[h], dsb[h]) for h in hs]
            dgl = [jnp.sum(jnp.sum(dsn[h] * s[h], axis=1, keepdims=True), axis=0, keepdims=True) * egl[h][:, 0:1]
                   for h in hs]
            dsn = [dsn[h] * egl[h] + _dot_tn(_c(jnp.concatenate([q[h] * e[h], -w[h]], axis=0)), dov[h]) for h in hs]
            dqd = [t1[h][0:DNC] for h in hs]
            dw = [-t1[h][DNC:] for h in hs]
            ab = [_dot3(aall_ref[cc, h], jnp.concatenate([dvn[h], dw[h]], axis=1), TN) for h in hs]
            dlo = [jnp.where(strict, -_dot3(ab[h], jnp.concatenate([u[h], w[h]], axis=1), NT), 0.0) for h in hs]
            dpm = [_c(jnp.concatenate([dattn[h] * dm[h], dlo[h] * dm[h]], axis=0)) for h in hs]
            t2 = [_dot(dpm[h], kc[h]) for h in hs]
            t4 = [_dot_tn(dpm[h], _c(jnp.concatenate([q[h], kb[h]], axis=0))) for h in hs]
            dff = [rowsum(dkt[h] * k[h]) * f[h][:, 0:1] for h in hs]
            de = [rowsum(dqd[h] * q[h]) + rowsum(ab[h][:, HD:] * kb[h]) for h in hs]
            dd = [(dattn[h] * p[h] + dlo[h] * m[h]) * dm[h] for h in hs]
            t3 = [_mask_dot(mk["upper"], jnp.concatenate(
                [jnp.broadcast_to(de[h] * e[h][:, 0:1] - dff[h], (DNC, HD)), dd[h]], axis=1)) for h in hs]
            for h in hs:
                dvb, dkbe = ab[h][:, 0:HD], ab[h][:, HD:]
                dkb = t2[h][DNC:] + dkbe * e[h]
                dbeta = rowsum(dkb * k[h]) + rowsum(dvb * v_ref[at(cc, h)])
                dg = (rowsum(jnp.where(strict, t3[h][:, HD:HD + DNC], 0.0)) + t3[h][:, 0:1]
                      + dgl[h] + jnp.sum(dff[h], axis=0, keepdims=True))
                dq_ref[at(cc, h)] = dqd[h] * e[h] + t2[h][0:DNC]
                dk_ref[at(cc, h)] = t4[h] + dkt[h] * f[h] + dkb * b[h]
                dv_ref[at(cc, h)] = dvb * b[h]
                db_ref[at(cc, h)] = jnp.broadcast_to(dbeta, (DNC, HD))
                dg_ref[at(cc, h)] = jnp.broadcast_to(dg, (DNC, HD))
        for h in hs:
            ds_sc[h] = dsn[h]

    row = pl.BlockSpec((rows, HW), lambda i: (n - 1 - i, 0))
    return _carried_call(
        body, comm, 10, 5, 1, _dn_when(n),
        grid=(n,),
        in_specs=[row] * 5 + [pl.BlockSpec((DN_STEP, HEADS, HD, HD), lambda i: (n - 1 - i, 0, 0, 0)),
                              pl.BlockSpec((DN_STEP, HEADS, DNC, DNC), lambda i: (n - 1 - i, 0, 0, 0)), row, row, row],
        out_specs=[row] * 5,
        out_shape=[jax.ShapeDtypeStruct((t, HW), F32)] * 5,
        scratch_shapes=[pltpu.VMEM((HEADS, HD, HD), F32)],
        operands=(q, k, v, beta, g, sall, aall, u, w, do), name="dn_bwd")


def _group_norm(a, nw):
    rs = []
    for h in range(HEADS):
        ah = a[:, h * HD:(h + 1) * HD]
        rs.append(jnp.broadcast_to(_rstd(ah), ah.shape))
    r = jnp.concatenate(rs, axis=-1)
    xh = a * r
    return xh * nw, xh, r


def _group_norm_bwd(dy, xh, r, nw):
    dxh = dy * nw
    return r * (dxh - xh * (_head_sum(dxh * xh) * (1.0 / HD)))


def _sg_mix(wt_ref, svn_b, nchunk):
    rows = []
    for cidx in range(nchunk):
        cols = []
        for g in range(HEADS):
            blk = svn_b[cidx * SGC:(cidx + 1) * SGC, g * HD:(g + 1) * HD]
            cols.append(_dot(wt_ref[g], blk))
        rows.append(jnp.concatenate(cols, axis=-1))
    return jnp.concatenate(rows, axis=0)


def _ab_out(x1, o, z, su, sv, dnw, sgnw, wtril, sgb, wout):
    t = x1.shape[0]
    nchunk = TM // SGC

    def body(x_ref, o_ref, z_ref, su_ref, sv_ref, dnw_ref, sgnw_ref, wt_ref, sgb_ref, wo_ref, x2_ref, cat_ref):
        on, _, _ = _group_norm(o_ref[...], dnw_ref[...])
        zv = z_ref[...]
        cat_ref[:, 0:HW] = _c(on * (zv * _sigmoid(zv)))
        svn, _, _ = _group_norm(_gelu(sv_ref[...]), sgnw_ref[...])
        mixed = _sg_mix(wt_ref, _c(svn), nchunk) + jnp.tile(sgb_ref[...], (nchunk, 1))
        cat_ref[:, HW:] = _c(_gelu(su_ref[...]) * mixed)
        x2_ref[...] = x_ref[...] + _dot(cat_ref[...], wo_ref[...])

    row = lambda w: pl.BlockSpec((TM, w), lambda i: (i, 0))
    full = lambda a: pl.BlockSpec(a.shape, lambda i: (0,) * a.ndim)
    return pl.pallas_call(
        body,
        grid=(t // TM,),
        in_specs=[row(D)] + [row(HW)] * 4 + [full(dnw), full(sgnw), full(wtril), full(sgb), full(wout)],
        out_specs=[row(D), row(D)],
        out_shape=[jax.ShapeDtypeStruct((t, D), F32), jax.ShapeDtypeStruct((t, D), _MM)],
        name="ab_out",
    )(x1, o, z, su, sv, dnw, sgnw, wtril, sgb, wout)


def _ab_out_bwd(dx2, o, z, su, sv, dnw, sgnw, wtril, wtril_t, sgb, wout):
    t = dx2.shape[0]
    nchunk = TM // SGC

    def body(dx_ref, o_ref, z_ref, su_ref, sv_ref, dnw_ref, sgnw_ref, wt_ref, wtt_ref, sgb_ref, wo_ref,
             do_ref, dz_ref, dsu_ref, dsv_ref, ddnw_ref, dsgnw_ref, dsgw_ref, dsgb_ref):
        i = pl.program_id(0)
        dcat = _dot_nt(_c(dx_ref[...]), wo_ref[...])
        doa = dcat[:, 0:HW]
        dob = dcat[:, HW:]
        on, oh, ro = _group_norm(o_ref[...], dnw_ref[...])
        zv = z_ref[...]
        sz = _sigmoid(zv)
        dz_ref[...] = _c(doa * on * (sz * (1.0 + zv * (1.0 - sz))))
        don = doa * (zv * sz)
        do_ref[...] = _group_norm_bwd(don, oh, ro, dnw_ref[...])
        dd = jnp.sum(don * oh, axis=0, keepdims=True)
        _accum(ddnw_ref, dd[:, 0:HD] + dd[:, HD:2 * HD] + dd[:, 2 * HD:3 * HD] + dd[:, 3 * HD:], i)
        suv = su_ref[...]
        svv = sv_ref[...]
        svg = _gelu(svv)
        svn, sh, rs = _group_norm(svg, sgnw_ref[...])
        svn_b = _c(svn)
        mixed = _sg_mix(wt_ref, svn_b, nchunk) + jnp.tile(sgb_ref[...], (nchunk, 1))
        dsu_ref[...] = _c(dob * mixed * _gelu_grad(suv))
        dmixed = dob * _gelu(suv)
        dmb = _c(dmixed)
        tri = lax.broadcasted_iota(jnp.int32, (SGC, SGC), 0) >= lax.broadcasted_iota(jnp.int32, (SGC, SGC), 1)
        lane = lax.broadcasted_iota(jnp.int32, (SGC, HD), 1)
        rows = []
        dbias = jnp.zeros((SGC, HD), F32)
        for g in range(HEADS):
            gs = slice(g * HD, (g + 1) * HD)
            dwg = jnp.zeros((SGC, SGC), F32)
            col = jnp.zeros((SGC, 1), F32)
            for cidx in range(nchunk):
                cs = slice(cidx * SGC, (cidx + 1) * SGC)
                dwg = dwg + _dot_nt(dmb[cs, gs], svn_b[cs, gs])
                col = col + jnp.sum(dmixed[cs, gs], axis=-1, keepdims=True)
            _accum(dsgw_ref.at[g], jnp.where(tri, dwg, 0.0), i)
            dbias = dbias + jnp.where(lane == g, col, 0.0)
        _accum(dsgb_ref, dbias, i)
        for cidx in range(nchunk):
            cs = slice(cidx * SGC, (cidx + 1) * SGC)
            rows.append(jnp.concatenate(
                [_dot(wtt_ref[g], dmb[cs, g * HD:(g + 1) * HD]) for g in range(HEADS)], axis=-1))
        dsvn = jnp.concatenate(rows, axis=0)
        _accum(dsgnw_ref, jnp.sum(dsvn * sh, axis=0, keepdims=True), i)
        dsv_ref[...] = _c(_group_norm_bwd(dsvn, sh, rs, sgnw_ref[...]) * _gelu_grad(svv))

    row = lambda w: pl.BlockSpec((TM, w), lambda i: (i, 0))
    full = lambda a: pl.BlockSpec(a.shape, lambda i: (0,) * a.ndim)
    const = lambda shape: pl.BlockSpec(shape, lambda i: (0,) * len(shape))
    return pl.pallas_call(
        body,
        grid=(t // TM,),
        in_specs=[row(D)] + [row(HW)] * 4 + [full(dnw), full(sgnw), full(wtril), full(wtril_t), full(sgb), full(wout)],
        out_specs=[row(HW)] * 4 + [const((1, HD)), const((1, HW)), const((HEADS, SGC, SGC)), const((SGC, HD))],
        out_shape=[jax.ShapeDtypeStruct((t, HW), F32)] + [jax.ShapeDtypeStruct((t, HW), _MM)] * 3
        + [jax.ShapeDtypeStruct((1, HD), F32), jax.ShapeDtypeStruct((1, HW), F32),
           jax.ShapeDtypeStruct((HEADS, SGC, SGC), F32), jax.ShapeDtypeStruct((SGC, HD), F32)],
        name="ab_out_bwd",
    )(dx2, o, z, su, sv, dnw, sgnw, wtril, wtril_t, sgb, wout)


def _dn_pre_bwd(qkv, b_rep, a_rep, cw, alog, dtb, dqn, dkn, dv, dbeta, dg):
    t = qkv.shape[0]
    qscale = HD ** -0.5

    def body(x_ref, halo_ref, b_ref, a_ref, cw_ref, alog_ref, dt_ref, dq_ref, dk_ref, dv_ref, dbeta_ref, dg_ref,
             dc_ref, dba_ref, dcw_ref, dalog_ref, ddt_ref):
        i = pl.program_id(0)
        halo = jnp.where(i == 0, 0.0, halo_ref[...])
        c, shifted = _conv_rows(x_ref[...], halo, cw_ref[...])
        s = _sigmoid(c)
        sc = c * s
        q = sc[:, 0:HW]
        k = sc[:, HW:2 * HW]
        rq = _head_rsq(q)
        rk = _head_rsq(k)
        qu = q * rq
        ku = k * rk
        dqn = dq_ref[...]
        dkn = dk_ref[...]
        dq = qscale * rq * (dqn - qu * _head_sum(dqn * qu))
        dk = rk * (dkn - ku * _head_sum(dkn * ku))
        dsc = jnp.concatenate([dq, dk, dv_ref[...]], axis=-1)
        dc = dsc * (s * (1.0 + c * (1.0 - s)))
        dc_ref[...] = _c(dc)
        for kk in range(4):
            _accum(dcw_ref.at[kk], jnp.sum(dc * shifted[kk], axis=0, keepdims=True), i)
        beta = _sigmoid(b_ref[...])
        dbp = dbeta_ref[...] * beta * (1.0 - beta)
        nea = -jnp.exp(alog_ref[...])
        spin = a_ref[...] + dt_ref[...]
        dgv = dg_ref[...]
        dap = dgv * nea * _sigmoid(spin)
        _accum(dalog_ref, jnp.sum(dgv * nea * _softplus(spin), axis=0, keepdims=True), i)
        _accum(ddt_ref, jnp.sum(dap, axis=0, keepdims=True), i)
        lane = lax.broadcasted_iota(jnp.int32, (TM, HD), 1)
        dba = jnp.zeros((TM, HD), F32)
        for h in range(HEADS):
            dba = dba + jnp.where(lane == h, dbp[:, h * HD:(h + 1) * HD], 0.0)
            dba = dba + jnp.where(lane == HEADS + h, dap[:, h * HD:(h + 1) * HD], 0.0)
        dba_ref[...] = _c(dba)

    row = lambda w: pl.BlockSpec((TM, w), lambda i: (i, 0))
    full = lambda a: pl.BlockSpec(a.shape, lambda i: (0,) * a.ndim)
    const = lambda shape: pl.BlockSpec(shape, lambda i: (0,) * len(shape))
    return pl.pallas_call(
        body,
        grid=(t // TM,),
        in_specs=[row(QKV), _halo_prev_spec(QKV, 8), row(HW), row(HW), full(cw), full(alog), full(dtb)] + [row(HW)] * 5,
        out_specs=[row(QKV), row(HD), const((4, 1, QKV)), const((1, HW)), const((1, HW))],
        out_shape=[jax.ShapeDtypeStruct((t, QKV), _MM), jax.ShapeDtypeStruct((t, HD), _MM),
                   jax.ShapeDtypeStruct((4, 1, QKV), F32), jax.ShapeDtypeStruct((1, HW), F32),
                   jax.ShapeDtypeStruct((1, HW), F32)],
        name="dn_pre_bwd",
    )(qkv, qkv, b_rep, a_rep, cw, alog, dtb, dqn, dkn, dv, dbeta, dg)


def _conv_bwd(dc, cw):
    t = dc.shape[0]
    nt = t // TM

    def body(dc_ref, halo_ref, cw_ref, dx_ref):
        i = pl.program_id(0)
        halo = jnp.where(i == nt - 1, 0.0, halo_ref[...].astype(F32))
        de = jnp.concatenate([dc_ref[...].astype(F32), halo], axis=0)
        cwv = cw_ref[...]
        acc = None
        for k in range(4):
            s = 3 - k
            ds = (de if s == 0 else pltpu.roll(de, TM + HALO - s, 0))[0:TM, :]
            term = cwv[k:k + 1, :] * ds
            acc = term if acc is None else acc + term
        dx_ref[...] = _c(acc)

    return pl.pallas_call(
        body,
        grid=(nt,),
        in_specs=[pl.BlockSpec((TM, QKV), lambda i: (i, 0)), _halo_next_spec(QKV, HALO, t),
                  pl.BlockSpec(cw.shape, lambda i: (0, 0))],
        out_specs=pl.BlockSpec((TM, QKV), lambda i: (i, 0)),
        out_shape=jax.ShapeDtypeStruct((t, QKV), _MM),
        name="conv_bwd",
    )(dc, dc, cw)


def _ab_proj_bwd(x1, nw, dqkv, dz, dsu, dsv, dba, wab_b, dres):
    t = x1.shape[0]

    def body(x_ref, nw_ref, dqkv_ref, dz_ref, dsu_ref, dsv_ref, dba_ref, w_ref, dres_ref, dx_ref, dcat_ref, dnw_ref):
        i = pl.program_id(0)
        dcat_ref[:, 0:QKV] = dqkv_ref[...]
        o = QKV
        for ref in (dz_ref, dsu_ref, dsv_ref):
            dcat_ref[:, o:o + HW] = ref[...]
            o += HW
        dcat_ref[:, o:o + 128] = dba_ref[...]
        dh = _dot_nt(dcat_ref[...], w_ref[...])
        xv = x_ref[...]
        r = _rstd(xv)
        dx, dnw = _rms_bwd(dh, xv * r, r, nw_ref[...])
        dx_ref[...] = dres_ref[...] + dx
        _accum(dnw_ref, dnw, i)

    row = lambda w: pl.BlockSpec((TM, w), lambda i: (i, 0))
    return pl.pallas_call(
        body,
        grid=(t // TM,),
        in_specs=[row(D), pl.BlockSpec((1, D), lambda i: (0, 0)), row(QKV), row(HW), row(HW), row(HW), row(128),
                  pl.BlockSpec((D, PW_B), lambda i: (0, 0)), row(D)],
        out_specs=[row(D), row(PW_B), pl.BlockSpec((1, D), lambda i: (0, 0))],
        out_shape=[jax.ShapeDtypeStruct((t, D), F32), jax.ShapeDtypeStruct((t, PW_B), _MM),
                   jax.ShapeDtypeStruct((1, D), F32)],
        name="ab_proj_bwd",
    )(x1, nw, dqkv, dz, dsu, dsv, dba, wab_b, dres)


def _pool_counts(i):
    pos = (lax.broadcasted_iota(jnp.int32, (TM + HALO, 1), 0) + i * TM + 1).astype(F32)
    return [1.0 / jnp.minimum(pos, float(w)) for w in POOL_WINDOWS]


def _window_sum(ext, win, back):
    r = ext.shape[0]
    s = ext
    step = 1
    while step < win:
        s = s + pltpu.roll(s, step if back else r - step, 0)
        step *= 2
    return s


def _pooled(h_ext, invc, g):
    gs = slice(g * PG, (g + 1) * PG)
    he = h_ext[:, gs]
    ws = _window_sum(he, POOL_WINDOWS[g], True)[HALO:, :]
    return ws * invc[g][0:TM, :] - he[HALO:, :]


def _pool_fwd(x1, nw, pw, scale):
    t = x1.shape[0]

    def body(x_ref, halo_ref, nw_ref, pw_ref, sc_ref, x2_ref):
        i = pl.program_id(0)
        xv = x_ref[...]
        hv = halo_ref[...]
        nwv = nw_ref[...]
        h_ext = jnp.concatenate([jnp.where(i == 0, 0.0, hv * _rstd(hv) * nwv), xv * _rstd(xv) * nwv], axis=0)
        invc = _pool_counts(i)
        outs = [_dot(_c(_pooled(h_ext, invc, g)), pw_ref[g]) for g in range(4)]
        x2_ref[...] = xv + jnp.concatenate(outs, axis=-1) * sc_ref[...]

    return pl.pallas_call(
        body,
        grid=(t // TM,),
        in_specs=[pl.BlockSpec((TM, D), lambda i: (i, 0)), _halo_prev_spec(D, HALO),
                  pl.BlockSpec((1, D), lambda i: (0, 0)), pl.BlockSpec((4, PG, PG), lambda i: (0, 0, 0)),
                  pl.BlockSpec((1, D), lambda i: (0, 0))],
        out_specs=pl.BlockSpec((TM, D), lambda i: (i, 0)),
        out_shape=jax.ShapeDtypeStruct((t, D), F32),
        name="pool_fwd",
    )(x1, x1, nw, pw, scale)


def _pool_bwd(x1, nw, pw, scale, dx2):
    t = x1.shape[0]
    nt = t // TM

    def body(x_ref, halo_ref, nw_ref, pw_ref, sc_ref, dx2_ref, dnext_ref, dx_ref, dnw_ref, dpw_ref, dsc_ref):
        i = pl.program_id(0)
        xv = x_ref[...]
        hv = halo_ref[...]
        nwv = nw_ref[...]
        r = _rstd(xv)
        xh = xv * r
        h_ext = jnp.concatenate([jnp.where(i == 0, 0.0, hv * _rstd(hv) * nwv), xh * nwv], axis=0)
        invc = _pool_counts(i)
        dyv = dx2_ref[...]
        dout_ext = jnp.concatenate([dyv, jnp.where(i == nt - 1, 0.0, dnext_ref[...])], axis=0) * sc_ref[...]
        dh_parts = []
        dsc_parts = []
        for g in range(4):
            gs = slice(g * PG, (g + 1) * PG)
            pooled_b = _c(_pooled(h_ext, invc, g))
            dout_b = _c(dout_ext[:, gs])
            dsc_parts.append(jnp.sum(dyv[:, gs] * _dot(pooled_b, pw_ref[g]), axis=0, keepdims=True))
            _accum(dpw_ref.at[g], _dot_tn(pooled_b, dout_b[0:TM, :]), i)
            dpool_ext = _dot_nt(dout_b, pw_ref[g])
            lead = _window_sum(dpool_ext * invc[g], POOL_WINDOWS[g], False)[0:TM, :]
            dh_parts.append(lead - dpool_ext[0:TM, :])
        _accum(dsc_ref, jnp.concatenate(dsc_parts, axis=-1), i)
        dx, dnw = _rms_bwd(jnp.concatenate(dh_parts, axis=-1), xh, r, nwv)
        dx_ref[...] = dyv + dx
        _accum(dnw_ref, dnw, i)

    vec = pl.BlockSpec((1, D), lambda i: (0, 0))
    return pl.pallas_call(
        body,
        grid=(nt,),
        in_specs=[pl.BlockSpec((TM, D), lambda i: (i, 0)), _halo_prev_spec(D, HALO), vec,
                  pl.BlockSpec((4, PG, PG), lambda i: (0, 0, 0)), vec,
                  pl.BlockSpec((TM, D), lambda i: (i, 0)), _halo_next_spec(D, HALO, t)],
        out_specs=[pl.BlockSpec((TM, D), lambda i: (i, 0)), vec, pl.BlockSpec((4, PG, PG), lambda i: (0, 0, 0)), vec],
        out_shape=[jax.ShapeDtypeStruct((t, D), F32), jax.ShapeDtypeStruct((1, D), F32),
                   jax.ShapeDtypeStruct((4, PG, PG), F32), jax.ShapeDtypeStruct((1, D), F32)],
        name="pool_bwd",
    )(x1, x1, nw, pw, scale, dx2, dx2)


def _adamw(lands, w, m, v, rb, name):
    nl, nr = w.shape[0], w.shape[1]
    rest = w.shape[2:]
    ns = lands[0].shape[0]
    zeros = (0,) * len(rest)

    def body(*refs):
        l_refs = refs[0:nl]
        w_ref, m_ref, v_ref, g_ref, d_ref, m2_ref, v2_ref = refs[nl:]
        for l in range(nl):
            g = l_refs[l][0].astype(F32)
            for s in range(1, ns):
                g = g + l_refs[l][s].astype(F32)
            m2 = ADAM_B1 * m_ref[l] + (1.0 - ADAM_B1) * g
            v2 = ADAM_B2 * v_ref[l] + (1.0 - ADAM_B2) * (g * g)
            m_hat = m2 / (1.0 - ADAM_B1 ** ADAM_STEP)
            v_hat = v2 / (1.0 - ADAM_B2 ** ADAM_STEP)
            g_ref[l] = g
            d_ref[l] = -ADAM_LR * (m_hat / (jnp.sqrt(v_hat) + ADAM_EPS) + ADAM_WD * w_ref[l])
            m2_ref[l] = m2
            v2_ref[l] = v2

    lspec = pl.BlockSpec((ns, rb) + rest, lambda r: (0, r) + zeros)
    wspec = pl.BlockSpec((nl, rb) + rest, lambda r: (0, r) + zeros)
    return pl.pallas_call(
        body,
        grid=(nr // rb,),
        in_specs=[lspec] * nl + [wspec] * 3,
        out_specs=[wspec] * 4,
        out_shape=[jax.ShapeDtypeStruct(w.shape, F32)] * 4,
        name=name,
    )(*lands, w, m, v)


WEIGHT_ORDER = ("ffn_norm1", "ffn1_w_in", "ffn1_w_out", "mix_norm", "ffn_norm2", "ffn2_w_in", "ffn2_w_out", "ab_w_in",
                "dn_conv_w", "dn_a_log", "dn_dt_bias", "dn_out_norm", "sg_norm", "sg_w", "sg_b", "ab_w_out", "pool_w",
                "pool_scale", "final_norm")
R_SMALL = 88
SMALL_ROWS = (
    ("ffn_norm1", (2, D), 2), ("mix_norm", (2, D), 2), ("ffn_norm2", (2, D), 2), ("final_norm", (D,), 1),
    ("sg_w", (1, 4, SGC, SGC), 64), ("sg_norm", (1, 4, HD), 1), ("sg_b", (1, 4, SGC), 1), ("dn_out_norm", (1, HD), 1),
    ("dn_a_log", (1, 4), 1), ("dn_dt_bias", (1, 4), 1), ("pool_scale", (1, D), 1), ("dn_conv_w", (1, 4, QKV), 8),
)
SMALL_SHARDED = ("pool_scale", "dn_conv_w")


def _rows_of(a, rows):
    if a.shape[-1] == QKV:
        return jnp.pad(a.reshape(4, QKV), ((0, 0), (0, 2 * ROW - QKV))).reshape(8, ROW)
    n = _numel(a.shape)
    if n % ROW == 0:
        return a.reshape(n // ROW, ROW)
    return jnp.pad(a.reshape(1, n), ((0, 0), (0, ROW - n)))


def _from_rows(r, shape):
    if shape[-1] == QKV:
        return r.reshape(4, 2 * ROW)[:, 0:QKV].reshape(shape)
    n = _numel(shape)
    if n % ROW == 0:
        return r.reshape(shape)
    return r[:, 0:n].reshape(shape)


def _pack_small(vals):
    parts = [(_rows_of(vals[n].astype(F32), r) if n in vals else jnp.zeros((r, ROW), F32)) for n, _, r in SMALL_ROWS]
    used = sum(r for _, _, r in SMALL_ROWS)
    return jnp.concatenate(parts + [jnp.zeros((R_SMALL - used, ROW), F32)], axis=0)


def _unpack_small(packed):
    out, o = {}, 0
    for n, shape, r in SMALL_ROWS:
        out[n] = _from_rows(packed[o:o + r], shape)
        o += r
    return out


def _pack_small_shard(ps, cw):
    return jnp.concatenate([
        jnp.pad(ps, ((0, 0), (0, ROW - D // N_DEV))), jnp.pad(cw[0], ((0, 0), (0, ROW - QKV // N_DEV))),
        jnp.zeros((3, ROW), F32)], axis=0)


def _mixer_weights(g_in, g_out, g_small, small):
    w = {}
    wi = jnp.transpose(g_in, (1, 0, 2)).reshape(D, AB_IN)
    gates = wi[:, AB_MAIN:AB_MAIN + AB_GATES]
    main = [wi[:, 0:AB_MAIN], wi[:, AB_MAIN + AB_GATES:AB_IN]]
    w["wab_f"] = jnp.concatenate(
        main + [jnp.repeat(gates[:, 0:HEADS], HD, axis=1), jnp.repeat(gates[:, HEADS:], HD, axis=1)], axis=1)
    w["wab_b"] = jnp.concatenate(main + [gates, jnp.zeros((D, 128 - AB_GATES), wi.dtype)], axis=1)
    w["cw"] = jnp.transpose(g_small[:, 1:5, 0:QKV // N_DEV], (1, 0, 2)).reshape(4, QKV)
    w["ps"] = g_small[:, 0, 0:D // N_DEV].reshape(1, D)
    w["alog"] = jnp.repeat(small["dn_a_log"][0].astype(F32), HD).reshape(1, HW)
    w["dtb"] = jnp.repeat(small["dn_dt_bias"][0].astype(F32), HD).reshape(1, HW)
    w["dnw"] = jnp.tile(small["dn_out_norm"][0].astype(F32), HEADS).reshape(1, HW)
    w["sgnw"] = small["sg_norm"][0].astype(F32).reshape(1, HW)
    tri = jnp.tril(jnp.ones((SGC, SGC), dtype=bool))
    wt = jnp.where(tri, small["sg_w"][0].astype(F32), 0.0)
    w["wtril"] = _c(wt)
    w["wtril_t"] = _c(jnp.transpose(wt, (0, 2, 1)))
    w["sgb"] = jnp.repeat(jnp.transpose(small["sg_b"][0].astype(F32)), HD, axis=1)
    w["wout_ab"] = g_out.reshape(D, D)
    return w


def kernel(x, ffn_norm1, ffn1_w_in, ffn1_w_out, mix_norm, ffn_norm2, ffn2_w_in, ffn2_w_out, ab_w_in, dn_conv_w, dn_a_log, dn_dt_bias, dn_out_norm, sg_norm, sg_w, sg_b, ab_w_out, pool_w, pool_scale, final_norm, loss_target, m_ffn_norm1, m_ffn1_w_in, m_ffn1_w_out, m_mix_norm, m_ffn_norm2, m_ffn2_w_in, m_ffn2_w_out, m_ab_w_in, m_dn_conv_w, m_dn_a_log, m_dn_dt_bias, m_dn_out_norm, m_sg_norm, m_sg_w, m_sg_b, m_ab_w_out, m_pool_w, m_pool_scale, m_final_norm, v_ffn_norm1, v_ffn1_w_in, v_ffn1_w_out, v_mix_norm, v_ffn_norm2, v_ffn2_w_in, v_ffn2_w_out, v_ab_w_in, v_dn_conv_w, v_dn_a_log, v_dn_dt_bias, v_dn_out_norm, v_sg_norm, v_sg_w, v_sg_b, v_ab_w_out, v_pool_w, v_pool_scale, v_final_norm):
    wl = dict(ffn_norm1=ffn_norm1, mix_norm=mix_norm, ffn_norm2=ffn_norm2, dn_a_log=dn_a_log, dn_dt_bias=dn_dt_bias,
              dn_out_norm=dn_out_norm, sg_norm=sg_norm, sg_w=sg_w, sg_b=sg_b, final_norm=final_norm)
    ml = dict(ffn_norm1=m_ffn_norm1, mix_norm=m_mix_norm, ffn_norm2=m_ffn_norm2, dn_a_log=m_dn_a_log,
              dn_dt_bias=m_dn_dt_bias, dn_out_norm=m_dn_out_norm, sg_norm=m_sg_norm, sg_w=m_sg_w, sg_b=m_sg_b,
              final_norm=m_final_norm)
    vl = dict(ffn_norm1=v_ffn_norm1, mix_norm=v_mix_norm, ffn_norm2=v_ffn_norm2, dn_a_log=v_dn_a_log,
              dn_dt_bias=v_dn_dt_bias, dn_out_norm=v_dn_out_norm, sg_norm=v_sg_norm, sg_w=v_sg_w, sg_b=v_sg_b,
              final_norm=v_final_norm)
    row = lambda a: a.reshape(1, -1).astype(F32)
    n1 = [row(ffn_norm1[l]) for l in range(2)]
    n2 = [row(ffn_norm2[l]) for l in range(2)]
    mix = [row(mix_norm[l]) for l in range(2)]
    s_in = {(f, l): _c(wf[l]) for f, wf in enumerate((ffn1_w_in, ffn2_w_in)) for l in range(2)}
    s_out = {(f, l): _c(wf[l]) for f, wf in enumerate((ffn1_w_out, ffn2_w_out)) for l in range(2)}
    xs, tgt = x[0], loss_target[0]

    wi00, wo00 = _comm_call(_Comm("gather", [s_in[0, 0], s_out[0, 0]]), "gather_first")
    x01, gu00, (g_abin, g_about, g_small, wi10) = _ffn_fwd(
        xs, n1[0], wi00, wo00,
        comm=_Comm("gather", [_c(ab_w_in[0]), _c(ab_w_out[0]), _pack_small_shard(pool_scale, dn_conv_w), s_in[1, 0]]))
    w = _mixer_weights(g_abin, g_about, g_small, wl)
    h, qkv, z, su, sv, b_rep, a_rep = _ab_proj(x01, mix[0], w["wab_f"])
    qn, kn, v, beta, g = _dn_pre(qkv, b_rep, a_rep, w["cw"], w["alog"], w["dtb"])
    (o, sall, aall, dn_u, dn_w), (wo10, g_pw) = _dn_fwd(
        qn, kn, v, beta, g, comm=_Comm("gather", [s_out[1, 0], _c(pool_w[0])]))
    pw = jnp.transpose(g_pw, (1, 0, 2, 3)).reshape(4, PG, PG)
    x02, cat = _ab_out(x01, o, z, su, sv, w["dnw"], w["sgnw"], w["wtril"], w["sgb"], w["wout_ab"])
    x10, gu10, (wi01, wo01) = _ffn_fwd(x02, n2[0], wi10, wo10, comm=_Comm("gather", [s_in[0, 1], s_out[0, 1]]))
    x11, gu01, (wi11, wo11) = _ffn_fwd(x10, n1[1], wi01, wo01, comm=_Comm("gather", [s_in[1, 1], s_out[1, 1]]))
    x12 = _pool_fwd(x11, mix[1], pw, w["ps"])
    x13, gu11, _ = _ffn_fwd(x12, n2[1], wi11, wo11)
    loss_local, dx, d_fn = _loss_head(x13, row(final_norm), tgt)

    bt = min(BT, xs.shape[0])

    def ffn_b(xin, nw, w_in, w_out, gu, dy, comm=None):
        (dxn, xn, act, dh, dnw, dyb), landed = _ffn_bwd(xin, nw, w_in, w_out, gu, dy, comm)
        return dxn, dnw, (xn, act, dh, dyb), landed

    def ffn_g(kept, dy):
        return [_mm_tn_win(kept[0], kept[2])[0], _mm_tn_wout(kept[1], kept[3])]

    dy = dx
    dx, d_n2_1, kept, _ = ffn_b(x12, n2[1], wi11, wo11, gu11, dy)
    g11 = ffn_g(kept, dy)
    dx, d_mix_1, d_pw, d_ps = _pool_bwd(x11, mix[1], pw, w["ps"], dx)
    d_pw_sh = _c(jnp.transpose(d_pw.reshape(4, N_DEV, PG // N_DEV, PG), (1, 0, 2, 3)))
    dy = dx
    dx, d_n1_1, kept, land11 = ffn_b(x10, n1[1], wi01, wo01, gu01, dy, _Comm("exchange", g11))
    g01 = ffn_g(kept, dy)
    dy = dx
    dx, d_n2_0, kept, land01 = ffn_b(x02, n2[0], wi10, wo10, gu10, dy, _Comm("exchange", g01 + [d_pw_sh]))
    g10 = ffn_g(kept, dy)
    do, dz, dsu, dsv, d_dnw, d_sgnw, d_sgw, d_sgb = _ab_out_bwd(
        dx, o, z, su, sv, w["dnw"], w["sgnw"], w["wtril"], w["wtril_t"], w["sgb"], w["wout_ab"])
    d_about = _mm_tn(cat, dx, D, D, bt, _MM, "mm_tn_about").reshape(N_DEV, D // N_DEV, D)
    (dqn, dkn, dv, dbeta, dg), _ = _dn_bwd(qn, kn, v, beta, g, sall, aall, dn_u, dn_w, do)
    dc, dba, d_cw, d_alog, d_dtb = _dn_pre_bwd(qkv, b_rep, a_rep, w["cw"], w["alog"], w["dtb"], dqn, dkn, dv, dbeta, dg)
    dqkv = _conv_bwd(dc, w["cw"])
    dx, dcat, d_mix_0 = _ab_proj_bwd(x01, mix[0], dqkv, dz, dsu, dsv, dba, w["wab_b"], dx)
    d_wab = _mm_tn(h, dcat, D, 640, bt, _MM, "mm_tn_abin")
    rest = PW_B - 128
    d_abin = jnp.concatenate([d_wab[:, 0:AB_MAIN], d_wab[:, rest:rest + AB_GATES], d_wab[:, AB_MAIN:rest]], axis=1)
    d_abin_sh = jnp.transpose(d_abin.reshape(D, N_DEV, AB_IN // N_DEV), (1, 0, 2))
    dy = dx
    grad_x, d_n1_0, kept, land_mid = ffn_b(xs, n1[0], wi00, wo00, gu00, dy,
                                           _Comm("exchange", g10 + [d_abin_sh, d_about]))
    land10, land_ab = land_mid[0:2], land_mid[2:4]

    g_small = {
        "ffn_norm1": jnp.concatenate([d_n1_0, d_n1_1], axis=0),
        "mix_norm": jnp.concatenate([d_mix_0, d_mix_1], axis=0),
        "ffn_norm2": jnp.concatenate([d_n2_0, d_n2_1], axis=0),
        "dn_conv_w": d_cw.reshape(1, 4, QKV),
        "dn_a_log": d_alog[:, ::HD],
        "dn_dt_bias": d_dtb[:, ::HD],
        "dn_out_norm": d_dnw,
        "sg_norm": d_sgnw.reshape(1, HEADS, HD),
        "sg_w": d_sgw[None],
        "sg_b": jnp.transpose(d_sgb[:, 0:HEADS])[None],
        "pool_scale": d_ps,
        "final_norm": d_fn.reshape(D),
    }
    g00_out = _mm_tn_wout(kept[1], kept[3])
    g00_in, (land00_out, land_small) = _mm_tn_win(
        kept[0], kept[2], comm=_Comm("exchange", [g00_out], repl=[_pack_small(g_small)]))
    (land00_in,) = _comm_call(_Comm("exchange", [g00_in]), "exchange_last")

    res = {}
    tr = lambda a: jnp.swapaxes(a, 1, 2)
    res["ffn1_w_in"] = [tr(a) for a in _adamw([land00_in, land01[0]], tr(ffn1_w_in), tr(m_ffn1_w_in), tr(v_ffn1_w_in),
                                              176, "adamw_w_in")]
    res["ffn2_w_in"] = [tr(a) for a in _adamw([land10[0], land11[0]], tr(ffn2_w_in), tr(m_ffn2_w_in), tr(v_ffn2_w_in),
                                              176, "adamw_w_in")]
    res["ffn1_w_out"] = _adamw([land00_out, land01[1]], ffn1_w_out, m_ffn1_w_out, v_ffn1_w_out, 176, "adamw_w_out")
    res["ffn2_w_out"] = _adamw([land10[1], land11[1]], ffn2_w_out, m_ffn2_w_out, v_ffn2_w_out, 176, "adamw_w_out")
    res["ab_w_in"] = _adamw([land_ab[0]], ab_w_in, m_ab_w_in, v_ab_w_in, 256, "adamw_ab_w_in")
    res["ab_w_out"] = _adamw([land_ab[1]], ab_w_out, m_ab_w_out, v_ab_w_out, D // N_DEV, "adamw_ab_w_out")
    res["pool_w"] = _adamw([land01[2]], pool_w, m_pool_w, v_pool_w, 4, "adamw_pool_w")
    sm = _adamw([land_small], _pack_small(wl)[None], _pack_small(ml)[None], _pack_small(vl)[None], R_SMALL,
                "adamw_replicated")
    sm = [_unpack_small(a[0]) for a in sm]
    for n in wl:
        res[n] = [d[n] for d in sm]
    me = 4 * lax.axis_index("x") + 2 * lax.axis_index("y") + lax.axis_index("c")
    g_ps = lax.dynamic_slice(sm[0]["pool_scale"], (0, me * (D // N_DEV)), (1, D // N_DEV))
    g_cw = lax.dynamic_slice(sm[0]["dn_conv_w"], (0, 0, me * (QKV // N_DEV)), (1, 4, QKV // N_DEV))
    s2 = _adamw([_pack_small_shard(g_ps, g_cw)[None]], _pack_small_shard(pool_scale, dn_conv_w)[None],
                _pack_small_shard(m_pool_scale, m_dn_conv_w)[None], _pack_small_shard(v_pool_scale, v_dn_conv_w)[None],
                8, "adamw_small_sharded")
    res["pool_scale"] = [a[0, 0:1, 0:D // N_DEV] for a in s2]
    res["dn_conv_w"] = [a[0, 1:5, 0:QKV // N_DEV][None] for a in s2]

    loss = lax.psum(loss_local[0, 0], ("x", "y", "c"))
    result = [loss, grad_x[None]]
    for i in range(4):
        result += [res[n][i] for n in WEIGHT_ORDER]
    return tuple(result)
```

```python
import jax
import jax.numpy as jnp
from jax import lax
from jax.experimental import pallas as pl
from jax.experimental.pallas import tpu as pltpu

F32 = jnp.float32
_MM = jnp.bfloat16

D = 1024
FF = 2816
EPS = 1e-6
HEADS = 4
HD = 128
DNC = 64
DN_STEP = 8
SGC = 128
QKV = 3 * HEADS * HD
HW = HEADS * HD
POOL_WINDOWS = (2, 4, 8, 16)
PG = D // 4
HALO = 16
N_DEV = 8
AB_IN = 3080
ROW = 1024

TM = 512
BT = 2048
BT_WIN = 4096
FT = 512
FWD_CHUNKS = 2
BWD_CHUNKS = 1
FC = 704
NJ = FF // FC
WO_ROWS = FF // N_DEV

ADAM_LR, ADAM_B1, ADAM_B2, ADAM_EPS, ADAM_WD, ADAM_STEP = 0.001, 0.9, 0.999, 1e-08, 0.01, 10

MESH_T = pl.DeviceIdType.MESH
NN = (((1,), (0,)), ((), ()))
NT = (((1,), (1,)), ((), ()))
TN = (((0,), (0,)), ((), ()))


def _c(a):
    return a.astype(_MM)


def _dg(a, b, dims):
    return lax.dot_general(a, b, dims, preferred_element_type=F32)


def _dot(a, b):
    return _dg(a, b, NN)


def _dot_nt(a, b):
    return _dg(a, b, NT)


def _dot_tn(a, b):
    return _dg(a, b, TN)


def _split2(a):
    hi = _c(a)
    return hi, _c(a - hi.astype(F32))


def _dot3(a, b, dims=NN):
    ah, al = _split2(a)
    bh, bl = _split2(b)
    return _dg(ah, bh, dims) + (_dg(ah, bl, dims) + _dg(al, bh, dims))


def _mask_dot(mask, x):
    x1 = _c(x)
    r = x - x1.astype(F32)
    x2 = _c(r)
    x3 = _c(r - x2.astype(F32))
    return _dot(mask, x1) + (_dot(mask, x2) + _dot(mask, x3))


def _sigmoid(x):
    return jax.nn.sigmoid(x)


def _gelu(x):
    return 0.5 * x * (1.0 + lax.erf(x * 0.7071067811865476))


def _gelu_grad(x):
    return 0.5 * (1.0 + lax.erf(x * 0.7071067811865476)) + x * jnp.exp(-0.5 * x * x) * 0.3989422804014327


def _accum(ref, val, step):
    @pl.when(step == 0)
    def _():
        ref[...] = val

    @pl.when(step > 0)
    def _():
        ref[...] += val


def _rstd(x):
    return lax.rsqrt(jnp.mean(x * x, axis=-1, keepdims=True) + EPS)


def _rms_bwd(dy, xhat, r, nw):
    dnw = jnp.sum(dy * xhat, axis=0, keepdims=True)
    dxh = dy * nw
    dx = r * (dxh - xhat * jnp.mean(dxh * xhat, axis=-1, keepdims=True))
    return dx, dnw


def _numel(shape):
    n = 1
    for s in shape:
        n *= s
    return n


def _peer(k, x, y, c):
    px = 1 - x if k & 4 else x
    py = 1 - y if k & 2 else y
    pc = 1 - c if k & 1 else c
    return px, py, pc


class _Comm:
    def __init__(self, kind, arrs, repl=()):
        self.kind = kind
        self.ns = len(arrs)
        self.arrs = list(arrs) + list(repl)
        self.na = len(self.arrs)

    @property
    def out_shape(self):
        out = []
        for i, a in enumerate(self.arrs):
            lead = (N_DEV,) if (self.kind == "gather" or i >= self.ns) else ()
            out.append(jax.ShapeDtypeStruct(lead + a.shape, a.dtype))
        return out

    @property
    def scratch(self):
        return [pltpu.SemaphoreType.DMA((7 * self.na,)), pltpu.SemaphoreType.DMA((7 * self.na,)),
                pltpu.SemaphoreType.DMA((self.na,))]

    def phases(self, ins, outs, sems):
        send_sems, recv_sems, local_sems = sems
        na = self.na
        x, y, c = lax.axis_index("x"), lax.axis_index("y"), lax.axis_index("c")
        if self.kind == "gather":
            me, sibling = (x, y, c), (x, y, 1 - c)
            chips = [(1 - x, y), (x, 1 - y), (1 - x, 1 - y)]

            def slot(a, px, py, pc):
                return outs[a].at[4 * px + 2 * py + pc]

            def copy(a, k, block, to, src=None):
                return pltpu.make_async_remote_copy(
                    src_ref=slot(a, *block) if src is None else src, dst_ref=slot(a, *block),
                    send_sem=send_sems.at[7 * a + k], recv_sem=recv_sems.at[7 * a + k],
                    device_id=to, device_id_type=MESH_T)

            mine = [pltpu.make_async_copy(ins[a], slot(a, *me), local_sems.at[a]) for a in range(na)]
            first, passed = [], []
            for a in range(na):
                first.append(copy(a, 0, me, sibling, src=ins[a]))
                first += [copy(a, 1 + j, me, (*chip, c), src=ins[a]) for j, chip in enumerate(chips)]
                passed += [copy(a, 4 + j, (*chip, c), sibling) for j, chip in enumerate(chips)]

            def start():
                for cp in mine + first:
                    cp.start()

            def middle():
                for a in range(na):
                    for j, chip in enumerate(chips):
                        copy(a, 1 + j, (*chip, c), me).wait_recv()
                        passed[3 * a + j].start()

            def finish():
                for a in range(na):
                    copy(a, 0, sibling, me).wait_recv()
                    for j, chip in enumerate(chips):
                        copy(a, 4 + j, (*chip, 1 - c), me).wait_recv()
                for cp in first + passed:
                    cp.wait_send()
                for cp in mine:
                    cp.wait()

            return start, middle, finish

        me = 4 * x + 2 * y + c
        ns = self.ns
        own = [pltpu.make_async_copy(ins[a].at[me] if a < ns else ins[a], outs[a].at[me], local_sems.at[a])
               for a in range(na)]
        copies = []
        for k in range(1, N_DEV):
            px, py, pc = _peer(k, x, y, c)
            peer = 4 * px + 2 * py + pc
            for a in range(na):
                copies.append(pltpu.make_async_remote_copy(
                    src_ref=ins[a].at[peer] if a < ns else ins[a], dst_ref=outs[a].at[me],
                    send_sem=send_sems.at[na * (k - 1) + a], recv_sem=recv_sems.at[na * (k - 1) + a],
                    device_id=(px, py, pc), device_id_type=MESH_T))

        def start():
            for cp in own + copies:
                cp.start()

        def middle():
            pass

        def finish():
            for cp in copies:
                cp.wait()
            for cp in own:
                cp.wait()

        return start, middle, finish


def _comm_call(comm, name):
    na = comm.na

    def body(*refs):
        start, middle, finish = comm.phases(refs[0:na], refs[na:2 * na], refs[2 * na:])
        start()
        middle()
        finish()

    hbm = pl.BlockSpec(memory_space=pltpu.HBM)
    return pl.pallas_call(
        body, out_shape=comm.out_shape, in_specs=[hbm] * na, out_specs=[hbm] * na, scratch_shapes=comm.scratch,
        name=name)(*comm.arrs)


def _carried_call(body, comm, n_in, n_out, n_scr, when, *, grid, in_specs, out_specs, out_shape, scratch_shapes,
                  operands, name):
    if comm is None:
        return pl.pallas_call(body, grid=grid, in_specs=in_specs, out_specs=out_specs, out_shape=out_shape,
                              scratch_shapes=scratch_shapes, name=name)(*operands), []
    na = comm.na

    def both(*refs):
        a = n_in + na
        b = a + n_out + na
        body(*refs[0:n_in], *refs[a:a + n_out], *refs[b:b + n_scr])
        start, middle, finish = comm.phases(refs[n_in:a], refs[a + n_out:b], refs[b + n_scr:])
        first, mid, last = when()
        pl.when(first)(start)
        pl.when(mid)(middle)
        pl.when(last)(finish)

    hbm = pl.BlockSpec(memory_space=pltpu.HBM)
    res = pl.pallas_call(
        both, grid=grid, in_specs=list(in_specs) + [hbm] * na, out_specs=list(out_specs) + [hbm] * na,
        out_shape=list(out_shape) + comm.out_shape, scratch_shapes=list(scratch_shapes) + comm.scratch,
        name=name)(*operands, *comm.arrs)
    return res[0:n_out], res[n_out:]


def _ffn_fwd(x, nw, w_in, w_out, comm=None):
    t = x.shape[0]
    tm = min(FT, t)
    nt = t // tm

    nj = NJ // FWD_CHUNKS

    def body(x_ref, nw_ref, wg_ref, wu_ref, wo3_ref, o_ref, gu_ref, xn_sc, acc_sc):
        j = pl.program_id(1)

        @pl.when(j == 0)
        def _():
            xv = x_ref[...]
            xn_sc[...] = _c(xv * _rstd(xv) * nw_ref[...])
            acc_sc[...] = jnp.zeros_like(acc_sc)

        xn = xn_sc[...]
        part = None
        for cc in range(FWD_CHUNKS):
            g = _dot(xn, wg_ref[cc])
            u = _dot(xn, wu_ref[cc])
            gu_ref[cc, 0] = _c(g)
            gu_ref[cc, 1] = _c(u)
            p = _dot(_c(g * _sigmoid(g) * u), wo3_ref[2 * cc:2 * cc + 2].reshape(FC, D))
            part = p if part is None else part + p
        acc_sc[...] += part

        @pl.when(j == nj - 1)
        def _():
            o_ref[...] = x_ref[...] + 0.5 * acc_sc[...]

    def when():
        i, j = pl.program_id(0), pl.program_id(1)
        return ((i == 0) & (j == 0), (i == (3 * nt) // 4) & (j == 0), (i == nt - 1) & (j == nj - 1))

    (out, gu), landed = _carried_call(
        body, comm, 5, 2, 2, when,
        grid=(nt, nj),
        in_specs=[pl.BlockSpec((tm, D), lambda i, j: (i, 0)), pl.BlockSpec((1, D), lambda i, j: (0, 0)),
                  pl.BlockSpec((FWD_CHUNKS, D, FC), lambda i, j: (j, 0, 0)),
                  pl.BlockSpec((FWD_CHUNKS, D, FC), lambda i, j: (j + nj, 0, 0)),
                  pl.BlockSpec((2 * FWD_CHUNKS, WO_ROWS, D), lambda i, j: (j, 0, 0))],
        out_specs=[pl.BlockSpec((tm, D), lambda i, j: (i, 0)),
                   pl.BlockSpec((FWD_CHUNKS, 2, tm, FC), lambda i, j: (j, 0, i, 0))],
        out_shape=[jax.ShapeDtypeStruct((t, D), F32), jax.ShapeDtypeStruct((NJ, 2, t, FC), _MM)],
        scratch_shapes=[pltpu.VMEM((tm, D), _MM), pltpu.VMEM((tm, D), F32)],
        operands=(x, nw, w_in, w_in, w_out), name="ffn_fwd")
    return out, gu, landed


def _ffn_bwd(x, nw, w_in, w_out, gu, dy, comm=None):
    t = x.shape[0]
    nt = t // TM
    nc = BWD_CHUNKS
    nj = NJ // nc

    def body(x_ref, nw_ref, wg_ref, wu_ref, wo3_ref, gu_ref, dy_ref, dx_ref, xn_ref, a_ref, dh_ref, dnw_ref, dyb_ref,
             r_sc, acc_sc):
        i = pl.program_id(0)
        j = pl.program_id(1)

        @pl.when(j == 0)
        def _():
            xv = x_ref[...]
            r = _rstd(xv)
            r_sc[...] = r
            xn_ref[...] = _c(xv * r * nw_ref[...])
            dyb_ref[...] = _c(0.5 * dy_ref[...])
            acc_sc[...] = jnp.zeros_like(acc_sc)

        part = None
        for cc in range(nc):
            g = gu_ref[cc, 0].astype(F32)
            u = gu_ref[cc, 1].astype(F32)
            s = _sigmoid(g)
            sl = g * s
            a_ref[cc] = _c(sl * u)
            da = _dot_nt(dyb_ref[...], wo3_ref[2 * cc:2 * cc + 2].reshape(FC, D))
            dg = _c(da * u * (s * (1.0 + g * (1.0 - s))))
            du = _c(da * sl)
            dh_ref[cc, 0] = dg
            dh_ref[cc, 1] = du
            p = _dot_nt(dg, wg_ref[cc]) + _dot_nt(du, wu_ref[cc])
            part = p if part is None else part + p
        acc_sc[...] += part

        @pl.when(j == nj - 1)
        def _():
            r = r_sc[...]
            dx, dnw = _rms_bwd(acc_sc[...], x_ref[...] * r, r, nw_ref[...])
            dx_ref[...] = dy_ref[...] + dx
            _accum(dnw_ref, dnw, i)

    def when():
        i, j = pl.program_id(0), pl.program_id(1)
        return ((i == 0) & (j == 0), (i == (3 * nt) // 4) & (j == 0), (i == nt - 1) & (j == nj - 1))

    return _carried_call(
        body, comm, 7, 6, 2, when,
        grid=(nt, nj),
        in_specs=[pl.BlockSpec((TM, D), lambda i, j: (i, 0)), pl.BlockSpec((1, D), lambda i, j: (0, 0)),
                  pl.BlockSpec((nc, D, FC), lambda i, j: (j, 0, 0)),
                  pl.BlockSpec((nc, D, FC), lambda i, j: (j + nj, 0, 0)),
                  pl.BlockSpec((2 * nc, WO_ROWS, D), lambda i, j: (j, 0, 0)),
                  pl.BlockSpec((nc, 2, TM, FC), lambda i, j: (j, 0, i, 0)),
                  pl.BlockSpec((TM, D), lambda i, j: (i, 0))],
        out_specs=[
            pl.BlockSpec((TM, D), lambda i, j: (i, 0)),
            pl.BlockSpec((TM, D), lambda i, j: (i, 0)),
            pl.BlockSpec((nc, TM, FC), lambda i, j: (j, i, 0)),
            pl.BlockSpec((nc, 2, TM, FC), lambda i, j: (j, 0, i, 0)),
            pl.BlockSpec((1, D), lambda i, j: (0, 0)),
            pl.BlockSpec((TM, D), lambda i, j: (i, 0)),
        ],
        out_shape=[
            jax.ShapeDtypeStruct((t, D), F32),
            jax.ShapeDtypeStruct((t, D), _MM),
            jax.ShapeDtypeStruct((NJ, t, FC), _MM),
            jax.ShapeDtypeStruct((NJ, 2, t, FC), _MM),
            jax.ShapeDtypeStruct((1, D), F32),
            jax.ShapeDtypeStruct((t, D), _MM),
        ],
        scratch_shapes=[pltpu.VMEM((TM, 1), F32), pltpu.VMEM((TM, D), F32)],
        operands=(x, nw, w_in, w_in, w_out, gu, dy), name="ffn_bwd")


def _mm_tn(a, b, bm, bn, bt, out_dtype, name):
    t, m = a.shape
    n = b.shape[1]
    nt = t // bt

    def body(a_ref, b_ref, o_ref, acc_sc):
        k = pl.program_id(2)
        _accum(acc_sc, _dot_tn(_c(a_ref[...]), _c(b_ref[...])), k)

        @pl.when(k == nt - 1)
        def _():
            o_ref[...] = acc_sc[...].astype(out_dtype)

    return pl.pallas_call(
        body,
        grid=(m // bm, n // bn, nt),
        in_specs=[pl.BlockSpec((bt, bm), lambda i, j, k: (k, i)), pl.BlockSpec((bt, bn), lambda i, j, k: (k, j))],
        out_specs=pl.BlockSpec((bm, bn), lambda i, j, k: (i, j)),
        out_shape=jax.ShapeDtypeStruct((m, n), out_dtype),
        scratch_shapes=[pltpu.VMEM((bm, bn), F32)],
        name=name,
    )(a, b)


def _mm_tn_win(xn, dh, comm=None):
    t = xn.shape[0]
    bt = min(BT_WIN, t)
    nt = t // bt

    def body(a_ref, b_ref, o_ref, acc_sc):
        k = pl.program_id(2)
        _accum(acc_sc, _dot_tn(b_ref[...], a_ref[...]), k)

        @pl.when(k == nt - 1)
        def _():
            o_ref[...] = _c(acc_sc[...])

    def when():
        h, j, k = pl.program_id(0), pl.program_id(1), pl.program_id(2)
        start = (h == 0) & (j == 0) & (k == 0)
        return start, start, (h == 1) & (j == NJ - 1) & (k == nt - 1)

    (out,), landed = _carried_call(
        body, comm, 2, 1, 1, when,
        grid=(2, NJ, nt),
        in_specs=[pl.BlockSpec((bt, D), lambda h, j, k: (k, 0)),
                  pl.BlockSpec((None, None, bt, FC), lambda h, j, k: (j, h, k, 0))],
        out_specs=[pl.BlockSpec((None, FC, D), lambda h, j, k: (h * NJ + j, 0, 0))],
        out_shape=[jax.ShapeDtypeStruct((N_DEV, FC, D), _MM)],
        scratch_shapes=[pltpu.VMEM((FC, D), F32)],
        operands=(xn, dh), name="mm_tn_win")
    return out, landed


def _mm_tn_wout(act, dy):
    t = dy.shape[0]
    bt = min(BT_WIN, t)
    nt = t // bt

    def body(a_ref, b_ref, o_ref, acc_sc):
        k = pl.program_id(1)
        _accum(acc_sc, _dot_tn(a_ref[...], b_ref[...]), k)

        @pl.when(k == nt - 1)
        def _():
            o_ref[...] = _c(acc_sc[...].reshape(2, WO_ROWS, D))

    return pl.pallas_call(
        body,
        grid=(NJ, nt),
        in_specs=[pl.BlockSpec((None, bt, FC), lambda j, k: (j, k, 0)), pl.BlockSpec((bt, D), lambda j, k: (k, 0))],
        out_specs=pl.BlockSpec((2, WO_ROWS, D), lambda j, k: (j, 0, 0)),
        out_shape=jax.ShapeDtypeStruct((N_DEV, WO_ROWS, D), _MM),
        scratch_shapes=[pltpu.VMEM((FC, D), F32)],
        name="mm_tn_wout",
    )(act, dy)


def _loss_head(x, nw, tgt):
    t = x.shape[0]

    def body(x_ref, nw_ref, t_ref, loss_ref, dx_ref, dnw_ref):
        i = pl.program_id(0)
        xv = x_ref[...]
        r = _rstd(xv)
        xh = xv * r
        e = xh * nw_ref[...] - t_ref[...]
        part = 0.5 * jnp.sum(jnp.mean(e * e, axis=-1, keepdims=True), axis=0, keepdims=True)
        _accum(loss_ref, jnp.broadcast_to(part, (1, 128)), i)
        dx, dnw = _rms_bwd(e * (1.0 / D), xh, r, nw_ref[...])
        dx_ref[...] = dx
        _accum(dnw_ref, dnw, i)

    return pl.pallas_call(
        body,
        grid=(t // TM,),
        in_specs=[pl.BlockSpec((TM, D), lambda i: (i, 0)), pl.BlockSpec((1, D), lambda i: (0, 0)),
                  pl.BlockSpec((TM, D), lambda i: (i, 0))],
        out_specs=[pl.BlockSpec((1, 128), lambda i: (0, 0)), pl.BlockSpec((TM, D), lambda i: (i, 0)),
                   pl.BlockSpec((1, D), lambda i: (0, 0))],
        out_shape=[jax.ShapeDtypeStruct((1, 128), F32), jax.ShapeDtypeStruct((t, D), F32),
                   jax.ShapeDtypeStruct((1, D), F32)],
        name="loss_head",
    )(x, nw, tgt)


PW_F = QKV + 5 * HW
PW_B = QKV + 3 * HW + 128
AB_MAIN = QKV + HW
AB_GATES = 2 * HEADS


def _ab_proj(x1, nw, wab):
    t = x1.shape[0]

    def body(x_ref, nw_ref, w_ref, h_ref, qkv_ref, z_ref, su_ref, sv_ref, b_ref, a_ref):
        xv = x_ref[...]
        h = _c(xv * _rstd(xv) * nw_ref[...])
        h_ref[...] = h
        p = _dot(h, w_ref[...])
        qkv_ref[...] = p[:, 0:QKV]
        o = QKV
        for ref in (z_ref, su_ref, sv_ref, b_ref, a_ref):
            ref[...] = p[:, o:o + HW]
            o += HW

    row = lambda w: pl.BlockSpec((TM, w), lambda i: (i, 0))
    return pl.pallas_call(
        body,
        grid=(t // TM,),
        in_specs=[row(D), pl.BlockSpec((1, D), lambda i: (0, 0)), pl.BlockSpec((D, PW_F), lambda i: (0, 0))],
        out_specs=[row(D), row(QKV)] + [row(HW)] * 5,
        out_shape=[jax.ShapeDtypeStruct((t, D), _MM), jax.ShapeDtypeStruct((t, QKV), F32)]
        + [jax.ShapeDtypeStruct((t, HW), F32)] * 5,
        name="ab_proj",
    )(x1, nw, wab)


def _conv_rows(x, halo, cw):
    xe = jnp.concatenate([halo, x], axis=0)
    shifted = []
    c = None
    for k in range(4):
        s = 3 - k
        xs = (xe if s == 0 else pltpu.roll(xe, s, 0))[8:, :]
        shifted.append(xs)
        term = cw[k:k + 1, :] * xs
        c = term if c is None else c + term
    return c, shifted


def _head_rsq(a):
    parts = []
    for h in range(HEADS):
        ah = a[:, h * HD:(h + 1) * HD]
        r = lax.rsqrt(jnp.sum(ah * ah, axis=-1, keepdims=True) + EPS)
        parts.append(jnp.broadcast_to(r, ah.shape))
    return jnp.concatenate(parts, axis=-1)


def _head_sum(a):
    parts = []
    for h in range(HEADS):
        ah = a[:, h * HD:(h + 1) * HD]
        parts.append(jnp.broadcast_to(jnp.sum(ah, axis=-1, keepdims=True), ah.shape))
    return jnp.concatenate(parts, axis=-1)


def _softplus(x):
    return jnp.maximum(x, 0.0) + jnp.log1p(jnp.exp(-jnp.abs(x)))


def _halo_prev_spec(width, rows):
    per = TM // rows
    return pl.BlockSpec((rows, width), lambda i: (jnp.maximum(i * per - 1, 0), 0))


def _halo_next_spec(width, rows, t):
    per = TM // rows
    last = t // rows - 1
    return pl.BlockSpec((rows, width), lambda i: (jnp.minimum((i + 1) * per, last), 0))


def _dn_pre(qkv, b_rep, a_rep, cw, alog, dtb):
    t = qkv.shape[0]
    qscale = HD ** -0.5

    def body(x_ref, halo_ref, b_ref, a_ref, cw_ref, alog_ref, dt_ref, q_ref, k_ref, v_ref, beta_ref, g_ref):
        i = pl.program_id(0)
        halo = jnp.where(i == 0, 0.0, halo_ref[...])
        c, _ = _conv_rows(x_ref[...], halo, cw_ref[...])
        sc = c * _sigmoid(c)
        q = sc[:, 0:HW]
        k = sc[:, HW:2 * HW]
        q_ref[...] = q * _head_rsq(q) * qscale
        k_ref[...] = k * _head_rsq(k)
        v_ref[...] = sc[:, 2 * HW:]
        beta_ref[...] = _sigmoid(b_ref[...])
        g_ref[...] = -jnp.exp(alog_ref[...]) * _softplus(a_ref[...] + dt_ref[...])

    row = lambda w: pl.BlockSpec((TM, w), lambda i: (i, 0))
    full = lambda a: pl.BlockSpec(a.shape, lambda i: (0,) * a.ndim)
    return pl.pallas_call(
        body,
        grid=(t // TM,),
        in_specs=[row(QKV), _halo_prev_spec(QKV, 8), row(HW), row(HW), full(cw), full(alog), full(dtb)],
        out_specs=[row(HW)] * 5,
        out_shape=[jax.ShapeDtypeStruct((t, HW), F32)] * 5,
        name="dn_pre",
    )(qkv, qkv, b_rep, a_rep, cw, alog, dtb)


def _unit_lower_inv(los, eye):
    ps = [eye - lo for lo in los]
    lps = list(los)
    for _ in range(5):
        lps = [_dot(_c(lp), _c(lp)) for lp in lps]
        ps = [p + _dot(_c(p), _c(lp)) for p, lp in zip(ps, lps)]
    rs = [eye - (p + _dot3(lo, p)) for lo, p in zip(los, ps)]
    return [p + _dot(_c(p), _c(r)) for p, r in zip(ps, rs)]


def _dn_masks():
    ri = lax.broadcasted_iota(jnp.int32, (DNC, DNC), 0)
    ci = lax.broadcasted_iota(jnp.int32, (DNC, DNC), 1)
    return dict(strict=ri > ci, causal=ri >= ci, eye=(ri == ci).astype(F32),
                ltri=_c((ri >= ci).astype(F32)), upper=_c((ri <= ci).astype(F32)))


def _dn_decay(gr, mk):
    rhs = jnp.concatenate([gr, jnp.where(mk["strict"], gr[:, 0:DNC], 0.0)], axis=1)
    cs = _mask_dot(mk["ltri"], rhs)
    gc = cs[:, 0:HD]
    dm = jnp.where(mk["causal"], jnp.exp(cs[:, HD:HD + DNC]), 0.0)
    gl = jnp.sum(gr, axis=0, keepdims=True)
    return dm, jnp.exp(gc), jnp.exp(gl - gc), gl


def _dn_when(n):
    def when():
        i = pl.program_id(0)
        return (i == 0, i == n // 2, i == n - 1)
    return when


def _dn_fwd(q, k, v, beta, g, comm=None):
    t = q.shape[0]
    rows = DN_STEP * DNC
    n = t // rows

    def body(q_ref, k_ref, v_ref, b_ref, g_ref, o_ref, sall_ref, aall_ref, u_ref, w_ref, s_sc):
        i = pl.program_id(0)

        @pl.when(i == 0)
        def _():
            s_sc[...] = jnp.zeros_like(s_sc)

        mk = _dn_masks()
        idx = [(cc, h) for cc in range(DN_STEP) for h in range(HEADS)]
        at = lambda cc, h: (slice(cc * DNC, (cc + 1) * DNC), slice(h * HD, (h + 1) * HD))
        qs = [q_ref[at(*i)] for i in idx]
        ks = [k_ref[at(*i)] for i in idx]
        bs = [b_ref[at(*i)] for i in idx]
        dec = [_dn_decay(g_ref[at(*i)], mk) for i in idx]
        kbs = [k_ * b_ for k_, b_ in zip(ks, bs)]
        los = [jnp.where(mk["strict"], _dot_nt(_c(kb), _c(k_)) * d[0], 0.0) for kb, k_, d in zip(kbs, ks, dec)]
        inv = _unit_lower_inv(los, mk["eye"])
        uws = [_dot3(a, jnp.concatenate([v_ref[at(*i)] * b_, kb * d[1]], axis=1))
               for a, i, b_, kb, d in zip(inv, idx, bs, kbs, dec)]
        attn = [_c(_dot_nt(_c(q_), _c(k_)) * d[0]) for q_, k_, d in zip(qs, ks, dec)]
        for n_, (cc, h) in enumerate(idx):
            aall_ref[cc, h] = inv[n_]
            u_ref[at(cc, h)] = uws[n_][:, 0:HD]
            w_ref[at(cc, h)] = uws[n_][:, HD:]
        ss = [s_sc[h] for h in range(HEADS)]
        for cc in range(DN_STEP):
            base = cc * HEADS
            for h in range(HEADS):
                sall_ref[cc, h] = ss[h]
            ws = [_dot(_c(jnp.concatenate([uws[base + h][:, HD:], qs[base + h] * dec[base + h][1]], axis=0)),
                       _c(ss[h])) for h in range(HEADS)]
            vn = [_c(uws[base + h][:, 0:HD] - ws[h][0:DNC]) for h in range(HEADS)]
            for h in range(HEADS):
                o_ref[at(cc, h)] = ws[h][DNC:] + _dot(attn[base + h], vn[h])
            ss = [ss[h] * jnp.exp(dec[base + h][3]) + _dot_tn(_c(ks[base + h] * dec[base + h][2]), vn[h])
                  for h in range(HEADS)]
        for h in range(HEADS):
            s_sc[h] = ss[h]

    row = pl.BlockSpec((rows, HW), lambda i: (i, 0))
    return _carried_call(
        body, comm, 5, 5, 1, _dn_when(n),
        grid=(n,),
        in_specs=[row] * 5,
        out_specs=[row, pl.BlockSpec((DN_STEP, HEADS, HD, HD), lambda i: (i, 0, 0, 0)),
                   pl.BlockSpec((DN_STEP, HEADS, DNC, DNC), lambda i: (i, 0, 0, 0)), row, row],
        out_shape=[jax.ShapeDtypeStruct((t, HW), F32), jax.ShapeDtypeStruct((t // DNC, HEADS, HD, HD), F32),
                   jax.ShapeDtypeStruct((t // DNC, HEADS, DNC, DNC), F32), jax.ShapeDtypeStruct((t, HW), F32),
                   jax.ShapeDtypeStruct((t, HW), F32)],
        scratch_shapes=[pltpu.VMEM((HEADS, HD, HD), F32)],
        operands=(q, k, v, beta, g), name="dn_fwd")


def _dn_bwd(q, k, v, beta, g, sall, aall, u, w, do, comm=None):
    t = q.shape[0]
    rows = DN_STEP * DNC
    n = t // rows

    def body(q_ref, k_ref, v_ref, b_ref, g_ref, sall_ref, aall_ref, u_ref, w_ref, do_ref,
             dq_ref, dk_ref, dv_ref, db_ref, dg_ref, ds_sc):
        i = pl.program_id(0)

        @pl.when(i == 0)
        def _():
            ds_sc[...] = jnp.zeros_like(ds_sc)

        mk = _dn_masks()
        strict = mk["strict"]
        hs = range(HEADS)
        at = lambda cc, h: (slice(cc * DNC, (cc + 1) * DNC), slice(h * HD, (h + 1) * HD))
        rowsum = lambda a: jnp.sum(a, axis=-1, keepdims=True)
        dsn = [ds_sc[h] for h in hs]
        for cc in reversed(range(DN_STEP)):
            q = [q_ref[at(cc, h)] for h in hs]
            k = [k_ref[at(cc, h)] for h in hs]
            b = [b_ref[at(cc, h)] for h in hs]
            u = [u_ref[at(cc, h)] for h in hs]
            w = [w_ref[at(cc, h)] for h in hs]
            do = [do_ref[at(cc, h)] for h in hs]
            s = [sall_ref[cc, h] for h in hs]
            dec = [_dn_decay(g_ref[at(cc, h)], mk) for h in hs]
            dm, e, f = [d[0] for d in dec], [d[1] for d in dec], [d[2] for d in dec]
            egl = [jnp.exp(d[3]) for d in dec]
            kb = [k[h] * b[h] for h in hs]
            kc = [_c(k[h]) for h in hs]
            sb = [_c(s[h]) for h in hs]
            dob = [_c(do[h]) for h in hs]
            m = [_dot_nt(_c(kb[h]), kc[h]) for h in hs]
            p = [_dot_nt(_c(q[h]), kc[h]) for h in hs]
            vnb = [_c(u[h] - _dot(_c(w[h]), sb[h])) for h in hs]
            dsb = [_c(dsn[h]) for h in hs]
            dvn = [_dot_tn(_c(p[h] * dm[h]), dob[h]) + _dot(_c(k[h] * f[h]), dsb[h]) for h in hs]
            dov = [_c(jnp.concatenate([do[h], dvn[h]], axis=0)) for h in hs]
            t1 = [_dot_nt(dov[h], sb[h]) for h in hs]
            dattn = [_dot_nt(dob[h], vnb[h]) for h in hs]
            dkt = [_dot_nt(vnb[h], dsb[h]) for h in hs]
            dgl = [jnp.sum(jnp.sum(dsn[h] * s[h], axis=1, keepdims=True), axis=0, keepdims=True) * egl[h][:, 0:1]
                   for h in hs]
            dsn = [dsn[h] * egl[h] + _dot_tn(_c(jnp.concatenate([q[h] * e[h], -w[h]], axis=0)), dov[h]) for h in hs]
            dqd = [t1[h][0:DNC] for h in hs]
            dw = [-t1[h][DNC:] for h in hs]
            ab = [_dot3(aall_ref[cc, h], jnp.concatenate([dvn[h], dw[h]], axis=1), TN) for h in hs]
            dlo = [jnp.where(strict, -_dot3(ab[h], jnp.concatenate([u[h], w[h]], axis=1), NT), 0.0) for h in hs]
            dpm = [_c(jnp.concatenate([dattn[h] * dm[h], dlo[h] * dm[h]], axis=0)) for h in hs]
            t2 = [_dot(dpm[h], kc[h]) for h in hs]
            t4 = [_dot_tn(dpm[h], _c(jnp.concatenate([q[h], kb[h]], axis=0))) for h in hs]
            dff = [rowsum(dkt[h] * k[h]) * f[h][:, 0:1] for h in hs]
            de = [rowsum(dqd[h] * q[h]) + rowsum(ab[h][:, HD:] * kb[h]) for h in hs]
            dd = [(dattn[h] * p[h] + dlo[h] * m[h]) * dm[h] for h in hs]
            t3 = [_mask_dot(mk["upper"], jnp.concatenate(
                [jnp.broadcast_to(de[h] * e[h][:, 0:1] - dff[h], (DNC, HD)), dd[h]], axis=1)) for h in hs]
            for h in hs:
                dvb, dkbe = ab[h][:, 0:HD], ab[h][:, HD:]
                dkb = t2[h][DNC:] + dkbe * e[h]
                dbeta = rowsum(dkb * k[h]) + rowsum(dvb * v_ref[at(cc, h)])
                dg = (rowsum(jnp.where(strict, t3[h][:, HD:HD + DNC], 0.0)) + t3[h][:, 0:1]
                      + dgl[h] + jnp.sum(dff[h], axis=0, keepdims=True))
                dq_ref[at(cc, h)] = dqd[h] * e[h] + t2[h][0:DNC]
                dk_ref[at(cc, h)] = t4[h] + dkt[h] * f[h] + dkb * b[h]
                dv_ref[at(cc, h)] = dvb * b[h]
                db_ref[at(cc, h)] = jnp.broadcast_to(dbeta, (DNC, HD))
                dg_ref[at(cc, h)] = jnp.broadcast_to(dg, (DNC, HD))
        for h in hs:
            ds_sc[h] = dsn[h]

    row = pl.BlockSpec((rows, HW), lambda i: (n - 1 - i, 0))
    return _carried_call(
        body, comm, 10, 5, 1, _dn_when(n),
        grid=(n,),
        in_specs=[row] * 5 + [pl.BlockSpec((DN_STEP, HEADS, HD, HD), lambda i: (n - 1 - i, 0, 0, 0)),
                              pl.BlockSpec((DN_STEP, HEADS, DNC, DNC), lambda i: (n - 1 - i, 0, 0, 0)), row, row, row],
        out_specs=[row] * 5,
        out_shape=[jax.ShapeDtypeStruct((t, HW), F32)] * 5,
        scratch_shapes=[pltpu.VMEM((HEADS, HD, HD), F32)],
        operands=(q, k, v, beta, g, sall, aall, u, w, do), name="dn_bwd")


def _group_norm(a, nw):
    rs = []
    for h in range(HEADS):
        ah = a[:, h * HD:(h + 1) * HD]
        rs.append(jnp.broadcast_to(_rstd(ah), ah.shape))
    r = jnp.concatenate(rs, axis=-1)
    xh = a * r
    return xh * nw, xh, r


def _group_norm_bwd(dy, xh, r, nw):
    dxh = dy * nw
    return r * (dxh - xh * (_head_sum(dxh * xh) * (1.0 / HD)))


def _sg_mix(wt_ref, svn_b, nchunk):
    rows = []
    for cidx in range(nchunk):
        cols = []
        for g in range(HEADS):
            blk = svn_b[cidx * SGC:(cidx + 1) * SGC, g * HD:(g + 1) * HD]
            cols.append(_dot(wt_ref[g], blk))
        rows.append(jnp.concatenate(cols, axis=-1))
    return jnp.concatenate(rows, axis=0)


def _ab_out(x1, o, z, su, sv, dnw, sgnw, wtril, sgb, wout):
    t = x1.shape[0]
    nchunk = TM // SGC

    def body(x_ref, o_ref, z_ref, su_ref, sv_ref, dnw_ref, sgnw_ref, wt_ref, sgb_ref, wo_ref, x2_ref, cat_ref):
        on, _, _ = _group_norm(o_ref[...], dnw_ref[...])
        zv = z_ref[...]
        cat_ref[:, 0:HW] = _c(on * (zv * _sigmoid(zv)))
        svn, _, _ = _group_norm(_gelu(sv_ref[...]), sgnw_ref[...])
        mixed = _sg_mix(wt_ref, _c(svn), nchunk) + jnp.tile(sgb_ref[...], (nchunk, 1))
        cat_ref[:, HW:] = _c(_gelu(su_ref[...]) * mixed)
        x2_ref[...] = x_ref[...] + _dot(cat_ref[...], wo_ref[...])

    row = lambda w: pl.BlockSpec((TM, w), lambda i: (i, 0))
    full = lambda a: pl.BlockSpec(a.shape, lambda i: (0,) * a.ndim)
    return pl.pallas_call(
        body,
        grid=(t // TM,),
        in_specs=[row(D)] + [row(HW)] * 4 + [full(dnw), full(sgnw), full(wtril), full(sgb), full(wout)],
        out_specs=[row(D), row(D)],
        out_shape=[jax.ShapeDtypeStruct((t, D), F32), jax.ShapeDtypeStruct((t, D), _MM)],
        name="ab_out",
    )(x1, o, z, su, sv, dnw, sgnw, wtril, sgb, wout)


def _ab_out_bwd(dx2, o, z, su, sv, dnw, sgnw, wtril, wtril_t, sgb, wout):
    t = dx2.shape[0]
    nchunk = TM // SGC

    def body(dx_ref, o_ref, z_ref, su_ref, sv_ref, dnw_ref, sgnw_ref, wt_ref, wtt_ref, sgb_ref, wo_ref,
             do_ref, dz_ref, dsu_ref, dsv_ref, ddnw_ref, dsgnw_ref, dsgw_ref, dsgb_ref):
        i = pl.program_id(0)
        dcat = _dot_nt(_c(dx_ref[...]), wo_ref[...])
        doa = dcat[:, 0:HW]
        dob = dcat[:, HW:]
        on, oh, ro = _group_norm(o_ref[...], dnw_ref[...])
        zv = z_ref[...]
        sz = _sigmoid(zv)
        dz_ref[...] = _c(doa * on * (sz * (1.0 + zv * (1.0 - sz))))
        don = doa * (zv * sz)
        do_ref[...] = _group_norm_bwd(don, oh, ro, dnw_ref[...])
        dd = jnp.sum(don * oh, axis=0, keepdims=True)
        _accum(ddnw_ref, dd[:, 0:HD] + dd[:, HD:2 * HD] + dd[:, 2 * HD:3 * HD] + dd[:, 3 * HD:], i)
        suv = su_ref[...]
        svv = sv_ref[...]
        svg = _gelu(svv)
        svn, sh, rs = _group_norm(svg, sgnw_ref[...])
        svn_b = _c(svn)
        mixed = _sg_mix(wt_ref, svn_b, nchunk) + jnp.tile(sgb_ref[...], (nchunk, 1))
        dsu_ref[...] = _c(dob * mixed * _gelu_grad(suv))
        dmixed = dob * _gelu(suv)
        dmb = _c(dmixed)
        tri = lax.broadcasted_iota(jnp.int32, (SGC, SGC), 0) >= lax.broadcasted_iota(jnp.int32, (SGC, SGC), 1)
        lane = lax.broadcasted_iota(jnp.int32, (SGC, HD), 1)
        rows = []
        dbias = jnp.zeros((SGC, HD), F32)
        for g in range(HEADS):
            gs = slice(g * HD, (g + 1) * HD)
            dwg = jnp.zeros((SGC, SGC), F32)
            col = jnp.zeros((SGC, 1), F32)
            for cidx in range(nchunk):
                cs = slice(cidx * SGC, (cidx + 1) * SGC)
                dwg = dwg + _dot_nt(dmb[cs, gs], svn_b[cs, gs])
                col = col + jnp.sum(dmixed[cs, gs], axis=-1, keepdims=True)
            _accum(dsgw_ref.at[g], jnp.where(tri, dwg, 0.0), i)
            dbias = dbias + jnp.where(lane == g, col, 0.0)
        _accum(dsgb_ref, dbias, i)
        for cidx in range(nchunk):
            cs = slice(cidx * SGC, (cidx + 1) * SGC)
            rows.append(jnp.concatenate(
                [_dot(wtt_ref[g], dmb[cs, g * HD:(g + 1) * HD]) for g in range(HEADS)], axis=-1))
        dsvn = jnp.concatenate(rows, axis=0)
        _accum(dsgnw_ref, jnp.sum(dsvn * sh, axis=0, keepdims=True), i)
        dsv_ref[...] = _c(_group_norm_bwd(dsvn, sh, rs, sgnw_ref[...]) * _gelu_grad(svv))

    row = lambda w: pl.BlockSpec((TM, w), lambda i: (i, 0))
    full = lambda a: pl.BlockSpec(a.shape, lambda i: (0,) * a.ndim)
    const = lambda shape: pl.BlockSpec(shape, lambda i: (0,) * len(shape))
    return pl.pallas_call(
        body,
        grid=(t // TM,),
        in_specs=[row(D)] + [row(HW)] * 4 + [full(dnw), full(sgnw), full(wtril), full(wtril_t), full(sgb), full(wout)],
        out_specs=[row(HW)] * 4 + [const((1, HD)), const((1, HW)), const((HEADS, SGC, SGC)), const((SGC, HD))],
        out_shape=[jax.ShapeDtypeStruct((t, HW), F32)] + [jax.ShapeDtypeStruct((t, HW), _MM)] * 3
        + [jax.ShapeDtypeStruct((1, HD), F32), jax.ShapeDtypeStruct((1, HW), F32),
           jax.ShapeDtypeStruct((HEADS, SGC, SGC), F32), jax.ShapeDtypeStruct((SGC, HD), F32)],
        name="ab_out_bwd",
    )(dx2, o, z, su, sv, dnw, sgnw, wtril, wtril_t, sgb, wout)


def _dn_pre_bwd(qkv, b_rep, a_rep, cw, alog, dtb, dqn, dkn, dv, dbeta, dg):
    t = qkv.shape[0]
    qscale = HD ** -0.5

    def body(x_ref, halo_ref, b_ref, a_ref, cw_ref, alog_ref, dt_ref, dq_ref, dk_ref, dv_ref, dbeta_ref, dg_ref,
             dc_ref, dba_ref, dcw_ref, dalog_ref, ddt_ref):
        i = pl.program_id(0)
        halo = jnp.where(i == 0, 0.0, halo_ref[...])
        c, shifted = _conv_rows(x_ref[...], halo, cw_ref[...])
        s = _sigmoid(c)
        sc = c * s
        q = sc[:, 0:HW]
        k = sc[:, HW:2 * HW]
        rq = _head_rsq(q)
        rk = _head_rsq(k)
        qu = q * rq
        ku = k * rk
        dqn = dq_ref[...]
        dkn = dk_ref[...]
        dq = qscale * rq * (dqn - qu * _head_sum(dqn * qu))
        dk = rk * (dkn - ku * _head_sum(dkn * ku))
        dsc = jnp.concatenate([dq, dk, dv_ref[...]], axis=-1)
        dc = dsc * (s * (1.0 + c * (1.0 - s)))
        dc_ref[...] = _c(dc)
        for kk in range(4):
            _accum(dcw_ref.at[kk], jnp.sum(dc * shifted[kk], axis=0, keepdims=True), i)
        beta = _sigmoid(b_ref[...])
        dbp = dbeta_ref[...] * beta * (1.0 - beta)
        nea = -jnp.exp(alog_ref[...])
        spin = a_ref[...] + dt_ref[...]
        dgv = dg_ref[...]
        dap = dgv * nea * _sigmoid(spin)
        _accum(dalog_ref, jnp.sum(dgv * nea * _softplus(spin), axis=0, keepdims=True), i)
        _accum(ddt_ref, jnp.sum(dap, axis=0, keepdims=True), i)
        lane = lax.broadcasted_iota(jnp.int32, (TM, HD), 1)
        dba = jnp.zeros((TM, HD), F32)
        for h in range(HEADS):
            dba = dba + jnp.where(lane == h, dbp[:, h * HD:(h + 1) * HD], 0.0)
            dba = dba + jnp.where(lane == HEADS + h, dap[:, h * HD:(h + 1) * HD], 0.0)
        dba_ref[...] = _c(dba)

    row = lambda w: pl.BlockSpec((TM, w), lambda i: (i, 0))
    full = lambda a: pl.BlockSpec(a.shape, lambda i: (0,) * a.ndim)
    const = lambda shape: pl.BlockSpec(shape, lambda i: (0,) * len(shape))
    return pl.pallas_call(
        body,
        grid=(t // TM,),
        in_specs=[row(QKV), _halo_prev_spec(QKV, 8), row(HW), row(HW), full(cw), full(alog), full(dtb)] + [row(HW)] * 5,
        out_specs=[row(QKV), row(HD), const((4, 1, QKV)), const((1, HW)), const((1, HW))],
        out_shape=[jax.ShapeDtypeStruct((t, QKV), _MM), jax.ShapeDtypeStruct((t, HD), _MM),
                   jax.ShapeDtypeStruct((4, 1, QKV), F32), jax.ShapeDtypeStruct((1, HW), F32),
                   jax.ShapeDtypeStruct((1, HW), F32)],
        name="dn_pre_bwd",
    )(qkv, qkv, b_rep, a_rep, cw, alog, dtb, dqn, dkn, dv, dbeta, dg)


def _conv_bwd(dc, cw):
    t = dc.shape[0]
    nt = t // TM

    def body(dc_ref, halo_ref, cw_ref, dx_ref):
        i = pl.program_id(0)
        halo = jnp.where(i == nt - 1, 0.0, halo_ref[...].astype(F32))
        de = jnp.concatenate([dc_ref[...].astype(F32), halo], axis=0)
        cwv = cw_ref[...]
        acc = None
        for k in range(4):
            s = 3 - k
            ds = (de if s == 0 else pltpu.roll(de, TM + HALO - s, 0))[0:TM, :]
            term = cwv[k:k + 1, :] * ds
            acc = term if acc is None else acc + term
        dx_ref[...] = _c(acc)

    return pl.pallas_call(
        body,
        grid=(nt,),
        in_specs=[pl.BlockSpec((TM, QKV), lambda i: (i, 0)), _halo_next_spec(QKV, HALO, t),
                  pl.BlockSpec(cw.shape, lambda i: (0, 0))],
        out_specs=pl.BlockSpec((TM, QKV), lambda i: (i, 0)),
        out_shape=jax.ShapeDtypeStruct((t, QKV), _MM),
        name="conv_bwd",
    )(dc, dc, cw)


def _ab_proj_bwd(x1, nw, dqkv, dz, dsu, dsv, dba, wab_b, dres):
    t = x1.shape[0]

    def body(x_ref, nw_ref, dqkv_ref, dz_ref, dsu_ref, dsv_ref, dba_ref, w_ref, dres_ref, dx_ref, dcat_ref, dnw_ref):
        i = pl.program_id(0)
        dcat_ref[:, 0:QKV] = dqkv_ref[...]
        o = QKV
        for ref in (dz_ref, dsu_ref, dsv_ref):
            dcat_ref[:, o:o + HW] = ref[...]
            o += HW
        dcat_ref[:, o:o + 128] = dba_ref[...]
        dh = _dot_nt(dcat_ref[...], w_ref[...])
        xv = x_ref[...]
        r = _rstd(xv)
        dx, dnw = _rms_bwd(dh, xv * r, r, nw_ref[...])
        dx_ref[...] = dres_ref[...] + dx
        _accum(dnw_ref, dnw, i)

    row = lambda w: pl.BlockSpec((TM, w), lambda i: (i, 0))
    return pl.pallas_call(
        body,
        grid=(t // TM,),
        in_specs=[row(D), pl.BlockSpec((1, D), lambda i: (0, 0)), row(QKV), row(HW), row(HW), row(HW), row(128),
                  pl.BlockSpec((D, PW_B), lambda i: (0, 0)), row(D)],
        out_specs=[row(D), row(PW_B), pl.BlockSpec((1, D), lambda i: (0, 0))],
        out_shape=[jax.ShapeDtypeStruct((t, D), F32), jax.ShapeDtypeStruct((t, PW_B), _MM),
                   jax.ShapeDtypeStruct((1, D), F32)],
        name="ab_proj_bwd",
    )(x1, nw, dqkv, dz, dsu, dsv, dba, wab_b, dres)


def _pool_counts(i):
    pos = (lax.broadcasted_iota(jnp.int32, (TM + HALO, 1), 0) + i * TM + 1).astype(F32)
    return [1.0 / jnp.minimum(pos, float(w)) for w in POOL_WINDOWS]


def _window_sum(ext, win, back):
    r = ext.shape[0]
    s = ext
    step = 1
    while step < win:
        s = s + pltpu.roll(s, step if back else r - step, 0)
        step *= 2
    return s


def _pooled(h_ext, invc, g):
    gs = slice(g * PG, (g + 1) * PG)
    he = h_ext[:, gs]
    ws = _window_sum(he, POOL_WINDOWS[g], True)[HALO:, :]
    return ws * invc[g][0:TM, :] - he[HALO:, :]


def _pool_fwd(x1, nw, pw, scale):
    t = x1.shape[0]

    def body(x_ref, halo_ref, nw_ref, pw_ref, sc_ref, x2_ref):
        i = pl.program_id(0)
        xv = x_ref[...]
        hv = halo_ref[...]
        nwv = nw_ref[...]
        h_ext = jnp.concatenate([jnp.where(i == 0, 0.0, hv * _rstd(hv) * nwv), xv * _rstd(xv) * nwv], axis=0)
        invc = _pool_counts(i)
        outs = [_dot(_c(_pooled(h_ext, invc, g)), pw_ref[g]) for g in range(4)]
        x2_ref[...] = xv + jnp.concatenate(outs, axis=-1) * sc_ref[...]

    return pl.pallas_call(
        body,
        grid=(t // TM,),
        in_specs=[pl.BlockSpec((TM, D), lambda i: (i, 0)), _halo_prev_spec(D, HALO),
                  pl.BlockSpec((1, D), lambda i: (0, 0)), pl.BlockSpec((4, PG, PG), lambda i: (0, 0, 0)),
                  pl.BlockSpec((1, D), lambda i: (0, 0))],
        out_specs=pl.BlockSpec((TM, D), lambda i: (i, 0)),
        out_shape=jax.ShapeDtypeStruct((t, D), F32),
        name="pool_fwd",
    )(x1, x1, nw, pw, scale)


def _pool_bwd(x1, nw, pw, scale, dx2):
    t = x1.shape[0]
    nt = t // TM

    def body(x_ref, halo_ref, nw_ref, pw_ref, sc_ref, dx2_ref, dnext_ref, dx_ref, dnw_ref, dpw_ref, dsc_ref):
        i = pl.program_id(0)
        xv = x_ref[...]
        hv = halo_ref[...]
        nwv = nw_ref[...]
        r = _rstd(xv)
        xh = xv * r
        h_ext = jnp.concatenate([jnp.where(i == 0, 0.0, hv * _rstd(hv) * nwv), xh * nwv], axis=0)
        invc = _pool_counts(i)
        dyv = dx2_ref[...]
        dout_ext = jnp.concatenate([dyv, jnp.where(i == nt - 1, 0.0, dnext_ref[...])], axis=0) * sc_ref[...]
        dh_parts = []
        dsc_parts = []
        for g in range(4):
            gs = slice(g * PG, (g + 1) * PG)
            pooled_b = _c(_pooled(h_ext, invc, g))
            dout_b = _c(dout_ext[:, gs])
            dsc_parts.append(jnp.sum(dyv[:, gs] * _dot(pooled_b, pw_ref[g]), axis=0, keepdims=True))
            _accum(dpw_ref.at[g], _dot_tn(pooled_b, dout_b[0:TM, :]), i)
            dpool_ext = _dot_nt(dout_b, pw_ref[g])
            lead = _window_sum(dpool_ext * invc[g], POOL_WINDOWS[g], False)[0:TM, :]
            dh_parts.append(lead - dpool_ext[0:TM, :])
        _accum(dsc_ref, jnp.concatenate(dsc_parts, axis=-1), i)
        dx, dnw = _rms_bwd(jnp.concatenate(dh_parts, axis=-1), xh, r, nwv)
        dx_ref[...] = dyv + dx
        _accum(dnw_ref, dnw, i)

    vec = pl.BlockSpec((1, D), lambda i: (0, 0))
    return pl.pallas_call(
        body,
        grid=(nt,),
        in_specs=[pl.BlockSpec((TM, D), lambda i: (i, 0)), _halo_prev_spec(D, HALO), vec,
                  pl.BlockSpec((4, PG, PG), lambda i: (0, 0, 0)), vec,
                  pl.BlockSpec((TM, D), lambda i: (i, 0)), _halo_next_spec(D, HALO, t)],
        out_specs=[pl.BlockSpec((TM, D), lambda i: (i, 0)), vec, pl.BlockSpec((4, PG, PG), lambda i: (0, 0, 0)), vec],
        out_shape=[jax.ShapeDtypeStruct((t, D), F32), jax.ShapeDtypeStruct((1, D), F32),
                   jax.ShapeDtypeStruct((4, PG, PG), F32), jax.ShapeDtypeStruct((1, D), F32)],
        name="pool_bwd",
    )(x1, x1, nw, pw, scale, dx2, dx2)


def _adamw(lands, w, m, v, rb, name):
    nl, nr = w.shape[0], w.shape[1]
    rest = w.shape[2:]
    ns = lands[0].shape[0]
    zeros = (0,) * len(rest)

    def body(*refs):
        l_refs = refs[0:nl]
        w_ref, m_ref, v_ref, g_ref, d_ref, m2_ref, v2_ref = refs[nl:]
        for l in range(nl):
            g = l_refs[l][0].astype(F32)
            for s in range(1, ns):
                g = g + l_refs[l][s].astype(F32)
            m2 = ADAM_B1 * m_ref[l] + (1.0 - ADAM_B1) * g
            v2 = ADAM_B2 * v_ref[l] + (1.0 - ADAM_B2) * (g * g)
            m_hat = m2 / (1.0 - ADAM_B1 ** ADAM_STEP)
            v_hat = v2 / (1.0 - ADAM_B2 ** ADAM_STEP)
            g_ref[l] = g
            d_ref[l] = -ADAM_LR * (m_hat / (jnp.sqrt(v_hat) + ADAM_EPS) + ADAM_WD * w_ref[l])
            m2_ref[l] = m2
            v2_ref[l] = v2

    lspec = pl.BlockSpec((ns, rb) + rest, lambda r: (0, r) + zeros)
    wspec = pl.BlockSpec((nl, rb) + rest, lambda r: (0, r) + zeros)
    return pl.pallas_call(
        body,
        grid=(nr // rb,),
        in_specs=[lspec] * nl + [wspec] * 3,
        out_specs=[wspec] * 4,
        out_shape=[jax.ShapeDtypeStruct(w.shape, F32)] * 4,
        name=name,
    )(*lands, w, m, v)


WEIGHT_ORDER = ("ffn_norm1", "ffn1_w_in", "ffn1_w_out", "mix_norm", "ffn_norm2", "ffn2_w_in", "ffn2_w_out", "ab_w_in",
                "dn_conv_w", "dn_a_log", "dn_dt_bias", "dn_out_norm", "sg_norm", "sg_w", "sg_b", "ab_w_out", "pool_w",
                "pool_scale", "final_norm")
R_SMALL = 88
SMALL_ROWS = (
    ("ffn_norm1", (2, D), 2), ("mix_norm", (2, D), 2), ("ffn_norm2", (2, D), 2), ("final_norm", (D,), 1),
    ("sg_w", (1, 4, SGC, SGC), 64), ("sg_norm", (1, 4, HD), 1), ("sg_b", (1, 4, SGC), 1), ("dn_out_norm", (1, HD), 1),
    ("dn_a_log", (1, 4), 1), ("dn_dt_bias", (1, 4), 1), ("pool_scale", (1, D), 1), ("dn_conv_w", (1, 4, QKV), 8),
)
SMALL_SHARDED = ("pool_scale", "dn_conv_w")


def _rows_of(a, rows):
    if a.shape[-1] == QKV:
        return jnp.pad(a.reshape(4, QKV), ((0, 0), (0, 2 * ROW - QKV))).reshape(8, ROW)
    n = _numel(a.shape)
    if n % ROW == 0:
        return a.reshape(n // ROW, ROW)
    return jnp.pad(a.reshape(1, n), ((0, 0), (0, ROW - n)))


def _from_rows(r, shape):
    if shape[-1] == QKV:
        return r.reshape(4, 2 * ROW)[:, 0:QKV].reshape(shape)
    n = _numel(shape)
    if n % ROW == 0:
        return r.reshape(shape)
    return r[:, 0:n].reshape(shape)


def _pack_small(vals):
    parts = [(_rows_of(vals[n].astype(F32), r) if n in vals else jnp.zeros((r, ROW), F32)) for n, _, r in SMALL_ROWS]
    used = sum(r for _, _, r in SMALL_ROWS)
    return jnp.concatenate(parts + [jnp.zeros((R_SMALL - used, ROW), F32)], axis=0)


def _unpack_small(packed):
    out, o = {}, 0
    for n, shape, r in SMALL_ROWS:
        out[n] = _from_rows(packed[o:o + r], shape)
        o += r
    return out


def _pack_small_shard(ps, cw):
    return jnp.concatenate([
        jnp.pad(ps, ((0, 0), (0, ROW - D // N_DEV))), jnp.pad(cw[0], ((0, 0), (0, ROW - QKV // N_DEV))),
        jnp.zeros((3, ROW), F32)], axis=0)


def _mixer_weights(g_in, g_out, g_small, small):
    w = {}
    wi = jnp.transpose(g_in, (1, 0, 2)).reshape(D, AB_IN)
    gates = wi[:, AB_MAIN:AB_MAIN + AB_GATES]
    main = [wi[:, 0:AB_MAIN], wi[:, AB_MAIN + AB_GATES:AB_IN]]
    w["wab_f"] = jnp.concatenate(
        main + [jnp.repeat(gates[:, 0:HEADS], HD, axis=1), jnp.repeat(gates[:, HEADS:], HD, axis=1)], axis=1)
    w["wab_b"] = jnp.concatenate(main + [gates, jnp.zeros((D, 128 - AB_GATES), wi.dtype)], axis=1)
    w["cw"] = jnp.transpose(g_small[:, 1:5, 0:QKV // N_DEV], (1, 0, 2)).reshape(4, QKV)
    w["ps"] = g_small[:, 0, 0:D // N_DEV].reshape(1, D)
    w["alog"] = jnp.repeat(small["dn_a_log"][0].astype(F32), HD).reshape(1, HW)
    w["dtb"] = jnp.repeat(small["dn_dt_bias"][0].astype(F32), HD).reshape(1, HW)
    w["dnw"] = jnp.tile(small["dn_out_norm"][0].astype(F32), HEADS).reshape(1, HW)
    w["sgnw"] = small["sg_norm"][0].astype(F32).reshape(1, HW)
    tri = jnp.tril(jnp.ones((SGC, SGC), dtype=bool))
    wt = jnp.where(tri, small["sg_w"][0].astype(F32), 0.0)
    w["wtril"] = _c(wt)
    w["wtril_t"] = _c(jnp.transpose(wt, (0, 2, 1)))
    w["sgb"] = jnp.repeat(jnp.transpose(small["sg_b"][0].astype(F32)), HD, axis=1)
    w["wout_ab"] = g_out.reshape(D, D)
    return w


def kernel(x, ffn_norm1, ffn1_w_in, ffn1_w_out, mix_norm, ffn_norm2, ffn2_w_in, ffn2_w_out, ab_w_in, dn_conv_w, dn_a_log, dn_dt_bias, dn_out_norm, sg_norm, sg_w, sg_b, ab_w_out, pool_w, pool_scale, final_norm, loss_target, m_ffn_norm1, m_ffn1_w_in, m_ffn1_w_out, m_mix_norm, m_ffn_norm2, m_ffn2_w_in, m_ffn2_w_out, m_ab_w_in, m_dn_conv_w, m_dn_a_log, m_dn_dt_bias, m_dn_out_norm, m_sg_norm, m_sg_w, m_sg_b, m_ab_w_out, m_pool_w, m_pool_scale, m_final_norm, v_ffn_norm1, v_ffn1_w_in, v_ffn1_w_out, v_mix_norm, v_ffn_norm2, v_ffn2_w_in, v_ffn2_w_out, v_ab_w_in, v_dn_conv_w, v_dn_a_log, v_dn_dt_bias, v_dn_out_norm, v_sg_norm, v_sg_w, v_sg_b, v_ab_w_out, v_pool_w, v_pool_scale, v_final_norm):
    wl = dict(ffn_norm1=ffn_norm1, mix_norm=mix_norm, ffn_norm2=ffn_norm2, dn_a_log=dn_a_log, dn_dt_bias=dn_dt_bias,
              dn_out_norm=dn_out_norm, sg_norm=sg_norm, sg_w=sg_w, sg_b=sg_b, final_norm=final_norm)
    ml = dict(ffn_norm1=m_ffn_norm1, mix_norm=m_mix_norm, ffn_norm2=m_ffn_norm2, dn_a_log=m_dn_a_log,
              dn_dt_bias=m_dn_dt_bias, dn_out_norm=m_dn_out_norm, sg_norm=m_sg_norm, sg_w=m_sg_w, sg_b=m_sg_b,
              final_norm=m_final_norm)
    vl = dict(ffn_norm1=v_ffn_norm1, mix_norm=v_mix_norm, ffn_norm2=v_ffn_norm2, dn_a_log=v_dn_a_log,
              dn_dt_bias=v_dn_dt_bias, dn_out_norm=v_dn_out_norm, sg_norm=v_sg_norm, sg_w=v_sg_w, sg_b=v_sg_b,
              final_norm=v_final_norm)
    row = lambda a: a.reshape(1, -1).astype(F32)
    n1 = [row(ffn_norm1[l]) for l in range(2)]
    n2 = [row(ffn_norm2[l]) for l in range(2)]
    mix = [row(mix_norm[l]) for l in range(2)]
    s_in = {(f, l): _c(wf[l]) for f, wf in enumerate((ffn1_w_in, ffn2_w_in)) for l in range(2)}
    s_out = {(f, l): _c(wf[l]) for f, wf in enumerate((ffn1_w_out, ffn2_w_out)) for l in range(2)}
    xs, tgt = x[0], loss_target[0]

    wi00, wo00 = _comm_call(_Comm("gather", [s_in[0, 0], s_out[0, 0]]), "gather_first")
    x01, gu00, (g_abin, g_about, g_small, wi10) = _ffn_fwd(
        xs, n1[0], wi00, wo00,
        comm=_Comm("gather", [_c(ab_w_in[0]), _c(ab_w_out[0]), _pack_small_shard(pool_scale, dn_conv_w), s_in[1, 0]]))
    w = _mixer_weights(g_abin, g_about, g_small, wl)
    h, qkv, z, su, sv, b_rep, a_rep = _ab_proj(x01, mix[0], w["wab_f"])
    qn, kn, v, beta, g = _dn_pre(qkv, b_rep, a_rep, w["cw"], w["alog"], w["dtb"])
    (o, sall, aall, dn_u, dn_w), (wo10, g_pw) = _dn_fwd(
        qn, kn, v, beta, g, comm=_Comm("gather", [s_out[1, 0], _c(pool_w[0])]))
    pw = jnp.transpose(g_pw, (1, 0, 2, 3)).reshape(4, PG, PG)
    x02, cat = _ab_out(x01, o, z, su, sv, w["dnw"], w["sgnw"], w["wtril"], w["sgb"], w["wout_ab"])
    x10, gu10, (wi01, wo01) = _ffn_fwd(x02, n2[0], wi10, wo10, comm=_Comm("gather", [s_in[0, 1], s_out[0, 1]]))
    x11, gu01, (wi11, wo11) = _ffn_fwd(x10, n1[1], wi01, wo01, comm=_Comm("gather", [s_in[1, 1], s_out[1, 1]]))
    x12 = _pool_fwd(x11, mix[1], pw, w["ps"])
    x13, gu11, _ = _ffn_fwd(x12, n2[1], wi11, wo11)
    loss_local, dx, d_fn = _loss_head(x13, row(final_norm), tgt)

    bt = min(BT, xs.shape[0])

    def ffn_b(xin, nw, w_in, w_out, gu, dy, comm=None):
        (dxn, xn, act, dh, dnw, dyb), landed = _ffn_bwd(xin, nw, w_in, w_out, gu, dy, comm)
        return dxn, dnw, (xn, act, dh, dyb), landed

    def ffn_g(kept, dy):
        return [_mm_tn_win(kept[0], kept[2])[0], _mm_tn_wout(kept[1], kept[3])]

    dy = dx
    dx, d_n2_1, kept, _ = ffn_b(x12, n2[1], wi11, wo11, gu11, dy)
    g11 = ffn_g(kept, dy)
    dx, d_mix_1, d_pw, d_ps = _pool_bwd(x11, mix[1], pw, w["ps"], dx)
    d_pw_sh = _c(jnp.transpose(d_pw.reshape(4, N_DEV, PG // N_DEV, PG), (1, 0, 2, 3)))
    dy = dx
    dx, d_n1_1, kept, land11 = ffn_b(x10, n1[1], wi01, wo01, gu01, dy, _Comm("exchange", g11))
    g01 = ffn_g(kept, dy)
    dy = dx
    dx, d_n2_0, kept, land01 = ffn_b(x02, n2[0], wi10, wo10, gu10, dy, _Comm("exchange", g01 + [d_pw_sh]))
    g10 = ffn_g(kept, dy)
    do, dz, dsu, dsv, d_dnw, d_sgnw, d_sgw, d_sgb = _ab_out_bwd(
        dx, o, z, su, sv, w["dnw"], w["sgnw"], w["wtril"], w["wtril_t"], w["sgb"], w["wout_ab"])
    d_about = _mm_tn(cat, dx, D, D, bt, _MM, "mm_tn_about").reshape(N_DEV, D // N_DEV, D)
    (dqn, dkn, dv, dbeta, dg), _ = _dn_bwd(qn, kn, v, beta, g, sall, aall, dn_u, dn_w, do)
    dc, dba, d_cw, d_alog, d_dtb = _dn_pre_bwd(qkv, b_rep, a_rep, w["cw"], w["alog"], w["dtb"], dqn, dkn, dv, dbeta, dg)
    dqkv = _conv_bwd(dc, w["cw"])
    dx, dcat, d_mix_0 = _ab_proj_bwd(x01, mix[0], dqkv, dz, dsu, dsv, dba, w["wab_b"], dx)
    d_wab = _mm_tn(h, dcat, D, 640, min(BT_WIN, xs.shape[0]), _MM, "mm_tn_abin")
    rest = PW_B - 128
    d_abin = jnp.concatenate([d_wab[:, 0:AB_MAIN], d_wab[:, rest:rest + AB_GATES], d_wab[:, AB_MAIN:rest]], axis=1)
    d_abin_sh = jnp.transpose(d_abin.reshape(D, N_DEV, AB_IN // N_DEV), (1, 0, 2))
    dy = dx
    grad_x, d_n1_0, kept, land_mid = ffn_b(xs, n1[0], wi00, wo00, gu00, dy,
                                           _Comm("exchange", g10 + [d_abin_sh, d_about]))
    land10, land_ab = land_mid[0:2], land_mid[2:4]

    g_small = {
        "ffn_norm1": jnp.concatenate([d_n1_0, d_n1_1], axis=0),
        "mix_norm": jnp.concatenate([d_mix_0, d_mix_1], axis=0),
        "ffn_norm2": jnp.concatenate([d_n2_0, d_n2_1], axis=0),
        "dn_conv_w": d_cw.reshape(1, 4, QKV),
        "dn_a_log": d_alog[:, ::HD],
        "dn_dt_bias": d_dtb[:, ::HD],
        "dn_out_norm": d_dnw,
        "sg_norm": d_sgnw.reshape(1, HEADS, HD),
        "sg_w": d_sgw[None],
        "sg_b": jnp.transpose(d_sgb[:, 0:HEADS])[None],
        "pool_scale": d_ps,
        "final_norm": d_fn.reshape(D),
    }
    g00_out = _mm_tn_wout(kept[1], kept[3])
    g00_in, (land00_out, land_small) = _mm_tn_win(
        kept[0], kept[2], comm=_Comm("exchange", [g00_out], repl=[_pack_small(g_small)]))
    (land00_in,) = _comm_call(_Comm("exchange", [g00_in]), "exchange_last")

    res = {}
    tr = lambda a: jnp.swapaxes(a, 1, 2)
    res["ffn1_w_in"] = [tr(a) for a in _adamw([land00_in, land01[0]], tr(ffn1_w_in), tr(m_ffn1_w_in), tr(v_ffn1_w_in),
                                              176, "adamw_w_in")]
    res["ffn2_w_in"] = [tr(a) for a in _adamw([land10[0], land11[0]], tr(ffn2_w_in), tr(m_ffn2_w_in), tr(v_ffn2_w_in),
                                              176, "adamw_w_in")]
    res["ffn1_w_out"] = _adamw([land00_out, land01[1]], ffn1_w_out, m_ffn1_w_out, v_ffn1_w_out, 176, "adamw_w_out")
    res["ffn2_w_out"] = _adamw([land10[1], land11[1]], ffn2_w_out, m_ffn2_w_out, v_ffn2_w_out, 176, "adamw_w_out")
    res["ab_w_in"] = _adamw([land_ab[0]], ab_w_in, m_ab_w_in, v_ab_w_in, 256, "adamw_ab_w_in")
    res["ab_w_out"] = _adamw([land_ab[1]], ab_w_out, m_ab_w_out, v_ab_w_out, D // N_DEV, "adamw_ab_w_out")
    res["pool_w"] = _adamw([land01[2]], pool_w, m_pool_w, v_pool_w, 4, "adamw_pool_w")
    sm = _adamw([land_small], _pack_small(wl)[None], _pack_small(ml)[None], _pack_small(vl)[None], R_SMALL,
                "adamw_replicated")
    sm = [_unpack_small(a[0]) for a in sm]
    for n in wl:
        res[n] = [d[n] for d in sm]
    me = 4 * lax.axis_index("x") + 2 * lax.axis_index("y") + lax.axis_index("c")
    g_ps = lax.dynamic_slice(sm[0]["pool_scale"], (0, me * (D // N_DEV)), (1, D // N_DEV))
    g_cw = lax.dynamic_slice(sm[0]["dn_conv_w"], (0, 0, me * (QKV // N_DEV)), (1, 4, QKV // N_DEV))
    s2 = _adamw([_pack_small_shard(g_ps, g_cw)[None]], _pack_small_shard(pool_scale, dn_conv_w)[None],
                _pack_small_shard(m_pool_scale, m_dn_conv_w)[None], _pack_small_shard(v_pool_scale, v_dn_conv_w)[None],
                8, "adamw_small_sharded")
    res["pool_scale"] = [a[0, 0:1, 0:D // N_DEV] for a in s2]
    res["dn_conv_w"] = [a[0, 1:5, 0:QKV // N_DEV][None] for a in s2]

    loss = lax.psum(loss_local[0, 0], ("x", "y", "c"))
    result = [loss, grad_x[None]]
    for i in range(4):
        result += [res[n][i] for n in WEIGHT_ORDER]
    return tuple(result)
```

```python
import jax
import jax.numpy as jnp
from jax import lax
from jax.experimental import pallas as pl
from jax.experimental.pallas import tpu as pltpu

F32 = jnp.float32
_MM = jnp.bfloat16

D = 1024
FF = 2816
EPS = 1e-6
HEADS = 4
HD = 128
DNC = 64
DN_STEP = 8
SGC = 128
QKV = 3 * HEADS * HD
HW = HEADS * HD
POOL_WINDOWS = (2, 4, 8, 16)
PG = D // 4
HALO = 16
N_DEV = 8
AB_IN = 3080
ROW = 1024

TM = 512
BT = 2048
BT_WIN = 4096
FT = 512
FWD_CHUNKS = 2
BWD_CHUNKS = 1
FC = 704
NJ = FF // FC
WO_ROWS = FF // N_DEV

ADAM_LR, ADAM_B1, ADAM_B2, ADAM_EPS, ADAM_WD, ADAM_STEP = 0.001, 0.9, 0.999, 1e-08, 0.01, 10

MESH_T = pl.DeviceIdType.MESH
NN = (((1,), (0,)), ((), ()))
NT = (((1,), (1,)), ((), ()))
TN = (((0,), (0,)), ((), ()))


def _c(a):
    return a.astype(_MM)


def _dg(a, b, dims):
    return lax.dot_general(a, b, dims, preferred_element_type=F32)


def _dot(a, b):
    return _dg(a, b, NN)


def _dot_nt(a, b):
    return _dg(a, b, NT)


def _dot_tn(a, b):
    return _dg(a, b, TN)


def _split2(a):
    hi = _c(a)
    return hi, _c(a - hi.astype(F32))


def _dot3(a, b, dims=NN):
    ah, al = _split2(a)
    bh, bl = _split2(b)
    return _dg(ah, bh, dims) + (_dg(ah, bl, dims) + _dg(al, bh, dims))


def _mask_dot(mask, x):
    x1 = _c(x)
    r = x - x1.astype(F32)
    x2 = _c(r)
    x3 = _c(r - x2.astype(F32))
    return _dot(mask, x1) + (_dot(mask, x2) + _dot(mask, x3))


def _sigmoid(x):
    return jax.nn.sigmoid(x)


def _gelu(x):
    return 0.5 * x * (1.0 + lax.erf(x * 0.7071067811865476))


def _gelu_grad(x):
    return 0.5 * (1.0 + lax.erf(x * 0.7071067811865476)) + x * jnp.exp(-0.5 * x * x) * 0.3989422804014327


def _accum(ref, val, step):
    @pl.when(step == 0)
    def _():
        ref[...] = val

    @pl.when(step > 0)
    def _():
        ref[...] += val


def _rstd(x):
    return lax.rsqrt(jnp.mean(x * x, axis=-1, keepdims=True) + EPS)


def _rms_bwd(dy, xhat, r, nw):
    dnw = jnp.sum(dy * xhat, axis=0, keepdims=True)
    dxh = dy * nw
    dx = r * (dxh - xhat * jnp.mean(dxh * xhat, axis=-1, keepdims=True))
    return dx, dnw


def _numel(shape):
    n = 1
    for s in shape:
        n *= s
    return n


def _peer(k, x, y, c):
    px = 1 - x if k & 4 else x
    py = 1 - y if k & 2 else y
    pc = 1 - c if k & 1 else c
    return px, py, pc


class _Comm:
    def __init__(self, kind, arrs, repl=()):
        self.kind = kind
        self.ns = len(arrs)
        self.arrs = list(arrs) + list(repl)
        self.na = len(self.arrs)

    @property
    def out_shape(self):
        out = []
        for i, a in enumerate(self.arrs):
            lead = (N_DEV,) if (self.kind == "gather" or i >= self.ns) else ()
            out.append(jax.ShapeDtypeStruct(lead + a.shape, a.dtype))
        return out

    @property
    def scratch(self):
        return [pltpu.SemaphoreType.DMA((7 * self.na,)), pltpu.SemaphoreType.DMA((7 * self.na,)),
                pltpu.SemaphoreType.DMA((self.na,))]

    def phases(self, ins, outs, sems):
        send_sems, recv_sems, local_sems = sems
        na = self.na
        x, y, c = lax.axis_index("x"), lax.axis_index("y"), lax.axis_index("c")
        if self.kind == "gather":
            me, sibling = (x, y, c), (x, y, 1 - c)
            chips = [(1 - x, y), (x, 1 - y), (1 - x, 1 - y)]

            def slot(a, px, py, pc):
                return outs[a].at[4 * px + 2 * py + pc]

            def copy(a, k, block, to, src=None):
                return pltpu.make_async_remote_copy(
                    src_ref=slot(a, *block) if src is None else src, dst_ref=slot(a, *block),
                    send_sem=send_sems.at[7 * a + k], recv_sem=recv_sems.at[7 * a + k],
                    device_id=to, device_id_type=MESH_T)

            mine = [pltpu.make_async_copy(ins[a], slot(a, *me), local_sems.at[a]) for a in range(na)]
            first, passed = [], []
            for a in range(na):
                first.append(copy(a, 0, me, sibling, src=ins[a]))
                first += [copy(a, 1 + j, me, (*chip, c), src=ins[a]) for j, chip in enumerate(chips)]
                passed += [copy(a, 4 + j, (*chip, c), sibling) for j, chip in enumerate(chips)]

            def start():
                for cp in mine + first:
                    cp.start()

            def middle():
                for a in range(na):
                    for j, chip in enumerate(chips):
                        copy(a, 1 + j, (*chip, c), me).wait_recv()
                        passed[3 * a + j].start()

            def finish():
                for a in range(na):
                    copy(a, 0, sibling, me).wait_recv()
                    for j, chip in enumerate(chips):
                        copy(a, 4 + j, (*chip, 1 - c), me).wait_recv()
                for cp in first + passed:
                    cp.wait_send()
                for cp in mine:
                    cp.wait()

            return start, middle, finish

        me = 4 * x + 2 * y + c
        ns = self.ns
        own = [pltpu.make_async_copy(ins[a].at[me] if a < ns else ins[a], outs[a].at[me], local_sems.at[a])
               for a in range(na)]
        copies = []
        for k in range(1, N_DEV):
            px, py, pc = _peer(k, x, y, c)
            peer = 4 * px + 2 * py + pc
            for a in range(na):
                copies.append(pltpu.make_async_remote_copy(
                    src_ref=ins[a].at[peer] if a < ns else ins[a], dst_ref=outs[a].at[me],
                    send_sem=send_sems.at[na * (k - 1) + a], recv_sem=recv_sems.at[na * (k - 1) + a],
                    device_id=(px, py, pc), device_id_type=MESH_T))

        def start():
            for cp in own + copies:
                cp.start()

        def middle():
            pass

        def finish():
            for cp in copies:
                cp.wait()
            for cp in own:
                cp.wait()

        return start, middle, finish


def _comm_call(comm, name):
    na = comm.na

    def body(*refs):
        start, middle, finish = comm.phases(refs[0:na], refs[na:2 * na], refs[2 * na:])
        start()
        middle()
        finish()

    hbm = pl.BlockSpec(memory_space=pltpu.HBM)
    return pl.pallas_call(
        body, out_shape=comm.out_shape, in_specs=[hbm] * na, out_specs=[hbm] * na, scratch_shapes=comm.scratch,
        name=name)(*comm.arrs)


def _carried_call(body, comm, n_in, n_out, n_scr, when, *, grid, in_specs, out_specs, out_shape, scratch_shapes,
                  operands, name):
    if comm is None:
        return pl.pallas_call(body, grid=grid, in_specs=in_specs, out_specs=out_specs, out_shape=out_shape,
                              scratch_shapes=scratch_shapes, name=name)(*operands), []
    na = comm.na

    def both(*refs):
        a = n_in + na
        b = a + n_out + na
        body(*refs[0:n_in], *refs[a:a + n_out], *refs[b:b + n_scr])
        start, middle, finish = comm.phases(refs[n_in:a], refs[a + n_out:b], refs[b + n_scr:])
        first, mid, last = when()
        pl.when(first)(start)
        pl.when(mid)(middle)
        pl.when(last)(finish)

    hbm = pl.BlockSpec(memory_space=pltpu.HBM)
    res = pl.pallas_call(
        both, grid=grid, in_specs=list(in_specs) + [hbm] * na, out_specs=list(out_specs) + [hbm] * na,
        out_shape=list(out_shape) + comm.out_shape, scratch_shapes=list(scratch_shapes) + comm.scratch,
        name=name)(*operands, *comm.arrs)
    return res[0:n_out], res[n_out:]


def _ffn_fwd(x, nw, w_in, w_out, comm=None):
    t = x.shape[0]
    tm = min(FT, t)
    nt = t // tm

    nj = NJ // FWD_CHUNKS

    def body(x_ref, nw_ref, wg_ref, wu_ref, wo3_ref, o_ref, gu_ref, xn_sc, acc_sc):
        j = pl.program_id(1)

        @pl.when(j == 0)
        def _():
            xv = x_ref[...]
            xn_sc[...] = _c(xv * _rstd(xv) * nw_ref[...])
            acc_sc[...] = jnp.zeros_like(acc_sc)

        xn = xn_sc[...]
        part = None
        for cc in range(FWD_CHUNKS):
            g = _dot(xn, wg_ref[cc])
            u = _dot(xn, wu_ref[cc])
            gu_ref[cc, 0] = _c(g)
            gu_ref[cc, 1] = _c(u)
            p = _dot(_c(g * _sigmoid(g) * u), wo3_ref[2 * cc:2 * cc + 2].reshape(FC, D))
            part = p if part is None else part + p
        acc_sc[...] += part

        @pl.when(j == nj - 1)
        def _():
            o_ref[...] = x_ref[...] + 0.5 * acc_sc[...]

    def when():
        i, j = pl.program_id(0), pl.program_id(1)
        return ((i == 0) & (j == 0), (i == (3 * nt) // 4) & (j == 0), (i == nt - 1) & (j == nj - 1))

    (out, gu), landed = _carried_call(
        body, comm, 5, 2, 2, when,
        grid=(nt, nj),
        in_specs=[pl.BlockSpec((tm, D), lambda i, j: (i, 0)), pl.BlockSpec((1, D), lambda i, j: (0, 0)),
                  pl.BlockSpec((FWD_CHUNKS, D, FC), lambda i, j: (j, 0, 0)),
                  pl.BlockSpec((FWD_CHUNKS, D, FC), lambda i, j: (j + nj, 0, 0)),
                  pl.BlockSpec((2 * FWD_CHUNKS, WO_ROWS, D), lambda i, j: (j, 0, 0))],
        out_specs=[pl.BlockSpec((tm, D), lambda i, j: (i, 0)),
                   pl.BlockSpec((FWD_CHUNKS, 2, tm, FC), lambda i, j: (j, 0, i, 0))],
        out_shape=[jax.ShapeDtypeStruct((t, D), F32), jax.ShapeDtypeStruct((NJ, 2, t, FC), _MM)],
        scratch_shapes=[pltpu.VMEM((tm, D), _MM), pltpu.VMEM((tm, D), F32)],
        operands=(x, nw, w_in, w_in, w_out), name="ffn_fwd")
    return out, gu, landed


def _ffn_bwd(x, nw, w_in, w_out, gu, dy, comm=None):
    t = x.shape[0]
    nt = t // TM
    nc = BWD_CHUNKS
    nj = NJ // nc

    def body(x_ref, nw_ref, wg_ref, wu_ref, wo3_ref, gu_ref, dy_ref, dx_ref, xn_ref, a_ref, dh_ref, dnw_ref, dyb_ref,
             r_sc, acc_sc):
        i = pl.program_id(0)
        j = pl.program_id(1)

        @pl.when(j == 0)
        def _():
            xv = x_ref[...]
            r = _rstd(xv)
            r_sc[...] = r
            xn_ref[...] = _c(xv * r * nw_ref[...])
            dyb_ref[...] = _c(0.5 * dy_ref[...])
            acc_sc[...] = jnp.zeros_like(acc_sc)

        part = None
        for cc in range(nc):
            da = _dot_nt(dyb_ref[...], wo3_ref[2 * cc:2 * cc + 2].reshape(FC, D))
            for half in range(2):
                rs = slice(half * (TM // 2), (half + 1) * (TM // 2))
                g = gu_ref[cc, 0, rs, :].astype(F32)
                u = gu_ref[cc, 1, rs, :].astype(F32)
                s = _sigmoid(g)
                sl = g * s
                a_ref[cc, rs, :] = _c(sl * u)
                dar = da[rs]
                dh_ref[cc, 0, rs, :] = _c(dar * u * (s * (1.0 + g * (1.0 - s))))
                dh_ref[cc, 1, rs, :] = _c(dar * sl)
            p = _dot_nt(dh_ref[cc, 0], wg_ref[cc]) + _dot_nt(dh_ref[cc, 1], wu_ref[cc])
            part = p if part is None else part + p
        acc_sc[...] += part

        @pl.when(j == nj - 1)
        def _():
            r = r_sc[...]
            dx, dnw = _rms_bwd(acc_sc[...], x_ref[...] * r, r, nw_ref[...])
            dx_ref[...] = dy_ref[...] + dx
            _accum(dnw_ref, dnw, i)

    def when():
        i, j = pl.program_id(0), pl.program_id(1)
        return ((i == 0) & (j == 0), (i == (3 * nt) // 4) & (j == 0), (i == nt - 1) & (j == nj - 1))

    return _carried_call(
        body, comm, 7, 6, 2, when,
        grid=(nt, nj),
        in_specs=[pl.BlockSpec((TM, D), lambda i, j: (i, 0)), pl.BlockSpec((1, D), lambda i, j: (0, 0)),
                  pl.BlockSpec((nc, D, FC), lambda i, j: (j, 0, 0)),
                  pl.BlockSpec((nc, D, FC), lambda i, j: (j + nj, 0, 0)),
                  pl.BlockSpec((2 * nc, WO_ROWS, D), lambda i, j: (j, 0, 0)),
                  pl.BlockSpec((nc, 2, TM, FC), lambda i, j: (j, 0, i, 0)),
                  pl.BlockSpec((TM, D), lambda i, j: (i, 0))],
        out_specs=[
            pl.BlockSpec((TM, D), lambda i, j: (i, 0)),
            pl.BlockSpec((TM, D), lambda i, j: (i, 0)),
            pl.BlockSpec((nc, TM, FC), lambda i, j: (j, i, 0)),
            pl.BlockSpec((nc, 2, TM, FC), lambda i, j: (j, 0, i, 0)),
            pl.BlockSpec((1, D), lambda i, j: (0, 0)),
            pl.BlockSpec((TM, D), lambda i, j: (i, 0)),
        ],
        out_shape=[
            jax.ShapeDtypeStruct((t, D), F32),
            jax.ShapeDtypeStruct((t, D), _MM),
            jax.ShapeDtypeStruct((NJ, t, FC), _MM),
            jax.ShapeDtypeStruct((NJ, 2, t, FC), _MM),
            jax.ShapeDtypeStruct((1, D), F32),
            jax.ShapeDtypeStruct((t, D), _MM),
        ],
        scratch_shapes=[pltpu.VMEM((TM, 1), F32), pltpu.VMEM((TM, D), F32)],
        operands=(x, nw, w_in, w_in, w_out, gu, dy), name="ffn_bwd")


def _mm_tn(a, b, bm, bn, bt, out_dtype, name):
    t, m = a.shape
    n = b.shape[1]
    nt = t // bt

    def body(a_ref, b_ref, o_ref, acc_sc):
        k = pl.program_id(2)
        _accum(acc_sc, _dot_tn(_c(a_ref[...]), _c(b_ref[...])), k)

        @pl.when(k == nt - 1)
        def _():
            o_ref[...] = acc_sc[...].astype(out_dtype)

    return pl.pallas_call(
        body,
        grid=(m // bm, n // bn, nt),
        in_specs=[pl.BlockSpec((bt, bm), lambda i, j, k: (k, i)), pl.BlockSpec((bt, bn), lambda i, j, k: (k, j))],
        out_specs=pl.BlockSpec((bm, bn), lambda i, j, k: (i, j)),
        out_shape=jax.ShapeDtypeStruct((m, n), out_dtype),
        scratch_shapes=[pltpu.VMEM((bm, bn), F32)],
        name=name,
    )(a, b)


def _mm_tn_win(xn, dh, comm=None):
    t = xn.shape[0]
    bt = min(BT_WIN, t)
    nt = t // bt

    def body(a_ref, b_ref, o_ref, acc_sc):
        k = pl.program_id(2)
        _accum(acc_sc, _dot_tn(b_ref[...], a_ref[...]), k)

        @pl.when(k == nt - 1)
        def _():
            o_ref[...] = _c(acc_sc[...])

    def when():
        h, j, k = pl.program_id(0), pl.program_id(1), pl.program_id(2)
        start = (h == 0) & (j == 0) & (k == 0)
        return start, start, (h == 1) & (j == NJ - 1) & (k == nt - 1)

    (out,), landed = _carried_call(
        body, comm, 2, 1, 1, when,
        grid=(2, NJ, nt),
        in_specs=[pl.BlockSpec((bt, D), lambda h, j, k: (k, 0)),
                  pl.BlockSpec((None, None, bt, FC), lambda h, j, k: (j, h, k, 0))],
        out_specs=[pl.BlockSpec((None, FC, D), lambda h, j, k: (h * NJ + j, 0, 0))],
        out_shape=[jax.ShapeDtypeStruct((N_DEV, FC, D), _MM)],
        scratch_shapes=[pltpu.VMEM((FC, D), F32)],
        operands=(xn, dh), name="mm_tn_win")
    return out, landed


def _mm_tn_wout(act, dy):
    t = dy.shape[0]
    bt = min(BT_WIN, t)
    nt = t // bt

    def body(a_ref, b_ref, o_ref, acc_sc):
        k = pl.program_id(1)
        _accum(acc_sc, _dot_tn(a_ref[...], b_ref[...]), k)

        @pl.when(k == nt - 1)
        def _():
            o_ref[...] = _c(acc_sc[...].reshape(2, WO_ROWS, D))

    return pl.pallas_call(
        body,
        grid=(NJ, nt),
        in_specs=[pl.BlockSpec((None, bt, FC), lambda j, k: (j, k, 0)), pl.BlockSpec((bt, D), lambda j, k: (k, 0))],
        out_specs=pl.BlockSpec((2, WO_ROWS, D), lambda j, k: (j, 0, 0)),
        out_shape=jax.ShapeDtypeStruct((N_DEV, WO_ROWS, D), _MM),
        scratch_shapes=[pltpu.VMEM((FC, D), F32)],
        name="mm_tn_wout",
    )(act, dy)


def _loss_head(x, nw, tgt):
    t = x.shape[0]

    def body(x_ref, nw_ref, t_ref, loss_ref, dx_ref, dnw_ref):
        i = pl.program_id(0)
        xv = x_ref[...]
        r = _rstd(xv)
        xh = xv * r
        e = xh * nw_ref[...] - t_ref[...]
        part = 0.5 * jnp.sum(jnp.mean(e * e, axis=-1, keepdims=True), axis=0, keepdims=True)
        _accum(loss_ref, jnp.broadcast_to(part, (1, 128)), i)
        dx, dnw = _rms_bwd(e * (1.0 / D), xh, r, nw_ref[...])
        dx_ref[...] = dx
        _accum(dnw_ref, dnw, i)

    return pl.pallas_call(
        body,
        grid=(t // TM,),
        in_specs=[pl.BlockSpec((TM, D), lambda i: (i, 0)), pl.BlockSpec((1, D), lambda i: (0, 0)),
                  pl.BlockSpec((TM, D), lambda i: (i, 0))],
        out_specs=[pl.BlockSpec((1, 128), lambda i: (0, 0)), pl.BlockSpec((TM, D), lambda i: (i, 0)),
                   pl.BlockSpec((1, D), lambda i: (0, 0))],
        out_shape=[jax.ShapeDtypeStruct((1, 128), F32), jax.ShapeDtypeStruct((t, D), F32),
                   jax.ShapeDtypeStruct((1, D), F32)],
        name="loss_head",
    )(x, nw, tgt)


PW_F = QKV + 5 * HW
PW_B = QKV + 3 * HW + 128
AB_MAIN = QKV + HW
AB_GATES = 2 * HEADS


def _ab_proj(x1, nw, wab):
    t = x1.shape[0]

    def body(x_ref, nw_ref, w_ref, h_ref, qkv_ref, z_ref, su_ref, sv_ref, b_ref, a_ref):
        xv = x_ref[...]
        h = _c(xv * _rstd(xv) * nw_ref[...])
        h_ref[...] = h
        p = _dot(h, w_ref[...])
        qkv_ref[...] = p[:, 0:QKV]
        o = QKV
        for ref in (z_ref, su_ref, sv_ref, b_ref, a_ref):
            ref[...] = p[:, o:o + HW]
            o += HW

    row = lambda w: pl.BlockSpec((TM, w), lambda i: (i, 0))
    return pl.pallas_call(
        body,
        grid=(t // TM,),
        in_specs=[row(D), pl.BlockSpec((1, D), lambda i: (0, 0)), pl.BlockSpec((D, PW_F), lambda i: (0, 0))],
        out_specs=[row(D), row(QKV)] + [row(HW)] * 5,
        out_shape=[jax.ShapeDtypeStruct((t, D), _MM), jax.ShapeDtypeStruct((t, QKV), F32)]
        + [jax.ShapeDtypeStruct((t, HW), F32)] * 5,
        name="ab_proj",
    )(x1, nw, wab)


def _conv_rows(x, halo, cw):
    xe = jnp.concatenate([halo, x], axis=0)
    shifted = []
    c = None
    for k in range(4):
        s = 3 - k
        xs = (xe if s == 0 else pltpu.roll(xe, s, 0))[8:, :]
        shifted.append(xs)
        term = cw[k:k + 1, :] * xs
        c = term if c is None else c + term
    return c, shifted


def _head_rsq(a):
    parts = []
    for h in range(HEADS):
        ah = a[:, h * HD:(h + 1) * HD]
        r = lax.rsqrt(jnp.sum(ah * ah, axis=-1, keepdims=True) + EPS)
        parts.append(jnp.broadcast_to(r, ah.shape))
    return jnp.concatenate(parts, axis=-1)


def _head_sum(a):
    parts = []
    for h in range(HEADS):
        ah = a[:, h * HD:(h + 1) * HD]
        parts.append(jnp.broadcast_to(jnp.sum(ah, axis=-1, keepdims=True), ah.shape))
    return jnp.concatenate(parts, axis=-1)


def _softplus(x):
    return jnp.maximum(x, 0.0) + jnp.log1p(jnp.exp(-jnp.abs(x)))


def _halo_prev_spec(width, rows):
    per = TM // rows
    return pl.BlockSpec((rows, width), lambda i: (jnp.maximum(i * per - 1, 0), 0))


def _halo_next_spec(width, rows, t):
    per = TM // rows
    last = t // rows - 1
    return pl.BlockSpec((rows, width), lambda i: (jnp.minimum((i + 1) * per, last), 0))


def _dn_pre(qkv, b_rep, a_rep, cw, alog, dtb):
    t = qkv.shape[0]
    qscale = HD ** -0.5

    def body(x_ref, halo_ref, b_ref, a_ref, cw_ref, alog_ref, dt_ref, q_ref, k_ref, v_ref, beta_ref, g_ref):
        i = pl.program_id(0)
        halo = jnp.where(i == 0, 0.0, halo_ref[...])
        c, _ = _conv_rows(x_ref[...], halo, cw_ref[...])
        sc = c * _sigmoid(c)
        q = sc[:, 0:HW]
        k = sc[:, HW:2 * HW]
        q_ref[...] = q * _head_rsq(q) * qscale
        k_ref[...] = k * _head_rsq(k)
        v_ref[...] = sc[:, 2 * HW:]
        beta_ref[...] = _sigmoid(b_ref[...])
        g_ref[...] = -jnp.exp(alog_ref[...]) * _softplus(a_ref[...] + dt_ref[...])

    row = lambda w: pl.BlockSpec((TM, w), lambda i: (i, 0))
    full = lambda a: pl.BlockSpec(a.shape, lambda i: (0,) * a.ndim)
    return pl.pallas_call(
        body,
        grid=(t // TM,),
        in_specs=[row(QKV), _halo_prev_spec(QKV, 8), row(HW), row(HW), full(cw), full(alog), full(dtb)],
        out_specs=[row(HW)] * 5,
        out_shape=[jax.ShapeDtypeStruct((t, HW), F32)] * 5,
        name="dn_pre",
    )(qkv, qkv, b_rep, a_rep, cw, alog, dtb)


def _unit_lower_inv(los, eye):
    ps = [eye - lo for lo in los]
    lps = list(los)
    for _ in range(5):
        lps = [_dot(_c(lp), _c(lp)) for lp in lps]
        ps = [p + _dot(_c(p), _c(lp)) for p, lp in zip(ps, lps)]
    rs = [eye - (p + _dot3(lo, p)) for lo, p in zip(los, ps)]
    return [p + _dot(_c(p), _c(r)) for p, r in zip(ps, rs)]


def _dn_masks():
    ri = lax.broadcasted_iota(jnp.int32, (DNC, DNC), 0)
    ci = lax.broadcasted_iota(jnp.int32, (DNC, DNC), 1)
    return dict(strict=ri > ci, causal=ri >= ci, eye=(ri == ci).astype(F32),
                ltri=_c((ri >= ci).astype(F32)), upper=_c((ri <= ci).astype(F32)))


def _dn_decay(gr, mk):
    rhs = jnp.concatenate([gr, jnp.where(mk["strict"], gr[:, 0:DNC], 0.0)], axis=1)
    cs = _mask_dot(mk["ltri"], rhs)
    gc = cs[:, 0:HD]
    dm = jnp.where(mk["causal"], jnp.exp(cs[:, HD:HD + DNC]), 0.0)
    gl = jnp.sum(gr, axis=0, keepdims=True)
    return dm, jnp.exp(gc), jnp.exp(gl - gc), gl


def _dn_when(n):
    def when():
        i = pl.program_id(0)
        return (i == 0, i == n // 2, i == n - 1)
    return when


def _dn_fwd(q, k, v, beta, g, comm=None):
    t = q.shape[0]
    rows = DN_STEP * DNC
    n = t // rows

    def body(q_ref, k_ref, v_ref, b_ref, g_ref, o_ref, sall_ref, aall_ref, u_ref, w_ref, s_sc):
        i = pl.program_id(0)

        @pl.when(i == 0)
        def _():
            s_sc[...] = jnp.zeros_like(s_sc)

        mk = _dn_masks()
        idx = [(cc, h) for cc in range(DN_STEP) for h in range(HEADS)]
        at = lambda cc, h: (slice(cc * DNC, (cc + 1) * DNC), slice(h * HD, (h + 1) * HD))
        qs = [q_ref[at(*i)] for i in idx]
        ks = [k_ref[at(*i)] for i in idx]
        bs = [b_ref[at(*i)] for i in idx]
        dec = [_dn_decay(g_ref[at(*i)], mk) for i in idx]
        kbs = [k_ * b_ for k_, b_ in zip(ks, bs)]
        los = [jnp.where(mk["strict"], _dot_nt(_c(kb), _c(k_)) * d[0], 0.0) for kb, k_, d in zip(kbs, ks, dec)]
        inv = _unit_lower_inv(los, mk["eye"])
        uws = [_dot3(a, jnp.concatenate([v_ref[at(*i)] * b_, kb * d[1]], axis=1))
               for a, i, b_, kb, d in zip(inv, idx, bs, kbs, dec)]
        attn = [_c(_dot_nt(_c(q_), _c(k_)) * d[0]) for q_, k_, d in zip(qs, ks, dec)]
        for n_, (cc, h) in enumerate(idx):
            aall_ref[cc, h] = inv[n_]
            u_ref[at(cc, h)] = uws[n_][:, 0:HD]
            w_ref[at(cc, h)] = uws[n_][:, HD:]
        ss = [s_sc[h] for h in range(HEADS)]
        for cc in range(DN_STEP):
            base = cc * HEADS
            for h in range(HEADS):
                sall_ref[cc, h] = ss[h]
            ws = [_dot(_c(jnp.concatenate([uws[base + h][:, HD:], qs[base + h] * dec[base + h][1]], axis=0)),
                       _c(ss[h])) for h in range(HEADS)]
            vn = [_c(uws[base + h][:, 0:HD] - ws[h][0:DNC]) for h in range(HEADS)]
            for h in range(HEADS):
                o_ref[at(cc, h)] = ws[h][DNC:] + _dot(attn[base + h], vn[h])
            ss = [ss[h] * jnp.exp(dec[base + h][3]) + _dot_tn(_c(ks[base + h] * dec[base + h][2]), vn[h])
                  for h in range(HEADS)]
        for h in range(HEADS):
            s_sc[h] = ss[h]

    row = pl.BlockSpec((rows, HW), lambda i: (i, 0))
    return _carried_call(
        body, comm, 5, 5, 1, _dn_when(n),
        grid=(n,),
        in_specs=[row] * 5,
        out_specs=[row, pl.BlockSpec((DN_STEP, HEADS, HD, HD), lambda i: (i, 0, 0, 0)),
                   pl.BlockSpec((DN_STEP, HEADS, DNC, DNC), lambda i: (i, 0, 0, 0)), row, row],
        out_shape=[jax.ShapeDtypeStruct((t, HW), F32), jax.ShapeDtypeStruct((t // DNC, HEADS, HD, HD), F32),
                   jax.ShapeDtypeStruct((t // DNC, HEADS, DNC, DNC), F32), jax.ShapeDtypeStruct((t, HW), F32),
                   jax.ShapeDtypeStruct((t, HW), F32)],
        scratch_shapes=[pltpu.VMEM((HEADS, HD, HD), F32)],
        operands=(q, k, v, beta, g), name="dn_fwd")


def _dn_bwd(q, k, v, beta, g, sall, aall, u, w, do, comm=None):
    t = q.shape[0]
    rows = DN_STEP * DNC
    n = t // rows

    def body(q_ref, k_ref, v_ref, b_ref, g_ref, sall_ref, aall_ref, u_ref, w_ref, do_ref,
             dq_ref, dk_ref, dv_ref, db_ref, dg_ref, ds_sc):
        i = pl.program_id(0)

        @pl.when(i == 0)
        def _():
            ds_sc[...] = jnp.zeros_like(ds_sc)

        mk = _dn_masks()
        strict = mk["strict"]
        hs = range(HEADS)
        at = lambda cc, h: (slice(cc * DNC, (cc + 1) * DNC), slice(h * HD, (h + 1) * HD))
        rowsum = lambda a: jnp.sum(a, axis=-1, keepdims=True)
        dsn = [ds_sc[h] for h in hs]
        for cc in reversed(range(DN_STEP)):
            q = [q_ref[at(cc, h)] for h in hs]
            k = [k_ref[at(cc, h)] for h in hs]
            b = [b_ref[at(cc, h)] for h in hs]
            u = [u_ref[at(cc, h)] for h in hs]
            w = [w_ref[at(cc, h)] for h in hs]
            do = [do_ref[at(cc, h)] for h in hs]
            s = [sall_ref[cc, h] for h in hs]
            dec = [_dn_decay(g_ref[at(cc, h)], mk) for h in hs]
            dm, e, f = [d[0] for d in dec], [d[1] for d in dec], [d[2] for d in dec]
            egl = [jnp.exp(d[3]) for d in dec]
            kb = [k[h] * b[h] for h in hs]
            kc = [_c(k[h]) for h in hs]
            sb = [_c(s[h]) for h in hs]
            dob = [_c(do[h]) for h in hs]
            m = [_dot_nt(_c(kb[h]), kc[h]) for h in hs]
            p = [_dot_nt(_c(q[h]), kc[h]) for h in hs]
            vnb = [_c(u[h] - _dot(_c(w[h]), sb[h])) for h in hs]
            dsb = [_c(dsn[h]) for h in hs]
            dvn = [_dot_tn(_c(p[h] * dm[h]), dob[h]) + _dot(_c(k[h] * f[h]), dsb[h]) for h in hs]
            dov = [_c(jnp.concatenate([do[h], dvn[h]], axis=0)) for h in hs]
            t1 = [_dot_nt(dov[h], sb[h]) for h in hs]
            dattn = [_dot_nt(dob[h], vnb[h]) for h in hs]
            dkt = [_dot_nt(vnb[h], dsb[h]) for h in hs]
            dgl = [jnp.sum(jnp.sum(dsn[h] * s[h], axis=1, keepdims=True), axis=0, keepdims=True) * egl[h][:, 0:1]
                   for h in hs]
            dsn = [dsn[h] * egl[h] + _dot_tn(_c(jnp.concatenate([q[h] * e[h], -w[h]], axis=0)), dov[h]) for h in hs]
            dqd = [t1[h][0:DNC] for h in hs]
            dw = [-t1[h][DNC:] for h in hs]
            ab = [_dot3(aall_ref[cc, h], jnp.concatenate([dvn[h], dw[h]], axis=1), TN) for h in hs]
            dlo = [jnp.where(strict, -_dot3(ab[h], jnp.concatenate([u[h], w[h]], axis=1), NT), 0.0) for h in hs]
            dpm = [_c(jnp.concatenate([dattn[h] * dm[h], dlo[h] * dm[h]], axis=0)) for h in hs]
            t2 = [_dot(dpm[h], kc[h]) for h in hs]
            t4 = [_dot_tn(dpm[h], _c(jnp.concatenate([q[h], kb[h]], axis=0))) for h in hs]
            dff = [rowsum(dkt[h] * k[h]) * f[h][:, 0:1] for h in hs]
            de = [rowsum(dqd[h] * q[h]) + rowsum(ab[h][:, HD:] * kb[h]) for h in hs]
            dd = [(dattn[h] * p[h] + dlo[h] * m[h]) * dm[h] for h in hs]
            t3 = [_mask_dot(mk["upper"], jnp.concatenate(
                [jnp.broadcast_to(de[h] * e[h][:, 0:1] - dff[h], (DNC, HD)), dd[h]], axis=1)) for h in hs]
            for h in hs:
                dvb, dkbe = ab[h][:, 0:HD], ab[h][:, HD:]
                dkb = t2[h][DNC:] + dkbe * e[h]
                dbeta = rowsum(dkb * k[h]) + rowsum(dvb * v_ref[at(cc, h)])
                dg = (rowsum(jnp.where(strict, t3[h][:, HD:HD + DNC], 0.0)) + t3[h][:, 0:1]
                      + dgl[h] + jnp.sum(dff[h], axis=0, keepdims=True))
                dq_ref[at(cc, h)] = dqd[h] * e[h] + t2[h][0:DNC]
                dk_ref[at(cc, h)] = t4[h] + dkt[h] * f[h] + dkb * b[h]
                dv_ref[at(cc, h)] = dvb * b[h]
                db_ref[at(cc, h)] = jnp.broadcast_to(dbeta, (DNC, HD))
                dg_ref[at(cc, h)] = jnp.broadcast_to(dg, (DNC, HD))
        for h in hs:
            ds_sc[h] = dsn[h]

    row = pl.BlockSpec((rows, HW), lambda i: (n - 1 - i, 0))
    return _carried_call(
        body, comm, 10, 5, 1, _dn_when(n),
        grid=(n,),
        in_specs=[row] * 5 + [pl.BlockSpec((DN_STEP, HEADS, HD, HD), lambda i: (n - 1 - i, 0, 0, 0)),
                              pl.BlockSpec((DN_STEP, HEADS, DNC, DNC), lambda i: (n - 1 - i, 0, 0, 0)), row, row, row],
        out_specs=[row] * 5,
        out_shape=[jax.ShapeDtypeStruct((t, HW), F32)] * 5,
        scratch_shapes=[pltpu.VMEM((HEADS, HD, HD), F32)],
        operands=(q, k, v, beta, g, sall, aall, u, w, do), name="dn_bwd")


def _group_norm(a, nw):
    rs = []
    for h in range(HEADS):
        ah = a[:, h * HD:(h + 1) * HD]
        rs.append(jnp.broadcast_to(_rstd(ah), ah.shape))
    r = jnp.concatenate(rs, axis=-1)
    xh = a * r
    return xh * nw, xh, r


def _group_norm_bwd(dy, xh, r, nw):
    dxh = dy * nw
    return r * (dxh - xh * (_head_sum(dxh * xh) * (1.0 / HD)))


def _sg_mix(wt_ref, svn_b, nchunk):
    rows = []
    for cidx in range(nchunk):
        cols = []
        for g in range(HEADS):
            blk = svn_b[cidx * SGC:(cidx + 1) * SGC, g * HD:(g + 1) * HD]
            cols.append(_dot(wt_ref[g], blk))
        rows.append(jnp.concatenate(cols, axis=-1))
    return jnp.concatenate(rows, axis=0)


def _ab_out(x1, o, z, su, sv, dnw, sgnw, wtril, sgb, wout):
    t = x1.shape[0]
    nchunk = TM // SGC

    def body(x_ref, o_ref, z_ref, su_ref, sv_ref, dnw_ref, sgnw_ref, wt_ref, sgb_ref, wo_ref, x2_ref, cat_ref):
        on, _, _ = _group_norm(o_ref[...], dnw_ref[...])
        zv = z_ref[...]
        cat_ref[:, 0:HW] = _c(on * (zv * _sigmoid(zv)))
        svn, _, _ = _group_norm(_gelu(sv_ref[...]), sgnw_ref[...])
        mixed = _sg_mix(wt_ref, _c(svn), nchunk) + jnp.tile(sgb_ref[...], (nchunk, 1))
        cat_ref[:, HW:] = _c(_gelu(su_ref[...]) * mixed)
        x2_ref[...] = x_ref[...] + _dot(cat_ref[...], wo_ref[...])

    row = lambda w: pl.BlockSpec((TM, w), lambda i: (i, 0))
    full = lambda a: pl.BlockSpec(a.shape, lambda i: (0,) * a.ndim)
    return pl.pallas_call(
        body,
        grid=(t // TM,),
        in_specs=[row(D)] + [row(HW)] * 4 + [full(dnw), full(sgnw), full(wtril), full(sgb), full(wout)],
        out_specs=[row(D), row(D)],
        out_shape=[jax.ShapeDtypeStruct((t, D), F32), jax.ShapeDtypeStruct((t, D), _MM)],
        name="ab_out",
    )(x1, o, z, su, sv, dnw, sgnw, wtril, sgb, wout)


def _ab_out_bwd(dx2, o, z, su, sv, dnw, sgnw, wtril, wtril_t, sgb, wout):
    t = dx2.shape[0]
    nchunk = TM // SGC

    def body(dx_ref, o_ref, z_ref, su_ref, sv_ref, dnw_ref, sgnw_ref, wt_ref, wtt_ref, sgb_ref, wo_ref,
             do_ref, dz_ref, dsu_ref, dsv_ref, ddnw_ref, dsgnw_ref, dsgw_ref, dsgb_ref):
        i = pl.program_id(0)
        dcat = _dot_nt(_c(dx_ref[...]), wo_ref[...])
        doa = dcat[:, 0:HW]
        dob = dcat[:, HW:]
        on, oh, ro = _group_norm(o_ref[...], dnw_ref[...])
        zv = z_ref[...]
        sz = _sigmoid(zv)
        dz_ref[...] = _c(doa * on * (sz * (1.0 + zv * (1.0 - sz))))
        don = doa * (zv * sz)
        do_ref[...] = _group_norm_bwd(don, oh, ro, dnw_ref[...])
        dd = jnp.sum(don * oh, axis=0, keepdims=True)
        _accum(ddnw_ref, dd[:, 0:HD] + dd[:, HD:2 * HD] + dd[:, 2 * HD:3 * HD] + dd[:, 3 * HD:], i)
        suv = su_ref[...]
        svv = sv_ref[...]
        svg = _gelu(svv)
        svn, sh, rs = _group_norm(svg, sgnw_ref[...])
        svn_b = _c(svn)
        mixed = _sg_mix(wt_ref, svn_b, nchunk) + jnp.tile(sgb_ref[...], (nchunk, 1))
        dsu_ref[...] = _c(dob * mixed * _gelu_grad(suv))
        dmixed = dob * _gelu(suv)
        dmb = _c(dmixed)
        tri = lax.broadcasted_iota(jnp.int32, (SGC, SGC), 0) >= lax.broadcasted_iota(jnp.int32, (SGC, SGC), 1)
        lane = lax.broadcasted_iota(jnp.int32, (SGC, HD), 1)
        rows = []
        dbias = jnp.zeros((SGC, HD), F32)
        for g in range(HEADS):
            gs = slice(g * HD, (g + 1) * HD)
            dwg = jnp.zeros((SGC, SGC), F32)
            col = jnp.zeros((SGC, 1), F32)
            for cidx in range(nchunk):
                cs = slice(cidx * SGC, (cidx + 1) * SGC)
                dwg = dwg + _dot_nt(dmb[cs, gs], svn_b[cs, gs])
                col = col + jnp.sum(dmixed[cs, gs], axis=-1, keepdims=True)
            _accum(dsgw_ref.at[g], jnp.where(tri, dwg, 0.0), i)
            dbias = dbias + jnp.where(lane == g, col, 0.0)
        _accum(dsgb_ref, dbias, i)
        for cidx in range(nchunk):
            cs = slice(cidx * SGC, (cidx + 1) * SGC)
            rows.append(jnp.concatenate(
                [_dot(wtt_ref[g], dmb[cs, g * HD:(g + 1) * HD]) for g in range(HEADS)], axis=-1))
        dsvn = jnp.concatenate(rows, axis=0)
        _accum(dsgnw_ref, jnp.sum(dsvn * sh, axis=0, keepdims=True), i)
        dsv_ref[...] = _c(_group_norm_bwd(dsvn, sh, rs, sgnw_ref[...]) * _gelu_grad(svv))

    row = lambda w: pl.BlockSpec((TM, w), lambda i: (i, 0))
    full = lambda a: pl.BlockSpec(a.shape, lambda i: (0,) * a.ndim)
    const = lambda shape: pl.BlockSpec(shape, lambda i: (0,) * len(shape))
    return pl.pallas_call(
        body,
        grid=(t // TM,),
        in_specs=[row(D)] + [row(HW)] * 4 + [full(dnw), full(sgnw), full(wtril), full(wtril_t), full(sgb), full(wout)],
        out_specs=[row(HW)] * 4 + [const((1, HD)), const((1, HW)), const((HEADS, SGC, SGC)), const((SGC, HD))],
        out_shape=[jax.ShapeDtypeStruct((t, HW), F32)] + [jax.ShapeDtypeStruct((t, HW), _MM)] * 3
        + [jax.ShapeDtypeStruct((1, HD), F32), jax.ShapeDtypeStruct((1, HW), F32),
           jax.ShapeDtypeStruct((HEADS, SGC, SGC), F32), jax.ShapeDtypeStruct((SGC, HD), F32)],
        name="ab_out_bwd",
    )(dx2, o, z, su, sv, dnw, sgnw, wtril, wtril_t, sgb, wout)


def _dn_pre_bwd(qkv, b_rep, a_rep, cw, alog, dtb, dqn, dkn, dv, dbeta, dg):
    t = qkv.shape[0]
    qscale = HD ** -0.5

    def body(x_ref, halo_ref, b_ref, a_ref, cw_ref, alog_ref, dt_ref, dq_ref, dk_ref, dv_ref, dbeta_ref, dg_ref,
             dc_ref, dba_ref, dcw_ref, dalog_ref, ddt_ref):
        i = pl.program_id(0)
        halo = jnp.where(i == 0, 0.0, halo_ref[...])
        c, shifted = _conv_rows(x_ref[...], halo, cw_ref[...])
        s = _sigmoid(c)
        sc = c * s
        q = sc[:, 0:HW]
        k = sc[:, HW:2 * HW]
        rq = _head_rsq(q)
        rk = _head_rsq(k)
        qu = q * rq
        ku = k * rk
        dqn = dq_ref[...]
        dkn = dk_ref[...]
        dq = qscale * rq * (dqn - qu * _head_sum(dqn * qu))
        dk = rk * (dkn - ku * _head_sum(dkn * ku))
        dsc = jnp.concatenate([dq, dk, dv_ref[...]], axis=-1)
        dc = dsc * (s * (1.0 + c * (1.0 - s)))
        dc_ref[...] = _c(dc)
        for kk in range(4):
            _accum(dcw_ref.at[kk], jnp.sum(dc * shifted[kk], axis=0, keepdims=True), i)
        beta = _sigmoid(b_ref[...])
        dbp = dbeta_ref[...] * beta * (1.0 - beta)
        nea = -jnp.exp(alog_ref[...])
        spin = a_ref[...] + dt_ref[...]
        dgv = dg_ref[...]
        dap = dgv * nea * _sigmoid(spin)
        _accum(dalog_ref, jnp.sum(dgv * nea * _softplus(spin), axis=0, keepdims=True), i)
        _accum(ddt_ref, jnp.sum(dap, axis=0, keepdims=True), i)
        lane = lax.broadcasted_iota(jnp.int32, (TM, HD), 1)
        dba = jnp.zeros((TM, HD), F32)
        for h in range(HEADS):
            dba = dba + jnp.where(lane == h, dbp[:, h * HD:(h + 1) * HD], 0.0)
            dba = dba + jnp.where(lane == HEADS + h, dap[:, h * HD:(h + 1) * HD], 0.0)
        dba_ref[...] = _c(dba)

    row = lambda w: pl.BlockSpec((TM, w), lambda i: (i, 0))
    full = lambda a: pl.BlockSpec(a.shape, lambda i: (0,) * a.ndim)
    const = lambda shape: pl.BlockSpec(shape, lambda i: (0,) * len(shape))
    return pl.pallas_call(
        body,
        grid=(t // TM,),
        in_specs=[row(QKV), _halo_prev_spec(QKV, 8), row(HW), row(HW), full(cw), full(alog), full(dtb)] + [row(HW)] * 5,
        out_specs=[row(QKV), row(HD), const((4, 1, QKV)), const((1, HW)), const((1, HW))],
        out_shape=[jax.ShapeDtypeStruct((t, QKV), _MM), jax.ShapeDtypeStruct((t, HD), _MM),
                   jax.ShapeDtypeStruct((4, 1, QKV), F32), jax.ShapeDtypeStruct((1, HW), F32),
                   jax.ShapeDtypeStruct((1, HW), F32)],
        name="dn_pre_bwd",
    )(qkv, qkv, b_rep, a_rep, cw, alog, dtb, dqn, dkn, dv, dbeta, dg)


def _conv_bwd(dc, cw):
    t = dc.shape[0]
    nt = t // TM

    def body(dc_ref, halo_ref, cw_ref, dx_ref):
        i = pl.program_id(0)
        halo = jnp.where(i == nt - 1, 0.0, halo_ref[...].astype(F32))
        de = jnp.concatenate([dc_ref[...].astype(F32), halo], axis=0)
        cwv = cw_ref[...]
        acc = None
        for k in range(4):
            s = 3 - k
            ds = (de if s == 0 else pltpu.roll(de, TM + HALO - s, 0))[0:TM, :]
            term = cwv[k:k + 1, :] * ds
            acc = term if acc is None else acc + term
        dx_ref[...] = _c(acc)

    return pl.pallas_call(
        body,
        grid=(nt,),
        in_specs=[pl.BlockSpec((TM, QKV), lambda i: (i, 0)), _halo_next_spec(QKV, HALO, t),
                  pl.BlockSpec(cw.shape, lambda i: (0, 0))],
        out_specs=pl.BlockSpec((TM, QKV), lambda i: (i, 0)),
        out_shape=jax.ShapeDtypeStruct((t, QKV), _MM),
        name="conv_bwd",
    )(dc, dc, cw)


def _ab_proj_bwd(x1, nw, dqkv, dz, dsu, dsv, dba, wab_b, dres):
    t = x1.shape[0]

    def body(x_ref, nw_ref, dqkv_ref, dz_ref, dsu_ref, dsv_ref, dba_ref, w_ref, dres_ref, dx_ref, dcat_ref, dnw_ref):
        i = pl.program_id(0)
        dcat_ref[:, 0:QKV] = dqkv_ref[...]
        o = QKV
        for ref in (dz_ref, dsu_ref, dsv_ref):
            dcat_ref[:, o:o + HW] = ref[...]
            o += HW
        dcat_ref[:, o:o + 128] = dba_ref[...]
        dh = _dot_nt(dcat_ref[...], w_ref[...])
        xv = x_ref[...]
        r = _rstd(xv)
        dx, dnw = _rms_bwd(dh, xv * r, r, nw_ref[...])
        dx_ref[...] = dres_ref[...] + dx
        _accum(dnw_ref, dnw, i)

    row = lambda w: pl.BlockSpec((TM, w), lambda i: (i, 0))
    return pl.pallas_call(
        body,
        grid=(t // TM,),
        in_specs=[row(D), pl.BlockSpec((1, D), lambda i: (0, 0)), row(QKV), row(HW), row(HW), row(HW), row(128),
                  pl.BlockSpec((D, PW_B), lambda i: (0, 0)), row(D)],
        out_specs=[row(D), row(PW_B), pl.BlockSpec((1, D), lambda i: (0, 0))],
        out_shape=[jax.ShapeDtypeStruct((t, D), F32), jax.ShapeDtypeStruct((t, PW_B), _MM),
                   jax.ShapeDtypeStruct((1, D), F32)],
        name="ab_proj_bwd",
    )(x1, nw, dqkv, dz, dsu, dsv, dba, wab_b, dres)


def _pool_counts(i):
    pos = (lax.broadcasted_iota(jnp.int32, (TM + HALO, 1), 0) + i * TM + 1).astype(F32)
    return [1.0 / jnp.minimum(pos, float(w)) for w in POOL_WINDOWS]


def _window_sum(ext, win, back):
    r = ext.shape[0]
    s = ext
    step = 1
    while step < win:
        s = s + pltpu.roll(s, step if back else r - step, 0)
        step *= 2
    return s


def _pooled(h_ext, invc, g):
    gs = slice(g * PG, (g + 1) * PG)
    he = h_ext[:, gs]
    ws = _window_sum(he, POOL_WINDOWS[g], True)[HALO:, :]
    return ws * invc[g][0:TM, :] - he[HALO:, :]


def _pool_fwd(x1, nw, pw, scale):
    t = x1.shape[0]

    def body(x_ref, halo_ref, nw_ref, pw_ref, sc_ref, x2_ref):
        i = pl.program_id(0)
        xv = x_ref[...]
        hv = halo_ref[...]
        nwv = nw_ref[...]
        h_ext = jnp.concatenate([jnp.where(i == 0, 0.0, hv * _rstd(hv) * nwv), xv * _rstd(xv) * nwv], axis=0)
        invc = _pool_counts(i)
        outs = [_dot(_c(_pooled(h_ext, invc, g)), pw_ref[g]) for g in range(4)]
        x2_ref[...] = xv + jnp.concatenate(outs, axis=-1) * sc_ref[...]

    return pl.pallas_call(
        body,
        grid=(t // TM,),
        in_specs=[pl.BlockSpec((TM, D), lambda i: (i, 0)), _halo_prev_spec(D, HALO),
                  pl.BlockSpec((1, D), lambda i: (0, 0)), pl.BlockSpec((4, PG, PG), lambda i: (0, 0, 0)),
                  pl.BlockSpec((1, D), lambda i: (0, 0))],
        out_specs=pl.BlockSpec((TM, D), lambda i: (i, 0)),
        out_shape=jax.ShapeDtypeStruct((t, D), F32),
        name="pool_fwd",
    )(x1, x1, nw, pw, scale)


def _pool_bwd(x1, nw, pw, scale, dx2):
    t = x1.shape[0]
    nt = t // TM

    def body(x_ref, halo_ref, nw_ref, pw_ref, sc_ref, dx2_ref, dnext_ref, dx_ref, dnw_ref, dpw_ref, dsc_ref):
        i = pl.program_id(0)
        xv = x_ref[...]
        hv = halo_ref[...]
        nwv = nw_ref[...]
        r = _rstd(xv)
        xh = xv * r
        h_ext = jnp.concatenate([jnp.where(i == 0, 0.0, hv * _rstd(hv) * nwv), xh * nwv], axis=0)
        invc = _pool_counts(i)
        dyv = dx2_ref[...]
        dout_ext = jnp.concatenate([dyv, jnp.where(i == nt - 1, 0.0, dnext_ref[...])], axis=0) * sc_ref[...]
        dh_parts = []
        dsc_parts = []
        for g in range(4):
            gs = slice(g * PG, (g + 1) * PG)
            pooled_b = _c(_pooled(h_ext, invc, g))
            dout_b = _c(dout_ext[:, gs])
            dsc_parts.append(jnp.sum(dyv[:, gs] * _dot(pooled_b, pw_ref[g]), axis=0, keepdims=True))
            _accum(dpw_ref.at[g], _dot_tn(pooled_b, dout_b[0:TM, :]), i)
            dpool_ext = _dot_nt(dout_b, pw_ref[g])
            lead = _window_sum(dpool_ext * invc[g], POOL_WINDOWS[g], False)[0:TM, :]
            dh_parts.append(lead - dpool_ext[0:TM, :])
        _accum(dsc_ref, jnp.concatenate(dsc_parts, axis=-1), i)
        dx, dnw = _rms_bwd(jnp.concatenate(dh_parts, axis=-1), xh, r, nwv)
        dx_ref[...] = dyv + dx
        _accum(dnw_ref, dnw, i)

    vec = pl.BlockSpec((1, D), lambda i: (0, 0))
    return pl.pallas_call(
        body,
        grid=(nt,),
        in_specs=[pl.BlockSpec((TM, D), lambda i: (i, 0)), _halo_prev_spec(D, HALO), vec,
                  pl.BlockSpec((4, PG, PG), lambda i: (0, 0, 0)), vec,
                  pl.BlockSpec((TM, D), lambda i: (i, 0)), _halo_next_spec(D, HALO, t)],
        out_specs=[pl.BlockSpec((TM, D), lambda i: (i, 0)), vec, pl.BlockSpec((4, PG, PG), lambda i: (0, 0, 0)), vec],
        out_shape=[jax.ShapeDtypeStruct((t, D), F32), jax.ShapeDtypeStruct((1, D), F32),
                   jax.ShapeDtypeStruct((4, PG, PG), F32), jax.ShapeDtypeStruct((1, D), F32)],
        name="pool_bwd",
    )(x1, x1, nw, pw, scale, dx2, dx2)


def _adamw(lands, w, m, v, rb, name):
    nl, nr = w.shape[0], w.shape[1]
    rest = w.shape[2:]
    ns = lands[0].shape[0]
    zeros = (0,) * len(rest)

    def body(*refs):
        l_refs = refs[0:nl]
        w_ref, m_ref, v_ref, g_ref, d_ref, m2_ref, v2_ref = refs[nl:]
        for l in range(nl):
            g = l_refs[l][0].astype(F32)
            for s in range(1, ns):
                g = g + l_refs[l][s].astype(F32)
            m2 = ADAM_B1 * m_ref[l] + (1.0 - ADAM_B1) * g
            v2 = ADAM_B2 * v_ref[l] + (1.0 - ADAM_B2) * (g * g)
            m_hat = m2 / (1.0 - ADAM_B1 ** ADAM_STEP)
            v_hat = v2 / (1.0 - ADAM_B2 ** ADAM_STEP)
            g_ref[l] = g
            d_ref[l] = -ADAM_LR * (m_hat / (jnp.sqrt(v_hat) + ADAM_EPS) + ADAM_WD * w_ref[l])
            m2_ref[l] = m2
            v2_ref[l] = v2

    lspec = pl.BlockSpec((ns, rb) + rest, lambda r: (0, r) + zeros)
    wspec = pl.BlockSpec((nl, rb) + rest, lambda r: (0, r) + zeros)
    return pl.pallas_call(
        body,
        grid=(nr // rb,),
        in_specs=[lspec] * nl + [wspec] * 3,
        out_specs=[wspec] * 4,
        out_shape=[jax.ShapeDtypeStruct(w.shape, F32)] * 4,
        name=name,
    )(*lands, w, m, v)


WEIGHT_ORDER = ("ffn_norm1", "ffn1_w_in", "ffn1_w_out", "mix_norm", "ffn_norm2", "ffn2_w_in", "ffn2_w_out", "ab_w_in",
                "dn_conv_w", "dn_a_log", "dn_dt_bias", "dn_out_norm", "sg_norm", "sg_w", "sg_b", "ab_w_out", "pool_w",
                "pool_scale", "final_norm")
R_SMALL = 88
SMALL_ROWS = (
    ("ffn_norm1", (2, D), 2), ("mix_norm", (2, D), 2), ("ffn_norm2", (2, D), 2), ("final_norm", (D,), 1),
    ("sg_w", (1, 4, SGC, SGC), 64), ("sg_norm", (1, 4, HD), 1), ("sg_b", (1, 4, SGC), 1), ("dn_out_norm", (1, HD), 1),
    ("dn_a_log", (1, 4), 1), ("dn_dt_bias", (1, 4), 1), ("pool_scale", (1, D), 1), ("dn_conv_w", (1, 4, QKV), 8),
)
SMALL_SHARDED = ("pool_scale", "dn_conv_w")


def _rows_of(a, rows):
    if a.shape[-1] == QKV:
        return jnp.pad(a.reshape(4, QKV), ((0, 0), (0, 2 * ROW - QKV))).reshape(8, ROW)
    n = _numel(a.shape)
    if n % ROW == 0:
        return a.reshape(n // ROW, ROW)
    return jnp.pad(a.reshape(1, n), ((0, 0), (0, ROW - n)))


def _from_rows(r, shape):
    if shape[-1] == QKV:
        return r.reshape(4, 2 * ROW)[:, 0:QKV].reshape(shape)
    n = _numel(shape)
    if n % ROW == 0:
        return r.reshape(shape)
    return r[:, 0:n].reshape(shape)


def _pack_small(vals):
    parts = [(_rows_of(vals[n].astype(F32), r) if n in vals else jnp.zeros((r, ROW), F32)) for n, _, r in SMALL_ROWS]
    used = sum(r for _, _, r in SMALL_ROWS)
    return jnp.concatenate(parts + [jnp.zeros((R_SMALL - used, ROW), F32)], axis=0)


def _unpack_small(packed):
    out, o = {}, 0
    for n, shape, r in SMALL_ROWS:
        out[n] = _from_rows(packed[o:o + r], shape)
        o += r
    return out


def _pack_small_shard(ps, cw):
    return jnp.concatenate([
        jnp.pad(ps, ((0, 0), (0, ROW - D // N_DEV))), jnp.pad(cw[0], ((0, 0), (0, ROW - QKV // N_DEV))),
        jnp.zeros((3, ROW), F32)], axis=0)


def _mixer_weights(g_in, g_out, g_small, small):
    w = {}
    wi = jnp.transpose(g_in, (1, 0, 2)).reshape(D, AB_IN)
    gates = wi[:, AB_MAIN:AB_MAIN + AB_GATES]
    main = [wi[:, 0:AB_MAIN], wi[:, AB_MAIN + AB_GATES:AB_IN]]
    w["wab_f"] = jnp.concatenate(
        main + [jnp.repeat(gates[:, 0:HEADS], HD, axis=1), jnp.repeat(gates[:, HEADS:], HD, axis=1)], axis=1)
    w["wab_b"] = jnp.concatenate(main + [gates, jnp.zeros((D, 128 - AB_GATES), wi.dtype)], axis=1)
    w["cw"] = jnp.transpose(g_small[:, 1:5, 0:QKV // N_DEV], (1, 0, 2)).reshape(4, QKV)
    w["ps"] = g_small[:, 0, 0:D // N_DEV].reshape(1, D)
    w["alog"] = jnp.repeat(small["dn_a_log"][0].astype(F32), HD).reshape(1, HW)
    w["dtb"] = jnp.repeat(small["dn_dt_bias"][0].astype(F32), HD).reshape(1, HW)
    w["dnw"] = jnp.tile(small["dn_out_norm"][0].astype(F32), HEADS).reshape(1, HW)
    w["sgnw"] = small["sg_norm"][0].astype(F32).reshape(1, HW)
    tri = jnp.tril(jnp.ones((SGC, SGC), dtype=bool))
    wt = jnp.where(tri, small["sg_w"][0].astype(F32), 0.0)
    w["wtril"] = _c(wt)
    w["wtril_t"] = _c(jnp.transpose(wt, (0, 2, 1)))
    w["sgb"] = jnp.repeat(jnp.transpose(small["sg_b"][0].astype(F32)), HD, axis=1)
    w["wout_ab"] = g_out.reshape(D, D)
    return w


def kernel(x, ffn_norm1, ffn1_w_in, ffn1_w_out, mix_norm, ffn_norm2, ffn2_w_in, ffn2_w_out, ab_w_in, dn_conv_w, dn_a_log, dn_dt_bias, dn_out_norm, sg_norm, sg_w, sg_b, ab_w_out, pool_w, pool_scale, final_norm, loss_target, m_ffn_norm1, m_ffn1_w_in, m_ffn1_w_out, m_mix_norm, m_ffn_norm2, m_ffn2_w_in, m_ffn2_w_out, m_ab_w_in, m_dn_conv_w, m_dn_a_log, m_dn_dt_bias, m_dn_out_norm, m_sg_norm, m_sg_w, m_sg_b, m_ab_w_out, m_pool_w, m_pool_scale, m_final_norm, v_ffn_norm1, v_ffn1_w_in, v_ffn1_w_out, v_mix_norm, v_ffn_norm2, v_ffn2_w_in, v_ffn2_w_out, v_ab_w_in, v_dn_conv_w, v_dn_a_log, v_dn_dt_bias, v_dn_out_norm, v_sg_norm, v_sg_w, v_sg_b, v_ab_w_out, v_pool_w, v_pool_scale, v_final_norm):
    wl = dict(ffn_norm1=ffn_norm1, mix_norm=mix_norm, ffn_norm2=ffn_norm2, dn_a_log=dn_a_log, dn_dt_bias=dn_dt_bias,
              dn_out_norm=dn_out_norm, sg_norm=sg_norm, sg_w=sg_w, sg_b=sg_b, final_norm=final_norm)
    ml = dict(ffn_norm1=m_ffn_norm1, mix_norm=m_mix_norm, ffn_norm2=m_ffn_norm2, dn_a_log=m_dn_a_log,
              dn_dt_bias=m_dn_dt_bias, dn_out_norm=m_dn_out_norm, sg_norm=m_sg_norm, sg_w=m_sg_w, sg_b=m_sg_b,
              final_norm=m_final_norm)
    vl = dict(ffn_norm1=v_ffn_norm1, mix_norm=v_mix_norm, ffn_norm2=v_ffn_norm2, dn_a_log=v_dn_a_log,
              dn_dt_bias=v_dn_dt_bias, dn_out_norm=v_dn_out_norm, sg_norm=v_sg_norm, sg_w=v_sg_w, sg_b=v_sg_b,
              final_norm=v_final_norm)
    row = lambda a: a.reshape(1, -1).astype(F32)
    n1 = [row(ffn_norm1[l]) for l in range(2)]
    n2 = [row(ffn_norm2[l]) for l in range(2)]
    mix = [row(mix_norm[l]) for l in range(2)]
    s_in = {(f, l): _c(wf[l]) for f, wf in enumerate((ffn1_w_in, ffn2_w_in)) for l in range(2)}
    s_out = {(f, l): _c(wf[l]) for f, wf in enumerate((ffn1_w_out, ffn2_w_out)) for l in range(2)}
    xs, tgt = x[0], loss_target[0]

    wi00, wo00 = _comm_call(_Comm("gather", [s_in[0, 0], s_out[0, 0]]), "gather_first")
    x01, gu00, (g_abin, g_about, g_small, wi10) = _ffn_fwd(
        xs, n1[0], wi00, wo00,
        comm=_Comm("gather", [_c(ab_w_in[0]), _c(ab_w_out[0]), _pack_small_shard(pool_scale, dn_conv_w), s_in[1, 0]]))
    w = _mixer_weights(g_abin, g_about, g_small, wl)
    h, qkv, z, su, sv, b_rep, a_rep = _ab_proj(x01, mix[0], w["wab_f"])
    qn, kn, v, beta, g = _dn_pre(qkv, b_rep, a_rep, w["cw"], w["alog"], w["dtb"])
    (o, sall, aall, dn_u, dn_w), (wo10, g_pw) = _dn_fwd(
        qn, kn, v, beta, g, comm=_Comm("gather", [s_out[1, 0], _c(pool_w[0])]))
    pw = jnp.transpose(g_pw, (1, 0, 2, 3)).reshape(4, PG, PG)
    x02, cat = _ab_out(x01, o, z, su, sv, w["dnw"], w["sgnw"], w["wtril"], w["sgb"], w["wout_ab"])
    x10, gu10, (wi01, wo01) = _ffn_fwd(x02, n2[0], wi10, wo10, comm=_Comm("gather", [s_in[0, 1], s_out[0, 1]]))
    x11, gu01, (wi11, wo11) = _ffn_fwd(x10, n1[1], wi01, wo01, comm=_Comm("gather", [s_in[1, 1], s_out[1, 1]]))
    x12 = _pool_fwd(x11, mix[1], pw, w["ps"])
    x13, gu11, _ = _ffn_fwd(x12, n2[1], wi11, wo11)
    loss_local, dx, d_fn = _loss_head(x13, row(final_norm), tgt)

    bt = min(BT, xs.shape[0])

    def ffn_b(xin, nw, w_in, w_out, gu, dy, comm=None):
        (dxn, xn, act, dh, dnw, dyb), landed = _ffn_bwd(xin, nw, w_in, w_out, gu, dy, comm)
        return dxn, dnw, (xn, act, dh, dyb), landed

    def ffn_g(kept, dy):
        return [_mm_tn_win(kept[0], kept[2])[0], _mm_tn_wout(kept[1], kept[3])]

    dy = dx
    dx, d_n2_1, kept, _ = ffn_b(x12, n2[1], wi11, wo11, gu11, dy)
    g11 = ffn_g(kept, dy)
    dx, d_mix_1, d_pw, d_ps = _pool_bwd(x11, mix[1], pw, w["ps"], dx)
    d_pw_sh = _c(jnp.transpose(d_pw.reshape(4, N_DEV, PG // N_DEV, PG), (1, 0, 2, 3)))
    dy = dx
    dx, d_n1_1, kept, land11 = ffn_b(x10, n1[1], wi01, wo01, gu01, dy, _Comm("exchange", g11))
    g01 = ffn_g(kept, dy)
    dy = dx
    dx, d_n2_0, kept, land01 = ffn_b(x02, n2[0], wi10, wo10, gu10, dy, _Comm("exchange", g01 + [d_pw_sh]))
    g10 = ffn_g(kept, dy)
    do, dz, dsu, dsv, d_dnw, d_sgnw, d_sgw, d_sgb = _ab_out_bwd(
        dx, o, z, su, sv, w["dnw"], w["sgnw"], w["wtril"], w["wtril_t"], w["sgb"], w["wout_ab"])
    d_about = _mm_tn(cat, dx, D, D, bt, _MM, "mm_tn_about").reshape(N_DEV, D // N_DEV, D)
    (dqn, dkn, dv, dbeta, dg), _ = _dn_bwd(qn, kn, v, beta, g, sall, aall, dn_u, dn_w, do)
    dc, dba, d_cw, d_alog, d_dtb = _dn_pre_bwd(qkv, b_rep, a_rep, w["cw"], w["alog"], w["dtb"], dqn, dkn, dv, dbeta, dg)
    dqkv = _conv_bwd(dc, w["cw"])
    dx, dcat, d_mix_0 = _ab_proj_bwd(x01, mix[0], dqkv, dz, dsu, dsv, dba, w["wab_b"], dx)
    d_wab = _mm_tn(h, dcat, D, 640, min(BT_WIN, xs.shape[0]), _MM, "mm_tn_abin")
    rest = PW_B - 128
    d_abin = jnp.concatenate([d_wab[:, 0:AB_MAIN], d_wab[:, rest:rest + AB_GATES], d_wab[:, AB_MAIN:rest]], axis=1)
    d_abin_sh = jnp.transpose(d_abin.reshape(D, N_DEV, AB_IN // N_DEV), (1, 0, 2))
    dy = dx
    grad_x, d_n1_0, kept, land_mid = ffn_b(xs, n1[0], wi00, wo00, gu00, dy,
                                           _Comm("exchange", g10 + [d_abin_sh, d_about]))
    land10, land_ab = land_mid[0:2], land_mid[2:4]

    g_small = {
        "ffn_norm1": jnp.concatenate([d_n1_0, d_n1_1], axis=0),
        "mix_norm": jnp.concatenate([d_mix_0, d_mix_1], axis=0),
        "ffn_norm2": jnp.concatenate([d_n2_0, d_n2_1], axis=0),
        "dn_conv_w": d_cw.reshape(1, 4, QKV),
        "dn_a_log": d_alog[:, ::HD],
        "dn_dt_bias": d_dtb[:, ::HD],
        "dn_out_norm": d_dnw,
        "sg_norm": d_sgnw.reshape(1, HEADS, HD),
        "sg_w": d_sgw[None],
        "sg_b": jnp.transpose(d_sgb[:, 0:HEADS])[None],
        "pool_scale": d_ps,
        "final_norm": d_fn.reshape(D),
    }
    g00_out = _mm_tn_wout(kept[1], kept[3])
    g00_in, (land00_out, land_small) = _mm_tn_win(
        kept[0], kept[2], comm=_Comm("exchange", [g00_out], repl=[_pack_small(g_small)]))
    (land00_in,) = _comm_call(_Comm("exchange", [g00_in]), "exchange_last")

    res = {}
    tr = lambda a: jnp.swapaxes(a, 1, 2)
    res["ffn1_w_in"] = [tr(a) for a in _adamw([land00_in, land01[0]], tr(ffn1_w_in), tr(m_ffn1_w_in), tr(v_ffn1_w_in),
                                              176, "adamw_w_in")]
    res["ffn2_w_in"] = [tr(a) for a in _adamw([land10[0], land11[0]], tr(ffn2_w_in), tr(m_ffn2_w_in), tr(v_ffn2_w_in),
                                              176, "adamw_w_in")]
    res["ffn1_w_out"] = _adamw([land00_out, land01[1]], ffn1_w_out, m_ffn1_w_out, v_ffn1_w_out, 176, "adamw_w_out")
    res["ffn2_w_out"] = _adamw([land10[1], land11[1]], ffn2_w_out, m_ffn2_w_out, v_ffn2_w_out, 176, "adamw_w_out")
    res["ab_w_in"] = _adamw([land_ab[0]], ab_w_in, m_ab_w_in, v_ab_w_in, 256, "adamw_ab_w_in")
    res["ab_w_out"] = _adamw([land_ab[1]], ab_w_out, m_ab_w_out, v_ab_w_out, D // N_DEV, "adamw_ab_w_out")
    res["pool_w"] = _adamw([land01[2]], pool_w, m_pool_w, v_pool_w, 4, "adamw_pool_w")
    sm = _adamw([land_small], _pack_small(wl)[None], _pack_small(ml)[None], _pack_small(vl)[None], R_SMALL,
                "adamw_replicated")
    sm = [_unpack_small(a[0]) for a in sm]
    for n in wl:
        res[n] = [d[n] for d in sm]
    me = 4 * lax.axis_index("x") + 2 * lax.axis_index("y") + lax.axis_index("c")
    g_ps = lax.dynamic_slice(sm[0]["pool_scale"], (0, me * (D // N_DEV)), (1, D // N_DEV))
    g_cw = lax.dynamic_slice(sm[0]["dn_conv_w"], (0, 0, me * (QKV // N_DEV)), (1, 4, QKV // N_DEV))
    s2 = _adamw([_pack_small_shard(g_ps, g_cw)[None]], _pack_small_shard(pool_scale, dn_conv_w)[None],
                _pack_small_shard(m_pool_scale, m_dn_conv_w)[None], _pack_small_shard(v_pool_scale, v_dn_conv_w)[None],
                8, "adamw_small_sharded")
    res["pool_scale"] = [a[0, 0:1, 0:D // N_DEV] for a in s2]
    res["dn_conv_w"] = [a[0, 1:5, 0:QKV // N_DEV][None] for a in s2]

    loss = lax.psum(loss_local[0, 0], ("x", "y", "c"))
    result = [loss, grad_x[None]]
    for i in range(4):
        result += [res[n][i] for n in WEIGHT_ORDER]
    return tuple(result)
```
